```python
import math
import jax
import jax.numpy as jnp
from jax import lax
import numpy as np

D_MODEL = 1024
BATCH = 4
SEQ = 8192
DEPTH = 2

GRID_W = 64
CTX_LEN = 256
EPS = 1e-6
F32 = jnp.float32

MIX_WIDTH = D_MODEL
GROUP_W = MIX_WIDTH // 4
A_WIDTH = GROUP_W
A_HD = 64
A_HEADS = A_WIDTH // A_HD
MLP_CHUNK = 128
B_WIDTH = GROUP_W
B_HD = 64
B_HEADS = B_WIDTH // B_HD
GATE_RANK = 16
GATE_TEMP = 16.0
GLA_CHUNK = 64
C_WIDTH = GROUP_W
S5_IN = 16
S5_GROUPS = C_WIDTH // S5_IN
S5_STATE = 64
DT_MIN = 1e-3
DT_MAX = 1e-1
D_WIDTH = GROUP_W
MLA_HEADS = 4
MLA_NOPE = 64
MLA_ROPE = 32
MLA_V = D_WIDTH // MLA_HEADS
MLA_Q_RANK = 224
MLA_KV_RANK = 96
ROPE_BASE = 10000.0
Q_BLOCK = 128
D_FF = 2816
CONV_W = 3

A_COLS = 2 * A_WIDTH
B_COLS = 4 * B_WIDTH + 2 * GATE_RANK
C_COLS = C_WIDTH
D_COLS = MLA_Q_RANK + MLA_KV_RANK + MLA_ROPE
IN_COLS = A_COLS + B_COLS + C_COLS + D_COLS
SPLITS = (A_COLS, A_COLS + B_COLS, A_COLS + B_COLS + C_COLS)

kernel_name = "hybrid_headgroup_diffusion_block"


def rmsnorm(x, g):
    xf = x.astype(F32)
    y = xf * lax.rsqrt(jnp.mean(xf * xf, axis=-1, keepdims=True) + EPS)
    return (y * g.astype(F32)).astype(x.dtype)


def _ident(t):
    return t


def _flip(t):
    return jnp.flip(t, axis=1)


def _rotate(t, pos):
    nf = t.shape[-1] // 2
    inv = ROPE_BASE ** (-jnp.arange(nf, dtype=F32) / nf)
    ang = pos[:, None] * inv[None, :]
    cos = jnp.cos(ang)[None, :, None, :].astype(t.dtype)
    sin = jnp.sin(ang)[None, :, None, :].astype(t.dtype)
    t1, t2 = t[..., :nf], t[..., nf:]
    return jnp.concatenate([t1 * cos - t2 * sin, t1 * sin + t2 * cos], axis=-1)


def axial_rope(t, row, col):
    half = t.shape[-1] // 2
    return jnp.concatenate([_rotate(t[..., :half], row), _rotate(t[..., half:], col)], axis=-1)


def attend(q, k, v):
    b, nq, h, dh = q.shape
    scale = dh ** -0.5
    qb = q.reshape(b, nq // Q_BLOCK, Q_BLOCK, h, dh).swapaxes(0, 1)

    def block(qblk):
        s = jnp.einsum('bqhd,bkhd->bhqk', qblk, k).astype(F32) * scale
        p = jax.nn.softmax(s, axis=-1).astype(v.dtype)
        return jnp.einsum('bhqk,bkhd->bqhd', p, v)

    o = lax.map(block, qb)
    return o.swapaxes(0, 1).reshape(b, nq, h, v.shape[-1])


def dwconv_centred(h, w, bias):
    n = h.shape[1]
    pad = CONV_W // 2
    hp = jnp.pad(h, ((0, 0), (pad, pad), (0, 0)))
    out = bias
    for j in range(CONV_W):
        out = out + hp[:, j:j + n] * w[j]
    return out


def conv_ffn(h, w_up, conv_w, conv_b, w_down):
    z = dwconv_centred(h @ w_up, conv_w, conv_b)
    a, g = jnp.split(z, 2, axis=-1)
    return (jax.nn.gelu(a) * g) @ w_down


def chunk_mlp(p, g_norm, w_sp, b_sp):
    b, n, _ = p.shape
    u, v = jnp.split(jax.nn.gelu(p), 2, axis=-1)
    v = rmsnorm(v.reshape(b, n, A_HEADS, A_HD), g_norm)
    v = v.reshape(b, n // MLP_CHUNK, MLP_CHUNK, A_HEADS, A_HD)
    s = jnp.einsum('hts,bcshd->bcthd', w_sp, v) + b_sp.T[:, :, None]
    return u * s.reshape(b, n, A_WIDTH)


def gla_chunked(q, k, v, g, s0):
    b, n, h, dk = q.shape
    nc = n // GLA_CHUNK

    def rs(t):
        return t.reshape(b, nc, GLA_CHUNK, h, t.shape[-1])

    q, k, v, g = rs(q), rs(k), rs(v), rs(g)
    cum = jnp.cumsum(g, axis=2)
    tot = cum[:, :, -1:]
    q_in = q * jnp.exp(cum)
    k_in = k * jnp.exp(-cum)
    k_end = k * jnp.exp(tot - cum)
    tri = jnp.tril(jnp.ones((GLA_CHUNK, GLA_CHUNK), dtype=bool))
    scores = jnp.where(tri, jnp.einsum('bcthk,bcshk->bchts', q_in, k_in), 0.0)
    o_intra = jnp.einsum('bchts,bcshv->bcthv', scores, v)
    kv_chunk = jnp.einsum('bcshk,bcshv->bchkv', k_end, v)
    decay = jnp.exp(tot[:, :, 0])

    def step(s, inp):
        dec, kv = inp
        return dec[..., None] * s + kv, s

    s_fin, s_start = lax.scan(step, s0, (decay.swapaxes(0, 1), kv_chunk.swapaxes(0, 1)))
    o_inter = jnp.einsum('bcthk,bchkv->bcthv', q_in, s_start.swapaxes(0, 1))
    return (o_intra + o_inter).reshape(b, n, h, v.shape[-1]), s_fin


def gla_final_state(k, v, g):
    cum = jnp.cumsum(g, axis=1)
    w = jnp.exp(cum[:, -1:] - cum)
    return jnp.einsum('bnhk,bnhv->bhkv', k * w, v)


def gla_inputs(p, w_gate, b_gate):
    b, n, _ = p.shape
    q, k, v, r, gl = jnp.split(p, [B_WIDTH, 2 * B_WIDTH, 3 * B_WIDTH, 4 * B_WIDTH], axis=-1)

    def heads(t):
        return t.reshape(b, n, B_HEADS, B_HD).astype(F32)

    z = jnp.einsum('bndr,drc->bndc', gl.reshape(b, n, 2, GATE_RANK), w_gate) + b_gate
    logg = (jax.nn.log_sigmoid(z.astype(F32)) / GATE_TEMP).reshape(b, n, 2, B_HEADS, B_HD)
    return heads(q) * (B_HD ** -0.5), heads(k), heads(v), r, logg


def gla_mixer(pc, px, w_gate, b_gate, g_norm, ctx_out):
    qc, kc, vc, rc, gc = gla_inputs(pc, w_gate, b_gate)
    qx, kx, vx, rx, gx = gla_inputs(px, w_gate, b_gate)
    s_zero = jnp.zeros((px.shape[0], B_HEADS, B_HD, B_HD), F32)
    outs_c, outs_x = [], []
    for d, f in enumerate((_ident, _flip)):
        if ctx_out:
            oc_d, s_ctx = gla_chunked(f(qc), f(kc), f(vc), f(gc[:, :, d]), s_zero)
            outs_c.append(f(oc_d))
        else:
            s_ctx = gla_final_state(f(kc), f(vc), f(gc[:, :, d]))
        ox_d, _ = gla_chunked(f(qx), f(kx), f(vx), f(gx[:, :, d]), s_ctx)
        outs_x.append(f(ox_d))

    def finish(outs, r):
        o = rmsnorm(outs[0] + outs[1], g_norm).astype(r.dtype)
        return o.reshape(r.shape) * jax.nn.silu(r)

    return (finish(outs_c, rc) if ctx_out else None), finish(outs_x, rx)


def s5_discretize(a_re, a_im, log_dt, b_re, b_im):
    lam = lax.complex(a_re.astype(F32), a_im.astype(F32))
    dt = jnp.exp(log_dt.astype(F32))[:, None]
    lam_bar = jnp.exp(lam * dt)
    b = lax.complex(b_re.astype(F32), b_im.astype(F32))
    b_bar = ((lam_bar - 1.0) / lam)[..., None] * b
    return lam_bar, b_bar


def _linear_recurrence(e1, e2):
    a1, b1 = e1
    a2, b2 = e2
    return a1 * a2, a2 * b1 + b2


def s5_scan(u, lam_bar, b_bar, h0=None):
    bu = jnp.einsum('bngi,gpi->bngp', u.astype(jnp.complex64), b_bar)
    a = jnp.broadcast_to(lam_bar, bu.shape)
    a_cum, h = lax.associative_scan(_linear_recurrence, (a, bu), axis=1)
    if h0 is not None:
        h = h + a_cum * h0[:, None]
    return h


def s5_mixer(uc, ux, a_re, a_im, log_dt, b_re, b_im, c_re, c_im, d_skip, w_glu, b_glu, ctx_out):
    def groups(u):
        return u.reshape(u.shape[0], u.shape[1], S5_GROUPS, S5_IN).astype(F32)

    gu_c, gu_x = groups(uc), groups(ux)
    d32 = d_skip.astype(F32)
    ys_c, ys_x = [gu_c * d32], [gu_x * d32]
    for d, f in enumerate((_ident, _flip)):
        lam_bar, b_bar = s5_discretize(a_re[d], a_im[d], log_dt[d], b_re[d], b_im[d])
        cmat = lax.complex(c_re[d].astype(F32), c_im[d].astype(F32))
        hc = s5_scan(f(gu_c), lam_bar, b_bar)
        hx = s5_scan(f(gu_x), lam_bar, b_bar, hc[:, -1])
        ys_x.append(f(jnp.einsum('bngp,gop->bngo', hx, cmat).real))
        if ctx_out:
            ys_c.append(f(jnp.einsum('bngp,gop->bngo', hc, cmat).real))

    def finish(ys, like):
        y = jax.nn.gelu(sum(ys).reshape(like.shape)).astype(like.dtype)
        return y * jax.nn.sigmoid(y @ w_glu + b_glu)

    return (finish(ys_c, uc) if ctx_out else None), finish(ys_x, ux)


def mla_mixer(pc, px, q_norm, w_uq, kv_norm, w_ukv, row, col, ctx_out):
    def split(p):
        return jnp.split(p, [MLA_Q_RANK, MLA_Q_RANK + MLA_KV_RANK], axis=-1)

    def queries(cq, rope):
        b, n, _ = cq.shape
        q = (rmsnorm(cq, q_norm) @ w_uq).reshape(b, n, MLA_HEADS, MLA_NOPE + MLA_ROPE)
        if rope:
            q = jnp.concatenate([q[..., :MLA_NOPE], axial_rope(q[..., MLA_NOPE:], row, col)], axis=-1)
        return q

    def keys_values(ckv, kr, rope):
        b, n, _ = ckv.shape
        kv = (rmsnorm(ckv, kv_norm) @ w_ukv).reshape(b, n, MLA_HEADS, MLA_NOPE + MLA_V)
        kr = kr[:, :, None, :]
        if rope:
            kr = axial_rope(kr, row, col)
        k = jnp.concatenate([kv[..., :MLA_NOPE], jnp.broadcast_to(kr, (b, n, MLA_HEADS, MLA_ROPE))], axis=-1)
        return k, kv[..., MLA_NOPE:]

    cq_c, ckv_c, kr_c = split(pc)
    cq_x, ckv_x, kr_x = split(px)
    kc, vc = keys_values(ckv_c, kr_c, False)
    kx, vx = keys_values(ckv_x, kr_x, True)
    ox = attend(queries(cq_x, True), jnp.concatenate([kc, kx], axis=1), jnp.concatenate([vc, vx], axis=1))
    ox = ox.reshape(px.shape[0], px.shape[1], D_WIDTH)
    if ctx_out:
        oc = attend(queries(cq_c, False), kc, vc).reshape(pc.shape[0], pc.shape[1], D_WIDTH)
        return oc, ox
    return None, ox


def setup_inputs(seed: int = 0) -> dict:
    key = jax.random.key(seed)
    ks = iter(jax.random.split(key, 40))
    L = DEPTH

    def nrm(shape, scale):
        return jax.random.normal(next(ks), shape, F32) * scale

    def gain(shape):
        return 1.0 + nrm(shape, 0.05)

    inputs = {
        'x': nrm((BATCH, SEQ, D_MODEL), 1.0),
        'c': nrm((BATCH, D_MODEL), 1.0),
        'ctx': nrm((BATCH, CTX_LEN, D_MODEL), 1.0),
        'c_ctx': nrm((D_MODEL,), 1.0),
        'w_mod': nrm((L, D_MODEL, 6 * D_MODEL), 0.5 * D_MODEL ** -0.5),
        'b_mod': nrm((L, 6 * D_MODEL), 0.02),
        'g_pre_mix': gain((L, D_MODEL)),
        'g_post_mix': gain((L, D_MODEL)),
        'g_pre_ffn': gain((L, D_MODEL)),
        'g_post_ffn': gain((L, D_MODEL)),
        'w_in': nrm((L, D_MODEL, IN_COLS), D_MODEL ** -0.5),
        'sgu_norm': gain((L, A_HEADS, A_HD)),
        'sgu_w': nrm((L, A_HEADS, MLP_CHUNK, MLP_CHUNK), MLP_CHUNK ** -0.5),
        'sgu_b': 1.0 + nrm((L, A_HEADS, MLP_CHUNK), 0.02),
        'gla_w_gate': nrm((L, 2, GATE_RANK, B_WIDTH), GATE_RANK ** -0.5),
        'gla_b_gate': nrm((L, 2, B_WIDTH), 0.02),
        'gla_norm': gain((L, B_HEADS, B_HD)),
        's5_a_re': -0.5 + nrm((L, 2, S5_GROUPS, S5_STATE), 0.01),
        's5_a_im': math.pi * jnp.arange(S5_STATE, dtype=F32) + nrm((L, 2, S5_GROUPS, S5_STATE), 0.01),
        's5_log_dt': jax.random.uniform(next(ks), (L, 2, S5_GROUPS), F32, math.log(DT_MIN), math.log(DT_MAX)),
        's5_b_re': nrm((L, 2, S5_GROUPS, S5_STATE, S5_IN), (2 * S5_IN) ** -0.5),
        's5_b_im': nrm((L, 2, S5_GROUPS, S5_STATE, S5_IN), (2 * S5_IN) ** -0.5),
        's5_c_re': nrm((L, 2, S5_GROUPS, S5_IN, S5_STATE), (2 * S5_STATE) ** -0.5),
        's5_c_im': nrm((L, 2, S5_GROUPS, S5_IN, S5_STATE), (2 * S5_STATE) ** -0.5),
        's5_d': nrm((L, S5_GROUPS, S5_IN), 1.0),
        's5_w_glu': nrm((L, C_WIDTH, C_WIDTH), C_WIDTH ** -0.5),
        's5_b_glu': nrm((L, C_WIDTH), 0.02),
        'mla_q_norm': gain((L, MLA_Q_RANK)),
        'mla_w_uq': nrm((L, MLA_Q_RANK, MLA_HEADS * (MLA_NOPE + MLA_ROPE)), MLA_Q_RANK ** -0.5),
        'mla_kv_norm': gain((L, MLA_KV_RANK)),
        'mla_w_ukv': nrm((L, MLA_KV_RANK, MLA_HEADS * (MLA_NOPE + MLA_V)), MLA_KV_RANK ** -0.5),
        'w_out': nrm((L, MIX_WIDTH, D_MODEL), MIX_WIDTH ** -0.5),
        'ffn_w_up': nrm((L, D_MODEL, 2 * D_FF), D_MODEL ** -0.5),
        'ffn_conv_w': nrm((L, CONV_W, 2 * D_FF), CONV_W ** -0.5),
        'ffn_conv_b': nrm((L, 2 * D_FF), 0.02),
        'ffn_w_down': nrm((L, D_FF, D_MODEL), D_FF ** -0.5),
    }
    return inputs


def reference(x, c, ctx, c_ctx, w_mod, b_mod, g_pre_mix, g_post_mix, g_pre_ffn, g_post_ffn, w_in,
              sgu_norm, sgu_w, sgu_b, gla_w_gate, gla_b_gate, gla_norm,
              s5_a_re, s5_a_im, s5_log_dt, s5_b_re, s5_b_im, s5_c_re, s5_c_im, s5_d, s5_w_glu, s5_b_glu,
              mla_q_norm, mla_w_uq, mla_kv_norm, mla_w_ukv, w_out,
              ffn_w_up, ffn_conv_w, ffn_conv_b, ffn_w_down):
    n = x.shape[1]
    rows = n // GRID_W
    row = jnp.repeat(jnp.arange(rows, dtype=F32), GRID_W)
    col = jnp.tile(jnp.arange(GRID_W, dtype=F32), rows)
    silu_c = jax.nn.silu(c)
    silu_cc = jax.nn.silu(c_ctx)

    for l in range(DEPTH):
        ctx_out = l < DEPTH - 1
        mod_x = (silu_c @ w_mod[l] + b_mod[l])[:, None, :]
        mod_c = silu_cc @ w_mod[l] + b_mod[l]
        sh1x, sc1x, gt1x, sh2x, sc2x, gt2x = jnp.split(mod_x, 6, axis=-1)
        sh1c, sc1c, gt1c, sh2c, sc2c, gt2c = jnp.split(mod_c, 6, axis=-1)

        hx = rmsnorm(x, g_pre_mix[l]) * (1.0 + sc1x) + sh1x
        hc = rmsnorm(ctx, g_pre_mix[l]) * (1.0 + sc1c) + sh1c
        in_x = jnp.split(hx @ w_in[l], SPLITS, axis=-1)
        in_c = jnp.split(hc @ w_in[l], SPLITS, axis=-1)

        oa_x = chunk_mlp(in_x[0], sgu_norm[l], sgu_w[l], sgu_b[l])
        ob_c, ob_x = gla_mixer(in_c[1], in_x[1], gla_w_gate[l], gla_b_gate[l], gla_norm[l], ctx_out)
        oc_c, oc_x = s5_mixer(in_c[2], in_x[2], s5_a_re[l], s5_a_im[l], s5_log_dt[l], s5_b_re[l], s5_b_im[l],
                              s5_c_re[l], s5_c_im[l], s5_d[l], s5_w_glu[l], s5_b_glu[l], ctx_out)
        od_c, od_x = mla_mixer(in_c[3], in_x[3], mla_q_norm[l], mla_w_uq[l], mla_kv_norm[l], mla_w_ukv[l],
                               row, col, ctx_out)

        mix_x = jnp.concatenate([oa_x, ob_x, oc_x, od_x], axis=-1) @ w_out[l]
        x = x + gt1x * rmsnorm(mix_x, g_post_mix[l])

        hx = rmsnorm(x, g_pre_ffn[l]) * (1.0 + sc2x) + sh2x
        x = x + gt2x * rmsnorm(conv_ffn(hx, ffn_w_up[l], ffn_conv_w[l], ffn_conv_b[l], ffn_w_down[l]), g_post_ffn[l])

        if ctx_out:
            oa_c = chunk_mlp(in_c[0], sgu_norm[l], sgu_w[l], sgu_b[l])
            mix_c = jnp.concatenate([oa_c, ob_c, oc_c, od_c], axis=-1) @ w_out[l]
            ctx = ctx + gt1c * rmsnorm(mix_c, g_post_mix[l])
            hc = rmsnorm(ctx, g_pre_ffn[l]) * (1.0 + sc2c) + sh2c
            ctx = ctx + gt2c * rmsnorm(conv_ffn(hc, ffn_w_up[l], ffn_conv_w[l], ffn_conv_b[l], ffn_w_down[l]), g_post_ffn[l])
    return x
```

```python
import functools
import math

import jax
import jax.numpy as jnp
from jax import lax
from jax.experimental import pallas as pl
from jax.experimental.pallas import tpu as pltpu

F32 = jnp.float32
BF16 = jnp.bfloat16

EPS = 1e-6
GRID_W = 64
GROUP_W = 256
HEAD_D = 64
MLP_CHUNK = 128
GATE_RANK = 16
GATE_TEMP = 16.0
GLA_CHUNK = 64
S5_IN = 16
S5_STATE = 64
S5_T = 16
MLA_HEADS = 4
MLA_NOPE = 64
MLA_ROPE = 32
MLA_V = 64
MLA_Q_RANK = 224
MLA_KV_RANK = 96
MLA_HEAD_PAD = 128
ROPE_BASE = 10000.0
LOG2E = 1.4426950408889634

VMEM_LIMIT = 48 * 1024 * 1024


def _cparams(*sem):
    return pltpu.CompilerParams(dimension_semantics=sem, vmem_limit_bytes=VMEM_LIMIT)


def _dot(a, b):
    return jnp.dot(a, b, preferred_element_type=F32)


def _dot_nt(a, b):
    return lax.dot_general(a, b, (((1,), (1,)), ((), ())), preferred_element_type=F32)


def _dot_tn(a, b):
    return lax.dot_general(a, b, (((0,), (0,)), ((), ())), preferred_element_type=F32)


def _split(a):
    hi = a.astype(BF16)
    lo = (a - hi.astype(F32)).astype(BF16)
    return hi, lo


def _dot_x2(a, b_bf16):
    hi, lo = _split(a)
    return _dot(hi, b_bf16) + _dot(lo, b_bf16)


def _dot_x3(a, b):
    ah, al = _split(a)
    bh, bl = _split(b)
    return _dot(ah, bh) + _dot(al, bh) + _dot(ah, bl)


def _rms(x):
    return x * lax.rsqrt(jnp.mean(x * x, axis=-1, keepdims=True) + EPS)


def _gelu(x):
    return 0.5 * x * (1.0 + jnp.tanh(0.7978845608028654 * (x + 0.044715 * (x * x * x))))


def _sigmoid(x):
    return 1.0 / (1.0 + jnp.exp(-x))


def _lane_group(shape, width):
    return lax.broadcasted_iota(jnp.int32, shape, len(shape) - 1) // width


def _mod_kernel(c_ref, w_ref, b_ref, o_ref):
    c = c_ref[...]
    s = c * _sigmoid(c)
    o_ref[...] = _dot_x3(s, w_ref[...]) + b_ref[...]


def _modulation(c8, w_mod, b_mod):
    L, D, W = w_mod.shape
    tn = 1536
    return pl.pallas_call(
        _mod_kernel,
        grid=(L, W // tn),
        in_specs=[
            pl.BlockSpec((8, D), lambda l, j: (0, 0)),
            pl.BlockSpec((None, D, tn), lambda l, j: (l, 0, j)),
            pl.BlockSpec((None, 1, tn), lambda l, j: (l, 0, j)),
        ],
        out_specs=pl.BlockSpec((None, 8, tn), lambda l, j: (l, 0, j)),
        out_shape=jax.ShapeDtypeStruct((L, 8, W), F32),
        compiler_params=_cparams("arbitrary", "arbitrary"),
        name="modulation",
    )(c8, w_mod, b_mod.reshape(L, 1, W))


def _mod_spec(l, j, D, bfn):
    return pl.BlockSpec((None, None, 1, D), lambda *g: (l, bfn(*g), 0, j))


def _vec_spec(l, width):
    return pl.BlockSpec((None, 1, width), lambda *g: (l, 0, 0))


IN_SLABS = (("a", 0, 512), ("b", 512, 1024), ("c", 1536, 256), ("d", 1792, 384), ("g", 2176, 128))
IN_PAD_COLS = 2304


def _inproj_kernel(x_ref, g_ref, sc_ref, sh_ref, w_ref, oa, ob, oc, od, og):
    h = _rms(x_ref[...]) * g_ref[...] * (1.0 + sc_ref[...]) + sh_ref[...]
    hb = h.astype(BF16)
    for (_, off, width), o_ref in zip(IN_SLABS, (oa, ob, oc, od, og)):
        o_ref[...] = _dot(hb, w_ref[:, off:off + width])


def _inproj(l, x2, mod4, brow, g_pre, w_in_p, tm):
    R, D = x2.shape
    grid = (R // tm,)
    outs = [jax.ShapeDtypeStruct((R, width), F32) for (_, _, width) in IN_SLABS]
    return pl.pallas_call(
        _inproj_kernel,
        grid=grid,
        in_specs=[
            pl.BlockSpec((tm, D), lambda i: (i, 0)),
            _vec_spec(l, D),
            _mod_spec(l, 1, D, brow),
            _mod_spec(l, 0, D, brow),
            pl.BlockSpec((None, D, IN_PAD_COLS), lambda i: (l, 0, 0)),
        ],
        out_specs=[pl.BlockSpec((tm, width), lambda i: (i, 0)) for (_, _, width) in IN_SLABS],
        out_shape=outs,
        compiler_params=_cparams("parallel"),
        name="inproj",
    )(x2, g_pre, mod4, mod4, w_in_p)


def _sgu_kernel(p_ref, gn_ref, w_ref, b_ref, ones_ref, o_ref):
    tm = p_ref.shape[0]
    g = _gelu(p_ref[...])
    u = g[:, :GROUP_W]
    v = g[:, GROUP_W:]
    ms = _dot_x2(v * v, ones_ref[...]) * (1.0 / HEAD_D)
    vb = (v * lax.rsqrt(ms + EPS) * gn_ref[...]).astype(BF16)
    head = _lane_group((MLP_CHUNK, GROUP_W), HEAD_D)
    w = w_ref[...]
    for c in range(tm // MLP_CHUNK):
        rows = slice(c * MLP_CHUNK, (c + 1) * MLP_CHUNK)
        r = _dot(w, vb[rows])
        s = b_ref[...]
        for h in range(GROUP_W // HEAD_D):
            s = s + jnp.where(head == h, r[h * MLP_CHUNK:(h + 1) * MLP_CHUNK], 0.0)
        o_ref[rows, :] = (u[rows] * s).astype(o_ref.dtype)


def _sgu(l, pa, gn, w_st, bias, ones_bd, tm):
    R = pa.shape[0]
    H = GROUP_W // HEAD_D
    return pl.pallas_call(
        _sgu_kernel,
        grid=(R // tm,),
        in_specs=[
            pl.BlockSpec((tm, 2 * GROUP_W), lambda i: (i, 0)),
            _vec_spec(l, GROUP_W),
            pl.BlockSpec((None, H * MLP_CHUNK, MLP_CHUNK), lambda i: (l, 0, 0)),
            pl.BlockSpec((None, MLP_CHUNK, GROUP_W), lambda i: (l, 0, 0)),
            pl.BlockSpec((GROUP_W, GROUP_W), lambda i: (0, 0)),
        ],
        out_specs=pl.BlockSpec((tm, GROUP_W), lambda i: (i, 0)),
        out_shape=jax.ShapeDtypeStruct((R, GROUP_W), BF16),
        compiler_params=_cparams("parallel"),
        name="sgu",
    )(pa, gn, w_st, bias, ones_bd)


def _gla_kernel(*refs, rev, finish):
    if finish:
        (pb_ref, pg_ref, wg_ref, bg_ref, s0_ref, ones_ref, oprev_ref, gn_ref, o_ref, sfin_ref, st_scr) = refs
    else:
        (pb_ref, pg_ref, wg_ref, bg_ref, s0_ref, ones_ref, o_ref, sfin_ref, st_scr) = refs
    i = pl.program_id(1)
    C = GLA_CHUNK
    W = GROUP_W
    H = W // HEAD_D
    tt = pb_ref.shape[0]

    @pl.when(i == 0)
    def _():
        st_scr[...] = s0_ref[...]

    z = _dot_x3(pg_ref[...], wg_ref[...]) + bg_ref[...]
    logg = (jnp.minimum(z, 0.0) - jnp.log(1.0 + jnp.exp(-jnp.abs(z)))) * (1.0 / GATE_TEMP)
    q = pb_ref[:, 0:W] * (HEAD_D ** -0.5)
    k = pb_ref[:, W:2 * W]
    v = pb_ref[:, 2 * W:3 * W]

    ri = lax.broadcasted_iota(jnp.int32, (C, C), 0)
    ci = lax.broadcasted_iota(jnp.int32, (C, C), 1)
    tri = (ri <= ci) if rev else (ri >= ci)
    tri_b = jnp.where(tri, 1.0, 0.0).astype(BF16)
    tri4 = jnp.concatenate([tri] * H, axis=0)
    head_c = _lane_group((C, W), HEAD_D)
    bd = (lax.broadcasted_iota(jnp.int32, (W, W), 0) // HEAD_D) == _lane_group((W, W), HEAD_D)

    order = range(tt // C - 1, -1, -1) if rev else range(tt // C)
    for j in order:
        rows = slice(j * C, (j + 1) * C)
        g_hi, g_lo = _split(logg[rows])
        cum = _dot(tri_b, g_hi) + _dot(tri_b, g_lo)
        tot = cum[0:1] if rev else cum[C - 1:C]
        qin = (q[rows] * jnp.exp(cum)).astype(BF16)
        kin = (k[rows] * jnp.exp(-cum)).astype(BF16)
        kend = (k[rows] * jnp.exp(tot - cum)).astype(BF16)
        vb = v[rows].astype(BF16)
        qst = jnp.concatenate([jnp.where(head_c == h, qin, jnp.zeros_like(qin)) for h in range(H)], axis=0)
        sc = _dot_nt(qst, kin)
        sc = jnp.where(tri4, sc, 0.0).astype(BF16)
        ost = _dot(sc, vb)
        st = st_scr[...]
        o = _dot_nt(qin, st.astype(BF16))
        for h in range(H):
            o = o + jnp.where(head_c == h, ost[h * C:(h + 1) * C], 0.0)
        kvt = _dot_tn(vb, kend)
        st_scr[...] = st * jnp.exp(tot) + jnp.where(bd, kvt, 0.0)
        if finish:
            o = o + oprev_ref[rows, :]
            ms = _dot_x2(o * o, ones_ref[...]) * (1.0 / HEAD_D)
            o = o * lax.rsqrt(ms + EPS) * gn_ref[...]
            r = pb_ref[rows, 3 * W:4 * W]
            o_ref[rows, :] = (o * (r * _sigmoid(r))).astype(o_ref.dtype)
        else:
            o_ref[rows, :] = o

    @pl.when(i == pl.num_programs(1) - 1)
    def _():
        sfin_ref[...] = st_scr[...]


def _gla_pass(l, d, pb3, pg3, wg, bg, s0, ones_bd, oprev, gn, tt):
    B, n, _ = pb3.shape
    nt = n // tt
    rev = d == 1
    finish = oprev is not None
    W = GROUP_W

    def tok(b, i):
        return (b, (nt - 1 - i) if rev else i, 0)

    in_specs = [
        pl.BlockSpec((None, tt, 4 * W), tok),
        pl.BlockSpec((None, tt, 128), tok),
        pl.BlockSpec((None, None, 128, W), lambda b, i: (l, d, 0, 0)),
        pl.BlockSpec((None, None, 1, W), lambda b, i: (l, d, 0, 0)),
        pl.BlockSpec((None, W, W), lambda b, i: (b, 0, 0)),
        pl.BlockSpec((W, W), lambda b, i: (0, 0)),
    ]
    args = [pb3, pg3, wg, bg, s0, ones_bd]
    if finish:
        in_specs += [pl.BlockSpec((None, tt, W), tok), _vec_spec(l, W)]
        args += [oprev, gn]
    return pl.pallas_call(
        functools.partial(_gla_kernel, rev=rev, finish=finish),
        grid=(B, nt),
        in_specs=in_specs,
        out_specs=[pl.BlockSpec((None, tt, W), tok), pl.BlockSpec((None, W, W), lambda b, i: (b, 0, 0))],
        out_shape=[jax.ShapeDtypeStruct((B, n, W), BF16 if finish else F32),
                   jax.ShapeDtypeStruct((B, W, W), F32)],
        scratch_shapes=[pltpu.VMEM((W, W), F32)],
        compiler_params=_cparams("arbitrary", "arbitrary"),
        name="gla_bwd" if rev else "gla_fwd",
    )(*args)


def _gla(l, pb_c, pg_c, pb_x, pg_x, wg, bg, gn, ones_bd, tt_c, tt_x):
    B = pb_x.shape[0]
    zero = jnp.zeros((B, GROUP_W, GROUP_W), F32)
    ofc, sfc = _gla_pass(l, 0, pb_c, pg_c, wg, bg, zero, ones_bd, None, None, tt_c)
    ofx, _ = _gla_pass(l, 0, pb_x, pg_x, wg, bg, sfc, ones_bd, None, None, tt_x)
    ob_c, sbc = _gla_pass(l, 1, pb_c, pg_c, wg, bg, zero, ones_bd, ofc, gn, tt_c)
    ob_x, _ = _gla_pass(l, 1, pb_x, pg_x, wg, bg, sbc, ones_bd, ofx, gn, tt_x)
    return ob_c, ob_x


def _s5_matrices(a_re, a_im, log_dt, b_re, b_im, c_re, c_im, d_skip):
    T = S5_T
    G, P = a_re.shape[1:]
    I = b_re.shape[-1]
    lam = lax.complex(a_re.astype(F32), a_im.astype(F32))
    ldt = lam * jnp.exp(log_dt.astype(F32))[..., None]
    lam_bar = jnp.exp(ldt)
    b_bar = ((lam_bar - 1.0) / lam)[..., None] * lax.complex(b_re.astype(F32), b_im.astype(F32))
    cmat = lax.complex(c_re.astype(F32), c_im.astype(F32))
    steps = jnp.arange(T + 1, dtype=F32)
    pw = jnp.exp(ldt[..., None] * steps)
    taps = jnp.einsum('dgop,dgpk,dgpi->dkgoi', cmat, pw[..., :T], b_bar).real

    s_idx = jnp.arange(T)[:, None]
    t_idx = jnp.arange(T)[None, :]
    df = t_idx - s_idx
    sel = lambda m: m[:, :, None, None, None]
    kf = jnp.where(sel(df >= 0), taps[0][jnp.clip(df, 0, T - 1)], 0.0)
    kb = jnp.where(sel(df <= 0), taps[1][jnp.clip(-df, 0, T - 1)], 0.0)
    skip = jnp.eye(I, dtype=F32)[None, :, :] * d_skip.astype(F32)[:, None, :]
    m = kf + kb + jnp.where(sel(df == 0), skip[None, None], 0.0)
    m = m.transpose(2, 0, 4, 1, 3).reshape(G, T * I, T * I)

    ar = jnp.arange(T)
    pf = pw[0][..., T - 1 - ar]
    pb = pw[1][..., ar]
    bf = jnp.einsum('gps,gpi->gsip', pf, b_bar[0]).reshape(G, T * I, P)
    bb = jnp.einsum('gps,gpi->gsip', pb, b_bar[1]).reshape(G, T * I, P)
    bmat = jnp.concatenate([bf.real, bf.imag, bb.real, bb.imag, bf.imag, bf.real, bb.imag, bb.real], axis=-1)

    cf = jnp.einsum('gop,gpt->gpto', cmat[0], pw[0][..., 1 + ar]).reshape(G, P, T * I)
    cb = jnp.einsum('gop,gpt->gpto', cmat[1], pw[1][..., T - ar]).reshape(G, P, T * I)
    w = jnp.concatenate([m, cf.real, -cf.imag, cb.real, -cb.imag], axis=1)

    a = pw[..., T]
    a1 = jnp.concatenate([a[0].real, a[0].real, a[1].real, a[1].real], axis=-1)
    a2 = jnp.concatenate([-a[0].imag, a[0].imag, -a[1].imag, a[1].imag], axis=-1)
    return bmat.astype(BF16), w.astype(BF16), a1, a2


def _s5_state_kernel(x_ref, b_ref, o_ref):
    o_ref[...] = _dot(x_ref[...], b_ref[...])


def _s5_state(xg, bmat, tr):
    G, R, K = xg.shape
    N = bmat.shape[-1]
    return pl.pallas_call(
        _s5_state_kernel,
        grid=(G, R // tr),
        in_specs=[pl.BlockSpec((None, tr, K), lambda g, i: (g, i, 0)),
                  pl.BlockSpec((None, K, N), lambda g, i: (g, 0, 0))],
        out_specs=pl.BlockSpec((None, tr, N), lambda g, i: (g, i, 0)),
        out_shape=jax.ShapeDtypeStruct((G, R, N), F32),
        compiler_params=_cparams("parallel", "parallel"),
        name="s5_state",
    )(xg, bmat)


def _s5_scan_kernel(sc_ref, sx_ref, a1_ref, a2_ref, hc_ref, hx_ref):
    a1 = a1_ref[...]
    a2 = a2_ref[...]
    half = a1.shape[-1] // 2
    full = a1.shape[-1]

    def run(s_ref, h_ref, carry):
        nc = s_ref.shape[0]

        def body(j, hs):
            h1, h2 = hs
            jr = nc - 1 - j
            sf = s_ref[j]
            sb = s_ref[jr]
            s1 = jnp.concatenate([sf[:, 0:half], sb[:, half:full]], axis=1)
            s2 = jnp.concatenate([sf[:, full:full + half], sb[:, full + half:2 * full]], axis=1)
            h_ref[j, :, 0:half] = h1[:, 0:half]
            h_ref[jr, :, half:full] = h1[:, half:full]
            return (h1 * a1 + h2 * a2 + s1, h2 * a1 - h1 * a2 + s2)

        return lax.fori_loop(0, nc, body, carry)

    zero = jnp.zeros(a1.shape, F32)
    carry = run(sc_ref, hc_ref, (zero, zero))
    run(sx_ref, hx_ref, carry)


def _s5_scan(s_c, s_x, a1, a2):
    ncc, ch, n2 = s_c.shape
    ncx = s_x.shape[0]
    n1 = n2 // 2
    cb = 8
    return pl.pallas_call(
        _s5_scan_kernel,
        grid=(ch // cb,),
        in_specs=[pl.BlockSpec((ncc, cb, n2), lambda i: (0, i, 0)),
                  pl.BlockSpec((ncx, cb, n2), lambda i: (0, i, 0)),
                  pl.BlockSpec((cb, n1), lambda i: (i, 0)),
                  pl.BlockSpec((cb, n1), lambda i: (i, 0))],
        out_specs=[pl.BlockSpec((ncc, cb, n1), lambda i: (0, i, 0)),
                   pl.BlockSpec((ncx, cb, n1), lambda i: (0, i, 0))],
        out_shape=[jax.ShapeDtypeStruct((ncc, ch, n1), F32), jax.ShapeDtypeStruct((ncx, ch, n1), F32)],
        compiler_params=_cparams("parallel"),
        name="s5_scan",
    )(s_c, s_x, a1, a2)


def _s5_out_kernel(x_ref, h_ref, w_ref, o_ref):
    k = x_ref.shape[-1]
    o_ref[...] = _dot(x_ref[...], w_ref[0:k, :]) + _dot(h_ref[...].astype(BF16), w_ref[k:, :])


def _s5_out(xg, hg, w, tr):
    G, R, K = xg.shape
    return pl.pallas_call(
        _s5_out_kernel,
        grid=(G, R // tr),
        in_specs=[pl.BlockSpec((None, tr, K), lambda g, i: (g, i, 0)),
                  pl.BlockSpec((None, tr, K), lambda g, i: (g, i, 0)),
                  pl.BlockSpec((None, 2 * K, K), lambda g, i: (g, 0, 0))],
        out_specs=pl.BlockSpec((None, tr, K), lambda g, i: (g, i, 0)),
        out_shape=jax.ShapeDtypeStruct((G, R, K), F32),
        compiler_params=_cparams("parallel", "parallel"),
        name="s5_out",
    )(xg, hg, w)


def _s5_finish_kernel(y_ref, w_ref, b_ref, o_ref):
    y = _gelu(y_ref[...])
    gate = _sigmoid(_dot(y.astype(BF16), w_ref[...]) + b_ref[...])
    o_ref[...] = (y * gate).astype(o_ref.dtype)


def _s5_finish(l, y2, w_glu, b_glu, tm):
    R, W = y2.shape
    return pl.pallas_call(
        _s5_finish_kernel,
        grid=(R // tm,),
        in_specs=[pl.BlockSpec((tm, W), lambda i: (i, 0)),
                  pl.BlockSpec((None, W, W), lambda i: (l, 0, 0)),
                  _vec_spec(l, W)],
        out_specs=pl.BlockSpec((tm, W), lambda i: (i, 0)),
        out_shape=jax.ShapeDtypeStruct((R, W), BF16),
        compiler_params=_cparams("parallel"),
        name="s5_finish",
    )(y2, w_glu, b_glu)


def _to_groups(u3):
    B, n, W = u3.shape
    G = W // S5_IN
    x = u3.astype(BF16).reshape(B, n // S5_T, S5_T, G, S5_IN).transpose(3, 0, 1, 2, 4)
    return x.reshape(G, B * (n // S5_T), S5_T * S5_IN)


def _chains_major(s, B):
    G, R, N = s.shape
    return s.reshape(G, B, R // B, N).transpose(2, 0, 1, 3).reshape(R // B, G * B, N)


def _groups_major(h, B):
    nc, ch, N = h.shape
    G = ch // B
    return h.reshape(nc, G, B, N).transpose(1, 2, 0, 3).reshape(G, B * nc, N)


def _from_groups(y, B):
    G, R, _ = y.shape
    nc = R // B
    y = y.reshape(G, B, nc, S5_T, S5_IN).transpose(1, 2, 3, 0, 4)
    return y.reshape(B * nc * S5_T, G * S5_IN)


def _s5(l, pc_c, pc_x, mats, w_glu, b_glu, need_ctx, tm_c, tm_x):
    bmat, w, a1, a2 = mats
    B = pc_x.shape[0]
    xg_c, xg_x = _to_groups(pc_c), _to_groups(pc_x)
    tr_c = min(512, xg_c.shape[1])
    tr_x = min(512, xg_x.shape[1])
    s_c = _chains_major(_s5_state(xg_c, bmat, tr_c), B)
    s_x = _chains_major(_s5_state(xg_x, bmat, tr_x), B)
    a1c = jnp.repeat(a1, B, axis=0)
    a2c = jnp.repeat(a2, B, axis=0)
    h_c, h_x = _s5_scan(s_c, s_x, a1c, a2c)
    y_x = _from_groups(_s5_out(xg_x, _groups_major(h_x, B), w, tr_x), B)
    oc_x = _s5_finish(l, y_x, w_glu, b_glu, tm_x)
    oc_c = None
    if need_ctx:
        y_c = _from_groups(_s5_out(xg_c, _groups_major(h_c, B), w, tr_c), B)
        oc_c = _s5_finish(l, y_c, w_glu, b_glu, tm_c)
    return oc_c, oc_x


def _mla_prep_kernel(*refs, rope):
    if rope:
        pd_ref, qn_ref, kvn_ref, wq_ref, wqs_ref, wk_ref, wks_ref, wv_ref, cos_ref, sin_ref, q_ref, k_ref, v_ref = refs
    else:
        pd_ref, qn_ref, kvn_ref, wq_ref, wk_ref, wv_ref, q_ref, k_ref, v_ref = refs
    cq = pd_ref[:, 0:256]
    ms = jnp.sum(cq * cq, axis=-1, keepdims=True) * (1.0 / MLA_Q_RANK)
    cqn = (cq * lax.rsqrt(ms + EPS) * qn_ref[...]).astype(BF16)
    ck = pd_ref[:, 256:384]
    lane = lax.broadcasted_iota(jnp.int32, ck.shape, 1)
    is_lat = lane < MLA_KV_RANK
    ms = jnp.sum(jnp.where(is_lat, ck * ck, 0.0), axis=-1, keepdims=True) * (1.0 / MLA_KV_RANK)
    ckn = jnp.where(is_lat, ck * lax.rsqrt(ms + EPS) * kvn_ref[...], ck).astype(BF16)
    q = _dot(cqn, wq_ref[...])
    k = _dot(ckn, wk_ref[...])
    if rope:
        cos = jnp.concatenate([cos_ref[...]] * MLA_HEADS, axis=1)
        sin = jnp.concatenate([sin_ref[...]] * MLA_HEADS, axis=1)
        q = q * cos + _dot(cqn, wqs_ref[...]) * sin
        k = k * cos + _dot(ckn, wks_ref[...]) * sin
    q_ref[...] = (q * ((MLA_NOPE + MLA_ROPE) ** -0.5 * LOG2E)).astype(BF16)
    k_ref[...] = k.astype(BF16)
    v_ref[...] = _dot(ckn, wv_ref[...]).astype(BF16)


def _mla_prep(l, pd, qn, kvn, wts, tables, tm, n):
    R = pd.shape[0]
    HP = MLA_HEADS * MLA_HEAD_PAD
    wq, wqs, wk, wks, wv = wts
    rope = tables is not None
    wspec = lambda r: pl.BlockSpec((None, r, HP), lambda i: (l, 0, 0))
    in_specs = [pl.BlockSpec((tm, 384), lambda i: (i, 0)), _vec_spec(l, 256), _vec_spec(l, 128)]
    if rope:
        npt = n // tm
        tspec = pl.BlockSpec((tm, MLA_HEAD_PAD), lambda i: (i % npt, 0))
        in_specs += [wspec(256), wspec(256), wspec(128), wspec(128), wspec(128), tspec, tspec]
        args = (pd, qn, kvn, wq, wqs, wk, wks, wv) + tuple(tables)
    else:
        in_specs += [wspec(256), wspec(128), wspec(128)]
        args = (pd, qn, kvn, wq, wk, wv)
    return pl.pallas_call(
        functools.partial(_mla_prep_kernel, rope=rope),
        grid=(R // tm,),
        in_specs=in_specs,
        out_specs=[pl.BlockSpec((tm, HP), lambda i: (i, 0))] * 3,
        out_shape=[jax.ShapeDtypeStruct((R, HP), BF16)] * 3,
        compiler_params=_cparams("parallel"),
        name="mla_prep",
    )(*args)


def _attn_kernel(q_ref, k_ref, v_ref, o_ref, m_scr, l_scr, acc_scr):
    kv = pl.program_id(2)
    HP = MLA_HEAD_PAD

    @pl.when(kv == 0)
    def _():
        m_scr[...] = jnp.full(m_scr.shape, -jnp.inf, F32)
        l_scr[...] = jnp.zeros(l_scr.shape, F32)
        acc_scr[...] = jnp.zeros(acc_scr.shape, F32)

    for h in range(MLA_HEADS):
        lanes = slice(h * HP, (h + 1) * HP)
        s = _dot_nt(q_ref[:, lanes], k_ref[:, lanes])
        m_prev = m_scr[h, :, 0:1]
        l_prev = l_scr[h, :, 0:1]
        m_new = jnp.maximum(m_prev, jnp.max(s, axis=1, keepdims=True))
        alpha = jnp.exp2(m_prev - m_new)
        p = jnp.exp2(s - m_new)
        l_new = alpha * l_prev + jnp.sum(p, axis=1, keepdims=True)
        acc_scr[h] = alpha * acc_scr[h] + _dot(p.astype(BF16), v_ref[:, lanes])
        m_scr[h] = jnp.broadcast_to(m_new, m_scr.shape[1:])
        l_scr[h] = jnp.broadcast_to(l_new, l_scr.shape[1:])

    @pl.when(kv == pl.num_programs(2) - 1)
    def _():
        outs = [acc_scr[h][:, 0:MLA_V] / l_scr[h, :, 0:1] for h in range(MLA_HEADS)]
        o_ref[...] = jnp.concatenate(outs, axis=1).astype(o_ref.dtype)


def _attention(q3, k3, v3, tq, tk):
    B, nq, HP = q3.shape
    nk = k3.shape[1]
    return pl.pallas_call(
        _attn_kernel,
        grid=(B, nq // tq, nk // tk),
        in_specs=[pl.BlockSpec((None, tq, HP), lambda b, i, j: (b, i, 0)),
                  pl.BlockSpec((None, tk, HP), lambda b, i, j: (b, j, 0)),
                  pl.BlockSpec((None, tk, HP), lambda b, i, j: (b, j, 0))],
        out_specs=pl.BlockSpec((None, tq, MLA_HEADS * MLA_V), lambda b, i, j: (b, i, 0)),
        out_shape=jax.ShapeDtypeStruct((B, nq, MLA_HEADS * MLA_V), BF16),
        scratch_shapes=[pltpu.VMEM((MLA_HEADS, tq, 128), F32),
                        pltpu.VMEM((MLA_HEADS, tq, 128), F32),
                        pltpu.VMEM((MLA_HEADS, tq, MLA_HEAD_PAD), F32)],
        compiler_params=_cparams("parallel", "parallel", "arbitrary"),
        name="attention",
    )(q3, k3, v3)


def _outproj_kernel(oa, ob, oc, od, w_ref, x_ref, gpost_ref, gt_ref, gpre_ref, sc_ref, sh_ref, xo_ref, h_ref):
    W = GROUP_W
    mix = _dot(oa[...], w_ref[0:W, :])
    mix = mix + _dot(ob[...], w_ref[W:2 * W, :])
    mix = mix + _dot(oc[...], w_ref[2 * W:3 * W, :])
    mix = mix + _dot(od[...], w_ref[3 * W:4 * W, :])
    x = x_ref[...] + gt_ref[...] * (_rms(mix) * gpost_ref[...])
    xo_ref[...] = x
    h_ref[...] = (_rms(x) * gpre_ref[...] * (1.0 + sc_ref[...]) + sh_ref[...]).astype(h_ref.dtype)


def _outproj(l, parts, w_out, x2, mod4, brow, g_post, g_pre_ffn, tm):
    R, D = x2.shape
    W = GROUP_W
    part_spec = pl.BlockSpec((tm, W), lambda i: (i, 0))
    row_spec = pl.BlockSpec((tm, D), lambda i: (i, 0))
    return pl.pallas_call(
        _outproj_kernel,
        grid=(R // tm,),
        in_specs=[part_spec] * 4 + [
            pl.BlockSpec((None, 4 * W, D), lambda i: (l, 0, 0)),
            row_spec,
            _vec_spec(l, D),
            _mod_spec(l, 2, D, brow),
            _vec_spec(l, D),
            _mod_spec(l, 4, D, brow),
            _mod_spec(l, 3, D, brow),
        ],
        out_specs=[row_spec, row_spec],
        out_shape=[jax.ShapeDtypeStruct((R, D), F32), jax.ShapeDtypeStruct((R, D), BF16)],
        compiler_params=_cparams("parallel"),
        name="outproj",
    )(*parts, w_out, x2, g_post, mod4, g_pre_ffn, mod4, mod4)


FFN_HALO = 16
FFN_COLS = 256


def _ffn_kernel(hp_ref, h_ref, hn_ref, wup_ref, cw_ref, cb_ref, wdn_ref, x_ref, gpost_ref, gt_ref, o_ref):
    i = pl.program_id(1)
    tm = h_ref.shape[0]
    dff = wdn_ref.shape[0]
    rows = tm + 2 * FFN_HALO
    prev = jnp.where(i == 0, jnp.zeros_like(hp_ref[...]), hp_ref[...])
    nxt = jnp.where(i == pl.num_programs(1) - 1, jnp.zeros_like(hn_ref[...]), hn_ref[...])
    hb = jnp.concatenate([prev, h_ref[...], nxt], axis=0)

    def conv(z, cols):
        w = cw_ref[:, cols]
        out = cb_ref[:, cols] + w[0:1] * pltpu.roll(z, 1, axis=0) + w[1:2] * z + w[2:3] * pltpu.roll(z, rows - 1, axis=0)
        return out[FFN_HALO:FFN_HALO + tm]

    acc = jnp.zeros(o_ref.shape, F32)
    for j in range(dff // FFN_COLS):
        ca = slice(j * FFN_COLS, (j + 1) * FFN_COLS)
        cg = slice(dff + j * FFN_COLS, dff + (j + 1) * FFN_COLS)
        a = conv(_dot(hb, wup_ref[:, ca]), ca)
        g = conv(_dot(hb, wup_ref[:, cg]), cg)
        acc = acc + _dot((_gelu(a) * g).astype(BF16), wdn_ref[ca, :])
    o_ref[...] = x_ref[...] + gt_ref[...] * (_rms(acc) * gpost_ref[...])


def _ffn(l, h3, x3, w_up, conv_w, conv_b, w_down, mod4, brow, g_post, tm):
    B, n, D = x3.shape
    nt = n // tm
    hb = tm // FFN_HALO
    nh = n // FFN_HALO
    dff = w_down.shape[1]
    once = pl.Buffered(1)
    return pl.pallas_call(
        _ffn_kernel,
        grid=(B, nt),
        in_specs=[
            pl.BlockSpec((None, FFN_HALO, D), lambda b, i: (b, jnp.maximum(i * hb - 1, 0), 0)),
            pl.BlockSpec((None, tm, D), lambda b, i: (b, i, 0)),
            pl.BlockSpec((None, FFN_HALO, D), lambda b, i: (b, jnp.minimum((i + 1) * hb, nh - 1), 0)),
            pl.BlockSpec((None, D, 2 * dff), lambda b, i: (l, 0, 0), pipeline_mode=once),
            pl.BlockSpec((None, 3, 2 * dff), lambda b, i: (l, 0, 0)),
            pl.BlockSpec((None, 1, 2 * dff), lambda b, i: (l, 0, 0)),
            pl.BlockSpec((None, dff, D), lambda b, i: (l, 0, 0), pipeline_mode=once),
            pl.BlockSpec((None, tm, D), lambda b, i: (b, i, 0)),
            _vec_spec(l, D),
            _mod_spec(l, 5, D, brow),
        ],
        out_specs=pl.BlockSpec((None, tm, D), lambda b, i: (b, i, 0)),
        out_shape=jax.ShapeDtypeStruct((B, n, D), F32),
        compiler_params=_cparams("parallel", "parallel"),
        name="conv_ffn",
    )(h3, h3, h3, w_up, conv_w, conv_b, w_down, x3, g_post, mod4)


def _prep_w_in(w_in):
    z = lambda n: jnp.zeros(w_in.shape[:-1] + (n,), w_in.dtype)
    a = w_in[..., 0:512]
    b = w_in[..., 512:1536]
    gl = w_in[..., 1536:1568]
    c = w_in[..., 1568:1824]
    cq = w_in[..., 1824:2048]
    ckv_kr = w_in[..., 2048:2176]
    return jnp.concatenate([a, b, c, cq, z(32), ckv_kr, gl, z(96)], axis=-1).astype(BF16)


def _rope_swap(t):
    q = MLA_ROPE // 4
    return jnp.concatenate([t[..., q:2 * q], t[..., 0:q], t[..., 3 * q:4 * q], t[..., 2 * q:3 * q]], axis=-1)


def _prep_mla(w_uq, w_ukv):
    L = w_uq.shape[0]
    H, NP, RP, HP = MLA_HEADS, MLA_NOPE, MLA_ROPE, MLA_HEAD_PAD
    wq = w_uq.reshape(L, MLA_Q_RANK, H, NP + RP)
    zq = jnp.zeros((L, MLA_Q_RANK, H, HP - NP - RP), w_uq.dtype)
    znope = jnp.zeros((L, MLA_Q_RANK, H, NP), w_uq.dtype)
    q_main = jnp.concatenate([wq, zq], axis=-1)
    q_swap = jnp.concatenate([znope, _rope_swap(wq[..., NP:]), zq], axis=-1)
    padq = lambda w: jnp.pad(w.reshape(L, MLA_Q_RANK, H * HP), ((0, 0), (0, 256 - MLA_Q_RANK), (0, 0)))

    wkv = w_ukv.reshape(L, MLA_KV_RANK, H, NP + MLA_V)
    zk = jnp.zeros((L, MLA_KV_RANK, H, HP - NP), w_ukv.dtype)
    k_lat = jnp.concatenate([wkv[..., :NP], zk], axis=-1)
    eye = jnp.eye(RP, dtype=w_ukv.dtype)
    place = lambda e: jnp.broadcast_to(
        jnp.concatenate([jnp.zeros((RP, NP), e.dtype), e, jnp.zeros((RP, HP - NP - RP), e.dtype)], axis=-1)[None, :, None, :],
        (L, RP, H, HP))
    k_main = jnp.concatenate([k_lat, place(eye)], axis=1)
    k_swap = jnp.concatenate([jnp.zeros_like(k_lat), place(_rope_swap(eye))], axis=1)
    v_lat = jnp.concatenate([wkv[..., NP:], jnp.zeros((L, MLA_KV_RANK, H, HP - MLA_V), w_ukv.dtype)], axis=-1)
    v_main = jnp.concatenate([v_lat, jnp.zeros((L, RP, H, HP), w_ukv.dtype)], axis=1)
    flat = lambda w: w.reshape(L, w.shape[1], H * HP).astype(BF16)
    return (padq(q_main).astype(BF16), padq(q_swap).astype(BF16), flat(k_main), flat(k_swap), flat(v_main))


def _rope_tables(n):
    rows = n // GRID_W
    row = jnp.repeat(jnp.arange(rows, dtype=F32), GRID_W)
    col = jnp.tile(jnp.arange(GRID_W, dtype=F32), rows)
    nf = MLA_ROPE // 4
    inv = ROPE_BASE ** (-jnp.arange(nf, dtype=F32) / nf)
    ar = row[:, None] * inv[None, :]
    ac = col[:, None] * inv[None, :]
    one = jnp.ones((n, MLA_NOPE), F32)
    zero = jnp.zeros((n, MLA_HEAD_PAD - MLA_NOPE - MLA_ROPE), F32)
    cos = jnp.concatenate([one, jnp.cos(ar), jnp.cos(ar), jnp.cos(ac), jnp.cos(ac), zero], axis=1)
    sin = jnp.concatenate([0.0 * one, -jnp.sin(ar), jnp.sin(ar), -jnp.sin(ac), jnp.sin(ac), zero], axis=1)
    return cos, sin


def _pick_tile(n, want):
    t = min(n, want)
    while n % t:
        t //= 2
    return t


def kernel(x, c, ctx, c_ctx, w_mod, b_mod, g_pre_mix, g_post_mix, g_pre_ffn, g_post_ffn, w_in,
           sgu_norm, sgu_w, sgu_b, gla_w_gate, gla_b_gate, gla_norm,
           s5_a_re, s5_a_im, s5_log_dt, s5_b_re, s5_b_im, s5_c_re, s5_c_im, s5_d, s5_w_glu, s5_b_glu,
           mla_q_norm, mla_w_uq, mla_kv_norm, mla_w_ukv, w_out,
           ffn_w_up, ffn_conv_w, ffn_conv_b, ffn_w_down):
    B, n, D = x.shape
    nctx = ctx.shape[1]
    L = w_mod.shape[0]
    W = GROUP_W
    assert B < 8 and n % 512 == 0 and nctx % 128 == 0 and n % GRID_W == 0

    c8 = jnp.concatenate([c, c_ctx[None, :], jnp.zeros((8 - B - 1, D), F32)], axis=0)
    mod4 = _modulation(c8, w_mod, b_mod).reshape(L, 8, 1, 6 * D)
    vec = lambda p: p.reshape(L, 1, -1).astype(F32)
    g_pre_mix, g_post_mix, g_pre_ffn, g_post_ffn = map(vec, (g_pre_mix, g_post_mix, g_pre_ffn, g_post_ffn))
    w_in_p = _prep_w_in(w_in)
    sgu_gn = vec(sgu_norm)
    sgu_w_st = sgu_w.reshape(L, -1, MLP_CHUNK).astype(BF16)
    sgu_bias = jnp.repeat(jnp.swapaxes(sgu_b, 1, 2), HEAD_D, axis=2).astype(F32)
    ones_bd = jnp.kron(jnp.eye(W // HEAD_D, dtype=F32), jnp.ones((HEAD_D, HEAD_D), F32)).astype(BF16)
    gla_wg = jnp.zeros((L, 2, 128, W), F32)
    gla_wg = gla_wg.at[:, 0, 0:GATE_RANK].set(gla_w_gate[:, 0]).at[:, 1, GATE_RANK:2 * GATE_RANK].set(gla_w_gate[:, 1])
    gla_bg = gla_b_gate.reshape(L, 2, 1, W).astype(F32)
    gla_gn = vec(gla_norm)
    s5_wglu = s5_w_glu.astype(BF16)
    s5_bglu = vec(s5_b_glu)
    mla_qn = jnp.pad(mla_q_norm, ((0, 0), (0, 256 - MLA_Q_RANK))).reshape(L, 1, 256).astype(F32)
    mla_kvn = jnp.pad(mla_kv_norm, ((0, 0), (0, 128 - MLA_KV_RANK))).reshape(L, 1, 128).astype(F32)
    mla_wts = _prep_mla(mla_w_uq, mla_w_ukv)
    tables = _rope_tables(n)
    w_out_b = w_out.astype(BF16)
    w_up_b = ffn_w_up.astype(BF16)
    w_down_b = ffn_w_down.astype(BF16)
    conv_w = ffn_conv_w.astype(F32)
    conv_b = ffn_conv_b.reshape(L, 1, -1).astype(F32)

    tm_x = _pick_tile(n, 512)
    tm_c = _pick_tile(nctx, 512)
    tpb_x = n // tm_x
    brow_x = lambda i: i // tpb_x
    brow_c = lambda *g: B
    brow_x2 = lambda b, i: b

    xs = x.reshape(B * n, D)
    cs = ctx.reshape(B * nctx, D)

    for l in range(L):
        need_ctx = l < L - 1
        pa_x, pb_x, pc_x, pd_x, pg_x = _inproj(l, xs, mod4, brow_x, g_pre_mix, w_in_p, tm_x)
        pa_c, pb_c, pc_c, pd_c, pg_c = _inproj(l, cs, mod4, brow_c, g_pre_mix, w_in_p, tm_c)

        oa_x = _sgu(l, pa_x, sgu_gn, sgu_w_st, sgu_bias, ones_bd, tm_x)
        r3 = lambda t, m: t.reshape(B, m, t.shape[-1])
        ob_c, ob_x = _gla(l, r3(pb_c, nctx), r3(pg_c, nctx), r3(pb_x, n), r3(pg_x, n),
                          gla_wg, gla_bg, gla_gn, ones_bd, _pick_tile(nctx, 256), _pick_tile(n, 256))
        mats = _s5_matrices(s5_a_re[l], s5_a_im[l], s5_log_dt[l], s5_b_re[l], s5_b_im[l],
                            s5_c_re[l], s5_c_im[l], s5_d[l])
        oc_c, oc_x = _s5(l, r3(pc_c, nctx), r3(pc_x, n), mats, s5_wglu, s5_bglu, need_ctx, tm_c, tm_x)
        wq, wqs, wk, wks, wv = mla_wts
        q_x, k_x, v_x = _mla_prep(l, pd_x, mla_qn, mla_kvn, mla_wts, tables, tm_x, n)
        q_c, k_c, v_c = _mla_prep(l, pd_c, mla_qn, mla_kvn, (wq, None, wk, None, wv), None, tm_c, nctx)
        k_all = jnp.concatenate([r3(k_c, nctx), r3(k_x, n)], axis=1)
        v_all = jnp.concatenate([r3(v_c, nctx), r3(v_x, n)], axis=1)
        nk = n + nctx
        tk = next(t for t in (1408, 768, 512, 256, 128) if nk % t == 0)
        od_x = _attention(r3(q_x, n), k_all, v_all, tm_x, tk).reshape(B * n, W)

        xs, hx = _outproj(l, (oa_x, ob_x.reshape(B * n, W), oc_x, od_x), w_out_b, xs, mod4, brow_x,
                          g_post_mix, g_pre_ffn, tm_x)
        xs = _ffn(l, hx.reshape(B, n, D), xs.reshape(B, n, D), w_up_b, conv_w, conv_b, w_down_b, mod4, brow_x2,
                  g_post_ffn, tm_x).reshape(B * n, D)

        if need_ctx:
            oa_c = _sgu(l, pa_c, sgu_gn, sgu_w_st, sgu_bias, ones_bd, tm_c)
            od_c = _attention(r3(q_c, nctx), r3(k_c, nctx), r3(v_c, nctx), tm_c, nctx).reshape(B * nctx, W)
            cs, hc = _outproj(l, (oa_c, ob_c.reshape(B * nctx, W), oc_c, od_c), w_out_b, cs, mod4, brow_c,
                              g_post_mix, g_pre_ffn, tm_c)
            cs = _ffn(l, hc.reshape(B, nctx, D), cs.reshape(B, nctx, D), w_up_b, conv_w, conv_b, w_down_b, mod4,
                      brow_c, g_post_ffn, tm_c).reshape(B * nctx, D)
    return xs.reshape(B, n, D)
```

```python
import functools
import math

import jax
import jax.numpy as jnp
from jax import lax
from jax.experimental import pallas as pl
from jax.experimental.pallas import tpu as pltpu

F32 = jnp.float32
BF16 = jnp.bfloat16

EPS = 1e-6
GRID_W = 64
GROUP_W = 256
HEAD_D = 64
MLP_CHUNK = 128
GATE_RANK = 16
GATE_TEMP = 16.0
GLA_CHUNK = 64
S5_IN = 16
S5_STATE = 64
S5_T = 16
MLA_HEADS = 4
MLA_NOPE = 64
MLA_ROPE = 32
MLA_V = 64
MLA_Q_RANK = 224
MLA_KV_RANK = 96
MLA_HEAD_PAD = 128
ROPE_BASE = 10000.0
LOG2E = 1.4426950408889634

VMEM_LIMIT = 48 * 1024 * 1024


def _cparams(*sem):
    return pltpu.CompilerParams(dimension_semantics=sem, vmem_limit_bytes=VMEM_LIMIT)


def _dot(a, b):
    return jnp.dot(a, b, preferred_element_type=F32)


def _dot_nt(a, b):
    return lax.dot_general(a, b, (((1,), (1,)), ((), ())), preferred_element_type=F32)


def _dot_tn(a, b):
    return lax.dot_general(a, b, (((0,), (0,)), ((), ())), preferred_element_type=F32)


def _split(a):
    hi = a.astype(BF16)
    lo = (a - hi.astype(F32)).astype(BF16)
    return hi, lo


def _dot_x2(a, b_bf16):
    hi, lo = _split(a)
    return _dot(hi, b_bf16) + _dot(lo, b_bf16)


def _dot_x3(a, b):
    ah, al = _split(a)
    bh, bl = _split(b)
    return _dot(ah, bh) + _dot(al, bh) + _dot(ah, bl)


def _rms(x):
    return x * lax.rsqrt(jnp.mean(x * x, axis=-1, keepdims=True) + EPS)


def _gelu(x):
    return 0.5 * x * (1.0 + jnp.tanh(0.7978845608028654 * (x + 0.044715 * (x * x * x))))


def _sigmoid(x):
    return 1.0 / (1.0 + jnp.exp(-x))


def _lane_group(shape, width):
    return lax.broadcasted_iota(jnp.int32, shape, len(shape) - 1) // width


def _mod_kernel(c_ref, w_ref, b_ref, o_ref):
    c = c_ref[...]
    s = c * _sigmoid(c)
    o_ref[...] = _dot_x3(s, w_ref[...]) + b_ref[...]


def _modulation(c8, w_mod, b_mod):
    L, D, W = w_mod.shape
    tn = 1536
    return pl.pallas_call(
        _mod_kernel,
        grid=(L, W // tn),
        in_specs=[
            pl.BlockSpec((8, D), lambda l, j: (0, 0)),
            pl.BlockSpec((None, D, tn), lambda l, j: (l, 0, j)),
            pl.BlockSpec((None, 1, tn), lambda l, j: (l, 0, j)),
        ],
        out_specs=pl.BlockSpec((None, 8, tn), lambda l, j: (l, 0, j)),
        out_shape=jax.ShapeDtypeStruct((L, 8, W), F32),
        compiler_params=_cparams("arbitrary", "arbitrary"),
        name="modulation",
    )(c8, w_mod, b_mod.reshape(L, 1, W))


def _mod_spec(l, j, D, bfn):
    return pl.BlockSpec((None, None, 1, D), lambda *g: (l, bfn(*g), 0, j))


def _vec_spec(l, width):
    return pl.BlockSpec((None, 1, width), lambda *g: (l, 0, 0))


IN_SLABS = (("a", 0, 512), ("b", 512, 1024), ("c", 1536, 256), ("d", 1792, 384), ("g", 2176, 128))
IN_PAD_COLS = 2304


def _inproj_kernel(x_ref, g_ref, sc_ref, sh_ref, w_ref, oa, ob, oc, od, og):
    h = _rms(x_ref[...]) * g_ref[...] * (1.0 + sc_ref[...]) + sh_ref[...]
    hb = h.astype(BF16)
    for (_, off, width), o_ref in zip(IN_SLABS, (oa, ob, oc, od, og)):
        o_ref[...] = _dot(hb, w_ref[:, off:off + width])


def _inproj(l, x2, mod4, brow, g_pre, w_in_p, tm):
    R, D = x2.shape
    grid = (R // tm,)
    outs = [jax.ShapeDtypeStruct((R, width), F32) for (_, _, width) in IN_SLABS]
    return pl.pallas_call(
        _inproj_kernel,
        grid=grid,
        in_specs=[
            pl.BlockSpec((tm, D), lambda i: (i, 0)),
            _vec_spec(l, D),
            _mod_spec(l, 1, D, brow),
            _mod_spec(l, 0, D, brow),
            pl.BlockSpec((None, D, IN_PAD_COLS), lambda i: (l, 0, 0)),
        ],
        out_specs=[pl.BlockSpec((tm, width), lambda i: (i, 0)) for (_, _, width) in IN_SLABS],
        out_shape=outs,
        compiler_params=_cparams("parallel"),
        name="inproj",
    )(x2, g_pre, mod4, mod4, w_in_p)


def _sgu_kernel(p_ref, gn_ref, w_ref, b_ref, ones_ref, o_ref):
    tm = p_ref.shape[0]
    g = _gelu(p_ref[...])
    u = g[:, :GROUP_W]
    v = g[:, GROUP_W:]
    ms = _dot_x2(v * v, ones_ref[...]) * (1.0 / HEAD_D)
    vb = (v * lax.rsqrt(ms + EPS) * gn_ref[...]).astype(BF16)
    head = _lane_group((MLP_CHUNK, GROUP_W), HEAD_D)
    w = w_ref[...]
    for c in range(tm // MLP_CHUNK):
        rows = slice(c * MLP_CHUNK, (c + 1) * MLP_CHUNK)
        r = _dot(w, vb[rows])
        s = b_ref[...]
        for h in range(GROUP_W // HEAD_D):
            s = s + jnp.where(head == h, r[h * MLP_CHUNK:(h + 1) * MLP_CHUNK], 0.0)
        o_ref[rows, :] = (u[rows] * s).astype(o_ref.dtype)


def _sgu(l, pa, gn, w_st, bias, ones_bd, tm):
    R = pa.shape[0]
    H = GROUP_W // HEAD_D
    return pl.pallas_call(
        _sgu_kernel,
        grid=(R // tm,),
        in_specs=[
            pl.BlockSpec((tm, 2 * GROUP_W), lambda i: (i, 0)),
            _vec_spec(l, GROUP_W),
            pl.BlockSpec((None, H * MLP_CHUNK, MLP_CHUNK), lambda i: (l, 0, 0)),
            pl.BlockSpec((None, MLP_CHUNK, GROUP_W), lambda i: (l, 0, 0)),
            pl.BlockSpec((GROUP_W, GROUP_W), lambda i: (0, 0)),
        ],
        out_specs=pl.BlockSpec((tm, GROUP_W), lambda i: (i, 0)),
        out_shape=jax.ShapeDtypeStruct((R, GROUP_W), BF16),
        compiler_params=_cparams("parallel"),
        name="sgu",
    )(pa, gn, w_st, bias, ones_bd)


def _gla_kernel(*refs, rev, finish):
    if finish:
        (pb_ref, pg_ref, wg_ref, bg_ref, s0_ref, ones_ref, oprev_ref, gn_ref, o_ref, sfin_ref, st_scr) = refs
    else:
        (pb_ref, pg_ref, wg_ref, bg_ref, s0_ref, ones_ref, o_ref, sfin_ref, st_scr) = refs
    i = pl.program_id(1)
    C = GLA_CHUNK
    W = GROUP_W
    H = W // HEAD_D
    tt = pb_ref.shape[0]

    @pl.when(i == 0)
    def _():
        st_scr[...] = s0_ref[...]

    z = _dot_x3(pg_ref[...], wg_ref[...]) + bg_ref[...]
    logg = (jnp.minimum(z, 0.0) - jnp.log(1.0 + jnp.exp(-jnp.abs(z)))) * (1.0 / GATE_TEMP)
    q = pb_ref[:, 0:W] * (HEAD_D ** -0.5)
    k = pb_ref[:, W:2 * W]
    v = pb_ref[:, 2 * W:3 * W]

    ri = lax.broadcasted_iota(jnp.int32, (C, C), 0)
    ci = lax.broadcasted_iota(jnp.int32, (C, C), 1)
    tri = (ri <= ci) if rev else (ri >= ci)
    tri_b = jnp.where(tri, 1.0, 0.0).astype(BF16)
    tri4 = jnp.concatenate([tri] * H, axis=0)
    head_c = _lane_group((C, W), HEAD_D)
    bd = (lax.broadcasted_iota(jnp.int32, (W, W), 0) // HEAD_D) == _lane_group((W, W), HEAD_D)

    order = range(tt // C - 1, -1, -1) if rev else range(tt // C)
    for j in order:
        rows = slice(j * C, (j + 1) * C)
        g_hi, g_lo = _split(logg[rows])
        cum = _dot(tri_b, g_hi) + _dot(tri_b, g_lo)
        tot = cum[0:1] if rev else cum[C - 1:C]
        qin = (q[rows] * jnp.exp(cum)).astype(BF16)
        kin = (k[rows] * jnp.exp(-cum)).astype(BF16)
        kend = (k[rows] * jnp.exp(tot - cum)).astype(BF16)
        vb = v[rows].astype(BF16)
        qst = jnp.concatenate([jnp.where(head_c == h, qin, jnp.zeros_like(qin)) for h in range(H)], axis=0)
        sc = _dot_nt(qst, kin)
        sc = jnp.where(tri4, sc, 0.0).astype(BF16)
        ost = _dot(sc, vb)
        st = st_scr[...]
        o = _dot_nt(qin, st.astype(BF16))
        for h in range(H):
            o = o + jnp.where(head_c == h, ost[h * C:(h + 1) * C], 0.0)
        kvt = _dot_tn(vb, kend)
        st_scr[...] = st * jnp.exp(tot) + jnp.where(bd, kvt, 0.0)
        if finish:
            o = o + oprev_ref[rows, :]
            ms = _dot_x2(o * o, ones_ref[...]) * (1.0 / HEAD_D)
            o = o * lax.rsqrt(ms + EPS) * gn_ref[...]
            r = pb_ref[rows, 3 * W:4 * W]
            o_ref[rows, :] = (o * (r * _sigmoid(r))).astype(o_ref.dtype)
        else:
            o_ref[rows, :] = o

    @pl.when(i == pl.num_programs(1) - 1)
    def _():
        sfin_ref[...] = st_scr[...]


def _gla_pass(l, d, pb3, pg3, wg, bg, s0, ones_bd, oprev, gn, tt):
    B, n, _ = pb3.shape
    nt = n // tt
    rev = d == 1
    finish = oprev is not None
    W = GROUP_W

    def tok(b, i):
        return (b, (nt - 1 - i) if rev else i, 0)

    in_specs = [
        pl.BlockSpec((None, tt, 4 * W), tok),
        pl.BlockSpec((None, tt, 128), tok),
        pl.BlockSpec((None, None, 128, W), lambda b, i: (l, d, 0, 0)),
        pl.BlockSpec((None, None, 1, W), lambda b, i: (l, d, 0, 0)),
        pl.BlockSpec((None, W, W), lambda b, i: (b, 0, 0)),
        pl.BlockSpec((W, W), lambda b, i: (0, 0)),
    ]
    args = [pb3, pg3, wg, bg, s0, ones_bd]
    if finish:
        in_specs += [pl.BlockSpec((None, tt, W), tok), _vec_spec(l, W)]
        args += [oprev, gn]
    return pl.pallas_call(
        functools.partial(_gla_kernel, rev=rev, finish=finish),
        grid=(B, nt),
        in_specs=in_specs,
        out_specs=[pl.BlockSpec((None, tt, W), tok), pl.BlockSpec((None, W, W), lambda b, i: (b, 0, 0))],
        out_shape=[jax.ShapeDtypeStruct((B, n, W), BF16 if finish else F32),
                   jax.ShapeDtypeStruct((B, W, W), F32)],
        scratch_shapes=[pltpu.VMEM((W, W), F32)],
        compiler_params=_cparams("arbitrary", "arbitrary"),
        name="gla_bwd" if rev else "gla_fwd",
    )(*args)


def _gla(l, pb_c, pg_c, pb_x, pg_x, wg, bg, gn, ones_bd, tt_c, tt_x):
    B = pb_x.shape[0]
    zero = jnp.zeros((B, GROUP_W, GROUP_W), F32)
    ofc, sfc = _gla_pass(l, 0, pb_c, pg_c, wg, bg, zero, ones_bd, None, None, tt_c)
    ofx, _ = _gla_pass(l, 0, pb_x, pg_x, wg, bg, sfc, ones_bd, None, None, tt_x)
    ob_c, sbc = _gla_pass(l, 1, pb_c, pg_c, wg, bg, zero, ones_bd, ofc, gn, tt_c)
    ob_x, _ = _gla_pass(l, 1, pb_x, pg_x, wg, bg, sbc, ones_bd, ofx, gn, tt_x)
    return ob_c, ob_x


def _s5_matrices(a_re, a_im, log_dt, b_re, b_im, c_re, c_im, d_skip):
    T = S5_T
    G, P = a_re.shape[1:]
    I = b_re.shape[-1]
    lam = lax.complex(a_re.astype(F32), a_im.astype(F32))
    ldt = lam * jnp.exp(log_dt.astype(F32))[..., None]
    lam_bar = jnp.exp(ldt)
    b_bar = ((lam_bar - 1.0) / lam)[..., None] * lax.complex(b_re.astype(F32), b_im.astype(F32))
    cmat = lax.complex(c_re.astype(F32), c_im.astype(F32))
    steps = jnp.arange(T + 1, dtype=F32)
    pw = jnp.exp(ldt[..., None] * steps)
    taps = jnp.einsum('dgop,dgpk,dgpi->dkgoi', cmat, pw[..., :T], b_bar).real

    s_idx = jnp.arange(T)[:, None]
    t_idx = jnp.arange(T)[None, :]
    df = t_idx - s_idx
    sel = lambda m: m[:, :, None, None, None]
    kf = jnp.where(sel(df >= 0), taps[0][jnp.clip(df, 0, T - 1)], 0.0)
    kb = jnp.where(sel(df <= 0), taps[1][jnp.clip(-df, 0, T - 1)], 0.0)
    skip = jnp.eye(I, dtype=F32)[None, :, :] * d_skip.astype(F32)[:, None, :]
    m = kf + kb + jnp.where(sel(df == 0), skip[None, None], 0.0)
    m = m.transpose(2, 0, 4, 1, 3).reshape(G, T * I, T * I)

    ar = jnp.arange(T)
    pf = pw[0][..., T - 1 - ar]
    pb = pw[1][..., ar]
    bf = jnp.einsum('gps,gpi->gsip', pf, b_bar[0]).reshape(G, T * I, P)
    bb = jnp.einsum('gps,gpi->gsip', pb, b_bar[1]).reshape(G, T * I, P)
    bmat = jnp.concatenate([bf.real, bf.imag, bb.real, bb.imag, bf.imag, bf.real, bb.imag, bb.real], axis=-1)

    cf = jnp.einsum('gop,gpt->gpto', cmat[0], pw[0][..., 1 + ar]).reshape(G, P, T * I)
    cb = jnp.einsum('gop,gpt->gpto', cmat[1], pw[1][..., T - ar]).reshape(G, P, T * I)
    w = jnp.concatenate([m, cf.real, -cf.imag, cb.real, -cb.imag], axis=1)

    a = pw[..., T]
    a1 = jnp.concatenate([a[0].real, a[0].real, a[1].real, a[1].real], axis=-1)
    a2 = jnp.concatenate([-a[0].imag, a[0].imag, -a[1].imag, a[1].imag], axis=-1)
    return bmat.astype(BF16), w.astype(BF16), a1, a2


def _s5_state_kernel(x_ref, b_ref, o_ref):
    o_ref[...] = _dot(x_ref[...], b_ref[...])


def _s5_state(xg, bmat, tr):
    G, R, K = xg.shape
    N = bmat.shape[-1]
    return pl.pallas_call(
        _s5_state_kernel,
        grid=(G, R // tr),
        in_specs=[pl.BlockSpec((None, tr, K), lambda g, i: (g, i, 0)),
                  pl.BlockSpec((None, K, N), lambda g, i: (g, 0, 0))],
        out_specs=pl.BlockSpec((None, tr, N), lambda g, i: (g, i, 0)),
        out_shape=jax.ShapeDtypeStruct((G, R, N), F32),
        compiler_params=_cparams("parallel", "parallel"),
        name="s5_state",
    )(xg, bmat)


def _s5_scan_kernel(sc_ref, sx_ref, a1_ref, a2_ref, hc_ref, hx_ref):
    a1 = a1_ref[...]
    a2 = a2_ref[...]
    half = a1.shape[-1] // 2
    full = a1.shape[-1]

    def run(s_ref, h_ref, carry):
        nc = s_ref.shape[0]

        def body(j, hs):
            h1, h2 = hs
            jr = nc - 1 - j
            sf = s_ref[j]
            sb = s_ref[jr]
            s1 = jnp.concatenate([sf[:, 0:half], sb[:, half:full]], axis=1)
            s2 = jnp.concatenate([sf[:, full:full + half], sb[:, full + half:2 * full]], axis=1)
            h_ref[j, :, 0:half] = h1[:, 0:half]
            h_ref[jr, :, half:full] = h1[:, half:full]
            return (h1 * a1 + h2 * a2 + s1, h2 * a1 - h1 * a2 + s2)

        return lax.fori_loop(0, nc, body, carry)

    zero = jnp.zeros(a1.shape, F32)
    carry = run(sc_ref, hc_ref, (zero, zero))
    run(sx_ref, hx_ref, carry)


def _s5_scan(s_c, s_x, a1, a2):
    ncc, ch, n2 = s_c.shape
    ncx = s_x.shape[0]
    n1 = n2 // 2
    cb = 8
    return pl.pallas_call(
        _s5_scan_kernel,
        grid=(ch // cb,),
        in_specs=[pl.BlockSpec((ncc, cb, n2), lambda i: (0, i, 0)),
                  pl.BlockSpec((ncx, cb, n2), lambda i: (0, i, 0)),
                  pl.BlockSpec((cb, n1), lambda i: (i, 0)),
                  pl.BlockSpec((cb, n1), lambda i: (i, 0))],
        out_specs=[pl.BlockSpec((ncc, cb, n1), lambda i: (0, i, 0)),
                   pl.BlockSpec((ncx, cb, n1), lambda i: (0, i, 0))],
        out_shape=[jax.ShapeDtypeStruct((ncc, ch, n1), F32), jax.ShapeDtypeStruct((ncx, ch, n1), F32)],
        compiler_params=_cparams("parallel"),
        name="s5_scan",
    )(s_c, s_x, a1, a2)


def _s5_out_kernel(x_ref, h_ref, w_ref, o_ref):
    k = x_ref.shape[-1]
    o_ref[...] = _dot(x_ref[...], w_ref[0:k, :]) + _dot(h_ref[...].astype(BF16), w_ref[k:, :])


def _s5_out(xg, hg, w, tr):
    G, R, K = xg.shape
    return pl.pallas_call(
        _s5_out_kernel,
        grid=(G, R // tr),
        in_specs=[pl.BlockSpec((None, tr, K), lambda g, i: (g, i, 0)),
                  pl.BlockSpec((None, tr, K), lambda g, i: (g, i, 0)),
                  pl.BlockSpec((None, 2 * K, K), lambda g, i: (g, 0, 0))],
        out_specs=pl.BlockSpec((None, tr, K), lambda g, i: (g, i, 0)),
        out_shape=jax.ShapeDtypeStruct((G, R, K), F32),
        compiler_params=_cparams("parallel", "parallel"),
        name="s5_out",
    )(xg, hg, w)


def _s5_finish_kernel(y_ref, w_ref, b_ref, o_ref):
    y = _gelu(y_ref[...])
    gate = _sigmoid(_dot(y.astype(BF16), w_ref[...]) + b_ref[...])
    o_ref[...] = (y * gate).astype(o_ref.dtype)


def _s5_finish(l, y2, w_glu, b_glu, tm):
    R, W = y2.shape
    return pl.pallas_call(
        _s5_finish_kernel,
        grid=(R // tm,),
        in_specs=[pl.BlockSpec((tm, W), lambda i: (i, 0)),
                  pl.BlockSpec((None, W, W), lambda i: (l, 0, 0)),
                  _vec_spec(l, W)],
        out_specs=pl.BlockSpec((tm, W), lambda i: (i, 0)),
        out_shape=jax.ShapeDtypeStruct((R, W), BF16),
        compiler_params=_cparams("parallel"),
        name="s5_finish",
    )(y2, w_glu, b_glu)


def _to_groups(u3):
    B, n, W = u3.shape
    G = W // S5_IN
    x = u3.astype(BF16).reshape(B, n // S5_T, S5_T, G, S5_IN).transpose(3, 0, 1, 2, 4)
    return x.reshape(G, B * (n // S5_T), S5_T * S5_IN)


def _chains_major(s, B):
    G, R, N = s.shape
    return s.reshape(G, B, R // B, N).transpose(2, 0, 1, 3).reshape(R // B, G * B, N)


def _groups_major(h, B):
    nc, ch, N = h.shape
    G = ch // B
    return h.reshape(nc, G, B, N).transpose(1, 2, 0, 3).reshape(G, B * nc, N)


def _from_groups(y, B):
    G, R, _ = y.shape
    nc = R // B
    y = y.reshape(G, B, nc, S5_T, S5_IN).transpose(1, 2, 3, 0, 4)
    return y.reshape(B * nc * S5_T, G * S5_IN)


def _s5(l, pc_c, pc_x, mats, w_glu, b_glu, need_ctx, tm_c, tm_x):
    bmat, w, a1, a2 = mats
    B = pc_x.shape[0]
    xg_c, xg_x = _to_groups(pc_c), _to_groups(pc_x)
    tr_c = min(512, xg_c.shape[1])
    tr_x = min(512, xg_x.shape[1])
    s_c = _chains_major(_s5_state(xg_c, bmat, tr_c), B)
    s_x = _chains_major(_s5_state(xg_x, bmat, tr_x), B)
    a1c = jnp.repeat(a1, B, axis=0)
    a2c = jnp.repeat(a2, B, axis=0)
    h_c, h_x = _s5_scan(s_c, s_x, a1c, a2c)
    y_x = _from_groups(_s5_out(xg_x, _groups_major(h_x, B), w, tr_x), B)
    oc_x = _s5_finish(l, y_x, w_glu, b_glu, tm_x)
    oc_c = None
    if need_ctx:
        y_c = _from_groups(_s5_out(xg_c, _groups_major(h_c, B), w, tr_c), B)
        oc_c = _s5_finish(l, y_c, w_glu, b_glu, tm_c)
    return oc_c, oc_x


def _mla_prep_kernel(*refs, rope):
    if rope:
        pd_ref, qn_ref, kvn_ref, wq_ref, wqs_ref, wk_ref, wks_ref, wv_ref, cos_ref, sin_ref, q_ref, k_ref, v_ref = refs
    else:
        pd_ref, qn_ref, kvn_ref, wq_ref, wk_ref, wv_ref, q_ref, k_ref, v_ref = refs
    cq = pd_ref[:, 0:256]
    ms = jnp.sum(cq * cq, axis=-1, keepdims=True) * (1.0 / MLA_Q_RANK)
    cqn = (cq * lax.rsqrt(ms + EPS) * qn_ref[...]).astype(BF16)
    ck = pd_ref[:, 256:384]
    lane = lax.broadcasted_iota(jnp.int32, ck.shape, 1)
    is_lat = lane < MLA_KV_RANK
    ms = jnp.sum(jnp.where(is_lat, ck * ck, 0.0), axis=-1, keepdims=True) * (1.0 / MLA_KV_RANK)
    ckn = jnp.where(is_lat, ck * lax.rsqrt(ms + EPS) * kvn_ref[...], ck).astype(BF16)
    q = _dot(cqn, wq_ref[...])
    k = _dot(ckn, wk_ref[...])
    if rope:
        cos = jnp.concatenate([cos_ref[...]] * MLA_HEADS, axis=1)
        sin = jnp.concatenate([sin_ref[...]] * MLA_HEADS, axis=1)
        q = q * cos + _dot(cqn, wqs_ref[...]) * sin
        k = k * cos + _dot(ckn, wks_ref[...]) * sin
    q_ref[...] = (q * ((MLA_NOPE + MLA_ROPE) ** -0.5 * LOG2E)).astype(BF16)
    k_ref[...] = k.astype(BF16)
    v_ref[...] = _dot(ckn, wv_ref[...]).astype(BF16)


def _mla_prep(l, pd, qn, kvn, wts, tables, tm, n):
    R = pd.shape[0]
    HP = MLA_HEADS * MLA_HEAD_PAD
    wq, wqs, wk, wks, wv = wts
    rope = tables is not None
    wspec = lambda r: pl.BlockSpec((None, r, HP), lambda i: (l, 0, 0))
    in_specs = [pl.BlockSpec((tm, 384), lambda i: (i, 0)), _vec_spec(l, 256), _vec_spec(l, 128)]
    if rope:
        npt = n // tm
        tspec = pl.BlockSpec((tm, MLA_HEAD_PAD), lambda i: (i % npt, 0))
        in_specs += [wspec(256), wspec(256), wspec(128), wspec(128), wspec(128), tspec, tspec]
        args = (pd, qn, kvn, wq, wqs, wk, wks, wv) + tuple(tables)
    else:
        in_specs += [wspec(256), wspec(128), wspec(128)]
        args = (pd, qn, kvn, wq, wk, wv)
    return pl.pallas_call(
        functools.partial(_mla_prep_kernel, rope=rope),
        grid=(R // tm,),
        in_specs=in_specs,
        out_specs=[pl.BlockSpec((tm, HP), lambda i: (i, 0))] * 3,
        out_shape=[jax.ShapeDtypeStruct((R, HP), BF16)] * 3,
        compiler_params=_cparams("parallel"),
        name="mla_prep",
    )(*args)


def _attn_kernel(q_ref, k_ref, v_ref, o_ref, m_scr, l_scr, acc_scr):
    kv = pl.program_id(2)
    HP = MLA_HEAD_PAD
    tk = k_ref.shape[0]

    @pl.when(kv == 0)
    def _():
        m_scr[...] = jnp.full(m_scr.shape, -jnp.inf, F32)
        l_scr[...] = jnp.zeros(l_scr.shape, F32)
        acc_scr[...] = jnp.zeros(acc_scr.shape, F32)

    def scores(h):
        lanes = slice(h * HP, (h + 1) * HP)
        return _dot_nt(q_ref[:, lanes], k_ref[:, lanes])

    s_next = scores(0)
    for h in range(MLA_HEADS):
        lanes = slice(h * HP, (h + 1) * HP)
        s = s_next
        if h + 1 < MLA_HEADS:
            s_next = scores(h + 1)
        m_prev = m_scr[h]
        m_new = jnp.maximum(m_prev, jnp.max(s, axis=1, keepdims=True))
        alpha = jnp.exp2(m_prev - m_new)
        p = jnp.exp2(s - m_new[:, 0:1])
        lp = p[:, 0:128]
        for c in range(1, tk // 128):
            lp = lp + p[:, c * 128:(c + 1) * 128]
        l_scr[h] = alpha * l_scr[h] + lp
        acc_scr[h] = alpha * acc_scr[h] + _dot(p.astype(BF16), v_ref[:, lanes])
        m_scr[h] = m_new

    @pl.when(kv == pl.num_programs(2) - 1)
    def _():
        outs = [acc_scr[h][:, 0:MLA_V] / jnp.sum(l_scr[h], axis=1, keepdims=True) for h in range(MLA_HEADS)]
        o_ref[...] = jnp.concatenate(outs, axis=1).astype(o_ref.dtype)


def _attention(q3, k3, v3, tq, tk):
    B, nq, HP = q3.shape
    nk = k3.shape[1]
    return pl.pallas_call(
        _attn_kernel,
        grid=(B, nq // tq, nk // tk),
        in_specs=[pl.BlockSpec((None, tq, HP), lambda b, i, j: (b, i, 0)),
                  pl.BlockSpec((None, tk, HP), lambda b, i, j: (b, j, 0)),
                  pl.BlockSpec((None, tk, HP), lambda b, i, j: (b, j, 0))],
        out_specs=pl.BlockSpec((None, tq, MLA_HEADS * MLA_V), lambda b, i, j: (b, i, 0)),
        out_shape=jax.ShapeDtypeStruct((B, nq, MLA_HEADS * MLA_V), BF16),
        scratch_shapes=[pltpu.VMEM((MLA_HEADS, tq, 128), F32),
                        pltpu.VMEM((MLA_HEADS, tq, 128), F32),
                        pltpu.VMEM((MLA_HEADS, tq, MLA_HEAD_PAD), F32)],
        compiler_params=_cparams("parallel", "parallel", "arbitrary"),
        name="attention",
    )(q3, k3, v3)


def _outproj_kernel(oa, ob, oc, od, w_ref, x_ref, gpost_ref, gt_ref, gpre_ref, sc_ref, sh_ref, xo_ref, h_ref):
    W = GROUP_W
    mix = _dot(oa[...], w_ref[0:W, :])
    mix = mix + _dot(ob[...], w_ref[W:2 * W, :])
    mix = mix + _dot(oc[...], w_ref[2 * W:3 * W, :])
    mix = mix + _dot(od[...], w_ref[3 * W:4 * W, :])
    x = x_ref[...] + gt_ref[...] * (_rms(mix) * gpost_ref[...])
    xo_ref[...] = x
    h_ref[...] = (_rms(x) * gpre_ref[...] * (1.0 + sc_ref[...]) + sh_ref[...]).astype(h_ref.dtype)


def _outproj(l, parts, w_out, x2, mod4, brow, g_post, g_pre_ffn, tm):
    R, D = x2.shape
    W = GROUP_W
    part_spec = pl.BlockSpec((tm, W), lambda i: (i, 0))
    row_spec = pl.BlockSpec((tm, D), lambda i: (i, 0))
    return pl.pallas_call(
        _outproj_kernel,
        grid=(R // tm,),
        in_specs=[part_spec] * 4 + [
            pl.BlockSpec((None, 4 * W, D), lambda i: (l, 0, 0)),
            row_spec,
            _vec_spec(l, D),
            _mod_spec(l, 2, D, brow),
            _vec_spec(l, D),
            _mod_spec(l, 4, D, brow),
            _mod_spec(l, 3, D, brow),
        ],
        out_specs=[row_spec, row_spec],
        out_shape=[jax.ShapeDtypeStruct((R, D), F32), jax.ShapeDtypeStruct((R, D), BF16)],
        compiler_params=_cparams("parallel"),
        name="outproj",
    )(*parts, w_out, x2, g_post, mod4, g_pre_ffn, mod4, mod4)


FFN_HALO = 16
FFN_COLS = 256


def _ffn_kernel(hp_ref, h_ref, hn_ref, wup_ref, cw_ref, cb_ref, wdn_ref, x_ref, gpost_ref, gt_ref, o_ref):
    i = pl.program_id(1)
    tm = h_ref.shape[0]
    dff = wdn_ref.shape[0]
    rows = tm + 2 * FFN_HALO
    prev = jnp.where(i == 0, jnp.zeros_like(hp_ref[...]), hp_ref[...])
    nxt = jnp.where(i == pl.num_programs(1) - 1, jnp.zeros_like(hn_ref[...]), hn_ref[...])
    hb = jnp.concatenate([prev, h_ref[...], nxt], axis=0)

    def conv(z, cols):
        w = cw_ref[:, cols]
        out = cb_ref[:, cols] + w[0:1] * pltpu.roll(z, 1, axis=0) + w[1:2] * z + w[2:3] * pltpu.roll(z, rows - 1, axis=0)
        return out[FFN_HALO:FFN_HALO + tm]

    def cols(j):
        return slice(j * FFN_COLS, (j + 1) * FFN_COLS), slice(dff + j * FFN_COLS, dff + (j + 1) * FFN_COLS)

    def up(j):
        ca, cg = cols(j)
        return _dot(hb, wup_ref[:, ca]), _dot(hb, wup_ref[:, cg])

    nchunks = dff // FFN_COLS
    acc = jnp.zeros(o_ref.shape, F32)
    z_next = up(0)
    for j in range(nchunks):
        ca, cg = cols(j)
        za, zg = z_next
        if j + 1 < nchunks:
            z_next = up(j + 1)
        a = conv(za, ca)
        g = conv(zg, cg)
        acc = acc + _dot((_gelu(a) * g).astype(BF16), wdn_ref[ca, :])
    o_ref[...] = x_ref[...] + gt_ref[...] * (_rms(acc) * gpost_ref[...])


def _ffn(l, h3, x3, w_up, conv_w, conv_b, w_down, mod4, brow, g_post, tm):
    B, n, D = x3.shape
    nt = n // tm
    hb = tm // FFN_HALO
    nh = n // FFN_HALO
    dff = w_down.shape[1]
    once = pl.Buffered(1)
    return pl.pallas_call(
        _ffn_kernel,
        grid=(B, nt),
        in_specs=[
            pl.BlockSpec((None, FFN_HALO, D), lambda b, i: (b, jnp.maximum(i * hb - 1, 0), 0)),
            pl.BlockSpec((None, tm, D), lambda b, i: (b, i, 0)),
            pl.BlockSpec((None, FFN_HALO, D), lambda b, i: (b, jnp.minimum((i + 1) * hb, nh - 1), 0)),
            pl.BlockSpec((None, D, 2 * dff), lambda b, i: (l, 0, 0), pipeline_mode=once),
            pl.BlockSpec((None, 3, 2 * dff), lambda b, i: (l, 0, 0)),
            pl.BlockSpec((None, 1, 2 * dff), lambda b, i: (l, 0, 0)),
            pl.BlockSpec((None, dff, D), lambda b, i: (l, 0, 0), pipeline_mode=once),
            pl.BlockSpec((None, tm, D), lambda b, i: (b, i, 0)),
            _vec_spec(l, D),
            _mod_spec(l, 5, D, brow),
        ],
        out_specs=pl.BlockSpec((None, tm, D), lambda b, i: (b, i, 0)),
        out_shape=jax.ShapeDtypeStruct((B, n, D), F32),
        compiler_params=_cparams("parallel", "parallel"),
        name="conv_ffn",
    )(h3, h3, h3, w_up, conv_w, conv_b, w_down, x3, g_post, mod4)


def _prep_w_in(w_in):
    z = lambda n: jnp.zeros(w_in.shape[:-1] + (n,), w_in.dtype)
    a = w_in[..., 0:512]
    b = w_in[..., 512:1536]
    gl = w_in[..., 1536:1568]
    c = w_in[..., 1568:1824]
    cq = w_in[..., 1824:2048]
    ckv_kr = w_in[..., 2048:2176]
    return jnp.concatenate([a, b, c, cq, z(32), ckv_kr, gl, z(96)], axis=-1).astype(BF16)


def _rope_swap(t):
    q = MLA_ROPE // 4
    return jnp.concatenate([t[..., q:2 * q], t[..., 0:q], t[..., 3 * q:4 * q], t[..., 2 * q:3 * q]], axis=-1)


def _prep_mla(w_uq, w_ukv):
    L = w_uq.shape[0]
    H, NP, RP, HP = MLA_HEADS, MLA_NOPE, MLA_ROPE, MLA_HEAD_PAD
    wq = w_uq.reshape(L, MLA_Q_RANK, H, NP + RP)
    zq = jnp.zeros((L, MLA_Q_RANK, H, HP - NP - RP), w_uq.dtype)
    znope = jnp.zeros((L, MLA_Q_RANK, H, NP), w_uq.dtype)
    q_main = jnp.concatenate([wq, zq], axis=-1)
    q_swap = jnp.concatenate([znope, _rope_swap(wq[..., NP:]), zq], axis=-1)
    padq = lambda w: jnp.pad(w.reshape(L, MLA_Q_RANK, H * HP), ((0, 0), (0, 256 - MLA_Q_RANK), (0, 0)))

    wkv = w_ukv.reshape(L, MLA_KV_RANK, H, NP + MLA_V)
    zk = jnp.zeros((L, MLA_KV_RANK, H, HP - NP), w_ukv.dtype)
    k_lat = jnp.concatenate([wkv[..., :NP], zk], axis=-1)
    eye = jnp.eye(RP, dtype=w_ukv.dtype)
    place = lambda e: jnp.broadcast_to(
        jnp.concatenate([jnp.zeros((RP, NP), e.dtype), e, jnp.zeros((RP, HP - NP - RP), e.dtype)], axis=-1)[None, :, None, :],
        (L, RP, H, HP))
    k_main = jnp.concatenate([k_lat, place(eye)], axis=1)
    k_swap = jnp.concatenate([jnp.zeros_like(k_lat), place(_rope_swap(eye))], axis=1)
    v_lat = jnp.concatenate([wkv[..., NP:], jnp.zeros((L, MLA_KV_RANK, H, HP - MLA_V), w_ukv.dtype)], axis=-1)
    v_main = jnp.concatenate([v_lat, jnp.zeros((L, RP, H, HP), w_ukv.dtype)], axis=1)
    flat = lambda w: w.reshape(L, w.shape[1], H * HP).astype(BF16)
    return (padq(q_main).astype(BF16), padq(q_swap).astype(BF16), flat(k_main), flat(k_swap), flat(v_main))


def _rope_tables(n):
    rows = n // GRID_W
    row = jnp.repeat(jnp.arange(rows, dtype=F32), GRID_W)
    col = jnp.tile(jnp.arange(GRID_W, dtype=F32), rows)
    nf = MLA_ROPE // 4
    inv = ROPE_BASE ** (-jnp.arange(nf, dtype=F32) / nf)
    ar = row[:, None] * inv[None, :]
    ac = col[:, None] * inv[None, :]
    one = jnp.ones((n, MLA_NOPE), F32)
    zero = jnp.zeros((n, MLA_HEAD_PAD - MLA_NOPE - MLA_ROPE), F32)
    cos = jnp.concatenate([one, jnp.cos(ar), jnp.cos(ar), jnp.cos(ac), jnp.cos(ac), zero], axis=1)
    sin = jnp.concatenate([0.0 * one, -jnp.sin(ar), jnp.sin(ar), -jnp.sin(ac), jnp.sin(ac), zero], axis=1)
    return cos, sin


def _pick_tile(n, want):
    t = min(n, want)
    while n % t:
        t //= 2
    return t


def kernel(x, c, ctx, c_ctx, w_mod, b_mod, g_pre_mix, g_post_mix, g_pre_ffn, g_post_ffn, w_in,
           sgu_norm, sgu_w, sgu_b, gla_w_gate, gla_b_gate, gla_norm,
           s5_a_re, s5_a_im, s5_log_dt, s5_b_re, s5_b_im, s5_c_re, s5_c_im, s5_d, s5_w_glu, s5_b_glu,
           mla_q_norm, mla_w_uq, mla_kv_norm, mla_w_ukv, w_out,
           ffn_w_up, ffn_conv_w, ffn_conv_b, ffn_w_down):
    B, n, D = x.shape
    nctx = ctx.shape[1]
    L = w_mod.shape[0]
    W = GROUP_W
    assert B < 8 and n % 512 == 0 and nctx % 128 == 0 and n % GRID_W == 0

    c8 = jnp.concatenate([c, c_ctx[None, :], jnp.zeros((8 - B - 1, D), F32)], axis=0)
    mod4 = _modulation(c8, w_mod, b_mod).reshape(L, 8, 1, 6 * D)
    vec = lambda p: p.reshape(L, 1, -1).astype(F32)
    g_pre_mix, g_post_mix, g_pre_ffn, g_post_ffn = map(vec, (g_pre_mix, g_post_mix, g_pre_ffn, g_post_ffn))
    w_in_p = _prep_w_in(w_in)
    sgu_gn = vec(sgu_norm)
    sgu_w_st = sgu_w.reshape(L, -1, MLP_CHUNK).astype(BF16)
    sgu_bias = jnp.repeat(jnp.swapaxes(sgu_b, 1, 2), HEAD_D, axis=2).astype(F32)
    ones_bd = jnp.kron(jnp.eye(W // HEAD_D, dtype=F32), jnp.ones((HEAD_D, HEAD_D), F32)).astype(BF16)
    gla_wg = jnp.zeros((L, 2, 128, W), F32)
    gla_wg = gla_wg.at[:, 0, 0:GATE_RANK].set(gla_w_gate[:, 0]).at[:, 1, GATE_RANK:2 * GATE_RANK].set(gla_w_gate[:, 1])
    gla_bg = gla_b_gate.reshape(L, 2, 1, W).astype(F32)
    gla_gn = vec(gla_norm)
    s5_wglu = s5_w_glu.astype(BF16)
    s5_bglu = vec(s5_b_glu)
    mla_qn = jnp.pad(mla_q_norm, ((0, 0), (0, 256 - MLA_Q_RANK))).reshape(L, 1, 256).astype(F32)
    mla_kvn = jnp.pad(mla_kv_norm, ((0, 0), (0, 128 - MLA_KV_RANK))).reshape(L, 1, 128).astype(F32)
    mla_wts = _prep_mla(mla_w_uq, mla_w_ukv)
    tables = _rope_tables(n)
    w_out_b = w_out.astype(BF16)
    w_up_b = ffn_w_up.astype(BF16)
    w_down_b = ffn_w_down.astype(BF16)
    conv_w = ffn_conv_w.astype(F32)
    conv_b = ffn_conv_b.reshape(L, 1, -1).astype(F32)

    tm_x = _pick_tile(n, 512)
    tm_c = _pick_tile(nctx, 512)
    tpb_x = n // tm_x
    brow_x = lambda i: i // tpb_x
    brow_c = lambda *g: B
    brow_x2 = lambda b, i: b

    xs = x.reshape(B * n, D)
    cs = ctx.reshape(B * nctx, D)

    for l in range(L):
        need_ctx = l < L - 1
        pa_x, pb_x, pc_x, pd_x, pg_x = _inproj(l, xs, mod4, brow_x, g_pre_mix, w_in_p, tm_x)
        pa_c, pb_c, pc_c, pd_c, pg_c = _inproj(l, cs, mod4, brow_c, g_pre_mix, w_in_p, tm_c)

        oa_x = _sgu(l, pa_x, sgu_gn, sgu_w_st, sgu_bias, ones_bd, tm_x)
        r3 = lambda t, m: t.reshape(B, m, t.shape[-1])
        ob_c, ob_x = _gla(l, r3(pb_c, nctx), r3(pg_c, nctx), r3(pb_x, n), r3(pg_x, n),
                          gla_wg, gla_bg, gla_gn, ones_bd, _pick_tile(nctx, 256), _pick_tile(n, 256))
        mats = _s5_matrices(s5_a_re[l], s5_a_im[l], s5_log_dt[l], s5_b_re[l], s5_b_im[l],
                            s5_c_re[l], s5_c_im[l], s5_d[l])
        oc_c, oc_x = _s5(l, r3(pc_c, nctx), r3(pc_x, n), mats, s5_wglu, s5_bglu, need_ctx, tm_c, tm_x)
        wq, wqs, wk, wks, wv = mla_wts
        q_x, k_x, v_x = _mla_prep(l, pd_x, mla_qn, mla_kvn, mla_wts, tables, tm_x, n)
        q_c, k_c, v_c = _mla_prep(l, pd_c, mla_qn, mla_kvn, (wq, None, wk, None, wv), None, tm_c, nctx)
        k_all = jnp.concatenate([r3(k_c, nctx), r3(k_x, n)], axis=1)
        v_all = jnp.concatenate([r3(v_c, nctx), r3(v_x, n)], axis=1)
        nk = n + nctx
        tk = next(t for t in (2816, 1408, 768, 512, 256, 128) if nk % t == 0)
        od_x = _attention(r3(q_x, n), k_all, v_all, tm_x, tk).reshape(B * n, W)

        xs, hx = _outproj(l, (oa_x, ob_x.reshape(B * n, W), oc_x, od_x), w_out_b, xs, mod4, brow_x,
                          g_post_mix, g_pre_ffn, tm_x)
        xs = _ffn(l, hx.reshape(B, n, D), xs.reshape(B, n, D), w_up_b, conv_w, conv_b, w_down_b, mod4, brow_x2,
                  g_post_ffn, tm_x).reshape(B * n, D)

        if need_ctx:
            oa_c = _sgu(l, pa_c, sgu_gn, sgu_w_st, sgu_bias, ones_bd, tm_c)
            od_c = _attention(r3(q_c, nctx), r3(k_c, nctx), r3(v_c, nctx), tm_c, nctx).reshape(B * nctx, W)
            cs, hc = _outproj(l, (oa_c, ob_c.reshape(B * nctx, W), oc_c, od_c), w_out_b, cs, mod4, brow_c,
                              g_post_mix, g_pre_ffn, tm_c)
            cs = _ffn(l, hc.reshape(B, nctx, D), cs.reshape(B, nctx, D), w_up_b, conv_w, conv_b, w_down_b, mod4,
                      brow_c, g_post_ffn, tm_c).reshape(B * nctx, D)
    return xs.reshape(B, n, D)
```

```python
import functools

import jax
import jax.numpy as jnp
from jax import lax
from jax.experimental import pallas as pl
from jax.experimental.pallas import tpu as pltpu

F32 = jnp.float32
BF16 = jnp.bfloat16

EPS = 1e-6
GRID_W = 64
GROUP_W = 256
HEAD_D = 64
MLP_CHUNK = 128
GATE_RANK = 16
GATE_TEMP = 16.0
GLA_CHUNK = 64
GLA_TILE = 256
S5_IN = 16
S5_T = 16
S5_TILE = 128
MLA_HEADS = 4
MLA_NOPE = 64
MLA_ROPE = 32
MLA_V = 64
MLA_Q_RANK = 224
MLA_KV_RANK = 96
MLA_HEAD_PAD = 128
ROPE_BASE = 10000.0
LOG2E = 1.4426950408889634

VMEM_LIMIT = 48 * 1024 * 1024


def _cparams(*sem):
    return pltpu.CompilerParams(dimension_semantics=sem, vmem_limit_bytes=VMEM_LIMIT)


def _dot(a, b):
    return jnp.dot(a, b, preferred_element_type=F32)


def _dot_nt(a, b):
    return lax.dot_general(a, b, (((1,), (1,)), ((), ())), preferred_element_type=F32)


def _dot_tn(a, b):
    return lax.dot_general(a, b, (((0,), (0,)), ((), ())), preferred_element_type=F32)


def _split(a):
    hi = a.astype(BF16)
    lo = (a - hi.astype(F32)).astype(BF16)
    return hi, lo


def _dot_x2(a, b_bf16):
    hi, lo = _split(a)
    return _dot(hi, b_bf16) + _dot(lo, b_bf16)


def _dot_x3(a, b):
    ah, al = _split(a)
    bh, bl = _split(b)
    return _dot(ah, bh) + _dot(al, bh) + _dot(ah, bl)


def _rms(x):
    return x * lax.rsqrt(jnp.mean(x * x, axis=-1, keepdims=True) + EPS)


def _gelu(x):
    return 0.5 * x * (1.0 + jnp.tanh(0.7978845608028654 * (x + 0.044715 * (x * x * x))))


def _gelu_gate(a, half_g):
    u = a * (0.7978845608028654 + 0.035677408136300125 * (a * a))
    return (a * half_g) * (1.0 + jnp.tanh(u))


def _sigmoid(x):
    return 1.0 / (1.0 + jnp.exp(-x))


def _lane_group(shape, width):
    return lax.broadcasted_iota(jnp.int32, shape, len(shape) - 1) // width


def _mod_kernel(c_ref, w_ref, b_ref, o_ref):
    c = c_ref[...]
    s = c * _sigmoid(c)
    o_ref[...] = _dot_x3(s, w_ref[...]) + b_ref[...]


def _modulation(c8, w_mod, b_mod):
    L, D, W = w_mod.shape
    tn = 1536
    return pl.pallas_call(
        _mod_kernel,
        grid=(L, W // tn),
        in_specs=[
            pl.BlockSpec((8, D), lambda l, j: (0, 0)),
            pl.BlockSpec((None, D, tn), lambda l, j: (l, 0, j)),
            pl.BlockSpec((None, 1, tn), lambda l, j: (l, 0, j)),
        ],
        out_specs=pl.BlockSpec((None, 8, tn), lambda l, j: (l, 0, j)),
        out_shape=jax.ShapeDtypeStruct((L, 8, W), F32),
        compiler_params=_cparams("arbitrary", "arbitrary"),
        name="modulation",
    )(c8, w_mod, b_mod.reshape(L, 1, W))


def _mod_spec(l, j, D, bfn):
    return pl.BlockSpec((None, None, 1, D), lambda *g: (l, bfn(*g), 0, j))


def _vec_spec(l, width):
    return pl.BlockSpec((None, 1, width), lambda *g: (l, 0, 0))


IN_SLABS = (("a", 0, 512), ("b", 512, 1024), ("c", 1536, 256), ("d", 1792, 384), ("g", 2176, 128))
IN_PAD_COLS = 2304


def _inproj_kernel(x_ref, g_ref, sc_ref, sh_ref, w_ref, oa, ob, oc, od, og):
    h = _rms(x_ref[...]) * g_ref[...] * (1.0 + sc_ref[...]) + sh_ref[...]
    hb = h.astype(BF16)
    for (_, off, width), o_ref in zip(IN_SLABS, (oa, ob, oc, od, og)):
        o_ref[...] = _dot(hb, w_ref[:, off:off + width])


def _inproj(l, x2, mod4, brow, g_pre, w_in_p, tm):
    R, D = x2.shape
    grid = (R // tm,)
    outs = [jax.ShapeDtypeStruct((R, width), F32) for (_, _, width) in IN_SLABS]
    return pl.pallas_call(
        _inproj_kernel,
        grid=grid,
        in_specs=[
            pl.BlockSpec((tm, D), lambda i: (i, 0)),
            _vec_spec(l, D),
            _mod_spec(l, 1, D, brow),
            _mod_spec(l, 0, D, brow),
            pl.BlockSpec((None, D, IN_PAD_COLS), lambda i: (l, 0, 0)),
        ],
        out_specs=[pl.BlockSpec((tm, width), lambda i: (i, 0)) for (_, _, width) in IN_SLABS],
        out_shape=outs,
        compiler_params=_cparams("parallel"),
        name="inproj",
    )(x2, g_pre, mod4, mod4, w_in_p)


def _sgu_kernel(p_ref, gn_ref, w_ref, b_ref, ones_ref, o_ref):
    tm = p_ref.shape[0]
    g = _gelu(p_ref[...])
    u = g[:, :GROUP_W]
    v = g[:, GROUP_W:]
    ms = _dot_x2(v * v, ones_ref[...]) * (1.0 / HEAD_D)
    vb = (v * lax.rsqrt(ms + EPS) * gn_ref[...]).astype(BF16)
    head = _lane_group((MLP_CHUNK, GROUP_W), HEAD_D)
    w = w_ref[...]
    for c in range(tm // MLP_CHUNK):
        rows = slice(c * MLP_CHUNK, (c + 1) * MLP_CHUNK)
        r = _dot(w, vb[rows])
        s = b_ref[...]
        for h in range(GROUP_W // HEAD_D):
            s = s + jnp.where(head == h, r[h * MLP_CHUNK:(h + 1) * MLP_CHUNK], 0.0)
        o_ref[rows, :] = (u[rows] * s).astype(o_ref.dtype)


def _sgu(l, pa, gn, w_st, bias, ones_bd, tm):
    R = pa.shape[0]
    H = GROUP_W // HEAD_D
    return pl.pallas_call(
        _sgu_kernel,
        grid=(R // tm,),
        in_specs=[
            pl.BlockSpec((tm, 2 * GROUP_W), lambda i: (i, 0)),
            _vec_spec(l, GROUP_W),
            pl.BlockSpec((None, H * MLP_CHUNK, MLP_CHUNK), lambda i: (l, 0, 0)),
            pl.BlockSpec((None, MLP_CHUNK, GROUP_W), lambda i: (l, 0, 0)),
            pl.BlockSpec((GROUP_W, GROUP_W), lambda i: (0, 0)),
        ],
        out_specs=pl.BlockSpec((tm, GROUP_W), lambda i: (i, 0)),
        out_shape=jax.ShapeDtypeStruct((R, GROUP_W), BF16),
        compiler_params=_cparams("parallel"),
        name="sgu",
    )(pa, gn, w_st, bias, ones_bd)


def _gla_kernel(*refs, rev, finish):
    if finish:
        (pb_ref, pg_ref, wg_ref, bg_ref, s0_ref, ones_ref, dec_ref, oprev_ref, gn_ref, o_ref, sfin_ref, st_scr) = refs
    else:
        (pb_ref, pg_ref, wg_ref, bg_ref, s0_ref, ones_ref, dec_ref, o_ref, sfin_ref, st_scr) = refs
    i = pl.program_id(1)
    C, W, T = GLA_CHUNK, GROUP_W, GLA_TILE
    H = W // HEAD_D
    nsub = T // C
    ntile = pb_ref.shape[0] // T

    @pl.when(i == 0)
    def _():
        st_scr[...] = s0_ref[...]

    dec = dec_ref[...]
    mask4 = jnp.concatenate([dec[0:T].astype(F32)] * H, axis=0)
    head_t = _lane_group((T, W), HEAD_D)
    bd = (lax.broadcasted_iota(jnp.int32, (W, W), 0) // HEAD_D) == _lane_group((W, W), HEAD_D)
    tiles = list(range(ntile - 1, -1, -1)) if rev else list(range(ntile))
    subs = list(range(nsub - 1, -1, -1)) if rev else list(range(nsub))
    rows = lambda a: slice(a * T, (a + 1) * T)
    sub = lambda j: slice(j * C, (j + 1) * C)

    logg = {}
    for a in tiles:
        z = _dot_x3(pg_ref[rows(a), :], wg_ref[...]) + bg_ref[...]
        logg[a] = (jnp.minimum(z, 0.0) - jnp.log(1.0 + jnp.exp(-jnp.abs(z)))) * (1.0 / GATE_TEMP)
    sums = {}
    for a in tiles:
        g_hi, g_lo = _split(logg[a])
        sums[a] = _dot(dec, g_hi) + _dot(dec, g_lo)
    qin, qoff, kin, kend, vb, tot, ptile = {}, {}, {}, {}, {}, {}, {}
    for a in tiles:
        cum, tot[a], off = sums[a][0:T], sums[a][T:2 * T], sums[a][2 * T:3 * T]
        q = pb_ref[rows(a), 0:W] * (HEAD_D ** -0.5)
        k = pb_ref[rows(a), W:2 * W]
        e = jnp.exp(cum)
        qin[a] = (q * e).astype(BF16)
        qoff[a] = (q * (e * jnp.exp(off))).astype(BF16)
        kin[a] = (k * jnp.exp(-cum)).astype(BF16)
        kend[a] = (k * jnp.exp(tot[a] - cum)).astype(BF16)
        vb[a] = pb_ref[rows(a), 2 * W:3 * W].astype(BF16)
        last = sub(subs[-1])
        ptile[a] = jnp.exp(off[last][0:1] + tot[a][last][0:1])
    sc = {}
    for a in tiles:
        qst = jnp.concatenate([jnp.where(head_t == h, qin[a], jnp.zeros_like(qin[a])) for h in range(H)], axis=0)
        sc[a] = (_dot_nt(qst, kin[a]) * mask4).astype(BF16)
    ost = {a: _dot(sc[a], vb[a]) for a in tiles}
    kvt = {(a, j): jnp.where(bd, _dot_tn(vb[a][sub(j)], kend[a][sub(j)]), 0.0) for a in tiles for j in subs}
    cross, rend = {}, {}
    for a in tiles:
        r = None
        for j in subs:
            if r is not None:
                cross[(a, j)] = _dot_nt(qin[a][sub(j)], r.astype(BF16))
                r = r * jnp.exp(tot[a][sub(j)][0:1]) + kvt[(a, j)]
            else:
                cross[(a, j)] = jnp.zeros((C, W), F32)
                r = kvt[(a, j)]
        rend[a] = r
    st = st_scr[...]
    inter = {}
    for a in tiles:
        inter[a] = _dot_nt(qoff[a], st.astype(BF16))
        st = st * ptile[a] + rend[a]
    st_scr[...] = st

    for a in tiles:
        o = inter[a] + jnp.concatenate([cross[(a, j)] for j in range(nsub)], axis=0)
        for h in range(H):
            o = o + jnp.where(head_t == h, ost[a][h * T:(h + 1) * T], 0.0)
        if finish:
            o = o + oprev_ref[rows(a), :]
            ms = _dot_x2(o * o, ones_ref[...]) * (1.0 / HEAD_D)
            o = o * lax.rsqrt(ms + EPS) * gn_ref[...]
            r = pb_ref[rows(a), 3 * W:4 * W]
            o_ref[rows(a), :] = (o * (r * _sigmoid(r))).astype(o_ref.dtype)
        else:
            o_ref[rows(a), :] = o

    @pl.when(i == pl.num_programs(1) - 1)
    def _():
        sfin_ref[...] = st_scr[...]


def _gla_decay_matrices():
    T, C = GLA_TILE, GLA_CHUNK
    t = jnp.arange(T)[:, None]
    s = jnp.arange(T)[None, :]
    same = (t // C) == (s // C)
    mats = []
    for rev in (False, True):
        seen = (s >= t) if rev else (s <= t)
        earlier = (s // C > t // C) if rev else (s // C < t // C)
        mats.append(jnp.concatenate([same & seen, same, earlier], axis=0))
    return jnp.stack(mats).astype(BF16)


def _gla_pass(l, d, pb3, pg3, wg, bg, s0, ones_bd, dec, oprev, gn, tt):
    B, n, _ = pb3.shape
    nt = n // tt
    rev = d == 1
    finish = oprev is not None
    W = GROUP_W

    def tok(b, i):
        return (b, (nt - 1 - i) if rev else i, 0)

    in_specs = [
        pl.BlockSpec((None, tt, 4 * W), tok),
        pl.BlockSpec((None, tt, 128), tok),
        pl.BlockSpec((None, None, 128, W), lambda b, i: (l, d, 0, 0)),
        pl.BlockSpec((None, None, 1, W), lambda b, i: (l, d, 0, 0)),
        pl.BlockSpec((None, W, W), lambda b, i: (b, 0, 0)),
        pl.BlockSpec((W, W), lambda b, i: (0, 0)),
        pl.BlockSpec((None, 3 * GLA_TILE, GLA_TILE), lambda b, i: (d, 0, 0)),
    ]
    args = [pb3, pg3, wg, bg, s0, ones_bd, dec]
    if finish:
        in_specs += [pl.BlockSpec((None, tt, W), tok), _vec_spec(l, W)]
        args += [oprev, gn]
    return pl.pallas_call(
        functools.partial(_gla_kernel, rev=rev, finish=finish),
        grid=(B, nt),
        in_specs=in_specs,
        out_specs=[pl.BlockSpec((None, tt, W), tok), pl.BlockSpec((None, W, W), lambda b, i: (b, 0, 0))],
        out_shape=[jax.ShapeDtypeStruct((B, n, W), BF16 if finish else F32),
                   jax.ShapeDtypeStruct((B, W, W), F32)],
        scratch_shapes=[pltpu.VMEM((W, W), F32)],
        compiler_params=_cparams("arbitrary", "arbitrary"),
        name="gla_bwd" if rev else "gla_fwd",
    )(*args)


def _gla(l, pb_c, pg_c, pb_x, pg_x, wg, bg, gn, ones_bd, tt_c, tt_x):
    B = pb_x.shape[0]
    zero = jnp.zeros((B, GROUP_W, GROUP_W), F32)
    dec = _gla_decay_matrices()
    ofc, sfc = _gla_pass(l, 0, pb_c, pg_c, wg, bg, zero, ones_bd, dec, None, None, tt_c)
    ofx, _ = _gla_pass(l, 0, pb_x, pg_x, wg, bg, sfc, ones_bd, dec, None, None, tt_x)
    ob_c, sbc = _gla_pass(l, 1, pb_c, pg_c, wg, bg, zero, ones_bd, dec, ofc, gn, tt_c)
    ob_x, _ = _gla_pass(l, 1, pb_x, pg_x, wg, bg, sbc, ones_bd, dec, ofx, gn, tt_x)
    return ob_c, ob_x


def _s5_matrices(a_re, a_im, log_dt, b_re, b_im, c_re, c_im, d_skip):
    T = S5_T
    G, P = a_re.shape[1:]
    I = b_re.shape[-1]
    lam = lax.complex(a_re.astype(F32), a_im.astype(F32))
    ldt = lam * jnp.exp(log_dt.astype(F32))[..., None]
    lam_bar = jnp.exp(ldt)
    b_bar = ((lam_bar - 1.0) / lam)[..., None] * lax.complex(b_re.astype(F32), b_im.astype(F32))
    cmat = lax.complex(c_re.astype(F32), c_im.astype(F32))
    steps = jnp.arange(T + 1, dtype=F32)
    pw = jnp.exp(ldt[..., None] * steps)
    taps = jnp.einsum('dgop,dgpk,dgpi->dkgoi', cmat, pw[..., :T], b_bar).real

    lag = jnp.arange(T)[:, None, None]
    df = jnp.arange(T)[None, None, :] - jnp.arange(T)[None, :, None]
    ef = (df == lag).astype(F32)
    eb = (-df == lag).astype(F32)
    skip = jnp.eye(I, dtype=F32)[None, :, :] * d_skip.astype(F32)[:, None, :]
    m = (jnp.einsum('kgoi,kst->stgoi', taps[0], ef) + jnp.einsum('kgoi,kst->stgoi', taps[1], eb)
         + jnp.eye(T, dtype=F32)[:, :, None, None, None] * skip[None, None])
    m = m.transpose(2, 0, 4, 1, 3).reshape(G, T * I, T * I)

    ar = jnp.arange(T)
    pf = pw[0][..., T - 1 - ar]
    pb = pw[1][..., ar]
    bf = jnp.einsum('gps,gpi->gsip', pf, b_bar[0]).reshape(G, T * I, P)
    bb = jnp.einsum('gps,gpi->gsip', pb, b_bar[1]).reshape(G, T * I, P)
    bmat = jnp.concatenate([bf.real, bf.imag, bf.imag, bf.real, bb.real, bb.imag, bb.imag, bb.real], axis=-1)

    cf = jnp.einsum('gop,gpt->gpto', cmat[0], pw[0][..., 1 + ar]).reshape(G, P, T * I)
    cb = jnp.einsum('gop,gpt->gpto', cmat[1], pw[1][..., T - ar]).reshape(G, P, T * I)
    w = jnp.concatenate([m, cf.real, -cf.imag, cb.real, -cb.imag], axis=1)

    a = pw[..., T]
    acoef = jnp.stack([jnp.concatenate([a[0].real, a[0].real], axis=-1), jnp.concatenate([-a[0].imag, a[0].imag], axis=-1),
                       jnp.concatenate([a[1].real, a[1].real], axis=-1), jnp.concatenate([-a[1].imag, a[1].imag], axis=-1)])
    return bmat.astype(BF16), w.astype(BF16), acoef


def _s5_state_kernel(x_ref, b_ref, o_ref):
    o_ref[...] = _dot(x_ref[...], b_ref[...])


def _s5_state(xg, bmat, tr):
    G, R, K = xg.shape
    N = bmat.shape[-1]
    return pl.pallas_call(
        _s5_state_kernel,
        grid=(G, R // tr),
        in_specs=[pl.BlockSpec((None, tr, K), lambda g, i: (g, i, 0)),
                  pl.BlockSpec((None, K, N), lambda g, i: (g, 0, 0))],
        out_specs=pl.BlockSpec((None, tr, N), lambda g, i: (g, i, 0)),
        out_shape=jax.ShapeDtypeStruct((G, R, N), F32),
        compiler_params=_cparams("parallel", "parallel"),
        name="s5_state",
    )(xg, bmat)


def _s5_scan_kernel(sf_ref, sb_ref, a_ref, h0_ref, hf_ref, hb_ref, hfin_ref, st_scr):
    j = pl.program_id(1)
    tc = sf_ref.shape[1]

    @pl.when(j == 0)
    def _():
        st_scr[...] = h0_ref[...]

    a1f, a2f, a1b, a2b = a_ref[0], a_ref[1], a_ref[2], a_ref[3]
    half = a1f.shape[-1]

    def body(i, hs):
        h1f, h2f, h1b, h2b = hs
        ib = tc - 1 - i
        sf = sf_ref[:, i, :]
        sb = sb_ref[:, ib, :]
        hf_ref[:, i, :] = h1f
        hb_ref[:, ib, :] = h1b
        return (h1f * a1f + h2f * a2f + sf[:, 0:half], h2f * a1f - h1f * a2f + sf[:, half:],
                h1b * a1b + h2b * a2b + sb[:, 0:half], h2b * a1b - h1b * a2b + sb[:, half:])

    hs = lax.fori_loop(0, tc, body, (st_scr[0], st_scr[1], st_scr[2], st_scr[3]))
    for k in range(4):
        st_scr[k] = hs[k]

    @pl.when(j == pl.num_programs(1) - 1)
    def _():
        hfin_ref[...] = st_scr[...]


def _s5_scan(s, acoef, h0, B, tc):
    G, R, n2 = s.shape
    nt = R // B // tc
    half = n2 // 4
    fwd = lambda b, j: (0, b * nt + j, 0)
    bwd = lambda b, j: (0, b * nt + nt - 1 - j, 0)
    st_spec = pl.BlockSpec((None, 4, G, half), lambda b, j: (b, 0, 0, 0))
    return pl.pallas_call(
        _s5_scan_kernel,
        grid=(B, nt),
        in_specs=[pl.BlockSpec((G, tc, 2 * half), fwd),
                  pl.BlockSpec((G, tc, 2 * half), lambda b, j: (0, b * nt + nt - 1 - j, 1)),
                  pl.BlockSpec((4, G, half), lambda b, j: (0, 0, 0)),
                  st_spec],
        out_specs=[pl.BlockSpec((G, tc, half), fwd), pl.BlockSpec((G, tc, half), bwd), st_spec],
        out_shape=[jax.ShapeDtypeStruct((G, R, half), F32), jax.ShapeDtypeStruct((G, R, half), F32),
                   jax.ShapeDtypeStruct((B, 4, G, half), F32)],
        scratch_shapes=[pltpu.VMEM((4, G, half), F32)],
        compiler_params=_cparams("arbitrary", "arbitrary"),
        name="s5_scan",
    )(s, s, acoef, h0)


def _s5_out_kernel(x_ref, hf_ref, hb_ref, w_ref, o_ref):
    k = x_ref.shape[-1]
    kh = hf_ref.shape[-1]
    y = _dot(x_ref[...], w_ref[0:k, :])
    y = y + _dot(hf_ref[...].astype(BF16), w_ref[k:k + kh, :])
    o_ref[...] = y + _dot(hb_ref[...].astype(BF16), w_ref[k + kh:, :])


def _s5_out(xg, hf, hb, w, tr):
    G, R, K = xg.shape
    kh = hf.shape[-1]
    row = lambda width: pl.BlockSpec((None, tr, width), lambda g, i: (g, i, 0))
    return pl.pallas_call(
        _s5_out_kernel,
        grid=(G, R // tr),
        in_specs=[row(K), row(kh), row(kh), pl.BlockSpec((None, K + 2 * kh, K), lambda g, i: (g, 0, 0))],
        out_specs=row(K),
        out_shape=jax.ShapeDtypeStruct((G, R, K), F32),
        compiler_params=_cparams("parallel", "parallel"),
        name="s5_out",
    )(xg, hf, hb, w)


def _s5_finish_kernel(y_ref, w_ref, b_ref, o_ref):
    y = _gelu(y_ref[...])
    gate = _sigmoid(_dot(y.astype(BF16), w_ref[...]) + b_ref[...])
    o_ref[...] = (y * gate).astype(o_ref.dtype)


def _s5_finish(l, y2, w_glu, b_glu, tm):
    R, W = y2.shape
    return pl.pallas_call(
        _s5_finish_kernel,
        grid=(R // tm,),
        in_specs=[pl.BlockSpec((tm, W), lambda i: (i, 0)),
                  pl.BlockSpec((None, W, W), lambda i: (l, 0, 0)),
                  _vec_spec(l, W)],
        out_specs=pl.BlockSpec((tm, W), lambda i: (i, 0)),
        out_shape=jax.ShapeDtypeStruct((R, W), BF16),
        compiler_params=_cparams("parallel"),
        name="s5_finish",
    )(y2, w_glu, b_glu)


def _to_groups(u3):
    B, n, W = u3.shape
    G = W // S5_IN
    x = u3.astype(BF16).reshape(B, n // S5_T, S5_T, G, S5_IN).transpose(3, 0, 1, 2, 4)
    return x.reshape(G, B * (n // S5_T), S5_T * S5_IN)


def _from_groups(y, B):
    G, R, _ = y.shape
    nc = R // B
    y = y.reshape(G, B, nc, S5_T, S5_IN).transpose(1, 2, 3, 0, 4)
    return y.reshape(B * nc * S5_T, G * S5_IN)


def _s5_gather_kernel(x_ref, o_ref, xs_scr, xt_scr):
    G, tc, _ = o_ref.shape
    nh = x_ref.shape[1] // 128
    gh = G // nh
    for h in range(nh):
        xs_scr[h] = x_ref[:, h * 128:(h + 1) * 128]
    for t in range(S5_T):
        for h in range(nh):
            rt = xs_scr[h, pl.ds(t, tc, stride=S5_T), :].T
            for g in range(gh):
                xt_scr[h * gh + g, t * S5_IN:(t + 1) * S5_IN, :] = rt[g * S5_IN:(g + 1) * S5_IN, :]
    for g in range(G):
        o_ref[g] = xt_scr[g].T.astype(o_ref.dtype)


def _s5_gather(x2, tc):
    N, W = x2.shape
    G = W // S5_IN
    R = N // S5_T
    return pl.pallas_call(
        _s5_gather_kernel,
        grid=(R // tc,),
        in_specs=[pl.BlockSpec((tc * S5_T, W), lambda i: (i, 0))],
        out_specs=pl.BlockSpec((G, tc, S5_T * S5_IN), lambda i: (0, i, 0)),
        out_shape=jax.ShapeDtypeStruct((G, R, S5_T * S5_IN), BF16),
        scratch_shapes=[pltpu.VMEM((W // 128, tc * S5_T, 128), F32), pltpu.VMEM((G, S5_T * S5_IN, tc), F32)],
        compiler_params=_cparams("parallel"),
        name="s5_gather",
    )(x2)


def _s5_scatter_kernel(y_ref, w_ref, b_ref, o_ref, yt_scr, tok_scr):
    G, tc, _ = y_ref.shape
    nh = tok_scr.shape[0]
    gh = G // nh
    for g in range(G):
        yt = y_ref[g].T
        for t in range(S5_T):
            yt_scr[t, g // gh, (g % gh) * S5_IN:(g % gh + 1) * S5_IN, :] = yt[t * S5_IN:(t + 1) * S5_IN, :]
    for t in range(S5_T):
        for h in range(nh):
            tok_scr[h, pl.ds(t, tc, stride=S5_T), :] = yt_scr[t, h].T
    y = _gelu(jnp.concatenate([tok_scr[h] for h in range(nh)], axis=1))
    gate = _sigmoid(_dot(y.astype(BF16), w_ref[...]) + b_ref[...])
    o_ref[...] = (y * gate).astype(o_ref.dtype)


def _s5_scatter_finish(l, yg, w_glu, b_glu, tc):
    G, R, K = yg.shape
    W = G * S5_IN
    return pl.pallas_call(
        _s5_scatter_kernel,
        grid=(R // tc,),
        in_specs=[pl.BlockSpec((G, tc, K), lambda i: (0, i, 0)),
                  pl.BlockSpec((None, W, W), lambda i: (l, 0, 0)),
                  _vec_spec(l, W)],
        out_specs=pl.BlockSpec((tc * S5_T, W), lambda i: (i, 0)),
        out_shape=jax.ShapeDtypeStruct((R * S5_T, W), BF16),
        scratch_shapes=[pltpu.VMEM((S5_T, W // 128, 128, tc), F32), pltpu.VMEM((W // 128, tc * S5_T, 128), F32)],
        compiler_params=_cparams("parallel"),
        name="s5_scatter",
    )(yg, w_glu, b_glu)


def _s5(l, pc_c, pc_x, mats, w_glu, b_glu, need_ctx, tm_c):
    bmat, w, acoef = mats
    B, n, W = pc_x.shape
    G = W // S5_IN
    xg_c = _to_groups(pc_c)
    tc_x = min(S5_TILE, n // S5_T)
    xg_x = _s5_gather(pc_x.reshape(B * n, W), tc_x)
    tr_c = min(512, xg_c.shape[1])
    tr_x = min(512, xg_x.shape[1])
    h0 = jnp.zeros((B, 4, G, acoef.shape[-1]), F32)
    hf_c, hb_c, h1 = _s5_scan(_s5_state(xg_c, bmat, tr_c), acoef, h0, B, xg_c.shape[1] // B)
    hf_x, hb_x, _ = _s5_scan(_s5_state(xg_x, bmat, tr_x), acoef, h1, B, tc_x)
    oc_x = _s5_scatter_finish(l, _s5_out(xg_x, hf_x, hb_x, w, tr_x), w_glu, b_glu, tc_x)
    oc_c = None
    if need_ctx:
        y_c = _from_groups(_s5_out(xg_c, hf_c, hb_c, w, tr_c), B)
        oc_c = _s5_finish(l, y_c, w_glu, b_glu, tm_c)
    return oc_c, oc_x


def _mla_prep_kernel(*refs, rope):
    if rope:
        pd_ref, qn_ref, kvn_ref, wq_ref, wqs_ref, wk_ref, wks_ref, wv_ref, cos_ref, sin_ref, q_ref, k_ref, v_ref = refs
    else:
        pd_ref, qn_ref, kvn_ref, wq_ref, wk_ref, wv_ref, q_ref, k_ref, v_ref = refs
    cq = pd_ref[:, 0:256]
    ms = jnp.sum(cq * cq, axis=-1, keepdims=True) * (1.0 / MLA_Q_RANK)
    cqn = (cq * lax.rsqrt(ms + EPS) * qn_ref[...]).astype(BF16)
    ck = pd_ref[:, 256:384]
    lane = lax.broadcasted_iota(jnp.int32, ck.shape, 1)
    is_lat = lane < MLA_KV_RANK
    ms = jnp.sum(jnp.where(is_lat, ck * ck, 0.0), axis=-1, keepdims=True) * (1.0 / MLA_KV_RANK)
    ckn = jnp.where(is_lat, ck * lax.rsqrt(ms + EPS) * kvn_ref[...], ck).astype(BF16)
    q = _dot(cqn, wq_ref[...])
    k = _dot(ckn, wk_ref[...])
    if rope:
        cos = jnp.concatenate([cos_ref[...]] * MLA_HEADS, axis=1)
        sin = jnp.concatenate([sin_ref[...]] * MLA_HEADS, axis=1)
        q = q * cos + _dot(cqn, wqs_ref[...]) * sin
        k = k * cos + _dot(ckn, wks_ref[...]) * sin
    q_ref[...] = (q * ((MLA_NOPE + MLA_ROPE) ** -0.5 * LOG2E)).astype(BF16)
    k_ref[...] = k.astype(BF16)
    v_ref[...] = _dot(ckn, wv_ref[...]).astype(BF16)


def _mla_prep(l, pd, qn, kvn, wts, tables, tm, n):
    R = pd.shape[0]
    HP = MLA_HEADS * MLA_HEAD_PAD
    wq, wqs, wk, wks, wv = wts
    rope = tables is not None
    wspec = lambda r: pl.BlockSpec((None, r, HP), lambda i: (l, 0, 0))
    in_specs = [pl.BlockSpec((tm, 384), lambda i: (i, 0)), _vec_spec(l, 256), _vec_spec(l, 128)]
    if rope:
        npt = n // tm
        tspec = pl.BlockSpec((tm, MLA_HEAD_PAD), lambda i: (i % npt, 0))
        in_specs += [wspec(256), wspec(256), wspec(128), wspec(128), wspec(128), tspec, tspec]
        args = (pd, qn, kvn, wq, wqs, wk, wks, wv) + tuple(tables)
    else:
        in_specs += [wspec(256), wspec(128), wspec(128)]
        args = (pd, qn, kvn, wq, wk, wv)
    return pl.pallas_call(
        functools.partial(_mla_prep_kernel, rope=rope),
        grid=(R // tm,),
        in_specs=in_specs,
        out_specs=[pl.BlockSpec((tm, HP), lambda i: (i, 0))] * 3,
        out_shape=[jax.ShapeDtypeStruct((R, HP), BF16)] * 3,
        compiler_params=_cparams("parallel"),
        name="mla_prep",
    )(*args)


def _attn_kernel(q_ref, k_ref, v_ref, o_ref, m_scr, l_scr, acc_scr):
    kv = pl.program_id(2)
    HP = MLA_HEAD_PAD
    tk = k_ref.shape[0]

    @pl.when(kv == 0)
    def _():
        m_scr[...] = jnp.full(m_scr.shape, -jnp.inf, F32)
        l_scr[...] = jnp.zeros(l_scr.shape, F32)
        acc_scr[...] = jnp.zeros(acc_scr.shape, F32)

    def scores(h):
        lanes = slice(h * HP, (h + 1) * HP)
        return _dot_nt(q_ref[:, lanes], k_ref[:, lanes])

    s_next = scores(0)
    for h in range(MLA_HEADS):
        lanes = slice(h * HP, (h + 1) * HP)
        s = s_next
        if h + 1 < MLA_HEADS:
            s_next = scores(h + 1)
        m_prev = m_scr[h]
        m_new = jnp.maximum(m_prev, jnp.max(s, axis=1, keepdims=True))
        alpha = jnp.exp2(m_prev - m_new)
        p = jnp.exp2(s - m_new[:, 0:1])
        lp = p[:, 0:128]
        for c in range(1, tk // 128):
            lp = lp + p[:, c * 128:(c + 1) * 128]
        l_scr[h] = alpha * l_scr[h] + lp
        acc_scr[h] = alpha * acc_scr[h] + _dot(p.astype(BF16), v_ref[:, lanes])
        m_scr[h] = m_new

    @pl.when(kv == pl.num_programs(2) - 1)
    def _():
        outs = [acc_scr[h][:, 0:MLA_V] / jnp.sum(l_scr[h], axis=1, keepdims=True) for h in range(MLA_HEADS)]
        o_ref[...] = jnp.concatenate(outs, axis=1).astype(o_ref.dtype)


def _attention(q3, k3, v3, tq, tk):
    B, nq, HP = q3.shape
    nk = k3.shape[1]
    return pl.pallas_call(
        _attn_kernel,
        grid=(B, nq // tq, nk // tk),
        in_specs=[pl.BlockSpec((None, tq, HP), lambda b, i, j: (b, i, 0)),
                  pl.BlockSpec((None, tk, HP), lambda b, i, j: (b, j, 0)),
                  pl.BlockSpec((None, tk, HP), lambda b, i, j: (b, j, 0))],
        out_specs=pl.BlockSpec((None, tq, MLA_HEADS * MLA_V), lambda b, i, j: (b, i, 0)),
        out_shape=jax.ShapeDtypeStruct((B, nq, MLA_HEADS * MLA_V), BF16),
        scratch_shapes=[pltpu.VMEM((MLA_HEADS, tq, 128), F32),
                        pltpu.VMEM((MLA_HEADS, tq, 128), F32),
                        pltpu.VMEM((MLA_HEADS, tq, MLA_HEAD_PAD), F32)],
        compiler_params=_cparams("parallel", "parallel", "arbitrary"),
        name="attention",
    )(q3, k3, v3)


def _outproj_kernel(oa, ob, oc, od, w_ref, x_ref, gpost_ref, gt_ref, gpre_ref, sc_ref, sh_ref, xo_ref, h_ref):
    W = GROUP_W
    mix = _dot(oa[...], w_ref[0:W, :])
    mix = mix + _dot(ob[...], w_ref[W:2 * W, :])
    mix = mix + _dot(oc[...], w_ref[2 * W:3 * W, :])
    mix = mix + _dot(od[...], w_ref[3 * W:4 * W, :])
    x = x_ref[...] + gt_ref[...] * (_rms(mix) * gpost_ref[...])
    xo_ref[...] = x
    h_ref[...] = (_rms(x) * gpre_ref[...] * (1.0 + sc_ref[...]) + sh_ref[...]).astype(h_ref.dtype)


def _outproj(l, parts, w_out, x2, mod4, brow, g_post, g_pre_ffn, tm):
    R, D = x2.shape
    W = GROUP_W
    part_spec = pl.BlockSpec((tm, W), lambda i: (i, 0))
    row_spec = pl.BlockSpec((tm, D), lambda i: (i, 0))
    return pl.pallas_call(
        _outproj_kernel,
        grid=(R // tm,),
        in_specs=[part_spec] * 4 + [
            pl.BlockSpec((None, 4 * W, D), lambda i: (l, 0, 0)),
            row_spec,
            _vec_spec(l, D),
            _mod_spec(l, 2, D, brow),
            _vec_spec(l, D),
            _mod_spec(l, 4, D, brow),
            _mod_spec(l, 3, D, brow),
        ],
        out_specs=[row_spec, row_spec],
        out_shape=[jax.ShapeDtypeStruct((R, D), F32), jax.ShapeDtypeStruct((R, D), BF16)],
        compiler_params=_cparams("parallel"),
        name="outproj",
    )(*parts, w_out, x2, g_post, mod4, g_pre_ffn, mod4, mod4)


FFN_HALO = 16
FFN_COLS = 256


def _ffn_kernel(hp_ref, h_ref, hn_ref, wup_ref, cw_ref, cb_ref, wdn_ref, x_ref, gpost_ref, gt_ref, o_ref):
    i = pl.program_id(1)
    tm = h_ref.shape[0]
    dff = wdn_ref.shape[0]
    rows = tm + 2 * FFN_HALO
    prev = jnp.where(i == 0, jnp.zeros_like(hp_ref[...]), hp_ref[...])
    nxt = jnp.where(i == pl.num_programs(1) - 1, jnp.zeros_like(hn_ref[...]), hn_ref[...])
    hb = jnp.concatenate([prev, h_ref[...], nxt], axis=0)

    def conv(z, cols):
        w = cw_ref[:, cols]
        out = cb_ref[:, cols] + w[0:1] * pltpu.roll(z, 1, axis=0) + w[1:2] * z + w[2:3] * pltpu.roll(z, rows - 1, axis=0)
        return out[FFN_HALO:FFN_HALO + tm]

    def cols(j):
        return slice(j * FFN_COLS, (j + 1) * FFN_COLS), slice(dff + j * FFN_COLS, dff + (j + 1) * FFN_COLS)

    def up(j):
        ca, cg = cols(j)
        return _dot(hb, wup_ref[:, ca]), _dot(hb, wup_ref[:, cg])

    nchunks = dff // FFN_COLS
    acc = jnp.zeros(o_ref.shape, F32)
    z_next = up(0)
    for j in range(nchunks):
        ca, cg = cols(j)
        za, zg = z_next
        if j + 1 < nchunks:
            z_next = up(j + 1)
        a = conv(za, ca)
        g = conv(zg, cg)
        acc = acc + _dot(_gelu_gate(a, g).astype(BF16), wdn_ref[ca, :])
    o_ref[...] = x_ref[...] + gt_ref[...] * (_rms(acc) * gpost_ref[...])


def _ffn(l, h3, x3, w_up, conv_w, conv_b, w_down, mod4, brow, g_post, tm):
    B, n, D = x3.shape
    nt = n // tm
    hb = tm // FFN_HALO
    nh = n // FFN_HALO
    dff = w_down.shape[1]
    once = pl.Buffered(1)
    return pl.pallas_call(
        _ffn_kernel,
        grid=(B, nt),
        in_specs=[
            pl.BlockSpec((None, FFN_HALO, D), lambda b, i: (b, jnp.maximum(i * hb - 1, 0), 0)),
            pl.BlockSpec((None, tm, D), lambda b, i: (b, i, 0)),
            pl.BlockSpec((None, FFN_HALO, D), lambda b, i: (b, jnp.minimum((i + 1) * hb, nh - 1), 0)),
            pl.BlockSpec((None, D, 2 * dff), lambda b, i: (l, 0, 0), pipeline_mode=once),
            pl.BlockSpec((None, 3, 2 * dff), lambda b, i: (l, 0, 0)),
            pl.BlockSpec((None, 1, 2 * dff), lambda b, i: (l, 0, 0)),
            pl.BlockSpec((None, dff, D), lambda b, i: (l, 0, 0), pipeline_mode=once),
            pl.BlockSpec((None, tm, D), lambda b, i: (b, i, 0)),
            _vec_spec(l, D),
            _mod_spec(l, 5, D, brow),
        ],
        out_specs=pl.BlockSpec((None, tm, D), lambda b, i: (b, i, 0)),
        out_shape=jax.ShapeDtypeStruct((B, n, D), F32),
        compiler_params=_cparams("parallel", "parallel"),
        name="conv_ffn",
    )(h3, h3, h3, w_up, conv_w, conv_b, w_down, x3, g_post, mod4)


def _prep_w_in(w_in):
    z = lambda n: jnp.zeros(w_in.shape[:-1] + (n,), w_in.dtype)
    a = w_in[..., 0:512]
    b = w_in[..., 512:1536]
    gl = w_in[..., 1536:1568]
    c = w_in[..., 1568:1824]
    cq = w_in[..., 1824:2048]
    ckv_kr = w_in[..., 2048:2176]
    return jnp.concatenate([a, b, c, cq, z(32), ckv_kr, gl, z(96)], axis=-1).astype(BF16)


def _rope_swap(t):
    q = MLA_ROPE // 4
    return jnp.concatenate([t[..., q:2 * q], t[..., 0:q], t[..., 3 * q:4 * q], t[..., 2 * q:3 * q]], axis=-1)


def _prep_mla(w_uq, w_ukv):
    L = w_uq.shape[0]
    H, NP, RP, HP = MLA_HEADS, MLA_NOPE, MLA_ROPE, MLA_HEAD_PAD
    wq = w_uq.reshape(L, MLA_Q_RANK, H, NP + RP)
    zq = jnp.zeros((L, MLA_Q_RANK, H, HP - NP - RP), w_uq.dtype)
    znope = jnp.zeros((L, MLA_Q_RANK, H, NP), w_uq.dtype)
    q_main = jnp.concatenate([wq, zq], axis=-1)
    q_swap = jnp.concatenate([znope, _rope_swap(wq[..., NP:]), zq], axis=-1)
    padq = lambda w: jnp.pad(w.reshape(L, MLA_Q_RANK, H * HP), ((0, 0), (0, 256 - MLA_Q_RANK), (0, 0)))

    wkv = w_ukv.reshape(L, MLA_KV_RANK, H, NP + MLA_V)
    zk = jnp.zeros((L, MLA_KV_RANK, H, HP - NP), w_ukv.dtype)
    k_lat = jnp.concatenate([wkv[..., :NP], zk], axis=-1)
    eye = jnp.eye(RP, dtype=w_ukv.dtype)
    place = lambda e: jnp.broadcast_to(
        jnp.concatenate([jnp.zeros((RP, NP), e.dtype), e, jnp.zeros((RP, HP - NP - RP), e.dtype)], axis=-1)[None, :, None, :],
        (L, RP, H, HP))
    k_main = jnp.concatenate([k_lat, place(eye)], axis=1)
    k_swap = jnp.concatenate([jnp.zeros_like(k_lat), place(_rope_swap(eye))], axis=1)
    v_lat = jnp.concatenate([wkv[..., NP:], jnp.zeros((L, MLA_KV_RANK, H, HP - MLA_V), w_ukv.dtype)], axis=-1)
    v_main = jnp.concatenate([v_lat, jnp.zeros((L, RP, H, HP), w_ukv.dtype)], axis=1)
    flat = lambda w: w.reshape(L, w.shape[1], H * HP).astype(BF16)
    return (padq(q_main).astype(BF16), padq(q_swap).astype(BF16), flat(k_main), flat(k_swap), flat(v_main))


def _rope_tables(n):
    rows = n // GRID_W
    row = jnp.repeat(jnp.arange(rows, dtype=F32), GRID_W)
    col = jnp.tile(jnp.arange(GRID_W, dtype=F32), rows)
    nf = MLA_ROPE // 4
    inv = ROPE_BASE ** (-jnp.arange(nf, dtype=F32) / nf)
    ar = row[:, None] * inv[None, :]
    ac = col[:, None] * inv[None, :]
    one = jnp.ones((n, MLA_NOPE), F32)
    zero = jnp.zeros((n, MLA_HEAD_PAD - MLA_NOPE - MLA_ROPE), F32)
    cos = jnp.concatenate([one, jnp.cos(ar), jnp.cos(ar), jnp.cos(ac), jnp.cos(ac), zero], axis=1)
    sin = jnp.concatenate([0.0 * one, -jnp.sin(ar), jnp.sin(ar), -jnp.sin(ac), jnp.sin(ac), zero], axis=1)
    return cos, sin


def _pick_tile(n, want):
    t = min(n, want)
    while n % t:
        t //= 2
    return t


def kernel(x, c, ctx, c_ctx, w_mod, b_mod, g_pre_mix, g_post_mix, g_pre_ffn, g_post_ffn, w_in,
           sgu_norm, sgu_w, sgu_b, gla_w_gate, gla_b_gate, gla_norm,
           s5_a_re, s5_a_im, s5_log_dt, s5_b_re, s5_b_im, s5_c_re, s5_c_im, s5_d, s5_w_glu, s5_b_glu,
           mla_q_norm, mla_w_uq, mla_kv_norm, mla_w_ukv, w_out,
           ffn_w_up, ffn_conv_w, ffn_conv_b, ffn_w_down):
    B, n, D = x.shape
    nctx = ctx.shape[1]
    L = w_mod.shape[0]
    W = GROUP_W
    assert B < 8 and n % 512 == 0 and nctx % 128 == 0 and n % GRID_W == 0

    c8 = jnp.concatenate([c, c_ctx[None, :], jnp.zeros((8 - B - 1, D), F32)], axis=0)
    mod4 = _modulation(c8, w_mod, b_mod).reshape(L, 8, 1, 6 * D)
    vec = lambda p: p.reshape(L, 1, -1).astype(F32)
    g_pre_mix, g_post_mix, g_pre_ffn, g_post_ffn = map(vec, (g_pre_mix, g_post_mix, g_pre_ffn, g_post_ffn))
    w_in_p = _prep_w_in(w_in)
    sgu_gn = vec(sgu_norm)
    sgu_w_st = sgu_w.reshape(L, -1, MLP_CHUNK).astype(BF16)
    sgu_bias = jnp.repeat(jnp.swapaxes(sgu_b, 1, 2), HEAD_D, axis=2).astype(F32)
    ones_bd = jnp.kron(jnp.eye(W // HEAD_D, dtype=F32), jnp.ones((HEAD_D, HEAD_D), F32)).astype(BF16)
    gla_wg = jnp.zeros((L, 2, 128, W), F32)
    gla_wg = gla_wg.at[:, 0, 0:GATE_RANK].set(gla_w_gate[:, 0]).at[:, 1, GATE_RANK:2 * GATE_RANK].set(gla_w_gate[:, 1])
    gla_bg = gla_b_gate.reshape(L, 2, 1, W).astype(F32)
    gla_gn = vec(gla_norm)
    s5_wglu = s5_w_glu.astype(BF16)
    s5_bglu = vec(s5_b_glu)
    mla_qn = jnp.pad(mla_q_norm, ((0, 0), (0, 256 - MLA_Q_RANK))).reshape(L, 1, 256).astype(F32)
    mla_kvn = jnp.pad(mla_kv_norm, ((0, 0), (0, 128 - MLA_KV_RANK))).reshape(L, 1, 128).astype(F32)
    mla_wts = _prep_mla(mla_w_uq, mla_w_ukv)
    tables = _rope_tables(n)
    w_out_b = w_out.astype(BF16)
    w_up_b = ffn_w_up.astype(BF16)
    w_down_b = ffn_w_down.astype(BF16)
    dff = ffn_w_down.shape[1]
    half_gate = jnp.concatenate([jnp.ones((dff,), F32), jnp.full((dff,), 0.5, F32)])
    conv_w = ffn_conv_w.astype(F32) * half_gate
    conv_b = (ffn_conv_b.astype(F32) * half_gate).reshape(L, 1, -1)

    tm_x = _pick_tile(n, 512)
    tm_c = _pick_tile(nctx, 512)
    tpb_x = n // tm_x
    brow_x = lambda i: i // tpb_x
    brow_c = lambda *g: B
    brow_x2 = lambda b, i: b

    xs = x.reshape(B * n, D)
    cs = ctx.reshape(B * nctx, D)

    for l in range(L):
        need_ctx = l < L - 1
        pa_x, pb_x, pc_x, pd_x, pg_x = _inproj(l, xs, mod4, brow_x, g_pre_mix, w_in_p, tm_x)
        pa_c, pb_c, pc_c, pd_c, pg_c = _inproj(l, cs, mod4, brow_c, g_pre_mix, w_in_p, tm_c)

        oa_x = _sgu(l, pa_x, sgu_gn, sgu_w_st, sgu_bias, ones_bd, tm_x)
        r3 = lambda t, m: t.reshape(B, m, t.shape[-1])
        ob_c, ob_x = _gla(l, r3(pb_c, nctx), r3(pg_c, nctx), r3(pb_x, n), r3(pg_x, n),
                          gla_wg, gla_bg, gla_gn, ones_bd, _pick_tile(nctx, GLA_TILE), _pick_tile(n, 4 * GLA_TILE))
        mats = _s5_matrices(s5_a_re[l], s5_a_im[l], s5_log_dt[l], s5_b_re[l], s5_b_im[l],
                            s5_c_re[l], s5_c_im[l], s5_d[l])
        oc_c, oc_x = _s5(l, r3(pc_c, nctx), r3(pc_x, n), mats, s5_wglu, s5_bglu, need_ctx, tm_c)
        wq, wqs, wk, wks, wv = mla_wts
        q_x, k_x, v_x = _mla_prep(l, pd_x, mla_qn, mla_kvn, mla_wts, tables, tm_x, n)
        q_c, k_c, v_c = _mla_prep(l, pd_c, mla_qn, mla_kvn, (wq, None, wk, None, wv), None, tm_c, nctx)
        k_all = jnp.concatenate([r3(k_c, nctx), r3(k_x, n)], axis=1)
        v_all = jnp.concatenate([r3(v_c, nctx), r3(v_x, n)], axis=1)
        nk = n + nctx
        tk = next(t for t in (2816, 1408, 768, 512, 256, 128) if nk % t == 0)
        od_x = _attention(r3(q_x, n), k_all, v_all, tm_x, tk).reshape(B * n, W)

        xs, hx = _outproj(l, (oa_x, ob_x.reshape(B * n, W), oc_x, od_x), w_out_b, xs, mod4, brow_x,
                          g_post_mix, g_pre_ffn, tm_x)
        xs = _ffn(l, hx.reshape(B, n, D), xs.reshape(B, n, D), w_up_b, conv_w, conv_b, w_down_b, mod4, brow_x2,
                  g_post_ffn, tm_x).reshape(B * n, D)

        if need_ctx:
            oa_c = _sgu(l, pa_c, sgu_gn, sgu_w_st, sgu_bias, ones_bd, tm_c)
            od_c = _attention(r3(q_c, nctx), r3(k_c, nctx), r3(v_c, nctx), tm_c, nctx).reshape(B * nctx, W)
            cs, hc = _outproj(l, (oa_c, ob_c.reshape(B * nctx, W), oc_c, od_c), w_out_b, cs, mod4, brow_c,
                              g_post_mix, g_pre_ffn, tm_c)
            cs = _ffn(l, hc.reshape(B, nctx, D), cs.reshape(B, nctx, D), w_up_b, conv_w, conv_b, w_down_b, mod4,
                      brow_c, g_post_ffn, tm_c).reshape(B * nctx, D)
    return xs.reshape(B, n, D)
```

```python
import functools

import jax
import jax.numpy as jnp
from jax import lax
from jax.experimental import pallas as pl
from jax.experimental.pallas import tpu as pltpu

F32 = jnp.float32
BF16 = jnp.bfloat16

EPS = 1e-6
GRID_W = 64
GROUP_W = 256
HEAD_D = 64
MLP_CHUNK = 128
GATE_RANK = 16
GATE_TEMP = 16.0
GLA_CHUNK = 64
GLA_TILE = 256
S5_IN = 16
S5_T = 16
S5_TILE = 128
MLA_HEADS = 4
MLA_NOPE = 64
MLA_ROPE = 32
MLA_V = 64
MLA_Q_RANK = 224
MLA_KV_RANK = 96
MLA_HEAD_PAD = 128
ROPE_BASE = 10000.0
LOG2E = 1.4426950408889634

VMEM_LIMIT = 48 * 1024 * 1024


def _cparams(*sem):
    return pltpu.CompilerParams(dimension_semantics=sem, vmem_limit_bytes=VMEM_LIMIT)


def _dot(a, b):
    return jnp.dot(a, b, preferred_element_type=F32)


def _dot_nt(a, b):
    return lax.dot_general(a, b, (((1,), (1,)), ((), ())), preferred_element_type=F32)


def _dot_tn(a, b):
    return lax.dot_general(a, b, (((0,), (0,)), ((), ())), preferred_element_type=F32)


def _split(a):
    hi = a.astype(BF16)
    lo = (a - hi.astype(F32)).astype(BF16)
    return hi, lo


def _dot_x2(a, b_bf16):
    hi, lo = _split(a)
    return _dot(hi, b_bf16) + _dot(lo, b_bf16)


def _dot_x3(a, b):
    ah, al = _split(a)
    bh, bl = _split(b)
    return _dot(ah, bh) + _dot(al, bh) + _dot(ah, bl)


def _rms(x):
    return x * lax.rsqrt(jnp.mean(x * x, axis=-1, keepdims=True) + EPS)


def _gelu(x):
    return 0.5 * x * (1.0 + jnp.tanh(0.7978845608028654 * (x + 0.044715 * (x * x * x))))


def _gelu_gate(a, half_g):
    u = a * (0.7978845608028654 + 0.035677408136300125 * (a * a))
    return (a * half_g) * (1.0 + jnp.tanh(u))


def _sigmoid(x):
    return 1.0 / (1.0 + jnp.exp(-x))


def _lane_group(shape, width):
    return lax.broadcasted_iota(jnp.int32, shape, len(shape) - 1) // width


def _mod_kernel(c_ref, w_ref, b_ref, o_ref):
    c = c_ref[...]
    s = c * _sigmoid(c)
    o_ref[...] = _dot_x3(s, w_ref[...]) + b_ref[...]


def _modulation(c8, w_mod, b_mod):
    L, D, W = w_mod.shape
    tn = 1536
    return pl.pallas_call(
        _mod_kernel,
        grid=(L, W // tn),
        in_specs=[
            pl.BlockSpec((8, D), lambda l, j: (0, 0)),
            pl.BlockSpec((None, D, tn), lambda l, j: (l, 0, j)),
            pl.BlockSpec((None, 1, tn), lambda l, j: (l, 0, j)),
        ],
        out_specs=pl.BlockSpec((None, 8, tn), lambda l, j: (l, 0, j)),
        out_shape=jax.ShapeDtypeStruct((L, 8, W), F32),
        compiler_params=_cparams("arbitrary", "arbitrary"),
        name="modulation",
    )(c8, w_mod, b_mod.reshape(L, 1, W))


def _mod_spec(l, j, D, bfn):
    return pl.BlockSpec((None, None, 1, D), lambda *g: (l, bfn(*g), 0, j))


def _vec_spec(l, width):
    return pl.BlockSpec((None, 1, width), lambda *g: (l, 0, 0))


IN_SLABS = (("a", 0, 512), ("b", 512, 1024), ("c", 1536, 256), ("d", 1792, 384), ("g", 2176, 128))
IN_PAD_COLS = 2304


def _inproj_kernel(x_ref, g_ref, sc_ref, sh_ref, w_ref, oa, ob, oc, od, og):
    h = _rms(x_ref[...]) * g_ref[...] * (1.0 + sc_ref[...]) + sh_ref[...]
    hb = h.astype(BF16)
    for (_, off, width), o_ref in zip(IN_SLABS, (oa, ob, oc, od, og)):
        o_ref[...] = _dot(hb, w_ref[:, off:off + width])


def _inproj(l, x2, mod4, brow, g_pre, w_in_p, tm):
    R, D = x2.shape
    grid = (R // tm,)
    outs = [jax.ShapeDtypeStruct((R, width), F32) for (_, _, width) in IN_SLABS]
    return pl.pallas_call(
        _inproj_kernel,
        grid=grid,
        in_specs=[
            pl.BlockSpec((tm, D), lambda i: (i, 0)),
            _vec_spec(l, D),
            _mod_spec(l, 1, D, brow),
            _mod_spec(l, 0, D, brow),
            pl.BlockSpec((None, D, IN_PAD_COLS), lambda i: (l, 0, 0)),
        ],
        out_specs=[pl.BlockSpec((tm, width), lambda i: (i, 0)) for (_, _, width) in IN_SLABS],
        out_shape=outs,
        compiler_params=_cparams("parallel"),
        name="inproj",
    )(x2, g_pre, mod4, mod4, w_in_p)


def _sgu_kernel(p_ref, gn_ref, w_ref, b_ref, ones_ref, o_ref):
    tm = p_ref.shape[0]
    g = _gelu(p_ref[...])
    u = g[:, :GROUP_W]
    v = g[:, GROUP_W:]
    ms = _dot_x2(v * v, ones_ref[...]) * (1.0 / HEAD_D)
    vb = (v * lax.rsqrt(ms + EPS) * gn_ref[...]).astype(BF16)
    head = _lane_group((MLP_CHUNK, GROUP_W), HEAD_D)
    w = w_ref[...]
    for c in range(tm // MLP_CHUNK):
        rows = slice(c * MLP_CHUNK, (c + 1) * MLP_CHUNK)
        r = _dot(w, vb[rows])
        s = b_ref[...]
        for h in range(GROUP_W // HEAD_D):
            s = s + jnp.where(head == h, r[h * MLP_CHUNK:(h + 1) * MLP_CHUNK], 0.0)
        o_ref[rows, :] = (u[rows] * s).astype(o_ref.dtype)


def _sgu(l, pa, gn, w_st, bias, ones_bd, tm):
    R = pa.shape[0]
    H = GROUP_W // HEAD_D
    return pl.pallas_call(
        _sgu_kernel,
        grid=(R // tm,),
        in_specs=[
            pl.BlockSpec((tm, 2 * GROUP_W), lambda i: (i, 0)),
            _vec_spec(l, GROUP_W),
            pl.BlockSpec((None, H * MLP_CHUNK, MLP_CHUNK), lambda i: (l, 0, 0)),
            pl.BlockSpec((None, MLP_CHUNK, GROUP_W), lambda i: (l, 0, 0)),
            pl.BlockSpec((GROUP_W, GROUP_W), lambda i: (0, 0)),
        ],
        out_specs=pl.BlockSpec((tm, GROUP_W), lambda i: (i, 0)),
        out_shape=jax.ShapeDtypeStruct((R, GROUP_W), BF16),
        compiler_params=_cparams("parallel"),
        name="sgu",
    )(pa, gn, w_st, bias, ones_bd)


def _gla_kernel(*refs, rev, finish):
    if finish:
        (pb_ref, pg_ref, wg_ref, bg_ref, s0_ref, ones_ref, dec_ref, oprev_ref, gn_ref, o_ref, sfin_ref, st_scr) = refs
    else:
        (pb_ref, pg_ref, wg_ref, bg_ref, s0_ref, ones_ref, dec_ref, o_ref, sfin_ref, st_scr) = refs
    i = pl.program_id(1)
    C, W, T = GLA_CHUNK, GROUP_W, GLA_TILE
    H = W // HEAD_D
    nsub = T // C
    ntile = pb_ref.shape[0] // T

    @pl.when(i == 0)
    def _():
        st_scr[...] = s0_ref[...]

    dec = dec_ref[...]
    mask4 = jnp.concatenate([dec[0:T].astype(F32)] * H, axis=0)
    head_t = _lane_group((T, W), HEAD_D)
    bd = (lax.broadcasted_iota(jnp.int32, (W, W), 0) // HEAD_D) == _lane_group((W, W), HEAD_D)
    tiles = list(range(ntile - 1, -1, -1)) if rev else list(range(ntile))
    subs = list(range(nsub - 1, -1, -1)) if rev else list(range(nsub))
    rows = lambda a: slice(a * T, (a + 1) * T)
    sub = lambda j: slice(j * C, (j + 1) * C)

    logg = {}
    for a in tiles:
        z = _dot_x3(pg_ref[rows(a), :], wg_ref[...]) + bg_ref[...]
        logg[a] = (jnp.minimum(z, 0.0) - jnp.log(1.0 + jnp.exp(-jnp.abs(z)))) * (1.0 / GATE_TEMP)
    sums = {}
    for a in tiles:
        g_hi, g_lo = _split(logg[a])
        sums[a] = _dot(dec, g_hi) + _dot(dec, g_lo)
    qin, qoff, kin, kend, vb, tot, ptile = {}, {}, {}, {}, {}, {}, {}
    for a in tiles:
        cum, tot[a], off = sums[a][0:T], sums[a][T:2 * T], sums[a][2 * T:3 * T]
        q = pb_ref[rows(a), 0:W] * (HEAD_D ** -0.5)
        k = pb_ref[rows(a), W:2 * W]
        e = jnp.exp(cum)
        qin[a] = (q * e).astype(BF16)
        qoff[a] = (q * (e * jnp.exp(off))).astype(BF16)
        kin[a] = (k * jnp.exp(-cum)).astype(BF16)
        kend[a] = (k * jnp.exp(tot[a] - cum)).astype(BF16)
        vb[a] = pb_ref[rows(a), 2 * W:3 * W].astype(BF16)
        last = sub(subs[-1])
        ptile[a] = jnp.exp(off[last][0:1] + tot[a][last][0:1])
    sc = {}
    for a in tiles:
        qst = jnp.concatenate([jnp.where(head_t == h, qin[a], jnp.zeros_like(qin[a])) for h in range(H)], axis=0)
        sc[a] = (_dot_nt(qst, kin[a]) * mask4).astype(BF16)
    ost = {a: _dot(sc[a], vb[a]) for a in tiles}
    kvt = {(a, j): jnp.where(bd, _dot_tn(vb[a][sub(j)], kend[a][sub(j)]), 0.0) for a in tiles for j in subs}
    cross, rend = {}, {}
    for a in tiles:
        r = None
        for j in subs:
            if r is not None:
                cross[(a, j)] = _dot_nt(qin[a][sub(j)], r.astype(BF16))
                r = r * jnp.exp(tot[a][sub(j)][0:1]) + kvt[(a, j)]
            else:
                cross[(a, j)] = jnp.zeros((C, W), F32)
                r = kvt[(a, j)]
        rend[a] = r
    st = st_scr[...]
    inter = {}
    for a in tiles:
        inter[a] = _dot_nt(qoff[a], st.astype(BF16))
        st = st * ptile[a] + rend[a]
    st_scr[...] = st

    for a in tiles:
        o = inter[a] + jnp.concatenate([cross[(a, j)] for j in range(nsub)], axis=0)
        for h in range(H):
            o = o + jnp.where(head_t == h, ost[a][h * T:(h + 1) * T], 0.0)
        if finish:
            o = o + oprev_ref[rows(a), :]
            ms = _dot_x2(o * o, ones_ref[...]) * (1.0 / HEAD_D)
            o = o * lax.rsqrt(ms + EPS) * gn_ref[...]
            r = pb_ref[rows(a), 3 * W:4 * W]
            o_ref[rows(a), :] = (o * (r * _sigmoid(r))).astype(o_ref.dtype)
        else:
            o_ref[rows(a), :] = o

    @pl.when(i == pl.num_programs(1) - 1)
    def _():
        sfin_ref[...] = st_scr[...]


def _gla_decay_matrices():
    T, C = GLA_TILE, GLA_CHUNK
    t = jnp.arange(T)[:, None]
    s = jnp.arange(T)[None, :]
    same = (t // C) == (s // C)
    mats = []
    for rev in (False, True):
        seen = (s >= t) if rev else (s <= t)
        earlier = (s // C > t // C) if rev else (s // C < t // C)
        mats.append(jnp.concatenate([same & seen, same, earlier], axis=0))
    return jnp.stack(mats).astype(BF16)


def _gla_pass(l, d, pb3, pg3, wg, bg, s0, ones_bd, dec, oprev, gn, tt):
    B, n, _ = pb3.shape
    nt = n // tt
    rev = d == 1
    finish = oprev is not None
    W = GROUP_W

    def tok(b, i):
        return (b, (nt - 1 - i) if rev else i, 0)

    in_specs = [
        pl.BlockSpec((None, tt, 4 * W), tok),
        pl.BlockSpec((None, tt, 128), tok),
        pl.BlockSpec((None, None, 128, W), lambda b, i: (l, d, 0, 0)),
        pl.BlockSpec((None, None, 1, W), lambda b, i: (l, d, 0, 0)),
        pl.BlockSpec((None, W, W), lambda b, i: (b, 0, 0)),
        pl.BlockSpec((W, W), lambda b, i: (0, 0)),
        pl.BlockSpec((None, 3 * GLA_TILE, GLA_TILE), lambda b, i: (d, 0, 0)),
    ]
    args = [pb3, pg3, wg, bg, s0, ones_bd, dec]
    if finish:
        in_specs += [pl.BlockSpec((None, tt, W), tok), _vec_spec(l, W)]
        args += [oprev, gn]
    return pl.pallas_call(
        functools.partial(_gla_kernel, rev=rev, finish=finish),
        grid=(B, nt),
        in_specs=in_specs,
        out_specs=[pl.BlockSpec((None, tt, W), tok), pl.BlockSpec((None, W, W), lambda b, i: (b, 0, 0))],
        out_shape=[jax.ShapeDtypeStruct((B, n, W), BF16 if finish else F32),
                   jax.ShapeDtypeStruct((B, W, W), F32)],
        scratch_shapes=[pltpu.VMEM((W, W), F32)],
        compiler_params=_cparams("arbitrary", "arbitrary"),
        name="gla_bwd" if rev else "gla_fwd",
    )(*args)


def _gla(l, pb_c, pg_c, pb_x, pg_x, wg, bg, gn, ones_bd, tt_c, tt_x):
    B = pb_x.shape[0]
    zero = jnp.zeros((B, GROUP_W, GROUP_W), F32)
    dec = _gla_decay_matrices()
    ofc, sfc = _gla_pass(l, 0, pb_c, pg_c, wg, bg, zero, ones_bd, dec, None, None, tt_c)
    ofx, _ = _gla_pass(l, 0, pb_x, pg_x, wg, bg, sfc, ones_bd, dec, None, None, tt_x)
    ob_c, sbc = _gla_pass(l, 1, pb_c, pg_c, wg, bg, zero, ones_bd, dec, ofc, gn, tt_c)
    ob_x, _ = _gla_pass(l, 1, pb_x, pg_x, wg, bg, sbc, ones_bd, dec, ofx, gn, tt_x)
    return ob_c, ob_x


def _s5_matrices(a_re, a_im, log_dt, b_re, b_im, c_re, c_im, d_skip):
    T = S5_T
    G, P = a_re.shape[1:]
    I = b_re.shape[-1]
    lam = lax.complex(a_re.astype(F32), a_im.astype(F32))
    ldt = lam * jnp.exp(log_dt.astype(F32))[..., None]
    lam_bar = jnp.exp(ldt)
    b_bar = ((lam_bar - 1.0) / lam)[..., None] * lax.complex(b_re.astype(F32), b_im.astype(F32))
    cmat = lax.complex(c_re.astype(F32), c_im.astype(F32))
    steps = jnp.arange(T + 1, dtype=F32)
    pw = jnp.exp(ldt[..., None] * steps)
    taps = jnp.einsum('dgop,dgpk,dgpi->dkgoi', cmat, pw[..., :T], b_bar).real

    lag = jnp.arange(T)[:, None, None]
    df = jnp.arange(T)[None, None, :] - jnp.arange(T)[None, :, None]
    ef = (df == lag).astype(F32)
    eb = (-df == lag).astype(F32)
    skip = jnp.eye(I, dtype=F32)[None, :, :] * d_skip.astype(F32)[:, None, :]
    m = (jnp.einsum('kgoi,kst->stgoi', taps[0], ef) + jnp.einsum('kgoi,kst->stgoi', taps[1], eb)
         + jnp.eye(T, dtype=F32)[:, :, None, None, None] * skip[None, None])
    m = m.transpose(2, 0, 4, 1, 3).reshape(G, T * I, T * I)

    ar = jnp.arange(T)
    pf = pw[0][..., T - 1 - ar]
    pb = pw[1][..., ar]
    bf = jnp.einsum('gps,gpi->gsip', pf, b_bar[0]).reshape(G, T * I, P)
    bb = jnp.einsum('gps,gpi->gsip', pb, b_bar[1]).reshape(G, T * I, P)
    bmat = jnp.concatenate([bf.real, bf.imag, bf.imag, bf.real, bb.real, bb.imag, bb.imag, bb.real], axis=-1)

    cf = jnp.einsum('gop,gpt->gpto', cmat[0], pw[0][..., 1 + ar]).reshape(G, P, T * I)
    cb = jnp.einsum('gop,gpt->gpto', cmat[1], pw[1][..., T - ar]).reshape(G, P, T * I)
    w = jnp.concatenate([m, cf.real, -cf.imag, cb.real, -cb.imag], axis=1)

    a = pw[..., T]
    acoef = jnp.stack([jnp.concatenate([a[0].real, a[0].real], axis=-1), jnp.concatenate([-a[0].imag, a[0].imag], axis=-1),
                       jnp.concatenate([a[1].real, a[1].real], axis=-1), jnp.concatenate([-a[1].imag, a[1].imag], axis=-1)])
    return bmat.astype(BF16), w.astype(BF16), acoef


def _s5_state_kernel(x_ref, b_ref, o_ref):
    o_ref[...] = _dot(x_ref[...], b_ref[...])


def _s5_state(l, xg, bmat, tr):
    G, R, K = xg.shape
    N = bmat.shape[-1]
    return pl.pallas_call(
        _s5_state_kernel,
        grid=(G, R // tr),
        in_specs=[pl.BlockSpec((None, tr, K), lambda g, i: (g, i, 0)),
                  pl.BlockSpec((None, None, K, N), lambda g, i: (l, g, 0, 0))],
        out_specs=pl.BlockSpec((None, tr, N), lambda g, i: (g, i, 0)),
        out_shape=jax.ShapeDtypeStruct((G, R, N), F32),
        compiler_params=_cparams("parallel", "parallel"),
        name="s5_state",
    )(xg, bmat)


def _s5_scan_kernel(sf_ref, sb_ref, a_ref, h0_ref, hf_ref, hb_ref, hfin_ref, st_scr):
    j = pl.program_id(1)
    tc = sf_ref.shape[1]

    @pl.when(j == 0)
    def _():
        st_scr[...] = h0_ref[...]

    a1f, a2f, a1b, a2b = a_ref[0], a_ref[1], a_ref[2], a_ref[3]
    half = a1f.shape[-1]

    def body(i, hs):
        h1f, h2f, h1b, h2b = hs
        ib = tc - 1 - i
        sf = sf_ref[:, i, :]
        sb = sb_ref[:, ib, :]
        hf_ref[:, i, :] = h1f
        hb_ref[:, ib, :] = h1b
        return (h1f * a1f + h2f * a2f + sf[:, 0:half], h2f * a1f - h1f * a2f + sf[:, half:],
                h1b * a1b + h2b * a2b + sb[:, 0:half], h2b * a1b - h1b * a2b + sb[:, half:])

    hs = lax.fori_loop(0, tc, body, (st_scr[0], st_scr[1], st_scr[2], st_scr[3]))
    for k in range(4):
        st_scr[k] = hs[k]

    @pl.when(j == pl.num_programs(1) - 1)
    def _():
        hfin_ref[...] = st_scr[...]


def _s5_scan(l, s, acoef, h0, B, tc):
    G, R, n2 = s.shape
    nt = R // B // tc
    half = n2 // 4
    fwd = lambda b, j: (0, b * nt + j, 0)
    bwd = lambda b, j: (0, b * nt + nt - 1 - j, 0)
    st_spec = pl.BlockSpec((None, 4, G, half), lambda b, j: (b, 0, 0, 0))
    return pl.pallas_call(
        _s5_scan_kernel,
        grid=(B, nt),
        in_specs=[pl.BlockSpec((G, tc, 2 * half), fwd),
                  pl.BlockSpec((G, tc, 2 * half), lambda b, j: (0, b * nt + nt - 1 - j, 1)),
                  pl.BlockSpec((None, 4, G, half), lambda b, j: (l, 0, 0, 0)),
                  st_spec],
        out_specs=[pl.BlockSpec((G, tc, half), fwd), pl.BlockSpec((G, tc, half), bwd), st_spec],
        out_shape=[jax.ShapeDtypeStruct((G, R, half), F32), jax.ShapeDtypeStruct((G, R, half), F32),
                   jax.ShapeDtypeStruct((B, 4, G, half), F32)],
        scratch_shapes=[pltpu.VMEM((4, G, half), F32)],
        compiler_params=_cparams("arbitrary", "arbitrary"),
        name="s5_scan",
    )(s, s, acoef, h0)


def _s5_out_kernel(x_ref, hf_ref, hb_ref, w_ref, o_ref):
    k = x_ref.shape[-1]
    kh = hf_ref.shape[-1]
    y = _dot(x_ref[...], w_ref[0:k, :])
    y = y + _dot(hf_ref[...].astype(BF16), w_ref[k:k + kh, :])
    o_ref[...] = y + _dot(hb_ref[...].astype(BF16), w_ref[k + kh:, :])


def _s5_out(l, xg, hf, hb, w, tr):
    G, R, K = xg.shape
    kh = hf.shape[-1]
    row = lambda width: pl.BlockSpec((None, tr, width), lambda g, i: (g, i, 0))
    return pl.pallas_call(
        _s5_out_kernel,
        grid=(G, R // tr),
        in_specs=[row(K), row(kh), row(kh), pl.BlockSpec((None, None, K + 2 * kh, K), lambda g, i: (l, g, 0, 0))],
        out_specs=row(K),
        out_shape=jax.ShapeDtypeStruct((G, R, K), F32),
        compiler_params=_cparams("parallel", "parallel"),
        name="s5_out",
    )(xg, hf, hb, w)


def _s5_finish_kernel(y_ref, w_ref, b_ref, o_ref):
    y = _gelu(y_ref[...])
    gate = _sigmoid(_dot(y.astype(BF16), w_ref[...]) + b_ref[...])
    o_ref[...] = (y * gate).astype(o_ref.dtype)


def _s5_finish(l, y2, w_glu, b_glu, tm):
    R, W = y2.shape
    return pl.pallas_call(
        _s5_finish_kernel,
        grid=(R // tm,),
        in_specs=[pl.BlockSpec((tm, W), lambda i: (i, 0)),
                  pl.BlockSpec((None, W, W), lambda i: (l, 0, 0)),
                  _vec_spec(l, W)],
        out_specs=pl.BlockSpec((tm, W), lambda i: (i, 0)),
        out_shape=jax.ShapeDtypeStruct((R, W), BF16),
        compiler_params=_cparams("parallel"),
        name="s5_finish",
    )(y2, w_glu, b_glu)


def _to_groups(u3):
    B, n, W = u3.shape
    G = W // S5_IN
    x = u3.astype(BF16).reshape(B, n // S5_T, S5_T, G, S5_IN).transpose(3, 0, 1, 2, 4)
    return x.reshape(G, B * (n // S5_T), S5_T * S5_IN)


def _from_groups(y, B):
    G, R, _ = y.shape
    nc = R // B
    y = y.reshape(G, B, nc, S5_T, S5_IN).transpose(1, 2, 3, 0, 4)
    return y.reshape(B * nc * S5_T, G * S5_IN)


def _s5_gather_kernel(x_ref, o_ref, xs_scr, xt_scr):
    G, tc, _ = o_ref.shape
    nh = x_ref.shape[1] // 128
    gh = G // nh
    for h in range(nh):
        xs_scr[h] = x_ref[:, h * 128:(h + 1) * 128]
    for t in range(S5_T):
        for h in range(nh):
            rt = xs_scr[h, pl.ds(t, tc, stride=S5_T), :].T
            for g in range(gh):
                xt_scr[h * gh + g, t * S5_IN:(t + 1) * S5_IN, :] = rt[g * S5_IN:(g + 1) * S5_IN, :]
    for g in range(G):
        o_ref[g] = xt_scr[g].T.astype(o_ref.dtype)


def _s5_gather(x2, tc):
    N, W = x2.shape
    G = W // S5_IN
    R = N // S5_T
    return pl.pallas_call(
        _s5_gather_kernel,
        grid=(R // tc,),
        in_specs=[pl.BlockSpec((tc * S5_T, W), lambda i: (i, 0))],
        out_specs=pl.BlockSpec((G, tc, S5_T * S5_IN), lambda i: (0, i, 0)),
        out_shape=jax.ShapeDtypeStruct((G, R, S5_T * S5_IN), BF16),
        scratch_shapes=[pltpu.VMEM((W // 128, tc * S5_T, 128), F32), pltpu.VMEM((G, S5_T * S5_IN, tc), F32)],
        compiler_params=_cparams("parallel"),
        name="s5_gather",
    )(x2)


def _s5_scatter_kernel(y_ref, w_ref, b_ref, o_ref, yt_scr, tok_scr):
    G, tc, _ = y_ref.shape
    nh = tok_scr.shape[0]
    gh = G // nh
    for g in range(G):
        yt = y_ref[g].T
        for t in range(S5_T):
            yt_scr[t, g // gh, (g % gh) * S5_IN:(g % gh + 1) * S5_IN, :] = yt[t * S5_IN:(t + 1) * S5_IN, :]
    for t in range(S5_T):
        for h in range(nh):
            tok_scr[h, pl.ds(t, tc, stride=S5_T), :] = yt_scr[t, h].T
    y = _gelu(jnp.concatenate([tok_scr[h] for h in range(nh)], axis=1))
    gate = _sigmoid(_dot(y.astype(BF16), w_ref[...]) + b_ref[...])
    o_ref[...] = (y * gate).astype(o_ref.dtype)


def _s5_scatter_finish(l, yg, w_glu, b_glu, tc):
    G, R, K = yg.shape
    W = G * S5_IN
    return pl.pallas_call(
        _s5_scatter_kernel,
        grid=(R // tc,),
        in_specs=[pl.BlockSpec((G, tc, K), lambda i: (0, i, 0)),
                  pl.BlockSpec((None, W, W), lambda i: (l, 0, 0)),
                  _vec_spec(l, W)],
        out_specs=pl.BlockSpec((tc * S5_T, W), lambda i: (i, 0)),
        out_shape=jax.ShapeDtypeStruct((R * S5_T, W), BF16),
        scratch_shapes=[pltpu.VMEM((S5_T, W // 128, 128, tc), F32), pltpu.VMEM((W // 128, tc * S5_T, 128), F32)],
        compiler_params=_cparams("parallel"),
        name="s5_scatter",
    )(yg, w_glu, b_glu)


def _s5(l, pc_c, pc_x, mats, w_glu, b_glu, need_ctx, tm_c):
    bmat, w, acoef = mats
    B, n, W = pc_x.shape
    G = W // S5_IN
    xg_c = _to_groups(pc_c)
    tc_x = min(S5_TILE, n // S5_T)
    xg_x = _s5_gather(pc_x.reshape(B * n, W), tc_x)
    tr_c = min(512, xg_c.shape[1])
    tr_x = min(512, xg_x.shape[1])
    h0 = jnp.zeros((B, 4, G, acoef.shape[-1]), F32)
    hf_c, hb_c, h1 = _s5_scan(l, _s5_state(l, xg_c, bmat, tr_c), acoef, h0, B, xg_c.shape[1] // B)
    hf_x, hb_x, _ = _s5_scan(l, _s5_state(l, xg_x, bmat, tr_x), acoef, h1, B, tc_x)
    oc_x = _s5_scatter_finish(l, _s5_out(l, xg_x, hf_x, hb_x, w, tr_x), w_glu, b_glu, tc_x)
    oc_c = None
    if need_ctx:
        y_c = _from_groups(_s5_out(l, xg_c, hf_c, hb_c, w, tr_c), B)
        oc_c = _s5_finish(l, y_c, w_glu, b_glu, tm_c)
    return oc_c, oc_x


def _mla_prep_kernel(*refs, rope, aliased):
    if aliased:
        refs = refs[:-5] + refs[-3:]
    if rope:
        pd_ref, qn_ref, kvn_ref, wq_ref, wqs_ref, wk_ref, wks_ref, wv_ref, cos_ref, sin_ref, q_ref, k_ref, v_ref = refs
    else:
        pd_ref, qn_ref, kvn_ref, wq_ref, wk_ref, wv_ref, q_ref, k_ref, v_ref = refs
    cq = pd_ref[:, 0:256]
    ms = jnp.sum(cq * cq, axis=-1, keepdims=True) * (1.0 / MLA_Q_RANK)
    cqn = (cq * lax.rsqrt(ms + EPS) * qn_ref[...]).astype(BF16)
    ck = pd_ref[:, 256:384]
    lane = lax.broadcasted_iota(jnp.int32, ck.shape, 1)
    is_lat = lane < MLA_KV_RANK
    ms = jnp.sum(jnp.where(is_lat, ck * ck, 0.0), axis=-1, keepdims=True) * (1.0 / MLA_KV_RANK)
    ckn = jnp.where(is_lat, ck * lax.rsqrt(ms + EPS) * kvn_ref[...], ck).astype(BF16)
    q = _dot(cqn, wq_ref[...])
    k = _dot(ckn, wk_ref[...])
    if rope:
        cos = jnp.concatenate([cos_ref[...]] * MLA_HEADS, axis=1)
        sin = jnp.concatenate([sin_ref[...]] * MLA_HEADS, axis=1)
        q = q * cos + _dot(cqn, wqs_ref[...]) * sin
        k = k * cos + _dot(ckn, wks_ref[...]) * sin
    q_ref[...] = (q * ((MLA_NOPE + MLA_ROPE) ** -0.5 * LOG2E)).astype(BF16)
    k_ref[...] = k.astype(BF16)
    v_ref[...] = _dot(ckn, wv_ref[...]).astype(BF16)


def _mla_prep(l, pd, qn, kvn, wts, tables, tm, n, B, row0, nk, kv_into=None):
    R = pd.shape[0]
    HP = MLA_HEADS * MLA_HEAD_PAD
    wq, wqs, wk, wks, wv = wts
    rope = tables is not None
    npt = n // tm
    wspec = lambda r: pl.BlockSpec((None, r, HP), lambda i: (l, 0, 0))
    in_specs = [pl.BlockSpec((tm, 384), lambda i: (i, 0)), _vec_spec(l, 256), _vec_spec(l, 128)]
    if rope:
        tspec = pl.BlockSpec((tm, MLA_HEAD_PAD), lambda i: (i % npt, 0))
        in_specs += [wspec(256), wspec(256), wspec(128), wspec(128), wspec(128), tspec, tspec]
        args = (pd, qn, kvn, wq, wqs, wk, wks, wv) + tuple(tables)
    else:
        in_specs += [wspec(256), wspec(128), wspec(128)]
        args = (pd, qn, kvn, wq, wk, wv)
    aliases = {}
    if kv_into is not None:
        aliases = {len(args): 1, len(args) + 1: 2}
        in_specs += [pl.BlockSpec(memory_space=pl.ANY)] * 2
        args = args + tuple(kv_into)
    kv_spec = pl.BlockSpec((None, tm, HP), lambda i: (i // npt, row0 // tm + i % npt, 0))
    kv_shape = jax.ShapeDtypeStruct((B, nk, HP), BF16)
    return pl.pallas_call(
        functools.partial(_mla_prep_kernel, rope=rope, aliased=kv_into is not None),
        grid=(R // tm,),
        in_specs=in_specs,
        out_specs=[pl.BlockSpec((tm, HP), lambda i: (i, 0)), kv_spec, kv_spec],
        out_shape=[jax.ShapeDtypeStruct((R, HP), BF16), kv_shape, kv_shape],
        input_output_aliases=aliases,
        compiler_params=_cparams("parallel"),
        name="mla_prep",
    )(*args)


def _attn_kernel(q_ref, k_ref, v_ref, o_ref, m_scr, l_scr, acc_scr):
    kv = pl.program_id(2)
    HP = MLA_HEAD_PAD
    tk = k_ref.shape[0]

    @pl.when(kv == 0)
    def _():
        m_scr[...] = jnp.full(m_scr.shape, -jnp.inf, F32)
        l_scr[...] = jnp.zeros(l_scr.shape, F32)
        acc_scr[...] = jnp.zeros(acc_scr.shape, F32)

    def scores(h):
        lanes = slice(h * HP, (h + 1) * HP)
        return _dot_nt(q_ref[:, lanes], k_ref[:, lanes])

    s_next = scores(0)
    for h in range(MLA_HEADS):
        lanes = slice(h * HP, (h + 1) * HP)
        s = s_next
        if h + 1 < MLA_HEADS:
            s_next = scores(h + 1)
        m_prev = m_scr[h]
        m_new = jnp.maximum(m_prev, jnp.max(s, axis=1, keepdims=True))
        alpha = jnp.exp2(m_prev - m_new)
        p = jnp.exp2(s - m_new[:, 0:1])
        lp = p[:, 0:128]
        for c in range(1, tk // 128):
            lp = lp + p[:, c * 128:(c + 1) * 128]
        l_scr[h] = alpha * l_scr[h] + lp
        acc_scr[h] = alpha * acc_scr[h] + _dot(p.astype(BF16), v_ref[:, lanes])
        m_scr[h] = m_new

    @pl.when(kv == pl.num_programs(2) - 1)
    def _():
        outs = [acc_scr[h][:, 0:MLA_V] / jnp.sum(l_scr[h], axis=1, keepdims=True) for h in range(MLA_HEADS)]
        o_ref[...] = jnp.concatenate(outs, axis=1).astype(o_ref.dtype)


def _attention(q3, k3, v3, tq, tk, nk):
    B, nq, HP = q3.shape
    return pl.pallas_call(
        _attn_kernel,
        grid=(B, nq // tq, nk // tk),
        in_specs=[pl.BlockSpec((None, tq, HP), lambda b, i, j: (b, i, 0)),
                  pl.BlockSpec((None, tk, HP), lambda b, i, j: (b, j, 0)),
                  pl.BlockSpec((None, tk, HP), lambda b, i, j: (b, j, 0))],
        out_specs=pl.BlockSpec((None, tq, MLA_HEADS * MLA_V), lambda b, i, j: (b, i, 0)),
        out_shape=jax.ShapeDtypeStruct((B, nq, MLA_HEADS * MLA_V), BF16),
        scratch_shapes=[pltpu.VMEM((MLA_HEADS, tq, 128), F32),
                        pltpu.VMEM((MLA_HEADS, tq, 128), F32),
                        pltpu.VMEM((MLA_HEADS, tq, MLA_HEAD_PAD), F32)],
        compiler_params=_cparams("parallel", "parallel", "arbitrary"),
        name="attention",
    )(q3, k3, v3)


def _outproj_kernel(oa, ob, oc, od, w_ref, x_ref, gpost_ref, gt_ref, gpre_ref, sc_ref, sh_ref, xo_ref, h_ref):
    W = GROUP_W
    mix = _dot(oa[...], w_ref[0:W, :])
    mix = mix + _dot(ob[...], w_ref[W:2 * W, :])
    mix = mix + _dot(oc[...], w_ref[2 * W:3 * W, :])
    mix = mix + _dot(od[...], w_ref[3 * W:4 * W, :])
    x = x_ref[...] + gt_ref[...] * (_rms(mix) * gpost_ref[...])
    xo_ref[...] = x
    h_ref[...] = (_rms(x) * gpre_ref[...] * (1.0 + sc_ref[...]) + sh_ref[...]).astype(h_ref.dtype)


def _outproj(l, parts, w_out, x2, mod4, brow, g_post, g_pre_ffn, tm):
    R, D = x2.shape
    W = GROUP_W
    part_spec = pl.BlockSpec((tm, W), lambda i: (i, 0))
    row_spec = pl.BlockSpec((tm, D), lambda i: (i, 0))
    return pl.pallas_call(
        _outproj_kernel,
        grid=(R // tm,),
        in_specs=[part_spec] * 4 + [
            pl.BlockSpec((None, 4 * W, D), lambda i: (l, 0, 0)),
            row_spec,
            _vec_spec(l, D),
            _mod_spec(l, 2, D, brow),
            _vec_spec(l, D),
            _mod_spec(l, 4, D, brow),
            _mod_spec(l, 3, D, brow),
        ],
        out_specs=[row_spec, row_spec],
        out_shape=[jax.ShapeDtypeStruct((R, D), F32), jax.ShapeDtypeStruct((R, D), BF16)],
        compiler_params=_cparams("parallel"),
        name="outproj",
    )(*parts, w_out, x2, g_post, mod4, g_pre_ffn, mod4, mod4)


FFN_HALO = 16
FFN_COLS = 256


def _ffn_kernel(hp_ref, h_ref, hn_ref, wup_ref, cw_ref, cb_ref, wdn_ref, x_ref, gpost_ref, gt_ref, o_ref, y_scr):
    i = pl.program_id(1)
    tm = h_ref.shape[0]
    dff = wdn_ref.shape[0]
    rows = tm + 2 * FFN_HALO
    prev = jnp.where(i == 0, jnp.zeros_like(hp_ref[...]), hp_ref[...])
    nxt = jnp.where(i == pl.num_programs(1) - 1, jnp.zeros_like(hn_ref[...]), hn_ref[...])
    hb = jnp.concatenate([prev, h_ref[...], nxt], axis=0)

    def conv(z, cols):
        w = cw_ref[:, cols]
        out = cb_ref[:, cols] + w[0:1] * pltpu.roll(z, 1, axis=0) + w[1:2] * z + w[2:3] * pltpu.roll(z, rows - 1, axis=0)
        return out[FFN_HALO:FFN_HALO + tm]

    def cols(j):
        return slice(j * FFN_COLS, (j + 1) * FFN_COLS), slice(dff + j * FFN_COLS, dff + (j + 1) * FFN_COLS)

    def up(j):
        ca, cg = cols(j)
        return _dot(hb, wup_ref[:, ca]), _dot(hb, wup_ref[:, cg])

    nchunks = dff // FFN_COLS
    z_next = up(0)
    for j in range(nchunks):
        ca, cg = cols(j)
        za, zg = z_next
        if j + 1 < nchunks:
            z_next = up(j + 1)
        y_scr[:, ca] = _gelu_gate(conv(za, ca), conv(zg, cg)).astype(BF16)
    acc = _dot(y_scr[...], wdn_ref[...])
    o_ref[...] = x_ref[...] + gt_ref[...] * (_rms(acc) * gpost_ref[...])


def _ffn(l, h3, x3, w_up, conv_w, conv_b, w_down, mod4, brow, g_post, tm):
    B, n, D = x3.shape
    nt = n // tm
    hb = tm // FFN_HALO
    nh = n // FFN_HALO
    dff = w_down.shape[1]
    once = pl.Buffered(1)
    return pl.pallas_call(
        _ffn_kernel,
        grid=(B, nt),
        in_specs=[
            pl.BlockSpec((None, FFN_HALO, D), lambda b, i: (b, jnp.maximum(i * hb - 1, 0), 0)),
            pl.BlockSpec((None, tm, D), lambda b, i: (b, i, 0)),
            pl.BlockSpec((None, FFN_HALO, D), lambda b, i: (b, jnp.minimum((i + 1) * hb, nh - 1), 0)),
            pl.BlockSpec((None, D, 2 * dff), lambda b, i: (l, 0, 0), pipeline_mode=once),
            pl.BlockSpec((None, 3, 2 * dff), lambda b, i: (l, 0, 0)),
            pl.BlockSpec((None, 1, 2 * dff), lambda b, i: (l, 0, 0)),
            pl.BlockSpec((None, dff, D), lambda b, i: (l, 0, 0), pipeline_mode=once),
            pl.BlockSpec((None, tm, D), lambda b, i: (b, i, 0)),
            _vec_spec(l, D),
            _mod_spec(l, 5, D, brow),
        ],
        out_specs=pl.BlockSpec((None, tm, D), lambda b, i: (b, i, 0)),
        out_shape=jax.ShapeDtypeStruct((B, n, D), F32),
        scratch_shapes=[pltpu.VMEM((tm, dff), BF16)],
        compiler_params=_cparams("parallel", "parallel"),
        name="conv_ffn",
    )(h3, h3, h3, w_up, conv_w, conv_b, w_down, x3, g_post, mod4)


def _prep_w_in(w_in):
    z = lambda n: jnp.zeros(w_in.shape[:-1] + (n,), w_in.dtype)
    a = w_in[..., 0:512]
    b = w_in[..., 512:1536]
    gl = w_in[..., 1536:1568]
    c = w_in[..., 1568:1824]
    cq = w_in[..., 1824:2048]
    ckv_kr = w_in[..., 2048:2176]
    return jnp.concatenate([a, b, c, cq, z(32), ckv_kr, gl, z(96)], axis=-1).astype(BF16)


def _rope_swap(t):
    q = MLA_ROPE // 4
    return jnp.concatenate([t[..., q:2 * q], t[..., 0:q], t[..., 3 * q:4 * q], t[..., 2 * q:3 * q]], axis=-1)


def _prep_mla(w_uq, w_ukv):
    L = w_uq.shape[0]
    H, NP, RP, HP = MLA_HEADS, MLA_NOPE, MLA_ROPE, MLA_HEAD_PAD
    wq = w_uq.reshape(L, MLA_Q_RANK, H, NP + RP)
    zq = jnp.zeros((L, MLA_Q_RANK, H, HP - NP - RP), w_uq.dtype)
    znope = jnp.zeros((L, MLA_Q_RANK, H, NP), w_uq.dtype)
    q_main = jnp.concatenate([wq, zq], axis=-1)
    q_swap = jnp.concatenate([znope, _rope_swap(wq[..., NP:]), zq], axis=-1)
    padq = lambda w: jnp.pad(w.reshape(L, MLA_Q_RANK, H * HP), ((0, 0), (0, 256 - MLA_Q_RANK), (0, 0)))

    wkv = w_ukv.reshape(L, MLA_KV_RANK, H, NP + MLA_V)
    zk = jnp.zeros((L, MLA_KV_RANK, H, HP - NP), w_ukv.dtype)
    k_lat = jnp.concatenate([wkv[..., :NP], zk], axis=-1)
    eye = jnp.eye(RP, dtype=w_ukv.dtype)
    place = lambda e: jnp.broadcast_to(
        jnp.concatenate([jnp.zeros((RP, NP), e.dtype), e, jnp.zeros((RP, HP - NP - RP), e.dtype)], axis=-1)[None, :, None, :],
        (L, RP, H, HP))
    k_main = jnp.concatenate([k_lat, place(eye)], axis=1)
    k_swap = jnp.concatenate([jnp.zeros_like(k_lat), place(_rope_swap(eye))], axis=1)
    v_lat = jnp.concatenate([wkv[..., NP:], jnp.zeros((L, MLA_KV_RANK, H, HP - MLA_V), w_ukv.dtype)], axis=-1)
    v_main = jnp.concatenate([v_lat, jnp.zeros((L, RP, H, HP), w_ukv.dtype)], axis=1)
    flat = lambda w: w.reshape(L, w.shape[1], H * HP).astype(BF16)
    return (padq(q_main).astype(BF16), padq(q_swap).astype(BF16), flat(k_main), flat(k_swap), flat(v_main))


def _rope_tables(n):
    rows = n // GRID_W
    nf = MLA_ROPE // 4
    inv = ROPE_BASE ** (-jnp.arange(nf, dtype=F32) / nf)
    ar = jnp.arange(rows, dtype=F32)[:, None] * inv[None, :]
    ac = jnp.arange(GRID_W, dtype=F32)[:, None] * inv[None, :]
    by_row = lambda t: jnp.repeat(t, GRID_W, axis=0)
    by_col = lambda t: jnp.tile(t, (rows, 1))
    cr, sr, cc, sn = by_row(jnp.cos(ar)), by_row(jnp.sin(ar)), by_col(jnp.cos(ac)), by_col(jnp.sin(ac))
    one = jnp.ones((n, MLA_NOPE), F32)
    zero = jnp.zeros((n, MLA_HEAD_PAD - MLA_NOPE - MLA_ROPE), F32)
    cos = jnp.concatenate([one, cr, cr, cc, cc, zero], axis=1)
    sin = jnp.concatenate([0.0 * one, -sr, sr, -sn, sn, zero], axis=1)
    return cos, sin


def _pick_tile(n, want):
    t = min(n, want)
    while n % t:
        t //= 2
    return t


def kernel(x, c, ctx, c_ctx, w_mod, b_mod, g_pre_mix, g_post_mix, g_pre_ffn, g_post_ffn, w_in,
           sgu_norm, sgu_w, sgu_b, gla_w_gate, gla_b_gate, gla_norm,
           s5_a_re, s5_a_im, s5_log_dt, s5_b_re, s5_b_im, s5_c_re, s5_c_im, s5_d, s5_w_glu, s5_b_glu,
           mla_q_norm, mla_w_uq, mla_kv_norm, mla_w_ukv, w_out,
           ffn_w_up, ffn_conv_w, ffn_conv_b, ffn_w_down):
    B, n, D = x.shape
    nctx = ctx.shape[1]
    L = w_mod.shape[0]
    W = GROUP_W
    assert B < 8 and n % 512 == 0 and nctx % 128 == 0 and n % GRID_W == 0

    c8 = jnp.concatenate([c, c_ctx[None, :], jnp.zeros((8 - B - 1, D), F32)], axis=0)
    mod4 = _modulation(c8, w_mod, b_mod).reshape(L, 8, 1, 6 * D)
    vec = lambda p: p.reshape(L, 1, -1).astype(F32)
    g_pre_mix, g_post_mix, g_pre_ffn, g_post_ffn = map(vec, (g_pre_mix, g_post_mix, g_pre_ffn, g_post_ffn))
    w_in_p = _prep_w_in(w_in)
    sgu_gn = vec(sgu_norm)
    sgu_w_st = sgu_w.reshape(L, -1, MLP_CHUNK).astype(BF16)
    sgu_bias = jnp.repeat(jnp.swapaxes(sgu_b, 1, 2), HEAD_D, axis=2).astype(F32)
    ones_bd = jnp.kron(jnp.eye(W // HEAD_D, dtype=F32), jnp.ones((HEAD_D, HEAD_D), F32)).astype(BF16)
    gla_wg = jnp.zeros((L, 2, 128, W), F32)
    gla_wg = gla_wg.at[:, 0, 0:GATE_RANK].set(gla_w_gate[:, 0]).at[:, 1, GATE_RANK:2 * GATE_RANK].set(gla_w_gate[:, 1])
    gla_bg = gla_b_gate.reshape(L, 2, 1, W).astype(F32)
    gla_gn = vec(gla_norm)
    s5_wglu = s5_w_glu.astype(BF16)
    s5_bglu = vec(s5_b_glu)
    mla_qn = jnp.pad(mla_q_norm, ((0, 0), (0, 256 - MLA_Q_RANK))).reshape(L, 1, 256).astype(F32)
    mla_kvn = jnp.pad(mla_kv_norm, ((0, 0), (0, 128 - MLA_KV_RANK))).reshape(L, 1, 128).astype(F32)
    mla_wts = _prep_mla(mla_w_uq, mla_w_ukv)
    tables = _rope_tables(n)
    w_out_b = w_out.astype(BF16)
    w_up_b = ffn_w_up.astype(BF16)
    w_down_b = ffn_w_down.astype(BF16)
    dff = ffn_w_down.shape[1]
    half_gate = jnp.concatenate([jnp.ones((dff,), F32), jnp.full((dff,), 0.5, F32)])
    conv_w = ffn_conv_w.astype(F32) * half_gate
    conv_b = (ffn_conv_b.astype(F32) * half_gate).reshape(L, 1, -1)

    s5_mats = jax.vmap(_s5_matrices)(s5_a_re, s5_a_im, s5_log_dt, s5_b_re, s5_b_im, s5_c_re, s5_c_im, s5_d)

    tm_x = _pick_tile(n, 512)
    tm_c = _pick_tile(nctx, 512)
    tpb_x = n // tm_x
    brow_x = lambda i: i // tpb_x
    brow_c = lambda *g: B
    brow_x2 = lambda b, i: b

    xs = x.reshape(B * n, D)
    cs = ctx.reshape(B * nctx, D)

    for l in range(L):
        need_ctx = l < L - 1
        pa_x, pb_x, pc_x, pd_x, pg_x = _inproj(l, xs, mod4, brow_x, g_pre_mix, w_in_p, tm_x)
        pa_c, pb_c, pc_c, pd_c, pg_c = _inproj(l, cs, mod4, brow_c, g_pre_mix, w_in_p, tm_c)

        oa_x = _sgu(l, pa_x, sgu_gn, sgu_w_st, sgu_bias, ones_bd, tm_x)
        r3 = lambda t, m: t.reshape(B, m, t.shape[-1])
        ob_c, ob_x = _gla(l, r3(pb_c, nctx), r3(pg_c, nctx), r3(pb_x, n), r3(pg_x, n),
                          gla_wg, gla_bg, gla_gn, ones_bd, _pick_tile(nctx, GLA_TILE), _pick_tile(n, 4 * GLA_TILE))
        oc_c, oc_x = _s5(l, r3(pc_c, nctx), r3(pc_x, n), s5_mats, s5_wglu, s5_bglu, need_ctx, tm_c)
        wq, wqs, wk, wks, wv = mla_wts
        nk = n + nctx
        tm_kv = _pick_tile(nctx, tm_x)
        q_x, k_all, v_all = _mla_prep(l, pd_x, mla_qn, mla_kvn, mla_wts, tables, tm_kv, n, B, nctx, nk)
        q_c, k_all, v_all = _mla_prep(l, pd_c, mla_qn, mla_kvn, (wq, None, wk, None, wv), None, tm_c, nctx, B, 0, nk,
                                      kv_into=(k_all, v_all))
        tk = next(t for t in (2816, 1408, 768, 512, 256, 128) if nk % t == 0)
        od_x = _attention(r3(q_x, n), k_all, v_all, tm_x, tk, nk).reshape(B * n, W)

        xs, hx = _outproj(l, (oa_x, ob_x.reshape(B * n, W), oc_x, od_x), w_out_b, xs, mod4, brow_x,
                          g_post_mix, g_pre_ffn, tm_x)
        xs = _ffn(l, hx.reshape(B, n, D), xs.reshape(B, n, D), w_up_b, conv_w, conv_b, w_down_b, mod4, brow_x2,
                  g_post_ffn, tm_x).reshape(B * n, D)

        if need_ctx:
            oa_c = _sgu(l, pa_c, sgu_gn, sgu_w_st, sgu_bias, ones_bd, tm_c)
            od_c = _attention(r3(q_c, nctx), k_all, v_all, tm_c, nctx, nctx).reshape(B * nctx, W)
            cs, hc = _outproj(l, (oa_c, ob_c.reshape(B * nctx, W), oc_c, od_c), w_out_b, cs, mod4, brow_c,
                              g_post_mix, g_pre_ffn, tm_c)
            cs = _ffn(l, hc.reshape(B, nctx, D), cs.reshape(B, nctx, D), w_up_b, conv_w, conv_b, w_down_b, mod4,
                      brow_c, g_post_ffn, tm_c).reshape(B * nctx, D)
    return xs.reshape(B, n, D)
```

```python
import functools

import jax
import jax.numpy as jnp
from jax import lax
from jax.experimental import pallas as pl
from jax.experimental.pallas import tpu as pltpu

F32 = jnp.float32
BF16 = jnp.bfloat16

EPS = 1e-6
GRID_W = 64
GROUP_W = 256
HEAD_D = 64
MLP_CHUNK = 128
GATE_RANK = 16
GATE_TEMP = 16.0
GLA_CHUNK = 64
GLA_TILE = 256
S5_IN = 16
S5_T = 16
S5_TILE = 128
MLA_HEADS = 4
MLA_NOPE = 64
MLA_ROPE = 32
MLA_V = 64
MLA_Q_RANK = 224
MLA_KV_RANK = 96
MLA_HEAD_PAD = 128
ROPE_BASE = 10000.0
LOG2E = 1.4426950408889634

VMEM_LIMIT = 48 * 1024 * 1024


def _cparams(*sem):
    return pltpu.CompilerParams(dimension_semantics=sem, vmem_limit_bytes=VMEM_LIMIT)


def _dot(a, b):
    return jnp.dot(a, b, preferred_element_type=F32)


def _dot_nt(a, b):
    return lax.dot_general(a, b, (((1,), (1,)), ((), ())), preferred_element_type=F32)


def _dot_tn(a, b):
    return lax.dot_general(a, b, (((0,), (0,)), ((), ())), preferred_element_type=F32)


def _split(a):
    hi = a.astype(BF16)
    lo = (a - hi.astype(F32)).astype(BF16)
    return hi, lo


def _dot_x2(a, b_bf16):
    hi, lo = _split(a)
    return _dot(hi, b_bf16) + _dot(lo, b_bf16)


def _dot_x3(a, b):
    ah, al = _split(a)
    bh, bl = _split(b)
    return _dot(ah, bh) + _dot(al, bh) + _dot(ah, bl)


def _rms(x):
    return x * lax.rsqrt(jnp.mean(x * x, axis=-1, keepdims=True) + EPS)


def _gelu(x):
    return 0.5 * x * (1.0 + jnp.tanh(0.7978845608028654 * (x + 0.044715 * (x * x * x))))


def _gelu_gate(a, half_g):
    u = a * (0.7978845608028654 + 0.035677408136300125 * (a * a))
    return (a * half_g) * (1.0 + jnp.tanh(u))


def _sigmoid(x):
    return 1.0 / (1.0 + jnp.exp(-x))


def _lane_group(shape, width):
    return lax.broadcasted_iota(jnp.int32, shape, len(shape) - 1) // width


def _mod_kernel(c_ref, w_ref, b_ref, o_ref):
    c = c_ref[...]
    s = c * _sigmoid(c)
    o_ref[...] = _dot_x3(s, w_ref[...]) + b_ref[...]


def _modulation(c8, w_mod, b_mod):
    L, D, W = w_mod.shape
    tn = 1536
    return pl.pallas_call(
        _mod_kernel,
        grid=(L, W // tn),
        in_specs=[
            pl.BlockSpec((8, D), lambda l, j: (0, 0)),
            pl.BlockSpec((None, D, tn), lambda l, j: (l, 0, j)),
            pl.BlockSpec((None, 1, tn), lambda l, j: (l, 0, j)),
        ],
        out_specs=pl.BlockSpec((None, 8, tn), lambda l, j: (l, 0, j)),
        out_shape=jax.ShapeDtypeStruct((L, 8, W), F32),
        compiler_params=_cparams("arbitrary", "arbitrary"),
        name="modulation",
    )(c8, w_mod, b_mod.reshape(L, 1, W))


def _mod_spec(l, j, D, bfn):
    return pl.BlockSpec((None, None, 1, D), lambda *g: (l, bfn(*g), 0, j))


def _vec_spec(l, width):
    return pl.BlockSpec((None, 1, width), lambda *g: (l, 0, 0))


IN_SLABS = (("a", 0, 512), ("b", 512, 1024), ("c", 1536, 256), ("d", 1792, 384), ("g", 2176, 128))
IN_PAD_COLS = 2304


def _sgu_tile(p, gn_ref, w_ref, b_ref, ones_ref, o_ref):
    tm = p.shape[0]
    g = _gelu(p)
    u = g[:, :GROUP_W]
    v = g[:, GROUP_W:]
    ms = _dot_x2(v * v, ones_ref[...]) * (1.0 / HEAD_D)
    vb = (v * lax.rsqrt(ms + EPS) * gn_ref[...]).astype(BF16)
    head = _lane_group((MLP_CHUNK, GROUP_W), HEAD_D)
    w = w_ref[...]
    for c in range(tm // MLP_CHUNK):
        rows = slice(c * MLP_CHUNK, (c + 1) * MLP_CHUNK)
        r = _dot(w, vb[rows])
        s = b_ref[...]
        for h in range(GROUP_W // HEAD_D):
            s = s + jnp.where(head == h, r[h * MLP_CHUNK:(h + 1) * MLP_CHUNK], 0.0)
        o_ref[rows, :] = (u[rows] * s).astype(o_ref.dtype)


def _inproj_kernel(x_ref, g_ref, sc_ref, sh_ref, w_ref, gn_ref, wsp_ref, bsp_ref, ones_ref, oa, ob, oc, od, og):
    h = _rms(x_ref[...]) * g_ref[...] * (1.0 + sc_ref[...]) + sh_ref[...]
    hb = h.astype(BF16)
    (_, off_a, width_a) = IN_SLABS[0]
    pa = _dot(hb, w_ref[:, off_a:off_a + width_a])
    for (_, off, width), o_ref in zip(IN_SLABS[1:], (ob, oc, od, og)):
        o_ref[...] = _dot(hb, w_ref[:, off:off + width])
    _sgu_tile(pa, gn_ref, wsp_ref, bsp_ref, ones_ref, oa)


def _inproj(l, x2, mod4, brow, g_pre, w_in_p, sgu, tm):
    R, D = x2.shape
    gn, w_st, bias, ones_bd = sgu
    H = GROUP_W // HEAD_D
    grid = (R // tm,)
    outs = [jax.ShapeDtypeStruct((R, GROUP_W), BF16)] + [jax.ShapeDtypeStruct((R, width), F32) for (_, _, width) in IN_SLABS[1:]]
    return pl.pallas_call(
        _inproj_kernel,
        grid=grid,
        in_specs=[
            pl.BlockSpec((tm, D), lambda i: (i, 0)),
            _vec_spec(l, D),
            _mod_spec(l, 1, D, brow),
            _mod_spec(l, 0, D, brow),
            pl.BlockSpec((None, D, IN_PAD_COLS), lambda i: (l, 0, 0)),
            _vec_spec(l, GROUP_W),
            pl.BlockSpec((None, H * MLP_CHUNK, MLP_CHUNK), lambda i: (l, 0, 0)),
            pl.BlockSpec((None, MLP_CHUNK, GROUP_W), lambda i: (l, 0, 0)),
            pl.BlockSpec((GROUP_W, GROUP_W), lambda i: (0, 0)),
        ],
        out_specs=[pl.BlockSpec((tm, GROUP_W), lambda i: (i, 0))]
        + [pl.BlockSpec((tm, width), lambda i: (i, 0)) for (_, _, width) in IN_SLABS[1:]],
        out_shape=outs,
        compiler_params=_cparams("parallel"),
        name="inproj",
    )(x2, g_pre, mod4, mod4, w_in_p, gn, w_st, bias, ones_bd)


def _gla_kernel(*refs, rev, finish):
    if finish:
        (pb_ref, pg_ref, wg_ref, bg_ref, s0_ref, ones_ref, dec_ref, oprev_ref, gn_ref, o_ref, sfin_ref, st_scr) = refs
    else:
        (pb_ref, pg_ref, wg_ref, bg_ref, s0_ref, ones_ref, dec_ref, o_ref, sfin_ref, st_scr) = refs
    i = pl.program_id(1)
    C, W, T = GLA_CHUNK, GROUP_W, GLA_TILE
    H = W // HEAD_D
    nsub = T // C
    ntile = pb_ref.shape[0] // T

    @pl.when(i == 0)
    def _():
        st_scr[...] = s0_ref[...]

    dec = dec_ref[...]
    mask4 = jnp.concatenate([dec[0:T].astype(F32)] * H, axis=0)
    head_t = _lane_group((T, W), HEAD_D)
    bd = (lax.broadcasted_iota(jnp.int32, (W, W), 0) // HEAD_D) == _lane_group((W, W), HEAD_D)
    tiles = list(range(ntile - 1, -1, -1)) if rev else list(range(ntile))
    subs = list(range(nsub - 1, -1, -1)) if rev else list(range(nsub))
    rows = lambda a: slice(a * T, (a + 1) * T)
    sub = lambda j: slice(j * C, (j + 1) * C)

    logg = {}
    for a in tiles:
        z = _dot_x3(pg_ref[rows(a), :], wg_ref[...]) + bg_ref[...]
        logg[a] = (jnp.minimum(z, 0.0) - jnp.log(1.0 + jnp.exp(-jnp.abs(z)))) * (1.0 / GATE_TEMP)
    sums = {}
    for a in tiles:
        g_hi, g_lo = _split(logg[a])
        sums[a] = _dot(dec, g_hi) + _dot(dec, g_lo)
    qin, qoff, kin, kend, vb, tot, ptile = {}, {}, {}, {}, {}, {}, {}
    for a in tiles:
        cum, tot[a], off = sums[a][0:T], sums[a][T:2 * T], sums[a][2 * T:3 * T]
        q = pb_ref[rows(a), 0:W] * (HEAD_D ** -0.5)
        k = pb_ref[rows(a), W:2 * W]
        e = jnp.exp(cum)
        qin[a] = (q * e).astype(BF16)
        qoff[a] = (q * (e * jnp.exp(off))).astype(BF16)
        kin[a] = (k * jnp.exp(-cum)).astype(BF16)
        kend[a] = (k * jnp.exp(tot[a] - cum)).astype(BF16)
        vb[a] = pb_ref[rows(a), 2 * W:3 * W].astype(BF16)
        last = sub(subs[-1])
        ptile[a] = jnp.exp(off[last][0:1] + tot[a][last][0:1])
    sc = {}
    for a in tiles:
        qst = jnp.concatenate([jnp.where(head_t == h, qin[a], jnp.zeros_like(qin[a])) for h in range(H)], axis=0)
        sc[a] = (_dot_nt(qst, kin[a]) * mask4).astype(BF16)
    ost = {a: _dot(sc[a], vb[a]) for a in tiles}
    kvt = {(a, j): jnp.where(bd, _dot_tn(vb[a][sub(j)], kend[a][sub(j)]), 0.0) for a in tiles for j in subs}
    cross, rend = {}, {}
    for a in tiles:
        r = None
        for j in subs:
            if r is not None:
                cross[(a, j)] = _dot_nt(qin[a][sub(j)], r.astype(BF16))
                r = r * jnp.exp(tot[a][sub(j)][0:1]) + kvt[(a, j)]
            else:
                cross[(a, j)] = jnp.zeros((C, W), F32)
                r = kvt[(a, j)]
        rend[a] = r
    st = st_scr[...]
    inter = {}
    for a in tiles:
        inter[a] = _dot_nt(qoff[a], st.astype(BF16))
        st = st * ptile[a] + rend[a]
    st_scr[...] = st

    for a in tiles:
        o = inter[a] + jnp.concatenate([cross[(a, j)] for j in range(nsub)], axis=0)
        for h in range(H):
            o = o + jnp.where(head_t == h, ost[a][h * T:(h + 1) * T], 0.0)
        if finish:
            o = o + oprev_ref[rows(a), :]
            ms = _dot_x2(o * o, ones_ref[...]) * (1.0 / HEAD_D)
            o = o * lax.rsqrt(ms + EPS) * gn_ref[...]
            r = pb_ref[rows(a), 3 * W:4 * W]
            o_ref[rows(a), :] = (o * (r * _sigmoid(r))).astype(o_ref.dtype)
        else:
            o_ref[rows(a), :] = o

    @pl.when(i == pl.num_programs(1) - 1)
    def _():
        sfin_ref[...] = st_scr[...]


def _gla_decay_matrices():
    T, C = GLA_TILE, GLA_CHUNK
    t = jnp.arange(T)[:, None]
    s = jnp.arange(T)[None, :]
    same = (t // C) == (s // C)
    mats = []
    for rev in (False, True):
        seen = (s >= t) if rev else (s <= t)
        earlier = (s // C > t // C) if rev else (s // C < t // C)
        mats.append(jnp.concatenate([same & seen, same, earlier], axis=0))
    return jnp.stack(mats).astype(BF16)


def _gla_pass(l, d, pb3, pg3, wg, bg, s0, ones_bd, dec, oprev, gn, tt):
    B, n, _ = pb3.shape
    nt = n // tt
    rev = d == 1
    finish = oprev is not None
    W = GROUP_W

    def tok(b, i):
        return (b, (nt - 1 - i) if rev else i, 0)

    in_specs = [
        pl.BlockSpec((None, tt, 4 * W), tok),
        pl.BlockSpec((None, tt, 128), tok),
        pl.BlockSpec((None, None, 128, W), lambda b, i: (l, d, 0, 0)),
        pl.BlockSpec((None, None, 1, W), lambda b, i: (l, d, 0, 0)),
        pl.BlockSpec((None, W, W), lambda b, i: (b, 0, 0)),
        pl.BlockSpec((W, W), lambda b, i: (0, 0)),
        pl.BlockSpec((None, 3 * GLA_TILE, GLA_TILE), lambda b, i: (d, 0, 0)),
    ]
    args = [pb3, pg3, wg, bg, s0, ones_bd, dec]
    if finish:
        in_specs += [pl.BlockSpec((None, tt, W), tok), _vec_spec(l, W)]
        args += [oprev, gn]
    return pl.pallas_call(
        functools.partial(_gla_kernel, rev=rev, finish=finish),
        grid=(B, nt),
        in_specs=in_specs,
        out_specs=[pl.BlockSpec((None, tt, W), tok), pl.BlockSpec((None, W, W), lambda b, i: (b, 0, 0))],
        out_shape=[jax.ShapeDtypeStruct((B, n, W), BF16 if finish else F32),
                   jax.ShapeDtypeStruct((B, W, W), F32)],
        scratch_shapes=[pltpu.VMEM((W, W), F32)],
        compiler_params=_cparams("arbitrary", "arbitrary"),
        name="gla_bwd" if rev else "gla_fwd",
    )(*args)


def _gla(l, pb_c, pg_c, pb_x, pg_x, wg, bg, gn, ones_bd, tt_c, tt_x):
    B = pb_x.shape[0]
    zero = jnp.zeros((B, GROUP_W, GROUP_W), F32)
    dec = _gla_decay_matrices()
    ofc, sfc = _gla_pass(l, 0, pb_c, pg_c, wg, bg, zero, ones_bd, dec, None, None, tt_c)
    ofx, _ = _gla_pass(l, 0, pb_x, pg_x, wg, bg, sfc, ones_bd, dec, None, None, tt_x)
    ob_c, sbc = _gla_pass(l, 1, pb_c, pg_c, wg, bg, zero, ones_bd, dec, ofc, gn, tt_c)
    ob_x, _ = _gla_pass(l, 1, pb_x, pg_x, wg, bg, sbc, ones_bd, dec, ofx, gn, tt_x)
    return ob_c, ob_x


def _s5_matrices(a_re, a_im, log_dt, b_re, b_im, c_re, c_im, d_skip):
    T = S5_T
    G, P = a_re.shape[1:]
    I = b_re.shape[-1]
    lam = lax.complex(a_re.astype(F32), a_im.astype(F32))
    ldt = lam * jnp.exp(log_dt.astype(F32))[..., None]
    lam_bar = jnp.exp(ldt)
    b_bar = ((lam_bar - 1.0) / lam)[..., None] * lax.complex(b_re.astype(F32), b_im.astype(F32))
    cmat = lax.complex(c_re.astype(F32), c_im.astype(F32))
    steps = jnp.arange(T + 1, dtype=F32)
    pw = jnp.exp(ldt[..., None] * steps)
    taps = jnp.einsum('dgop,dgpk,dgpi->dgiko', cmat, pw[..., :T], b_bar).real
    taps = taps.at[0, :, :, 0, :].add(jnp.eye(I, dtype=F32)[None] * d_skip.astype(F32)[:, :, None])
    row = T * I
    zeros = jnp.zeros((G, I, row), F32)
    fwd = jnp.concatenate([zeros, taps[0].reshape(G, I, row)], axis=-1)
    bwd = jnp.concatenate([jnp.flip(taps[1], axis=2).reshape(G, I, row), zeros], axis=-1)
    m = jnp.stack([fwd[..., (T - s) * I:(T - s) * I + row] + bwd[..., (T - 1 - s) * I:(T - 1 - s) * I + row]
                   for s in range(T)], axis=1).reshape(G, row, row)

    ar = jnp.arange(T)
    pf = pw[0][..., T - 1 - ar]
    pb = pw[1][..., ar]
    bf = jnp.einsum('gps,gpi->gsip', pf, b_bar[0]).reshape(G, T * I, P)
    bb = jnp.einsum('gps,gpi->gsip', pb, b_bar[1]).reshape(G, T * I, P)
    bmat = jnp.concatenate([bf.real, bf.imag, bf.imag, bf.real, bb.real, bb.imag, bb.imag, bb.real], axis=-1)

    cf = jnp.einsum('gop,gpt->gpto', cmat[0], pw[0][..., 1 + ar]).reshape(G, P, T * I)
    cb = jnp.einsum('gop,gpt->gpto', cmat[1], pw[1][..., T - ar]).reshape(G, P, T * I)
    w = jnp.concatenate([m, cf.real, -cf.imag, cb.real, -cb.imag], axis=1)

    a = pw[..., T]
    acoef = jnp.stack([jnp.concatenate([a[0].real, a[0].real], axis=-1), jnp.concatenate([-a[0].imag, a[0].imag], axis=-1),
                       jnp.concatenate([a[1].real, a[1].real], axis=-1), jnp.concatenate([-a[1].imag, a[1].imag], axis=-1)])
    return bmat.astype(BF16), w.astype(BF16), acoef


def _s5_state_kernel(x_ref, b_ref, o_ref):
    o_ref[...] = _dot(x_ref[...], b_ref[...])


def _s5_state(l, xg, bmat, tr):
    G, R, K = xg.shape
    N = bmat.shape[-1]
    return pl.pallas_call(
        _s5_state_kernel,
        grid=(G, R // tr),
        in_specs=[pl.BlockSpec((None, tr, K), lambda g, i: (g, i, 0)),
                  pl.BlockSpec((None, None, K, N), lambda g, i: (l, g, 0, 0))],
        out_specs=pl.BlockSpec((None, tr, N), lambda g, i: (g, i, 0)),
        out_shape=jax.ShapeDtypeStruct((G, R, N), F32),
        compiler_params=_cparams("parallel", "parallel"),
        name="s5_state",
    )(xg, bmat)


def _s5_scan_kernel(sf_ref, sb_ref, a_ref, h0_ref, hf_ref, hb_ref, hfin_ref, st_scr):
    j = pl.program_id(1)
    tc = sf_ref.shape[1]

    @pl.when(j == 0)
    def _():
        st_scr[...] = h0_ref[...]

    a1f, a2f, a1b, a2b = a_ref[0], a_ref[1], a_ref[2], a_ref[3]
    half = a1f.shape[-1]

    def body(i, hs):
        h1f, h2f, h1b, h2b = hs
        ib = tc - 1 - i
        sf = sf_ref[:, i, :]
        sb = sb_ref[:, ib, :]
        hf_ref[:, i, :] = h1f
        hb_ref[:, ib, :] = h1b
        return (h1f * a1f + h2f * a2f + sf[:, 0:half], h2f * a1f - h1f * a2f + sf[:, half:],
                h1b * a1b + h2b * a2b + sb[:, 0:half], h2b * a1b - h1b * a2b + sb[:, half:])

    hs = lax.fori_loop(0, tc, body, (st_scr[0], st_scr[1], st_scr[2], st_scr[3]))
    for k in range(4):
        st_scr[k] = hs[k]

    @pl.when(j == pl.num_programs(1) - 1)
    def _():
        hfin_ref[...] = st_scr[...]


def _s5_scan(l, s, acoef, h0, B, tc):
    G, R, n2 = s.shape
    nt = R // B // tc
    half = n2 // 4
    fwd = lambda b, j: (0, b * nt + j, 0)
    bwd = lambda b, j: (0, b * nt + nt - 1 - j, 0)
    st_spec = pl.BlockSpec((None, 4, G, half), lambda b, j: (b, 0, 0, 0))
    return pl.pallas_call(
        _s5_scan_kernel,
        grid=(B, nt),
        in_specs=[pl.BlockSpec((G, tc, 2 * half), fwd),
                  pl.BlockSpec((G, tc, 2 * half), lambda b, j: (0, b * nt + nt - 1 - j, 1)),
                  pl.BlockSpec((None, 4, G, half), lambda b, j: (l, 0, 0, 0)),
                  st_spec],
        out_specs=[pl.BlockSpec((G, tc, half), fwd), pl.BlockSpec((G, tc, half), bwd), st_spec],
        out_shape=[jax.ShapeDtypeStruct((G, R, half), F32), jax.ShapeDtypeStruct((G, R, half), F32),
                   jax.ShapeDtypeStruct((B, 4, G, half), F32)],
        scratch_shapes=[pltpu.VMEM((4, G, half), F32)],
        compiler_params=_cparams("arbitrary", "arbitrary"),
        name="s5_scan",
    )(s, s, acoef, h0)


def _s5_out_kernel(x_ref, hf_ref, hb_ref, w_ref, o_ref):
    k = x_ref.shape[-1]
    kh = hf_ref.shape[-1]
    y = _dot(x_ref[...], w_ref[0:k, :])
    y = y + _dot(hf_ref[...].astype(BF16), w_ref[k:k + kh, :])
    o_ref[...] = y + _dot(hb_ref[...].astype(BF16), w_ref[k + kh:, :])


def _s5_out(l, xg, hf, hb, w, tr):
    G, R, K = xg.shape
    kh = hf.shape[-1]
    row = lambda width: pl.BlockSpec((None, tr, width), lambda g, i: (g, i, 0))
    return pl.pallas_call(
        _s5_out_kernel,
        grid=(G, R // tr),
        in_specs=[row(K), row(kh), row(kh), pl.BlockSpec((None, None, K + 2 * kh, K), lambda g, i: (l, g, 0, 0))],
        out_specs=row(K),
        out_shape=jax.ShapeDtypeStruct((G, R, K), F32),
        compiler_params=_cparams("parallel", "parallel"),
        name="s5_out",
    )(xg, hf, hb, w)


def _s5_finish_kernel(y_ref, w_ref, b_ref, o_ref):
    y = _gelu(y_ref[...])
    gate = _sigmoid(_dot(y.astype(BF16), w_ref[...]) + b_ref[...])
    o_ref[...] = (y * gate).astype(o_ref.dtype)


def _s5_finish(l, y2, w_glu, b_glu, tm):
    R, W = y2.shape
    return pl.pallas_call(
        _s5_finish_kernel,
        grid=(R // tm,),
        in_specs=[pl.BlockSpec((tm, W), lambda i: (i, 0)),
                  pl.BlockSpec((None, W, W), lambda i: (l, 0, 0)),
                  _vec_spec(l, W)],
        out_specs=pl.BlockSpec((tm, W), lambda i: (i, 0)),
        out_shape=jax.ShapeDtypeStruct((R, W), BF16),
        compiler_params=_cparams("parallel"),
        name="s5_finish",
    )(y2, w_glu, b_glu)


def _to_groups(u3):
    B, n, W = u3.shape
    G = W // S5_IN
    x = u3.astype(BF16).reshape(B, n // S5_T, S5_T, G, S5_IN).transpose(3, 0, 1, 2, 4)
    return x.reshape(G, B * (n // S5_T), S5_T * S5_IN)


def _from_groups(y, B):
    G, R, _ = y.shape
    nc = R // B
    y = y.reshape(G, B, nc, S5_T, S5_IN).transpose(1, 2, 3, 0, 4)
    return y.reshape(B * nc * S5_T, G * S5_IN)


def _s5_gather_kernel(x_ref, o_ref, xs_scr, xt_scr):
    G, tc, _ = o_ref.shape
    nh = x_ref.shape[1] // 128
    gh = G // nh
    for h in range(nh):
        xs_scr[h] = x_ref[:, h * 128:(h + 1) * 128]
    for t in range(S5_T):
        for h in range(nh):
            rt = xs_scr[h, pl.ds(t, tc, stride=S5_T), :].T
            for g in range(gh):
                xt_scr[h * gh + g, t * S5_IN:(t + 1) * S5_IN, :] = rt[g * S5_IN:(g + 1) * S5_IN, :]
    for g in range(G):
        o_ref[g] = xt_scr[g].T.astype(o_ref.dtype)


def _s5_gather(x2, tc):
    N, W = x2.shape
    G = W // S5_IN
    R = N // S5_T
    return pl.pallas_call(
        _s5_gather_kernel,
        grid=(R // tc,),
        in_specs=[pl.BlockSpec((tc * S5_T, W), lambda i: (i, 0))],
        out_specs=pl.BlockSpec((G, tc, S5_T * S5_IN), lambda i: (0, i, 0)),
        out_shape=jax.ShapeDtypeStruct((G, R, S5_T * S5_IN), BF16),
        scratch_shapes=[pltpu.VMEM((W // 128, tc * S5_T, 128), F32), pltpu.VMEM((G, S5_T * S5_IN, tc), F32)],
        compiler_params=_cparams("parallel"),
        name="s5_gather",
    )(x2)


def _s5_scatter_kernel(y_ref, w_ref, b_ref, o_ref, yt_scr, tok_scr):
    G, tc, _ = y_ref.shape
    nh = tok_scr.shape[0]
    gh = G // nh
    for g in range(G):
        yt = y_ref[g].T
        for t in range(S5_T):
            yt_scr[t, g // gh, (g % gh) * S5_IN:(g % gh + 1) * S5_IN, :] = yt[t * S5_IN:(t + 1) * S5_IN, :]
    for t in range(S5_T):
        for h in range(nh):
            tok_scr[h, pl.ds(t, tc, stride=S5_T), :] = yt_scr[t, h].T
    y = _gelu(jnp.concatenate([tok_scr[h] for h in range(nh)], axis=1))
    gate = _sigmoid(_dot(y.astype(BF16), w_ref[...]) + b_ref[...])
    o_ref[...] = (y * gate).astype(o_ref.dtype)


def _s5_scatter_finish(l, yg, w_glu, b_glu, tc):
    G, R, K = yg.shape
    W = G * S5_IN
    return pl.pallas_call(
        _s5_scatter_kernel,
        grid=(R // tc,),
        in_specs=[pl.BlockSpec((G, tc, K), lambda i: (0, i, 0)),
                  pl.BlockSpec((None, W, W), lambda i: (l, 0, 0)),
                  _vec_spec(l, W)],
        out_specs=pl.BlockSpec((tc * S5_T, W), lambda i: (i, 0)),
        out_shape=jax.ShapeDtypeStruct((R * S5_T, W), BF16),
        scratch_shapes=[pltpu.VMEM((S5_T, W // 128, 128, tc), F32), pltpu.VMEM((W // 128, tc * S5_T, 128), F32)],
        compiler_params=_cparams("parallel"),
        name="s5_scatter",
    )(yg, w_glu, b_glu)


def _s5(l, pc_c, pc_x, mats, w_glu, b_glu, need_ctx, tm_c):
    bmat, w, acoef = mats
    B, n, W = pc_x.shape
    G = W // S5_IN
    xg_c = _to_groups(pc_c)
    tc_x = min(S5_TILE, n // S5_T)
    xg_x = _s5_gather(pc_x.reshape(B * n, W), tc_x)
    tr_c = min(512, xg_c.shape[1])
    tr_x = min(512, xg_x.shape[1])
    h0 = jnp.zeros((B, 4, G, acoef.shape[-1]), F32)
    hf_c, hb_c, h1 = _s5_scan(l, _s5_state(l, xg_c, bmat, tr_c), acoef, h0, B, xg_c.shape[1] // B)
    hf_x, hb_x, _ = _s5_scan(l, _s5_state(l, xg_x, bmat, tr_x), acoef, h1, B, tc_x)
    oc_x = _s5_scatter_finish(l, _s5_out(l, xg_x, hf_x, hb_x, w, tr_x), w_glu, b_glu, tc_x)
    oc_c = None
    if need_ctx:
        y_c = _from_groups(_s5_out(l, xg_c, hf_c, hb_c, w, tr_c), B)
        oc_c = _s5_finish(l, y_c, w_glu, b_glu, tm_c)
    return oc_c, oc_x


def _mla_prep_kernel(*refs, rope, aliased):
    if aliased:
        refs = refs[:-5] + refs[-3:]
    if rope:
        pd_ref, qn_ref, kvn_ref, wq_ref, wqs_ref, wk_ref, wks_ref, wv_ref, cos_ref, sin_ref, q_ref, k_ref, v_ref = refs
    else:
        pd_ref, qn_ref, kvn_ref, wq_ref, wk_ref, wv_ref, q_ref, k_ref, v_ref = refs
    cq = pd_ref[:, 0:256]
    ms = jnp.sum(cq * cq, axis=-1, keepdims=True) * (1.0 / MLA_Q_RANK)
    cqn = (cq * lax.rsqrt(ms + EPS) * qn_ref[...]).astype(BF16)
    ck = pd_ref[:, 256:384]
    lane = lax.broadcasted_iota(jnp.int32, ck.shape, 1)
    is_lat = lane < MLA_KV_RANK
    ms = jnp.sum(jnp.where(is_lat, ck * ck, 0.0), axis=-1, keepdims=True) * (1.0 / MLA_KV_RANK)
    ckn = jnp.where(is_lat, ck * lax.rsqrt(ms + EPS) * kvn_ref[...], ck).astype(BF16)
    q = _dot(cqn, wq_ref[...])
    k = _dot(ckn, wk_ref[...])
    if rope:
        cos = jnp.concatenate([cos_ref[...]] * MLA_HEADS, axis=1)
        sin = jnp.concatenate([sin_ref[...]] * MLA_HEADS, axis=1)
        q = q * cos + _dot(cqn, wqs_ref[...]) * sin
        k = k * cos + _dot(ckn, wks_ref[...]) * sin
    q_ref[...] = (q * ((MLA_NOPE + MLA_ROPE) ** -0.5 * LOG2E)).astype(BF16)
    k_ref[...] = k.astype(BF16)
    v_ref[...] = _dot(ckn, wv_ref[...]).astype(BF16)


def _mla_prep(l, pd, qn, kvn, wts, tables, tm, n, B, row0, nk, kv_into=None):
    R = pd.shape[0]
    HP = MLA_HEADS * MLA_HEAD_PAD
    wq, wqs, wk, wks, wv = wts
    rope = tables is not None
    npt = n // tm
    wspec = lambda r: pl.BlockSpec((None, r, HP), lambda i: (l, 0, 0))
    in_specs = [pl.BlockSpec((tm, 384), lambda i: (i, 0)), _vec_spec(l, 256), _vec_spec(l, 128)]
    if rope:
        tspec = pl.BlockSpec((tm, MLA_HEAD_PAD), lambda i: (i % npt, 0))
        in_specs += [wspec(256), wspec(256), wspec(128), wspec(128), wspec(128), tspec, tspec]
        args = (pd, qn, kvn, wq, wqs, wk, wks, wv) + tuple(tables)
    else:
        in_specs += [wspec(256), wspec(128), wspec(128)]
        args = (pd, qn, kvn, wq, wk, wv)
    aliases = {}
    if kv_into is not None:
        aliases = {len(args): 1, len(args) + 1: 2}
        in_specs += [pl.BlockSpec(memory_space=pl.ANY)] * 2
        args = args + tuple(kv_into)
    kv_spec = pl.BlockSpec((None, tm, HP), lambda i: (i // npt, row0 // tm + i % npt, 0))
    kv_shape = jax.ShapeDtypeStruct((B, nk, HP), BF16)
    return pl.pallas_call(
        functools.partial(_mla_prep_kernel, rope=rope, aliased=kv_into is not None),
        grid=(R // tm,),
        in_specs=in_specs,
        out_specs=[pl.BlockSpec((tm, HP), lambda i: (i, 0)), kv_spec, kv_spec],
        out_shape=[jax.ShapeDtypeStruct((R, HP), BF16), kv_shape, kv_shape],
        input_output_aliases=aliases,
        compiler_params=_cparams("parallel"),
        name="mla_prep",
    )(*args)


def _attn_kernel(q_ref, k_ref, v_ref, o_ref, m_scr, l_scr, acc_scr):
    kv = pl.program_id(2)
    HP = MLA_HEAD_PAD
    tk = k_ref.shape[0]

    @pl.when(kv == 0)
    def _():
        m_scr[...] = jnp.full(m_scr.shape, -jnp.inf, F32)
        l_scr[...] = jnp.zeros(l_scr.shape, F32)
        acc_scr[...] = jnp.zeros(acc_scr.shape, F32)

    def scores(h):
        lanes = slice(h * HP, (h + 1) * HP)
        return _dot_nt(q_ref[:, lanes], k_ref[:, lanes])

    s_next = scores(0)
    for h in range(MLA_HEADS):
        lanes = slice(h * HP, (h + 1) * HP)
        s = s_next
        if h + 1 < MLA_HEADS:
            s_next = scores(h + 1)
        m_prev = m_scr[h]
        m_new = jnp.maximum(m_prev, jnp.max(s, axis=1, keepdims=True))
        alpha = jnp.exp2(m_prev - m_new)
        p = jnp.exp2(s - m_new[:, 0:1])
        lp = p[:, 0:128]
        for c in range(1, tk // 128):
            lp = lp + p[:, c * 128:(c + 1) * 128]
        l_scr[h] = alpha * l_scr[h] + lp
        acc_scr[h] = alpha * acc_scr[h] + _dot(p.astype(BF16), v_ref[:, lanes])
        m_scr[h] = m_new

    @pl.when(kv == pl.num_programs(2) - 1)
    def _():
        outs = [acc_scr[h][:, 0:MLA_V] / jnp.sum(l_scr[h], axis=1, keepdims=True) for h in range(MLA_HEADS)]
        o_ref[...] = jnp.concatenate(outs, axis=1).astype(o_ref.dtype)


def _attention(q3, k3, v3, tq, tk, k0, nk):
    B, nq, HP = q3.shape
    kb = k0 // tk
    return pl.pallas_call(
        _attn_kernel,
        grid=(B, nq // tq, nk // tk),
        in_specs=[pl.BlockSpec((None, tq, HP), lambda b, i, j: (b, i, 0)),
                  pl.BlockSpec((None, tk, HP), lambda b, i, j: (b, kb + j, 0)),
                  pl.BlockSpec((None, tk, HP), lambda b, i, j: (b, kb + j, 0))],
        out_specs=pl.BlockSpec((None, tq, MLA_HEADS * MLA_V), lambda b, i, j: (b, i, 0)),
        out_shape=jax.ShapeDtypeStruct((B, nq, MLA_HEADS * MLA_V), BF16),
        scratch_shapes=[pltpu.VMEM((MLA_HEADS, tq, 128), F32),
                        pltpu.VMEM((MLA_HEADS, tq, 128), F32),
                        pltpu.VMEM((MLA_HEADS, tq, MLA_HEAD_PAD), F32)],
        compiler_params=_cparams("parallel", "parallel", "arbitrary"),
        name="attention",
    )(q3, k3, v3)


def _outproj_kernel(oa, ob, oc, od, w_ref, x_ref, gpost_ref, gt_ref, gpre_ref, sc_ref, sh_ref, xo_ref, h_ref):
    W = GROUP_W
    mix = _dot(oa[...], w_ref[0:W, :])
    mix = mix + _dot(ob[...], w_ref[W:2 * W, :])
    mix = mix + _dot(oc[...], w_ref[2 * W:3 * W, :])
    mix = mix + _dot(od[...], w_ref[3 * W:4 * W, :])
    x = x_ref[...] + gt_ref[...] * (_rms(mix) * gpost_ref[...])
    xo_ref[...] = x
    h_ref[...] = (_rms(x) * gpre_ref[...] * (1.0 + sc_ref[...]) + sh_ref[...]).astype(h_ref.dtype)


def _outproj(l, parts, w_out, x2, mod4, brow, g_post, g_pre_ffn, tm):
    R, D = x2.shape
    W = GROUP_W
    part_spec = pl.BlockSpec((tm, W), lambda i: (i, 0))
    row_spec = pl.BlockSpec((tm, D), lambda i: (i, 0))
    return pl.pallas_call(
        _outproj_kernel,
        grid=(R // tm,),
        in_specs=[part_spec] * 4 + [
            pl.BlockSpec((None, 4 * W, D), lambda i: (l, 0, 0)),
            row_spec,
            _vec_spec(l, D),
            _mod_spec(l, 2, D, brow),
            _vec_spec(l, D),
            _mod_spec(l, 4, D, brow),
            _mod_spec(l, 3, D, brow),
        ],
        out_specs=[row_spec, row_spec],
        out_shape=[jax.ShapeDtypeStruct((R, D), F32), jax.ShapeDtypeStruct((R, D), BF16)],
        compiler_params=_cparams("parallel"),
        name="outproj",
    )(*parts, w_out, x2, g_post, mod4, g_pre_ffn, mod4, mod4)


FFN_HALO = 16
FFN_COLS = 256


def _ffn_kernel(hp_ref, h_ref, hn_ref, wup_ref, cw_ref, cb_ref, wdn_ref, x_ref, gpost_ref, gt_ref, o_ref, y_scr):
    i = pl.program_id(1)
    tm = h_ref.shape[0]
    dff = wdn_ref.shape[0]
    rows = tm + 2 * FFN_HALO
    prev = jnp.where(i == 0, jnp.zeros_like(hp_ref[...]), hp_ref[...])
    nxt = jnp.where(i == pl.num_programs(1) - 1, jnp.zeros_like(hn_ref[...]), hn_ref[...])
    hb = jnp.concatenate([prev, h_ref[...], nxt], axis=0)

    def conv(z, cols):
        w = cw_ref[:, cols]
        out = cb_ref[:, cols] + w[0:1] * pltpu.roll(z, 1, axis=0) + w[1:2] * z + w[2:3] * pltpu.roll(z, rows - 1, axis=0)
        return out[FFN_HALO:FFN_HALO + tm]

    def cols(j):
        return slice(j * FFN_COLS, (j + 1) * FFN_COLS), slice(dff + j * FFN_COLS, dff + (j + 1) * FFN_COLS)

    def up(j):
        ca, cg = cols(j)
        return _dot(hb, wup_ref[:, ca]), _dot(hb, wup_ref[:, cg])

    nchunks = dff // FFN_COLS
    z_next = up(0)
    for j in range(nchunks):
        ca, cg = cols(j)
        za, zg = z_next
        if j + 1 < nchunks:
            z_next = up(j + 1)
        y_scr[:, ca] = _gelu_gate(conv(za, ca), conv(zg, cg)).astype(BF16)
    acc = _dot(y_scr[...], wdn_ref[...])
    o_ref[...] = x_ref[...] + gt_ref[...] * (_rms(acc) * gpost_ref[...])


def _ffn(l, h3, x3, w_up, conv_w, conv_b, w_down, mod4, brow, g_post, tm):
    B, n, D = x3.shape
    nt = n // tm
    hb = tm // FFN_HALO
    nh = n // FFN_HALO
    dff = w_down.shape[1]
    once = pl.Buffered(1)
    return pl.pallas_call(
        _ffn_kernel,
        grid=(B, nt),
        in_specs=[
            pl.BlockSpec((None, FFN_HALO, D), lambda b, i: (b, jnp.maximum(i * hb - 1, 0), 0)),
            pl.BlockSpec((None, tm, D), lambda b, i: (b, i, 0)),
            pl.BlockSpec((None, FFN_HALO, D), lambda b, i: (b, jnp.minimum((i + 1) * hb, nh - 1), 0)),
            pl.BlockSpec((None, D, 2 * dff), lambda b, i: (l, 0, 0), pipeline_mode=once),
            pl.BlockSpec((None, 3, 2 * dff), lambda b, i: (l, 0, 0)),
            pl.BlockSpec((None, 1, 2 * dff), lambda b, i: (l, 0, 0)),
            pl.BlockSpec((None, dff, D), lambda b, i: (l, 0, 0), pipeline_mode=once),
            pl.BlockSpec((None, tm, D), lambda b, i: (b, i, 0)),
            _vec_spec(l, D),
            _mod_spec(l, 5, D, brow),
        ],
        out_specs=pl.BlockSpec((None, tm, D), lambda b, i: (b, i, 0)),
        out_shape=jax.ShapeDtypeStruct((B, n, D), F32),
        scratch_shapes=[pltpu.VMEM((tm, dff), BF16)],
        compiler_params=_cparams("parallel", "parallel"),
        name="conv_ffn",
    )(h3, h3, h3, w_up, conv_w, conv_b, w_down, x3, g_post, mod4)


def _prep_w_in(w_in):
    z = lambda n: jnp.zeros(w_in.shape[:-1] + (n,), w_in.dtype)
    a = w_in[..., 0:512]
    b = w_in[..., 512:1536]
    gl = w_in[..., 1536:1568]
    c = w_in[..., 1568:1824]
    cq = w_in[..., 1824:2048]
    ckv_kr = w_in[..., 2048:2176]
    return jnp.concatenate([a, b, c, cq, z(32), ckv_kr, gl, z(96)], axis=-1).astype(BF16)


def _rope_swap(t):
    q = MLA_ROPE // 4
    return jnp.concatenate([t[..., q:2 * q], t[..., 0:q], t[..., 3 * q:4 * q], t[..., 2 * q:3 * q]], axis=-1)


def _prep_mla(w_uq, w_ukv):
    L = w_uq.shape[0]
    H, NP, RP, HP = MLA_HEADS, MLA_NOPE, MLA_ROPE, MLA_HEAD_PAD
    wq = w_uq.reshape(L, MLA_Q_RANK, H, NP + RP)
    zq = jnp.zeros((L, MLA_Q_RANK, H, HP - NP - RP), w_uq.dtype)
    znope = jnp.zeros((L, MLA_Q_RANK, H, NP), w_uq.dtype)
    q_main = jnp.concatenate([wq, zq], axis=-1)
    q_swap = jnp.concatenate([znope, _rope_swap(wq[..., NP:]), zq], axis=-1)
    padq = lambda w: jnp.pad(w.reshape(L, MLA_Q_RANK, H * HP), ((0, 0), (0, 256 - MLA_Q_RANK), (0, 0)))

    wkv = w_ukv.reshape(L, MLA_KV_RANK, H, NP + MLA_V)
    zk = jnp.zeros((L, MLA_KV_RANK, H, HP - NP), w_ukv.dtype)
    k_lat = jnp.concatenate([wkv[..., :NP], zk], axis=-1)
    eye = jnp.eye(RP, dtype=w_ukv.dtype)
    place = lambda e: jnp.broadcast_to(
        jnp.concatenate([jnp.zeros((RP, NP), e.dtype), e, jnp.zeros((RP, HP - NP - RP), e.dtype)], axis=-1)[None, :, None, :],
        (L, RP, H, HP))
    k_main = jnp.concatenate([k_lat, place(eye)], axis=1)
    k_swap = jnp.concatenate([jnp.zeros_like(k_lat), place(_rope_swap(eye))], axis=1)
    v_lat = jnp.concatenate([wkv[..., NP:], jnp.zeros((L, MLA_KV_RANK, H, HP - MLA_V), w_ukv.dtype)], axis=-1)
    v_main = jnp.concatenate([v_lat, jnp.zeros((L, RP, H, HP), w_ukv.dtype)], axis=1)
    flat = lambda w: w.reshape(L, w.shape[1], H * HP).astype(BF16)
    return (padq(q_main).astype(BF16), padq(q_swap).astype(BF16), flat(k_main), flat(k_swap), flat(v_main))


def _rope_tables(n):
    rows = n // GRID_W
    nf = MLA_ROPE // 4
    inv = ROPE_BASE ** (-jnp.arange(nf, dtype=F32) / nf)
    ar = jnp.arange(rows, dtype=F32)[:, None] * inv[None, :]
    ac = jnp.arange(GRID_W, dtype=F32)[:, None] * inv[None, :]
    by_row = lambda t: jnp.repeat(t, GRID_W, axis=0)
    by_col = lambda t: jnp.tile(t, (rows, 1))
    cr, sr, cc, sn = by_row(jnp.cos(ar)), by_row(jnp.sin(ar)), by_col(jnp.cos(ac)), by_col(jnp.sin(ac))
    one = jnp.ones((n, MLA_NOPE), F32)
    zero = jnp.zeros((n, MLA_HEAD_PAD - MLA_NOPE - MLA_ROPE), F32)
    cos = jnp.concatenate([one, cr, cr, cc, cc, zero], axis=1)
    sin = jnp.concatenate([0.0 * one, -sr, sr, -sn, sn, zero], axis=1)
    return cos, sin


def _pick_tile(n, want):
    t = min(n, want)
    while n % t:
        t //= 2
    return t


def kernel(x, c, ctx, c_ctx, w_mod, b_mod, g_pre_mix, g_post_mix, g_pre_ffn, g_post_ffn, w_in,
           sgu_norm, sgu_w, sgu_b, gla_w_gate, gla_b_gate, gla_norm,
           s5_a_re, s5_a_im, s5_log_dt, s5_b_re, s5_b_im, s5_c_re, s5_c_im, s5_d, s5_w_glu, s5_b_glu,
           mla_q_norm, mla_w_uq, mla_kv_norm, mla_w_ukv, w_out,
           ffn_w_up, ffn_conv_w, ffn_conv_b, ffn_w_down):
    B, n, D = x.shape
    nctx = ctx.shape[1]
    L = w_mod.shape[0]
    W = GROUP_W
    assert B < 8 and n % 512 == 0 and nctx % 128 == 0 and n % GRID_W == 0

    c8 = jnp.concatenate([c, c_ctx[None, :], jnp.zeros((8 - B - 1, D), F32)], axis=0)
    mod4 = _modulation(c8, w_mod, b_mod).reshape(L, 8, 1, 6 * D)
    vec = lambda p: p.reshape(L, 1, -1).astype(F32)
    g_pre_mix, g_post_mix, g_pre_ffn, g_post_ffn = map(vec, (g_pre_mix, g_post_mix, g_pre_ffn, g_post_ffn))
    w_in_p = _prep_w_in(w_in)
    sgu_gn = vec(sgu_norm)
    sgu_w_st = sgu_w.reshape(L, -1, MLP_CHUNK).astype(BF16)
    sgu_bias = jnp.repeat(jnp.swapaxes(sgu_b, 1, 2), HEAD_D, axis=2).astype(F32)
    ones_bd = jnp.kron(jnp.eye(W // HEAD_D, dtype=F32), jnp.ones((HEAD_D, HEAD_D), F32)).astype(BF16)
    gla_wg = jnp.zeros((L, 2, 128, W), F32)
    gla_wg = gla_wg.at[:, 0, 0:GATE_RANK].set(gla_w_gate[:, 0]).at[:, 1, GATE_RANK:2 * GATE_RANK].set(gla_w_gate[:, 1])
    gla_bg = gla_b_gate.reshape(L, 2, 1, W).astype(F32)
    gla_gn = vec(gla_norm)
    s5_wglu = s5_w_glu.astype(BF16)
    s5_bglu = vec(s5_b_glu)
    mla_qn = jnp.pad(mla_q_norm, ((0, 0), (0, 256 - MLA_Q_RANK))).reshape(L, 1, 256).astype(F32)
    mla_kvn = jnp.pad(mla_kv_norm, ((0, 0), (0, 128 - MLA_KV_RANK))).reshape(L, 1, 128).astype(F32)
    mla_wts = _prep_mla(mla_w_uq, mla_w_ukv)
    tables = _rope_tables(n)
    w_out_b = w_out.astype(BF16)
    w_up_b = ffn_w_up.astype(BF16)
    w_down_b = ffn_w_down.astype(BF16)
    dff = ffn_w_down.shape[1]
    half_gate = jnp.concatenate([jnp.ones((dff,), F32), jnp.full((dff,), 0.5, F32)])
    conv_w = ffn_conv_w.astype(F32) * half_gate
    conv_b = (ffn_conv_b.astype(F32) * half_gate).reshape(L, 1, -1)

    s5_mats = jax.vmap(_s5_matrices)(s5_a_re, s5_a_im, s5_log_dt, s5_b_re, s5_b_im, s5_c_re, s5_c_im, s5_d)

    tm_x = _pick_tile(n, 512)
    tm_c = _pick_tile(nctx, 512)
    tpb_x = n // tm_x
    brow_x = lambda i: i // tpb_x
    brow_c = lambda *g: B
    brow_x2 = lambda b, i: b

    xs = x.reshape(B * n, D)
    cs = ctx.reshape(B * nctx, D)

    for l in range(L):
        need_ctx = l < L - 1
        sgu = (sgu_gn, sgu_w_st, sgu_bias, ones_bd)
        oa_x, pb_x, pc_x, pd_x, pg_x = _inproj(l, xs, mod4, brow_x, g_pre_mix, w_in_p, sgu, tm_x)
        oa_c, pb_c, pc_c, pd_c, pg_c = _inproj(l, cs, mod4, brow_c, g_pre_mix, w_in_p, sgu, tm_c)

        r3 = lambda t, m: t.reshape(B, m, t.shape[-1])
        ob_c, ob_x = _gla(l, r3(pb_c, nctx), r3(pg_c, nctx), r3(pb_x, n), r3(pg_x, n),
                          gla_wg, gla_bg, gla_gn, ones_bd, _pick_tile(nctx, GLA_TILE), _pick_tile(n, 4 * GLA_TILE))
        oc_c, oc_x = _s5(l, r3(pc_c, nctx), r3(pc_x, n), s5_mats, s5_wglu, s5_bglu, need_ctx, tm_c)
        wq, wqs, wk, wks, wv = mla_wts
        nk = n + nctx
        q_x, k_all, v_all = _mla_prep(l, pd_x, mla_qn, mla_kvn, mla_wts, tables, tm_x, n, B, 0, nk)
        q_c, k_all, v_all = _mla_prep(l, pd_c, mla_qn, mla_kvn, (wq, None, wk, None, wv), None, tm_c, nctx, B, n, nk,
                                      kv_into=(k_all, v_all))
        tk = next(t for t in (2816, 1408, 768, 512, 256, 128) if nk % t == 0)
        od_x = _attention(r3(q_x, n), k_all, v_all, tm_x, tk, 0, nk).reshape(B * n, W)

        xs, hx = _outproj(l, (oa_x, ob_x.reshape(B * n, W), oc_x, od_x), w_out_b, xs, mod4, brow_x,
                          g_post_mix, g_pre_ffn, tm_x)
        xs = _ffn(l, hx.reshape(B, n, D), xs.reshape(B, n, D), w_up_b, conv_w, conv_b, w_down_b, mod4, brow_x2,
                  g_post_ffn, tm_x).reshape(B * n, D)

        if need_ctx:
            od_c = _attention(r3(q_c, nctx), k_all, v_all, tm_c, nctx, n, nctx).reshape(B * nctx, W)
            cs, hc = _outproj(l, (oa_c, ob_c.reshape(B * nctx, W), oc_c, od_c), w_out_b, cs, mod4, brow_c,
                              g_post_mix, g_pre_ffn, tm_c)
            cs = _ffn(l, hc.reshape(B, nctx, D), cs.reshape(B, nctx, D), w_up_b, conv_w, conv_b, w_down_b, mod4,
                      brow_c, g_post_ffn, tm_c).reshape(B * nctx, D)
    return xs.reshape(B, n, D)
```

```python
import functools

import jax
import jax.numpy as jnp
from jax import lax
from jax.experimental import pallas as pl
from jax.experimental.pallas import tpu as pltpu

F32 = jnp.float32
BF16 = jnp.bfloat16

EPS = 1e-6
GRID_W = 64
GROUP_W = 256
HEAD_D = 64
MLP_CHUNK = 128
GATE_RANK = 16
GATE_TEMP = 16.0
GLA_CHUNK = 64
GLA_TILE = 256
S5_IN = 16
S5_T = 16
S5_TILE = 128
MLA_HEADS = 4
MLA_NOPE = 64
MLA_ROPE = 32
MLA_V = 64
MLA_Q_RANK = 224
MLA_KV_RANK = 96
MLA_HEAD_PAD = 128
ROPE_BASE = 10000.0
LOG2E = 1.4426950408889634

VMEM_LIMIT = 48 * 1024 * 1024


def _cparams(*sem):
    return pltpu.CompilerParams(dimension_semantics=sem, vmem_limit_bytes=VMEM_LIMIT)


def _dot(a, b):
    return jnp.dot(a, b, preferred_element_type=F32)


def _dot_nt(a, b):
    return lax.dot_general(a, b, (((1,), (1,)), ((), ())), preferred_element_type=F32)


def _dot_tn(a, b):
    return lax.dot_general(a, b, (((0,), (0,)), ((), ())), preferred_element_type=F32)


def _split(a):
    hi = a.astype(BF16)
    lo = (a - hi.astype(F32)).astype(BF16)
    return hi, lo


def _dot_x2(a, b_bf16):
    hi, lo = _split(a)
    return _dot(hi, b_bf16) + _dot(lo, b_bf16)


def _dot_x3(a, b):
    ah, al = _split(a)
    bh, bl = _split(b)
    return _dot(ah, bh) + _dot(al, bh) + _dot(ah, bl)


def _rms(x):
    return x * lax.rsqrt(jnp.mean(x * x, axis=-1, keepdims=True) + EPS)


def _gelu(x):
    return 0.5 * x * (1.0 + jnp.tanh(0.7978845608028654 * (x + 0.044715 * (x * x * x))))


def _gelu_gate(a, half_g):
    u = a * (0.7978845608028654 + 0.035677408136300125 * (a * a))
    return (a * half_g) * (1.0 + jnp.tanh(u))


def _sigmoid(x):
    return 1.0 / (1.0 + jnp.exp(-x))


def _lane_group(shape, width):
    return lax.broadcasted_iota(jnp.int32, shape, len(shape) - 1) // width


def _mod_kernel(c_ref, w_ref, b_ref, o_ref):
    c = c_ref[...]
    s = c * _sigmoid(c)
    o_ref[...] = _dot_x3(s, w_ref[...]) + b_ref[...]


def _modulation(c8, w_mod, b_mod):
    L, D, W = w_mod.shape
    tn = 1536
    return pl.pallas_call(
        _mod_kernel,
        grid=(L, W // tn),
        in_specs=[
            pl.BlockSpec((8, D), lambda l, j: (0, 0)),
            pl.BlockSpec((None, D, tn), lambda l, j: (l, 0, j)),
            pl.BlockSpec((None, 1, tn), lambda l, j: (l, 0, j)),
        ],
        out_specs=pl.BlockSpec((None, 8, tn), lambda l, j: (l, 0, j)),
        out_shape=jax.ShapeDtypeStruct((L, 8, W), F32),
        compiler_params=_cparams("arbitrary", "arbitrary"),
        name="modulation",
    )(c8, w_mod, b_mod.reshape(L, 1, W))


def _mod_spec(l, j, D, bfn):
    return pl.BlockSpec((None, None, 1, D), lambda *g: (l, bfn(*g), 0, j))


def _vec_spec(l, width):
    return pl.BlockSpec((None, 1, width), lambda *g: (l, 0, 0))


IN_SLABS = (("a", 0, 512), ("b", 512, 1024), ("c", 1536, 256), ("d", 1792, 384), ("g", 2176, 128))
IN_PAD_COLS = 2304


def _sgu_tile(p, gn_ref, w_ref, b_ref, ones_ref, o_ref):
    tm = p.shape[0]
    g = _gelu(p)
    u = g[:, :GROUP_W]
    v = g[:, GROUP_W:]
    ms = _dot_x2(v * v, ones_ref[...]) * (1.0 / HEAD_D)
    vb = (v * lax.rsqrt(ms + EPS) * gn_ref[...]).astype(BF16)
    head = _lane_group((MLP_CHUNK, GROUP_W), HEAD_D)
    w = w_ref[...]
    for c in range(tm // MLP_CHUNK):
        rows = slice(c * MLP_CHUNK, (c + 1) * MLP_CHUNK)
        r = _dot(w, vb[rows])
        s = b_ref[...]
        for h in range(GROUP_W // HEAD_D):
            s = s + jnp.where(head == h, r[h * MLP_CHUNK:(h + 1) * MLP_CHUNK], 0.0)
        o_ref[rows, :] = (u[rows] * s).astype(o_ref.dtype)


def _inproj_kernel(x_ref, g_ref, sc_ref, sh_ref, w_ref, gn_ref, wsp_ref, bsp_ref, ones_ref, oa, ob, oc, od, og):
    h = _rms(x_ref[...]) * g_ref[...] * (1.0 + sc_ref[...]) + sh_ref[...]
    hb = h.astype(BF16)
    (_, off_a, width_a) = IN_SLABS[0]
    pa = _dot(hb, w_ref[:, off_a:off_a + width_a])
    for (_, off, width), o_ref in zip(IN_SLABS[1:], (ob, oc, od, og)):
        o_ref[...] = _dot(hb, w_ref[:, off:off + width])
    _sgu_tile(pa, gn_ref, wsp_ref, bsp_ref, ones_ref, oa)


def _inproj(l, x2, mod4, brow, g_pre, w_in_p, sgu, tm):
    R, D = x2.shape
    gn, w_st, bias, ones_bd = sgu
    H = GROUP_W // HEAD_D
    grid = (R // tm,)
    outs = [jax.ShapeDtypeStruct((R, GROUP_W), BF16)] + [jax.ShapeDtypeStruct((R, width), F32) for (_, _, width) in IN_SLABS[1:]]
    return pl.pallas_call(
        _inproj_kernel,
        grid=grid,
        in_specs=[
            pl.BlockSpec((tm, D), lambda i: (i, 0)),
            _vec_spec(l, D),
            _mod_spec(l, 1, D, brow),
            _mod_spec(l, 0, D, brow),
            pl.BlockSpec((None, D, IN_PAD_COLS), lambda i: (l, 0, 0)),
            _vec_spec(l, GROUP_W),
            pl.BlockSpec((None, H * MLP_CHUNK, MLP_CHUNK), lambda i: (l, 0, 0)),
            pl.BlockSpec((None, MLP_CHUNK, GROUP_W), lambda i: (l, 0, 0)),
            pl.BlockSpec((GROUP_W, GROUP_W), lambda i: (0, 0)),
        ],
        out_specs=[pl.BlockSpec((tm, GROUP_W), lambda i: (i, 0))]
        + [pl.BlockSpec((tm, width), lambda i: (i, 0)) for (_, _, width) in IN_SLABS[1:]],
        out_shape=outs,
        compiler_params=_cparams("parallel"),
        name="inproj",
    )(x2, g_pre, mod4, mod4, w_in_p, gn, w_st, bias, ones_bd)


def _gla_kernel(*refs, rev, finish):
    if finish:
        (pb_ref, pg_ref, wg_ref, bg_ref, s0_ref, ones_ref, dec_ref, oprev_ref, gn_ref, o_ref, sfin_ref, st_scr) = refs
    else:
        (pb_ref, pg_ref, wg_ref, bg_ref, s0_ref, ones_ref, dec_ref, o_ref, sfin_ref, st_scr) = refs
    i = pl.program_id(1)
    C, W, T = GLA_CHUNK, GROUP_W, GLA_TILE
    H = W // HEAD_D
    nsub = T // C
    ntile = pb_ref.shape[0] // T

    @pl.when(i == 0)
    def _():
        st_scr[...] = s0_ref[...]

    tri = dec_ref[...]
    mask4 = jnp.concatenate([tri.astype(F32)] * H, axis=0)
    head_t = _lane_group((T, W), HEAD_D)
    bd = (lax.broadcasted_iota(jnp.int32, (W, W), 0) // HEAD_D) == _lane_group((W, W), HEAD_D)
    tiles = list(range(ntile - 1, -1, -1)) if rev else list(range(ntile))
    subs = list(range(nsub - 1, -1, -1)) if rev else list(range(nsub))
    rows = lambda a: slice(a * T, (a + 1) * T)
    sub = lambda j: slice(j * C, (j + 1) * C)
    stack = lambda blocks: jnp.concatenate([blocks[j] for j in range(nsub)], axis=0)

    logg = {}
    for a in tiles:
        z = _dot_x3(pg_ref[rows(a), :], wg_ref[...]) + bg_ref[...]
        logg[a] = (jnp.minimum(z, 0.0) - jnp.log(1.0 + jnp.exp(-jnp.abs(z)))) * (1.0 / GATE_TEMP)
    cums = {}
    for a in tiles:
        g_hi, g_lo = _split(logg[a])
        cums[a] = _dot(tri, g_hi) + _dot(tri, g_lo)
    qin, qoff, kin, kend, vb, tot, ptile = {}, {}, {}, {}, {}, {}, {}
    for a in tiles:
        cum = cums[a]
        q = pb_ref[rows(a), 0:W] * (HEAD_D ** -0.5)
        k = pb_ref[rows(a), W:2 * W]
        last_row = (lambda j: j * C) if rev else (lambda j: j * C + C - 1)
        off, run = {}, jnp.zeros((1, W), F32)
        for j in subs:
            tot[(a, j)] = cum[last_row(j):last_row(j) + 1]
            off[j] = run
            run = run + tot[(a, j)]
        ptile[a] = jnp.exp(run)
        e = jnp.exp(cum)
        qin[a] = (q * e).astype(BF16)
        qoff[a] = (q * (e * jnp.exp(stack({j: jnp.broadcast_to(off[j], (C, W)) for j in subs})))).astype(BF16)
        kin[a] = (k * jnp.exp(-cum)).astype(BF16)
        kend[a] = (k * jnp.exp(stack({j: jnp.broadcast_to(tot[(a, j)], (C, W)) for j in subs}) - cum)).astype(BF16)
        vb[a] = pb_ref[rows(a), 2 * W:3 * W].astype(BF16)
    sc = {}
    for a in tiles:
        qst = jnp.concatenate([jnp.where(head_t == h, qin[a], jnp.zeros_like(qin[a])) for h in range(H)], axis=0)
        sc[a] = (_dot_nt(qst, kin[a]) * mask4).astype(BF16)
    ost = {a: _dot(sc[a], vb[a]) for a in tiles}
    kvt = {(a, j): jnp.where(bd, _dot_tn(vb[a][sub(j)], kend[a][sub(j)]), 0.0) for a in tiles for j in subs}
    cross, rend = {}, {}
    for a in tiles:
        r = None
        for j in subs:
            if r is not None:
                cross[(a, j)] = _dot_nt(qin[a][sub(j)], r.astype(BF16))
                r = r * jnp.exp(tot[(a, j)]) + kvt[(a, j)]
            else:
                cross[(a, j)] = jnp.zeros((C, W), F32)
                r = kvt[(a, j)]
        rend[a] = r
    st = st_scr[...]
    inter = {}
    for a in tiles:
        inter[a] = _dot_nt(qoff[a], st.astype(BF16))
        st = st * ptile[a] + rend[a]
    st_scr[...] = st

    for a in tiles:
        o = inter[a] + jnp.concatenate([cross[(a, j)] for j in range(nsub)], axis=0)
        for h in range(H):
            o = o + jnp.where(head_t == h, ost[a][h * T:(h + 1) * T], 0.0)
        if finish:
            o = o + oprev_ref[rows(a), :]
            ms = _dot_x2(o * o, ones_ref[...]) * (1.0 / HEAD_D)
            o = o * lax.rsqrt(ms + EPS) * gn_ref[...]
            r = pb_ref[rows(a), 3 * W:4 * W]
            o_ref[rows(a), :] = (o * (r * _sigmoid(r))).astype(o_ref.dtype)
        else:
            o_ref[rows(a), :] = o

    @pl.when(i == pl.num_programs(1) - 1)
    def _():
        sfin_ref[...] = st_scr[...]


def _gla_decay_matrices():
    T, C = GLA_TILE, GLA_CHUNK
    t = jnp.arange(T)[:, None]
    s = jnp.arange(T)[None, :]
    same = (t // C) == (s // C)
    return jnp.stack([same & (s <= t), same & (s >= t)]).astype(BF16)


def _gla_pass(l, d, pb3, pg3, wg, bg, s0, ones_bd, dec, oprev, gn, tt):
    B, n, _ = pb3.shape
    nt = n // tt
    rev = d == 1
    finish = oprev is not None
    W = GROUP_W

    def tok(b, i):
        return (b, (nt - 1 - i) if rev else i, 0)

    in_specs = [
        pl.BlockSpec((None, tt, 4 * W), tok),
        pl.BlockSpec((None, tt, 128), tok),
        pl.BlockSpec((None, None, 128, W), lambda b, i: (l, d, 0, 0)),
        pl.BlockSpec((None, None, 1, W), lambda b, i: (l, d, 0, 0)),
        pl.BlockSpec((None, W, W), lambda b, i: (b, 0, 0)),
        pl.BlockSpec((W, W), lambda b, i: (0, 0)),
        pl.BlockSpec((None, GLA_TILE, GLA_TILE), lambda b, i: (d, 0, 0)),
    ]
    args = [pb3, pg3, wg, bg, s0, ones_bd, dec]
    if finish:
        in_specs += [pl.BlockSpec((None, tt, W), tok), _vec_spec(l, W)]
        args += [oprev, gn]
    return pl.pallas_call(
        functools.partial(_gla_kernel, rev=rev, finish=finish),
        grid=(B, nt),
        in_specs=in_specs,
        out_specs=[pl.BlockSpec((None, tt, W), tok), pl.BlockSpec((None, W, W), lambda b, i: (b, 0, 0))],
        out_shape=[jax.ShapeDtypeStruct((B, n, W), BF16 if finish else F32),
                   jax.ShapeDtypeStruct((B, W, W), F32)],
        scratch_shapes=[pltpu.VMEM((W, W), F32)],
        compiler_params=_cparams("arbitrary", "arbitrary"),
        name="gla_bwd" if rev else "gla_fwd",
    )(*args)


def _gla(l, pb_c, pg_c, pb_x, pg_x, wg, bg, gn, ones_bd, tt_c, tt_x):
    B = pb_x.shape[0]
    zero = jnp.zeros((B, GROUP_W, GROUP_W), F32)
    dec = _gla_decay_matrices()
    ofc, sfc = _gla_pass(l, 0, pb_c, pg_c, wg, bg, zero, ones_bd, dec, None, None, tt_c)
    ofx, _ = _gla_pass(l, 0, pb_x, pg_x, wg, bg, sfc, ones_bd, dec, None, None, tt_x)
    ob_c, sbc = _gla_pass(l, 1, pb_c, pg_c, wg, bg, zero, ones_bd, dec, ofc, gn, tt_c)
    ob_x, _ = _gla_pass(l, 1, pb_x, pg_x, wg, bg, sbc, ones_bd, dec, ofx, gn, tt_x)
    return ob_c, ob_x


def _s5_matrices(a_re, a_im, log_dt, b_re, b_im, c_re, c_im, d_skip):
    T = S5_T
    G, P = a_re.shape[1:]
    I = b_re.shape[-1]
    lam = lax.complex(a_re.astype(F32), a_im.astype(F32))
    ldt = lam * jnp.exp(log_dt.astype(F32))[..., None]
    lam_bar = jnp.exp(ldt)
    b_bar = ((lam_bar - 1.0) / lam)[..., None] * lax.complex(b_re.astype(F32), b_im.astype(F32))
    cmat = lax.complex(c_re.astype(F32), c_im.astype(F32))
    steps = jnp.arange(T + 1, dtype=F32)
    pw = jnp.exp(ldt[..., None] * steps)
    taps = jnp.einsum('dgop,dgpk,dgpi->dgiko', cmat, pw[..., :T], b_bar).real
    taps = taps.at[0, :, :, 0, :].add(jnp.eye(I, dtype=F32)[None] * d_skip.astype(F32)[:, :, None])
    row = T * I
    zeros = jnp.zeros((G, I, row), F32)
    fwd = jnp.concatenate([zeros, taps[0].reshape(G, I, row)], axis=-1)
    bwd = jnp.concatenate([jnp.flip(taps[1], axis=2).reshape(G, I, row), zeros], axis=-1)
    m = jnp.stack([fwd[..., (T - s) * I:(T - s) * I + row] + bwd[..., (T - 1 - s) * I:(T - 1 - s) * I + row]
                   for s in range(T)], axis=1).reshape(G, row, row)

    ar = jnp.arange(T)
    pf = pw[0][..., T - 1 - ar]
    pb = pw[1][..., ar]
    bf = jnp.einsum('gps,gpi->gsip', pf, b_bar[0]).reshape(G, T * I, P)
    bb = jnp.einsum('gps,gpi->gsip', pb, b_bar[1]).reshape(G, T * I, P)
    bmat = jnp.concatenate([bf.real, bf.imag, bf.imag, bf.real, bb.real, bb.imag, bb.imag, bb.real], axis=-1)

    cf = jnp.einsum('gop,gpt->gpto', cmat[0], pw[0][..., 1 + ar]).reshape(G, P, T * I)
    cb = jnp.einsum('gop,gpt->gpto', cmat[1], pw[1][..., T - ar]).reshape(G, P, T * I)
    w = jnp.concatenate([m, cf.real, -cf.imag, cb.real, -cb.imag], axis=1)

    a = pw[..., T]
    acoef = jnp.stack([jnp.concatenate([a[0].real, a[0].real], axis=-1), jnp.concatenate([-a[0].imag, a[0].imag], axis=-1),
                       jnp.concatenate([a[1].real, a[1].real], axis=-1), jnp.concatenate([-a[1].imag, a[1].imag], axis=-1)])
    return bmat.astype(BF16), w.astype(BF16), acoef


def _s5_state_kernel(x_ref, b_ref, o_ref):
    o_ref[...] = _dot(x_ref[...], b_ref[...])


def _s5_state(l, xg, bmat, tr):
    G, R, K = xg.shape
    N = bmat.shape[-1]
    return pl.pallas_call(
        _s5_state_kernel,
        grid=(G, R // tr),
        in_specs=[pl.BlockSpec((None, tr, K), lambda g, i: (g, i, 0)),
                  pl.BlockSpec((None, None, K, N), lambda g, i: (l, g, 0, 0))],
        out_specs=pl.BlockSpec((None, tr, N), lambda g, i: (g, i, 0)),
        out_shape=jax.ShapeDtypeStruct((G, R, N), F32),
        compiler_params=_cparams("parallel", "parallel"),
        name="s5_state",
    )(xg, bmat)


def _s5_scan_kernel(sf_ref, sb_ref, a_ref, h0_ref, hf_ref, hb_ref, hfin_ref, st_scr):
    j = pl.program_id(1)
    tc = sf_ref.shape[1]

    @pl.when(j == 0)
    def _():
        st_scr[...] = h0_ref[...]

    a1f, a2f, a1b, a2b = a_ref[0], a_ref[1], a_ref[2], a_ref[3]
    half = a1f.shape[-1]

    def body(i, hs):
        h1f, h2f, h1b, h2b = hs
        ib = tc - 1 - i
        sf = sf_ref[:, i, :]
        sb = sb_ref[:, ib, :]
        hf_ref[:, i, :] = h1f
        hb_ref[:, ib, :] = h1b
        return (h1f * a1f + h2f * a2f + sf[:, 0:half], h2f * a1f - h1f * a2f + sf[:, half:],
                h1b * a1b + h2b * a2b + sb[:, 0:half], h2b * a1b - h1b * a2b + sb[:, half:])

    hs = lax.fori_loop(0, tc, body, (st_scr[0], st_scr[1], st_scr[2], st_scr[3]))
    for k in range(4):
        st_scr[k] = hs[k]

    @pl.when(j == pl.num_programs(1) - 1)
    def _():
        hfin_ref[...] = st_scr[...]


def _s5_scan(l, s, acoef, h0, B, tc):
    G, R, n2 = s.shape
    nt = R // B // tc
    half = n2 // 4
    fwd = lambda b, j: (0, b * nt + j, 0)
    bwd = lambda b, j: (0, b * nt + nt - 1 - j, 0)
    st_spec = pl.BlockSpec((None, 4, G, half), lambda b, j: (b, 0, 0, 0))
    return pl.pallas_call(
        _s5_scan_kernel,
        grid=(B, nt),
        in_specs=[pl.BlockSpec((G, tc, 2 * half), fwd),
                  pl.BlockSpec((G, tc, 2 * half), lambda b, j: (0, b * nt + nt - 1 - j, 1)),
                  pl.BlockSpec((None, 4, G, half), lambda b, j: (l, 0, 0, 0)),
                  st_spec],
        out_specs=[pl.BlockSpec((G, tc, half), fwd), pl.BlockSpec((G, tc, half), bwd), st_spec],
        out_shape=[jax.ShapeDtypeStruct((G, R, half), F32), jax.ShapeDtypeStruct((G, R, half), F32),
                   jax.ShapeDtypeStruct((B, 4, G, half), F32)],
        scratch_shapes=[pltpu.VMEM((4, G, half), F32)],
        compiler_params=_cparams("arbitrary", "arbitrary"),
        name="s5_scan",
    )(s, s, acoef, h0)


def _s5_out_kernel(x_ref, hf_ref, hb_ref, w_ref, o_ref):
    k = x_ref.shape[-1]
    kh = hf_ref.shape[-1]
    y = _dot(x_ref[...], w_ref[0:k, :])
    y = y + _dot(hf_ref[...].astype(BF16), w_ref[k:k + kh, :])
    o_ref[...] = y + _dot(hb_ref[...].astype(BF16), w_ref[k + kh:, :])


def _s5_out(l, xg, hf, hb, w, tr):
    G, R, K = xg.shape
    kh = hf.shape[-1]
    row = lambda width: pl.BlockSpec((None, tr, width), lambda g, i: (g, i, 0))
    return pl.pallas_call(
        _s5_out_kernel,
        grid=(G, R // tr),
        in_specs=[row(K), row(kh), row(kh), pl.BlockSpec((None, None, K + 2 * kh, K), lambda g, i: (l, g, 0, 0))],
        out_specs=row(K),
        out_shape=jax.ShapeDtypeStruct((G, R, K), F32),
        compiler_params=_cparams("parallel", "parallel"),
        name="s5_out",
    )(xg, hf, hb, w)


def _s5_finish_kernel(y_ref, w_ref, b_ref, o_ref):
    y = _gelu(y_ref[...])
    gate = _sigmoid(_dot(y.astype(BF16), w_ref[...]) + b_ref[...])
    o_ref[...] = (y * gate).astype(o_ref.dtype)


def _s5_finish(l, y2, w_glu, b_glu, tm):
    R, W = y2.shape
    return pl.pallas_call(
        _s5_finish_kernel,
        grid=(R // tm,),
        in_specs=[pl.BlockSpec((tm, W), lambda i: (i, 0)),
                  pl.BlockSpec((None, W, W), lambda i: (l, 0, 0)),
                  _vec_spec(l, W)],
        out_specs=pl.BlockSpec((tm, W), lambda i: (i, 0)),
        out_shape=jax.ShapeDtypeStruct((R, W), BF16),
        compiler_params=_cparams("parallel"),
        name="s5_finish",
    )(y2, w_glu, b_glu)


def _to_groups(u3):
    B, n, W = u3.shape
    G = W // S5_IN
    x = u3.astype(BF16).reshape(B, n // S5_T, S5_T, G, S5_IN).transpose(3, 0, 1, 2, 4)
    return x.reshape(G, B * (n // S5_T), S5_T * S5_IN)


def _from_groups(y, B):
    G, R, _ = y.shape
    nc = R // B
    y = y.reshape(G, B, nc, S5_T, S5_IN).transpose(1, 2, 3, 0, 4)
    return y.reshape(B * nc * S5_T, G * S5_IN)


def _s5_gather_kernel(x_ref, o_ref, xs_scr, xt_scr):
    G, tc, _ = o_ref.shape
    nh = x_ref.shape[1] // 128
    gh = G // nh
    for h in range(nh):
        xs_scr[h] = x_ref[:, h * 128:(h + 1) * 128]
    for t in range(S5_T):
        for h in range(nh):
            rt = xs_scr[h, pl.ds(t, tc, stride=S5_T), :].T
            for g in range(gh):
                xt_scr[h * gh + g, t * S5_IN:(t + 1) * S5_IN, :] = rt[g * S5_IN:(g + 1) * S5_IN, :]
    for g in range(G):
        o_ref[g] = xt_scr[g].T.astype(o_ref.dtype)


def _s5_gather(x2, tc):
    N, W = x2.shape
    G = W // S5_IN
    R = N // S5_T
    return pl.pallas_call(
        _s5_gather_kernel,
        grid=(R // tc,),
        in_specs=[pl.BlockSpec((tc * S5_T, W), lambda i: (i, 0))],
        out_specs=pl.BlockSpec((G, tc, S5_T * S5_IN), lambda i: (0, i, 0)),
        out_shape=jax.ShapeDtypeStruct((G, R, S5_T * S5_IN), BF16),
        scratch_shapes=[pltpu.VMEM((W // 128, tc * S5_T, 128), F32), pltpu.VMEM((G, S5_T * S5_IN, tc), F32)],
        compiler_params=_cparams("parallel"),
        name="s5_gather",
    )(x2)


def _s5_scatter_kernel(y_ref, w_ref, b_ref, o_ref, yt_scr, tok_scr):
    G, tc, _ = y_ref.shape
    nh = tok_scr.shape[0]
    gh = G // nh
    for g in range(G):
        yt = y_ref[g].T
        for t in range(S5_T):
            yt_scr[t, g // gh, (g % gh) * S5_IN:(g % gh + 1) * S5_IN, :] = yt[t * S5_IN:(t + 1) * S5_IN, :]
    for t in range(S5_T):
        for h in range(nh):
            tok_scr[h, pl.ds(t, tc, stride=S5_T), :] = yt_scr[t, h].T
    y = _gelu(jnp.concatenate([tok_scr[h] for h in range(nh)], axis=1))
    gate = _sigmoid(_dot(y.astype(BF16), w_ref[...]) + b_ref[...])
    o_ref[...] = (y * gate).astype(o_ref.dtype)


def _s5_scatter_finish(l, yg, w_glu, b_glu, tc):
    G, R, K = yg.shape
    W = G * S5_IN
    return pl.pallas_call(
        _s5_scatter_kernel,
        grid=(R // tc,),
        in_specs=[pl.BlockSpec((G, tc, K), lambda i: (0, i, 0)),
                  pl.BlockSpec((None, W, W), lambda i: (l, 0, 0)),
                  _vec_spec(l, W)],
        out_specs=pl.BlockSpec((tc * S5_T, W), lambda i: (i, 0)),
        out_shape=jax.ShapeDtypeStruct((R * S5_T, W), BF16),
        scratch_shapes=[pltpu.VMEM((S5_T, W // 128, 128, tc), F32), pltpu.VMEM((W // 128, tc * S5_T, 128), F32)],
        compiler_params=_cparams("parallel"),
        name="s5_scatter",
    )(yg, w_glu, b_glu)


def _s5(l, pc_c, pc_x, mats, w_glu, b_glu, need_ctx, tm_c):
    bmat, w, acoef = mats
    B, n, W = pc_x.shape
    G = W // S5_IN
    xg_c = _to_groups(pc_c)
    tc_x = min(S5_TILE, n // S5_T)
    xg_x = _s5_gather(pc_x.reshape(B * n, W), tc_x)
    tr_c = min(512, xg_c.shape[1])
    tr_x = min(512, xg_x.shape[1])
    h0 = jnp.zeros((B, 4, G, acoef.shape[-1]), F32)
    hf_c, hb_c, h1 = _s5_scan(l, _s5_state(l, xg_c, bmat, tr_c), acoef, h0, B, xg_c.shape[1] // B)
    hf_x, hb_x, _ = _s5_scan(l, _s5_state(l, xg_x, bmat, tr_x), acoef, h1, B, tc_x)
    oc_x = _s5_scatter_finish(l, _s5_out(l, xg_x, hf_x, hb_x, w, tr_x), w_glu, b_glu, tc_x)
    oc_c = None
    if need_ctx:
        y_c = _from_groups(_s5_out(l, xg_c, hf_c, hb_c, w, tr_c), B)
        oc_c = _s5_finish(l, y_c, w_glu, b_glu, tm_c)
    return oc_c, oc_x


def _mla_prep_kernel(*refs, rope, aliased):
    if aliased:
        refs = refs[:-5] + refs[-3:]
    if rope:
        pd_ref, qn_ref, kvn_ref, wq_ref, wqs_ref, wk_ref, wks_ref, wv_ref, cos_ref, sin_ref, q_ref, k_ref, v_ref = refs
    else:
        pd_ref, qn_ref, kvn_ref, wq_ref, wk_ref, wv_ref, q_ref, k_ref, v_ref = refs
    cq = pd_ref[:, 0:256]
    ms = jnp.sum(cq * cq, axis=-1, keepdims=True) * (1.0 / MLA_Q_RANK)
    cqn = (cq * lax.rsqrt(ms + EPS) * qn_ref[...]).astype(BF16)
    ck = pd_ref[:, 256:384]
    lane = lax.broadcasted_iota(jnp.int32, ck.shape, 1)
    is_lat = lane < MLA_KV_RANK
    ms = jnp.sum(jnp.where(is_lat, ck * ck, 0.0), axis=-1, keepdims=True) * (1.0 / MLA_KV_RANK)
    ckn = jnp.where(is_lat, ck * lax.rsqrt(ms + EPS) * kvn_ref[...], ck).astype(BF16)
    q = _dot(cqn, wq_ref[...])
    k = _dot(ckn, wk_ref[...])
    if rope:
        cos = jnp.concatenate([cos_ref[...]] * MLA_HEADS, axis=1)
        sin = jnp.concatenate([sin_ref[...]] * MLA_HEADS, axis=1)
        q = q * cos + _dot(cqn, wqs_ref[...]) * sin
        k = k * cos + _dot(ckn, wks_ref[...]) * sin
    q_ref[...] = (q * ((MLA_NOPE + MLA_ROPE) ** -0.5 * LOG2E)).astype(BF16)
    k_ref[...] = k.astype(BF16)
    v_ref[...] = _dot(ckn, wv_ref[...]).astype(BF16)


def _mla_prep(l, pd, qn, kvn, wts, tables, tm, n, B, row0, nk, kv_into=None):
    R = pd.shape[0]
    HP = MLA_HEADS * MLA_HEAD_PAD
    wq, wqs, wk, wks, wv = wts
    rope = tables is not None
    npt = n // tm
    wspec = lambda r: pl.BlockSpec((None, r, HP), lambda i: (l, 0, 0))
    in_specs = [pl.BlockSpec((tm, 384), lambda i: (i, 0)), _vec_spec(l, 256), _vec_spec(l, 128)]
    if rope:
        tspec = pl.BlockSpec((tm, MLA_HEAD_PAD), lambda i: (i % npt, 0))
        in_specs += [wspec(256), wspec(256), wspec(128), wspec(128), wspec(128), tspec, tspec]
        args = (pd, qn, kvn, wq, wqs, wk, wks, wv) + tuple(tables)
    else:
        in_specs += [wspec(256), wspec(128), wspec(128)]
        args = (pd, qn, kvn, wq, wk, wv)
    aliases = {}
    if kv_into is not None:
        aliases = {len(args): 1, len(args) + 1: 2}
        in_specs += [pl.BlockSpec(memory_space=pl.ANY)] * 2
        args = args + tuple(kv_into)
    kv_spec = pl.BlockSpec((None, tm, HP), lambda i: (i // npt, row0 // tm + i % npt, 0))
    kv_shape = jax.ShapeDtypeStruct((B, nk, HP), BF16)
    return pl.pallas_call(
        functools.partial(_mla_prep_kernel, rope=rope, aliased=kv_into is not None),
        grid=(R // tm,),
        in_specs=in_specs,
        out_specs=[pl.BlockSpec((tm, HP), lambda i: (i, 0)), kv_spec, kv_spec],
        out_shape=[jax.ShapeDtypeStruct((R, HP), BF16), kv_shape, kv_shape],
        input_output_aliases=aliases,
        compiler_params=_cparams("parallel"),
        name="mla_prep",
    )(*args)


def _attn_kernel(q_ref, k_ref, v_ref, o_ref, m_scr, l_scr, acc_scr):
    kv = pl.program_id(2)
    HP = MLA_HEAD_PAD
    tk = k_ref.shape[0]

    @pl.when(kv == 0)
    def _():
        m_scr[...] = jnp.full(m_scr.shape, -jnp.inf, F32)
        l_scr[...] = jnp.zeros(l_scr.shape, F32)
        acc_scr[...] = jnp.zeros(acc_scr.shape, F32)

    def scores(h):
        lanes = slice(h * HP, (h + 1) * HP)
        return _dot_nt(q_ref[:, lanes], k_ref[:, lanes])

    s_next = scores(0)
    for h in range(MLA_HEADS):
        lanes = slice(h * HP, (h + 1) * HP)
        s = s_next
        if h + 1 < MLA_HEADS:
            s_next = scores(h + 1)
        m_prev = m_scr[h]
        m_new = jnp.maximum(m_prev, jnp.max(s, axis=1, keepdims=True))
        alpha = jnp.exp2(m_prev - m_new)
        p = jnp.exp2(s - m_new[:, 0:1])
        lp = p[:, 0:128]
        for c in range(1, tk // 128):
            lp = lp + p[:, c * 128:(c + 1) * 128]
        l_scr[h] = alpha * l_scr[h] + lp
        acc_scr[h] = alpha * acc_scr[h] + _dot(p.astype(BF16), v_ref[:, lanes])
        m_scr[h] = m_new

    @pl.when(kv == pl.num_programs(2) - 1)
    def _():
        low = lax.broadcasted_iota(jnp.int32, acc_scr.shape[1:], 1) < MLA_V
        norm = lambda h: acc_scr[h] * (1.0 / jnp.sum(l_scr[h], axis=1, keepdims=True))
        outs = [jnp.where(low, norm(h), norm(h + 1)) for h in range(0, MLA_HEADS, 2)]
        o_ref[...] = jnp.concatenate(outs, axis=1).astype(o_ref.dtype)


def _attention(q3, k3, v3, tq, tk, k0, nk):
    B, nq, HP = q3.shape
    kb = k0 // tk
    return pl.pallas_call(
        _attn_kernel,
        grid=(B, nq // tq, nk // tk),
        in_specs=[pl.BlockSpec((None, tq, HP), lambda b, i, j: (b, i, 0)),
                  pl.BlockSpec((None, tk, HP), lambda b, i, j: (b, kb + j, 0)),
                  pl.BlockSpec((None, tk, HP), lambda b, i, j: (b, kb + j, 0))],
        out_specs=pl.BlockSpec((None, tq, MLA_HEADS * MLA_V), lambda b, i, j: (b, i, 0)),
        out_shape=jax.ShapeDtypeStruct((B, nq, MLA_HEADS * MLA_V), BF16),
        scratch_shapes=[pltpu.VMEM((MLA_HEADS, tq, 128), F32),
                        pltpu.VMEM((MLA_HEADS, tq, 128), F32),
                        pltpu.VMEM((MLA_HEADS, tq, MLA_HEAD_PAD), F32)],
        compiler_params=_cparams("parallel", "parallel", "arbitrary"),
        name="attention",
    )(q3, k3, v3)


def _outproj_kernel(oa, ob, oc, od, w_ref, x_ref, gpost_ref, gt_ref, gpre_ref, sc_ref, sh_ref, xo_ref, h_ref):
    W = GROUP_W
    mix = _dot(oa[...], w_ref[0:W, :])
    mix = mix + _dot(ob[...], w_ref[W:2 * W, :])
    mix = mix + _dot(oc[...], w_ref[2 * W:3 * W, :])
    mix = mix + _dot(od[...], w_ref[3 * W:4 * W, :])
    x = x_ref[...] + gt_ref[...] * (_rms(mix) * gpost_ref[...])
    xo_ref[...] = x
    h_ref[...] = (_rms(x) * gpre_ref[...] * (1.0 + sc_ref[...]) + sh_ref[...]).astype(h_ref.dtype)


def _outproj(l, parts, w_out, x2, mod4, brow, g_post, g_pre_ffn, tm):
    R, D = x2.shape
    W = GROUP_W
    part_spec = pl.BlockSpec((tm, W), lambda i: (i, 0))
    row_spec = pl.BlockSpec((tm, D), lambda i: (i, 0))
    return pl.pallas_call(
        _outproj_kernel,
        grid=(R // tm,),
        in_specs=[part_spec] * 4 + [
            pl.BlockSpec((None, 4 * W, D), lambda i: (l, 0, 0)),
            row_spec,
            _vec_spec(l, D),
            _mod_spec(l, 2, D, brow),
            _vec_spec(l, D),
            _mod_spec(l, 4, D, brow),
            _mod_spec(l, 3, D, brow),
        ],
        out_specs=[row_spec, row_spec],
        out_shape=[jax.ShapeDtypeStruct((R, D), F32), jax.ShapeDtypeStruct((R, D), BF16)],
        compiler_params=_cparams("parallel"),
        name="outproj",
    )(*parts, w_out, x2, g_post, mod4, g_pre_ffn, mod4, mod4)


FFN_HALO = 16
FFN_COLS = 256


def _ffn_kernel(hp_ref, h_ref, hn_ref, wup_ref, cw_ref, cb_ref, wdn_ref, x_ref, gpost_ref, gt_ref, o_ref, y_scr):
    i = pl.program_id(1)
    tm = h_ref.shape[0]
    dff = wdn_ref.shape[0]
    rows = tm + 2 * FFN_HALO
    prev = jnp.where(i == 0, jnp.zeros_like(hp_ref[...]), hp_ref[...])
    nxt = jnp.where(i == pl.num_programs(1) - 1, jnp.zeros_like(hn_ref[...]), hn_ref[...])
    hb = jnp.concatenate([prev, h_ref[...], nxt], axis=0)

    def conv(z, cols):
        w = cw_ref[:, cols]
        out = cb_ref[:, cols] + w[0:1] * pltpu.roll(z, 1, axis=0) + w[1:2] * z + w[2:3] * pltpu.roll(z, rows - 1, axis=0)
        return out[FFN_HALO:FFN_HALO + tm]

    def cols(j):
        return slice(j * FFN_COLS, (j + 1) * FFN_COLS), slice(dff + j * FFN_COLS, dff + (j + 1) * FFN_COLS)

    def up(j):
        ca, cg = cols(j)
        return _dot(hb, wup_ref[:, ca]), _dot(hb, wup_ref[:, cg])

    nchunks = dff // FFN_COLS
    z_next = up(0)
    for j in range(nchunks):
        ca, cg = cols(j)
        za, zg = z_next
        if j + 1 < nchunks:
            z_next = up(j + 1)
        y_scr[:, ca] = _gelu_gate(conv(za, ca), conv(zg, cg)).astype(BF16)
    acc = _dot(y_scr[...], wdn_ref[...])
    o_ref[...] = x_ref[...] + gt_ref[...] * (_rms(acc) * gpost_ref[...])


def _ffn(l, h3, x3, w_up, conv_w, conv_b, w_down, mod4, brow, g_post, tm):
    B, n, D = x3.shape
    nt = n // tm
    hb = tm // FFN_HALO
    nh = n // FFN_HALO
    dff = w_down.shape[1]
    once = pl.Buffered(1)
    return pl.pallas_call(
        _ffn_kernel,
        grid=(B, nt),
        in_specs=[
            pl.BlockSpec((None, FFN_HALO, D), lambda b, i: (b, jnp.maximum(i * hb - 1, 0), 0)),
            pl.BlockSpec((None, tm, D), lambda b, i: (b, i, 0)),
            pl.BlockSpec((None, FFN_HALO, D), lambda b, i: (b, jnp.minimum((i + 1) * hb, nh - 1), 0)),
            pl.BlockSpec((None, D, 2 * dff), lambda b, i: (l, 0, 0), pipeline_mode=once),
            pl.BlockSpec((None, 3, 2 * dff), lambda b, i: (l, 0, 0)),
            pl.BlockSpec((None, 1, 2 * dff), lambda b, i: (l, 0, 0)),
            pl.BlockSpec((None, dff, D), lambda b, i: (l, 0, 0), pipeline_mode=once),
            pl.BlockSpec((None, tm, D), lambda b, i: (b, i, 0)),
            _vec_spec(l, D),
            _mod_spec(l, 5, D, brow),
        ],
        out_specs=pl.BlockSpec((None, tm, D), lambda b, i: (b, i, 0)),
        out_shape=jax.ShapeDtypeStruct((B, n, D), F32),
        scratch_shapes=[pltpu.VMEM((tm, dff), BF16)],
        compiler_params=_cparams("parallel", "parallel"),
        name="conv_ffn",
    )(h3, h3, h3, w_up, conv_w, conv_b, w_down, x3, g_post, mod4)


def _prep_w_in(w_in):
    w_in = w_in.astype(BF16)
    z = lambda n: jnp.zeros(w_in.shape[:-1] + (n,), w_in.dtype)
    a = w_in[..., 0:512]
    b = w_in[..., 512:1536]
    gl = w_in[..., 1536:1568]
    c = w_in[..., 1568:1824]
    cq = w_in[..., 1824:2048]
    ckv_kr = w_in[..., 2048:2176]
    return jnp.concatenate([a, b, c, cq, z(32), ckv_kr, gl, z(96)], axis=-1)


def _rope_swap(t):
    q = MLA_ROPE // 4
    return jnp.concatenate([t[..., q:2 * q], t[..., 0:q], t[..., 3 * q:4 * q], t[..., 2 * q:3 * q]], axis=-1)


def _prep_mla(w_uq, w_ukv):
    L = w_uq.shape[0]
    H, NP, RP, HP = MLA_HEADS, MLA_NOPE, MLA_ROPE, MLA_HEAD_PAD
    wq = w_uq.reshape(L, MLA_Q_RANK, H, NP + RP)
    zq = jnp.zeros((L, MLA_Q_RANK, H, HP - NP - RP), w_uq.dtype)
    znope = jnp.zeros((L, MLA_Q_RANK, H, NP), w_uq.dtype)
    q_main = jnp.concatenate([wq, zq], axis=-1)
    q_swap = jnp.concatenate([znope, _rope_swap(wq[..., NP:]), zq], axis=-1)
    padq = lambda w: jnp.pad(w.reshape(L, MLA_Q_RANK, H * HP), ((0, 0), (0, 256 - MLA_Q_RANK), (0, 0)))

    wkv = w_ukv.reshape(L, MLA_KV_RANK, H, NP + MLA_V)
    zk = jnp.zeros((L, MLA_KV_RANK, H, HP - NP), w_ukv.dtype)
    k_lat = jnp.concatenate([wkv[..., :NP], zk], axis=-1)
    eye = jnp.eye(RP, dtype=w_ukv.dtype)
    place = lambda e: jnp.broadcast_to(
        jnp.concatenate([jnp.zeros((RP, NP), e.dtype), e, jnp.zeros((RP, HP - NP - RP), e.dtype)], axis=-1)[None, :, None, :],
        (L, RP, H, HP))
    k_main = jnp.concatenate([k_lat, place(eye)], axis=1)
    k_swap = jnp.concatenate([jnp.zeros_like(k_lat), place(_rope_swap(eye))], axis=1)
    zv = jnp.zeros((L, MLA_KV_RANK, H, HP - MLA_V), w_ukv.dtype)
    odd = (jnp.arange(H) % 2 == 1)[None, None, :, None]
    v_lat = jnp.where(odd, jnp.concatenate([zv, wkv[..., NP:]], axis=-1), jnp.concatenate([wkv[..., NP:], zv], axis=-1))
    v_main = jnp.concatenate([v_lat, jnp.zeros((L, RP, H, HP), w_ukv.dtype)], axis=1)
    flat = lambda w: w.reshape(L, w.shape[1], H * HP).astype(BF16)
    return (padq(q_main).astype(BF16), padq(q_swap).astype(BF16), flat(k_main), flat(k_swap), flat(v_main))


def _rope_tables(n):
    rows = n // GRID_W
    nf = MLA_ROPE // 4
    inv = ROPE_BASE ** (-jnp.arange(nf, dtype=F32) / nf)
    ar = jnp.arange(rows, dtype=F32)[:, None] * inv[None, :]
    ac = jnp.arange(GRID_W, dtype=F32)[:, None] * inv[None, :]
    by_row = lambda t: jnp.repeat(t, GRID_W, axis=0)
    by_col = lambda t: jnp.tile(t, (rows, 1))
    cr, sr, cc, sn = by_row(jnp.cos(ar)), by_row(jnp.sin(ar)), by_col(jnp.cos(ac)), by_col(jnp.sin(ac))
    one = jnp.ones((n, MLA_NOPE), F32)
    zero = jnp.zeros((n, MLA_HEAD_PAD - MLA_NOPE - MLA_ROPE), F32)
    cos = jnp.concatenate([one, cr, cr, cc, cc, zero], axis=1)
    sin = jnp.concatenate([0.0 * one, -sr, sr, -sn, sn, zero], axis=1)
    return cos, sin


def _pick_tile(n, want):
    t = min(n, want)
    while n % t:
        t //= 2
    return t


def kernel(x, c, ctx, c_ctx, w_mod, b_mod, g_pre_mix, g_post_mix, g_pre_ffn, g_post_ffn, w_in,
           sgu_norm, sgu_w, sgu_b, gla_w_gate, gla_b_gate, gla_norm,
           s5_a_re, s5_a_im, s5_log_dt, s5_b_re, s5_b_im, s5_c_re, s5_c_im, s5_d, s5_w_glu, s5_b_glu,
           mla_q_norm, mla_w_uq, mla_kv_norm, mla_w_ukv, w_out,
           ffn_w_up, ffn_conv_w, ffn_conv_b, ffn_w_down):
    B, n, D = x.shape
    nctx = ctx.shape[1]
    L = w_mod.shape[0]
    W = GROUP_W
    assert B < 8 and n % 512 == 0 and nctx % 128 == 0 and n % GRID_W == 0

    c8 = jnp.concatenate([c, c_ctx[None, :], jnp.zeros((8 - B - 1, D), F32)], axis=0)
    mod4 = _modulation(c8, w_mod, b_mod).reshape(L, 8, 1, 6 * D)
    vec = lambda p: p.reshape(L, 1, -1).astype(F32)
    g_pre_mix, g_post_mix, g_pre_ffn, g_post_ffn = map(vec, (g_pre_mix, g_post_mix, g_pre_ffn, g_post_ffn))
    w_in_p = _prep_w_in(w_in)
    sgu_gn = vec(sgu_norm)
    sgu_w_st = sgu_w.reshape(L, -1, MLP_CHUNK).astype(BF16)
    sgu_bias = jnp.repeat(jnp.swapaxes(sgu_b, 1, 2), HEAD_D, axis=2).astype(F32)
    ones_bd = jnp.kron(jnp.eye(W // HEAD_D, dtype=F32), jnp.ones((HEAD_D, HEAD_D), F32)).astype(BF16)
    gla_wg = jnp.zeros((L, 2, 128, W), F32)
    gla_wg = gla_wg.at[:, 0, 0:GATE_RANK].set(gla_w_gate[:, 0]).at[:, 1, GATE_RANK:2 * GATE_RANK].set(gla_w_gate[:, 1])
    gla_bg = gla_b_gate.reshape(L, 2, 1, W).astype(F32)
    gla_gn = vec(gla_norm)
    s5_wglu = s5_w_glu.astype(BF16)
    s5_bglu = vec(s5_b_glu)
    mla_qn = jnp.pad(mla_q_norm, ((0, 0), (0, 256 - MLA_Q_RANK))).reshape(L, 1, 256).astype(F32)
    mla_kvn = jnp.pad(mla_kv_norm, ((0, 0), (0, 128 - MLA_KV_RANK))).reshape(L, 1, 128).astype(F32)
    mla_wts = _prep_mla(mla_w_uq, mla_w_ukv)
    tables = _rope_tables(n)
    w_out_b = w_out.astype(BF16)
    w_up_b = ffn_w_up.astype(BF16)
    w_down_b = ffn_w_down.astype(BF16)
    dff = ffn_w_down.shape[1]
    half_gate = jnp.concatenate([jnp.ones((dff,), F32), jnp.full((dff,), 0.5, F32)])
    conv_w = ffn_conv_w.astype(F32) * half_gate
    conv_b = (ffn_conv_b.astype(F32) * half_gate).reshape(L, 1, -1)

    s5_mats = jax.vmap(_s5_matrices)(s5_a_re, s5_a_im, s5_log_dt, s5_b_re, s5_b_im, s5_c_re, s5_c_im, s5_d)

    tm_x = _pick_tile(n, 512)
    tm_c = _pick_tile(nctx, 512)
    tpb_x = n // tm_x
    brow_x = lambda i: i // tpb_x
    brow_c = lambda *g: B
    brow_x2 = lambda b, i: b

    xs = x.reshape(B * n, D)
    cs = ctx.reshape(B * nctx, D)

    for l in range(L):
        need_ctx = l < L - 1
        sgu = (sgu_gn, sgu_w_st, sgu_bias, ones_bd)
        oa_x, pb_x, pc_x, pd_x, pg_x = _inproj(l, xs, mod4, brow_x, g_pre_mix, w_in_p, sgu, tm_x)
        oa_c, pb_c, pc_c, pd_c, pg_c = _inproj(l, cs, mod4, brow_c, g_pre_mix, w_in_p, sgu, tm_c)

        r3 = lambda t, m: t.reshape(B, m, t.shape[-1])
        ob_c, ob_x = _gla(l, r3(pb_c, nctx), r3(pg_c, nctx), r3(pb_x, n), r3(pg_x, n),
                          gla_wg, gla_bg, gla_gn, ones_bd, _pick_tile(nctx, GLA_TILE), _pick_tile(n, 4 * GLA_TILE))
        oc_c, oc_x = _s5(l, r3(pc_c, nctx), r3(pc_x, n), s5_mats, s5_wglu, s5_bglu, need_ctx, tm_c)
        wq, wqs, wk, wks, wv = mla_wts
        nk = n + nctx
        q_x, k_all, v_all = _mla_prep(l, pd_x, mla_qn, mla_kvn, mla_wts, tables, tm_x, n, B, 0, nk)
        q_c, k_all, v_all = _mla_prep(l, pd_c, mla_qn, mla_kvn, (wq, None, wk, None, wv), None, tm_c, nctx, B, n, nk,
                                      kv_into=(k_all, v_all))
        tk = next(t for t in (2816, 1408, 768, 512, 256, 128) if nk % t == 0)
        od_x = _attention(r3(q_x, n), k_all, v_all, tm_x, tk, 0, nk).reshape(B * n, W)

        xs, hx = _outproj(l, (oa_x, ob_x.reshape(B * n, W), oc_x, od_x), w_out_b, xs, mod4, brow_x,
                          g_post_mix, g_pre_ffn, tm_x)
        xs = _ffn(l, hx.reshape(B, n, D), xs.reshape(B, n, D), w_up_b, conv_w, conv_b, w_down_b, mod4, brow_x2,
                  g_post_ffn, tm_x).reshape(B * n, D)

        if need_ctx:
            od_c = _attention(r3(q_c, nctx), k_all, v_all, tm_c, nctx, n, nctx).reshape(B * nctx, W)
            cs, hc = _outproj(l, (oa_c, ob_c.reshape(B * nctx, W), oc_c, od_c), w_out_b, cs, mod4, brow_c,
                              g_post_mix, g_pre_ffn, tm_c)
            cs = _ffn(l, hc.reshape(B, nctx, D), cs.reshape(B, nctx, D), w_up_b, conv_w, conv_b, w_down_b, mod4,
                      brow_c, g_post_ffn, tm_c).reshape(B * nctx, D)
    return xs.reshape(B, n, D)
```

```python
import functools

import jax
import jax.numpy as jnp
from jax import lax
from jax.experimental import pallas as pl
from jax.experimental.pallas import tpu as pltpu

F32 = jnp.float32
BF16 = jnp.bfloat16

EPS = 1e-6
GRID_W = 64
GROUP_W = 256
HEAD_D = 64
MLP_CHUNK = 128
GATE_RANK = 16
GATE_TEMP = 16.0
GLA_CHUNK = 64
GLA_TILE = 256
S5_IN = 16
S5_T = 16
S5_TILE = 128
MLA_HEADS = 4
MLA_NOPE = 64
MLA_ROPE = 32
MLA_V = 64
MLA_Q_RANK = 224
MLA_KV_RANK = 96
MLA_HEAD_PAD = 128
ROPE_BASE = 10000.0
LOG2E = 1.4426950408889634

VMEM_LIMIT = 48 * 1024 * 1024


def _cparams(*sem):
    return pltpu.CompilerParams(dimension_semantics=sem, vmem_limit_bytes=VMEM_LIMIT)


def _dot(a, b):
    return jnp.dot(a, b, preferred_element_type=F32)


def _dot_nt(a, b):
    return lax.dot_general(a, b, (((1,), (1,)), ((), ())), preferred_element_type=F32)


def _dot_tn(a, b):
    return lax.dot_general(a, b, (((0,), (0,)), ((), ())), preferred_element_type=F32)


def _split(a):
    hi = a.astype(BF16)
    lo = (a - hi.astype(F32)).astype(BF16)
    return hi, lo


def _dot_x2(a, b_bf16):
    hi, lo = _split(a)
    return _dot(hi, b_bf16) + _dot(lo, b_bf16)


def _dot_x3(a, b):
    ah, al = _split(a)
    bh, bl = _split(b)
    return _dot(ah, bh) + _dot(al, bh) + _dot(ah, bl)


def _rms(x):
    return x * lax.rsqrt(jnp.mean(x * x, axis=-1, keepdims=True) + EPS)


def _gelu(x):
    return 0.5 * x * (1.0 + jnp.tanh(0.7978845608028654 * (x + 0.044715 * (x * x * x))))


def _gelu_gate(a, half_g):
    u = a * (0.7978845608028654 + 0.035677408136300125 * (a * a))
    return (a * half_g) * (1.0 + jnp.tanh(u))


def _sigmoid(x):
    return 1.0 / (1.0 + jnp.exp(-x))


def _lane_group(shape, width):
    return lax.broadcasted_iota(jnp.int32, shape, len(shape) - 1) // width


def _mod_kernel(c_ref, w_ref, b_ref, o_ref):
    c = c_ref[...]
    s = c * _sigmoid(c)
    o_ref[...] = _dot_x3(s, w_ref[...]) + b_ref[...]


def _modulation(c8, w_mod, b_mod):
    L, D, W = w_mod.shape
    tn = 1536
    return pl.pallas_call(
        _mod_kernel,
        grid=(L, W // tn),
        in_specs=[
            pl.BlockSpec((8, D), lambda l, j: (0, 0)),
            pl.BlockSpec((None, D, tn), lambda l, j: (l, 0, j)),
            pl.BlockSpec((None, 1, tn), lambda l, j: (l, 0, j)),
        ],
        out_specs=pl.BlockSpec((None, 8, tn), lambda l, j: (l, 0, j)),
        out_shape=jax.ShapeDtypeStruct((L, 8, W), F32),
        compiler_params=_cparams("arbitrary", "arbitrary"),
        name="modulation",
    )(c8, w_mod, b_mod.reshape(L, 1, W))


def _mod_spec(l, j, D, bfn):
    return pl.BlockSpec((None, None, 1, D), lambda *g: (l, bfn(*g), 0, j))


def _vec_spec(l, width):
    return pl.BlockSpec((None, 1, width), lambda *g: (l, 0, 0))


IN_SLABS = (("a", 0, 512), ("b", 512, 1024), ("c", 1536, 256), ("d", 1792, 384), ("g", 2176, 128))
IN_PAD_COLS = 2304


def _sgu_tile(p, gn_ref, w_ref, b_ref, ones_ref, o_ref):
    tm = p.shape[0]
    g = _gelu(p)
    u = g[:, :GROUP_W]
    v = g[:, GROUP_W:]
    ms = _dot_x2(v * v, ones_ref[...]) * (1.0 / HEAD_D)
    vb = (v * lax.rsqrt(ms + EPS) * gn_ref[...]).astype(BF16)
    head = _lane_group((MLP_CHUNK, GROUP_W), HEAD_D)
    w = w_ref[...]
    for c in range(tm // MLP_CHUNK):
        rows = slice(c * MLP_CHUNK, (c + 1) * MLP_CHUNK)
        r = _dot(w, vb[rows])
        s = b_ref[...]
        for h in range(GROUP_W // HEAD_D):
            s = s + jnp.where(head == h, r[h * MLP_CHUNK:(h + 1) * MLP_CHUNK], 0.0)
        o_ref[rows, :] = (u[rows] * s).astype(o_ref.dtype)


def _inproj_kernel(x_ref, g_ref, sc_ref, sh_ref, w_ref, gn_ref, wsp_ref, bsp_ref, ones_ref, oa, ob, oc, od, og):
    h = _rms(x_ref[...]) * g_ref[...] * (1.0 + sc_ref[...]) + sh_ref[...]
    hb = h.astype(BF16)
    (_, off_a, width_a) = IN_SLABS[0]
    pa = _dot(hb, w_ref[:, off_a:off_a + width_a])
    for (_, off, width), o_ref in zip(IN_SLABS[1:], (ob, oc, od, og)):
        o_ref[...] = _dot(hb, w_ref[:, off:off + width])
    _sgu_tile(pa, gn_ref, wsp_ref, bsp_ref, ones_ref, oa)


def _inproj(l, x2, mod4, brow, g_pre, w_in_p, sgu, tm):
    R, D = x2.shape
    gn, w_st, bias, ones_bd = sgu
    H = GROUP_W // HEAD_D
    grid = (R // tm,)
    outs = [jax.ShapeDtypeStruct((R, GROUP_W), BF16)] + [jax.ShapeDtypeStruct((R, width), F32) for (_, _, width) in IN_SLABS[1:]]
    return pl.pallas_call(
        _inproj_kernel,
        grid=grid,
        in_specs=[
            pl.BlockSpec((tm, D), lambda i: (i, 0)),
            _vec_spec(l, D),
            _mod_spec(l, 1, D, brow),
            _mod_spec(l, 0, D, brow),
            pl.BlockSpec((None, D, IN_PAD_COLS), lambda i: (l, 0, 0)),
            _vec_spec(l, GROUP_W),
            pl.BlockSpec((None, H * MLP_CHUNK, MLP_CHUNK), lambda i: (l, 0, 0)),
            pl.BlockSpec((None, MLP_CHUNK, GROUP_W), lambda i: (l, 0, 0)),
            pl.BlockSpec((GROUP_W, GROUP_W), lambda i: (0, 0)),
        ],
        out_specs=[pl.BlockSpec((tm, GROUP_W), lambda i: (i, 0))]
        + [pl.BlockSpec((tm, width), lambda i: (i, 0)) for (_, _, width) in IN_SLABS[1:]],
        out_shape=outs,
        compiler_params=_cparams("parallel"),
        name="inproj",
    )(x2, g_pre, mod4, mod4, w_in_p, gn, w_st, bias, ones_bd)


def _gla_kernel(*refs, rev, finish):
    if finish:
        (pb_ref, pg_ref, wg_ref, bg_ref, s0_ref, ones_ref, dec_ref, oprev_ref, gn_ref, o_ref, sfin_ref, st_scr) = refs
    else:
        (pb_ref, pg_ref, wg_ref, bg_ref, s0_ref, ones_ref, dec_ref, o_ref, sfin_ref, st_scr) = refs
    i = pl.program_id(1)
    C, W, T = GLA_CHUNK, GROUP_W, GLA_TILE
    H = W // HEAD_D
    nsub = T // C
    ntile = pb_ref.shape[0] // T

    @pl.when(i == 0)
    def _():
        st_scr[...] = s0_ref[...]

    tri = dec_ref[...]
    mask4 = jnp.concatenate([tri.astype(F32)] * H, axis=0)
    head_t = _lane_group((T, W), HEAD_D)
    bd = (lax.broadcasted_iota(jnp.int32, (W, W), 0) // HEAD_D) == _lane_group((W, W), HEAD_D)
    tiles = list(range(ntile - 1, -1, -1)) if rev else list(range(ntile))
    subs = list(range(nsub - 1, -1, -1)) if rev else list(range(nsub))
    rows = lambda a: slice(a * T, (a + 1) * T)
    sub = lambda j: slice(j * C, (j + 1) * C)
    stack = lambda blocks: jnp.concatenate([blocks[j] for j in range(nsub)], axis=0)

    logg = {}
    for a in tiles:
        z = _dot_x3(pg_ref[rows(a), :], wg_ref[...]) + bg_ref[...]
        logg[a] = (jnp.minimum(z, 0.0) - jnp.log(1.0 + jnp.exp(-jnp.abs(z)))) * (1.0 / GATE_TEMP)
    cums = {}
    for a in tiles:
        g_hi, g_lo = _split(logg[a])
        cums[a] = _dot(tri, g_hi) + _dot(tri, g_lo)
    qin, qoff, kin, kend, vb, tot, ptile = {}, {}, {}, {}, {}, {}, {}
    for a in tiles:
        cum = cums[a]
        q = pb_ref[rows(a), 0:W] * (HEAD_D ** -0.5)
        k = pb_ref[rows(a), W:2 * W]
        last_row = (lambda j: j * C) if rev else (lambda j: j * C + C - 1)
        off, run = {}, jnp.zeros((1, W), F32)
        for j in subs:
            tot[(a, j)] = cum[last_row(j):last_row(j) + 1]
            off[j] = run
            run = run + tot[(a, j)]
        ptile[a] = jnp.exp(run)
        e = jnp.exp(cum)
        qin[a] = (q * e).astype(BF16)
        qoff[a] = (q * (e * jnp.exp(stack({j: jnp.broadcast_to(off[j], (C, W)) for j in subs})))).astype(BF16)
        kin[a] = (k * jnp.exp(-cum)).astype(BF16)
        kend[a] = (k * jnp.exp(stack({j: jnp.broadcast_to(tot[(a, j)], (C, W)) for j in subs}) - cum)).astype(BF16)
        vb[a] = pb_ref[rows(a), 2 * W:3 * W].astype(BF16)
    sc = {}
    for a in tiles:
        qst = jnp.concatenate([jnp.where(head_t == h, qin[a], jnp.zeros_like(qin[a])) for h in range(H)], axis=0)
        sc[a] = (_dot_nt(qst, kin[a]) * mask4).astype(BF16)
    ost = {a: _dot(sc[a], vb[a]) for a in tiles}
    kvt = {(a, j): jnp.where(bd, _dot_tn(vb[a][sub(j)], kend[a][sub(j)]), 0.0) for a in tiles for j in subs}
    cross, rend = {}, {}
    for a in tiles:
        r = None
        for j in subs:
            if r is not None:
                cross[(a, j)] = _dot_nt(qin[a][sub(j)], r.astype(BF16))
                r = r * jnp.exp(tot[(a, j)]) + kvt[(a, j)]
            else:
                cross[(a, j)] = jnp.zeros((C, W), F32)
                r = kvt[(a, j)]
        rend[a] = r
    st = st_scr[...]
    inter = {}
    for a in tiles:
        inter[a] = _dot_nt(qoff[a], st.astype(BF16))
        st = st * ptile[a] + rend[a]
    st_scr[...] = st

    for a in tiles:
        o = inter[a] + jnp.concatenate([cross[(a, j)] for j in range(nsub)], axis=0)
        for h in range(H):
            o = o + jnp.where(head_t == h, ost[a][h * T:(h + 1) * T], 0.0)
        if finish:
            o = o + oprev_ref[rows(a), :]
            ms = _dot_x2(o * o, ones_ref[...]) * (1.0 / HEAD_D)
            o = o * lax.rsqrt(ms + EPS) * gn_ref[...]
            r = pb_ref[rows(a), 3 * W:4 * W]
            o_ref[rows(a), :] = (o * (r * _sigmoid(r))).astype(o_ref.dtype)
        else:
            o_ref[rows(a), :] = o

    @pl.when(i == pl.num_programs(1) - 1)
    def _():
        sfin_ref[...] = st_scr[...]


def _gla_decay_matrices():
    T, C = GLA_TILE, GLA_CHUNK
    t = jnp.arange(T)[:, None]
    s = jnp.arange(T)[None, :]
    same = (t // C) == (s // C)
    return jnp.stack([same & (s <= t), same & (s >= t)]).astype(BF16)


def _gla_pass(l, d, pb3, pg3, wg, bg, s0, ones_bd, dec, oprev, gn, tt):
    B, n, _ = pb3.shape
    nt = n // tt
    rev = d == 1
    finish = oprev is not None
    W = GROUP_W

    def tok(b, i):
        return (b, (nt - 1 - i) if rev else i, 0)

    in_specs = [
        pl.BlockSpec((None, tt, 4 * W), tok),
        pl.BlockSpec((None, tt, 128), tok),
        pl.BlockSpec((None, None, 128, W), lambda b, i: (l, d, 0, 0)),
        pl.BlockSpec((None, None, 1, W), lambda b, i: (l, d, 0, 0)),
        pl.BlockSpec((None, W, W), lambda b, i: (b, 0, 0)),
        pl.BlockSpec((W, W), lambda b, i: (0, 0)),
        pl.BlockSpec((None, GLA_TILE, GLA_TILE), lambda b, i: (d, 0, 0)),
    ]
    args = [pb3, pg3, wg, bg, s0, ones_bd, dec]
    if finish:
        in_specs += [pl.BlockSpec((None, tt, W), tok), _vec_spec(l, W)]
        args += [oprev, gn]
    return pl.pallas_call(
        functools.partial(_gla_kernel, rev=rev, finish=finish),
        grid=(B, nt),
        in_specs=in_specs,
        out_specs=[pl.BlockSpec((None, tt, W), tok), pl.BlockSpec((None, W, W), lambda b, i: (b, 0, 0))],
        out_shape=[jax.ShapeDtypeStruct((B, n, W), BF16 if finish else F32),
                   jax.ShapeDtypeStruct((B, W, W), F32)],
        scratch_shapes=[pltpu.VMEM((W, W), F32)],
        compiler_params=_cparams("arbitrary", "arbitrary"),
        name="gla_bwd" if rev else "gla_fwd",
    )(*args)


def _gla(l, pb_c, pg_c, pb_x, pg_x, wg, bg, gn, ones_bd, tt_c, tt_x):
    B = pb_x.shape[0]
    zero = jnp.zeros((B, GROUP_W, GROUP_W), F32)
    dec = _gla_decay_matrices()
    ofc, sfc = _gla_pass(l, 0, pb_c, pg_c, wg, bg, zero, ones_bd, dec, None, None, tt_c)
    ofx, _ = _gla_pass(l, 0, pb_x, pg_x, wg, bg, sfc, ones_bd, dec, None, None, tt_x)
    ob_c, sbc = _gla_pass(l, 1, pb_c, pg_c, wg, bg, zero, ones_bd, dec, ofc, gn, tt_c)
    ob_x, _ = _gla_pass(l, 1, pb_x, pg_x, wg, bg, sbc, ones_bd, dec, ofx, gn, tt_x)
    return ob_c, ob_x


def _s5_matrices(a_re, a_im, log_dt, b_re, b_im, c_re, c_im, d_skip):
    T = S5_T
    G, P = a_re.shape[1:]
    I = b_re.shape[-1]
    lam = lax.complex(a_re.astype(F32), a_im.astype(F32))
    ldt = lam * jnp.exp(log_dt.astype(F32))[..., None]
    lam_bar = jnp.exp(ldt)
    b_bar = ((lam_bar - 1.0) / lam)[..., None] * lax.complex(b_re.astype(F32), b_im.astype(F32))
    cmat = lax.complex(c_re.astype(F32), c_im.astype(F32))
    steps = jnp.arange(T + 1, dtype=F32)
    pw = jnp.exp(ldt[..., None] * steps)
    taps = jnp.einsum('dgop,dgpk,dgpi->dgiko', cmat, pw[..., :T], b_bar).real
    taps = taps.at[0, :, :, 0, :].add(jnp.eye(I, dtype=F32)[None] * d_skip.astype(F32)[:, :, None])
    row = T * I
    zeros = jnp.zeros((G, I, row), F32)
    fwd = jnp.concatenate([zeros, taps[0].reshape(G, I, row)], axis=-1)
    bwd = jnp.concatenate([jnp.flip(taps[1], axis=2).reshape(G, I, row), zeros], axis=-1)
    m = jnp.stack([fwd[..., (T - s) * I:(T - s) * I + row] + bwd[..., (T - 1 - s) * I:(T - 1 - s) * I + row]
                   for s in range(T)], axis=1).reshape(G, row, row)

    ar = jnp.arange(T)
    pf = pw[0][..., T - 1 - ar]
    pb = pw[1][..., ar]
    bf = jnp.einsum('gps,gpi->gsip', pf, b_bar[0]).reshape(G, T * I, P)
    bb = jnp.einsum('gps,gpi->gsip', pb, b_bar[1]).reshape(G, T * I, P)
    bmat = jnp.concatenate([bf.real, bf.imag, bf.imag, bf.real, bb.real, bb.imag, bb.imag, bb.real], axis=-1)

    cf = jnp.einsum('gop,gpt->gpto', cmat[0], pw[0][..., 1 + ar]).reshape(G, P, T * I)
    cb = jnp.einsum('gop,gpt->gpto', cmat[1], pw[1][..., T - ar]).reshape(G, P, T * I)
    w = jnp.concatenate([m, cf.real, -cf.imag, cb.real, -cb.imag], axis=1)

    a = pw[..., T]
    acoef = jnp.stack([jnp.concatenate([a[0].real, a[0].real], axis=-1), jnp.concatenate([-a[0].imag, a[0].imag], axis=-1),
                       jnp.concatenate([a[1].real, a[1].real], axis=-1), jnp.concatenate([-a[1].imag, a[1].imag], axis=-1)])
    return bmat.astype(BF16), w.astype(BF16), acoef


def _s5_state_kernel(x_ref, b_ref, o_ref):
    o_ref[...] = _dot(x_ref[...], b_ref[...])


def _s5_state(l, xg, bmat, tr):
    G, R, K = xg.shape
    N = bmat.shape[-1]
    return pl.pallas_call(
        _s5_state_kernel,
        grid=(G, R // tr),
        in_specs=[pl.BlockSpec((None, tr, K), lambda g, i: (g, i, 0)),
                  pl.BlockSpec((None, None, K, N), lambda g, i: (l, g, 0, 0))],
        out_specs=pl.BlockSpec((None, tr, N), lambda g, i: (g, i, 0)),
        out_shape=jax.ShapeDtypeStruct((G, R, N), F32),
        compiler_params=_cparams("parallel", "parallel"),
        name="s5_state",
    )(xg, bmat)


def _s5_scan_kernel(sf_ref, sb_ref, a_ref, h0_ref, hf_ref, hb_ref, hfin_ref, st_scr):
    j = pl.program_id(1)
    tc = sf_ref.shape[1]

    @pl.when(j == 0)
    def _():
        st_scr[...] = h0_ref[...]

    a1f, a2f, a1b, a2b = a_ref[0], a_ref[1], a_ref[2], a_ref[3]
    half = a1f.shape[-1]

    def body(i, hs):
        h1f, h2f, h1b, h2b = hs
        ib = tc - 1 - i
        sf = sf_ref[:, i, :]
        sb = sb_ref[:, ib, :]
        hf_ref[:, i, :] = h1f
        hb_ref[:, ib, :] = h1b
        return (h1f * a1f + h2f * a2f + sf[:, 0:half], h2f * a1f - h1f * a2f + sf[:, half:],
                h1b * a1b + h2b * a2b + sb[:, 0:half], h2b * a1b - h1b * a2b + sb[:, half:])

    hs = lax.fori_loop(0, tc, body, (st_scr[0], st_scr[1], st_scr[2], st_scr[3]))
    for k in range(4):
        st_scr[k] = hs[k]

    @pl.when(j == pl.num_programs(1) - 1)
    def _():
        hfin_ref[...] = st_scr[...]


def _s5_scan(l, s, acoef, h0, B, tc):
    G, R, n2 = s.shape
    nt = R // B // tc
    half = n2 // 4
    fwd = lambda b, j: (0, b * nt + j, 0)
    bwd = lambda b, j: (0, b * nt + nt - 1 - j, 0)
    st_spec = pl.BlockSpec((None, 4, G, half), lambda b, j: (b, 0, 0, 0))
    return pl.pallas_call(
        _s5_scan_kernel,
        grid=(B, nt),
        in_specs=[pl.BlockSpec((G, tc, 2 * half), fwd),
                  pl.BlockSpec((G, tc, 2 * half), lambda b, j: (0, b * nt + nt - 1 - j, 1)),
                  pl.BlockSpec((None, 4, G, half), lambda b, j: (l, 0, 0, 0)),
                  st_spec],
        out_specs=[pl.BlockSpec((G, tc, half), fwd), pl.BlockSpec((G, tc, half), bwd), st_spec],
        out_shape=[jax.ShapeDtypeStruct((G, R, half), F32), jax.ShapeDtypeStruct((G, R, half), F32),
                   jax.ShapeDtypeStruct((B, 4, G, half), F32)],
        scratch_shapes=[pltpu.VMEM((4, G, half), F32)],
        compiler_params=_cparams("arbitrary", "arbitrary"),
        name="s5_scan",
    )(s, s, acoef, h0)


def _s5_out_kernel(x_ref, hf_ref, hb_ref, w_ref, o_ref):
    k = x_ref.shape[-1]
    kh = hf_ref.shape[-1]
    y = _dot(x_ref[...], w_ref[0:k, :])
    y = y + _dot(hf_ref[...].astype(BF16), w_ref[k:k + kh, :])
    o_ref[...] = y + _dot(hb_ref[...].astype(BF16), w_ref[k + kh:, :])


def _s5_out(l, xg, hf, hb, w, tr):
    G, R, K = xg.shape
    kh = hf.shape[-1]
    row = lambda width: pl.BlockSpec((None, tr, width), lambda g, i: (g, i, 0))
    return pl.pallas_call(
        _s5_out_kernel,
        grid=(G, R // tr),
        in_specs=[row(K), row(kh), row(kh), pl.BlockSpec((None, None, K + 2 * kh, K), lambda g, i: (l, g, 0, 0))],
        out_specs=row(K),
        out_shape=jax.ShapeDtypeStruct((G, R, K), F32),
        compiler_params=_cparams("parallel", "parallel"),
        name="s5_out",
    )(xg, hf, hb, w)


def _s5_finish_kernel(y_ref, w_ref, b_ref, o_ref):
    y = _gelu(y_ref[...])
    gate = _sigmoid(_dot(y.astype(BF16), w_ref[...]) + b_ref[...])
    o_ref[...] = (y * gate).astype(o_ref.dtype)


def _s5_finish(l, y2, w_glu, b_glu, tm):
    R, W = y2.shape
    return pl.pallas_call(
        _s5_finish_kernel,
        grid=(R // tm,),
        in_specs=[pl.BlockSpec((tm, W), lambda i: (i, 0)),
                  pl.BlockSpec((None, W, W), lambda i: (l, 0, 0)),
                  _vec_spec(l, W)],
        out_specs=pl.BlockSpec((tm, W), lambda i: (i, 0)),
        out_shape=jax.ShapeDtypeStruct((R, W), BF16),
        compiler_params=_cparams("parallel"),
        name="s5_finish",
    )(y2, w_glu, b_glu)


def _to_groups(u3):
    B, n, W = u3.shape
    G = W // S5_IN
    x = u3.astype(BF16).reshape(B, n // S5_T, S5_T, G, S5_IN).transpose(3, 0, 1, 2, 4)
    return x.reshape(G, B * (n // S5_T), S5_T * S5_IN)


def _from_groups(y, B):
    G, R, _ = y.shape
    nc = R // B
    y = y.reshape(G, B, nc, S5_T, S5_IN).transpose(1, 2, 3, 0, 4)
    return y.reshape(B * nc * S5_T, G * S5_IN)


def _s5_gather_kernel(x_ref, o_ref, xs_scr, xt_scr):
    G, tc, _ = o_ref.shape
    nh = x_ref.shape[1] // 128
    gh = G // nh
    for h in range(nh):
        xs_scr[h] = x_ref[:, h * 128:(h + 1) * 128]
    for t in range(S5_T):
        for h in range(nh):
            rt = xs_scr[h, pl.ds(t, tc, stride=S5_T), :].T
            for g in range(gh):
                xt_scr[h * gh + g, t * S5_IN:(t + 1) * S5_IN, :] = rt[g * S5_IN:(g + 1) * S5_IN, :]
    for g in range(G):
        o_ref[g] = xt_scr[g].T.astype(o_ref.dtype)


def _s5_gather(x2, tc):
    N, W = x2.shape
    G = W // S5_IN
    R = N // S5_T
    return pl.pallas_call(
        _s5_gather_kernel,
        grid=(R // tc,),
        in_specs=[pl.BlockSpec((tc * S5_T, W), lambda i: (i, 0))],
        out_specs=pl.BlockSpec((G, tc, S5_T * S5_IN), lambda i: (0, i, 0)),
        out_shape=jax.ShapeDtypeStruct((G, R, S5_T * S5_IN), BF16),
        scratch_shapes=[pltpu.VMEM((W // 128, tc * S5_T, 128), F32), pltpu.VMEM((G, S5_T * S5_IN, tc), F32)],
        compiler_params=_cparams("parallel"),
        name="s5_gather",
    )(x2)


def _s5_scatter_kernel(y_ref, w_ref, b_ref, o_ref, yt_scr, tok_scr):
    G, tc, _ = y_ref.shape
    nh = tok_scr.shape[0]
    gh = G // nh
    for g in range(G):
        yt = y_ref[g].T
        for t in range(S5_T):
            yt_scr[t, g // gh, (g % gh) * S5_IN:(g % gh + 1) * S5_IN, :] = yt[t * S5_IN:(t + 1) * S5_IN, :]
    for t in range(S5_T):
        for h in range(nh):
            tok_scr[h, pl.ds(t, tc, stride=S5_T), :] = yt_scr[t, h].T
    y = _gelu(jnp.concatenate([tok_scr[h] for h in range(nh)], axis=1))
    gate = _sigmoid(_dot(y.astype(BF16), w_ref[...]) + b_ref[...])
    o_ref[...] = (y * gate).astype(o_ref.dtype)


def _s5_scatter_finish(l, yg, w_glu, b_glu, tc):
    G, R, K = yg.shape
    W = G * S5_IN
    return pl.pallas_call(
        _s5_scatter_kernel,
        grid=(R // tc,),
        in_specs=[pl.BlockSpec((G, tc, K), lambda i: (0, i, 0)),
                  pl.BlockSpec((None, W, W), lambda i: (l, 0, 0)),
                  _vec_spec(l, W)],
        out_specs=pl.BlockSpec((tc * S5_T, W), lambda i: (i, 0)),
        out_shape=jax.ShapeDtypeStruct((R * S5_T, W), BF16),
        scratch_shapes=[pltpu.VMEM((S5_T, W // 128, 128, tc), F32), pltpu.VMEM((W // 128, tc * S5_T, 128), F32)],
        compiler_params=_cparams("parallel"),
        name="s5_scatter",
    )(yg, w_glu, b_glu)


def _s5(l, pc_c, pc_x, mats, w_glu, b_glu, need_ctx, tm_c):
    bmat, w, acoef = mats
    B, n, W = pc_x.shape
    G = W // S5_IN
    xg_c = _to_groups(pc_c)
    tc_x = min(S5_TILE, n // S5_T)
    xg_x = _s5_gather(pc_x.reshape(B * n, W), tc_x)
    tr_c = min(512, xg_c.shape[1])
    tr_x = min(512, xg_x.shape[1])
    h0 = jnp.zeros((B, 4, G, acoef.shape[-1]), F32)
    hf_c, hb_c, h1 = _s5_scan(l, _s5_state(l, xg_c, bmat, tr_c), acoef, h0, B, xg_c.shape[1] // B)
    hf_x, hb_x, _ = _s5_scan(l, _s5_state(l, xg_x, bmat, tr_x), acoef, h1, B, tc_x)
    oc_x = _s5_scatter_finish(l, _s5_out(l, xg_x, hf_x, hb_x, w, tr_x), w_glu, b_glu, tc_x)
    oc_c = None
    if need_ctx:
        y_c = _from_groups(_s5_out(l, xg_c, hf_c, hb_c, w, tr_c), B)
        oc_c = _s5_finish(l, y_c, w_glu, b_glu, tm_c)
    return oc_c, oc_x


def _mla_prep_kernel(*refs, rope):
    refs = refs[:-5] + refs[-3:]
    if rope:
        pd_ref, qn_ref, kvn_ref, wq_ref, wqs_ref, wk_ref, wks_ref, wv_ref, cos_ref, sin_ref, q_ref, k_ref, v_ref = refs
    else:
        pd_ref, qn_ref, kvn_ref, wq_ref, wk_ref, wv_ref, q_ref, k_ref, v_ref = refs
    cq = pd_ref[:, 0:256]
    ms = jnp.sum(cq * cq, axis=-1, keepdims=True) * (1.0 / MLA_Q_RANK)
    cqn = (cq * lax.rsqrt(ms + EPS) * qn_ref[...]).astype(BF16)
    ck = pd_ref[:, 256:384]
    lane = lax.broadcasted_iota(jnp.int32, ck.shape, 1)
    is_lat = lane < MLA_KV_RANK
    ms = jnp.sum(jnp.where(is_lat, ck * ck, 0.0), axis=-1, keepdims=True) * (1.0 / MLA_KV_RANK)
    ckn = jnp.where(is_lat, ck * lax.rsqrt(ms + EPS) * kvn_ref[...], ck).astype(BF16)
    q = _dot(cqn, wq_ref[...])
    k = _dot(ckn, wk_ref[...])
    if rope:
        cos = jnp.concatenate([cos_ref[...]] * MLA_HEADS, axis=1)
        sin = jnp.concatenate([sin_ref[...]] * MLA_HEADS, axis=1)
        q = q * cos + _dot(cqn, wqs_ref[...]) * sin
        k = k * cos + _dot(ckn, wks_ref[...]) * sin
    q_ref[...] = (q * ((MLA_NOPE + MLA_ROPE) ** -0.5 * LOG2E)).astype(BF16)
    k_ref[...] = k.astype(BF16)
    v_ref[...] = _dot(ckn, wv_ref[...]).astype(BF16)


def _mla_prep(l, pd, qn, kvn, wts, tables, tm, n, B, row0, nk, kv_into):
    R = pd.shape[0]
    HP = MLA_HEADS * MLA_HEAD_PAD
    wq, wqs, wk, wks, wv = wts
    rope = tables is not None
    npt = n // tm
    wspec = lambda r: pl.BlockSpec((None, r, HP), lambda i: (l, 0, 0))
    in_specs = [pl.BlockSpec((tm, 384), lambda i: (i, 0)), _vec_spec(l, 256), _vec_spec(l, 128)]
    if rope:
        tspec = pl.BlockSpec((tm, MLA_HEAD_PAD), lambda i: (i % npt, 0))
        in_specs += [wspec(256), wspec(256), wspec(128), wspec(128), wspec(128), tspec, tspec]
        args = (pd, qn, kvn, wq, wqs, wk, wks, wv) + tuple(tables)
    else:
        in_specs += [wspec(256), wspec(128), wspec(128)]
        args = (pd, qn, kvn, wq, wk, wv)
    aliases = {len(args): 1, len(args) + 1: 2}
    in_specs += [pl.BlockSpec(memory_space=pl.ANY)] * 2
    args = args + tuple(kv_into)
    kv_spec = pl.BlockSpec((None, tm, HP), lambda i: (i // npt, row0 // tm + i % npt, 0))
    kv_shape = jax.ShapeDtypeStruct((B, nk, HP), BF16)
    return pl.pallas_call(
        functools.partial(_mla_prep_kernel, rope=rope),
        grid=(R // tm,),
        in_specs=in_specs,
        out_specs=[pl.BlockSpec((tm, HP), lambda i: (i, 0)), kv_spec, kv_spec],
        out_shape=[jax.ShapeDtypeStruct((R, HP), BF16), kv_shape, kv_shape],
        input_output_aliases=aliases,
        compiler_params=_cparams("parallel"),
        name="mla_prep",
    )(*args)


def _attn_kernel(q_ref, k_ref, v_ref, o_ref, m_scr, l_scr, acc_scr):
    kv = pl.program_id(2)
    HP = MLA_HEAD_PAD
    tk = k_ref.shape[0]

    @pl.when(kv == 0)
    def _():
        m_scr[...] = jnp.full(m_scr.shape, -jnp.inf, F32)
        l_scr[...] = jnp.zeros(l_scr.shape, F32)
        acc_scr[...] = jnp.zeros(acc_scr.shape, F32)

    def scores(h):
        lanes = slice(h * HP, (h + 1) * HP)
        return _dot_nt(q_ref[:, lanes], k_ref[:, lanes])

    s_next = scores(0)
    for h in range(MLA_HEADS):
        lanes = slice(h * HP, (h + 1) * HP)
        s = s_next
        if h + 1 < MLA_HEADS:
            s_next = scores(h + 1)
        m_prev = m_scr[h]
        m_new = jnp.maximum(m_prev, jnp.max(s, axis=1, keepdims=True))
        alpha = jnp.exp2(m_prev - m_new)
        p = jnp.exp2(s - m_new[:, 0:1])
        lp = p[:, 0:128]
        for c in range(1, tk // 128):
            lp = lp + p[:, c * 128:(c + 1) * 128]
        l_scr[h] = alpha * l_scr[h] + lp
        acc_scr[h] = alpha * acc_scr[h] + _dot(p.astype(BF16), v_ref[:, lanes])
        m_scr[h] = m_new

    @pl.when(kv == pl.num_programs(2) - 1)
    def _():
        low = lax.broadcasted_iota(jnp.int32, acc_scr.shape[1:], 1) < MLA_V
        norm = lambda h: acc_scr[h] * (1.0 / jnp.sum(l_scr[h], axis=1, keepdims=True))
        outs = [jnp.where(low, norm(h), norm(h + 1)) for h in range(0, MLA_HEADS, 2)]
        o_ref[...] = jnp.concatenate(outs, axis=1).astype(o_ref.dtype)


def _attention(q3, k3, v3, tq, tk, k0, nk):
    B, nq, HP = q3.shape
    kb = k0 // tk
    return pl.pallas_call(
        _attn_kernel,
        grid=(B, nq // tq, nk // tk),
        in_specs=[pl.BlockSpec((None, tq, HP), lambda b, i, j: (b, i, 0)),
                  pl.BlockSpec((None, tk, HP), lambda b, i, j: (b, kb + j, 0)),
                  pl.BlockSpec((None, tk, HP), lambda b, i, j: (b, kb + j, 0))],
        out_specs=pl.BlockSpec((None, tq, MLA_HEADS * MLA_V), lambda b, i, j: (b, i, 0)),
        out_shape=jax.ShapeDtypeStruct((B, nq, MLA_HEADS * MLA_V), BF16),
        scratch_shapes=[pltpu.VMEM((MLA_HEADS, tq, 128), F32),
                        pltpu.VMEM((MLA_HEADS, tq, 128), F32),
                        pltpu.VMEM((MLA_HEADS, tq, MLA_HEAD_PAD), F32)],
        compiler_params=_cparams("parallel", "parallel", "arbitrary"),
        name="attention",
    )(q3, k3, v3)


def _outproj_kernel(oa, ob, oc, od, w_ref, x_ref, gpost_ref, gt_ref, gpre_ref, sc_ref, sh_ref, xo_ref, h_ref):
    W = GROUP_W
    mix = _dot(oa[...], w_ref[0:W, :])
    mix = mix + _dot(ob[...], w_ref[W:2 * W, :])
    mix = mix + _dot(oc[...], w_ref[2 * W:3 * W, :])
    mix = mix + _dot(od[...], w_ref[3 * W:4 * W, :])
    x = x_ref[...] + gt_ref[...] * (_rms(mix) * gpost_ref[...])
    xo_ref[...] = x
    h_ref[...] = (_rms(x) * gpre_ref[...] * (1.0 + sc_ref[...]) + sh_ref[...]).astype(h_ref.dtype)


def _outproj(l, parts, w_out, x2, mod4, brow, g_post, g_pre_ffn, tm):
    R, D = x2.shape
    W = GROUP_W
    part_spec = pl.BlockSpec((tm, W), lambda i: (i, 0))
    row_spec = pl.BlockSpec((tm, D), lambda i: (i, 0))
    return pl.pallas_call(
        _outproj_kernel,
        grid=(R // tm,),
        in_specs=[part_spec] * 4 + [
            pl.BlockSpec((None, 4 * W, D), lambda i: (l, 0, 0)),
            row_spec,
            _vec_spec(l, D),
            _mod_spec(l, 2, D, brow),
            _vec_spec(l, D),
            _mod_spec(l, 4, D, brow),
            _mod_spec(l, 3, D, brow),
        ],
        out_specs=[row_spec, row_spec],
        out_shape=[jax.ShapeDtypeStruct((R, D), F32), jax.ShapeDtypeStruct((R, D), BF16)],
        compiler_params=_cparams("parallel"),
        name="outproj",
    )(*parts, w_out, x2, g_post, mod4, g_pre_ffn, mod4, mod4)


FFN_HALO = 16
FFN_COLS = 256


def _ffn_kernel(hp_ref, h_ref, hn_ref, wup_ref, cw_ref, cb_ref, wdn_ref, x_ref, gpost_ref, gt_ref, o_ref, y_scr):
    i = pl.program_id(1)
    tm = h_ref.shape[0]
    dff = wdn_ref.shape[0]
    rows = tm + 2 * FFN_HALO
    prev = jnp.where(i == 0, jnp.zeros_like(hp_ref[...]), hp_ref[...])
    nxt = jnp.where(i == pl.num_programs(1) - 1, jnp.zeros_like(hn_ref[...]), hn_ref[...])
    hb = jnp.concatenate([prev, h_ref[...], nxt], axis=0)

    def conv(z, cols):
        w = cw_ref[:, cols]
        out = cb_ref[:, cols] + w[0:1] * pltpu.roll(z, 1, axis=0) + w[1:2] * z + w[2:3] * pltpu.roll(z, rows - 1, axis=0)
        return out[FFN_HALO:FFN_HALO + tm]

    def cols(j):
        return slice(j * FFN_COLS, (j + 1) * FFN_COLS), slice(dff + j * FFN_COLS, dff + (j + 1) * FFN_COLS)

    def up(j):
        ca, cg = cols(j)
        return _dot(hb, wup_ref[:, ca]), _dot(hb, wup_ref[:, cg])

    nchunks = dff // FFN_COLS
    z_next = up(0)
    for j in range(nchunks):
        ca, cg = cols(j)
        za, zg = z_next
        if j + 1 < nchunks:
            z_next = up(j + 1)
        y_scr[:, ca] = _gelu_gate(conv(za, ca), conv(zg, cg)).astype(BF16)
    acc = _dot(y_scr[...], wdn_ref[...])
    o_ref[...] = x_ref[...] + gt_ref[...] * (_rms(acc) * gpost_ref[...])


def _ffn(l, h3, x3, w_up, conv_w, conv_b, w_down, mod4, brow, g_post, tm):
    B, n, D = x3.shape
    nt = n // tm
    hb = tm // FFN_HALO
    nh = n // FFN_HALO
    dff = w_down.shape[1]
    once = pl.Buffered(1)
    return pl.pallas_call(
        _ffn_kernel,
        grid=(B, nt),
        in_specs=[
            pl.BlockSpec((None, FFN_HALO, D), lambda b, i: (b, jnp.maximum(i * hb - 1, 0), 0)),
            pl.BlockSpec((None, tm, D), lambda b, i: (b, i, 0)),
            pl.BlockSpec((None, FFN_HALO, D), lambda b, i: (b, jnp.minimum((i + 1) * hb, nh - 1), 0)),
            pl.BlockSpec((None, D, 2 * dff), lambda b, i: (l, 0, 0), pipeline_mode=once),
            pl.BlockSpec((None, 3, 2 * dff), lambda b, i: (l, 0, 0)),
            pl.BlockSpec((None, 1, 2 * dff), lambda b, i: (l, 0, 0)),
            pl.BlockSpec((None, dff, D), lambda b, i: (l, 0, 0), pipeline_mode=once),
            pl.BlockSpec((None, tm, D), lambda b, i: (b, i, 0)),
            _vec_spec(l, D),
            _mod_spec(l, 5, D, brow),
        ],
        out_specs=pl.BlockSpec((None, tm, D), lambda b, i: (b, i, 0)),
        out_shape=jax.ShapeDtypeStruct((B, n, D), F32),
        scratch_shapes=[pltpu.VMEM((tm, dff), BF16)],
        compiler_params=_cparams("parallel", "parallel"),
        name="conv_ffn",
    )(h3, h3, h3, w_up, conv_w, conv_b, w_down, x3, g_post, mod4)


def _prep_w_in(w_in):
    w_in = w_in.astype(BF16)
    z = lambda n: jnp.zeros(w_in.shape[:-1] + (n,), w_in.dtype)
    a = w_in[..., 0:512]
    b = w_in[..., 512:1536]
    gl = w_in[..., 1536:1568]
    c = w_in[..., 1568:1824]
    cq = w_in[..., 1824:2048]
    ckv_kr = w_in[..., 2048:2176]
    return jnp.concatenate([a, b, c, cq, z(32), ckv_kr, gl, z(96)], axis=-1)


def _rope_swap(t):
    q = MLA_ROPE // 4
    return jnp.concatenate([t[..., q:2 * q], t[..., 0:q], t[..., 3 * q:4 * q], t[..., 2 * q:3 * q]], axis=-1)


def _prep_mla(w_uq, w_ukv):
    L = w_uq.shape[0]
    H, NP, RP, HP = MLA_HEADS, MLA_NOPE, MLA_ROPE, MLA_HEAD_PAD
    wq = w_uq.reshape(L, MLA_Q_RANK, H, NP + RP)
    zq = jnp.zeros((L, MLA_Q_RANK, H, HP - NP - RP), w_uq.dtype)
    znope = jnp.zeros((L, MLA_Q_RANK, H, NP), w_uq.dtype)
    q_main = jnp.concatenate([wq, zq], axis=-1)
    q_swap = jnp.concatenate([znope, _rope_swap(wq[..., NP:]), zq], axis=-1)
    padq = lambda w: jnp.pad(w.reshape(L, MLA_Q_RANK, H * HP), ((0, 0), (0, 256 - MLA_Q_RANK), (0, 0)))

    wkv = w_ukv.reshape(L, MLA_KV_RANK, H, NP + MLA_V)
    zk = jnp.zeros((L, MLA_KV_RANK, H, HP - NP), w_ukv.dtype)
    k_lat = jnp.concatenate([wkv[..., :NP], zk], axis=-1)
    eye = jnp.eye(RP, dtype=w_ukv.dtype)
    place = lambda e: jnp.broadcast_to(
        jnp.concatenate([jnp.zeros((RP, NP), e.dtype), e, jnp.zeros((RP, HP - NP - RP), e.dtype)], axis=-1)[None, :, None, :],
        (L, RP, H, HP))
    k_main = jnp.concatenate([k_lat, place(eye)], axis=1)
    k_swap = jnp.concatenate([jnp.zeros_like(k_lat), place(_rope_swap(eye))], axis=1)
    zv = jnp.zeros((L, MLA_KV_RANK, H, HP - MLA_V), w_ukv.dtype)
    odd = (jnp.arange(H) % 2 == 1)[None, None, :, None]
    v_lat = jnp.where(odd, jnp.concatenate([zv, wkv[..., NP:]], axis=-1), jnp.concatenate([wkv[..., NP:], zv], axis=-1))
    v_main = jnp.concatenate([v_lat, jnp.zeros((L, RP, H, HP), w_ukv.dtype)], axis=1)
    flat = lambda w: w.reshape(L, w.shape[1], H * HP).astype(BF16)
    return (padq(q_main).astype(BF16), padq(q_swap).astype(BF16), flat(k_main), flat(k_swap), flat(v_main))


def _rope_tables(n):
    rows = n // GRID_W
    nf = MLA_ROPE // 4
    inv = ROPE_BASE ** (-jnp.arange(nf, dtype=F32) / nf)
    ar = jnp.arange(rows, dtype=F32)[:, None] * inv[None, :]
    ac = jnp.arange(GRID_W, dtype=F32)[:, None] * inv[None, :]
    by_row = lambda t: jnp.repeat(t, GRID_W, axis=0)
    by_col = lambda t: jnp.tile(t, (rows, 1))
    cr, sr, cc, sn = by_row(jnp.cos(ar)), by_row(jnp.sin(ar)), by_col(jnp.cos(ac)), by_col(jnp.sin(ac))
    one = jnp.ones((n, MLA_NOPE), F32)
    zero = jnp.zeros((n, MLA_HEAD_PAD - MLA_NOPE - MLA_ROPE), F32)
    cos = jnp.concatenate([one, cr, cr, cc, cc, zero], axis=1)
    sin = jnp.concatenate([0.0 * one, -sr, sr, -sn, sn, zero], axis=1)
    return cos, sin


def _pick_tile(n, want):
    t = min(n, want)
    while n % t:
        t //= 2
    return t


def kernel(x, c, ctx, c_ctx, w_mod, b_mod, g_pre_mix, g_post_mix, g_pre_ffn, g_post_ffn, w_in,
           sgu_norm, sgu_w, sgu_b, gla_w_gate, gla_b_gate, gla_norm,
           s5_a_re, s5_a_im, s5_log_dt, s5_b_re, s5_b_im, s5_c_re, s5_c_im, s5_d, s5_w_glu, s5_b_glu,
           mla_q_norm, mla_w_uq, mla_kv_norm, mla_w_ukv, w_out,
           ffn_w_up, ffn_conv_w, ffn_conv_b, ffn_w_down):
    B, n, D = x.shape
    nctx = ctx.shape[1]
    L = w_mod.shape[0]
    W = GROUP_W
    assert B < 8 and n % 512 == 0 and nctx % 128 == 0 and n % GRID_W == 0

    c8 = jnp.concatenate([c, c_ctx[None, :], jnp.zeros((8 - B - 1, D), F32)], axis=0)
    mod4 = _modulation(c8, w_mod, b_mod).reshape(L, 8, 1, 6 * D)
    vec = lambda p: p.reshape(L, 1, -1).astype(F32)
    g_pre_mix, g_post_mix, g_pre_ffn, g_post_ffn = map(vec, (g_pre_mix, g_post_mix, g_pre_ffn, g_post_ffn))
    w_in_p = _prep_w_in(w_in)
    sgu_gn = vec(sgu_norm)
    sgu_w_st = sgu_w.reshape(L, -1, MLP_CHUNK).astype(BF16)
    sgu_bias = jnp.repeat(jnp.swapaxes(sgu_b, 1, 2), HEAD_D, axis=2).astype(F32)
    ones_bd = jnp.kron(jnp.eye(W // HEAD_D, dtype=F32), jnp.ones((HEAD_D, HEAD_D), F32)).astype(BF16)
    gla_wg = jnp.zeros((L, 2, 128, W), F32)
    gla_wg = gla_wg.at[:, 0, 0:GATE_RANK].set(gla_w_gate[:, 0]).at[:, 1, GATE_RANK:2 * GATE_RANK].set(gla_w_gate[:, 1])
    gla_bg = gla_b_gate.reshape(L, 2, 1, W).astype(F32)
    gla_gn = vec(gla_norm)
    s5_wglu = s5_w_glu.astype(BF16)
    s5_bglu = vec(s5_b_glu)
    mla_qn = jnp.pad(mla_q_norm, ((0, 0), (0, 256 - MLA_Q_RANK))).reshape(L, 1, 256).astype(F32)
    mla_kvn = jnp.pad(mla_kv_norm, ((0, 0), (0, 128 - MLA_KV_RANK))).reshape(L, 1, 128).astype(F32)
    mla_wts = _prep_mla(mla_w_uq, mla_w_ukv)
    tables = _rope_tables(n)
    w_out_b = w_out.astype(BF16)
    w_up_b = ffn_w_up.astype(BF16)
    w_down_b = ffn_w_down.astype(BF16)
    dff = ffn_w_down.shape[1]
    half_gate = jnp.concatenate([jnp.ones((dff,), F32), jnp.full((dff,), 0.5, F32)])
    conv_w = ffn_conv_w.astype(F32) * half_gate
    conv_b = (ffn_conv_b.astype(F32) * half_gate).reshape(L, 1, -1)

    s5_mats = jax.vmap(_s5_matrices)(s5_a_re, s5_a_im, s5_log_dt, s5_b_re, s5_b_im, s5_c_re, s5_c_im, s5_d)

    tm_x = _pick_tile(n, 512)
    tm_c = _pick_tile(nctx, 512)
    tpb_x = n // tm_x
    brow_x = lambda i: i // tpb_x
    brow_c = lambda *g: B
    brow_x2 = lambda b, i: b

    xs = x.reshape(B * n, D)
    cs = ctx.reshape(B * nctx, D)

    for l in range(L):
        need_ctx = l < L - 1
        sgu = (sgu_gn, sgu_w_st, sgu_bias, ones_bd)
        oa_x, pb_x, pc_x, pd_x, pg_x = _inproj(l, xs, mod4, brow_x, g_pre_mix, w_in_p, sgu, tm_x)
        oa_c, pb_c, pc_c, pd_c, pg_c = _inproj(l, cs, mod4, brow_c, g_pre_mix, w_in_p, sgu, tm_c)

        r3 = lambda t, m: t.reshape(B, m, t.shape[-1])
        ob_c, ob_x = _gla(l, r3(pb_c, nctx), r3(pg_c, nctx), r3(pb_x, n), r3(pg_x, n),
                          gla_wg, gla_bg, gla_gn, ones_bd, _pick_tile(nctx, GLA_TILE), _pick_tile(n, 4 * GLA_TILE))
        oc_c, oc_x = _s5(l, r3(pc_c, nctx), r3(pc_x, n), s5_mats, s5_wglu, s5_bglu, need_ctx, tm_c)
        wq, wqs, wk, wks, wv = mla_wts
        nk = n + nctx
        kv0 = jnp.zeros((B, nk, MLA_HEADS * MLA_HEAD_PAD), BF16)
        q_x, k_all, v_all = _mla_prep(l, pd_x, mla_qn, mla_kvn, mla_wts, tables, tm_x, n, B, 0, nk, (kv0, kv0))
        q_c, k_all, v_all = _mla_prep(l, pd_c, mla_qn, mla_kvn, (wq, None, wk, None, wv), None, tm_c, nctx, B, n, nk,
                                      (k_all, v_all))
        tk = next(t for t in (2816, 1408, 768, 512, 256, 128) if nk % t == 0)
        od_x = _attention(r3(q_x, n), k_all, v_all, tm_x, tk, 0, nk).reshape(B * n, W)

        xs, hx = _outproj(l, (oa_x, ob_x.reshape(B * n, W), oc_x, od_x), w_out_b, xs, mod4, brow_x,
                          g_post_mix, g_pre_ffn, tm_x)
        xs = _ffn(l, hx.reshape(B, n, D), xs.reshape(B, n, D), w_up_b, conv_w, conv_b, w_down_b, mod4, brow_x2,
                  g_post_ffn, tm_x).reshape(B * n, D)

        if need_ctx:
            od_c = _attention(r3(q_c, nctx), k_all, v_all, tm_c, nctx, n, nctx).reshape(B * nctx, W)
            cs, hc = _outproj(l, (oa_c, ob_c.reshape(B * nctx, W), oc_c, od_c), w_out_b, cs, mod4, brow_c,
                              g_post_mix, g_pre_ffn, tm_c)
            cs = _ffn(l, hc.reshape(B, nctx, D), cs.reshape(B, nctx, D), w_up_b, conv_w, conv_b, w_down_b, mod4,
                      brow_c, g_post_ffn, tm_c).reshape(B * nctx, D)
    return xs.reshape(B, n, D)
```

```python
import functools

import jax
import jax.numpy as jnp
from jax import lax
from jax.experimental import pallas as pl
from jax.experimental.pallas import tpu as pltpu

F32 = jnp.float32
BF16 = jnp.bfloat16

EPS = 1e-6
GRID_W = 64
GROUP_W = 256
HEAD_D = 64
MLP_CHUNK = 128
GATE_RANK = 16
GATE_TEMP = 16.0
GLA_CHUNK = 64
GLA_TILE = 256
S5_IN = 16
S5_T = 16
S5_TILE = 128
MLA_HEADS = 4
MLA_NOPE = 64
MLA_ROPE = 32
MLA_V = 64
MLA_Q_RANK = 224
MLA_KV_RANK = 96
MLA_HEAD_PAD = 128
ROPE_BASE = 10000.0
LOG2E = 1.4426950408889634

LANES = 128
VMEM_LIMIT = 48 * 1024 * 1024
TOKEN_TILE = 512
MOD_COL_TILE = 1536
S5_ROW_TILE = 512
ATTN_KV_TILES = (2816, 1408, 768, 512, 256, 128)
MLA_Q_LANES = 256
MLA_KV_LANES = 128


def _cparams(*sem):
    return pltpu.CompilerParams(dimension_semantics=sem, vmem_limit_bytes=VMEM_LIMIT)


def _dot(a, b):
    return jnp.dot(a, b, preferred_element_type=F32)


def _dot_nt(a, b):
    return lax.dot_general(a, b, (((1,), (1,)), ((), ())), preferred_element_type=F32)


def _dot_tn(a, b):
    return lax.dot_general(a, b, (((0,), (0,)), ((), ())), preferred_element_type=F32)


def _split(a):
    hi = a.astype(BF16)
    lo = (a - hi.astype(F32)).astype(BF16)
    return hi, lo


def _dot_x2(a, b_bf16):
    hi, lo = _split(a)
    return _dot(hi, b_bf16) + _dot(lo, b_bf16)


def _dot_x3(a, b):
    ah, al = _split(a)
    bh, bl = _split(b)
    return _dot(ah, bh) + _dot(al, bh) + _dot(ah, bl)


def _rms(x):
    return x * lax.rsqrt(jnp.mean(x * x, axis=-1, keepdims=True) + EPS)


def _gelu(x):
    return 0.5 * x * (1.0 + jnp.tanh(0.7978845608028654 * (x + 0.044715 * (x * x * x))))


def _gelu_gate(a, half_g):
    u = a * (0.7978845608028654 + 0.035677408136300125 * (a * a))
    return (a * half_g) * (1.0 + jnp.tanh(u))


def _sigmoid(x):
    return 1.0 / (1.0 + jnp.exp(-x))


def _lane_group(shape, width):
    return lax.broadcasted_iota(jnp.int32, shape, len(shape) - 1) // width


def _mod_kernel(c_ref, w_ref, b_ref, o_ref):
    c = c_ref[...]
    s = c * _sigmoid(c)
    o_ref[...] = _dot_x3(s, w_ref[...]) + b_ref[...]


def _modulation(c8, w_mod, b_mod):
    L, D, W = w_mod.shape
    tn = MOD_COL_TILE
    return pl.pallas_call(
        _mod_kernel,
        grid=(L, W // tn),
        in_specs=[
            pl.BlockSpec((8, D), lambda l, j: (0, 0)),
            pl.BlockSpec((None, D, tn), lambda l, j: (l, 0, j)),
            pl.BlockSpec((None, 1, tn), lambda l, j: (l, 0, j)),
        ],
        out_specs=pl.BlockSpec((None, 8, tn), lambda l, j: (l, 0, j)),
        out_shape=jax.ShapeDtypeStruct((L, 8, W), F32),
        compiler_params=_cparams("arbitrary", "arbitrary"),
        name="modulation",
    )(c8, w_mod, b_mod.reshape(L, 1, W))


def _mod_spec(l, j, D, bfn):
    return pl.BlockSpec((None, None, 1, D), lambda *g: (l, bfn(*g), 0, j))


def _vec_spec(l, width):
    return pl.BlockSpec((None, 1, width), lambda *g: (l, 0, 0))


IN_SLABS = (("a", 0, 2 * GROUP_W), ("b", 512, 4 * GROUP_W), ("c", 1536, GROUP_W), ("d", 1792, MLA_Q_LANES + MLA_KV_LANES),
            ("g", 2176, LANES))
IN_PAD_COLS = 2304


def _sgu_tile(p, gn_ref, w_ref, b_ref, ones_ref, o_ref):
    tm = p.shape[0]
    g = _gelu(p)
    u = g[:, :GROUP_W]
    v = g[:, GROUP_W:]
    ms = _dot_x2(v * v, ones_ref[...]) * (1.0 / HEAD_D)
    vb = (v * lax.rsqrt(ms + EPS) * gn_ref[...]).astype(BF16)
    head = _lane_group((MLP_CHUNK, GROUP_W), HEAD_D)
    w = w_ref[...]
    for c in range(tm // MLP_CHUNK):
        rows = slice(c * MLP_CHUNK, (c + 1) * MLP_CHUNK)
        r = _dot(w, vb[rows])
        s = b_ref[...]
        for h in range(GROUP_W // HEAD_D):
            s = s + jnp.where(head == h, r[h * MLP_CHUNK:(h + 1) * MLP_CHUNK], 0.0)
        o_ref[rows, :] = (u[rows] * s).astype(o_ref.dtype)


def _inproj_kernel(x_ref, g_ref, sc_ref, sh_ref, w_ref, gn_ref, wsp_ref, bsp_ref, ones_ref, oa, ob, oc, od, og):
    h = _rms(x_ref[...]) * g_ref[...] * (1.0 + sc_ref[...]) + sh_ref[...]
    hb = h.astype(BF16)
    (_, off_a, width_a) = IN_SLABS[0]
    pa = _dot(hb, w_ref[:, off_a:off_a + width_a])
    for (_, off, width), o_ref in zip(IN_SLABS[1:], (ob, oc, od, og)):
        o_ref[...] = _dot(hb, w_ref[:, off:off + width])
    _sgu_tile(pa, gn_ref, wsp_ref, bsp_ref, ones_ref, oa)


def _inproj(l, x2, mod4, brow, g_pre, w_in_p, sgu, tm):
    R, D = x2.shape
    gn, w_st, bias, ones_bd = sgu
    H = GROUP_W // HEAD_D
    grid = (R // tm,)
    outs = [jax.ShapeDtypeStruct((R, GROUP_W), BF16)] + [jax.ShapeDtypeStruct((R, width), F32) for (_, _, width) in IN_SLABS[1:]]
    return pl.pallas_call(
        _inproj_kernel,
        grid=grid,
        in_specs=[
            pl.BlockSpec((tm, D), lambda i: (i, 0)),
            _vec_spec(l, D),
            _mod_spec(l, 1, D, brow),
            _mod_spec(l, 0, D, brow),
            pl.BlockSpec((None, D, IN_PAD_COLS), lambda i: (l, 0, 0)),
            _vec_spec(l, GROUP_W),
            pl.BlockSpec((None, H * MLP_CHUNK, MLP_CHUNK), lambda i: (l, 0, 0)),
            pl.BlockSpec((None, MLP_CHUNK, GROUP_W), lambda i: (l, 0, 0)),
            pl.BlockSpec((GROUP_W, GROUP_W), lambda i: (0, 0)),
        ],
        out_specs=[pl.BlockSpec((tm, GROUP_W), lambda i: (i, 0))]
        + [pl.BlockSpec((tm, width), lambda i: (i, 0)) for (_, _, width) in IN_SLABS[1:]],
        out_shape=outs,
        compiler_params=_cparams("parallel"),
        name="inproj",
    )(x2, g_pre, mod4, mod4, w_in_p, gn, w_st, bias, ones_bd)


def _gla_kernel(*refs, rev, finish):
    if finish:
        (pb_ref, pg_ref, wg_ref, bg_ref, s0_ref, ones_ref, dec_ref, oprev_ref, gn_ref, o_ref, sfin_ref, st_scr) = refs
    else:
        (pb_ref, pg_ref, wg_ref, bg_ref, s0_ref, ones_ref, dec_ref, o_ref, sfin_ref, st_scr) = refs
    i = pl.program_id(1)
    C, W, T = GLA_CHUNK, GROUP_W, GLA_TILE
    H = W // HEAD_D
    nsub = T // C
    ntile = pb_ref.shape[0] // T

    @pl.when(i == 0)
    def _():
        st_scr[...] = s0_ref[...]

    tri = dec_ref[...]
    mask4 = jnp.concatenate([tri.astype(F32)] * H, axis=0)
    head_t = _lane_group((T, W), HEAD_D)
    bd = (lax.broadcasted_iota(jnp.int32, (W, W), 0) // HEAD_D) == _lane_group((W, W), HEAD_D)
    tiles = list(range(ntile - 1, -1, -1)) if rev else list(range(ntile))
    subs = list(range(nsub - 1, -1, -1)) if rev else list(range(nsub))
    rows = lambda a: slice(a * T, (a + 1) * T)
    sub = lambda j: slice(j * C, (j + 1) * C)
    stack = lambda blocks: jnp.concatenate([blocks[j] for j in range(nsub)], axis=0)

    logg = {}
    for a in tiles:
        z = _dot_x3(pg_ref[rows(a), :], wg_ref[...]) + bg_ref[...]
        logg[a] = (jnp.minimum(z, 0.0) - jnp.log(1.0 + jnp.exp(-jnp.abs(z)))) * (1.0 / GATE_TEMP)
    cums = {}
    for a in tiles:
        g_hi, g_lo = _split(logg[a])
        cums[a] = _dot(tri, g_hi) + _dot(tri, g_lo)
    qin, qoff, kin, kend, vb, tot, ptile = {}, {}, {}, {}, {}, {}, {}
    for a in tiles:
        cum = cums[a]
        q = pb_ref[rows(a), 0:W] * (HEAD_D ** -0.5)
        k = pb_ref[rows(a), W:2 * W]
        last_row = (lambda j: j * C) if rev else (lambda j: j * C + C - 1)
        off, run = {}, jnp.zeros((1, W), F32)
        for j in subs:
            tot[(a, j)] = cum[last_row(j):last_row(j) + 1]
            off[j] = run
            run = run + tot[(a, j)]
        ptile[a] = jnp.exp(run)
        e = jnp.exp(cum)
        qin[a] = (q * e).astype(BF16)
        qoff[a] = (q * (e * jnp.exp(stack({j: jnp.broadcast_to(off[j], (C, W)) for j in subs})))).astype(BF16)
        kin[a] = (k * jnp.exp(-cum)).astype(BF16)
        kend[a] = (k * jnp.exp(stack({j: jnp.broadcast_to(tot[(a, j)], (C, W)) for j in subs}) - cum)).astype(BF16)
        vb[a] = pb_ref[rows(a), 2 * W:3 * W].astype(BF16)
    sc = {}
    for a in tiles:
        qst = jnp.concatenate([jnp.where(head_t == h, qin[a], jnp.zeros_like(qin[a])) for h in range(H)], axis=0)
        sc[a] = (_dot_nt(qst, kin[a]) * mask4).astype(BF16)
    ost = {a: _dot(sc[a], vb[a]) for a in tiles}
    kvt = {(a, j): jnp.where(bd, _dot_tn(vb[a][sub(j)], kend[a][sub(j)]), 0.0) for a in tiles for j in subs}
    cross, rend = {}, {}
    for a in tiles:
        r = None
        for j in subs:
            if r is not None:
                cross[(a, j)] = _dot_nt(qin[a][sub(j)], r.astype(BF16))
                r = r * jnp.exp(tot[(a, j)]) + kvt[(a, j)]
            else:
                cross[(a, j)] = jnp.zeros((C, W), F32)
                r = kvt[(a, j)]
        rend[a] = r
    st = st_scr[...]
    inter = {}
    for a in tiles:
        inter[a] = _dot_nt(qoff[a], st.astype(BF16))
        st = st * ptile[a] + rend[a]
    st_scr[...] = st

    for a in tiles:
        o = inter[a] + jnp.concatenate([cross[(a, j)] for j in range(nsub)], axis=0)
        for h in range(H):
            o = o + jnp.where(head_t == h, ost[a][h * T:(h + 1) * T], 0.0)
        if finish:
            o = o + oprev_ref[rows(a), :]
            ms = _dot_x2(o * o, ones_ref[...]) * (1.0 / HEAD_D)
            o = o * lax.rsqrt(ms + EPS) * gn_ref[...]
            r = pb_ref[rows(a), 3 * W:4 * W]
            o_ref[rows(a), :] = (o * (r * _sigmoid(r))).astype(o_ref.dtype)
        else:
            o_ref[rows(a), :] = o

    @pl.when(i == pl.num_programs(1) - 1)
    def _():
        sfin_ref[...] = st_scr[...]


def _gla_decay_matrices():
    T, C = GLA_TILE, GLA_CHUNK
    t = jnp.arange(T)[:, None]
    s = jnp.arange(T)[None, :]
    same = (t // C) == (s // C)
    return jnp.stack([same & (s <= t), same & (s >= t)]).astype(BF16)


def _gla_pass(l, d, pb3, pg3, wg, bg, s0, ones_bd, dec, oprev, gn, tt):
    B, n, _ = pb3.shape
    nt = n // tt
    rev = d == 1
    finish = oprev is not None
    W = GROUP_W

    def tok(b, i):
        return (b, (nt - 1 - i) if rev else i, 0)

    in_specs = [
        pl.BlockSpec((None, tt, 4 * W), tok),
        pl.BlockSpec((None, tt, LANES), tok),
        pl.BlockSpec((None, None, LANES, W), lambda b, i: (l, d, 0, 0)),
        pl.BlockSpec((None, None, 1, W), lambda b, i: (l, d, 0, 0)),
        pl.BlockSpec((None, W, W), lambda b, i: (b, 0, 0)),
        pl.BlockSpec((W, W), lambda b, i: (0, 0)),
        pl.BlockSpec((None, GLA_TILE, GLA_TILE), lambda b, i: (d, 0, 0)),
    ]
    args = [pb3, pg3, wg, bg, s0, ones_bd, dec]
    if finish:
        in_specs += [pl.BlockSpec((None, tt, W), tok), _vec_spec(l, W)]
        args += [oprev, gn]
    return pl.pallas_call(
        functools.partial(_gla_kernel, rev=rev, finish=finish),
        grid=(B, nt),
        in_specs=in_specs,
        out_specs=[pl.BlockSpec((None, tt, W), tok), pl.BlockSpec((None, W, W), lambda b, i: (b, 0, 0))],
        out_shape=[jax.ShapeDtypeStruct((B, n, W), BF16 if finish else F32),
                   jax.ShapeDtypeStruct((B, W, W), F32)],
        scratch_shapes=[pltpu.VMEM((W, W), F32)],
        compiler_params=_cparams("arbitrary", "arbitrary"),
        name="gla_bwd" if rev else "gla_fwd",
    )(*args)


def _gla(l, pb_c, pg_c, pb_x, pg_x, wg, bg, gn, ones_bd, tt_c, tt_x):
    B = pb_x.shape[0]
    zero = jnp.zeros((B, GROUP_W, GROUP_W), F32)
    dec = _gla_decay_matrices()
    ofc, sfc = _gla_pass(l, 0, pb_c, pg_c, wg, bg, zero, ones_bd, dec, None, None, tt_c)
    ofx, _ = _gla_pass(l, 0, pb_x, pg_x, wg, bg, sfc, ones_bd, dec, None, None, tt_x)
    ob_c, sbc = _gla_pass(l, 1, pb_c, pg_c, wg, bg, zero, ones_bd, dec, ofc, gn, tt_c)
    ob_x, _ = _gla_pass(l, 1, pb_x, pg_x, wg, bg, sbc, ones_bd, dec, ofx, gn, tt_x)
    return ob_c, ob_x


def _s5_matrices(a_re, a_im, log_dt, b_re, b_im, c_re, c_im, d_skip):
    T = S5_T
    G, P = a_re.shape[1:]
    I = b_re.shape[-1]
    lam = lax.complex(a_re.astype(F32), a_im.astype(F32))
    ldt = lam * jnp.exp(log_dt.astype(F32))[..., None]
    lam_bar = jnp.exp(ldt)
    b_bar = ((lam_bar - 1.0) / lam)[..., None] * lax.complex(b_re.astype(F32), b_im.astype(F32))
    cmat = lax.complex(c_re.astype(F32), c_im.astype(F32))
    steps = jnp.arange(T + 1, dtype=F32)
    pw = jnp.exp(ldt[..., None] * steps)
    taps = jnp.einsum('dgop,dgpk,dgpi->dgiko', cmat, pw[..., :T], b_bar).real
    taps = taps.at[0, :, :, 0, :].add(jnp.eye(I, dtype=F32)[None] * d_skip.astype(F32)[:, :, None])
    row = T * I
    zeros = jnp.zeros((G, I, row), F32)
    fwd = jnp.concatenate([zeros, taps[0].reshape(G, I, row)], axis=-1)
    bwd = jnp.concatenate([jnp.flip(taps[1], axis=2).reshape(G, I, row), zeros], axis=-1)
    m = jnp.stack([fwd[..., (T - s) * I:(T - s) * I + row] + bwd[..., (T - 1 - s) * I:(T - 1 - s) * I + row]
                   for s in range(T)], axis=1).reshape(G, row, row)

    ar = jnp.arange(T)
    pf = pw[0][..., T - 1 - ar]
    pb = pw[1][..., ar]
    bf = jnp.einsum('gps,gpi->gsip', pf, b_bar[0]).reshape(G, T * I, P)
    bb = jnp.einsum('gps,gpi->gsip', pb, b_bar[1]).reshape(G, T * I, P)
    bmat = jnp.concatenate([bf.real, bf.imag, bf.imag, bf.real, bb.real, bb.imag, bb.imag, bb.real], axis=-1)

    cf = jnp.einsum('gop,gpt->gpto', cmat[0], pw[0][..., 1 + ar]).reshape(G, P, T * I)
    cb = jnp.einsum('gop,gpt->gpto', cmat[1], pw[1][..., T - ar]).reshape(G, P, T * I)
    w = jnp.concatenate([m, cf.real, -cf.imag, cb.real, -cb.imag], axis=1)

    a = pw[..., T]
    acoef = jnp.stack([jnp.concatenate([a[0].real, a[0].real], axis=-1), jnp.concatenate([-a[0].imag, a[0].imag], axis=-1),
                       jnp.concatenate([a[1].real, a[1].real], axis=-1), jnp.concatenate([-a[1].imag, a[1].imag], axis=-1)])
    return bmat.astype(BF16), w.astype(BF16), acoef


def _s5_state_kernel(x_ref, b_ref, o_ref):
    o_ref[...] = _dot(x_ref[...], b_ref[...])


def _s5_state(l, xg, bmat, tr):
    G, R, K = xg.shape
    N = bmat.shape[-1]
    return pl.pallas_call(
        _s5_state_kernel,
        grid=(G, R // tr),
        in_specs=[pl.BlockSpec((None, tr, K), lambda g, i: (g, i, 0)),
                  pl.BlockSpec((None, None, K, N), lambda g, i: (l, g, 0, 0))],
        out_specs=pl.BlockSpec((None, tr, N), lambda g, i: (g, i, 0)),
        out_shape=jax.ShapeDtypeStruct((G, R, N), F32),
        compiler_params=_cparams("parallel", "parallel"),
        name="s5_state",
    )(xg, bmat)


def _s5_scan_kernel(sf_ref, sb_ref, a_ref, h0_ref, hf_ref, hb_ref, hfin_ref, st_scr):
    j = pl.program_id(1)
    tc = sf_ref.shape[1]

    @pl.when(j == 0)
    def _():
        st_scr[...] = h0_ref[...]

    a1f, a2f, a1b, a2b = a_ref[0], a_ref[1], a_ref[2], a_ref[3]
    half = a1f.shape[-1]

    def body(i, hs):
        h1f, h2f, h1b, h2b = hs
        ib = tc - 1 - i
        sf = sf_ref[:, i, :]
        sb = sb_ref[:, ib, :]
        hf_ref[:, i, :] = h1f
        hb_ref[:, ib, :] = h1b
        return (h1f * a1f + h2f * a2f + sf[:, 0:half], h2f * a1f - h1f * a2f + sf[:, half:],
                h1b * a1b + h2b * a2b + sb[:, 0:half], h2b * a1b - h1b * a2b + sb[:, half:])

    hs = lax.fori_loop(0, tc, body, (st_scr[0], st_scr[1], st_scr[2], st_scr[3]))
    for k in range(4):
        st_scr[k] = hs[k]

    @pl.when(j == pl.num_programs(1) - 1)
    def _():
        hfin_ref[...] = st_scr[...]


def _s5_scan(l, s, acoef, h0, B, tc):
    G, R, n2 = s.shape
    nt = R // B // tc
    half = n2 // 4
    fwd = lambda b, j: (0, b * nt + j, 0)
    bwd = lambda b, j: (0, b * nt + nt - 1 - j, 0)
    st_spec = pl.BlockSpec((None, 4, G, half), lambda b, j: (b, 0, 0, 0))
    return pl.pallas_call(
        _s5_scan_kernel,
        grid=(B, nt),
        in_specs=[pl.BlockSpec((G, tc, 2 * half), fwd),
                  pl.BlockSpec((G, tc, 2 * half), lambda b, j: (0, b * nt + nt - 1 - j, 1)),
                  pl.BlockSpec((None, 4, G, half), lambda b, j: (l, 0, 0, 0)),
                  st_spec],
        out_specs=[pl.BlockSpec((G, tc, half), fwd), pl.BlockSpec((G, tc, half), bwd), st_spec],
        out_shape=[jax.ShapeDtypeStruct((G, R, half), F32), jax.ShapeDtypeStruct((G, R, half), F32),
                   jax.ShapeDtypeStruct((B, 4, G, half), F32)],
        scratch_shapes=[pltpu.VMEM((4, G, half), F32)],
        compiler_params=_cparams("arbitrary", "arbitrary"),
        name="s5_scan",
    )(s, s, acoef, h0)


def _s5_out_kernel(x_ref, hf_ref, hb_ref, w_ref, o_ref):
    k = x_ref.shape[-1]
    kh = hf_ref.shape[-1]
    y = _dot(x_ref[...], w_ref[0:k, :])
    y = y + _dot(hf_ref[...].astype(BF16), w_ref[k:k + kh, :])
    o_ref[...] = y + _dot(hb_ref[...].astype(BF16), w_ref[k + kh:, :])


def _s5_out(l, xg, hf, hb, w, tr):
    G, R, K = xg.shape
    kh = hf.shape[-1]
    row = lambda width: pl.BlockSpec((None, tr, width), lambda g, i: (g, i, 0))
    return pl.pallas_call(
        _s5_out_kernel,
        grid=(G, R // tr),
        in_specs=[row(K), row(kh), row(kh), pl.BlockSpec((None, None, K + 2 * kh, K), lambda g, i: (l, g, 0, 0))],
        out_specs=row(K),
        out_shape=jax.ShapeDtypeStruct((G, R, K), F32),
        compiler_params=_cparams("parallel", "parallel"),
        name="s5_out",
    )(xg, hf, hb, w)


def _s5_finish_kernel(y_ref, w_ref, b_ref, o_ref):
    y = _gelu(y_ref[...])
    gate = _sigmoid(_dot(y.astype(BF16), w_ref[...]) + b_ref[...])
    o_ref[...] = (y * gate).astype(o_ref.dtype)


def _s5_finish(l, y2, w_glu, b_glu, tm):
    R, W = y2.shape
    return pl.pallas_call(
        _s5_finish_kernel,
        grid=(R // tm,),
        in_specs=[pl.BlockSpec((tm, W), lambda i: (i, 0)),
                  pl.BlockSpec((None, W, W), lambda i: (l, 0, 0)),
                  _vec_spec(l, W)],
        out_specs=pl.BlockSpec((tm, W), lambda i: (i, 0)),
        out_shape=jax.ShapeDtypeStruct((R, W), BF16),
        compiler_params=_cparams("parallel"),
        name="s5_finish",
    )(y2, w_glu, b_glu)


def _to_groups(u3):
    B, n, W = u3.shape
    G = W // S5_IN
    x = u3.astype(BF16).reshape(B, n // S5_T, S5_T, G, S5_IN).transpose(3, 0, 1, 2, 4)
    return x.reshape(G, B * (n // S5_T), S5_T * S5_IN)


def _from_groups(y, B):
    G, R, _ = y.shape
    nc = R // B
    y = y.reshape(G, B, nc, S5_T, S5_IN).transpose(1, 2, 3, 0, 4)
    return y.reshape(B * nc * S5_T, G * S5_IN)


def _s5_gather_kernel(x_ref, o_ref, xs_scr, xt_scr):
    G, tc, _ = o_ref.shape
    nh = x_ref.shape[1] // LANES
    gh = G // nh
    for h in range(nh):
        xs_scr[h] = x_ref[:, h * LANES:(h + 1) * LANES]
    for t in range(S5_T):
        for h in range(nh):
            rt = xs_scr[h, pl.ds(t, tc, stride=S5_T), :].T
            for g in range(gh):
                xt_scr[h * gh + g, t * S5_IN:(t + 1) * S5_IN, :] = rt[g * S5_IN:(g + 1) * S5_IN, :]
    for g in range(G):
        o_ref[g] = xt_scr[g].T.astype(o_ref.dtype)


def _s5_gather(x2, tc):
    N, W = x2.shape
    G = W // S5_IN
    R = N // S5_T
    return pl.pallas_call(
        _s5_gather_kernel,
        grid=(R // tc,),
        in_specs=[pl.BlockSpec((tc * S5_T, W), lambda i: (i, 0))],
        out_specs=pl.BlockSpec((G, tc, S5_T * S5_IN), lambda i: (0, i, 0)),
        out_shape=jax.ShapeDtypeStruct((G, R, S5_T * S5_IN), BF16),
        scratch_shapes=[pltpu.VMEM((W // LANES, tc * S5_T, LANES), F32), pltpu.VMEM((G, S5_T * S5_IN, tc), F32)],
        compiler_params=_cparams("parallel"),
        name="s5_gather",
    )(x2)


def _s5_scatter_kernel(y_ref, w_ref, b_ref, o_ref, yt_scr, tok_scr):
    G, tc, _ = y_ref.shape
    nh = tok_scr.shape[0]
    gh = G // nh
    for g in range(G):
        yt = y_ref[g].T
        for t in range(S5_T):
            yt_scr[t, g // gh, (g % gh) * S5_IN:(g % gh + 1) * S5_IN, :] = yt[t * S5_IN:(t + 1) * S5_IN, :]
    for t in range(S5_T):
        for h in range(nh):
            tok_scr[h, pl.ds(t, tc, stride=S5_T), :] = yt_scr[t, h].T
    y = _gelu(jnp.concatenate([tok_scr[h] for h in range(nh)], axis=1))
    gate = _sigmoid(_dot(y.astype(BF16), w_ref[...]) + b_ref[...])
    o_ref[...] = (y * gate).astype(o_ref.dtype)


def _s5_scatter_finish(l, yg, w_glu, b_glu, tc):
    G, R, K = yg.shape
    W = G * S5_IN
    return pl.pallas_call(
        _s5_scatter_kernel,
        grid=(R // tc,),
        in_specs=[pl.BlockSpec((G, tc, K), lambda i: (0, i, 0)),
                  pl.BlockSpec((None, W, W), lambda i: (l, 0, 0)),
                  _vec_spec(l, W)],
        out_specs=pl.BlockSpec((tc * S5_T, W), lambda i: (i, 0)),
        out_shape=jax.ShapeDtypeStruct((R * S5_T, W), BF16),
        scratch_shapes=[pltpu.VMEM((S5_T, W // LANES, LANES, tc), F32), pltpu.VMEM((W // LANES, tc * S5_T, LANES), F32)],
        compiler_params=_cparams("parallel"),
        name="s5_scatter",
    )(yg, w_glu, b_glu)


def _s5(l, pc_c, pc_x, mats, w_glu, b_glu, need_ctx, tm_c):
    bmat, w, acoef = mats
    B, n, W = pc_x.shape
    G = W // S5_IN
    xg_c = _to_groups(pc_c)
    tc_x = min(S5_TILE, n // S5_T)
    xg_x = _s5_gather(pc_x.reshape(B * n, W), tc_x)
    tr_c = min(S5_ROW_TILE, xg_c.shape[1])
    tr_x = min(S5_ROW_TILE, xg_x.shape[1])
    h0 = jnp.zeros((B, 4, G, acoef.shape[-1]), F32)
    hf_c, hb_c, h1 = _s5_scan(l, _s5_state(l, xg_c, bmat, tr_c), acoef, h0, B, xg_c.shape[1] // B)
    hf_x, hb_x, _ = _s5_scan(l, _s5_state(l, xg_x, bmat, tr_x), acoef, h1, B, tc_x)
    oc_x = _s5_scatter_finish(l, _s5_out(l, xg_x, hf_x, hb_x, w, tr_x), w_glu, b_glu, tc_x)
    oc_c = None
    if need_ctx:
        y_c = _from_groups(_s5_out(l, xg_c, hf_c, hb_c, w, tr_c), B)
        oc_c = _s5_finish(l, y_c, w_glu, b_glu, tm_c)
    return oc_c, oc_x


def _mla_prep_kernel(*refs, rope):
    refs = refs[:-5] + refs[-3:]
    if rope:
        pd_ref, qn_ref, kvn_ref, wq_ref, wqs_ref, wk_ref, wks_ref, wv_ref, cos_ref, sin_ref, q_ref, k_ref, v_ref = refs
    else:
        pd_ref, qn_ref, kvn_ref, wq_ref, wk_ref, wv_ref, q_ref, k_ref, v_ref = refs
    cq = pd_ref[:, 0:MLA_Q_LANES]
    ms = jnp.sum(cq * cq, axis=-1, keepdims=True) * (1.0 / MLA_Q_RANK)
    cqn = (cq * lax.rsqrt(ms + EPS) * qn_ref[...]).astype(BF16)
    ck = pd_ref[:, MLA_Q_LANES:MLA_Q_LANES + MLA_KV_LANES]
    lane = lax.broadcasted_iota(jnp.int32, ck.shape, 1)
    is_lat = lane < MLA_KV_RANK
    ms = jnp.sum(jnp.where(is_lat, ck * ck, 0.0), axis=-1, keepdims=True) * (1.0 / MLA_KV_RANK)
    ckn = jnp.where(is_lat, ck * lax.rsqrt(ms + EPS) * kvn_ref[...], ck).astype(BF16)
    q = _dot(cqn, wq_ref[...])
    k = _dot(ckn, wk_ref[...])
    if rope:
        cos = jnp.concatenate([cos_ref[...]] * MLA_HEADS, axis=1)
        sin = jnp.concatenate([sin_ref[...]] * MLA_HEADS, axis=1)
        q = q * cos + _dot(cqn, wqs_ref[...]) * sin
        k = k * cos + _dot(ckn, wks_ref[...]) * sin
    q_ref[...] = (q * ((MLA_NOPE + MLA_ROPE) ** -0.5 * LOG2E)).astype(BF16)
    k_ref[...] = k.astype(BF16)
    v_ref[...] = _dot(ckn, wv_ref[...]).astype(BF16)


def _mla_prep(l, pd, qn, kvn, wts, tables, tm, n, B, row0, nk, kv_into):
    R = pd.shape[0]
    HP = MLA_HEADS * MLA_HEAD_PAD
    wq, wqs, wk, wks, wv = wts
    rope = tables is not None
    npt = n // tm
    wspec = lambda r: pl.BlockSpec((None, r, HP), lambda i: (l, 0, 0))
    QL, KL = MLA_Q_LANES, MLA_KV_LANES
    in_specs = [pl.BlockSpec((tm, QL + KL), lambda i: (i, 0)), _vec_spec(l, QL), _vec_spec(l, KL)]
    if rope:
        tspec = pl.BlockSpec((tm, MLA_HEAD_PAD), lambda i: (i % npt, 0))
        in_specs += [wspec(QL), wspec(QL), wspec(KL), wspec(KL), wspec(KL), tspec, tspec]
        args = (pd, qn, kvn, wq, wqs, wk, wks, wv) + tuple(tables)
    else:
        in_specs += [wspec(QL), wspec(KL), wspec(KL)]
        args = (pd, qn, kvn, wq, wk, wv)
    aliases = {len(args): 1, len(args) + 1: 2}
    in_specs += [pl.BlockSpec(memory_space=pl.ANY)] * 2
    args = args + tuple(kv_into)
    kv_spec = pl.BlockSpec((None, tm, HP), lambda i: (i // npt, row0 // tm + i % npt, 0))
    kv_shape = jax.ShapeDtypeStruct((B, nk, HP), BF16)
    return pl.pallas_call(
        functools.partial(_mla_prep_kernel, rope=rope),
        grid=(R // tm,),
        in_specs=in_specs,
        out_specs=[pl.BlockSpec((tm, HP), lambda i: (i, 0)), kv_spec, kv_spec],
        out_shape=[jax.ShapeDtypeStruct((R, HP), BF16), kv_shape, kv_shape],
        input_output_aliases=aliases,
        compiler_params=_cparams("parallel"),
        name="mla_prep",
    )(*args)


def _attn_kernel(q_ref, k_ref, v_ref, o_ref, m_scr, l_scr, acc_scr):
    kv = pl.program_id(2)
    HP = MLA_HEAD_PAD
    tk = k_ref.shape[0]

    @pl.when(kv == 0)
    def _():
        m_scr[...] = jnp.full(m_scr.shape, -jnp.inf, F32)
        l_scr[...] = jnp.zeros(l_scr.shape, F32)
        acc_scr[...] = jnp.zeros(acc_scr.shape, F32)

    def scores(h):
        lanes = slice(h * HP, (h + 1) * HP)
        return _dot_nt(q_ref[:, lanes], k_ref[:, lanes])

    s_next = scores(0)
    for h in range(MLA_HEADS):
        lanes = slice(h * HP, (h + 1) * HP)
        s = s_next
        if h + 1 < MLA_HEADS:
            s_next = scores(h + 1)
        m_prev = m_scr[h]
        m_new = jnp.maximum(m_prev, jnp.max(s, axis=1, keepdims=True))
        alpha = jnp.exp2(m_prev - m_new)
        p = jnp.exp2(s - m_new[:, 0:1])
        lp = p[:, 0:LANES]
        for c in range(1, tk // LANES):
            lp = lp + p[:, c * LANES:(c + 1) * LANES]
        l_scr[h] = alpha * l_scr[h] + lp
        acc_scr[h] = alpha * acc_scr[h] + _dot(p.astype(BF16), v_ref[:, lanes])
        m_scr[h] = m_new

    @pl.when(kv == pl.num_programs(2) - 1)
    def _():
        low = lax.broadcasted_iota(jnp.int32, acc_scr.shape[1:], 1) < MLA_V
        norm = lambda h: acc_scr[h] * (1.0 / jnp.sum(l_scr[h], axis=1, keepdims=True))
        outs = [jnp.where(low, norm(h), norm(h + 1)) for h in range(0, MLA_HEADS, 2)]
        o_ref[...] = jnp.concatenate(outs, axis=1).astype(o_ref.dtype)


def _attention(q3, k3, v3, tq, tk, k0, nk):
    B, nq, HP = q3.shape
    kb = k0 // tk
    return pl.pallas_call(
        _attn_kernel,
        grid=(B, nq // tq, nk // tk),
        in_specs=[pl.BlockSpec((None, tq, HP), lambda b, i, j: (b, i, 0)),
                  pl.BlockSpec((None, tk, HP), lambda b, i, j: (b, kb + j, 0)),
                  pl.BlockSpec((None, tk, HP), lambda b, i, j: (b, kb + j, 0))],
        out_specs=pl.BlockSpec((None, tq, MLA_HEADS * MLA_V), lambda b, i, j: (b, i, 0)),
        out_shape=jax.ShapeDtypeStruct((B, nq, MLA_HEADS * MLA_V), BF16),
        scratch_shapes=[pltpu.VMEM((MLA_HEADS, tq, LANES), F32),
                        pltpu.VMEM((MLA_HEADS, tq, LANES), F32),
                        pltpu.VMEM((MLA_HEADS, tq, MLA_HEAD_PAD), F32)],
        compiler_params=_cparams("parallel", "parallel", "arbitrary"),
        name="attention",
    )(q3, k3, v3)


def _outproj_kernel(oa, ob, oc, od, w_ref, x_ref, gpost_ref, gt_ref, gpre_ref, sc_ref, sh_ref, xo_ref, h_ref):
    W = GROUP_W
    mix = _dot(oa[...], w_ref[0:W, :])
    mix = mix + _dot(ob[...], w_ref[W:2 * W, :])
    mix = mix + _dot(oc[...], w_ref[2 * W:3 * W, :])
    mix = mix + _dot(od[...], w_ref[3 * W:4 * W, :])
    x = x_ref[...] + gt_ref[...] * (_rms(mix) * gpost_ref[...])
    xo_ref[...] = x
    h_ref[...] = (_rms(x) * gpre_ref[...] * (1.0 + sc_ref[...]) + sh_ref[...]).astype(h_ref.dtype)


def _outproj(l, parts, w_out, x2, mod4, brow, g_post, g_pre_ffn, tm):
    R, D = x2.shape
    W = GROUP_W
    part_spec = pl.BlockSpec((tm, W), lambda i: (i, 0))
    row_spec = pl.BlockSpec((tm, D), lambda i: (i, 0))
    return pl.pallas_call(
        _outproj_kernel,
        grid=(R // tm,),
        in_specs=[part_spec] * 4 + [
            pl.BlockSpec((None, 4 * W, D), lambda i: (l, 0, 0)),
            row_spec,
            _vec_spec(l, D),
            _mod_spec(l, 2, D, brow),
            _vec_spec(l, D),
            _mod_spec(l, 4, D, brow),
            _mod_spec(l, 3, D, brow),
        ],
        out_specs=[row_spec, row_spec],
        out_shape=[jax.ShapeDtypeStruct((R, D), F32), jax.ShapeDtypeStruct((R, D), BF16)],
        compiler_params=_cparams("parallel"),
        name="outproj",
    )(*parts, w_out, x2, g_post, mod4, g_pre_ffn, mod4, mod4)


FFN_HALO = 16
FFN_COLS = 256


def _ffn_kernel(hp_ref, h_ref, hn_ref, wup_ref, cw_ref, cb_ref, wdn_ref, x_ref, gpost_ref, gt_ref, o_ref, y_scr):
    i = pl.program_id(1)
    tm = h_ref.shape[0]
    dff = wdn_ref.shape[0]
    use = FFN_HALO // 2
    rows = tm + 2 * use
    prev = jnp.where(i == 0, jnp.zeros_like(hp_ref[...]), hp_ref[...])[FFN_HALO - use:]
    nxt = jnp.where(i == pl.num_programs(1) - 1, jnp.zeros_like(hn_ref[...]), hn_ref[...])[:use]
    hb = jnp.concatenate([prev, h_ref[...], nxt], axis=0)

    def conv(z, cols):
        w = cw_ref[:, cols]
        out = cb_ref[:, cols] + w[0:1] * pltpu.roll(z, 1, axis=0) + w[1:2] * z + w[2:3] * pltpu.roll(z, rows - 1, axis=0)
        return out[use:use + tm]

    def cols(j):
        return slice(j * FFN_COLS, (j + 1) * FFN_COLS), slice(dff + j * FFN_COLS, dff + (j + 1) * FFN_COLS)

    def up(j):
        ca, cg = cols(j)
        return _dot(hb, wup_ref[:, ca]), _dot(hb, wup_ref[:, cg])

    nchunks = dff // FFN_COLS
    z_next = up(0)
    for j in range(nchunks):
        ca, cg = cols(j)
        za, zg = z_next
        if j + 1 < nchunks:
            z_next = up(j + 1)
        y_scr[:, ca] = _gelu_gate(conv(za, ca), conv(zg, cg)).astype(BF16)
    acc = _dot(y_scr[...], wdn_ref[...])
    o_ref[...] = x_ref[...] + gt_ref[...] * (_rms(acc) * gpost_ref[...])


def _ffn(l, h3, x3, w_up, conv_w, conv_b, w_down, mod4, brow, g_post, tm):
    B, n, D = x3.shape
    nt = n // tm
    hb = tm // FFN_HALO
    nh = n // FFN_HALO
    dff = w_down.shape[1]
    once = pl.Buffered(1)
    return pl.pallas_call(
        _ffn_kernel,
        grid=(B, nt),
        in_specs=[
            pl.BlockSpec((None, FFN_HALO, D), lambda b, i: (b, jnp.maximum(i * hb - 1, 0), 0)),
            pl.BlockSpec((None, tm, D), lambda b, i: (b, i, 0)),
            pl.BlockSpec((None, FFN_HALO, D), lambda b, i: (b, jnp.minimum((i + 1) * hb, nh - 1), 0)),
            pl.BlockSpec((None, D, 2 * dff), lambda b, i: (l, 0, 0), pipeline_mode=once),
            pl.BlockSpec((None, 3, 2 * dff), lambda b, i: (l, 0, 0)),
            pl.BlockSpec((None, 1, 2 * dff), lambda b, i: (l, 0, 0)),
            pl.BlockSpec((None, dff, D), lambda b, i: (l, 0, 0), pipeline_mode=once),
            pl.BlockSpec((None, tm, D), lambda b, i: (b, i, 0)),
            _vec_spec(l, D),
            _mod_spec(l, 5, D, brow),
        ],
        out_specs=pl.BlockSpec((None, tm, D), lambda b, i: (b, i, 0)),
        out_shape=jax.ShapeDtypeStruct((B, n, D), F32),
        scratch_shapes=[pltpu.VMEM((tm, dff), BF16)],
        compiler_params=_cparams("parallel", "parallel"),
        name="conv_ffn",
    )(h3, h3, h3, w_up, conv_w, conv_b, w_down, x3, g_post, mod4)


def _prep_w_in(w_in):
    w_in = w_in.astype(BF16)
    z = lambda n: jnp.zeros(w_in.shape[:-1] + (n,), w_in.dtype)
    a = w_in[..., 0:512]
    b = w_in[..., 512:1536]
    gl = w_in[..., 1536:1568]
    c = w_in[..., 1568:1824]
    cq = w_in[..., 1824:2048]
    ckv_kr = w_in[..., 2048:2176]
    return jnp.concatenate([a, b, c, cq, z(32), ckv_kr, gl, z(96)], axis=-1)


def _rope_swap(t):
    q = MLA_ROPE // 4
    return jnp.concatenate([t[..., q:2 * q], t[..., 0:q], t[..., 3 * q:4 * q], t[..., 2 * q:3 * q]], axis=-1)


def _prep_mla(w_uq, w_ukv):
    L = w_uq.shape[0]
    H, NP, RP, HP = MLA_HEADS, MLA_NOPE, MLA_ROPE, MLA_HEAD_PAD
    wq = w_uq.reshape(L, MLA_Q_RANK, H, NP + RP)
    zq = jnp.zeros((L, MLA_Q_RANK, H, HP - NP - RP), w_uq.dtype)
    znope = jnp.zeros((L, MLA_Q_RANK, H, NP), w_uq.dtype)
    q_main = jnp.concatenate([wq, zq], axis=-1)
    q_swap = jnp.concatenate([znope, _rope_swap(wq[..., NP:]), zq], axis=-1)
    padq = lambda w: jnp.pad(w.reshape(L, MLA_Q_RANK, H * HP), ((0, 0), (0, MLA_Q_LANES - MLA_Q_RANK), (0, 0)))

    wkv = w_ukv.reshape(L, MLA_KV_RANK, H, NP + MLA_V)
    zk = jnp.zeros((L, MLA_KV_RANK, H, HP - NP), w_ukv.dtype)
    k_lat = jnp.concatenate([wkv[..., :NP], zk], axis=-1)
    eye = jnp.eye(RP, dtype=w_ukv.dtype)
    place = lambda e: jnp.broadcast_to(
        jnp.concatenate([jnp.zeros((RP, NP), e.dtype), e, jnp.zeros((RP, HP - NP - RP), e.dtype)], axis=-1)[None, :, None, :],
        (L, RP, H, HP))
    k_main = jnp.concatenate([k_lat, place(eye)], axis=1)
    k_swap = jnp.concatenate([jnp.zeros_like(k_lat), place(_rope_swap(eye))], axis=1)
    zv = jnp.zeros((L, MLA_KV_RANK, H, HP - MLA_V), w_ukv.dtype)
    odd = (jnp.arange(H) % 2 == 1)[None, None, :, None]
    v_lat = jnp.where(odd, jnp.concatenate([zv, wkv[..., NP:]], axis=-1), jnp.concatenate([wkv[..., NP:], zv], axis=-1))
    v_main = jnp.concatenate([v_lat, jnp.zeros((L, RP, H, HP), w_ukv.dtype)], axis=1)
    flat = lambda w: w.reshape(L, w.shape[1], H * HP).astype(BF16)
    return (padq(q_main).astype(BF16), padq(q_swap).astype(BF16), flat(k_main), flat(k_swap), flat(v_main))


def _rope_tables(n):
    rows = n // GRID_W
    nf = MLA_ROPE // 4
    inv = ROPE_BASE ** (-jnp.arange(nf, dtype=F32) / nf)
    ar = jnp.arange(rows, dtype=F32)[:, None] * inv[None, :]
    ac = jnp.arange(GRID_W, dtype=F32)[:, None] * inv[None, :]
    by_row = lambda t: jnp.repeat(t, GRID_W, axis=0)
    by_col = lambda t: jnp.tile(t, (rows, 1))
    cr, sr, cc, sn = by_row(jnp.cos(ar)), by_row(jnp.sin(ar)), by_col(jnp.cos(ac)), by_col(jnp.sin(ac))
    one = jnp.ones((n, MLA_NOPE), F32)
    zero = jnp.zeros((n, MLA_HEAD_PAD - MLA_NOPE - MLA_ROPE), F32)
    cos = jnp.concatenate([one, cr, cr, cc, cc, zero], axis=1)
    sin = jnp.concatenate([0.0 * one, -sr, sr, -sn, sn, zero], axis=1)
    return cos, sin


def _pick_tile(n, want):
    t = min(n, want)
    while n % t:
        t //= 2
    return t


def kernel(x, c, ctx, c_ctx, w_mod, b_mod, g_pre_mix, g_post_mix, g_pre_ffn, g_post_ffn, w_in,
           sgu_norm, sgu_w, sgu_b, gla_w_gate, gla_b_gate, gla_norm,
           s5_a_re, s5_a_im, s5_log_dt, s5_b_re, s5_b_im, s5_c_re, s5_c_im, s5_d, s5_w_glu, s5_b_glu,
           mla_q_norm, mla_w_uq, mla_kv_norm, mla_w_ukv, w_out,
           ffn_w_up, ffn_conv_w, ffn_conv_b, ffn_w_down):
    B, n, D = x.shape
    nctx = ctx.shape[1]
    L = w_mod.shape[0]
    W = GROUP_W
    assert B < 8 and n % TOKEN_TILE == 0 and nctx % GLA_TILE == 0 and n % GRID_W == 0

    c8 = jnp.concatenate([c, c_ctx[None, :], jnp.zeros((8 - B - 1, D), F32)], axis=0)
    mod4 = _modulation(c8, w_mod, b_mod).reshape(L, 8, 1, 6 * D)
    vec = lambda p: p.reshape(L, 1, -1).astype(F32)
    g_pre_mix, g_post_mix, g_pre_ffn, g_post_ffn = map(vec, (g_pre_mix, g_post_mix, g_pre_ffn, g_post_ffn))
    w_in_p = _prep_w_in(w_in)
    sgu_gn = vec(sgu_norm)
    sgu_w_st = sgu_w.reshape(L, -1, MLP_CHUNK).astype(BF16)
    sgu_bias = jnp.repeat(jnp.swapaxes(sgu_b, 1, 2), HEAD_D, axis=2).astype(F32)
    ones_bd = jnp.kron(jnp.eye(W // HEAD_D, dtype=F32), jnp.ones((HEAD_D, HEAD_D), F32)).astype(BF16)
    gla_wg = jnp.zeros((L, 2, LANES, W), F32)
    gla_wg = gla_wg.at[:, 0, 0:GATE_RANK].set(gla_w_gate[:, 0]).at[:, 1, GATE_RANK:2 * GATE_RANK].set(gla_w_gate[:, 1])
    gla_bg = gla_b_gate.reshape(L, 2, 1, W).astype(F32)
    gla_gn = vec(gla_norm)
    s5_wglu = s5_w_glu.astype(BF16)
    s5_bglu = vec(s5_b_glu)
    mla_qn = jnp.pad(mla_q_norm, ((0, 0), (0, MLA_Q_LANES - MLA_Q_RANK))).reshape(L, 1, MLA_Q_LANES).astype(F32)
    mla_kvn = jnp.pad(mla_kv_norm, ((0, 0), (0, MLA_KV_LANES - MLA_KV_RANK))).reshape(L, 1, MLA_KV_LANES).astype(F32)
    mla_wts = _prep_mla(mla_w_uq, mla_w_ukv)
    tables = _rope_tables(n)
    w_out_b = w_out.astype(BF16)
    w_up_b = ffn_w_up.astype(BF16)
    w_down_b = ffn_w_down.astype(BF16)
    dff = ffn_w_down.shape[1]
    half_gate = jnp.concatenate([jnp.ones((dff,), F32), jnp.full((dff,), 0.5, F32)])
    conv_w = ffn_conv_w.astype(F32) * half_gate
    conv_b = (ffn_conv_b.astype(F32) * half_gate).reshape(L, 1, -1)

    s5_mats = jax.vmap(_s5_matrices)(s5_a_re, s5_a_im, s5_log_dt, s5_b_re, s5_b_im, s5_c_re, s5_c_im, s5_d)

    tm_x = _pick_tile(n, TOKEN_TILE)
    tm_c = _pick_tile(nctx, TOKEN_TILE)
    tpb_x = n // tm_x
    brow_x = lambda i: i // tpb_x
    brow_c = lambda *g: B
    brow_x2 = lambda b, i: b

    xs = x.reshape(B * n, D)
    cs = ctx.reshape(B * nctx, D)

    for l in range(L):
        need_ctx = l < L - 1
        sgu = (sgu_gn, sgu_w_st, sgu_bias, ones_bd)
        oa_x, pb_x, pc_x, pd_x, pg_x = _inproj(l, xs, mod4, brow_x, g_pre_mix, w_in_p, sgu, tm_x)
        oa_c, pb_c, pc_c, pd_c, pg_c = _inproj(l, cs, mod4, brow_c, g_pre_mix, w_in_p, sgu, tm_c)

        r3 = lambda t, m: t.reshape(B, m, t.shape[-1])
        ob_c, ob_x = _gla(l, r3(pb_c, nctx), r3(pg_c, nctx), r3(pb_x, n), r3(pg_x, n),
                          gla_wg, gla_bg, gla_gn, ones_bd, _pick_tile(nctx, GLA_TILE), _pick_tile(n, 4 * GLA_TILE))
        oc_c, oc_x = _s5(l, r3(pc_c, nctx), r3(pc_x, n), s5_mats, s5_wglu, s5_bglu, need_ctx, tm_c)
        wq, wqs, wk, wks, wv = mla_wts
        nk = n + nctx
        kv0 = jnp.zeros((B, nk, MLA_HEADS * MLA_HEAD_PAD), BF16)
        q_x, k_all, v_all = _mla_prep(l, pd_x, mla_qn, mla_kvn, mla_wts, tables, tm_x, n, B, 0, nk, (kv0, kv0))
        q_c, k_all, v_all = _mla_prep(l, pd_c, mla_qn, mla_kvn, (wq, None, wk, None, wv), None, tm_c, nctx, B, n, nk,
                                      (k_all, v_all))
        tk = next(t for t in ATTN_KV_TILES if nk % t == 0)
        od_x = _attention(r3(q_x, n), k_all, v_all, tm_x, tk, 0, nk).reshape(B * n, W)

        xs, hx = _outproj(l, (oa_x, ob_x.reshape(B * n, W), oc_x, od_x), w_out_b, xs, mod4, brow_x,
                          g_post_mix, g_pre_ffn, tm_x)
        xs = _ffn(l, hx.reshape(B, n, D), xs.reshape(B, n, D), w_up_b, conv_w, conv_b, w_down_b, mod4, brow_x2,
                  g_post_ffn, tm_x).reshape(B * n, D)

        if need_ctx:
            od_c = _attention(r3(q_c, nctx), k_all, v_all, tm_c, nctx, n, nctx).reshape(B * nctx, W)
            cs, hc = _outproj(l, (oa_c, ob_c.reshape(B * nctx, W), oc_c, od_c), w_out_b, cs, mod4, brow_c,
                              g_post_mix, g_pre_ffn, tm_c)
            cs = _ffn(l, hc.reshape(B, nctx, D), cs.reshape(B, nctx, D), w_up_b, conv_w, conv_b, w_down_b, mod4,
                      brow_c, g_post_ffn, tm_c).reshape(B * nctx, D)
    return xs.reshape(B, n, D)
```

```python
import functools

import jax
import jax.numpy as jnp
from jax import lax
from jax.experimental import pallas as pl
from jax.experimental.pallas import tpu as pltpu

F32 = jnp.float32
BF16 = jnp.bfloat16

EPS = 1e-6
GRID_W = 64
GROUP_W = 256
HEAD_D = 64
MLP_CHUNK = 128
GATE_RANK = 16
GATE_TEMP = 16.0
GLA_CHUNK = 64
GLA_TILE = 256
S5_IN = 16
S5_T = 16
S5_TILE = 128
S5_PITCH_PAD = 8
MLA_HEADS = 4
MLA_NOPE = 64
MLA_ROPE = 32
MLA_V = 64
MLA_Q_RANK = 224
MLA_KV_RANK = 96
MLA_HEAD_PAD = 128
ROPE_BASE = 10000.0
LOG2E = 1.4426950408889634

LANES = 128
VMEM_LIMIT = 48 * 1024 * 1024
TOKEN_TILE = 512
MOD_COL_TILE = 1536
S5_ROW_TILE = 512
ATTN_KV_TILES = (2816, 1408, 768, 512, 256, 128)
MLA_Q_LANES = 256
MLA_KV_LANES = 128


def _cparams(*sem):
    return pltpu.CompilerParams(dimension_semantics=sem, vmem_limit_bytes=VMEM_LIMIT)


def _dot(a, b):
    return jnp.dot(a, b, preferred_element_type=F32)


def _dot_nt(a, b):
    return lax.dot_general(a, b, (((1,), (1,)), ((), ())), preferred_element_type=F32)


def _dot_tn(a, b):
    return lax.dot_general(a, b, (((0,), (0,)), ((), ())), preferred_element_type=F32)


def _split(a):
    hi = a.astype(BF16)
    lo = (a - hi.astype(F32)).astype(BF16)
    return hi, lo


def _dot_x2(a, b_bf16):
    hi, lo = _split(a)
    return _dot(hi, b_bf16) + _dot(lo, b_bf16)


def _dot_x3(a, b):
    ah, al = _split(a)
    bh, bl = _split(b)
    return _dot(ah, bh) + _dot(al, bh) + _dot(ah, bl)


def _rms(x):
    return x * lax.rsqrt(jnp.mean(x * x, axis=-1, keepdims=True) + EPS)


def _gelu(x):
    return 0.5 * x * (1.0 + jnp.tanh(0.7978845608028654 * (x + 0.044715 * (x * x * x))))


def _gelu_gate(a, half_g):
    u = a * (0.7978845608028654 + 0.035677408136300125 * (a * a))
    return (a * half_g) * (1.0 + jnp.tanh(u))


def _sigmoid(x):
    return 1.0 / (1.0 + jnp.exp(-x))


def _lane_group(shape, width):
    return lax.broadcasted_iota(jnp.int32, shape, len(shape) - 1) // width


def _mod_kernel(c_ref, w_ref, b_ref, o_ref):
    c = c_ref[...]
    s = c * _sigmoid(c)
    o_ref[...] = _dot_x3(s, w_ref[...]) + b_ref[...]


def _modulation(c8, w_mod, b_mod):
    L, D, W = w_mod.shape
    tn = MOD_COL_TILE
    return pl.pallas_call(
        _mod_kernel,
        grid=(L, W // tn),
        in_specs=[
            pl.BlockSpec((8, D), lambda l, j: (0, 0)),
            pl.BlockSpec((None, D, tn), lambda l, j: (l, 0, j)),
            pl.BlockSpec((None, 1, tn), lambda l, j: (l, 0, j)),
        ],
        out_specs=pl.BlockSpec((None, 8, tn), lambda l, j: (l, 0, j)),
        out_shape=jax.ShapeDtypeStruct((L, 8, W), F32),
        compiler_params=_cparams("arbitrary", "arbitrary"),
        name="modulation",
    )(c8, w_mod, b_mod.reshape(L, 1, W))


def _mod_spec(l, j, D, bfn):
    return pl.BlockSpec((None, None, 1, D), lambda *g: (l, bfn(*g), 0, j))


def _vec_spec(l, width):
    return pl.BlockSpec((None, 1, width), lambda *g: (l, 0, 0))


IN_SLABS = (("a", 0, 2 * GROUP_W), ("b", 512, 4 * GROUP_W), ("c", 1536, GROUP_W), ("d", 1792, MLA_Q_LANES + MLA_KV_LANES),
            ("g", 2176, LANES))
IN_PAD_COLS = 2304


def _sgu_tile(p, gn_ref, w_ref, b_ref, ones_ref, o_ref):
    tm = p.shape[0]
    g = _gelu(p)
    u = g[:, :GROUP_W]
    v = g[:, GROUP_W:]
    ms = _dot_x2(v * v, ones_ref[...]) * (1.0 / HEAD_D)
    vb = (v * lax.rsqrt(ms + EPS) * gn_ref[...]).astype(BF16)
    head = _lane_group((MLP_CHUNK, GROUP_W), HEAD_D)
    w = w_ref[...]
    for c in range(tm // MLP_CHUNK):
        rows = slice(c * MLP_CHUNK, (c + 1) * MLP_CHUNK)
        r = _dot(w, vb[rows])
        s = b_ref[...]
        for h in range(GROUP_W // HEAD_D):
            s = s + jnp.where(head == h, r[h * MLP_CHUNK:(h + 1) * MLP_CHUNK], 0.0)
        o_ref[rows, :] = (u[rows] * s).astype(o_ref.dtype)


def _inproj_kernel(x_ref, g_ref, sc_ref, sh_ref, w_ref, gn_ref, wsp_ref, bsp_ref, ones_ref, oa, ob, oc, od, og):
    h = _rms(x_ref[...]) * g_ref[...] * (1.0 + sc_ref[...]) + sh_ref[...]
    hb = h.astype(BF16)
    (_, off_a, width_a) = IN_SLABS[0]
    pa = _dot(hb, w_ref[:, off_a:off_a + width_a])
    for (_, off, width), o_ref in zip(IN_SLABS[1:], (ob, oc, od, og)):
        o_ref[...] = _dot(hb, w_ref[:, off:off + width])
    _sgu_tile(pa, gn_ref, wsp_ref, bsp_ref, ones_ref, oa)


def _inproj(l, x2, mod4, brow, g_pre, w_in_p, sgu, tm):
    R, D = x2.shape
    gn, w_st, bias, ones_bd = sgu
    H = GROUP_W // HEAD_D
    grid = (R // tm,)
    outs = [jax.ShapeDtypeStruct((R, GROUP_W), BF16)] + [jax.ShapeDtypeStruct((R, width), F32) for (_, _, width) in IN_SLABS[1:]]
    return pl.pallas_call(
        _inproj_kernel,
        grid=grid,
        in_specs=[
            pl.BlockSpec((tm, D), lambda i: (i, 0)),
            _vec_spec(l, D),
            _mod_spec(l, 1, D, brow),
            _mod_spec(l, 0, D, brow),
            pl.BlockSpec((None, D, IN_PAD_COLS), lambda i: (l, 0, 0)),
            _vec_spec(l, GROUP_W),
            pl.BlockSpec((None, H * MLP_CHUNK, MLP_CHUNK), lambda i: (l, 0, 0)),
            pl.BlockSpec((None, MLP_CHUNK, GROUP_W), lambda i: (l, 0, 0)),
            pl.BlockSpec((GROUP_W, GROUP_W), lambda i: (0, 0)),
        ],
        out_specs=[pl.BlockSpec((tm, GROUP_W), lambda i: (i, 0))]
        + [pl.BlockSpec((tm, width), lambda i: (i, 0)) for (_, _, width) in IN_SLABS[1:]],
        out_shape=outs,
        compiler_params=_cparams("parallel"),
        name="inproj",
    )(x2, g_pre, mod4, mod4, w_in_p, gn, w_st, bias, ones_bd)


def _gla_kernel(*refs, rev, finish):
    if finish:
        (pb_ref, pg_ref, wg_ref, bg_ref, s0_ref, ones_ref, dec_ref, oprev_ref, gn_ref, o_ref, sfin_ref, st_scr) = refs
    else:
        (pb_ref, pg_ref, wg_ref, bg_ref, s0_ref, ones_ref, dec_ref, o_ref, sfin_ref, st_scr) = refs
    i = pl.program_id(1)
    C, W, T = GLA_CHUNK, GROUP_W, GLA_TILE
    H = W // HEAD_D
    nsub = T // C
    ntile = pb_ref.shape[0] // T

    @pl.when(i == 0)
    def _():
        st_scr[...] = s0_ref[...]

    tri = dec_ref[...]
    mask4 = jnp.concatenate([tri.astype(F32)] * H, axis=0)
    head_t = _lane_group((T, W), HEAD_D)
    bd = (lax.broadcasted_iota(jnp.int32, (W, W), 0) // HEAD_D) == _lane_group((W, W), HEAD_D)
    tiles = list(range(ntile - 1, -1, -1)) if rev else list(range(ntile))
    subs = list(range(nsub - 1, -1, -1)) if rev else list(range(nsub))
    rows = lambda a: slice(a * T, (a + 1) * T)
    sub = lambda j: slice(j * C, (j + 1) * C)
    stack = lambda blocks: jnp.concatenate([blocks[j] for j in range(nsub)], axis=0)

    logg = {}
    for a in tiles:
        z = _dot_x3(pg_ref[rows(a), :], wg_ref[...]) + bg_ref[...]
        logg[a] = (jnp.minimum(z, 0.0) - jnp.log(1.0 + jnp.exp(-jnp.abs(z)))) * (1.0 / GATE_TEMP)
    cums = {}
    for a in tiles:
        g_hi, g_lo = _split(logg[a])
        cums[a] = _dot(tri, g_hi) + _dot(tri, g_lo)
    qin, qoff, kin, kend, vb, tot, ptile = {}, {}, {}, {}, {}, {}, {}
    for a in tiles:
        cum = cums[a]
        q = pb_ref[rows(a), 0:W] * (HEAD_D ** -0.5)
        k = pb_ref[rows(a), W:2 * W]
        last_row = (lambda j: j * C) if rev else (lambda j: j * C + C - 1)
        off, run = {}, jnp.zeros((1, W), F32)
        for j in subs:
            tot[(a, j)] = cum[last_row(j):last_row(j) + 1]
            off[j] = run
            run = run + tot[(a, j)]
        ptile[a] = jnp.exp(run)
        e = jnp.exp(cum)
        qin[a] = (q * e).astype(BF16)
        qoff[a] = (q * (e * jnp.exp(stack({j: jnp.broadcast_to(off[j], (C, W)) for j in subs})))).astype(BF16)
        kin[a] = (k * jnp.exp(-cum)).astype(BF16)
        kend[a] = (k * jnp.exp(stack({j: jnp.broadcast_to(tot[(a, j)], (C, W)) for j in subs}) - cum)).astype(BF16)
        vb[a] = pb_ref[rows(a), 2 * W:3 * W].astype(BF16)
    sc = {}
    for a in tiles:
        qst = jnp.concatenate([jnp.where(head_t == h, qin[a], jnp.zeros_like(qin[a])) for h in range(H)], axis=0)
        sc[a] = (_dot_nt(qst, kin[a]) * mask4).astype(BF16)
    ost = {a: _dot(sc[a], vb[a]) for a in tiles}
    kvt = {(a, j): jnp.where(bd, _dot_tn(vb[a][sub(j)], kend[a][sub(j)]), 0.0) for a in tiles for j in subs}
    cross, rend = {}, {}
    for a in tiles:
        r = None
        for j in subs:
            if r is not None:
                cross[(a, j)] = _dot_nt(qin[a][sub(j)], r.astype(BF16))
                r = r * jnp.exp(tot[(a, j)]) + kvt[(a, j)]
            else:
                cross[(a, j)] = jnp.zeros((C, W), F32)
                r = kvt[(a, j)]
        rend[a] = r
    st = st_scr[...]
    inter = {}
    for a in tiles:
        inter[a] = _dot_nt(qoff[a], st.astype(BF16))
        st = st * ptile[a] + rend[a]
    st_scr[...] = st

    for a in tiles:
        o = inter[a] + jnp.concatenate([cross[(a, j)] for j in range(nsub)], axis=0)
        for h in range(H):
            o = o + jnp.where(head_t == h, ost[a][h * T:(h + 1) * T], 0.0)
        if finish:
            o = o + oprev_ref[rows(a), :]
            ms = _dot_x2(o * o, ones_ref[...]) * (1.0 / HEAD_D)
            o = o * lax.rsqrt(ms + EPS) * gn_ref[...]
            r = pb_ref[rows(a), 3 * W:4 * W]
            o_ref[rows(a), :] = (o * (r * _sigmoid(r))).astype(o_ref.dtype)
        else:
            o_ref[rows(a), :] = o

    @pl.when(i == pl.num_programs(1) - 1)
    def _():
        sfin_ref[...] = st_scr[...]


def _gla_decay_matrices():
    T, C = GLA_TILE, GLA_CHUNK
    t = jnp.arange(T)[:, None]
    s = jnp.arange(T)[None, :]
    same = (t // C) == (s // C)
    return jnp.stack([same & (s <= t), same & (s >= t)]).astype(BF16)


def _gla_pass(l, d, pb3, pg3, wg, bg, s0, ones_bd, dec, oprev, gn, tt):
    B, n, _ = pb3.shape
    nt = n // tt
    rev = d == 1
    finish = oprev is not None
    W = GROUP_W

    def tok(b, i):
        return (b, (nt - 1 - i) if rev else i, 0)

    in_specs = [
        pl.BlockSpec((None, tt, 4 * W), tok),
        pl.BlockSpec((None, tt, LANES), tok),
        pl.BlockSpec((None, None, LANES, W), lambda b, i: (l, d, 0, 0)),
        pl.BlockSpec((None, None, 1, W), lambda b, i: (l, d, 0, 0)),
        pl.BlockSpec((None, W, W), lambda b, i: (b, 0, 0)),
        pl.BlockSpec((W, W), lambda b, i: (0, 0)),
        pl.BlockSpec((None, GLA_TILE, GLA_TILE), lambda b, i: (d, 0, 0)),
    ]
    args = [pb3, pg3, wg, bg, s0, ones_bd, dec]
    if finish:
        in_specs += [pl.BlockSpec((None, tt, W), tok), _vec_spec(l, W)]
        args += [oprev, gn]
    return pl.pallas_call(
        functools.partial(_gla_kernel, rev=rev, finish=finish),
        grid=(B, nt),
        in_specs=in_specs,
        out_specs=[pl.BlockSpec((None, tt, W), tok), pl.BlockSpec((None, W, W), lambda b, i: (b, 0, 0))],
        out_shape=[jax.ShapeDtypeStruct((B, n, W), BF16 if finish else F32),
                   jax.ShapeDtypeStruct((B, W, W), F32)],
        scratch_shapes=[pltpu.VMEM((W, W), F32)],
        compiler_params=_cparams("arbitrary", "arbitrary"),
        name="gla_bwd" if rev else "gla_fwd",
    )(*args)


def _gla(l, pb_c, pg_c, pb_x, pg_x, wg, bg, gn, ones_bd, tt_c, tt_x):
    B = pb_x.shape[0]
    zero = jnp.zeros((B, GROUP_W, GROUP_W), F32)
    dec = _gla_decay_matrices()
    ofc, sfc = _gla_pass(l, 0, pb_c, pg_c, wg, bg, zero, ones_bd, dec, None, None, tt_c)
    ofx, _ = _gla_pass(l, 0, pb_x, pg_x, wg, bg, sfc, ones_bd, dec, None, None, tt_x)
    ob_c, sbc = _gla_pass(l, 1, pb_c, pg_c, wg, bg, zero, ones_bd, dec, ofc, gn, tt_c)
    ob_x, _ = _gla_pass(l, 1, pb_x, pg_x, wg, bg, sbc, ones_bd, dec, ofx, gn, tt_x)
    return ob_c, ob_x


def _s5_matrices(a_re, a_im, log_dt, b_re, b_im, c_re, c_im, d_skip):
    T = S5_T
    G, P = a_re.shape[1:]
    I = b_re.shape[-1]
    lam = lax.complex(a_re.astype(F32), a_im.astype(F32))
    ldt = lam * jnp.exp(log_dt.astype(F32))[..., None]
    lam_bar = jnp.exp(ldt)
    b_bar = ((lam_bar - 1.0) / lam)[..., None] * lax.complex(b_re.astype(F32), b_im.astype(F32))
    cmat = lax.complex(c_re.astype(F32), c_im.astype(F32))
    steps = jnp.arange(T + 1, dtype=F32)
    pw = jnp.exp(ldt[..., None] * steps)
    taps = jnp.einsum('dgop,dgpk,dgpi->dgiko', cmat, pw[..., :T], b_bar).real
    taps = taps.at[0, :, :, 0, :].add(jnp.eye(I, dtype=F32)[None] * d_skip.astype(F32)[:, :, None])
    row = T * I
    zeros = jnp.zeros((G, I, row), F32)
    fwd = jnp.concatenate([zeros, taps[0].reshape(G, I, row)], axis=-1)
    bwd = jnp.concatenate([jnp.flip(taps[1], axis=2).reshape(G, I, row), zeros], axis=-1)
    m = jnp.stack([fwd[..., (T - s) * I:(T - s) * I + row] + bwd[..., (T - 1 - s) * I:(T - 1 - s) * I + row]
                   for s in range(T)], axis=1).reshape(G, row, row)

    ar = jnp.arange(T)
    pf = pw[0][..., T - 1 - ar]
    pb = pw[1][..., ar]
    bf = jnp.einsum('gps,gpi->gsip', pf, b_bar[0]).reshape(G, T * I, P)
    bb = jnp.einsum('gps,gpi->gsip', pb, b_bar[1]).reshape(G, T * I, P)
    bmat = jnp.concatenate([bf.real, bf.imag, bf.imag, bf.real, bb.real, bb.imag, bb.imag, bb.real], axis=-1)

    cf = jnp.einsum('gop,gpt->gpto', cmat[0], pw[0][..., 1 + ar]).reshape(G, P, T * I)
    cb = jnp.einsum('gop,gpt->gpto', cmat[1], pw[1][..., T - ar]).reshape(G, P, T * I)
    w = jnp.concatenate([m, cf.real, -cf.imag, cb.real, -cb.imag], axis=1)

    a = pw[..., T]
    acoef = jnp.stack([jnp.concatenate([a[0].real, a[0].real], axis=-1), jnp.concatenate([-a[0].imag, a[0].imag], axis=-1),
                       jnp.concatenate([a[1].real, a[1].real], axis=-1), jnp.concatenate([-a[1].imag, a[1].imag], axis=-1)])
    return bmat.astype(BF16), w.astype(BF16), acoef


def _s5_state_kernel(x_ref, b_ref, o_ref):
    o_ref[...] = _dot(x_ref[...], b_ref[...])


def _s5_state(l, xg, bmat, tr):
    G, R, K = xg.shape
    N = bmat.shape[-1]
    return pl.pallas_call(
        _s5_state_kernel,
        grid=(G, R // tr),
        in_specs=[pl.BlockSpec((None, tr, K), lambda g, i: (g, i, 0)),
                  pl.BlockSpec((None, None, K, N), lambda g, i: (l, g, 0, 0))],
        out_specs=pl.BlockSpec((None, tr, N), lambda g, i: (g, i, 0)),
        out_shape=jax.ShapeDtypeStruct((G, R, N), F32),
        compiler_params=_cparams("parallel", "parallel"),
        name="s5_state",
    )(xg, bmat)


def _s5_scan_kernel(sf_ref, sb_ref, a_ref, h0_ref, hf_ref, hb_ref, hfin_ref, st_scr, sf_scr, sb_scr, hf_scr, hb_scr):
    j = pl.program_id(1)
    G, tc, _ = sf_ref.shape
    pitch = tc + S5_PITCH_PAD
    half = hf_ref.shape[-1]

    @pl.when(j == 0)
    def _():
        st_scr[...] = h0_ref[...]

    for g in range(G):
        for k in range(2):
            sf_scr[k, g * pitch:g * pitch + tc, :] = sf_ref[g, :, k * half:(k + 1) * half]
            sb_scr[k, g * pitch:g * pitch + tc, :] = sb_ref[g, :, k * half:(k + 1) * half]

    a1f, a2f, a1b, a2b = a_ref[0], a_ref[1], a_ref[2], a_ref[3]
    chunk = lambda i: pl.ds(i, G, stride=pitch)

    def body(i, hs):
        h1f, h2f, h1b, h2b = hs
        ib = tc - 1 - i
        hf_scr[chunk(i), :] = h1f
        hb_scr[chunk(ib), :] = h1b
        return (h1f * a1f + h2f * a2f + sf_scr[0, chunk(i), :], h2f * a1f - h1f * a2f + sf_scr[1, chunk(i), :],
                h1b * a1b + h2b * a2b + sb_scr[0, chunk(ib), :], h2b * a1b - h1b * a2b + sb_scr[1, chunk(ib), :])

    hs = lax.fori_loop(0, tc, body, (st_scr[0], st_scr[1], st_scr[2], st_scr[3]))
    for k in range(4):
        st_scr[k] = hs[k]
    for g in range(G):
        hf_ref[g] = hf_scr[g * pitch:g * pitch + tc, :]
        hb_ref[g] = hb_scr[g * pitch:g * pitch + tc, :]

    @pl.when(j == pl.num_programs(1) - 1)
    def _():
        hfin_ref[...] = st_scr[...]


def _s5_scan(l, s, acoef, h0, B, tc):
    G, R, n2 = s.shape
    nt = R // B // tc
    half = n2 // 4
    fwd = lambda b, j: (0, b * nt + j, 0)
    bwd = lambda b, j: (0, b * nt + nt - 1 - j, 0)
    st_spec = pl.BlockSpec((None, 4, G, half), lambda b, j: (b, 0, 0, 0))
    return pl.pallas_call(
        _s5_scan_kernel,
        grid=(B, nt),
        in_specs=[pl.BlockSpec((G, tc, 2 * half), fwd),
                  pl.BlockSpec((G, tc, 2 * half), lambda b, j: (0, b * nt + nt - 1 - j, 1)),
                  pl.BlockSpec((None, 4, G, half), lambda b, j: (l, 0, 0, 0)),
                  st_spec],
        out_specs=[pl.BlockSpec((G, tc, half), fwd), pl.BlockSpec((G, tc, half), bwd), st_spec],
        out_shape=[jax.ShapeDtypeStruct((G, R, half), F32), jax.ShapeDtypeStruct((G, R, half), F32),
                   jax.ShapeDtypeStruct((B, 4, G, half), F32)],
        scratch_shapes=[pltpu.VMEM((4, G, half), F32),
                        pltpu.VMEM((2, G * (tc + S5_PITCH_PAD), half), F32), pltpu.VMEM((2, G * (tc + S5_PITCH_PAD), half), F32),
                        pltpu.VMEM((G * (tc + S5_PITCH_PAD), half), F32), pltpu.VMEM((G * (tc + S5_PITCH_PAD), half), F32)],
        compiler_params=_cparams("arbitrary", "arbitrary"),
        name="s5_scan",
    )(s, s, acoef, h0)


def _s5_out_kernel(x_ref, hf_ref, hb_ref, w_ref, o_ref):
    k = x_ref.shape[-1]
    kh = hf_ref.shape[-1]
    y = _dot(x_ref[...], w_ref[0:k, :])
    y = y + _dot(hf_ref[...].astype(BF16), w_ref[k:k + kh, :])
    o_ref[...] = y + _dot(hb_ref[...].astype(BF16), w_ref[k + kh:, :])


def _s5_out(l, xg, hf, hb, w, tr):
    G, R, K = xg.shape
    kh = hf.shape[-1]
    row = lambda width: pl.BlockSpec((None, tr, width), lambda g, i: (g, i, 0))
    return pl.pallas_call(
        _s5_out_kernel,
        grid=(G, R // tr),
        in_specs=[row(K), row(kh), row(kh), pl.BlockSpec((None, None, K + 2 * kh, K), lambda g, i: (l, g, 0, 0))],
        out_specs=row(K),
        out_shape=jax.ShapeDtypeStruct((G, R, K), F32),
        compiler_params=_cparams("parallel", "parallel"),
        name="s5_out",
    )(xg, hf, hb, w)


def _s5_finish_kernel(y_ref, w_ref, b_ref, o_ref):
    y = _gelu(y_ref[...])
    gate = _sigmoid(_dot(y.astype(BF16), w_ref[...]) + b_ref[...])
    o_ref[...] = (y * gate).astype(o_ref.dtype)


def _s5_finish(l, y2, w_glu, b_glu, tm):
    R, W = y2.shape
    return pl.pallas_call(
        _s5_finish_kernel,
        grid=(R // tm,),
        in_specs=[pl.BlockSpec((tm, W), lambda i: (i, 0)),
                  pl.BlockSpec((None, W, W), lambda i: (l, 0, 0)),
                  _vec_spec(l, W)],
        out_specs=pl.BlockSpec((tm, W), lambda i: (i, 0)),
        out_shape=jax.ShapeDtypeStruct((R, W), BF16),
        compiler_params=_cparams("parallel"),
        name="s5_finish",
    )(y2, w_glu, b_glu)


def _to_groups(u3):
    B, n, W = u3.shape
    G = W // S5_IN
    x = u3.astype(BF16).reshape(B, n // S5_T, S5_T, G, S5_IN).transpose(3, 0, 1, 2, 4)
    return x.reshape(G, B * (n // S5_T), S5_T * S5_IN)


def _from_groups(y, B):
    G, R, _ = y.shape
    nc = R // B
    y = y.reshape(G, B, nc, S5_T, S5_IN).transpose(1, 2, 3, 0, 4)
    return y.reshape(B * nc * S5_T, G * S5_IN)


def _s5_gather_kernel(x_ref, o_ref, xs_scr, xt_scr):
    G, tc, _ = o_ref.shape
    nh = x_ref.shape[1] // LANES
    gh = G // nh
    for h in range(nh):
        xs_scr[h] = x_ref[:, h * LANES:(h + 1) * LANES]
    for t in range(S5_T):
        for h in range(nh):
            rt = xs_scr[h, pl.ds(t, tc, stride=S5_T), :].T
            for g in range(gh):
                xt_scr[h * gh + g, t * S5_IN:(t + 1) * S5_IN, :] = rt[g * S5_IN:(g + 1) * S5_IN, :]
    for g in range(G):
        o_ref[g] = xt_scr[g].T.astype(o_ref.dtype)


def _s5_gather(x2, tc):
    N, W = x2.shape
    G = W // S5_IN
    R = N // S5_T
    return pl.pallas_call(
        _s5_gather_kernel,
        grid=(R // tc,),
        in_specs=[pl.BlockSpec((tc * S5_T, W), lambda i: (i, 0))],
        out_specs=pl.BlockSpec((G, tc, S5_T * S5_IN), lambda i: (0, i, 0)),
        out_shape=jax.ShapeDtypeStruct((G, R, S5_T * S5_IN), BF16),
        scratch_shapes=[pltpu.VMEM((W // LANES, tc * S5_T, LANES), F32), pltpu.VMEM((G, S5_T * S5_IN, tc), F32)],
        compiler_params=_cparams("parallel"),
        name="s5_gather",
    )(x2)


def _s5_scatter_kernel(y_ref, w_ref, b_ref, o_ref, yt_scr, tok_scr):
    G, tc, _ = y_ref.shape
    nh = tok_scr.shape[0]
    gh = G // nh
    for g in range(G):
        yt = y_ref[g].T
        for t in range(S5_T):
            yt_scr[t, g // gh, (g % gh) * S5_IN:(g % gh + 1) * S5_IN, :] = yt[t * S5_IN:(t + 1) * S5_IN, :]
    for t in range(S5_T):
        for h in range(nh):
            tok_scr[h, pl.ds(t, tc, stride=S5_T), :] = yt_scr[t, h].T
    y = _gelu(jnp.concatenate([tok_scr[h] for h in range(nh)], axis=1))
    gate = _sigmoid(_dot(y.astype(BF16), w_ref[...]) + b_ref[...])
    o_ref[...] = (y * gate).astype(o_ref.dtype)


def _s5_scatter_finish(l, yg, w_glu, b_glu, tc):
    G, R, K = yg.shape
    W = G * S5_IN
    return pl.pallas_call(
        _s5_scatter_kernel,
        grid=(R // tc,),
        in_specs=[pl.BlockSpec((G, tc, K), lambda i: (0, i, 0)),
                  pl.BlockSpec((None, W, W), lambda i: (l, 0, 0)),
                  _vec_spec(l, W)],
        out_specs=pl.BlockSpec((tc * S5_T, W), lambda i: (i, 0)),
        out_shape=jax.ShapeDtypeStruct((R * S5_T, W), BF16),
        scratch_shapes=[pltpu.VMEM((S5_T, W // LANES, LANES, tc), F32), pltpu.VMEM((W // LANES, tc * S5_T, LANES), F32)],
        compiler_params=_cparams("parallel"),
        name="s5_scatter",
    )(yg, w_glu, b_glu)


def _s5(l, pc_c, pc_x, mats, w_glu, b_glu, need_ctx, tm_c):
    bmat, w, acoef = mats
    B, n, W = pc_x.shape
    G = W // S5_IN
    xg_c = _to_groups(pc_c)
    tc_x = min(S5_TILE, n // S5_T)
    xg_x = _s5_gather(pc_x.reshape(B * n, W), tc_x)
    tr_c = min(S5_ROW_TILE, xg_c.shape[1])
    tr_x = min(S5_ROW_TILE, xg_x.shape[1])
    h0 = jnp.zeros((B, 4, G, acoef.shape[-1]), F32)
    hf_c, hb_c, h1 = _s5_scan(l, _s5_state(l, xg_c, bmat, tr_c), acoef, h0, B, xg_c.shape[1] // B)
    hf_x, hb_x, _ = _s5_scan(l, _s5_state(l, xg_x, bmat, tr_x), acoef, h1, B, tc_x)
    oc_x = _s5_scatter_finish(l, _s5_out(l, xg_x, hf_x, hb_x, w, tr_x), w_glu, b_glu, tc_x)
    oc_c = None
    if need_ctx:
        y_c = _from_groups(_s5_out(l, xg_c, hf_c, hb_c, w, tr_c), B)
        oc_c = _s5_finish(l, y_c, w_glu, b_glu, tm_c)
    return oc_c, oc_x


def _mla_prep_kernel(*refs, rope):
    refs = refs[:-5] + refs[-3:]
    if rope:
        pd_ref, qn_ref, kvn_ref, wq_ref, wqs_ref, wk_ref, wks_ref, wv_ref, cos_ref, sin_ref, q_ref, k_ref, v_ref = refs
    else:
        pd_ref, qn_ref, kvn_ref, wq_ref, wk_ref, wv_ref, q_ref, k_ref, v_ref = refs
    cq = pd_ref[:, 0:MLA_Q_LANES]
    ms = jnp.sum(cq * cq, axis=-1, keepdims=True) * (1.0 / MLA_Q_RANK)
    cqn = (cq * lax.rsqrt(ms + EPS) * qn_ref[...]).astype(BF16)
    ck = pd_ref[:, MLA_Q_LANES:MLA_Q_LANES + MLA_KV_LANES]
    lane = lax.broadcasted_iota(jnp.int32, ck.shape, 1)
    is_lat = lane < MLA_KV_RANK
    ms = jnp.sum(jnp.where(is_lat, ck * ck, 0.0), axis=-1, keepdims=True) * (1.0 / MLA_KV_RANK)
    ckn = jnp.where(is_lat, ck * lax.rsqrt(ms + EPS) * kvn_ref[...], ck).astype(BF16)
    q = _dot(cqn, wq_ref[...])
    k = _dot(ckn, wk_ref[...])
    if rope:
        cos = jnp.concatenate([cos_ref[...]] * MLA_HEADS, axis=1)
        sin = jnp.concatenate([sin_ref[...]] * MLA_HEADS, axis=1)
        q = q * cos + _dot(cqn, wqs_ref[...]) * sin
        k = k * cos + _dot(ckn, wks_ref[...]) * sin
    q_ref[...] = (q * ((MLA_NOPE + MLA_ROPE) ** -0.5 * LOG2E)).astype(BF16)
    k_ref[...] = k.astype(BF16)
    v_ref[...] = _dot(ckn, wv_ref[...]).astype(BF16)


def _mla_prep(l, pd, qn, kvn, wts, tables, tm, n, B, row0, nk, kv_into):
    R = pd.shape[0]
    HP = MLA_HEADS * MLA_HEAD_PAD
    wq, wqs, wk, wks, wv = wts
    rope = tables is not None
    npt = n // tm
    wspec = lambda r: pl.BlockSpec((None, r, HP), lambda i: (l, 0, 0))
    QL, KL = MLA_Q_LANES, MLA_KV_LANES
    in_specs = [pl.BlockSpec((tm, QL + KL), lambda i: (i, 0)), _vec_spec(l, QL), _vec_spec(l, KL)]
    if rope:
        tspec = pl.BlockSpec((tm, MLA_HEAD_PAD), lambda i: (i % npt, 0))
        in_specs += [wspec(QL), wspec(QL), wspec(KL), wspec(KL), wspec(KL), tspec, tspec]
        args = (pd, qn, kvn, wq, wqs, wk, wks, wv) + tuple(tables)
    else:
        in_specs += [wspec(QL), wspec(KL), wspec(KL)]
        args = (pd, qn, kvn, wq, wk, wv)
    aliases = {len(args): 1, len(args) + 1: 2}
    in_specs += [pl.BlockSpec(memory_space=pl.ANY)] * 2
    args = args + tuple(kv_into)
    kv_spec = pl.BlockSpec((None, tm, HP), lambda i: (i // npt, row0 // tm + i % npt, 0))
    kv_shape = jax.ShapeDtypeStruct((B, nk, HP), BF16)
    return pl.pallas_call(
        functools.partial(_mla_prep_kernel, rope=rope),
        grid=(R // tm,),
        in_specs=in_specs,
        out_specs=[pl.BlockSpec((tm, HP), lambda i: (i, 0)), kv_spec, kv_spec],
        out_shape=[jax.ShapeDtypeStruct((R, HP), BF16), kv_shape, kv_shape],
        input_output_aliases=aliases,
        compiler_params=_cparams("parallel"),
        name="mla_prep",
    )(*args)


def _attn_kernel(q_ref, k_ref, v_ref, o_ref, m_scr, l_scr, acc_scr):
    kv = pl.program_id(2)
    HP = MLA_HEAD_PAD
    tk = k_ref.shape[0]

    @pl.when(kv == 0)
    def _():
        m_scr[...] = jnp.full(m_scr.shape, -jnp.inf, F32)
        l_scr[...] = jnp.zeros(l_scr.shape, F32)
        acc_scr[...] = jnp.zeros(acc_scr.shape, F32)

    def scores(h):
        lanes = slice(h * HP, (h + 1) * HP)
        return _dot_nt(q_ref[:, lanes], k_ref[:, lanes])

    s_next = scores(0)
    for h in range(MLA_HEADS):
        lanes = slice(h * HP, (h + 1) * HP)
        s = s_next
        if h + 1 < MLA_HEADS:
            s_next = scores(h + 1)
        m_prev = m_scr[h]
        m_new = jnp.maximum(m_prev, jnp.max(s, axis=1, keepdims=True))
        alpha = jnp.exp2(m_prev - m_new)
        p = jnp.exp2(s - m_new[:, 0:1])
        lp = p[:, 0:LANES]
        for c in range(1, tk // LANES):
            lp = lp + p[:, c * LANES:(c + 1) * LANES]
        l_scr[h] = alpha * l_scr[h] + lp
        acc_scr[h] = alpha * acc_scr[h] + _dot(p.astype(BF16), v_ref[:, lanes])
        m_scr[h] = m_new

    @pl.when(kv == pl.num_programs(2) - 1)
    def _():
        low = lax.broadcasted_iota(jnp.int32, acc_scr.shape[1:], 1) < MLA_V
        norm = lambda h: acc_scr[h] * (1.0 / jnp.sum(l_scr[h], axis=1, keepdims=True))
        outs = [jnp.where(low, norm(h), norm(h + 1)) for h in range(0, MLA_HEADS, 2)]
        o_ref[...] = jnp.concatenate(outs, axis=1).astype(o_ref.dtype)


def _attention(q3, k3, v3, tq, tk, k0, nk):
    B, nq, HP = q3.shape
    kb = k0 // tk
    return pl.pallas_call(
        _attn_kernel,
        grid=(B, nq // tq, nk // tk),
        in_specs=[pl.BlockSpec((None, tq, HP), lambda b, i, j: (b, i, 0)),
                  pl.BlockSpec((None, tk, HP), lambda b, i, j: (b, kb + j, 0)),
                  pl.BlockSpec((None, tk, HP), lambda b, i, j: (b, kb + j, 0))],
        out_specs=pl.BlockSpec((None, tq, MLA_HEADS * MLA_V), lambda b, i, j: (b, i, 0)),
        out_shape=jax.ShapeDtypeStruct((B, nq, MLA_HEADS * MLA_V), BF16),
        scratch_shapes=[pltpu.VMEM((MLA_HEADS, tq, LANES), F32),
                        pltpu.VMEM((MLA_HEADS, tq, LANES), F32),
                        pltpu.VMEM((MLA_HEADS, tq, MLA_HEAD_PAD), F32)],
        compiler_params=_cparams("parallel", "parallel", "arbitrary"),
        name="attention",
    )(q3, k3, v3)


def _outproj_kernel(oa, ob, oc, od, w_ref, x_ref, gpost_ref, gt_ref, gpre_ref, sc_ref, sh_ref, xo_ref, h_ref):
    W = GROUP_W
    mix = _dot(oa[...], w_ref[0:W, :])
    mix = mix + _dot(ob[...], w_ref[W:2 * W, :])
    mix = mix + _dot(oc[...], w_ref[2 * W:3 * W, :])
    mix = mix + _dot(od[...], w_ref[3 * W:4 * W, :])
    x = x_ref[...] + gt_ref[...] * (_rms(mix) * gpost_ref[...])
    xo_ref[...] = x
    h_ref[...] = (_rms(x) * gpre_ref[...] * (1.0 + sc_ref[...]) + sh_ref[...]).astype(h_ref.dtype)


def _outproj(l, parts, w_out, x2, mod4, brow, g_post, g_pre_ffn, tm):
    R, D = x2.shape
    W = GROUP_W
    part_spec = pl.BlockSpec((tm, W), lambda i: (i, 0))
    row_spec = pl.BlockSpec((tm, D), lambda i: (i, 0))
    return pl.pallas_call(
        _outproj_kernel,
        grid=(R // tm,),
        in_specs=[part_spec] * 4 + [
            pl.BlockSpec((None, 4 * W, D), lambda i: (l, 0, 0)),
            row_spec,
            _vec_spec(l, D),
            _mod_spec(l, 2, D, brow),
            _vec_spec(l, D),
            _mod_spec(l, 4, D, brow),
            _mod_spec(l, 3, D, brow),
        ],
        out_specs=[row_spec, row_spec],
        out_shape=[jax.ShapeDtypeStruct((R, D), F32), jax.ShapeDtypeStruct((R, D), BF16)],
        compiler_params=_cparams("parallel"),
        name="outproj",
    )(*parts, w_out, x2, g_post, mod4, g_pre_ffn, mod4, mod4)


FFN_HALO = 16
FFN_COLS = 256


def _ffn_kernel(hp_ref, h_ref, hn_ref, wup_ref, cw_ref, cb_ref, wdn_ref, x_ref, gpost_ref, gt_ref, o_ref, y_scr):
    i = pl.program_id(1)
    tm = h_ref.shape[0]
    dff = wdn_ref.shape[0]
    use = FFN_HALO // 2
    rows = tm + 2 * use
    prev = jnp.where(i == 0, jnp.zeros_like(hp_ref[...]), hp_ref[...])[FFN_HALO - use:]
    nxt = jnp.where(i == pl.num_programs(1) - 1, jnp.zeros_like(hn_ref[...]), hn_ref[...])[:use]
    hb = jnp.concatenate([prev, h_ref[...], nxt], axis=0)

    def conv(z, cols):
        w = cw_ref[:, cols]
        out = cb_ref[:, cols] + w[0:1] * pltpu.roll(z, 1, axis=0) + w[1:2] * z + w[2:3] * pltpu.roll(z, rows - 1, axis=0)
        return out[use:use + tm]

    def cols(j):
        return slice(j * FFN_COLS, (j + 1) * FFN_COLS), slice(dff + j * FFN_COLS, dff + (j + 1) * FFN_COLS)

    def up(j):
        ca, cg = cols(j)
        return _dot(hb, wup_ref[:, ca]), _dot(hb, wup_ref[:, cg])

    nchunks = dff // FFN_COLS
    z_next = up(0)
    for j in range(nchunks):
        ca, cg = cols(j)
        za, zg = z_next
        if j + 1 < nchunks:
            z_next = up(j + 1)
        y_scr[:, ca] = _gelu_gate(conv(za, ca), conv(zg, cg)).astype(BF16)
    acc = _dot(y_scr[...], wdn_ref[...])
    o_ref[...] = x_ref[...] + gt_ref[...] * (_rms(acc) * gpost_ref[...])


def _ffn(l, h3, x3, w_up, conv_w, conv_b, w_down, mod4, brow, g_post, tm):
    B, n, D = x3.shape
    nt = n // tm
    hb = tm // FFN_HALO
    nh = n // FFN_HALO
    dff = w_down.shape[1]
    once = pl.Buffered(1)
    return pl.pallas_call(
        _ffn_kernel,
        grid=(B, nt),
        in_specs=[
            pl.BlockSpec((None, FFN_HALO, D), lambda b, i: (b, jnp.maximum(i * hb - 1, 0), 0)),
            pl.BlockSpec((None, tm, D), lambda b, i: (b, i, 0)),
            pl.BlockSpec((None, FFN_HALO, D), lambda b, i: (b, jnp.minimum((i + 1) * hb, nh - 1), 0)),
            pl.BlockSpec((None, D, 2 * dff), lambda b, i: (l, 0, 0), pipeline_mode=once),
            pl.BlockSpec((None, 3, 2 * dff), lambda b, i: (l, 0, 0)),
            pl.BlockSpec((None, 1, 2 * dff), lambda b, i: (l, 0, 0)),
            pl.BlockSpec((None, dff, D), lambda b, i: (l, 0, 0), pipeline_mode=once),
            pl.BlockSpec((None, tm, D), lambda b, i: (b, i, 0)),
            _vec_spec(l, D),
            _mod_spec(l, 5, D, brow),
        ],
        out_specs=pl.BlockSpec((None, tm, D), lambda b, i: (b, i, 0)),
        out_shape=jax.ShapeDtypeStruct((B, n, D), F32),
        scratch_shapes=[pltpu.VMEM((tm, dff), BF16)],
        compiler_params=_cparams("parallel", "parallel"),
        name="conv_ffn",
    )(h3, h3, h3, w_up, conv_w, conv_b, w_down, x3, g_post, mod4)


def _prep_w_in(w_in):
    w_in = w_in.astype(BF16)
    z = lambda n: jnp.zeros(w_in.shape[:-1] + (n,), w_in.dtype)
    a = w_in[..., 0:512]
    b = w_in[..., 512:1536]
    gl = w_in[..., 1536:1568]
    c = w_in[..., 1568:1824]
    cq = w_in[..., 1824:2048]
    ckv_kr = w_in[..., 2048:2176]
    return jnp.concatenate([a, b, c, cq, z(32), ckv_kr, gl, z(96)], axis=-1)


def _rope_swap(t):
    q = MLA_ROPE // 4
    return jnp.concatenate([t[..., q:2 * q], t[..., 0:q], t[..., 3 * q:4 * q], t[..., 2 * q:3 * q]], axis=-1)


def _prep_mla(w_uq, w_ukv):
    L = w_uq.shape[0]
    H, NP, RP, HP = MLA_HEADS, MLA_NOPE, MLA_ROPE, MLA_HEAD_PAD
    wq = w_uq.reshape(L, MLA_Q_RANK, H, NP + RP)
    zq = jnp.zeros((L, MLA_Q_RANK, H, HP - NP - RP), w_uq.dtype)
    znope = jnp.zeros((L, MLA_Q_RANK, H, NP), w_uq.dtype)
    q_main = jnp.concatenate([wq, zq], axis=-1)
    q_swap = jnp.concatenate([znope, _rope_swap(wq[..., NP:]), zq], axis=-1)
    padq = lambda w: jnp.pad(w.reshape(L, MLA_Q_RANK, H * HP), ((0, 0), (0, MLA_Q_LANES - MLA_Q_RANK), (0, 0)))

    wkv = w_ukv.reshape(L, MLA_KV_RANK, H, NP + MLA_V)
    zk = jnp.zeros((L, MLA_KV_RANK, H, HP - NP), w_ukv.dtype)
    k_lat = jnp.concatenate([wkv[..., :NP], zk], axis=-1)
    eye = jnp.eye(RP, dtype=w_ukv.dtype)
    place = lambda e: jnp.broadcast_to(
        jnp.concatenate([jnp.zeros((RP, NP), e.dtype), e, jnp.zeros((RP, HP - NP - RP), e.dtype)], axis=-1)[None, :, None, :],
        (L, RP, H, HP))
    k_main = jnp.concatenate([k_lat, place(eye)], axis=1)
    k_swap = jnp.concatenate([jnp.zeros_like(k_lat), place(_rope_swap(eye))], axis=1)
    zv = jnp.zeros((L, MLA_KV_RANK, H, HP - MLA_V), w_ukv.dtype)
    odd = (jnp.arange(H) % 2 == 1)[None, None, :, None]
    v_lat = jnp.where(odd, jnp.concatenate([zv, wkv[..., NP:]], axis=-1), jnp.concatenate([wkv[..., NP:], zv], axis=-1))
    v_main = jnp.concatenate([v_lat, jnp.zeros((L, RP, H, HP), w_ukv.dtype)], axis=1)
    flat = lambda w: w.reshape(L, w.shape[1], H * HP).astype(BF16)
    return (padq(q_main).astype(BF16), padq(q_swap).astype(BF16), flat(k_main), flat(k_swap), flat(v_main))


def _rope_tables(n):
    rows = n // GRID_W
    nf = MLA_ROPE // 4
    inv = ROPE_BASE ** (-jnp.arange(nf, dtype=F32) / nf)
    ar = jnp.arange(rows, dtype=F32)[:, None] * inv[None, :]
    ac = jnp.arange(GRID_W, dtype=F32)[:, None] * inv[None, :]
    by_row = lambda t: jnp.repeat(t, GRID_W, axis=0)
    by_col = lambda t: jnp.tile(t, (rows, 1))
    cr, sr, cc, sn = by_row(jnp.cos(ar)), by_row(jnp.sin(ar)), by_col(jnp.cos(ac)), by_col(jnp.sin(ac))
    one = jnp.ones((n, MLA_NOPE), F32)
    zero = jnp.zeros((n, MLA_HEAD_PAD - MLA_NOPE - MLA_ROPE), F32)
    cos = jnp.concatenate([one, cr, cr, cc, cc, zero], axis=1)
    sin = jnp.concatenate([0.0 * one, -sr, sr, -sn, sn, zero], axis=1)
    return cos, sin


def _pick_tile(n, want):
    t = min(n, want)
    while n % t:
        t //= 2
    return t


def kernel(x, c, ctx, c_ctx, w_mod, b_mod, g_pre_mix, g_post_mix, g_pre_ffn, g_post_ffn, w_in,
           sgu_norm, sgu_w, sgu_b, gla_w_gate, gla_b_gate, gla_norm,
           s5_a_re, s5_a_im, s5_log_dt, s5_b_re, s5_b_im, s5_c_re, s5_c_im, s5_d, s5_w_glu, s5_b_glu,
           mla_q_norm, mla_w_uq, mla_kv_norm, mla_w_ukv, w_out,
           ffn_w_up, ffn_conv_w, ffn_conv_b, ffn_w_down):
    B, n, D = x.shape
    nctx = ctx.shape[1]
    L = w_mod.shape[0]
    W = GROUP_W
    assert B < 8 and n % TOKEN_TILE == 0 and nctx % GLA_TILE == 0 and n % GRID_W == 0

    c8 = jnp.concatenate([c, c_ctx[None, :], jnp.zeros((8 - B - 1, D), F32)], axis=0)
    mod4 = _modulation(c8, w_mod, b_mod).reshape(L, 8, 1, 6 * D)
    vec = lambda p: p.reshape(L, 1, -1).astype(F32)
    g_pre_mix, g_post_mix, g_pre_ffn, g_post_ffn = map(vec, (g_pre_mix, g_post_mix, g_pre_ffn, g_post_ffn))
    w_in_p = _prep_w_in(w_in)
    sgu_gn = vec(sgu_norm)
    sgu_w_st = sgu_w.reshape(L, -1, MLP_CHUNK).astype(BF16)
    sgu_bias = jnp.repeat(jnp.swapaxes(sgu_b, 1, 2), HEAD_D, axis=2).astype(F32)
    ones_bd = jnp.kron(jnp.eye(W // HEAD_D, dtype=F32), jnp.ones((HEAD_D, HEAD_D), F32)).astype(BF16)
    gla_wg = jnp.zeros((L, 2, LANES, W), F32)
    gla_wg = gla_wg.at[:, 0, 0:GATE_RANK].set(gla_w_gate[:, 0]).at[:, 1, GATE_RANK:2 * GATE_RANK].set(gla_w_gate[:, 1])
    gla_bg = gla_b_gate.reshape(L, 2, 1, W).astype(F32)
    gla_gn = vec(gla_norm)
    s5_wglu = s5_w_glu.astype(BF16)
    s5_bglu = vec(s5_b_glu)
    mla_qn = jnp.pad(mla_q_norm, ((0, 0), (0, MLA_Q_LANES - MLA_Q_RANK))).reshape(L, 1, MLA_Q_LANES).astype(F32)
    mla_kvn = jnp.pad(mla_kv_norm, ((0, 0), (0, MLA_KV_LANES - MLA_KV_RANK))).reshape(L, 1, MLA_KV_LANES).astype(F32)
    mla_wts = _prep_mla(mla_w_uq, mla_w_ukv)
    tables = _rope_tables(n)
    w_out_b = w_out.astype(BF16)
    w_up_b = ffn_w_up.astype(BF16)
    w_down_b = ffn_w_down.astype(BF16)
    dff = ffn_w_down.shape[1]
    half_gate = jnp.concatenate([jnp.ones((dff,), F32), jnp.full((dff,), 0.5, F32)])
    conv_w = ffn_conv_w.astype(F32) * half_gate
    conv_b = (ffn_conv_b.astype(F32) * half_gate).reshape(L, 1, -1)

    s5_mats = jax.vmap(_s5_matrices)(s5_a_re, s5_a_im, s5_log_dt, s5_b_re, s5_b_im, s5_c_re, s5_c_im, s5_d)

    tm_x = _pick_tile(n, TOKEN_TILE)
    tm_c = _pick_tile(nctx, TOKEN_TILE)
    tpb_x = n // tm_x
    brow_x = lambda i: i // tpb_x
    brow_c = lambda *g: B
    brow_x2 = lambda b, i: b

    xs = x.reshape(B * n, D)
    cs = ctx.reshape(B * nctx, D)

    for l in range(L):
        need_ctx = l < L - 1
        sgu = (sgu_gn, sgu_w_st, sgu_bias, ones_bd)
        oa_x, pb_x, pc_x, pd_x, pg_x = _inproj(l, xs, mod4, brow_x, g_pre_mix, w_in_p, sgu, tm_x)
        oa_c, pb_c, pc_c, pd_c, pg_c = _inproj(l, cs, mod4, brow_c, g_pre_mix, w_in_p, sgu, tm_c)

        r3 = lambda t, m: t.reshape(B, m, t.shape[-1])
        ob_c, ob_x = _gla(l, r3(pb_c, nctx), r3(pg_c, nctx), r3(pb_x, n), r3(pg_x, n),
                          gla_wg, gla_bg, gla_gn, ones_bd, _pick_tile(nctx, GLA_TILE), _pick_tile(n, 4 * GLA_TILE))
        oc_c, oc_x = _s5(l, r3(pc_c, nctx), r3(pc_x, n), s5_mats, s5_wglu, s5_bglu, need_ctx, tm_c)
        wq, wqs, wk, wks, wv = mla_wts
        nk = n + nctx
        kv0 = jnp.zeros((B, nk, MLA_HEADS * MLA_HEAD_PAD), BF16)
        q_x, k_all, v_all = _mla_prep(l, pd_x, mla_qn, mla_kvn, mla_wts, tables, tm_x, n, B, 0, nk, (kv0, kv0))
        q_c, k_all, v_all = _mla_prep(l, pd_c, mla_qn, mla_kvn, (wq, None, wk, None, wv), None, tm_c, nctx, B, n, nk,
                                      (k_all, v_all))
        tk = next(t for t in ATTN_KV_TILES if nk % t == 0)
        od_x = _attention(r3(q_x, n), k_all, v_all, tm_x, tk, 0, nk).reshape(B * n, W)

        xs, hx = _outproj(l, (oa_x, ob_x.reshape(B * n, W), oc_x, od_x), w_out_b, xs, mod4, brow_x,
                          g_post_mix, g_pre_ffn, tm_x)
        xs = _ffn(l, hx.reshape(B, n, D), xs.reshape(B, n, D), w_up_b, conv_w, conv_b, w_down_b, mod4, brow_x2,
                  g_post_ffn, tm_x).reshape(B * n, D)

        if need_ctx:
            od_c = _attention(r3(q_c, nctx), k_all, v_all, tm_c, nctx, n, nctx).reshape(B * nctx, W)
            cs, hc = _outproj(l, (oa_c, ob_c.reshape(B * nctx, W), oc_c, od_c), w_out_b, cs, mod4, brow_c,
                              g_post_mix, g_pre_ffn, tm_c)
            cs = _ffn(l, hc.reshape(B, nctx, D), cs.reshape(B, nctx, D), w_up_b, conv_w, conv_b, w_down_b, mod4,
                      brow_c, g_post_ffn, tm_c).reshape(B * nctx, D)
    return xs.reshape(B, n, D)
```

```python
import functools

import jax
import jax.numpy as jnp
from jax import lax
from jax.experimental import pallas as pl
from jax.experimental.pallas import tpu as pltpu

F32 = jnp.float32
BF16 = jnp.bfloat16

EPS = 1e-6
GRID_W = 64
GROUP_W = 256
HEAD_D = 64
MLP_CHUNK = 128
GATE_RANK = 16
GATE_TEMP = 16.0
GLA_CHUNK = 64
GLA_TILE = 256
S5_IN = 16
S5_T = 16
S5_TILE = 128
S5_PITCH_PAD = 8
MLA_HEADS = 4
MLA_NOPE = 64
MLA_ROPE = 32
MLA_V = 64
MLA_Q_RANK = 224
MLA_KV_RANK = 96
MLA_HEAD_PAD = 128
ROPE_BASE = 10000.0
LOG2E = 1.4426950408889634

LANES = 128
VMEM_LIMIT = 48 * 1024 * 1024
TOKEN_TILE = 512
MOD_COL_TILE = 1536
S5_ROW_TILE = 512
ATTN_KV_TILES = (2816, 1408, 768, 512, 256, 128)
MLA_Q_LANES = 256
MLA_KV_LANES = 128


def _cparams(*sem):
    return pltpu.CompilerParams(dimension_semantics=sem, vmem_limit_bytes=VMEM_LIMIT)


def _dot(a, b):
    return jnp.dot(a, b, preferred_element_type=F32)


def _dot_nt(a, b):
    return lax.dot_general(a, b, (((1,), (1,)), ((), ())), preferred_element_type=F32)


def _dot_tn(a, b):
    return lax.dot_general(a, b, (((0,), (0,)), ((), ())), preferred_element_type=F32)


def _split(a):
    hi = a.astype(BF16)
    lo = (a - hi.astype(F32)).astype(BF16)
    return hi, lo


def _dot_x2(a, b_bf16):
    hi, lo = _split(a)
    return _dot(hi, b_bf16) + _dot(lo, b_bf16)


def _dot_x3(a, b):
    ah, al = _split(a)
    bh, bl = _split(b)
    return _dot(ah, bh) + _dot(al, bh) + _dot(ah, bl)


def _rms(x):
    return x * lax.rsqrt(jnp.mean(x * x, axis=-1, keepdims=True) + EPS)


def _gelu(x):
    return 0.5 * x * (1.0 + jnp.tanh(0.7978845608028654 * (x + 0.044715 * (x * x * x))))


def _gelu_gate(a, half_g):
    u = a * (0.7978845608028654 + 0.035677408136300125 * (a * a))
    return (a * half_g) * (1.0 + jnp.tanh(u))


def _sigmoid(x):
    return 1.0 / (1.0 + jnp.exp(-x))


def _lane_group(shape, width):
    return lax.broadcasted_iota(jnp.int32, shape, len(shape) - 1) // width


def _mod_kernel(c_ref, w_ref, b_ref, o_ref):
    c = c_ref[...]
    s = c * _sigmoid(c)
    o_ref[...] = _dot_x3(s, w_ref[...]) + b_ref[...]


def _modulation(c8, w_mod, b_mod):
    L, D, W = w_mod.shape
    tn = MOD_COL_TILE
    return pl.pallas_call(
        _mod_kernel,
        grid=(L, W // tn),
        in_specs=[
            pl.BlockSpec((8, D), lambda l, j: (0, 0)),
            pl.BlockSpec((None, D, tn), lambda l, j: (l, 0, j)),
            pl.BlockSpec((None, 1, tn), lambda l, j: (l, 0, j)),
        ],
        out_specs=pl.BlockSpec((None, 8, tn), lambda l, j: (l, 0, j)),
        out_shape=jax.ShapeDtypeStruct((L, 8, W), F32),
        compiler_params=_cparams("arbitrary", "arbitrary"),
        name="modulation",
    )(c8, w_mod, b_mod.reshape(L, 1, W))


def _mod_spec(l, j, D, bfn):
    return pl.BlockSpec((None, None, 1, D), lambda *g: (l, bfn(*g), 0, j))


def _vec_spec(l, width):
    return pl.BlockSpec((None, 1, width), lambda *g: (l, 0, 0))


IN_SLABS = (("a", 0, 2 * GROUP_W), ("b", 512, 4 * GROUP_W), ("c", 1536, GROUP_W), ("d", 1792, MLA_Q_LANES + MLA_KV_LANES),
            ("g", 2176, LANES))
IN_PAD_COLS = 2304


def _sgu_tile(p, gn_ref, w_ref, b_ref, ones_ref, o_ref):
    tm = p.shape[0]
    g = _gelu(p)
    u = g[:, :GROUP_W]
    v = g[:, GROUP_W:]
    ms = _dot_x2(v * v, ones_ref[...]) * (1.0 / HEAD_D)
    vb = (v * lax.rsqrt(ms + EPS) * gn_ref[...]).astype(BF16)
    head = _lane_group((MLP_CHUNK, GROUP_W), HEAD_D)
    w = w_ref[...]
    for c in range(tm // MLP_CHUNK):
        rows = slice(c * MLP_CHUNK, (c + 1) * MLP_CHUNK)
        r = _dot(w, vb[rows])
        s = b_ref[...]
        for h in range(GROUP_W // HEAD_D):
            s = s + jnp.where(head == h, r[h * MLP_CHUNK:(h + 1) * MLP_CHUNK], 0.0)
        o_ref[rows, :] = (u[rows] * s).astype(o_ref.dtype)


def _inproj_kernel(x_ref, g_ref, sc_ref, sh_ref, w_ref, gn_ref, wsp_ref, bsp_ref, ones_ref, oa, ob, oc, od, og):
    h = _rms(x_ref[...]) * g_ref[...] * (1.0 + sc_ref[...]) + sh_ref[...]
    hb = h.astype(BF16)
    (_, off_a, width_a) = IN_SLABS[0]
    pa = _dot(hb, w_ref[:, off_a:off_a + width_a])
    for (_, off, width), o_ref in zip(IN_SLABS[1:], (ob, oc, od, og)):
        o_ref[...] = _dot(hb, w_ref[:, off:off + width])
    _sgu_tile(pa, gn_ref, wsp_ref, bsp_ref, ones_ref, oa)


def _inproj(l, x2, mod4, brow, g_pre, w_in_p, sgu, tm):
    R, D = x2.shape
    gn, w_st, bias, ones_bd = sgu
    H = GROUP_W // HEAD_D
    grid = (R // tm,)
    outs = [jax.ShapeDtypeStruct((R, GROUP_W), BF16)] + [jax.ShapeDtypeStruct((R, width), F32) for (_, _, width) in IN_SLABS[1:]]
    return pl.pallas_call(
        _inproj_kernel,
        grid=grid,
        in_specs=[
            pl.BlockSpec((tm, D), lambda i: (i, 0)),
            _vec_spec(l, D),
            _mod_spec(l, 1, D, brow),
            _mod_spec(l, 0, D, brow),
            pl.BlockSpec((None, D, IN_PAD_COLS), lambda i: (l, 0, 0)),
            _vec_spec(l, GROUP_W),
            pl.BlockSpec((None, H * MLP_CHUNK, MLP_CHUNK), lambda i: (l, 0, 0)),
            pl.BlockSpec((None, MLP_CHUNK, GROUP_W), lambda i: (l, 0, 0)),
            pl.BlockSpec((GROUP_W, GROUP_W), lambda i: (0, 0)),
        ],
        out_specs=[pl.BlockSpec((tm, GROUP_W), lambda i: (i, 0))]
        + [pl.BlockSpec((tm, width), lambda i: (i, 0)) for (_, _, width) in IN_SLABS[1:]],
        out_shape=outs,
        compiler_params=_cparams("parallel"),
        name="inproj",
    )(x2, g_pre, mod4, mod4, w_in_p, gn, w_st, bias, ones_bd)


def _gla_kernel(*refs, rev, finish):
    if finish:
        (pb_ref, pg_ref, wg_ref, bg_ref, s0_ref, ones_ref, dec_ref, oprev_ref, gn_ref, o_ref, sfin_ref, st_scr) = refs
    else:
        (pb_ref, pg_ref, wg_ref, bg_ref, s0_ref, ones_ref, dec_ref, o_ref, sfin_ref, st_scr) = refs
    i = pl.program_id(1)
    C, W, T = GLA_CHUNK, GROUP_W, GLA_TILE
    H = W // HEAD_D
    nsub = T // C
    ntile = pb_ref.shape[0] // T

    @pl.when(i == 0)
    def _():
        st_scr[...] = s0_ref[...]

    tri = dec_ref[...]
    mask4 = jnp.concatenate([tri.astype(F32)] * H, axis=0)
    head_t = _lane_group((T, W), HEAD_D)
    bd = (lax.broadcasted_iota(jnp.int32, (W, W), 0) // HEAD_D) == _lane_group((W, W), HEAD_D)
    tiles = list(range(ntile - 1, -1, -1)) if rev else list(range(ntile))
    subs = list(range(nsub - 1, -1, -1)) if rev else list(range(nsub))
    rows = lambda a: slice(a * T, (a + 1) * T)
    sub = lambda j: slice(j * C, (j + 1) * C)
    stack = lambda blocks: jnp.concatenate([blocks[j] for j in range(nsub)], axis=0)

    logg = {}
    for a in tiles:
        z = _dot_x3(pg_ref[rows(a), :], wg_ref[...]) + bg_ref[...]
        logg[a] = (jnp.minimum(z, 0.0) - jnp.log(1.0 + jnp.exp(-jnp.abs(z)))) * (1.0 / GATE_TEMP)
    cums = {}
    for a in tiles:
        g_hi, g_lo = _split(logg[a])
        cums[a] = _dot(tri, g_hi) + _dot(tri, g_lo)
    qin, qoff, kin, kend, vb, tot, ptile = {}, {}, {}, {}, {}, {}, {}
    for a in tiles:
        cum = cums[a]
        q = pb_ref[rows(a), 0:W] * (HEAD_D ** -0.5)
        k = pb_ref[rows(a), W:2 * W]
        last_row = (lambda j: j * C) if rev else (lambda j: j * C + C - 1)
        off, run = {}, jnp.zeros((1, W), F32)
        for j in subs:
            tot[(a, j)] = cum[last_row(j):last_row(j) + 1]
            off[j] = run
            run = run + tot[(a, j)]
        ptile[a] = jnp.exp(run)
        e = jnp.exp(cum)
        qin[a] = (q * e).astype(BF16)
        qoff[a] = (q * (e * jnp.exp(stack({j: jnp.broadcast_to(off[j], (C, W)) for j in subs})))).astype(BF16)
        kin[a] = (k * jnp.exp(-cum)).astype(BF16)
        kend[a] = (k * jnp.exp(stack({j: jnp.broadcast_to(tot[(a, j)], (C, W)) for j in subs}) - cum)).astype(BF16)
        vb[a] = pb_ref[rows(a), 2 * W:3 * W].astype(BF16)
    sc = {}
    for a in tiles:
        qst = jnp.concatenate([jnp.where(head_t == h, qin[a], jnp.zeros_like(qin[a])) for h in range(H)], axis=0)
        sc[a] = (_dot_nt(qst, kin[a]) * mask4).astype(BF16)
    ost = {a: _dot(sc[a], vb[a]) for a in tiles}
    kvt = {(a, j): jnp.where(bd, _dot_tn(vb[a][sub(j)], kend[a][sub(j)]), 0.0) for a in tiles for j in subs}
    cross, rend = {}, {}
    for a in tiles:
        r = None
        for j in subs:
            if r is not None:
                cross[(a, j)] = _dot_nt(qin[a][sub(j)], r.astype(BF16))
                r = r * jnp.exp(tot[(a, j)]) + kvt[(a, j)]
            else:
                cross[(a, j)] = jnp.zeros((C, W), F32)
                r = kvt[(a, j)]
        rend[a] = r
    st = st_scr[...]
    inter = {}
    for a in tiles:
        inter[a] = _dot_nt(qoff[a], st.astype(BF16))
        st = st * ptile[a] + rend[a]
    st_scr[...] = st

    for a in tiles:
        o = inter[a] + jnp.concatenate([cross[(a, j)] for j in range(nsub)], axis=0)
        for h in range(H):
            o = o + jnp.where(head_t == h, ost[a][h * T:(h + 1) * T], 0.0)
        if finish:
            o = o + oprev_ref[rows(a), :]
            ms = _dot_x2(o * o, ones_ref[...]) * (1.0 / HEAD_D)
            o = o * lax.rsqrt(ms + EPS) * gn_ref[...]
            r = pb_ref[rows(a), 3 * W:4 * W]
            o_ref[rows(a), :] = (o * (r * _sigmoid(r))).astype(o_ref.dtype)
        else:
            o_ref[rows(a), :] = o

    @pl.when(i == pl.num_programs(1) - 1)
    def _():
        sfin_ref[...] = st_scr[...]


def _gla_decay_matrices():
    T, C = GLA_TILE, GLA_CHUNK
    t = jnp.arange(T)[:, None]
    s = jnp.arange(T)[None, :]
    same = (t // C) == (s // C)
    return jnp.stack([same & (s <= t), same & (s >= t)]).astype(BF16)


def _gla_pass(l, d, pb3, pg3, wg, bg, s0, ones_bd, dec, oprev, gn, tt):
    B, n, _ = pb3.shape
    nt = n // tt
    rev = d == 1
    finish = oprev is not None
    W = GROUP_W

    def tok(b, i):
        return (b, (nt - 1 - i) if rev else i, 0)

    in_specs = [
        pl.BlockSpec((None, tt, 4 * W), tok),
        pl.BlockSpec((None, tt, LANES), tok),
        pl.BlockSpec((None, None, LANES, W), lambda b, i: (l, d, 0, 0)),
        pl.BlockSpec((None, None, 1, W), lambda b, i: (l, d, 0, 0)),
        pl.BlockSpec((None, W, W), lambda b, i: (b, 0, 0)),
        pl.BlockSpec((W, W), lambda b, i: (0, 0)),
        pl.BlockSpec((None, GLA_TILE, GLA_TILE), lambda b, i: (d, 0, 0)),
    ]
    args = [pb3, pg3, wg, bg, s0, ones_bd, dec]
    if finish:
        in_specs += [pl.BlockSpec((None, tt, W), tok), _vec_spec(l, W)]
        args += [oprev, gn]
    return pl.pallas_call(
        functools.partial(_gla_kernel, rev=rev, finish=finish),
        grid=(B, nt),
        in_specs=in_specs,
        out_specs=[pl.BlockSpec((None, tt, W), tok), pl.BlockSpec((None, W, W), lambda b, i: (b, 0, 0))],
        out_shape=[jax.ShapeDtypeStruct((B, n, W), BF16 if finish else F32),
                   jax.ShapeDtypeStruct((B, W, W), F32)],
        scratch_shapes=[pltpu.VMEM((W, W), F32)],
        compiler_params=_cparams("arbitrary", "arbitrary"),
        name="gla_bwd" if rev else "gla_fwd",
    )(*args)


def _gla(l, pb_c, pg_c, pb_x, pg_x, wg, bg, gn, ones_bd, tt_c, tt_x):
    B = pb_x.shape[0]
    zero = jnp.zeros((B, GROUP_W, GROUP_W), F32)
    dec = _gla_decay_matrices()
    ofc, sfc = _gla_pass(l, 0, pb_c, pg_c, wg, bg, zero, ones_bd, dec, None, None, tt_c)
    ofx, _ = _gla_pass(l, 0, pb_x, pg_x, wg, bg, sfc, ones_bd, dec, None, None, tt_x)
    ob_c, sbc = _gla_pass(l, 1, pb_c, pg_c, wg, bg, zero, ones_bd, dec, ofc, gn, tt_c)
    ob_x, _ = _gla_pass(l, 1, pb_x, pg_x, wg, bg, sbc, ones_bd, dec, ofx, gn, tt_x)
    return ob_c, ob_x


def _s5_matrices(a_re, a_im, log_dt, b_re, b_im, c_re, c_im, d_skip):
    T = S5_T
    G, P = a_re.shape[1:]
    I = b_re.shape[-1]
    lam = lax.complex(a_re.astype(F32), a_im.astype(F32))
    ldt = lam * jnp.exp(log_dt.astype(F32))[..., None]
    lam_bar = jnp.exp(ldt)
    b_bar = ((lam_bar - 1.0) / lam)[..., None] * lax.complex(b_re.astype(F32), b_im.astype(F32))
    cmat = lax.complex(c_re.astype(F32), c_im.astype(F32))
    steps = jnp.arange(T + 1, dtype=F32)
    pw = jnp.exp(ldt[..., None] * steps)
    taps = jnp.einsum('dgop,dgpk,dgpi->dgiko', cmat, pw[..., :T], b_bar).real
    taps = taps.at[0, :, :, 0, :].add(jnp.eye(I, dtype=F32)[None] * d_skip.astype(F32)[:, :, None])
    row = T * I
    zeros = jnp.zeros((G, I, row), F32)
    fwd = jnp.concatenate([zeros, taps[0].reshape(G, I, row)], axis=-1)
    bwd = jnp.concatenate([jnp.flip(taps[1], axis=2).reshape(G, I, row), zeros], axis=-1)
    m = jnp.stack([fwd[..., (T - s) * I:(T - s) * I + row] + bwd[..., (T - 1 - s) * I:(T - 1 - s) * I + row]
                   for s in range(T)], axis=1).reshape(G, row, row)

    ar = jnp.arange(T)
    pf = pw[0][..., T - 1 - ar]
    pb = pw[1][..., ar]
    bf = jnp.einsum('gps,gpi->gsip', pf, b_bar[0]).reshape(G, T * I, P)
    bb = jnp.einsum('gps,gpi->gsip', pb, b_bar[1]).reshape(G, T * I, P)
    bmat = jnp.concatenate([bf.real, bf.imag, bf.imag, bf.real, bb.real, bb.imag, bb.imag, bb.real], axis=-1)

    cf = jnp.einsum('gop,gpt->gpto', cmat[0], pw[0][..., 1 + ar]).reshape(G, P, T * I)
    cb = jnp.einsum('gop,gpt->gpto', cmat[1], pw[1][..., T - ar]).reshape(G, P, T * I)
    w = jnp.concatenate([m, cf.real, -cf.imag, cb.real, -cb.imag], axis=1)

    a = pw[..., T]
    acoef = jnp.stack([jnp.concatenate([a[0].real, a[0].real], axis=-1), jnp.concatenate([-a[0].imag, a[0].imag], axis=-1),
                       jnp.concatenate([a[1].real, a[1].real], axis=-1), jnp.concatenate([-a[1].imag, a[1].imag], axis=-1)])
    return bmat.astype(BF16), w.astype(BF16), acoef


def _s5_scan_kernel(xf_ref, xb_ref, bm_ref, a_ref, h0_ref, hf_ref, hb_ref, hfin_ref, st_scr, sf_scr, sb_scr, hf_scr, hb_scr):
    j = pl.program_id(1)
    G, tc, _ = xf_ref.shape
    pitch = tc + S5_PITCH_PAD
    half = hf_ref.shape[-1]

    @pl.when(j == 0)
    def _():
        st_scr[...] = h0_ref[...]

    for g in range(G):
        sf = _dot(xf_ref[g], bm_ref[g, :, 0:2 * half])
        sb = _dot(xb_ref[g], bm_ref[g, :, 2 * half:4 * half])
        for k in range(2):
            sf_scr[k, g * pitch:g * pitch + tc, :] = sf[:, k * half:(k + 1) * half]
            sb_scr[k, g * pitch:g * pitch + tc, :] = sb[:, k * half:(k + 1) * half]

    a1f, a2f, a1b, a2b = a_ref[0], a_ref[1], a_ref[2], a_ref[3]
    chunk = lambda i: pl.ds(i, G, stride=pitch)

    def body(i, hs):
        h1f, h2f, h1b, h2b = hs
        ib = tc - 1 - i
        hf_scr[chunk(i), :] = h1f
        hb_scr[chunk(ib), :] = h1b
        return (h1f * a1f + h2f * a2f + sf_scr[0, chunk(i), :], h2f * a1f - h1f * a2f + sf_scr[1, chunk(i), :],
                h1b * a1b + h2b * a2b + sb_scr[0, chunk(ib), :], h2b * a1b - h1b * a2b + sb_scr[1, chunk(ib), :])

    hs = lax.fori_loop(0, tc, body, (st_scr[0], st_scr[1], st_scr[2], st_scr[3]))
    for k in range(4):
        st_scr[k] = hs[k]
    for g in range(G):
        hf_ref[g] = hf_scr[g * pitch:g * pitch + tc, :]
        hb_ref[g] = hb_scr[g * pitch:g * pitch + tc, :]

    @pl.when(j == pl.num_programs(1) - 1)
    def _():
        hfin_ref[...] = st_scr[...]


def _s5_scan(l, xg, bmat, acoef, h0, B, tc):
    G, R, K = xg.shape
    nt = R // B // tc
    half = bmat.shape[-1] // 4
    fwd = lambda b, j: (0, b * nt + j, 0)
    bwd = lambda b, j: (0, b * nt + nt - 1 - j, 0)
    st_spec = pl.BlockSpec((None, 4, G, half), lambda b, j: (b, 0, 0, 0))
    pitch = tc + S5_PITCH_PAD
    return pl.pallas_call(
        _s5_scan_kernel,
        grid=(B, nt),
        in_specs=[pl.BlockSpec((G, tc, K), fwd),
                  pl.BlockSpec((G, tc, K), bwd),
                  pl.BlockSpec((None, G, K, 4 * half), lambda b, j: (l, 0, 0, 0)),
                  pl.BlockSpec((None, 4, G, half), lambda b, j: (l, 0, 0, 0)),
                  st_spec],
        out_specs=[pl.BlockSpec((G, tc, half), fwd), pl.BlockSpec((G, tc, half), bwd), st_spec],
        out_shape=[jax.ShapeDtypeStruct((G, R, half), F32), jax.ShapeDtypeStruct((G, R, half), F32),
                   jax.ShapeDtypeStruct((B, 4, G, half), F32)],
        scratch_shapes=[pltpu.VMEM((4, G, half), F32),
                        pltpu.VMEM((2, G * pitch, half), F32), pltpu.VMEM((2, G * pitch, half), F32),
                        pltpu.VMEM((G * pitch, half), F32), pltpu.VMEM((G * pitch, half), F32)],
        compiler_params=_cparams("arbitrary", "arbitrary"),
        name="s5_scan",
    )(xg, xg, bmat, acoef, h0)


def _s5_out_kernel(x_ref, hf_ref, hb_ref, w_ref, o_ref):
    k = x_ref.shape[-1]
    kh = hf_ref.shape[-1]
    y = _dot(x_ref[...], w_ref[0:k, :])
    y = y + _dot(hf_ref[...].astype(BF16), w_ref[k:k + kh, :])
    o_ref[...] = y + _dot(hb_ref[...].astype(BF16), w_ref[k + kh:, :])


def _s5_out(l, xg, hf, hb, w, tr):
    G, R, K = xg.shape
    kh = hf.shape[-1]
    row = lambda width: pl.BlockSpec((None, tr, width), lambda g, i: (g, i, 0))
    return pl.pallas_call(
        _s5_out_kernel,
        grid=(G, R // tr),
        in_specs=[row(K), row(kh), row(kh), pl.BlockSpec((None, None, K + 2 * kh, K), lambda g, i: (l, g, 0, 0))],
        out_specs=row(K),
        out_shape=jax.ShapeDtypeStruct((G, R, K), F32),
        compiler_params=_cparams("parallel", "parallel"),
        name="s5_out",
    )(xg, hf, hb, w)


def _s5_finish_kernel(y_ref, w_ref, b_ref, o_ref):
    y = _gelu(y_ref[...])
    gate = _sigmoid(_dot(y.astype(BF16), w_ref[...]) + b_ref[...])
    o_ref[...] = (y * gate).astype(o_ref.dtype)


def _s5_finish(l, y2, w_glu, b_glu, tm):
    R, W = y2.shape
    return pl.pallas_call(
        _s5_finish_kernel,
        grid=(R // tm,),
        in_specs=[pl.BlockSpec((tm, W), lambda i: (i, 0)),
                  pl.BlockSpec((None, W, W), lambda i: (l, 0, 0)),
                  _vec_spec(l, W)],
        out_specs=pl.BlockSpec((tm, W), lambda i: (i, 0)),
        out_shape=jax.ShapeDtypeStruct((R, W), BF16),
        compiler_params=_cparams("parallel"),
        name="s5_finish",
    )(y2, w_glu, b_glu)


def _to_groups(u3):
    B, n, W = u3.shape
    G = W // S5_IN
    x = u3.astype(BF16).reshape(B, n // S5_T, S5_T, G, S5_IN).transpose(3, 0, 1, 2, 4)
    return x.reshape(G, B * (n // S5_T), S5_T * S5_IN)


def _from_groups(y, B):
    G, R, _ = y.shape
    nc = R // B
    y = y.reshape(G, B, nc, S5_T, S5_IN).transpose(1, 2, 3, 0, 4)
    return y.reshape(B * nc * S5_T, G * S5_IN)


def _s5_gather_kernel(x_ref, o_ref, xs_scr, xt_scr):
    G, tc, _ = o_ref.shape
    nh = x_ref.shape[1] // LANES
    gh = G // nh
    for h in range(nh):
        xs_scr[h] = x_ref[:, h * LANES:(h + 1) * LANES]
    for t in range(S5_T):
        for h in range(nh):
            rt = xs_scr[h, pl.ds(t, tc, stride=S5_T), :].T
            for g in range(gh):
                xt_scr[h * gh + g, t * S5_IN:(t + 1) * S5_IN, :] = rt[g * S5_IN:(g + 1) * S5_IN, :]
    for g in range(G):
        o_ref[g] = xt_scr[g].T.astype(o_ref.dtype)


def _s5_gather(x2, tc):
    N, W = x2.shape
    G = W // S5_IN
    R = N // S5_T
    return pl.pallas_call(
        _s5_gather_kernel,
        grid=(R // tc,),
        in_specs=[pl.BlockSpec((tc * S5_T, W), lambda i: (i, 0))],
        out_specs=pl.BlockSpec((G, tc, S5_T * S5_IN), lambda i: (0, i, 0)),
        out_shape=jax.ShapeDtypeStruct((G, R, S5_T * S5_IN), BF16),
        scratch_shapes=[pltpu.VMEM((W // LANES, tc * S5_T, LANES), F32), pltpu.VMEM((G, S5_T * S5_IN, tc), F32)],
        compiler_params=_cparams("parallel"),
        name="s5_gather",
    )(x2)


def _s5_scatter_kernel(x_ref, hf_ref, hb_ref, wy_ref, w_ref, b_ref, o_ref, yt_scr, tok_scr):
    G, tc, k = x_ref.shape
    kh = hf_ref.shape[-1]
    nh = tok_scr.shape[0]
    gh = G // nh
    for g in range(G):
        y = _dot(x_ref[g], wy_ref[g, 0:k, :])
        y = y + _dot(hf_ref[g].astype(BF16), wy_ref[g, k:k + kh, :])
        y = y + _dot(hb_ref[g].astype(BF16), wy_ref[g, k + kh:, :])
        yt = y.T
        for t in range(S5_T):
            yt_scr[t, g // gh, (g % gh) * S5_IN:(g % gh + 1) * S5_IN, :] = yt[t * S5_IN:(t + 1) * S5_IN, :]
    for t in range(S5_T):
        for h in range(nh):
            tok_scr[h, pl.ds(t, tc, stride=S5_T), :] = yt_scr[t, h].T
    y = _gelu(jnp.concatenate([tok_scr[h] for h in range(nh)], axis=1))
    gate = _sigmoid(_dot(y.astype(BF16), w_ref[...]) + b_ref[...])
    o_ref[...] = (y * gate).astype(o_ref.dtype)


def _s5_scatter_finish(l, xg, hf, hb, wy, w_glu, b_glu, tc):
    G, R, K = xg.shape
    kh = hf.shape[-1]
    W = G * S5_IN
    grp = lambda width: pl.BlockSpec((G, tc, width), lambda i: (0, i, 0))
    return pl.pallas_call(
        _s5_scatter_kernel,
        grid=(R // tc,),
        in_specs=[grp(K), grp(kh), grp(kh),
                  pl.BlockSpec((None, G, K + 2 * kh, K), lambda i: (l, 0, 0, 0)),
                  pl.BlockSpec((None, W, W), lambda i: (l, 0, 0)),
                  _vec_spec(l, W)],
        out_specs=pl.BlockSpec((tc * S5_T, W), lambda i: (i, 0)),
        out_shape=jax.ShapeDtypeStruct((R * S5_T, W), BF16),
        scratch_shapes=[pltpu.VMEM((S5_T, W // LANES, LANES, tc), F32), pltpu.VMEM((W // LANES, tc * S5_T, LANES), F32)],
        compiler_params=_cparams("parallel"),
        name="s5_scatter",
    )(xg, hf, hb, wy, w_glu, b_glu)


def _s5(l, pc_c, pc_x, mats, w_glu, b_glu, need_ctx, tm_c):
    bmat, w, acoef = mats
    B, n, W = pc_x.shape
    G = W // S5_IN
    xg_c = _to_groups(pc_c)
    tc_x = min(S5_TILE, n // S5_T)
    xg_x = _s5_gather(pc_x.reshape(B * n, W), tc_x)
    tr_c = min(S5_ROW_TILE, xg_c.shape[1])
    h0 = jnp.zeros((B, 4, G, acoef.shape[-1]), F32)
    hf_c, hb_c, h1 = _s5_scan(l, xg_c, bmat, acoef, h0, B, xg_c.shape[1] // B)
    hf_x, hb_x, _ = _s5_scan(l, xg_x, bmat, acoef, h1, B, tc_x)
    oc_x = _s5_scatter_finish(l, xg_x, hf_x, hb_x, w, w_glu, b_glu, tc_x)
    oc_c = None
    if need_ctx:
        y_c = _from_groups(_s5_out(l, xg_c, hf_c, hb_c, w, tr_c), B)
        oc_c = _s5_finish(l, y_c, w_glu, b_glu, tm_c)
    return oc_c, oc_x


def _mla_prep_kernel(*refs, rope):
    refs = refs[:-5] + refs[-3:]
    if rope:
        pd_ref, qn_ref, kvn_ref, wq_ref, wqs_ref, wk_ref, wks_ref, wv_ref, cos_ref, sin_ref, q_ref, k_ref, v_ref = refs
    else:
        pd_ref, qn_ref, kvn_ref, wq_ref, wk_ref, wv_ref, q_ref, k_ref, v_ref = refs
    cq = pd_ref[:, 0:MLA_Q_LANES]
    ms = jnp.sum(cq * cq, axis=-1, keepdims=True) * (1.0 / MLA_Q_RANK)
    cqn = (cq * lax.rsqrt(ms + EPS) * qn_ref[...]).astype(BF16)
    ck = pd_ref[:, MLA_Q_LANES:MLA_Q_LANES + MLA_KV_LANES]
    lane = lax.broadcasted_iota(jnp.int32, ck.shape, 1)
    is_lat = lane < MLA_KV_RANK
    ms = jnp.sum(jnp.where(is_lat, ck * ck, 0.0), axis=-1, keepdims=True) * (1.0 / MLA_KV_RANK)
    ckn = jnp.where(is_lat, ck * lax.rsqrt(ms + EPS) * kvn_ref[...], ck).astype(BF16)
    q = _dot(cqn, wq_ref[...])
    k = _dot(ckn, wk_ref[...])
    if rope:
        cos = jnp.concatenate([cos_ref[...]] * MLA_HEADS, axis=1)
        sin = jnp.concatenate([sin_ref[...]] * MLA_HEADS, axis=1)
        q = q * cos + _dot(cqn, wqs_ref[...]) * sin
        k = k * cos + _dot(ckn, wks_ref[...]) * sin
    q_ref[...] = (q * ((MLA_NOPE + MLA_ROPE) ** -0.5 * LOG2E)).astype(BF16)
    k_ref[...] = k.astype(BF16)
    v_ref[...] = _dot(ckn, wv_ref[...]).astype(BF16)


def _mla_prep(l, pd, qn, kvn, wts, tables, tm, n, B, row0, nk, kv_into):
    R = pd.shape[0]
    HP = MLA_HEADS * MLA_HEAD_PAD
    wq, wqs, wk, wks, wv = wts
    rope = tables is not None
    npt = n // tm
    wspec = lambda r: pl.BlockSpec((None, r, HP), lambda i: (l, 0, 0))
    QL, KL = MLA_Q_LANES, MLA_KV_LANES
    in_specs = [pl.BlockSpec((tm, QL + KL), lambda i: (i, 0)), _vec_spec(l, QL), _vec_spec(l, KL)]
    if rope:
        tspec = pl.BlockSpec((tm, MLA_HEAD_PAD), lambda i: (i % npt, 0))
        in_specs += [wspec(QL), wspec(QL), wspec(KL), wspec(KL), wspec(KL), tspec, tspec]
        args = (pd, qn, kvn, wq, wqs, wk, wks, wv) + tuple(tables)
    else:
        in_specs += [wspec(QL), wspec(KL), wspec(KL)]
        args = (pd, qn, kvn, wq, wk, wv)
    aliases = {len(args): 1, len(args) + 1: 2}
    in_specs += [pl.BlockSpec(memory_space=pl.ANY)] * 2
    args = args + tuple(kv_into)
    kv_spec = pl.BlockSpec((None, tm, HP), lambda i: (i // npt, row0 // tm + i % npt, 0))
    kv_shape = jax.ShapeDtypeStruct((B, nk, HP), BF16)
    return pl.pallas_call(
        functools.partial(_mla_prep_kernel, rope=rope),
        grid=(R // tm,),
        in_specs=in_specs,
        out_specs=[pl.BlockSpec((tm, HP), lambda i: (i, 0)), kv_spec, kv_spec],
        out_shape=[jax.ShapeDtypeStruct((R, HP), BF16), kv_shape, kv_shape],
        input_output_aliases=aliases,
        compiler_params=_cparams("parallel"),
        name="mla_prep",
    )(*args)


def _attn_kernel(q_ref, k_ref, v_ref, o_ref, m_scr, l_scr, acc_scr):
    kv = pl.program_id(2)
    HP = MLA_HEAD_PAD
    tk = k_ref.shape[0]

    @pl.when(kv == 0)
    def _():
        m_scr[...] = jnp.full(m_scr.shape, -jnp.inf, F32)
        l_scr[...] = jnp.zeros(l_scr.shape, F32)
        acc_scr[...] = jnp.zeros(acc_scr.shape, F32)

    def scores(h):
        lanes = slice(h * HP, (h + 1) * HP)
        return _dot_nt(q_ref[:, lanes], k_ref[:, lanes])

    s_next = scores(0)
    for h in range(MLA_HEADS):
        lanes = slice(h * HP, (h + 1) * HP)
        s = s_next
        if h + 1 < MLA_HEADS:
            s_next = scores(h + 1)
        m_prev = m_scr[h]
        m_new = jnp.maximum(m_prev, jnp.max(s, axis=1, keepdims=True))
        alpha = jnp.exp2(m_prev - m_new)
        p = jnp.exp2(s - m_new[:, 0:1])
        lp = p[:, 0:LANES]
        for c in range(1, tk // LANES):
            lp = lp + p[:, c * LANES:(c + 1) * LANES]
        l_scr[h] = alpha * l_scr[h] + lp
        acc_scr[h] = alpha * acc_scr[h] + _dot(p.astype(BF16), v_ref[:, lanes])
        m_scr[h] = m_new

    @pl.when(kv == pl.num_programs(2) - 1)
    def _():
        low = lax.broadcasted_iota(jnp.int32, acc_scr.shape[1:], 1) < MLA_V
        norm = lambda h: acc_scr[h] * (1.0 / jnp.sum(l_scr[h], axis=1, keepdims=True))
        outs = [jnp.where(low, norm(h), norm(h + 1)) for h in range(0, MLA_HEADS, 2)]
        o_ref[...] = jnp.concatenate(outs, axis=1).astype(o_ref.dtype)


def _attention(q3, k3, v3, tq, tk, k0, nk):
    B, nq, HP = q3.shape
    kb = k0 // tk
    return pl.pallas_call(
        _attn_kernel,
        grid=(B, nq // tq, nk // tk),
        in_specs=[pl.BlockSpec((None, tq, HP), lambda b, i, j: (b, i, 0)),
                  pl.BlockSpec((None, tk, HP), lambda b, i, j: (b, kb + j, 0)),
                  pl.BlockSpec((None, tk, HP), lambda b, i, j: (b, kb + j, 0))],
        out_specs=pl.BlockSpec((None, tq, MLA_HEADS * MLA_V), lambda b, i, j: (b, i, 0)),
        out_shape=jax.ShapeDtypeStruct((B, nq, MLA_HEADS * MLA_V), BF16),
        scratch_shapes=[pltpu.VMEM((MLA_HEADS, tq, LANES), F32),
                        pltpu.VMEM((MLA_HEADS, tq, LANES), F32),
                        pltpu.VMEM((MLA_HEADS, tq, MLA_HEAD_PAD), F32)],
        compiler_params=_cparams("parallel", "parallel", "arbitrary"),
        name="attention",
    )(q3, k3, v3)


def _outproj_kernel(oa, ob, oc, od, w_ref, x_ref, gpost_ref, gt_ref, gpre_ref, sc_ref, sh_ref, xo_ref, h_ref):
    W = GROUP_W
    mix = _dot(oa[...], w_ref[0:W, :])
    mix = mix + _dot(ob[...], w_ref[W:2 * W, :])
    mix = mix + _dot(oc[...], w_ref[2 * W:3 * W, :])
    mix = mix + _dot(od[...], w_ref[3 * W:4 * W, :])
    x = x_ref[...] + gt_ref[...] * (_rms(mix) * gpost_ref[...])
    xo_ref[...] = x
    h_ref[...] = (_rms(x) * gpre_ref[...] * (1.0 + sc_ref[...]) + sh_ref[...]).astype(h_ref.dtype)


def _outproj(l, parts, w_out, x2, mod4, brow, g_post, g_pre_ffn, tm):
    R, D = x2.shape
    W = GROUP_W
    part_spec = pl.BlockSpec((tm, W), lambda i: (i, 0))
    row_spec = pl.BlockSpec((tm, D), lambda i: (i, 0))
    return pl.pallas_call(
        _outproj_kernel,
        grid=(R // tm,),
        in_specs=[part_spec] * 4 + [
            pl.BlockSpec((None, 4 * W, D), lambda i: (l, 0, 0)),
            row_spec,
            _vec_spec(l, D),
            _mod_spec(l, 2, D, brow),
            _vec_spec(l, D),
            _mod_spec(l, 4, D, brow),
            _mod_spec(l, 3, D, brow),
        ],
        out_specs=[row_spec, row_spec],
        out_shape=[jax.ShapeDtypeStruct((R, D), F32), jax.ShapeDtypeStruct((R, D), BF16)],
        compiler_params=_cparams("parallel"),
        name="outproj",
    )(*parts, w_out, x2, g_post, mod4, g_pre_ffn, mod4, mod4)


FFN_HALO = 16
FFN_COLS = 256


def _ffn_kernel(hp_ref, h_ref, hn_ref, wup_ref, cw_ref, cb_ref, wdn_ref, x_ref, gpost_ref, gt_ref, o_ref, y_scr):
    i = pl.program_id(1)
    tm = h_ref.shape[0]
    dff = wdn_ref.shape[0]
    use = FFN_HALO // 2
    rows = tm + 2 * use
    prev = jnp.where(i == 0, jnp.zeros_like(hp_ref[...]), hp_ref[...])[FFN_HALO - use:]
    nxt = jnp.where(i == pl.num_programs(1) - 1, jnp.zeros_like(hn_ref[...]), hn_ref[...])[:use]
    hb = jnp.concatenate([prev, h_ref[...], nxt], axis=0)

    def conv(z, cols):
        w = cw_ref[:, cols]
        out = cb_ref[:, cols] + w[0:1] * pltpu.roll(z, 1, axis=0) + w[1:2] * z + w[2:3] * pltpu.roll(z, rows - 1, axis=0)
        return out[use:use + tm]

    def cols(j):
        return slice(j * FFN_COLS, (j + 1) * FFN_COLS), slice(dff + j * FFN_COLS, dff + (j + 1) * FFN_COLS)

    def up(j):
        ca, cg = cols(j)
        return _dot(hb, wup_ref[:, ca]), _dot(hb, wup_ref[:, cg])

    nchunks = dff // FFN_COLS
    z_next = up(0)
    for j in range(nchunks):
        ca, cg = cols(j)
        za, zg = z_next
        if j + 1 < nchunks:
            z_next = up(j + 1)
        y_scr[:, ca] = _gelu_gate(conv(za, ca), conv(zg, cg)).astype(BF16)
    acc = _dot(y_scr[...], wdn_ref[...])
    o_ref[...] = x_ref[...] + gt_ref[...] * (_rms(acc) * gpost_ref[...])


def _ffn(l, h3, x3, w_up, conv_w, conv_b, w_down, mod4, brow, g_post, tm):
    B, n, D = x3.shape
    nt = n // tm
    hb = tm // FFN_HALO
    nh = n // FFN_HALO
    dff = w_down.shape[1]
    once = pl.Buffered(1)
    return pl.pallas_call(
        _ffn_kernel,
        grid=(B, nt),
        in_specs=[
            pl.BlockSpec((None, FFN_HALO, D), lambda b, i: (b, jnp.maximum(i * hb - 1, 0), 0)),
            pl.BlockSpec((None, tm, D), lambda b, i: (b, i, 0)),
            pl.BlockSpec((None, FFN_HALO, D), lambda b, i: (b, jnp.minimum((i + 1) * hb, nh - 1), 0)),
            pl.BlockSpec((None, D, 2 * dff), lambda b, i: (l, 0, 0), pipeline_mode=once),
            pl.BlockSpec((None, 3, 2 * dff), lambda b, i: (l, 0, 0)),
            pl.BlockSpec((None, 1, 2 * dff), lambda b, i: (l, 0, 0)),
            pl.BlockSpec((None, dff, D), lambda b, i: (l, 0, 0), pipeline_mode=once),
            pl.BlockSpec((None, tm, D), lambda b, i: (b, i, 0)),
            _vec_spec(l, D),
            _mod_spec(l, 5, D, brow),
        ],
        out_specs=pl.BlockSpec((None, tm, D), lambda b, i: (b, i, 0)),
        out_shape=jax.ShapeDtypeStruct((B, n, D), F32),
        scratch_shapes=[pltpu.VMEM((tm, dff), BF16)],
        compiler_params=_cparams("parallel", "parallel"),
        name="conv_ffn",
    )(h3, h3, h3, w_up, conv_w, conv_b, w_down, x3, g_post, mod4)


def _prep_w_in(w_in):
    w_in = w_in.astype(BF16)
    z = lambda n: jnp.zeros(w_in.shape[:-1] + (n,), w_in.dtype)
    a = w_in[..., 0:512]
    b = w_in[..., 512:1536]
    gl = w_in[..., 1536:1568]
    c = w_in[..., 1568:1824]
    cq = w_in[..., 1824:2048]
    ckv_kr = w_in[..., 2048:2176]
    return jnp.concatenate([a, b, c, cq, z(32), ckv_kr, gl, z(96)], axis=-1)


def _rope_swap(t):
    q = MLA_ROPE // 4
    return jnp.concatenate([t[..., q:2 * q], t[..., 0:q], t[..., 3 * q:4 * q], t[..., 2 * q:3 * q]], axis=-1)


def _prep_mla(w_uq, w_ukv):
    L = w_uq.shape[0]
    H, NP, RP, HP = MLA_HEADS, MLA_NOPE, MLA_ROPE, MLA_HEAD_PAD
    wq = w_uq.reshape(L, MLA_Q_RANK, H, NP + RP)
    zq = jnp.zeros((L, MLA_Q_RANK, H, HP - NP - RP), w_uq.dtype)
    znope = jnp.zeros((L, MLA_Q_RANK, H, NP), w_uq.dtype)
    q_main = jnp.concatenate([wq, zq], axis=-1)
    q_swap = jnp.concatenate([znope, _rope_swap(wq[..., NP:]), zq], axis=-1)
    padq = lambda w: jnp.pad(w.reshape(L, MLA_Q_RANK, H * HP), ((0, 0), (0, MLA_Q_LANES - MLA_Q_RANK), (0, 0)))

    wkv = w_ukv.reshape(L, MLA_KV_RANK, H, NP + MLA_V)
    zk = jnp.zeros((L, MLA_KV_RANK, H, HP - NP), w_ukv.dtype)
    k_lat = jnp.concatenate([wkv[..., :NP], zk], axis=-1)
    eye = jnp.eye(RP, dtype=w_ukv.dtype)
    place = lambda e: jnp.broadcast_to(
        jnp.concatenate([jnp.zeros((RP, NP), e.dtype), e, jnp.zeros((RP, HP - NP - RP), e.dtype)], axis=-1)[None, :, None, :],
        (L, RP, H, HP))
    k_main = jnp.concatenate([k_lat, place(eye)], axis=1)
    k_swap = jnp.concatenate([jnp.zeros_like(k_lat), place(_rope_swap(eye))], axis=1)
    zv = jnp.zeros((L, MLA_KV_RANK, H, HP - MLA_V), w_ukv.dtype)
    odd = (jnp.arange(H) % 2 == 1)[None, None, :, None]
    v_lat = jnp.where(odd, jnp.concatenate([zv, wkv[..., NP:]], axis=-1), jnp.concatenate([wkv[..., NP:], zv], axis=-1))
    v_main = jnp.concatenate([v_lat, jnp.zeros((L, RP, H, HP), w_ukv.dtype)], axis=1)
    flat = lambda w: w.reshape(L, w.shape[1], H * HP).astype(BF16)
    return (padq(q_main).astype(BF16), padq(q_swap).astype(BF16), flat(k_main), flat(k_swap), flat(v_main))


def _rope_tables(n):
    rows = n // GRID_W
    nf = MLA_ROPE // 4
    inv = ROPE_BASE ** (-jnp.arange(nf, dtype=F32) / nf)
    ar = jnp.arange(rows, dtype=F32)[:, None] * inv[None, :]
    ac = jnp.arange(GRID_W, dtype=F32)[:, None] * inv[None, :]
    by_row = lambda t: jnp.repeat(t, GRID_W, axis=0)
    by_col = lambda t: jnp.tile(t, (rows, 1))
    cr, sr, cc, sn = by_row(jnp.cos(ar)), by_row(jnp.sin(ar)), by_col(jnp.cos(ac)), by_col(jnp.sin(ac))
    one = jnp.ones((n, MLA_NOPE), F32)
    zero = jnp.zeros((n, MLA_HEAD_PAD - MLA_NOPE - MLA_ROPE), F32)
    cos = jnp.concatenate([one, cr, cr, cc, cc, zero], axis=1)
    sin = jnp.concatenate([0.0 * one, -sr, sr, -sn, sn, zero], axis=1)
    return cos, sin


def _pick_tile(n, want):
    t = min(n, want)
    while n % t:
        t //= 2
    return t


def kernel(x, c, ctx, c_ctx, w_mod, b_mod, g_pre_mix, g_post_mix, g_pre_ffn, g_post_ffn, w_in,
           sgu_norm, sgu_w, sgu_b, gla_w_gate, gla_b_gate, gla_norm,
           s5_a_re, s5_a_im, s5_log_dt, s5_b_re, s5_b_im, s5_c_re, s5_c_im, s5_d, s5_w_glu, s5_b_glu,
           mla_q_norm, mla_w_uq, mla_kv_norm, mla_w_ukv, w_out,
           ffn_w_up, ffn_conv_w, ffn_conv_b, ffn_w_down):
    B, n, D = x.shape
    nctx = ctx.shape[1]
    L = w_mod.shape[0]
    W = GROUP_W
    assert B < 8 and n % TOKEN_TILE == 0 and nctx % GLA_TILE == 0 and n % GRID_W == 0

    c8 = jnp.concatenate([c, c_ctx[None, :], jnp.zeros((8 - B - 1, D), F32)], axis=0)
    mod4 = _modulation(c8, w_mod, b_mod).reshape(L, 8, 1, 6 * D)
    vec = lambda p: p.reshape(L, 1, -1).astype(F32)
    g_pre_mix, g_post_mix, g_pre_ffn, g_post_ffn = map(vec, (g_pre_mix, g_post_mix, g_pre_ffn, g_post_ffn))
    w_in_p = _prep_w_in(w_in)
    sgu_gn = vec(sgu_norm)
    sgu_w_st = sgu_w.reshape(L, -1, MLP_CHUNK).astype(BF16)
    sgu_bias = jnp.repeat(jnp.swapaxes(sgu_b, 1, 2), HEAD_D, axis=2).astype(F32)
    ones_bd = jnp.kron(jnp.eye(W // HEAD_D, dtype=F32), jnp.ones((HEAD_D, HEAD_D), F32)).astype(BF16)
    gla_wg = jnp.zeros((L, 2, LANES, W), F32)
    gla_wg = gla_wg.at[:, 0, 0:GATE_RANK].set(gla_w_gate[:, 0]).at[:, 1, GATE_RANK:2 * GATE_RANK].set(gla_w_gate[:, 1])
    gla_bg = gla_b_gate.reshape(L, 2, 1, W).astype(F32)
    gla_gn = vec(gla_norm)
    s5_wglu = s5_w_glu.astype(BF16)
    s5_bglu = vec(s5_b_glu)
    mla_qn = jnp.pad(mla_q_norm, ((0, 0), (0, MLA_Q_LANES - MLA_Q_RANK))).reshape(L, 1, MLA_Q_LANES).astype(F32)
    mla_kvn = jnp.pad(mla_kv_norm, ((0, 0), (0, MLA_KV_LANES - MLA_KV_RANK))).reshape(L, 1, MLA_KV_LANES).astype(F32)
    mla_wts = _prep_mla(mla_w_uq, mla_w_ukv)
    tables = _rope_tables(n)
    w_out_b = w_out.astype(BF16)
    w_up_b = ffn_w_up.astype(BF16)
    w_down_b = ffn_w_down.astype(BF16)
    dff = ffn_w_down.shape[1]
    half_gate = jnp.concatenate([jnp.ones((dff,), F32), jnp.full((dff,), 0.5, F32)])
    conv_w = ffn_conv_w.astype(F32) * half_gate
    conv_b = (ffn_conv_b.astype(F32) * half_gate).reshape(L, 1, -1)

    s5_mats = jax.vmap(_s5_matrices)(s5_a_re, s5_a_im, s5_log_dt, s5_b_re, s5_b_im, s5_c_re, s5_c_im, s5_d)

    tm_x = _pick_tile(n, TOKEN_TILE)
    tm_c = _pick_tile(nctx, TOKEN_TILE)
    tpb_x = n // tm_x
    brow_x = lambda i: i // tpb_x
    brow_c = lambda *g: B
    brow_x2 = lambda b, i: b

    xs = x.reshape(B * n, D)
    cs = ctx.reshape(B * nctx, D)

    for l in range(L):
        need_ctx = l < L - 1
        sgu = (sgu_gn, sgu_w_st, sgu_bias, ones_bd)
        oa_x, pb_x, pc_x, pd_x, pg_x = _inproj(l, xs, mod4, brow_x, g_pre_mix, w_in_p, sgu, tm_x)
        oa_c, pb_c, pc_c, pd_c, pg_c = _inproj(l, cs, mod4, brow_c, g_pre_mix, w_in_p, sgu, tm_c)

        r3 = lambda t, m: t.reshape(B, m, t.shape[-1])
        ob_c, ob_x = _gla(l, r3(pb_c, nctx), r3(pg_c, nctx), r3(pb_x, n), r3(pg_x, n),
                          gla_wg, gla_bg, gla_gn, ones_bd, _pick_tile(nctx, GLA_TILE), _pick_tile(n, 4 * GLA_TILE))
        oc_c, oc_x = _s5(l, r3(pc_c, nctx), r3(pc_x, n), s5_mats, s5_wglu, s5_bglu, need_ctx, tm_c)
        wq, wqs, wk, wks, wv = mla_wts
        nk = n + nctx
        kv0 = jnp.zeros((B, nk, MLA_HEADS * MLA_HEAD_PAD), BF16)
        q_x, k_all, v_all = _mla_prep(l, pd_x, mla_qn, mla_kvn, mla_wts, tables, tm_x, n, B, 0, nk, (kv0, kv0))
        q_c, k_all, v_all = _mla_prep(l, pd_c, mla_qn, mla_kvn, (wq, None, wk, None, wv), None, tm_c, nctx, B, n, nk,
                                      (k_all, v_all))
        tk = next(t for t in ATTN_KV_TILES if nk % t == 0)
        od_x = _attention(r3(q_x, n), k_all, v_all, tm_x, tk, 0, nk).reshape(B * n, W)

        xs, hx = _outproj(l, (oa_x, ob_x.reshape(B * n, W), oc_x, od_x), w_out_b, xs, mod4, brow_x,
                          g_post_mix, g_pre_ffn, tm_x)
        xs = _ffn(l, hx.reshape(B, n, D), xs.reshape(B, n, D), w_up_b, conv_w, conv_b, w_down_b, mod4, brow_x2,
                  g_post_ffn, tm_x).reshape(B * n, D)

        if need_ctx:
            od_c = _attention(r3(q_c, nctx), k_all, v_all, tm_c, nctx, n, nctx).reshape(B * nctx, W)
            cs, hc = _outproj(l, (oa_c, ob_c.reshape(B * nctx, W), oc_c, od_c), w_out_b, cs, mod4, brow_c,
                              g_post_mix, g_pre_ffn, tm_c)
            cs = _ffn(l, hc.reshape(B, nctx, D), cs.reshape(B, nctx, D), w_up_b, conv_w, conv_b, w_down_b, mod4,
                      brow_c, g_post_ffn, tm_c).reshape(B * nctx, D)
    return xs.reshape(B, n, D)
```

```python
import functools

import jax
import jax.numpy as jnp
from jax import lax
from jax.experimental import pallas as pl
from jax.experimental.pallas import tpu as pltpu

F32 = jnp.float32
BF16 = jnp.bfloat16

EPS = 1e-6
GRID_W = 64
GROUP_W = 256
HEAD_D = 64
MLP_CHUNK = 128
GATE_RANK = 16
GATE_TEMP = 16.0
GLA_CHUNK = 64
GLA_TILE = 256
S5_IN = 16
S5_T = 16
S5_TILE = 128
S5_PITCH_PAD = 8
MLA_HEADS = 4
MLA_NOPE = 64
MLA_ROPE = 32
MLA_V = 64
MLA_Q_RANK = 224
MLA_KV_RANK = 96
MLA_HEAD_PAD = 128
ROPE_BASE = 10000.0
LOG2E = 1.4426950408889634

LANES = 128
VMEM_LIMIT = 48 * 1024 * 1024
TOKEN_TILE = 512
MOD_COL_TILE = 1536
S5_ROW_TILE = 512
ATTN_KV_TILES = (2816, 1408, 768, 512, 256, 128)
MLA_Q_LANES = 256
MLA_KV_LANES = 128


def _cparams(*sem):
    return pltpu.CompilerParams(dimension_semantics=sem, vmem_limit_bytes=VMEM_LIMIT)


def _dot(a, b):
    return jnp.dot(a, b, preferred_element_type=F32)


def _dot_nt(a, b):
    return lax.dot_general(a, b, (((1,), (1,)), ((), ())), preferred_element_type=F32)


def _dot_tn(a, b):
    return lax.dot_general(a, b, (((0,), (0,)), ((), ())), preferred_element_type=F32)


def _split(a):
    hi = a.astype(BF16)
    lo = (a - hi.astype(F32)).astype(BF16)
    return hi, lo


def _dot_x2(a, b_bf16):
    hi, lo = _split(a)
    return _dot(hi, b_bf16) + _dot(lo, b_bf16)


def _dot_x3(a, b):
    ah, al = _split(a)
    bh, bl = _split(b)
    return _dot(ah, bh) + _dot(al, bh) + _dot(ah, bl)


def _rms(x):
    return x * lax.rsqrt(jnp.mean(x * x, axis=-1, keepdims=True) + EPS)


def _gelu(x):
    return 0.5 * x * (1.0 + jnp.tanh(0.7978845608028654 * (x + 0.044715 * (x * x * x))))


def _gelu_gate(a, half_g):
    u = a * (0.7978845608028654 + 0.035677408136300125 * (a * a))
    return (a * half_g) * (1.0 + jnp.tanh(u))


def _sigmoid(x):
    return 1.0 / (1.0 + jnp.exp(-x))


def _lane_group(shape, width):
    return lax.broadcasted_iota(jnp.int32, shape, len(shape) - 1) // width


def _mod_kernel(c_ref, w_ref, b_ref, o_ref):
    c = c_ref[...]
    s = c * _sigmoid(c)
    o_ref[...] = _dot_x3(s, w_ref[...]) + b_ref[...]


def _modulation(c8, w_mod, b_mod):
    L, D, W = w_mod.shape
    tn = MOD_COL_TILE
    return pl.pallas_call(
        _mod_kernel,
        grid=(L, W // tn),
        in_specs=[
            pl.BlockSpec((8, D), lambda l, j: (0, 0)),
            pl.BlockSpec((None, D, tn), lambda l, j: (l, 0, j)),
            pl.BlockSpec((None, 1, tn), lambda l, j: (l, 0, j)),
        ],
        out_specs=pl.BlockSpec((None, 8, tn), lambda l, j: (l, 0, j)),
        out_shape=jax.ShapeDtypeStruct((L, 8, W), F32),
        compiler_params=_cparams("arbitrary", "arbitrary"),
        name="modulation",
    )(c8, w_mod, b_mod.reshape(L, 1, W))


def _mod_spec(l, j, D, bfn):
    return pl.BlockSpec((None, None, 1, D), lambda *g: (l, bfn(*g), 0, j))


def _vec_spec(l, width):
    return pl.BlockSpec((None, 1, width), lambda *g: (l, 0, 0))


IN_SLABS = (("a", 0, 2 * GROUP_W), ("b", 512, 4 * GROUP_W), ("c", 1536, GROUP_W), ("d", 1792, MLA_Q_LANES + MLA_KV_LANES),
            ("g", 2176, LANES))
IN_PAD_COLS = 2304


def _sgu_tile(p, gn_ref, w_ref, b_ref, ones_ref, o_ref):
    tm = p.shape[0]
    g = _gelu(p)
    u = g[:, :GROUP_W]
    v = g[:, GROUP_W:]
    ms = _dot_x2(v * v, ones_ref[...]) * (1.0 / HEAD_D)
    vb = (v * lax.rsqrt(ms + EPS) * gn_ref[...]).astype(BF16)
    head = _lane_group((MLP_CHUNK, GROUP_W), HEAD_D)
    w = w_ref[...]
    for c in range(tm // MLP_CHUNK):
        rows = slice(c * MLP_CHUNK, (c + 1) * MLP_CHUNK)
        r = _dot(w, vb[rows])
        s = b_ref[...]
        for h in range(GROUP_W // HEAD_D):
            s = s + jnp.where(head == h, r[h * MLP_CHUNK:(h + 1) * MLP_CHUNK], 0.0)
        o_ref[rows, :] = (u[rows] * s).astype(o_ref.dtype)


def _inproj_kernel(x_ref, g_ref, sc_ref, sh_ref, w_ref, gn_ref, wsp_ref, bsp_ref, ones_ref, oa, ob, oc, od, og):
    h = _rms(x_ref[...]) * g_ref[...] * (1.0 + sc_ref[...]) + sh_ref[...]
    hb = h.astype(BF16)
    (_, off_a, width_a) = IN_SLABS[0]
    pa = _dot(hb, w_ref[:, off_a:off_a + width_a])
    for (_, off, width), o_ref in zip(IN_SLABS[1:], (ob, oc, od, og)):
        o_ref[...] = _dot(hb, w_ref[:, off:off + width]).astype(o_ref.dtype)
    _sgu_tile(pa, gn_ref, wsp_ref, bsp_ref, ones_ref, oa)


def _inproj(l, x2, mod4, brow, g_pre, w_in_p, sgu, tm):
    R, D = x2.shape
    gn, w_st, bias, ones_bd = sgu
    H = GROUP_W // HEAD_D
    grid = (R // tm,)
    outs = [jax.ShapeDtypeStruct((R, GROUP_W), BF16)] + [jax.ShapeDtypeStruct((R, width), BF16) for (_, _, width) in IN_SLABS[1:]]
    return pl.pallas_call(
        _inproj_kernel,
        grid=grid,
        in_specs=[
            pl.BlockSpec((tm, D), lambda i: (i, 0)),
            _vec_spec(l, D),
            _mod_spec(l, 1, D, brow),
            _mod_spec(l, 0, D, brow),
            pl.BlockSpec((None, D, IN_PAD_COLS), lambda i: (l, 0, 0)),
            _vec_spec(l, GROUP_W),
            pl.BlockSpec((None, H * MLP_CHUNK, MLP_CHUNK), lambda i: (l, 0, 0)),
            pl.BlockSpec((None, MLP_CHUNK, GROUP_W), lambda i: (l, 0, 0)),
            pl.BlockSpec((GROUP_W, GROUP_W), lambda i: (0, 0)),
        ],
        out_specs=[pl.BlockSpec((tm, GROUP_W), lambda i: (i, 0))]
        + [pl.BlockSpec((tm, width), lambda i: (i, 0)) for (_, _, width) in IN_SLABS[1:]],
        out_shape=outs,
        compiler_params=_cparams("parallel"),
        name="inproj",
    )(x2, g_pre, mod4, mod4, w_in_p, gn, w_st, bias, ones_bd)


def _gla_kernel(*refs, rev, finish):
    if finish:
        (pb_ref, pg_ref, wg_ref, bg_ref, s0_ref, ones_ref, dec_ref, oprev_ref, gn_ref, o_ref, sfin_ref, st_scr) = refs
    else:
        (pb_ref, pg_ref, wg_ref, bg_ref, s0_ref, ones_ref, dec_ref, o_ref, sfin_ref, st_scr) = refs
    i = pl.program_id(1)
    C, W, T = GLA_CHUNK, GROUP_W, GLA_TILE
    H = W // HEAD_D
    nsub = T // C
    ntile = pb_ref.shape[0] // T

    @pl.when(i == 0)
    def _():
        st_scr[...] = s0_ref[...]

    tri = dec_ref[...]
    mask4 = jnp.concatenate([tri.astype(F32)] * H, axis=0)
    head_t = _lane_group((T, W), HEAD_D)
    bd = (lax.broadcasted_iota(jnp.int32, (W, W), 0) // HEAD_D) == _lane_group((W, W), HEAD_D)
    tiles = list(range(ntile - 1, -1, -1)) if rev else list(range(ntile))
    subs = list(range(nsub - 1, -1, -1)) if rev else list(range(nsub))
    rows = lambda a: slice(a * T, (a + 1) * T)
    sub = lambda j: slice(j * C, (j + 1) * C)
    stack = lambda blocks: jnp.concatenate([blocks[j] for j in range(nsub)], axis=0)

    wg_hi, wg_lo = _split(wg_ref[...])
    logg = {}
    for a in tiles:
        z = _dot(pg_ref[rows(a), :], wg_hi) + _dot(pg_ref[rows(a), :], wg_lo) + bg_ref[...]
        logg[a] = (jnp.minimum(z, 0.0) - jnp.log(1.0 + jnp.exp(-jnp.abs(z)))) * (1.0 / GATE_TEMP)
    cums = {}
    for a in tiles:
        g_hi, g_lo = _split(logg[a])
        cums[a] = _dot(tri, g_hi) + _dot(tri, g_lo)
    qin, qoff, kin, kend, vb, tot, ptile = {}, {}, {}, {}, {}, {}, {}
    for a in tiles:
        cum = cums[a]
        q = pb_ref[rows(a), 0:W].astype(F32) * (HEAD_D ** -0.5)
        k = pb_ref[rows(a), W:2 * W].astype(F32)
        last_row = (lambda j: j * C) if rev else (lambda j: j * C + C - 1)
        off, run = {}, jnp.zeros((1, W), F32)
        for j in subs:
            tot[(a, j)] = cum[last_row(j):last_row(j) + 1]
            off[j] = run
            run = run + tot[(a, j)]
        ptile[a] = jnp.exp(run)
        e = jnp.exp(cum)
        qin[a] = (q * e).astype(BF16)
        qoff[a] = (q * (e * jnp.exp(stack({j: jnp.broadcast_to(off[j], (C, W)) for j in subs})))).astype(BF16)
        kin[a] = (k * jnp.exp(-cum)).astype(BF16)
        kend[a] = (k * jnp.exp(stack({j: jnp.broadcast_to(tot[(a, j)], (C, W)) for j in subs}) - cum)).astype(BF16)
        vb[a] = pb_ref[rows(a), 2 * W:3 * W]
    sc = {}
    for a in tiles:
        qst = jnp.concatenate([jnp.where(head_t == h, qin[a], jnp.zeros_like(qin[a])) for h in range(H)], axis=0)
        sc[a] = (_dot_nt(qst, kin[a]) * mask4).astype(BF16)
    ost = {a: _dot(sc[a], vb[a]) for a in tiles}
    kvt = {(a, j): jnp.where(bd, _dot_tn(vb[a][sub(j)], kend[a][sub(j)]), 0.0) for a in tiles for j in subs}
    cross, rend = {}, {}
    for a in tiles:
        r = None
        for j in subs:
            if r is not None:
                cross[(a, j)] = _dot_nt(qin[a][sub(j)], r.astype(BF16))
                r = r * jnp.exp(tot[(a, j)]) + kvt[(a, j)]
            else:
                cross[(a, j)] = jnp.zeros((C, W), F32)
                r = kvt[(a, j)]
        rend[a] = r
    st = st_scr[...]
    inter = {}
    for a in tiles:
        inter[a] = _dot_nt(qoff[a], st.astype(BF16))
        st = st * ptile[a] + rend[a]
    st_scr[...] = st

    for a in tiles:
        o = inter[a] + jnp.concatenate([cross[(a, j)] for j in range(nsub)], axis=0)
        for h in range(H):
            o = o + jnp.where(head_t == h, ost[a][h * T:(h + 1) * T], 0.0)
        if finish:
            o = o + oprev_ref[rows(a), :]
            ms = _dot_x2(o * o, ones_ref[...]) * (1.0 / HEAD_D)
            o = o * lax.rsqrt(ms + EPS) * gn_ref[...]
            r = pb_ref[rows(a), 3 * W:4 * W].astype(F32)
            o_ref[rows(a), :] = (o * (r * _sigmoid(r))).astype(o_ref.dtype)
        else:
            o_ref[rows(a), :] = o

    @pl.when(i == pl.num_programs(1) - 1)
    def _():
        sfin_ref[...] = st_scr[...]


def _gla_decay_matrices():
    T, C = GLA_TILE, GLA_CHUNK
    t = jnp.arange(T)[:, None]
    s = jnp.arange(T)[None, :]
    same = (t // C) == (s // C)
    return jnp.stack([same & (s <= t), same & (s >= t)]).astype(BF16)


def _gla_pass(l, d, pb3, pg3, wg, bg, s0, ones_bd, dec, oprev, gn, tt):
    B, n, _ = pb3.shape
    nt = n // tt
    rev = d == 1
    finish = oprev is not None
    W = GROUP_W

    def tok(b, i):
        return (b, (nt - 1 - i) if rev else i, 0)

    in_specs = [
        pl.BlockSpec((None, tt, 4 * W), tok),
        pl.BlockSpec((None, tt, LANES), tok),
        pl.BlockSpec((None, None, LANES, W), lambda b, i: (l, d, 0, 0)),
        pl.BlockSpec((None, None, 1, W), lambda b, i: (l, d, 0, 0)),
        pl.BlockSpec((None, W, W), lambda b, i: (b, 0, 0)),
        pl.BlockSpec((W, W), lambda b, i: (0, 0)),
        pl.BlockSpec((None, GLA_TILE, GLA_TILE), lambda b, i: (d, 0, 0)),
    ]
    args = [pb3, pg3, wg, bg, s0, ones_bd, dec]
    if finish:
        in_specs += [pl.BlockSpec((None, tt, W), tok), _vec_spec(l, W)]
        args += [oprev, gn]
    return pl.pallas_call(
        functools.partial(_gla_kernel, rev=rev, finish=finish),
        grid=(B, nt),
        in_specs=in_specs,
        out_specs=[pl.BlockSpec((None, tt, W), tok), pl.BlockSpec((None, W, W), lambda b, i: (b, 0, 0))],
        out_shape=[jax.ShapeDtypeStruct((B, n, W), BF16 if finish else F32),
                   jax.ShapeDtypeStruct((B, W, W), F32)],
        scratch_shapes=[pltpu.VMEM((W, W), F32)],
        compiler_params=_cparams("arbitrary", "arbitrary"),
        name="gla_bwd" if rev else "gla_fwd",
    )(*args)


def _gla(l, pb_c, pg_c, pb_x, pg_x, wg, bg, gn, ones_bd, tt_c, tt_x):
    B = pb_x.shape[0]
    zero = jnp.zeros((B, GROUP_W, GROUP_W), F32)
    dec = _gla_decay_matrices()
    ofc, sfc = _gla_pass(l, 0, pb_c, pg_c, wg, bg, zero, ones_bd, dec, None, None, tt_c)
    ofx, _ = _gla_pass(l, 0, pb_x, pg_x, wg, bg, sfc, ones_bd, dec, None, None, tt_x)
    ob_c, sbc = _gla_pass(l, 1, pb_c, pg_c, wg, bg, zero, ones_bd, dec, ofc, gn, tt_c)
    ob_x, _ = _gla_pass(l, 1, pb_x, pg_x, wg, bg, sbc, ones_bd, dec, ofx, gn, tt_x)
    return ob_c, ob_x


def _s5_matrices(a_re, a_im, log_dt, b_re, b_im, c_re, c_im, d_skip):
    T = S5_T
    G, P = a_re.shape[1:]
    I = b_re.shape[-1]
    lam = lax.complex(a_re.astype(F32), a_im.astype(F32))
    ldt = lam * jnp.exp(log_dt.astype(F32))[..., None]
    lam_bar = jnp.exp(ldt)
    b_bar = ((lam_bar - 1.0) / lam)[..., None] * lax.complex(b_re.astype(F32), b_im.astype(F32))
    cmat = lax.complex(c_re.astype(F32), c_im.astype(F32))
    steps = jnp.arange(T + 1, dtype=F32)
    pw = jnp.exp(ldt[..., None] * steps)
    taps = jnp.einsum('dgop,dgpk,dgpi->dgiko', cmat, pw[..., :T], b_bar).real
    taps = taps.at[0, :, :, 0, :].add(jnp.eye(I, dtype=F32)[None] * d_skip.astype(F32)[:, :, None])
    row = T * I
    zeros = jnp.zeros((G, I, row), F32)
    fwd = jnp.concatenate([zeros, taps[0].reshape(G, I, row)], axis=-1)
    bwd = jnp.concatenate([jnp.flip(taps[1], axis=2).reshape(G, I, row), zeros], axis=-1)
    m = jnp.stack([fwd[..., (T - s) * I:(T - s) * I + row] + bwd[..., (T - 1 - s) * I:(T - 1 - s) * I + row]
                   for s in range(T)], axis=1).reshape(G, row, row)

    ar = jnp.arange(T)
    pf = pw[0][..., T - 1 - ar]
    pb = pw[1][..., ar]
    bf = jnp.einsum('gps,gpi->gsip', pf, b_bar[0]).reshape(G, T * I, P)
    bb = jnp.einsum('gps,gpi->gsip', pb, b_bar[1]).reshape(G, T * I, P)
    bmat = jnp.concatenate([bf.real, bf.imag, bf.imag, bf.real, bb.real, bb.imag, bb.imag, bb.real], axis=-1)

    cf = jnp.einsum('gop,gpt->gpto', cmat[0], pw[0][..., 1 + ar]).reshape(G, P, T * I)
    cb = jnp.einsum('gop,gpt->gpto', cmat[1], pw[1][..., T - ar]).reshape(G, P, T * I)
    w = jnp.concatenate([m, cf.real, -cf.imag, cb.real, -cb.imag], axis=1)

    a = pw[..., T]
    acoef = jnp.stack([jnp.concatenate([a[0].real, a[0].real], axis=-1), jnp.concatenate([-a[0].imag, a[0].imag], axis=-1),
                       jnp.concatenate([a[1].real, a[1].real], axis=-1), jnp.concatenate([-a[1].imag, a[1].imag], axis=-1)])
    return bmat.astype(BF16), w.astype(BF16), acoef


def _s5_scan_kernel(xf_ref, xb_ref, bm_ref, a_ref, h0_ref, hf_ref, hb_ref, hfin_ref, st_scr, sf_scr, sb_scr, hf_scr, hb_scr):
    j = pl.program_id(1)
    G, tc, _ = xf_ref.shape
    pitch = tc + S5_PITCH_PAD
    half = hf_ref.shape[-1]

    @pl.when(j == 0)
    def _():
        st_scr[...] = h0_ref[...]

    for g in range(G):
        sf = _dot(xf_ref[g], bm_ref[g, :, 0:2 * half])
        sb = _dot(xb_ref[g], bm_ref[g, :, 2 * half:4 * half])
        for k in range(2):
            sf_scr[k, g * pitch:g * pitch + tc, :] = sf[:, k * half:(k + 1) * half]
            sb_scr[k, g * pitch:g * pitch + tc, :] = sb[:, k * half:(k + 1) * half]

    a1f, a2f, a1b, a2b = a_ref[0], a_ref[1], a_ref[2], a_ref[3]
    chunk = lambda i: pl.ds(i, G, stride=pitch)

    def body(i, hs):
        h1f, h2f, h1b, h2b = hs
        ib = tc - 1 - i
        hf_scr[chunk(i), :] = h1f
        hb_scr[chunk(ib), :] = h1b
        return (h1f * a1f + h2f * a2f + sf_scr[0, chunk(i), :], h2f * a1f - h1f * a2f + sf_scr[1, chunk(i), :],
                h1b * a1b + h2b * a2b + sb_scr[0, chunk(ib), :], h2b * a1b - h1b * a2b + sb_scr[1, chunk(ib), :])

    hs = lax.fori_loop(0, tc, body, (st_scr[0], st_scr[1], st_scr[2], st_scr[3]))
    for k in range(4):
        st_scr[k] = hs[k]
    for g in range(G):
        hf_ref[g] = hf_scr[g * pitch:g * pitch + tc, :]
        hb_ref[g] = hb_scr[g * pitch:g * pitch + tc, :]

    @pl.when(j == pl.num_programs(1) - 1)
    def _():
        hfin_ref[...] = st_scr[...]


def _s5_scan(l, xg, bmat, acoef, h0, B, tc):
    G, R, K = xg.shape
    nt = R // B // tc
    half = bmat.shape[-1] // 4
    fwd = lambda b, j: (0, b * nt + j, 0)
    bwd = lambda b, j: (0, b * nt + nt - 1 - j, 0)
    st_spec = pl.BlockSpec((None, 4, G, half), lambda b, j: (b, 0, 0, 0))
    pitch = tc + S5_PITCH_PAD
    return pl.pallas_call(
        _s5_scan_kernel,
        grid=(B, nt),
        in_specs=[pl.BlockSpec((G, tc, K), fwd),
                  pl.BlockSpec((G, tc, K), bwd),
                  pl.BlockSpec((None, G, K, 4 * half), lambda b, j: (l, 0, 0, 0)),
                  pl.BlockSpec((None, 4, G, half), lambda b, j: (l, 0, 0, 0)),
                  st_spec],
        out_specs=[pl.BlockSpec((G, tc, half), fwd), pl.BlockSpec((G, tc, half), bwd), st_spec],
        out_shape=[jax.ShapeDtypeStruct((G, R, half), F32), jax.ShapeDtypeStruct((G, R, half), F32),
                   jax.ShapeDtypeStruct((B, 4, G, half), F32)],
        scratch_shapes=[pltpu.VMEM((4, G, half), F32),
                        pltpu.VMEM((2, G * pitch, half), F32), pltpu.VMEM((2, G * pitch, half), F32),
                        pltpu.VMEM((G * pitch, half), F32), pltpu.VMEM((G * pitch, half), F32)],
        compiler_params=_cparams("arbitrary", "arbitrary"),
        name="s5_scan",
    )(xg, xg, bmat, acoef, h0)


def _s5_out_kernel(x_ref, hf_ref, hb_ref, w_ref, o_ref):
    k = x_ref.shape[-1]
    kh = hf_ref.shape[-1]
    y = _dot(x_ref[...], w_ref[0:k, :])
    y = y + _dot(hf_ref[...].astype(BF16), w_ref[k:k + kh, :])
    o_ref[...] = y + _dot(hb_ref[...].astype(BF16), w_ref[k + kh:, :])


def _s5_out(l, xg, hf, hb, w, tr):
    G, R, K = xg.shape
    kh = hf.shape[-1]
    row = lambda width: pl.BlockSpec((None, tr, width), lambda g, i: (g, i, 0))
    return pl.pallas_call(
        _s5_out_kernel,
        grid=(G, R // tr),
        in_specs=[row(K), row(kh), row(kh), pl.BlockSpec((None, None, K + 2 * kh, K), lambda g, i: (l, g, 0, 0))],
        out_specs=row(K),
        out_shape=jax.ShapeDtypeStruct((G, R, K), F32),
        compiler_params=_cparams("parallel", "parallel"),
        name="s5_out",
    )(xg, hf, hb, w)


def _s5_finish_kernel(y_ref, w_ref, b_ref, o_ref):
    y = _gelu(y_ref[...])
    gate = _sigmoid(_dot(y.astype(BF16), w_ref[...]) + b_ref[...])
    o_ref[...] = (y * gate).astype(o_ref.dtype)


def _s5_finish(l, y2, w_glu, b_glu, tm):
    R, W = y2.shape
    return pl.pallas_call(
        _s5_finish_kernel,
        grid=(R // tm,),
        in_specs=[pl.BlockSpec((tm, W), lambda i: (i, 0)),
                  pl.BlockSpec((None, W, W), lambda i: (l, 0, 0)),
                  _vec_spec(l, W)],
        out_specs=pl.BlockSpec((tm, W), lambda i: (i, 0)),
        out_shape=jax.ShapeDtypeStruct((R, W), BF16),
        compiler_params=_cparams("parallel"),
        name="s5_finish",
    )(y2, w_glu, b_glu)


def _to_groups(u3):
    B, n, W = u3.shape
    G = W // S5_IN
    x = u3.astype(BF16).reshape(B, n // S5_T, S5_T, G, S5_IN).transpose(3, 0, 1, 2, 4)
    return x.reshape(G, B * (n // S5_T), S5_T * S5_IN)


def _from_groups(y, B):
    G, R, _ = y.shape
    nc = R // B
    y = y.reshape(G, B, nc, S5_T, S5_IN).transpose(1, 2, 3, 0, 4)
    return y.reshape(B * nc * S5_T, G * S5_IN)


def _s5_gather_kernel(x_ref, o_ref, xs_scr, xt_scr):
    G, tc, _ = o_ref.shape
    nh = x_ref.shape[1] // LANES
    gh = G // nh
    for h in range(nh):
        xs_scr[h] = x_ref[:, h * LANES:(h + 1) * LANES].astype(F32)
    for t in range(S5_T):
        for h in range(nh):
            rt = xs_scr[h, pl.ds(t, tc, stride=S5_T), :].T
            for g in range(gh):
                xt_scr[h * gh + g, t * S5_IN:(t + 1) * S5_IN, :] = rt[g * S5_IN:(g + 1) * S5_IN, :]
    for g in range(G):
        o_ref[g] = xt_scr[g].T.astype(o_ref.dtype)


def _s5_gather(x2, tc):
    N, W = x2.shape
    G = W // S5_IN
    R = N // S5_T
    return pl.pallas_call(
        _s5_gather_kernel,
        grid=(R // tc,),
        in_specs=[pl.BlockSpec((tc * S5_T, W), lambda i: (i, 0))],
        out_specs=pl.BlockSpec((G, tc, S5_T * S5_IN), lambda i: (0, i, 0)),
        out_shape=jax.ShapeDtypeStruct((G, R, S5_T * S5_IN), BF16),
        scratch_shapes=[pltpu.VMEM((W // LANES, tc * S5_T, LANES), F32), pltpu.VMEM((G, S5_T * S5_IN, tc), F32)],
        compiler_params=_cparams("parallel"),
        name="s5_gather",
    )(x2)


def _s5_scatter_kernel(x_ref, hf_ref, hb_ref, wy_ref, w_ref, b_ref, o_ref, yt_scr, tok_scr):
    G, tc, k = x_ref.shape
    kh = hf_ref.shape[-1]
    nh = tok_scr.shape[0]
    gh = G // nh
    for g in range(G):
        y = _dot(x_ref[g], wy_ref[g, 0:k, :])
        y = y + _dot(hf_ref[g].astype(BF16), wy_ref[g, k:k + kh, :])
        y = y + _dot(hb_ref[g].astype(BF16), wy_ref[g, k + kh:, :])
        yt = y.T
        for t in range(S5_T):
            yt_scr[t, g // gh, (g % gh) * S5_IN:(g % gh + 1) * S5_IN, :] = yt[t * S5_IN:(t + 1) * S5_IN, :]
    for t in range(S5_T):
        for h in range(nh):
            tok_scr[h, pl.ds(t, tc, stride=S5_T), :] = yt_scr[t, h].T
    y = _gelu(jnp.concatenate([tok_scr[h] for h in range(nh)], axis=1))
    gate = _sigmoid(_dot(y.astype(BF16), w_ref[...]) + b_ref[...])
    o_ref[...] = (y * gate).astype(o_ref.dtype)


def _s5_scatter_finish(l, xg, hf, hb, wy, w_glu, b_glu, tc):
    G, R, K = xg.shape
    kh = hf.shape[-1]
    W = G * S5_IN
    grp = lambda width: pl.BlockSpec((G, tc, width), lambda i: (0, i, 0))
    return pl.pallas_call(
        _s5_scatter_kernel,
        grid=(R // tc,),
        in_specs=[grp(K), grp(kh), grp(kh),
                  pl.BlockSpec((None, G, K + 2 * kh, K), lambda i: (l, 0, 0, 0)),
                  pl.BlockSpec((None, W, W), lambda i: (l, 0, 0)),
                  _vec_spec(l, W)],
        out_specs=pl.BlockSpec((tc * S5_T, W), lambda i: (i, 0)),
        out_shape=jax.ShapeDtypeStruct((R * S5_T, W), BF16),
        scratch_shapes=[pltpu.VMEM((S5_T, W // LANES, LANES, tc), F32), pltpu.VMEM((W // LANES, tc * S5_T, LANES), F32)],
        compiler_params=_cparams("parallel"),
        name="s5_scatter",
    )(xg, hf, hb, wy, w_glu, b_glu)


def _s5(l, pc_c, pc_x, mats, w_glu, b_glu, need_ctx, tm_c):
    bmat, w, acoef = mats
    B, n, W = pc_x.shape
    G = W // S5_IN
    xg_c = _to_groups(pc_c)
    tc_x = min(S5_TILE, n // S5_T)
    xg_x = _s5_gather(pc_x.reshape(B * n, W), tc_x)
    tr_c = min(S5_ROW_TILE, xg_c.shape[1])
    h0 = jnp.zeros((B, 4, G, acoef.shape[-1]), F32)
    hf_c, hb_c, h1 = _s5_scan(l, xg_c, bmat, acoef, h0, B, xg_c.shape[1] // B)
    hf_x, hb_x, _ = _s5_scan(l, xg_x, bmat, acoef, h1, B, tc_x)
    oc_x = _s5_scatter_finish(l, xg_x, hf_x, hb_x, w, w_glu, b_glu, tc_x)
    oc_c = None
    if need_ctx:
        y_c = _from_groups(_s5_out(l, xg_c, hf_c, hb_c, w, tr_c), B)
        oc_c = _s5_finish(l, y_c, w_glu, b_glu, tm_c)
    return oc_c, oc_x


def _mla_prep_kernel(*refs, rope):
    refs = refs[:-5] + refs[-3:]
    if rope:
        pd_ref, qn_ref, kvn_ref, wq_ref, wqs_ref, wk_ref, wks_ref, wv_ref, cos_ref, sin_ref, q_ref, k_ref, v_ref = refs
    else:
        pd_ref, qn_ref, kvn_ref, wq_ref, wk_ref, wv_ref, q_ref, k_ref, v_ref = refs
    cq = pd_ref[:, 0:MLA_Q_LANES].astype(F32)
    ms = jnp.sum(cq * cq, axis=-1, keepdims=True) * (1.0 / MLA_Q_RANK)
    cqn = (cq * lax.rsqrt(ms + EPS) * qn_ref[...]).astype(BF16)
    ck = pd_ref[:, MLA_Q_LANES:MLA_Q_LANES + MLA_KV_LANES].astype(F32)
    lane = lax.broadcasted_iota(jnp.int32, ck.shape, 1)
    is_lat = lane < MLA_KV_RANK
    ms = jnp.sum(jnp.where(is_lat, ck * ck, 0.0), axis=-1, keepdims=True) * (1.0 / MLA_KV_RANK)
    ckn = jnp.where(is_lat, ck * lax.rsqrt(ms + EPS) * kvn_ref[...], ck).astype(BF16)
    q = _dot(cqn, wq_ref[...])
    k = _dot(ckn, wk_ref[...])
    if rope:
        cos = jnp.concatenate([cos_ref[...]] * MLA_HEADS, axis=1)
        sin = jnp.concatenate([sin_ref[...]] * MLA_HEADS, axis=1)
        q = q * cos + _dot(cqn, wqs_ref[...]) * sin
        k = k * cos + _dot(ckn, wks_ref[...]) * sin
    q_ref[...] = (q * ((MLA_NOPE + MLA_ROPE) ** -0.5 * LOG2E)).astype(BF16)
    k_ref[...] = k.astype(BF16)
    v_ref[...] = _dot(ckn, wv_ref[...]).astype(BF16)


def _mla_prep(l, pd, qn, kvn, wts, tables, tm, n, B, row0, nk, kv_into):
    R = pd.shape[0]
    HP = MLA_HEADS * MLA_HEAD_PAD
    wq, wqs, wk, wks, wv = wts
    rope = tables is not None
    npt = n // tm
    wspec = lambda r: pl.BlockSpec((None, r, HP), lambda i: (l, 0, 0))
    QL, KL = MLA_Q_LANES, MLA_KV_LANES
    in_specs = [pl.BlockSpec((tm, QL + KL), lambda i: (i, 0)), _vec_spec(l, QL), _vec_spec(l, KL)]
    if rope:
        tspec = pl.BlockSpec((tm, MLA_HEAD_PAD), lambda i: (i % npt, 0))
        in_specs += [wspec(QL), wspec(QL), wspec(KL), wspec(KL), wspec(KL), tspec, tspec]
        args = (pd, qn, kvn, wq, wqs, wk, wks, wv) + tuple(tables)
    else:
        in_specs += [wspec(QL), wspec(KL), wspec(KL)]
        args = (pd, qn, kvn, wq, wk, wv)
    aliases = {len(args): 1, len(args) + 1: 2}
    in_specs += [pl.BlockSpec(memory_space=pl.ANY)] * 2
    args = args + tuple(kv_into)
    kv_spec = pl.BlockSpec((None, tm, HP), lambda i: (i // npt, row0 // tm + i % npt, 0))
    kv_shape = jax.ShapeDtypeStruct((B, nk, HP), BF16)
    return pl.pallas_call(
        functools.partial(_mla_prep_kernel, rope=rope),
        grid=(R // tm,),
        in_specs=in_specs,
        out_specs=[pl.BlockSpec((tm, HP), lambda i: (i, 0)), kv_spec, kv_spec],
        out_shape=[jax.ShapeDtypeStruct((R, HP), BF16), kv_shape, kv_shape],
        input_output_aliases=aliases,
        compiler_params=_cparams("parallel"),
        name="mla_prep",
    )(*args)


def _attn_kernel(q_ref, k_ref, v_ref, o_ref, m_scr, l_scr, acc_scr):
    kv = pl.program_id(2)
    HP = MLA_HEAD_PAD
    tk = k_ref.shape[0]

    @pl.when(kv == 0)
    def _():
        m_scr[...] = jnp.full(m_scr.shape, -jnp.inf, F32)
        l_scr[...] = jnp.zeros(l_scr.shape, F32)
        acc_scr[...] = jnp.zeros(acc_scr.shape, F32)

    def scores(h):
        lanes = slice(h * HP, (h + 1) * HP)
        return _dot_nt(q_ref[:, lanes], k_ref[:, lanes])

    s_next = scores(0)
    for h in range(MLA_HEADS):
        lanes = slice(h * HP, (h + 1) * HP)
        s = s_next
        if h + 1 < MLA_HEADS:
            s_next = scores(h + 1)
        m_prev = m_scr[h]
        m_new = jnp.maximum(m_prev, jnp.max(s, axis=1, keepdims=True))
        alpha = jnp.exp2(m_prev - m_new)
        p = jnp.exp2(s - m_new[:, 0:1])
        lp = p[:, 0:LANES]
        for c in range(1, tk // LANES):
            lp = lp + p[:, c * LANES:(c + 1) * LANES]
        l_scr[h] = alpha * l_scr[h] + lp
        acc_scr[h] = alpha * acc_scr[h] + _dot(p.astype(BF16), v_ref[:, lanes])
        m_scr[h] = m_new

    @pl.when(kv == pl.num_programs(2) - 1)
    def _():
        low = lax.broadcasted_iota(jnp.int32, acc_scr.shape[1:], 1) < MLA_V
        norm = lambda h: acc_scr[h] * (1.0 / jnp.sum(l_scr[h], axis=1, keepdims=True))
        outs = [jnp.where(low, norm(h), norm(h + 1)) for h in range(0, MLA_HEADS, 2)]
        o_ref[...] = jnp.concatenate(outs, axis=1).astype(o_ref.dtype)


def _attention(q3, k3, v3, tq, tk, k0, nk):
    B, nq, HP = q3.shape
    kb = k0 // tk
    return pl.pallas_call(
        _attn_kernel,
        grid=(B, nq // tq, nk // tk),
        in_specs=[pl.BlockSpec((None, tq, HP), lambda b, i, j: (b, i, 0)),
                  pl.BlockSpec((None, tk, HP), lambda b, i, j: (b, kb + j, 0)),
                  pl.BlockSpec((None, tk, HP), lambda b, i, j: (b, kb + j, 0))],
        out_specs=pl.BlockSpec((None, tq, MLA_HEADS * MLA_V), lambda b, i, j: (b, i, 0)),
        out_shape=jax.ShapeDtypeStruct((B, nq, MLA_HEADS * MLA_V), BF16),
        scratch_shapes=[pltpu.VMEM((MLA_HEADS, tq, LANES), F32),
                        pltpu.VMEM((MLA_HEADS, tq, LANES), F32),
                        pltpu.VMEM((MLA_HEADS, tq, MLA_HEAD_PAD), F32)],
        compiler_params=_cparams("parallel", "parallel", "arbitrary"),
        name="attention",
    )(q3, k3, v3)


def _outproj_kernel(oa, ob, oc, od, w_ref, x_ref, gpost_ref, gt_ref, gpre_ref, sc_ref, sh_ref, xo_ref, h_ref):
    W = GROUP_W
    mix = _dot(oa[...], w_ref[0:W, :])
    mix = mix + _dot(ob[...], w_ref[W:2 * W, :])
    mix = mix + _dot(oc[...], w_ref[2 * W:3 * W, :])
    mix = mix + _dot(od[...], w_ref[3 * W:4 * W, :])
    x = x_ref[...] + gt_ref[...] * (_rms(mix) * gpost_ref[...])
    xo_ref[...] = x
    h_ref[...] = (_rms(x) * gpre_ref[...] * (1.0 + sc_ref[...]) + sh_ref[...]).astype(h_ref.dtype)


def _outproj(l, parts, w_out, x2, mod4, brow, g_post, g_pre_ffn, tm):
    R, D = x2.shape
    W = GROUP_W
    part_spec = pl.BlockSpec((tm, W), lambda i: (i, 0))
    row_spec = pl.BlockSpec((tm, D), lambda i: (i, 0))
    return pl.pallas_call(
        _outproj_kernel,
        grid=(R // tm,),
        in_specs=[part_spec] * 4 + [
            pl.BlockSpec((None, 4 * W, D), lambda i: (l, 0, 0)),
            row_spec,
            _vec_spec(l, D),
            _mod_spec(l, 2, D, brow),
            _vec_spec(l, D),
            _mod_spec(l, 4, D, brow),
            _mod_spec(l, 3, D, brow),
        ],
        out_specs=[row_spec, row_spec],
        out_shape=[jax.ShapeDtypeStruct((R, D), F32), jax.ShapeDtypeStruct((R, D), BF16)],
        compiler_params=_cparams("parallel"),
        name="outproj",
    )(*parts, w_out, x2, g_post, mod4, g_pre_ffn, mod4, mod4)


FFN_HALO = 16
FFN_COLS = 256


def _ffn_kernel(hp_ref, h_ref, hn_ref, wup_ref, cw_ref, cb_ref, wdn_ref, x_ref, gpost_ref, gt_ref, o_ref, y_scr):
    i = pl.program_id(1)
    tm = h_ref.shape[0]
    dff = wdn_ref.shape[0]
    use = FFN_HALO // 2
    rows = tm + 2 * use
    prev = jnp.where(i == 0, jnp.zeros_like(hp_ref[...]), hp_ref[...])[FFN_HALO - use:]
    nxt = jnp.where(i == pl.num_programs(1) - 1, jnp.zeros_like(hn_ref[...]), hn_ref[...])[:use]
    hb = jnp.concatenate([prev, h_ref[...], nxt], axis=0)

    def conv(z, cols):
        w = cw_ref[:, cols]
        out = cb_ref[:, cols] + w[0:1] * pltpu.roll(z, 1, axis=0) + w[1:2] * z + w[2:3] * pltpu.roll(z, rows - 1, axis=0)
        return out[use:use + tm]

    def cols(j):
        return slice(j * FFN_COLS, (j + 1) * FFN_COLS), slice(dff + j * FFN_COLS, dff + (j + 1) * FFN_COLS)

    def up(j):
        ca, cg = cols(j)
        return _dot(hb, wup_ref[:, ca]), _dot(hb, wup_ref[:, cg])

    nchunks = dff // FFN_COLS
    z_next = up(0)
    for j in range(nchunks):
        ca, cg = cols(j)
        za, zg = z_next
        if j + 1 < nchunks:
            z_next = up(j + 1)
        y_scr[:, ca] = _gelu_gate(conv(za, ca), conv(zg, cg)).astype(BF16)
    acc = _dot(y_scr[...], wdn_ref[...])
    o_ref[...] = x_ref[...] + gt_ref[...] * (_rms(acc) * gpost_ref[...])


def _ffn(l, h3, x3, w_up, conv_w, conv_b, w_down, mod4, brow, g_post, tm):
    B, n, D = x3.shape
    nt = n // tm
    hb = tm // FFN_HALO
    nh = n // FFN_HALO
    dff = w_down.shape[1]
    once = pl.Buffered(1)
    return pl.pallas_call(
        _ffn_kernel,
        grid=(B, nt),
        in_specs=[
            pl.BlockSpec((None, FFN_HALO, D), lambda b, i: (b, jnp.maximum(i * hb - 1, 0), 0)),
            pl.BlockSpec((None, tm, D), lambda b, i: (b, i, 0)),
            pl.BlockSpec((None, FFN_HALO, D), lambda b, i: (b, jnp.minimum((i + 1) * hb, nh - 1), 0)),
            pl.BlockSpec((None, D, 2 * dff), lambda b, i: (l, 0, 0), pipeline_mode=once),
            pl.BlockSpec((None, 3, 2 * dff), lambda b, i: (l, 0, 0)),
            pl.BlockSpec((None, 1, 2 * dff), lambda b, i: (l, 0, 0)),
            pl.BlockSpec((None, dff, D), lambda b, i: (l, 0, 0), pipeline_mode=once),
            pl.BlockSpec((None, tm, D), lambda b, i: (b, i, 0)),
            _vec_spec(l, D),
            _mod_spec(l, 5, D, brow),
        ],
        out_specs=pl.BlockSpec((None, tm, D), lambda b, i: (b, i, 0)),
        out_shape=jax.ShapeDtypeStruct((B, n, D), F32),
        scratch_shapes=[pltpu.VMEM((tm, dff), BF16)],
        compiler_params=_cparams("parallel", "parallel"),
        name="conv_ffn",
    )(h3, h3, h3, w_up, conv_w, conv_b, w_down, x3, g_post, mod4)


def _prep_w_in(w_in):
    w_in = w_in.astype(BF16)
    z = lambda n: jnp.zeros(w_in.shape[:-1] + (n,), w_in.dtype)
    a = w_in[..., 0:512]
    b = w_in[..., 512:1536]
    gl = w_in[..., 1536:1568]
    c = w_in[..., 1568:1824]
    cq = w_in[..., 1824:2048]
    ckv_kr = w_in[..., 2048:2176]
    return jnp.concatenate([a, b, c, cq, z(32), ckv_kr, gl, z(96)], axis=-1)


def _rope_swap(t):
    q = MLA_ROPE // 4
    return jnp.concatenate([t[..., q:2 * q], t[..., 0:q], t[..., 3 * q:4 * q], t[..., 2 * q:3 * q]], axis=-1)


def _prep_mla(w_uq, w_ukv):
    L = w_uq.shape[0]
    H, NP, RP, HP = MLA_HEADS, MLA_NOPE, MLA_ROPE, MLA_HEAD_PAD
    wq = w_uq.reshape(L, MLA_Q_RANK, H, NP + RP)
    zq = jnp.zeros((L, MLA_Q_RANK, H, HP - NP - RP), w_uq.dtype)
    znope = jnp.zeros((L, MLA_Q_RANK, H, NP), w_uq.dtype)
    q_main = jnp.concatenate([wq, zq], axis=-1)
    q_swap = jnp.concatenate([znope, _rope_swap(wq[..., NP:]), zq], axis=-1)
    padq = lambda w: jnp.pad(w.reshape(L, MLA_Q_RANK, H * HP), ((0, 0), (0, MLA_Q_LANES - MLA_Q_RANK), (0, 0)))

    wkv = w_ukv.reshape(L, MLA_KV_RANK, H, NP + MLA_V)
    zk = jnp.zeros((L, MLA_KV_RANK, H, HP - NP), w_ukv.dtype)
    k_lat = jnp.concatenate([wkv[..., :NP], zk], axis=-1)
    eye = jnp.eye(RP, dtype=w_ukv.dtype)
    place = lambda e: jnp.broadcast_to(
        jnp.concatenate([jnp.zeros((RP, NP), e.dtype), e, jnp.zeros((RP, HP - NP - RP), e.dtype)], axis=-1)[None, :, None, :],
        (L, RP, H, HP))
    k_main = jnp.concatenate([k_lat, place(eye)], axis=1)
    k_swap = jnp.concatenate([jnp.zeros_like(k_lat), place(_rope_swap(eye))], axis=1)
    zv = jnp.zeros((L, MLA_KV_RANK, H, HP - MLA_V), w_ukv.dtype)
    odd = (jnp.arange(H) % 2 == 1)[None, None, :, None]
    v_lat = jnp.where(odd, jnp.concatenate([zv, wkv[..., NP:]], axis=-1), jnp.concatenate([wkv[..., NP:], zv], axis=-1))
    v_main = jnp.concatenate([v_lat, jnp.zeros((L, RP, H, HP), w_ukv.dtype)], axis=1)
    flat = lambda w: w.reshape(L, w.shape[1], H * HP).astype(BF16)
    return (padq(q_main).astype(BF16), padq(q_swap).astype(BF16), flat(k_main), flat(k_swap), flat(v_main))


def _rope_tables(n):
    rows = n // GRID_W
    nf = MLA_ROPE // 4
    inv = ROPE_BASE ** (-jnp.arange(nf, dtype=F32) / nf)
    ar = jnp.arange(rows, dtype=F32)[:, None] * inv[None, :]
    ac = jnp.arange(GRID_W, dtype=F32)[:, None] * inv[None, :]
    by_row = lambda t: jnp.repeat(t, GRID_W, axis=0)
    by_col = lambda t: jnp.tile(t, (rows, 1))
    cr, sr, cc, sn = by_row(jnp.cos(ar)), by_row(jnp.sin(ar)), by_col(jnp.cos(ac)), by_col(jnp.sin(ac))
    one = jnp.ones((n, MLA_NOPE), F32)
    zero = jnp.zeros((n, MLA_HEAD_PAD - MLA_NOPE - MLA_ROPE), F32)
    cos = jnp.concatenate([one, cr, cr, cc, cc, zero], axis=1)
    sin = jnp.concatenate([0.0 * one, -sr, sr, -sn, sn, zero], axis=1)
    return cos, sin


def _pick_tile(n, want):
    t = min(n, want)
    while n % t:
        t //= 2
    return t


def kernel(x, c, ctx, c_ctx, w_mod, b_mod, g_pre_mix, g_post_mix, g_pre_ffn, g_post_ffn, w_in,
           sgu_norm, sgu_w, sgu_b, gla_w_gate, gla_b_gate, gla_norm,
           s5_a_re, s5_a_im, s5_log_dt, s5_b_re, s5_b_im, s5_c_re, s5_c_im, s5_d, s5_w_glu, s5_b_glu,
           mla_q_norm, mla_w_uq, mla_kv_norm, mla_w_ukv, w_out,
           ffn_w_up, ffn_conv_w, ffn_conv_b, ffn_w_down):
    B, n, D = x.shape
    nctx = ctx.shape[1]
    L = w_mod.shape[0]
    W = GROUP_W
    assert B < 8 and n % TOKEN_TILE == 0 and nctx % GLA_TILE == 0 and n % GRID_W == 0

    c8 = jnp.concatenate([c, c_ctx[None, :], jnp.zeros((8 - B - 1, D), F32)], axis=0)
    mod4 = _modulation(c8, w_mod, b_mod).reshape(L, 8, 1, 6 * D)
    vec = lambda p: p.reshape(L, 1, -1).astype(F32)
    g_pre_mix, g_post_mix, g_pre_ffn, g_post_ffn = map(vec, (g_pre_mix, g_post_mix, g_pre_ffn, g_post_ffn))
    w_in_p = _prep_w_in(w_in)
    sgu_gn = vec(sgu_norm)
    sgu_w_st = sgu_w.reshape(L, -1, MLP_CHUNK).astype(BF16)
    sgu_bias = jnp.repeat(jnp.swapaxes(sgu_b, 1, 2), HEAD_D, axis=2).astype(F32)
    ones_bd = jnp.kron(jnp.eye(W // HEAD_D, dtype=F32), jnp.ones((HEAD_D, HEAD_D), F32)).astype(BF16)
    gla_wg = jnp.zeros((L, 2, LANES, W), F32)
    gla_wg = gla_wg.at[:, 0, 0:GATE_RANK].set(gla_w_gate[:, 0]).at[:, 1, GATE_RANK:2 * GATE_RANK].set(gla_w_gate[:, 1])
    gla_bg = gla_b_gate.reshape(L, 2, 1, W).astype(F32)
    gla_gn = vec(gla_norm)
    s5_wglu = s5_w_glu.astype(BF16)
    s5_bglu = vec(s5_b_glu)
    mla_qn = jnp.pad(mla_q_norm, ((0, 0), (0, MLA_Q_LANES - MLA_Q_RANK))).reshape(L, 1, MLA_Q_LANES).astype(F32)
    mla_kvn = jnp.pad(mla_kv_norm, ((0, 0), (0, MLA_KV_LANES - MLA_KV_RANK))).reshape(L, 1, MLA_KV_LANES).astype(F32)
    mla_wts = _prep_mla(mla_w_uq, mla_w_ukv)
    tables = _rope_tables(n)
    w_out_b = w_out.astype(BF16)
    w_up_b = ffn_w_up.astype(BF16)
    w_down_b = ffn_w_down.astype(BF16)
    dff = ffn_w_down.shape[1]
    half_gate = jnp.concatenate([jnp.ones((dff,), F32), jnp.full((dff,), 0.5, F32)])
    conv_w = ffn_conv_w.astype(F32) * half_gate
    conv_b = (ffn_conv_b.astype(F32) * half_gate).reshape(L, 1, -1)

    s5_mats = jax.vmap(_s5_matrices)(s5_a_re, s5_a_im, s5_log_dt, s5_b_re, s5_b_im, s5_c_re, s5_c_im, s5_d)

    tm_x = _pick_tile(n, TOKEN_TILE)
    tm_c = _pick_tile(nctx, TOKEN_TILE)
    tpb_x = n // tm_x
    brow_x = lambda i: i // tpb_x
    brow_c = lambda *g: B
    brow_x2 = lambda b, i: b

    xs = x.reshape(B * n, D)
    cs = ctx.reshape(B * nctx, D)

    for l in range(L):
        need_ctx = l < L - 1
        sgu = (sgu_gn, sgu_w_st, sgu_bias, ones_bd)
        oa_x, pb_x, pc_x, pd_x, pg_x = _inproj(l, xs, mod4, brow_x, g_pre_mix, w_in_p, sgu, tm_x)
        oa_c, pb_c, pc_c, pd_c, pg_c = _inproj(l, cs, mod4, brow_c, g_pre_mix, w_in_p, sgu, tm_c)

        r3 = lambda t, m: t.reshape(B, m, t.shape[-1])
        ob_c, ob_x = _gla(l, r3(pb_c, nctx), r3(pg_c, nctx), r3(pb_x, n), r3(pg_x, n),
                          gla_wg, gla_bg, gla_gn, ones_bd, _pick_tile(nctx, GLA_TILE), _pick_tile(n, 4 * GLA_TILE))
        oc_c, oc_x = _s5(l, r3(pc_c, nctx), r3(pc_x, n), s5_mats, s5_wglu, s5_bglu, need_ctx, tm_c)
        wq, wqs, wk, wks, wv = mla_wts
        nk = n + nctx
        kv0 = jnp.zeros((B, nk, MLA_HEADS * MLA_HEAD_PAD), BF16)
        q_x, k_all, v_all = _mla_prep(l, pd_x, mla_qn, mla_kvn, mla_wts, tables, tm_x, n, B, 0, nk, (kv0, kv0))
        q_c, k_all, v_all = _mla_prep(l, pd_c, mla_qn, mla_kvn, (wq, None, wk, None, wv), None, tm_c, nctx, B, n, nk,
                                      (k_all, v_all))
        tk = next(t for t in ATTN_KV_TILES if nk % t == 0)
        od_x = _attention(r3(q_x, n), k_all, v_all, tm_x, tk, 0, nk).reshape(B * n, W)

        xs, hx = _outproj(l, (oa_x, ob_x.reshape(B * n, W), oc_x, od_x), w_out_b, xs, mod4, brow_x,
                          g_post_mix, g_pre_ffn, tm_x)
        xs = _ffn(l, hx.reshape(B, n, D), xs.reshape(B, n, D), w_up_b, conv_w, conv_b, w_down_b, mod4, brow_x2,
                  g_post_ffn, tm_x).reshape(B * n, D)

        if need_ctx:
            od_c = _attention(r3(q_c, nctx), k_all, v_all, tm_c, nctx, n, nctx).reshape(B * nctx, W)
            cs, hc = _outproj(l, (oa_c, ob_c.reshape(B * nctx, W), oc_c, od_c), w_out_b, cs, mod4, brow_c,
                              g_post_mix, g_pre_ffn, tm_c)
            cs = _ffn(l, hc.reshape(B, nctx, D), cs.reshape(B, nctx, D), w_up_b, conv_w, conv_b, w_down_b, mod4,
                      brow_c, g_post_ffn, tm_c).reshape(B * nctx, D)
    return xs.reshape(B, n, D)
```

```python
import functools

import jax
import jax.numpy as jnp
from jax import lax
from jax.experimental import pallas as pl
from jax.experimental.pallas import tpu as pltpu

F32 = jnp.float32
BF16 = jnp.bfloat16

EPS = 1e-6
GRID_W = 64
GROUP_W = 256
HEAD_D = 64
MLP_CHUNK = 128
GATE_RANK = 16
GATE_TEMP = 16.0
GLA_CHUNK = 64
GLA_TILE = 256
S5_IN = 16
S5_T = 16
S5_TILE = 128
S5_PITCH_PAD = 8
MLA_HEADS = 4
MLA_NOPE = 64
MLA_ROPE = 32
MLA_V = 64
MLA_Q_RANK = 224
MLA_KV_RANK = 96
MLA_HEAD_PAD = 128
ROPE_BASE = 10000.0
LOG2E = 1.4426950408889634

LANES = 128
VMEM_LIMIT = 48 * 1024 * 1024
TOKEN_TILE = 512
MOD_COL_TILE = 1536
ATTN_KV_TILES = (2816, 1408, 768, 512, 256, 128)
MLA_Q_LANES = 256
MLA_KV_LANES = 128


def _cparams(*sem):
    return pltpu.CompilerParams(dimension_semantics=sem, vmem_limit_bytes=VMEM_LIMIT)


def _dot(a, b):
    return jnp.dot(a, b, preferred_element_type=F32)


def _dot_nt(a, b):
    return lax.dot_general(a, b, (((1,), (1,)), ((), ())), preferred_element_type=F32)


def _dot_tn(a, b):
    return lax.dot_general(a, b, (((0,), (0,)), ((), ())), preferred_element_type=F32)


def _split(a):
    hi = a.astype(BF16)
    lo = (a - hi.astype(F32)).astype(BF16)
    return hi, lo


def _dot_x2(a, b_bf16):
    hi, lo = _split(a)
    return _dot(hi, b_bf16) + _dot(lo, b_bf16)


def _dot_x3(a, b):
    ah, al = _split(a)
    bh, bl = _split(b)
    return _dot(ah, bh) + _dot(al, bh) + _dot(ah, bl)


def _rms(x):
    return x * lax.rsqrt(jnp.mean(x * x, axis=-1, keepdims=True) + EPS)


def _gelu(x):
    return 0.5 * x * (1.0 + jnp.tanh(0.7978845608028654 * (x + 0.044715 * (x * x * x))))


def _gelu_gate(a, half_g):
    u = a * (0.7978845608028654 + 0.035677408136300125 * (a * a))
    return (a * half_g) * (1.0 + jnp.tanh(u))


def _sigmoid(x):
    return 1.0 / (1.0 + jnp.exp(-x))


def _lane_group(shape, width):
    return lax.broadcasted_iota(jnp.int32, shape, len(shape) - 1) // width


def _mod_kernel(c_ref, w_ref, b_ref, o_ref):
    c = c_ref[...]
    s = c * _sigmoid(c)
    o_ref[...] = _dot_x3(s, w_ref[...]) + b_ref[...]


def _modulation(c8, w_mod, b_mod):
    L, D, W = w_mod.shape
    tn = MOD_COL_TILE
    return pl.pallas_call(
        _mod_kernel,
        grid=(L, W // tn),
        in_specs=[
            pl.BlockSpec((8, D), lambda l, j: (0, 0)),
            pl.BlockSpec((None, D, tn), lambda l, j: (l, 0, j)),
            pl.BlockSpec((None, 1, tn), lambda l, j: (l, 0, j)),
        ],
        out_specs=pl.BlockSpec((None, 8, tn), lambda l, j: (l, 0, j)),
        out_shape=jax.ShapeDtypeStruct((L, 8, W), F32),
        compiler_params=_cparams("arbitrary", "arbitrary"),
        name="modulation",
    )(c8, w_mod, b_mod.reshape(L, 1, W))


def _mod_spec(l, j, D, bfn):
    return pl.BlockSpec((None, None, 1, D), lambda *g: (l, bfn(*g), 0, j))


def _vec_spec(l, width):
    return pl.BlockSpec((None, 1, width), lambda *g: (l, 0, 0))


IN_SLABS = (("a", 0, 2 * GROUP_W), ("b", 512, 4 * GROUP_W), ("c", 1536, GROUP_W), ("d", 1792, MLA_Q_LANES + MLA_KV_LANES),
            ("g", 2176, LANES))
IN_PAD_COLS = 2304


def _sgu_tile(p, gn_ref, w_ref, b_ref, ones_ref, o_ref):
    tm = p.shape[0]
    g = _gelu(p)
    u = g[:, :GROUP_W]
    v = g[:, GROUP_W:]
    ms = _dot_x2(v * v, ones_ref[...]) * (1.0 / HEAD_D)
    vb = (v * lax.rsqrt(ms + EPS) * gn_ref[...]).astype(BF16)
    head = _lane_group((MLP_CHUNK, GROUP_W), HEAD_D)
    w = w_ref[...]
    for c in range(tm // MLP_CHUNK):
        rows = slice(c * MLP_CHUNK, (c + 1) * MLP_CHUNK)
        r = _dot(w, vb[rows])
        s = b_ref[...]
        for h in range(GROUP_W // HEAD_D):
            s = s + jnp.where(head == h, r[h * MLP_CHUNK:(h + 1) * MLP_CHUNK], 0.0)
        o_ref[rows, :] = (u[rows] * s).astype(o_ref.dtype)


def _inproj_kernel(x_ref, g_ref, sc_ref, sh_ref, w_ref, gn_ref, wsp_ref, bsp_ref, ones_ref, oa, ob, oc, od, og):
    h = _rms(x_ref[...]) * g_ref[...] * (1.0 + sc_ref[...]) + sh_ref[...]
    hb = h.astype(BF16)
    (_, off_a, width_a) = IN_SLABS[0]
    pa = _dot(hb, w_ref[:, off_a:off_a + width_a])
    for (_, off, width), o_ref in zip(IN_SLABS[1:], (ob, oc, od, og)):
        o_ref[...] = _dot(hb, w_ref[:, off:off + width]).astype(o_ref.dtype)
    _sgu_tile(pa, gn_ref, wsp_ref, bsp_ref, ones_ref, oa)


def _inproj(l, x2, mod4, brow, g_pre, w_in_p, sgu, tm):
    R, D = x2.shape
    gn, w_st, bias, ones_bd = sgu
    H = GROUP_W // HEAD_D
    grid = (R // tm,)
    outs = [jax.ShapeDtypeStruct((R, GROUP_W), BF16)] + [jax.ShapeDtypeStruct((R, width), BF16) for (_, _, width) in IN_SLABS[1:]]
    return pl.pallas_call(
        _inproj_kernel,
        grid=grid,
        in_specs=[
            pl.BlockSpec((tm, D), lambda i: (i, 0)),
            _vec_spec(l, D),
            _mod_spec(l, 1, D, brow),
            _mod_spec(l, 0, D, brow),
            pl.BlockSpec((None, D, IN_PAD_COLS), lambda i: (l, 0, 0)),
            _vec_spec(l, GROUP_W),
            pl.BlockSpec((None, H * MLP_CHUNK, MLP_CHUNK), lambda i: (l, 0, 0)),
            pl.BlockSpec((None, MLP_CHUNK, GROUP_W), lambda i: (l, 0, 0)),
            pl.BlockSpec((GROUP_W, GROUP_W), lambda i: (0, 0)),
        ],
        out_specs=[pl.BlockSpec((tm, GROUP_W), lambda i: (i, 0))]
        + [pl.BlockSpec((tm, width), lambda i: (i, 0)) for (_, _, width) in IN_SLABS[1:]],
        out_shape=outs,
        compiler_params=_cparams("parallel"),
        name="inproj",
    )(x2, g_pre, mod4, mod4, w_in_p, gn, w_st, bias, ones_bd)


def _gla_kernel(*refs, rev, finish):
    if finish:
        (pb_ref, pg_ref, wg_ref, bg_ref, s0_ref, ones_ref, dec_ref, oprev_ref, gn_ref, o_ref, sfin_ref, st_scr) = refs
    else:
        (pb_ref, pg_ref, wg_ref, bg_ref, s0_ref, ones_ref, dec_ref, o_ref, sfin_ref, st_scr) = refs
    i = pl.program_id(1)
    C, W, T = GLA_CHUNK, GROUP_W, GLA_TILE
    H = W // HEAD_D
    nsub = T // C
    ntile = pb_ref.shape[0] // T

    @pl.when(i == 0)
    def _():
        st_scr[...] = s0_ref[...]

    tri = dec_ref[...]
    mask4 = jnp.concatenate([tri.astype(F32)] * H, axis=0)
    head_t = _lane_group((T, W), HEAD_D)
    bd = (lax.broadcasted_iota(jnp.int32, (W, W), 0) // HEAD_D) == _lane_group((W, W), HEAD_D)
    tiles = list(range(ntile - 1, -1, -1)) if rev else list(range(ntile))
    subs = list(range(nsub - 1, -1, -1)) if rev else list(range(nsub))
    rows = lambda a: slice(a * T, (a + 1) * T)
    sub = lambda j: slice(j * C, (j + 1) * C)
    stack = lambda blocks: jnp.concatenate([blocks[j] for j in range(nsub)], axis=0)

    wg_hi, wg_lo = _split(wg_ref[...])
    logg = {}
    for a in tiles:
        z = _dot(pg_ref[rows(a), :], wg_hi) + _dot(pg_ref[rows(a), :], wg_lo) + bg_ref[...]
        logg[a] = (jnp.minimum(z, 0.0) - jnp.log(1.0 + jnp.exp(-jnp.abs(z)))) * (1.0 / GATE_TEMP)
    cums = {}
    for a in tiles:
        g_hi, g_lo = _split(logg[a])
        cums[a] = _dot(tri, g_hi) + _dot(tri, g_lo)
    qin, qoff, kin, kend, vb, tot, ptile = {}, {}, {}, {}, {}, {}, {}
    for a in tiles:
        cum = cums[a]
        q = pb_ref[rows(a), 0:W].astype(F32) * (HEAD_D ** -0.5)
        k = pb_ref[rows(a), W:2 * W].astype(F32)
        last_row = (lambda j: j * C) if rev else (lambda j: j * C + C - 1)
        off, run = {}, jnp.zeros((1, W), F32)
        for j in subs:
            tot[(a, j)] = cum[last_row(j):last_row(j) + 1]
            off[j] = run
            run = run + tot[(a, j)]
        ptile[a] = jnp.exp(run)
        e = jnp.exp(cum)
        qin[a] = (q * e).astype(BF16)
        qoff[a] = (q * (e * jnp.exp(stack({j: jnp.broadcast_to(off[j], (C, W)) for j in subs})))).astype(BF16)
        kin[a] = (k * jnp.exp(-cum)).astype(BF16)
        kend[a] = (k * jnp.exp(stack({j: jnp.broadcast_to(tot[(a, j)], (C, W)) for j in subs}) - cum)).astype(BF16)
        vb[a] = pb_ref[rows(a), 2 * W:3 * W]
    sc = {}
    for a in tiles:
        qst = jnp.concatenate([jnp.where(head_t == h, qin[a], jnp.zeros_like(qin[a])) for h in range(H)], axis=0)
        sc[a] = (_dot_nt(qst, kin[a]) * mask4).astype(BF16)
    ost = {a: _dot(sc[a], vb[a]) for a in tiles}
    kvt = {(a, j): jnp.where(bd, _dot_tn(vb[a][sub(j)], kend[a][sub(j)]), 0.0) for a in tiles for j in subs}
    cross, rend = {}, {}
    for a in tiles:
        r = None
        for j in subs:
            if r is not None:
                cross[(a, j)] = _dot_nt(qin[a][sub(j)], r.astype(BF16))
                r = r * jnp.exp(tot[(a, j)]) + kvt[(a, j)]
            else:
                cross[(a, j)] = jnp.zeros((C, W), F32)
                r = kvt[(a, j)]
        rend[a] = r
    st = st_scr[...]
    inter = {}
    for a in tiles:
        inter[a] = _dot_nt(qoff[a], st.astype(BF16))
        st = st * ptile[a] + rend[a]
    st_scr[...] = st

    for a in tiles:
        o = inter[a] + jnp.concatenate([cross[(a, j)] for j in range(nsub)], axis=0)
        for h in range(H):
            o = o + jnp.where(head_t == h, ost[a][h * T:(h + 1) * T], 0.0)
        if finish:
            o = o + oprev_ref[rows(a), :]
            ms = _dot_x2(o * o, ones_ref[...]) * (1.0 / HEAD_D)
            o = o * lax.rsqrt(ms + EPS) * gn_ref[...]
            r = pb_ref[rows(a), 3 * W:4 * W].astype(F32)
            o_ref[rows(a), :] = (o * (r * _sigmoid(r))).astype(o_ref.dtype)
        else:
            o_ref[rows(a), :] = o

    @pl.when(i == pl.num_programs(1) - 1)
    def _():
        sfin_ref[...] = st_scr[...]


def _gla_decay_matrices():
    T, C = GLA_TILE, GLA_CHUNK
    t = jnp.arange(T)[:, None]
    s = jnp.arange(T)[None, :]
    same = (t // C) == (s // C)
    return jnp.stack([same & (s <= t), same & (s >= t)]).astype(BF16)


def _gla_pass(l, d, pb3, pg3, wg, bg, s0, ones_bd, dec, oprev, gn, tt):
    B, n, _ = pb3.shape
    nt = n // tt
    rev = d == 1
    finish = oprev is not None
    W = GROUP_W

    def tok(b, i):
        return (b, (nt - 1 - i) if rev else i, 0)

    in_specs = [
        pl.BlockSpec((None, tt, 4 * W), tok),
        pl.BlockSpec((None, tt, LANES), tok),
        pl.BlockSpec((None, None, LANES, W), lambda b, i: (l, d, 0, 0)),
        pl.BlockSpec((None, None, 1, W), lambda b, i: (l, d, 0, 0)),
        pl.BlockSpec((None, W, W), lambda b, i: (b, 0, 0)),
        pl.BlockSpec((W, W), lambda b, i: (0, 0)),
        pl.BlockSpec((None, GLA_TILE, GLA_TILE), lambda b, i: (d, 0, 0)),
    ]
    args = [pb3, pg3, wg, bg, s0, ones_bd, dec]
    if finish:
        in_specs += [pl.BlockSpec((None, tt, W), tok), _vec_spec(l, W)]
        args += [oprev, gn]
    return pl.pallas_call(
        functools.partial(_gla_kernel, rev=rev, finish=finish),
        grid=(B, nt),
        in_specs=in_specs,
        out_specs=[pl.BlockSpec((None, tt, W), tok), pl.BlockSpec((None, W, W), lambda b, i: (b, 0, 0))],
        out_shape=[jax.ShapeDtypeStruct((B, n, W), BF16 if finish else F32),
                   jax.ShapeDtypeStruct((B, W, W), F32)],
        scratch_shapes=[pltpu.VMEM((W, W), F32)],
        compiler_params=_cparams("arbitrary", "arbitrary"),
        name="gla_bwd" if rev else "gla_fwd",
    )(*args)


def _gla(l, pb_c, pg_c, pb_x, pg_x, wg, bg, gn, ones_bd, tt_c, tt_x):
    B = pb_x.shape[0]
    zero = jnp.zeros((B, GROUP_W, GROUP_W), F32)
    dec = _gla_decay_matrices()
    ofc, sfc = _gla_pass(l, 0, pb_c, pg_c, wg, bg, zero, ones_bd, dec, None, None, tt_c)
    ofx, _ = _gla_pass(l, 0, pb_x, pg_x, wg, bg, sfc, ones_bd, dec, None, None, tt_x)
    ob_c, sbc = _gla_pass(l, 1, pb_c, pg_c, wg, bg, zero, ones_bd, dec, ofc, gn, tt_c)
    ob_x, _ = _gla_pass(l, 1, pb_x, pg_x, wg, bg, sbc, ones_bd, dec, ofx, gn, tt_x)
    return ob_c, ob_x


def _s5_matrices(a_re, a_im, log_dt, b_re, b_im, c_re, c_im, d_skip):
    T = S5_T
    G, P = a_re.shape[1:]
    I = b_re.shape[-1]
    lam = lax.complex(a_re.astype(F32), a_im.astype(F32))
    ldt = lam * jnp.exp(log_dt.astype(F32))[..., None]
    lam_bar = jnp.exp(ldt)
    b_bar = ((lam_bar - 1.0) / lam)[..., None] * lax.complex(b_re.astype(F32), b_im.astype(F32))
    cmat = lax.complex(c_re.astype(F32), c_im.astype(F32))
    steps = jnp.arange(T + 1, dtype=F32)
    pw = jnp.exp(ldt[..., None] * steps)
    taps = jnp.einsum('dgop,dgpk,dgpi->dgiko', cmat, pw[..., :T], b_bar).real
    taps = taps.at[0, :, :, 0, :].add(jnp.eye(I, dtype=F32)[None] * d_skip.astype(F32)[:, :, None])
    row = T * I
    zeros = jnp.zeros((G, I, row), F32)
    fwd = jnp.concatenate([zeros, taps[0].reshape(G, I, row)], axis=-1)
    bwd = jnp.concatenate([jnp.flip(taps[1], axis=2).reshape(G, I, row), zeros], axis=-1)
    m = jnp.stack([fwd[..., (T - s) * I:(T - s) * I + row] + bwd[..., (T - 1 - s) * I:(T - 1 - s) * I + row]
                   for s in range(T)], axis=1).reshape(G, row, row)

    ar = jnp.arange(T)
    pf = pw[0][..., T - 1 - ar]
    pb = pw[1][..., ar]
    bf = jnp.einsum('gps,gpi->gsip', pf, b_bar[0]).reshape(G, T * I, P)
    bb = jnp.einsum('gps,gpi->gsip', pb, b_bar[1]).reshape(G, T * I, P)
    bmat = jnp.concatenate([bf.real, bf.imag, bf.imag, bf.real, bb.real, bb.imag, bb.imag, bb.real], axis=-1)

    cf = jnp.einsum('gop,gpt->gpto', cmat[0], pw[0][..., 1 + ar]).reshape(G, P, T * I)
    cb = jnp.einsum('gop,gpt->gpto', cmat[1], pw[1][..., T - ar]).reshape(G, P, T * I)
    w = jnp.concatenate([m, cf.real, -cf.imag, cb.real, -cb.imag], axis=1)

    a = pw[..., T]
    acoef = jnp.stack([jnp.concatenate([a[0].real, a[0].real], axis=-1), jnp.concatenate([-a[0].imag, a[0].imag], axis=-1),
                       jnp.concatenate([a[1].real, a[1].real], axis=-1), jnp.concatenate([-a[1].imag, a[1].imag], axis=-1)])
    return bmat.astype(BF16), w.astype(BF16), acoef


def _s5_scan_kernel(xf_ref, xb_ref, bm_ref, a_ref, h0_ref, hf_ref, hb_ref, hfin_ref, st_scr, sf_scr, sb_scr, hf_scr, hb_scr):
    j = pl.program_id(1)
    G, tc, _ = xf_ref.shape
    pitch = tc + S5_PITCH_PAD
    half = hf_ref.shape[-1]

    @pl.when(j == 0)
    def _():
        st_scr[...] = h0_ref[...]

    for g in range(G):
        sf = _dot(xf_ref[g], bm_ref[g, :, 0:2 * half])
        sb = _dot(xb_ref[g], bm_ref[g, :, 2 * half:4 * half])
        for k in range(2):
            sf_scr[k, g * pitch:g * pitch + tc, :] = sf[:, k * half:(k + 1) * half]
            sb_scr[k, g * pitch:g * pitch + tc, :] = sb[:, k * half:(k + 1) * half]

    a1f, a2f, a1b, a2b = a_ref[0], a_ref[1], a_ref[2], a_ref[3]
    chunk = lambda i: pl.ds(i, G, stride=pitch)

    def body(i, hs):
        h1f, h2f, h1b, h2b = hs
        ib = tc - 1 - i
        hf_scr[chunk(i), :] = h1f
        hb_scr[chunk(ib), :] = h1b
        return (h1f * a1f + h2f * a2f + sf_scr[0, chunk(i), :], h2f * a1f - h1f * a2f + sf_scr[1, chunk(i), :],
                h1b * a1b + h2b * a2b + sb_scr[0, chunk(ib), :], h2b * a1b - h1b * a2b + sb_scr[1, chunk(ib), :])

    hs = lax.fori_loop(0, tc, body, (st_scr[0], st_scr[1], st_scr[2], st_scr[3]))
    for k in range(4):
        st_scr[k] = hs[k]
    for g in range(G):
        hf_ref[g] = hf_scr[g * pitch:g * pitch + tc, :]
        hb_ref[g] = hb_scr[g * pitch:g * pitch + tc, :]

    @pl.when(j == pl.num_programs(1) - 1)
    def _():
        hfin_ref[...] = st_scr[...]


def _s5_scan(l, xg, bmat, acoef, h0, B, tc):
    G, R, K = xg.shape
    nt = R // B // tc
    half = bmat.shape[-1] // 4
    fwd = lambda b, j: (0, b * nt + j, 0)
    bwd = lambda b, j: (0, b * nt + nt - 1 - j, 0)
    st_spec = pl.BlockSpec((None, 4, G, half), lambda b, j: (b, 0, 0, 0))
    pitch = tc + S5_PITCH_PAD
    return pl.pallas_call(
        _s5_scan_kernel,
        grid=(B, nt),
        in_specs=[pl.BlockSpec((G, tc, K), fwd),
                  pl.BlockSpec((G, tc, K), bwd),
                  pl.BlockSpec((None, G, K, 4 * half), lambda b, j: (l, 0, 0, 0)),
                  pl.BlockSpec((None, 4, G, half), lambda b, j: (l, 0, 0, 0)),
                  st_spec],
        out_specs=[pl.BlockSpec((G, tc, half), fwd), pl.BlockSpec((G, tc, half), bwd), st_spec],
        out_shape=[jax.ShapeDtypeStruct((G, R, half), F32), jax.ShapeDtypeStruct((G, R, half), F32),
                   jax.ShapeDtypeStruct((B, 4, G, half), F32)],
        scratch_shapes=[pltpu.VMEM((4, G, half), F32),
                        pltpu.VMEM((2, G * pitch, half), F32), pltpu.VMEM((2, G * pitch, half), F32),
                        pltpu.VMEM((G * pitch, half), F32), pltpu.VMEM((G * pitch, half), F32)],
        compiler_params=_cparams("arbitrary", "arbitrary"),
        name="s5_scan",
    )(xg, xg, bmat, acoef, h0)


def _s5_gather_kernel(x_ref, o_ref, xs_scr, xt_scr):
    G, tc, _ = o_ref.shape
    nh = x_ref.shape[1] // LANES
    gh = G // nh
    for h in range(nh):
        xs_scr[h] = x_ref[:, h * LANES:(h + 1) * LANES].astype(F32)
    for t in range(S5_T):
        for h in range(nh):
            rt = xs_scr[h, pl.ds(t, tc, stride=S5_T), :].T
            for g in range(gh):
                xt_scr[h * gh + g, t * S5_IN:(t + 1) * S5_IN, :] = rt[g * S5_IN:(g + 1) * S5_IN, :]
    for g in range(G):
        o_ref[g] = xt_scr[g].T.astype(o_ref.dtype)


def _s5_gather(x2, tc):
    N, W = x2.shape
    G = W // S5_IN
    R = N // S5_T
    return pl.pallas_call(
        _s5_gather_kernel,
        grid=(R // tc,),
        in_specs=[pl.BlockSpec((tc * S5_T, W), lambda i: (i, 0))],
        out_specs=pl.BlockSpec((G, tc, S5_T * S5_IN), lambda i: (0, i, 0)),
        out_shape=jax.ShapeDtypeStruct((G, R, S5_T * S5_IN), BF16),
        scratch_shapes=[pltpu.VMEM((W // LANES, tc * S5_T, LANES), F32), pltpu.VMEM((G, S5_T * S5_IN, tc), F32)],
        compiler_params=_cparams("parallel"),
        name="s5_gather",
    )(x2)


def _s5_scatter_kernel(x_ref, hf_ref, hb_ref, wy_ref, w_ref, b_ref, o_ref, yt_scr, tok_scr):
    G, tc, k = x_ref.shape
    kh = hf_ref.shape[-1]
    nh = tok_scr.shape[0]
    gh = G // nh
    for g in range(G):
        y = _dot(x_ref[g], wy_ref[g, 0:k, :])
        y = y + _dot(hf_ref[g].astype(BF16), wy_ref[g, k:k + kh, :])
        y = y + _dot(hb_ref[g].astype(BF16), wy_ref[g, k + kh:, :])
        yt = y.T
        for t in range(S5_T):
            yt_scr[t, g // gh, (g % gh) * S5_IN:(g % gh + 1) * S5_IN, :] = yt[t * S5_IN:(t + 1) * S5_IN, :]
    for t in range(S5_T):
        for h in range(nh):
            tok_scr[h, pl.ds(t, tc, stride=S5_T), :] = yt_scr[t, h].T
    y = _gelu(jnp.concatenate([tok_scr[h] for h in range(nh)], axis=1))
    gate = _sigmoid(_dot(y.astype(BF16), w_ref[...]) + b_ref[...])
    o_ref[...] = (y * gate).astype(o_ref.dtype)


def _s5_scatter_finish(l, xg, hf, hb, wy, w_glu, b_glu, tc):
    G, R, K = xg.shape
    kh = hf.shape[-1]
    W = G * S5_IN
    grp = lambda width: pl.BlockSpec((G, tc, width), lambda i: (0, i, 0))
    return pl.pallas_call(
        _s5_scatter_kernel,
        grid=(R // tc,),
        in_specs=[grp(K), grp(kh), grp(kh),
                  pl.BlockSpec((None, G, K + 2 * kh, K), lambda i: (l, 0, 0, 0)),
                  pl.BlockSpec((None, W, W), lambda i: (l, 0, 0)),
                  _vec_spec(l, W)],
        out_specs=pl.BlockSpec((tc * S5_T, W), lambda i: (i, 0)),
        out_shape=jax.ShapeDtypeStruct((R * S5_T, W), BF16),
        scratch_shapes=[pltpu.VMEM((S5_T, W // LANES, LANES, tc), F32), pltpu.VMEM((W // LANES, tc * S5_T, LANES), F32)],
        compiler_params=_cparams("parallel"),
        name="s5_scatter",
    )(xg, hf, hb, wy, w_glu, b_glu)


def _s5(l, pc_c, pc_x, mats, w_glu, b_glu, need_ctx):
    bmat, w, acoef = mats
    B, n, W = pc_x.shape
    nctx = pc_c.shape[1]
    G = W // S5_IN
    tc_c = min(S5_TILE, B * nctx // S5_T)
    tc_x = min(S5_TILE, n // S5_T)
    xg_c = _s5_gather(pc_c.reshape(B * nctx, W), tc_c)
    xg_x = _s5_gather(pc_x.reshape(B * n, W), tc_x)
    h0 = jnp.zeros((B, 4, G, acoef.shape[-1]), F32)
    hf_c, hb_c, h1 = _s5_scan(l, xg_c, bmat, acoef, h0, B, nctx // S5_T)
    hf_x, hb_x, _ = _s5_scan(l, xg_x, bmat, acoef, h1, B, tc_x)
    oc_x = _s5_scatter_finish(l, xg_x, hf_x, hb_x, w, w_glu, b_glu, tc_x)
    oc_c = _s5_scatter_finish(l, xg_c, hf_c, hb_c, w, w_glu, b_glu, tc_c) if need_ctx else None
    return oc_c, oc_x


def _mla_prep_kernel(*refs, rope):
    refs = refs[:-5] + refs[-3:]
    if rope:
        pd_ref, qn_ref, kvn_ref, wq_ref, wqs_ref, wk_ref, wks_ref, wv_ref, cos_ref, sin_ref, q_ref, k_ref, v_ref = refs
    else:
        pd_ref, qn_ref, kvn_ref, wq_ref, wk_ref, wv_ref, q_ref, k_ref, v_ref = refs
    cq = pd_ref[:, 0:MLA_Q_LANES].astype(F32)
    ms = jnp.sum(cq * cq, axis=-1, keepdims=True) * (1.0 / MLA_Q_RANK)
    cqn = (cq * lax.rsqrt(ms + EPS) * qn_ref[...]).astype(BF16)
    ck = pd_ref[:, MLA_Q_LANES:MLA_Q_LANES + MLA_KV_LANES].astype(F32)
    lane = lax.broadcasted_iota(jnp.int32, ck.shape, 1)
    is_lat = lane < MLA_KV_RANK
    ms = jnp.sum(jnp.where(is_lat, ck * ck, 0.0), axis=-1, keepdims=True) * (1.0 / MLA_KV_RANK)
    ckn = jnp.where(is_lat, ck * lax.rsqrt(ms + EPS) * kvn_ref[...], ck).astype(BF16)
    q = _dot(cqn, wq_ref[...])
    k = _dot(ckn, wk_ref[...])
    if rope:
        cos = jnp.concatenate([cos_ref[...]] * MLA_HEADS, axis=1)
        sin = jnp.concatenate([sin_ref[...]] * MLA_HEADS, axis=1)
        q = q * cos + _dot(cqn, wqs_ref[...]) * sin
        k = k * cos + _dot(ckn, wks_ref[...]) * sin
    q_ref[...] = (q * ((MLA_NOPE + MLA_ROPE) ** -0.5 * LOG2E)).astype(BF16)
    k_ref[...] = k.astype(BF16)
    v_ref[...] = _dot(ckn, wv_ref[...]).astype(BF16)


def _mla_prep(l, pd, qn, kvn, wts, tables, tm, n, B, row0, nk, kv_into):
    R = pd.shape[0]
    HP = MLA_HEADS * MLA_HEAD_PAD
    wq, wqs, wk, wks, wv = wts
    rope = tables is not None
    npt = n // tm
    wspec = lambda r: pl.BlockSpec((None, r, HP), lambda i: (l, 0, 0))
    QL, KL = MLA_Q_LANES, MLA_KV_LANES
    in_specs = [pl.BlockSpec((tm, QL + KL), lambda i: (i, 0)), _vec_spec(l, QL), _vec_spec(l, KL)]
    if rope:
        tspec = pl.BlockSpec((tm, MLA_HEAD_PAD), lambda i: (i % npt, 0))
        in_specs += [wspec(QL), wspec(QL), wspec(KL), wspec(KL), wspec(KL), tspec, tspec]
        args = (pd, qn, kvn, wq, wqs, wk, wks, wv) + tuple(tables)
    else:
        in_specs += [wspec(QL), wspec(KL), wspec(KL)]
        args = (pd, qn, kvn, wq, wk, wv)
    aliases = {len(args): 1, len(args) + 1: 2}
    in_specs += [pl.BlockSpec(memory_space=pl.ANY)] * 2
    args = args + tuple(kv_into)
    kv_spec = pl.BlockSpec((None, tm, HP), lambda i: (i // npt, row0 // tm + i % npt, 0))
    kv_shape = jax.ShapeDtypeStruct((B, nk, HP), BF16)
    return pl.pallas_call(
        functools.partial(_mla_prep_kernel, rope=rope),
        grid=(R // tm,),
        in_specs=in_specs,
        out_specs=[pl.BlockSpec((tm, HP), lambda i: (i, 0)), kv_spec, kv_spec],
        out_shape=[jax.ShapeDtypeStruct((R, HP), BF16), kv_shape, kv_shape],
        input_output_aliases=aliases,
        compiler_params=_cparams("parallel"),
        name="mla_prep",
    )(*args)


def _attn_kernel(q_ref, k_ref, v_ref, o_ref, m_scr, l_scr, acc_scr):
    kv = pl.program_id(2)
    HP = MLA_HEAD_PAD
    tk = k_ref.shape[0]

    @pl.when(kv == 0)
    def _():
        m_scr[...] = jnp.full(m_scr.shape, -jnp.inf, F32)
        l_scr[...] = jnp.zeros(l_scr.shape, F32)
        acc_scr[...] = jnp.zeros(acc_scr.shape, F32)

    def scores(h):
        lanes = slice(h * HP, (h + 1) * HP)
        return _dot_nt(q_ref[:, lanes], k_ref[:, lanes])

    s_next = scores(0)
    for h in range(MLA_HEADS):
        lanes = slice(h * HP, (h + 1) * HP)
        s = s_next
        if h + 1 < MLA_HEADS:
            s_next = scores(h + 1)
        m_prev = m_scr[h]
        m_new = jnp.maximum(m_prev, jnp.max(s, axis=1, keepdims=True))
        alpha = jnp.exp2(m_prev - m_new)
        p = jnp.exp2(s - m_new[:, 0:1])
        lp = p[:, 0:LANES]
        for c in range(1, tk // LANES):
            lp = lp + p[:, c * LANES:(c + 1) * LANES]
        l_scr[h] = alpha * l_scr[h] + lp
        acc_scr[h] = alpha * acc_scr[h] + _dot(p.astype(BF16), v_ref[:, lanes])
        m_scr[h] = m_new

    @pl.when(kv == pl.num_programs(2) - 1)
    def _():
        low = lax.broadcasted_iota(jnp.int32, acc_scr.shape[1:], 1) < MLA_V
        norm = lambda h: acc_scr[h] * (1.0 / jnp.sum(l_scr[h], axis=1, keepdims=True))
        outs = [jnp.where(low, norm(h), norm(h + 1)) for h in range(0, MLA_HEADS, 2)]
        o_ref[...] = jnp.concatenate(outs, axis=1).astype(o_ref.dtype)


def _attention(q3, k3, v3, tq, tk, k0, nk):
    B, nq, HP = q3.shape
    kb = k0 // tk
    return pl.pallas_call(
        _attn_kernel,
        grid=(B, nq // tq, nk // tk),
        in_specs=[pl.BlockSpec((None, tq, HP), lambda b, i, j: (b, i, 0)),
                  pl.BlockSpec((None, tk, HP), lambda b, i, j: (b, kb + j, 0)),
                  pl.BlockSpec((None, tk, HP), lambda b, i, j: (b, kb + j, 0))],
        out_specs=pl.BlockSpec((None, tq, MLA_HEADS * MLA_V), lambda b, i, j: (b, i, 0)),
        out_shape=jax.ShapeDtypeStruct((B, nq, MLA_HEADS * MLA_V), BF16),
        scratch_shapes=[pltpu.VMEM((MLA_HEADS, tq, LANES), F32),
                        pltpu.VMEM((MLA_HEADS, tq, LANES), F32),
                        pltpu.VMEM((MLA_HEADS, tq, MLA_HEAD_PAD), F32)],
        compiler_params=_cparams("parallel", "parallel", "arbitrary"),
        name="attention",
    )(q3, k3, v3)


def _outproj_kernel(oa, ob, oc, od, w_ref, x_ref, gpost_ref, gt_ref, gpre_ref, sc_ref, sh_ref, xo_ref, h_ref):
    W = GROUP_W
    mix = _dot(oa[...], w_ref[0:W, :])
    mix = mix + _dot(ob[...], w_ref[W:2 * W, :])
    mix = mix + _dot(oc[...], w_ref[2 * W:3 * W, :])
    mix = mix + _dot(od[...], w_ref[3 * W:4 * W, :])
    x = x_ref[...] + gt_ref[...] * (_rms(mix) * gpost_ref[...])
    xo_ref[...] = x
    h_ref[...] = (_rms(x) * gpre_ref[...] * (1.0 + sc_ref[...]) + sh_ref[...]).astype(h_ref.dtype)


def _outproj(l, parts, w_out, x2, mod4, brow, g_post, g_pre_ffn, tm):
    R, D = x2.shape
    W = GROUP_W
    part_spec = pl.BlockSpec((tm, W), lambda i: (i, 0))
    row_spec = pl.BlockSpec((tm, D), lambda i: (i, 0))
    return pl.pallas_call(
        _outproj_kernel,
        grid=(R // tm,),
        in_specs=[part_spec] * 4 + [
            pl.BlockSpec((None, 4 * W, D), lambda i: (l, 0, 0)),
            row_spec,
            _vec_spec(l, D),
            _mod_spec(l, 2, D, brow),
            _vec_spec(l, D),
            _mod_spec(l, 4, D, brow),
            _mod_spec(l, 3, D, brow),
        ],
        out_specs=[row_spec, row_spec],
        out_shape=[jax.ShapeDtypeStruct((R, D), F32), jax.ShapeDtypeStruct((R, D), BF16)],
        compiler_params=_cparams("parallel"),
        name="outproj",
    )(*parts, w_out, x2, g_post, mod4, g_pre_ffn, mod4, mod4)


FFN_HALO = 16
FFN_COLS = 256


def _ffn_kernel(hp_ref, h_ref, hn_ref, wup_ref, cw_ref, cb_ref, wdn_ref, x_ref, gpost_ref, gt_ref, o_ref, y_scr):
    i = pl.program_id(1)
    tm = h_ref.shape[0]
    dff = wdn_ref.shape[0]
    use = FFN_HALO // 2
    rows = tm + 2 * use
    prev = jnp.where(i == 0, jnp.zeros_like(hp_ref[...]), hp_ref[...])[FFN_HALO - use:]
    nxt = jnp.where(i == pl.num_programs(1) - 1, jnp.zeros_like(hn_ref[...]), hn_ref[...])[:use]
    hb = jnp.concatenate([prev, h_ref[...], nxt], axis=0)

    def conv(z, cols):
        w = cw_ref[:, cols]
        out = cb_ref[:, cols] + w[0:1] * pltpu.roll(z, 1, axis=0) + w[1:2] * z + w[2:3] * pltpu.roll(z, rows - 1, axis=0)
        return out[use:use + tm]

    def cols(j):
        return slice(j * FFN_COLS, (j + 1) * FFN_COLS), slice(dff + j * FFN_COLS, dff + (j + 1) * FFN_COLS)

    def up(j):
        ca, cg = cols(j)
        return _dot(hb, wup_ref[:, ca]), _dot(hb, wup_ref[:, cg])

    nchunks = dff // FFN_COLS
    z_next = up(0)
    for j in range(nchunks):
        ca, cg = cols(j)
        za, zg = z_next
        if j + 1 < nchunks:
            z_next = up(j + 1)
        y_scr[:, ca] = _gelu_gate(conv(za, ca), conv(zg, cg)).astype(BF16)
    acc = _dot(y_scr[...], wdn_ref[...])
    o_ref[...] = x_ref[...] + gt_ref[...] * (_rms(acc) * gpost_ref[...])


def _ffn(l, h3, x3, w_up, conv_w, conv_b, w_down, mod4, brow, g_post, tm):
    B, n, D = x3.shape
    nt = n // tm
    hb = tm // FFN_HALO
    nh = n // FFN_HALO
    dff = w_down.shape[1]
    once = pl.Buffered(1)
    return pl.pallas_call(
        _ffn_kernel,
        grid=(B, nt),
        in_specs=[
            pl.BlockSpec((None, FFN_HALO, D), lambda b, i: (b, jnp.maximum(i * hb - 1, 0), 0)),
            pl.BlockSpec((None, tm, D), lambda b, i: (b, i, 0)),
            pl.BlockSpec((None, FFN_HALO, D), lambda b, i: (b, jnp.minimum((i + 1) * hb, nh - 1), 0)),
            pl.BlockSpec((None, D, 2 * dff), lambda b, i: (l, 0, 0), pipeline_mode=once),
            pl.BlockSpec((None, 3, 2 * dff), lambda b, i: (l, 0, 0)),
            pl.BlockSpec((None, 1, 2 * dff), lambda b, i: (l, 0, 0)),
            pl.BlockSpec((None, dff, D), lambda b, i: (l, 0, 0), pipeline_mode=once),
            pl.BlockSpec((None, tm, D), lambda b, i: (b, i, 0)),
            _vec_spec(l, D),
            _mod_spec(l, 5, D, brow),
        ],
        out_specs=pl.BlockSpec((None, tm, D), lambda b, i: (b, i, 0)),
        out_shape=jax.ShapeDtypeStruct((B, n, D), F32),
        scratch_shapes=[pltpu.VMEM((tm, dff), BF16)],
        compiler_params=_cparams("parallel", "parallel"),
        name="conv_ffn",
    )(h3, h3, h3, w_up, conv_w, conv_b, w_down, x3, g_post, mod4)


def _prep_w_in(w_in):
    w_in = w_in.astype(BF16)
    z = lambda n: jnp.zeros(w_in.shape[:-1] + (n,), w_in.dtype)
    a = w_in[..., 0:512]
    b = w_in[..., 512:1536]
    gl = w_in[..., 1536:1568]
    c = w_in[..., 1568:1824]
    cq = w_in[..., 1824:2048]
    ckv_kr = w_in[..., 2048:2176]
    return jnp.concatenate([a, b, c, cq, z(32), ckv_kr, gl, z(96)], axis=-1)


def _rope_swap(t):
    q = MLA_ROPE // 4
    return jnp.concatenate([t[..., q:2 * q], t[..., 0:q], t[..., 3 * q:4 * q], t[..., 2 * q:3 * q]], axis=-1)


def _prep_mla(w_uq, w_ukv):
    L = w_uq.shape[0]
    H, NP, RP, HP = MLA_HEADS, MLA_NOPE, MLA_ROPE, MLA_HEAD_PAD
    wq = w_uq.reshape(L, MLA_Q_RANK, H, NP + RP)
    zq = jnp.zeros((L, MLA_Q_RANK, H, HP - NP - RP), w_uq.dtype)
    znope = jnp.zeros((L, MLA_Q_RANK, H, NP), w_uq.dtype)
    q_main = jnp.concatenate([wq, zq], axis=-1)
    q_swap = jnp.concatenate([znope, _rope_swap(wq[..., NP:]), zq], axis=-1)
    padq = lambda w: jnp.pad(w.reshape(L, MLA_Q_RANK, H * HP), ((0, 0), (0, MLA_Q_LANES - MLA_Q_RANK), (0, 0)))

    wkv = w_ukv.reshape(L, MLA_KV_RANK, H, NP + MLA_V)
    zk = jnp.zeros((L, MLA_KV_RANK, H, HP - NP), w_ukv.dtype)
    k_lat = jnp.concatenate([wkv[..., :NP], zk], axis=-1)
    eye = jnp.eye(RP, dtype=w_ukv.dtype)
    place = lambda e: jnp.broadcast_to(
        jnp.concatenate([jnp.zeros((RP, NP), e.dtype), e, jnp.zeros((RP, HP - NP - RP), e.dtype)], axis=-1)[None, :, None, :],
        (L, RP, H, HP))
    k_main = jnp.concatenate([k_lat, place(eye)], axis=1)
    k_swap = jnp.concatenate([jnp.zeros_like(k_lat), place(_rope_swap(eye))], axis=1)
    zv = jnp.zeros((L, MLA_KV_RANK, H, HP - MLA_V), w_ukv.dtype)
    odd = (jnp.arange(H) % 2 == 1)[None, None, :, None]
    v_lat = jnp.where(odd, jnp.concatenate([zv, wkv[..., NP:]], axis=-1), jnp.concatenate([wkv[..., NP:], zv], axis=-1))
    v_main = jnp.concatenate([v_lat, jnp.zeros((L, RP, H, HP), w_ukv.dtype)], axis=1)
    flat = lambda w: w.reshape(L, w.shape[1], H * HP).astype(BF16)
    return (padq(q_main).astype(BF16), padq(q_swap).astype(BF16), flat(k_main), flat(k_swap), flat(v_main))


def _rope_tables(n):
    rows = n // GRID_W
    nf = MLA_ROPE // 4
    inv = ROPE_BASE ** (-jnp.arange(nf, dtype=F32) / nf)
    ar = jnp.arange(rows, dtype=F32)[:, None] * inv[None, :]
    ac = jnp.arange(GRID_W, dtype=F32)[:, None] * inv[None, :]
    by_row = lambda t: jnp.repeat(t, GRID_W, axis=0)
    by_col = lambda t: jnp.tile(t, (rows, 1))
    cr, sr, cc, sn = by_row(jnp.cos(ar)), by_row(jnp.sin(ar)), by_col(jnp.cos(ac)), by_col(jnp.sin(ac))
    one = jnp.ones((n, MLA_NOPE), F32)
    zero = jnp.zeros((n, MLA_HEAD_PAD - MLA_NOPE - MLA_ROPE), F32)
    cos = jnp.concatenate([one, cr, cr, cc, cc, zero], axis=1)
    sin = jnp.concatenate([0.0 * one, -sr, sr, -sn, sn, zero], axis=1)
    return cos, sin


def _pick_tile(n, want):
    t = min(n, want)
    while n % t:
        t //= 2
    return t


def kernel(x, c, ctx, c_ctx, w_mod, b_mod, g_pre_mix, g_post_mix, g_pre_ffn, g_post_ffn, w_in,
           sgu_norm, sgu_w, sgu_b, gla_w_gate, gla_b_gate, gla_norm,
           s5_a_re, s5_a_im, s5_log_dt, s5_b_re, s5_b_im, s5_c_re, s5_c_im, s5_d, s5_w_glu, s5_b_glu,
           mla_q_norm, mla_w_uq, mla_kv_norm, mla_w_ukv, w_out,
           ffn_w_up, ffn_conv_w, ffn_conv_b, ffn_w_down):
    B, n, D = x.shape
    nctx = ctx.shape[1]
    L = w_mod.shape[0]
    W = GROUP_W
    assert B < 8 and n % TOKEN_TILE == 0 and nctx % GLA_TILE == 0 and n % GRID_W == 0

    c8 = jnp.concatenate([c, c_ctx[None, :], jnp.zeros((8 - B - 1, D), F32)], axis=0)
    mod4 = _modulation(c8, w_mod, b_mod).reshape(L, 8, 1, 6 * D)
    vec = lambda p: p.reshape(L, 1, -1).astype(F32)
    g_pre_mix, g_post_mix, g_pre_ffn, g_post_ffn = map(vec, (g_pre_mix, g_post_mix, g_pre_ffn, g_post_ffn))
    w_in_p = _prep_w_in(w_in)
    sgu_gn = vec(sgu_norm)
    sgu_w_st = sgu_w.reshape(L, -1, MLP_CHUNK).astype(BF16)
    sgu_bias = jnp.repeat(jnp.swapaxes(sgu_b, 1, 2), HEAD_D, axis=2).astype(F32)
    ones_bd = jnp.kron(jnp.eye(W // HEAD_D, dtype=F32), jnp.ones((HEAD_D, HEAD_D), F32)).astype(BF16)
    gla_wg = jnp.zeros((L, 2, LANES, W), F32)
    gla_wg = gla_wg.at[:, 0, 0:GATE_RANK].set(gla_w_gate[:, 0]).at[:, 1, GATE_RANK:2 * GATE_RANK].set(gla_w_gate[:, 1])
    gla_bg = gla_b_gate.reshape(L, 2, 1, W).astype(F32)
    gla_gn = vec(gla_norm)
    s5_wglu = s5_w_glu.astype(BF16)
    s5_bglu = vec(s5_b_glu)
    mla_qn = jnp.pad(mla_q_norm, ((0, 0), (0, MLA_Q_LANES - MLA_Q_RANK))).reshape(L, 1, MLA_Q_LANES).astype(F32)
    mla_kvn = jnp.pad(mla_kv_norm, ((0, 0), (0, MLA_KV_LANES - MLA_KV_RANK))).reshape(L, 1, MLA_KV_LANES).astype(F32)
    mla_wts = _prep_mla(mla_w_uq, mla_w_ukv)
    tables = _rope_tables(n)
    w_out_b = w_out.astype(BF16)
    w_up_b = ffn_w_up.astype(BF16)
    w_down_b = ffn_w_down.astype(BF16)
    dff = ffn_w_down.shape[1]
    half_gate = jnp.concatenate([jnp.ones((dff,), F32), jnp.full((dff,), 0.5, F32)])
    conv_w = ffn_conv_w.astype(F32) * half_gate
    conv_b = (ffn_conv_b.astype(F32) * half_gate).reshape(L, 1, -1)

    s5_mats = jax.vmap(_s5_matrices)(s5_a_re, s5_a_im, s5_log_dt, s5_b_re, s5_b_im, s5_c_re, s5_c_im, s5_d)

    tm_x = _pick_tile(n, TOKEN_TILE)
    tm_c = _pick_tile(nctx, TOKEN_TILE)
    tpb_x = n // tm_x
    brow_x = lambda i: i // tpb_x
    brow_c = lambda *g: B
    brow_x2 = lambda b, i: b

    xs = x.reshape(B * n, D)
    cs = ctx.reshape(B * nctx, D)

    for l in range(L):
        need_ctx = l < L - 1
        sgu = (sgu_gn, sgu_w_st, sgu_bias, ones_bd)
        oa_x, pb_x, pc_x, pd_x, pg_x = _inproj(l, xs, mod4, brow_x, g_pre_mix, w_in_p, sgu, tm_x)
        oa_c, pb_c, pc_c, pd_c, pg_c = _inproj(l, cs, mod4, brow_c, g_pre_mix, w_in_p, sgu, tm_c)

        r3 = lambda t, m: t.reshape(B, m, t.shape[-1])
        ob_c, ob_x = _gla(l, r3(pb_c, nctx), r3(pg_c, nctx), r3(pb_x, n), r3(pg_x, n),
                          gla_wg, gla_bg, gla_gn, ones_bd, _pick_tile(nctx, GLA_TILE), _pick_tile(n, 4 * GLA_TILE))
        oc_c, oc_x = _s5(l, r3(pc_c, nctx), r3(pc_x, n), s5_mats, s5_wglu, s5_bglu, need_ctx)
        wq, wqs, wk, wks, wv = mla_wts
        nk = n + nctx
        kv0 = jnp.zeros((B, nk, MLA_HEADS * MLA_HEAD_PAD), BF16)
        q_x, k_all, v_all = _mla_prep(l, pd_x, mla_qn, mla_kvn, mla_wts, tables, tm_x, n, B, 0, nk, (kv0, kv0))
        q_c, k_all, v_all = _mla_prep(l, pd_c, mla_qn, mla_kvn, (wq, None, wk, None, wv), None, tm_c, nctx, B, n, nk,
                                      (k_all, v_all))
        tk = next(t for t in ATTN_KV_TILES if nk % t == 0)
        od_x = _attention(r3(q_x, n), k_all, v_all, tm_x, tk, 0, nk).reshape(B * n, W)

        xs, hx = _outproj(l, (oa_x, ob_x.reshape(B * n, W), oc_x, od_x), w_out_b, xs, mod4, brow_x,
                          g_post_mix, g_pre_ffn, tm_x)
        xs = _ffn(l, hx.reshape(B, n, D), xs.reshape(B, n, D), w_up_b, conv_w, conv_b, w_down_b, mod4, brow_x2,
                  g_post_ffn, tm_x).reshape(B * n, D)

        if need_ctx:
            od_c = _attention(r3(q_c, nctx), k_all, v_all, tm_c, nctx, n, nctx).reshape(B * nctx, W)
            cs, hc = _outproj(l, (oa_c, ob_c.reshape(B * nctx, W), oc_c, od_c), w_out_b, cs, mod4, brow_c,
                              g_post_mix, g_pre_ffn, tm_c)
            cs = _ffn(l, hc.reshape(B, nctx, D), cs.reshape(B, nctx, D), w_up_b, conv_w, conv_b, w_down_b, mod4,
                      brow_c, g_post_ffn, tm_c).reshape(B * nctx, D)
    return xs.reshape(B, n, D)
```

```python
import functools

import jax
import jax.numpy as jnp
from jax import lax
from jax.experimental import pallas as pl
from jax.experimental.pallas import tpu as pltpu

F32 = jnp.float32
BF16 = jnp.bfloat16

EPS = 1e-6
GRID_W = 64
GROUP_W = 256
HEAD_D = 64
MLP_CHUNK = 128
GATE_RANK = 16
GATE_TEMP = 16.0
GLA_CHUNK = 64
GLA_TILE = 256
S5_IN = 16
S5_T = 16
S5_TILE = 128
S5_PITCH_PAD = 8
MLA_HEADS = 4
MLA_NOPE = 64
MLA_ROPE = 32
MLA_V = 64
MLA_Q_RANK = 224
MLA_KV_RANK = 96
MLA_HEAD_PAD = 128
ROPE_BASE = 10000.0
LOG2E = 1.4426950408889634

LANES = 128
VMEM_LIMIT = 48 * 1024 * 1024
TOKEN_TILE = 512
MOD_COL_TILE = 1536
ATTN_KV_TILES = (2816, 1408, 768, 512, 256, 128)
MLA_Q_LANES = 256
MLA_KV_LANES = 128


def _cparams(*sem):
    return pltpu.CompilerParams(dimension_semantics=sem, vmem_limit_bytes=VMEM_LIMIT)


def _dot(a, b):
    return jnp.dot(a, b, preferred_element_type=F32)


def _dot_nt(a, b):
    return lax.dot_general(a, b, (((1,), (1,)), ((), ())), preferred_element_type=F32)


def _dot_tn(a, b):
    return lax.dot_general(a, b, (((0,), (0,)), ((), ())), preferred_element_type=F32)


def _split(a):
    hi = a.astype(BF16)
    lo = (a - hi.astype(F32)).astype(BF16)
    return hi, lo


def _dot_x2(a, b_bf16):
    hi, lo = _split(a)
    return _dot(hi, b_bf16) + _dot(lo, b_bf16)


def _dot_x3(a, b):
    ah, al = _split(a)
    bh, bl = _split(b)
    return _dot(ah, bh) + _dot(al, bh) + _dot(ah, bl)


def _rms(x):
    return x * lax.rsqrt(jnp.mean(x * x, axis=-1, keepdims=True) + EPS)


def _gelu(x):
    return 0.5 * x * (1.0 + jnp.tanh(0.7978845608028654 * (x + 0.044715 * (x * x * x))))


def _gelu_gate(a, half_g):
    u = a * (0.7978845608028654 + 0.035677408136300125 * (a * a))
    return (a * half_g) * (1.0 + jnp.tanh(u))


def _sigmoid(x):
    return 1.0 / (1.0 + jnp.exp(-x))


def _lane_group(shape, width):
    return lax.broadcasted_iota(jnp.int32, shape, len(shape) - 1) // width


def _mod_kernel(c_ref, w_ref, b_ref, o_ref):
    c = c_ref[...]
    s = c * _sigmoid(c)
    o_ref[...] = _dot_x3(s, w_ref[...]) + b_ref[...]


def _modulation(c8, w_mod, b_mod):
    L, D, W = w_mod.shape
    tn = MOD_COL_TILE
    return pl.pallas_call(
        _mod_kernel,
        grid=(L, W // tn),
        in_specs=[
            pl.BlockSpec((8, D), lambda l, j: (0, 0)),
            pl.BlockSpec((None, D, tn), lambda l, j: (l, 0, j)),
            pl.BlockSpec((None, 1, tn), lambda l, j: (l, 0, j)),
        ],
        out_specs=pl.BlockSpec((None, 8, tn), lambda l, j: (l, 0, j)),
        out_shape=jax.ShapeDtypeStruct((L, 8, W), F32),
        compiler_params=_cparams("arbitrary", "arbitrary"),
        name="modulation",
    )(c8, w_mod, b_mod.reshape(L, 1, W))


def _mod_spec(l, j, D, bfn):
    return pl.BlockSpec((None, None, 1, D), lambda *g: (l, bfn(*g), 0, j))


def _vec_spec(l, width):
    return pl.BlockSpec((None, 1, width), lambda *g: (l, 0, 0))


IN_SLABS = (("a", 0, 2 * GROUP_W), ("b", 512, 4 * GROUP_W), ("c", 1536, GROUP_W), ("d", 1792, MLA_Q_LANES + MLA_KV_LANES),
            ("g", 2176, LANES))
IN_PAD_COLS = 2304


def _sgu_tile(p, gn_ref, w_ref, b_ref, ones_ref, o_ref):
    tm = p.shape[0]
    g = _gelu(p)
    u = g[:, :GROUP_W]
    v = g[:, GROUP_W:]
    ms = _dot_x2(v * v, ones_ref[...]) * (1.0 / HEAD_D)
    vb = (v * lax.rsqrt(ms + EPS) * gn_ref[...]).astype(BF16)
    head = _lane_group((MLP_CHUNK, GROUP_W), HEAD_D)
    w = w_ref[...]
    for c in range(tm // MLP_CHUNK):
        rows = slice(c * MLP_CHUNK, (c + 1) * MLP_CHUNK)
        r = _dot(w, vb[rows])
        s = b_ref[...]
        for h in range(GROUP_W // HEAD_D):
            s = s + jnp.where(head == h, r[h * MLP_CHUNK:(h + 1) * MLP_CHUNK], 0.0)
        o_ref[rows, :] = (u[rows] * s).astype(o_ref.dtype)


def _mla_tile(pd, qn_ref, kvn_ref, wq_ref, wqs_ref, wk_ref, wks_ref, wv_ref, cos_ref, sin_ref, q_ref, k_ref, v_ref):
    cq = pd[:, 0:MLA_Q_LANES]
    ms = jnp.sum(cq * cq, axis=-1, keepdims=True) * (1.0 / MLA_Q_RANK)
    cqn = (cq * lax.rsqrt(ms + EPS) * qn_ref[...]).astype(BF16)
    ck = pd[:, MLA_Q_LANES:MLA_Q_LANES + MLA_KV_LANES]
    lane = lax.broadcasted_iota(jnp.int32, ck.shape, 1)
    is_lat = lane < MLA_KV_RANK
    ms = jnp.sum(jnp.where(is_lat, ck * ck, 0.0), axis=-1, keepdims=True) * (1.0 / MLA_KV_RANK)
    ckn = jnp.where(is_lat, ck * lax.rsqrt(ms + EPS) * kvn_ref[...], ck).astype(BF16)
    q = _dot(cqn, wq_ref[...])
    k = _dot(ckn, wk_ref[...])
    if cos_ref is not None:
        cos = jnp.concatenate([cos_ref[...]] * MLA_HEADS, axis=1)
        sin = jnp.concatenate([sin_ref[...]] * MLA_HEADS, axis=1)
        q = q * cos + _dot(cqn, wqs_ref[...]) * sin
        k = k * cos + _dot(ckn, wks_ref[...]) * sin
    q_ref[...] = (q * ((MLA_NOPE + MLA_ROPE) ** -0.5 * LOG2E)).astype(BF16)
    k_ref[...] = k.astype(BF16)
    v_ref[...] = _dot(ckn, wv_ref[...]).astype(BF16)


def _inproj_kernel(*refs, rope):
    x_ref, g_ref, sc_ref, sh_ref, w_ref, gn_ref, wsp_ref, bsp_ref, ones_ref, qn_ref, kvn_ref = refs[:11]
    if rope:
        wq_ref, wqs_ref, wk_ref, wks_ref, wv_ref, cos_ref, sin_ref = refs[11:18]
    else:
        wq_ref, wk_ref, wv_ref = refs[11:14]
        wqs_ref = wks_ref = cos_ref = sin_ref = None
    oa, ob, oc, og, q_ref, k_ref, v_ref = refs[-7:]
    h = _rms(x_ref[...]) * g_ref[...] * (1.0 + sc_ref[...]) + sh_ref[...]
    hb = h.astype(BF16)
    slab = {name: slice(off, off + width) for name, off, width in IN_SLABS}
    pa = _dot(hb, w_ref[:, slab["a"]])
    for name, o_ref in (("b", ob), ("c", oc), ("g", og)):
        o_ref[...] = _dot(hb, w_ref[:, slab[name]]).astype(o_ref.dtype)
    pd = _dot(hb, w_ref[:, slab["d"]])
    _mla_tile(pd, qn_ref, kvn_ref, wq_ref, wqs_ref, wk_ref, wks_ref, wv_ref, cos_ref, sin_ref, q_ref, k_ref, v_ref)
    _sgu_tile(pa, gn_ref, wsp_ref, bsp_ref, ones_ref, oa)


def _inproj(l, x2, mod4, brow, g_pre, w_in_p, sgu, mla, tm, n, B, row0, nk, kv_into):
    R, D = x2.shape
    gn, w_st, bias, ones_bd = sgu
    qn, kvn, (wq, wqs, wk, wks, wv), tables = mla
    rope = tables is not None
    H = GROUP_W // HEAD_D
    HP = MLA_HEADS * MLA_HEAD_PAD
    QL, KL = MLA_Q_LANES, MLA_KV_LANES
    npt = n // tm
    width = {name: w for name, _, w in IN_SLABS}
    wspec = lambda r: pl.BlockSpec((None, r, HP), lambda i: (l, 0, 0))
    in_specs = [
        pl.BlockSpec((tm, D), lambda i: (i, 0)),
        _vec_spec(l, D),
        _mod_spec(l, 1, D, brow),
        _mod_spec(l, 0, D, brow),
        pl.BlockSpec((None, D, IN_PAD_COLS), lambda i: (l, 0, 0)),
        _vec_spec(l, GROUP_W),
        pl.BlockSpec((None, H * MLP_CHUNK, MLP_CHUNK), lambda i: (l, 0, 0)),
        pl.BlockSpec((None, MLP_CHUNK, GROUP_W), lambda i: (l, 0, 0)),
        pl.BlockSpec((GROUP_W, GROUP_W), lambda i: (0, 0)),
        _vec_spec(l, QL),
        _vec_spec(l, KL),
    ]
    args = [x2, g_pre, mod4, mod4, w_in_p, gn, w_st, bias, ones_bd, qn, kvn]
    if rope:
        tspec = pl.BlockSpec((tm, MLA_HEAD_PAD), lambda i: (i % npt, 0))
        in_specs += [wspec(QL), wspec(QL), wspec(KL), wspec(KL), wspec(KL), tspec, tspec]
        args += [wq, wqs, wk, wks, wv, tables[0], tables[1]]
    else:
        in_specs += [wspec(QL), wspec(KL), wspec(KL)]
        args += [wq, wk, wv]
    aliases = {len(args): 5, len(args) + 1: 6}
    in_specs += [pl.BlockSpec(memory_space=pl.ANY)] * 2
    args += list(kv_into)
    row = lambda w: pl.BlockSpec((tm, w), lambda i: (i, 0))
    kv_spec = pl.BlockSpec((None, tm, HP), lambda i: (i // npt, row0 // tm + i % npt, 0))
    kv_shape = jax.ShapeDtypeStruct((B, nk, HP), BF16)
    return pl.pallas_call(
        functools.partial(_inproj_kernel, rope=rope),
        grid=(R // tm,),
        in_specs=in_specs,
        out_specs=[row(GROUP_W), row(width["b"]), row(width["c"]), row(width["g"]), row(HP), kv_spec, kv_spec],
        out_shape=[jax.ShapeDtypeStruct((R, w), BF16) for w in (GROUP_W, width["b"], width["c"], width["g"], HP)]
        + [kv_shape, kv_shape],
        input_output_aliases=aliases,
        compiler_params=_cparams("parallel"),
        name="inproj",
    )(*args)


def _gla_kernel(*refs, rev, finish):
    if finish:
        (pb_ref, pg_ref, wg_ref, bg_ref, s0_ref, ones_ref, dec_ref, oprev_ref, gn_ref, o_ref, sfin_ref, st_scr) = refs
    else:
        (pb_ref, pg_ref, wg_ref, bg_ref, s0_ref, ones_ref, dec_ref, o_ref, sfin_ref, st_scr) = refs
    i = pl.program_id(1)
    C, W, T = GLA_CHUNK, GROUP_W, GLA_TILE
    H = W // HEAD_D
    nsub = T // C
    ntile = pb_ref.shape[0] // T

    @pl.when(i == 0)
    def _():
        st_scr[...] = s0_ref[...]

    tri = dec_ref[...]
    mask4 = jnp.concatenate([tri.astype(F32)] * H, axis=0)
    head_t = _lane_group((T, W), HEAD_D)
    bd = (lax.broadcasted_iota(jnp.int32, (W, W), 0) // HEAD_D) == _lane_group((W, W), HEAD_D)
    tiles = list(range(ntile - 1, -1, -1)) if rev else list(range(ntile))
    subs = list(range(nsub - 1, -1, -1)) if rev else list(range(nsub))
    rows = lambda a: slice(a * T, (a + 1) * T)
    sub = lambda j: slice(j * C, (j + 1) * C)
    stack = lambda blocks: jnp.concatenate([blocks[j] for j in range(nsub)], axis=0)

    wg_hi, wg_lo = _split(wg_ref[...])
    logg = {}
    for a in tiles:
        z = _dot(pg_ref[rows(a), :], wg_hi) + _dot(pg_ref[rows(a), :], wg_lo) + bg_ref[...]
        logg[a] = (jnp.minimum(z, 0.0) - jnp.log(1.0 + jnp.exp(-jnp.abs(z)))) * (1.0 / GATE_TEMP)
    cums = {}
    for a in tiles:
        g_hi, g_lo = _split(logg[a])
        cums[a] = _dot(tri, g_hi) + _dot(tri, g_lo)
    qin, qoff, kin, kend, vb, tot, ptile = {}, {}, {}, {}, {}, {}, {}
    for a in tiles:
        cum = cums[a]
        q = pb_ref[rows(a), 0:W].astype(F32) * (HEAD_D ** -0.5)
        k = pb_ref[rows(a), W:2 * W].astype(F32)
        last_row = (lambda j: j * C) if rev else (lambda j: j * C + C - 1)
        off, run = {}, jnp.zeros((1, W), F32)
        for j in subs:
            tot[(a, j)] = cum[last_row(j):last_row(j) + 1]
            off[j] = run
            run = run + tot[(a, j)]
        ptile[a] = jnp.exp(run)
        e = jnp.exp(cum)
        qin[a] = (q * e).astype(BF16)
        qoff[a] = (q * (e * jnp.exp(stack({j: jnp.broadcast_to(off[j], (C, W)) for j in subs})))).astype(BF16)
        kin[a] = (k * jnp.exp(-cum)).astype(BF16)
        kend[a] = (k * jnp.exp(stack({j: jnp.broadcast_to(tot[(a, j)], (C, W)) for j in subs}) - cum)).astype(BF16)
        vb[a] = pb_ref[rows(a), 2 * W:3 * W]
    sc = {}
    for a in tiles:
        qst = jnp.concatenate([jnp.where(head_t == h, qin[a], jnp.zeros_like(qin[a])) for h in range(H)], axis=0)
        sc[a] = (_dot_nt(qst, kin[a]) * mask4).astype(BF16)
    ost = {a: _dot(sc[a], vb[a]) for a in tiles}
    kvt = {(a, j): jnp.where(bd, _dot_tn(vb[a][sub(j)], kend[a][sub(j)]), 0.0) for a in tiles for j in subs}
    cross, rend = {}, {}
    for a in tiles:
        r = None
        for j in subs:
            if r is not None:
                cross[(a, j)] = _dot_nt(qin[a][sub(j)], r.astype(BF16))
                r = r * jnp.exp(tot[(a, j)]) + kvt[(a, j)]
            else:
                cross[(a, j)] = jnp.zeros((C, W), F32)
                r = kvt[(a, j)]
        rend[a] = r
    st = st_scr[...]
    inter = {}
    for a in tiles:
        inter[a] = _dot_nt(qoff[a], st.astype(BF16))
        st = st * ptile[a] + rend[a]
    st_scr[...] = st

    for a in tiles:
        o = inter[a] + jnp.concatenate([cross[(a, j)] for j in range(nsub)], axis=0)
        for h in range(H):
            o = o + jnp.where(head_t == h, ost[a][h * T:(h + 1) * T], 0.0)
        if finish:
            o = o + oprev_ref[rows(a), :]
            ms = _dot_x2(o * o, ones_ref[...]) * (1.0 / HEAD_D)
            o = o * lax.rsqrt(ms + EPS) * gn_ref[...]
            r = pb_ref[rows(a), 3 * W:4 * W].astype(F32)
            o_ref[rows(a), :] = (o * (r * _sigmoid(r))).astype(o_ref.dtype)
        else:
            o_ref[rows(a), :] = o

    @pl.when(i == pl.num_programs(1) - 1)
    def _():
        sfin_ref[...] = st_scr[...]


def _gla_decay_matrices():
    T, C = GLA_TILE, GLA_CHUNK
    t = jnp.arange(T)[:, None]
    s = jnp.arange(T)[None, :]
    same = (t // C) == (s // C)
    return jnp.stack([same & (s <= t), same & (s >= t)]).astype(BF16)


def _gla_pass(l, d, pb3, pg3, wg, bg, s0, ones_bd, dec, oprev, gn, tt):
    B, n, _ = pb3.shape
    nt = n // tt
    rev = d == 1
    finish = oprev is not None
    W = GROUP_W

    def tok(b, i):
        return (b, (nt - 1 - i) if rev else i, 0)

    in_specs = [
        pl.BlockSpec((None, tt, 4 * W), tok),
        pl.BlockSpec((None, tt, LANES), tok),
        pl.BlockSpec((None, None, LANES, W), lambda b, i: (l, d, 0, 0)),
        pl.BlockSpec((None, None, 1, W), lambda b, i: (l, d, 0, 0)),
        pl.BlockSpec((None, W, W), lambda b, i: (b, 0, 0)),
        pl.BlockSpec((W, W), lambda b, i: (0, 0)),
        pl.BlockSpec((None, GLA_TILE, GLA_TILE), lambda b, i: (d, 0, 0)),
    ]
    args = [pb3, pg3, wg, bg, s0, ones_bd, dec]
    if finish:
        in_specs += [pl.BlockSpec((None, tt, W), tok), _vec_spec(l, W)]
        args += [oprev, gn]
    return pl.pallas_call(
        functools.partial(_gla_kernel, rev=rev, finish=finish),
        grid=(B, nt),
        in_specs=in_specs,
        out_specs=[pl.BlockSpec((None, tt, W), tok), pl.BlockSpec((None, W, W), lambda b, i: (b, 0, 0))],
        out_shape=[jax.ShapeDtypeStruct((B, n, W), BF16 if finish else F32),
                   jax.ShapeDtypeStruct((B, W, W), F32)],
        scratch_shapes=[pltpu.VMEM((W, W), F32)],
        compiler_params=_cparams("arbitrary", "arbitrary"),
        name="gla_bwd" if rev else "gla_fwd",
    )(*args)


def _gla(l, pb_c, pg_c, pb_x, pg_x, wg, bg, gn, ones_bd, tt_c, tt_x):
    B = pb_x.shape[0]
    zero = jnp.zeros((B, GROUP_W, GROUP_W), F32)
    dec = _gla_decay_matrices()
    ofc, sfc = _gla_pass(l, 0, pb_c, pg_c, wg, bg, zero, ones_bd, dec, None, None, tt_c)
    ofx, _ = _gla_pass(l, 0, pb_x, pg_x, wg, bg, sfc, ones_bd, dec, None, None, tt_x)
    ob_c, sbc = _gla_pass(l, 1, pb_c, pg_c, wg, bg, zero, ones_bd, dec, ofc, gn, tt_c)
    ob_x, _ = _gla_pass(l, 1, pb_x, pg_x, wg, bg, sbc, ones_bd, dec, ofx, gn, tt_x)
    return ob_c, ob_x


def _s5_matrices(a_re, a_im, log_dt, b_re, b_im, c_re, c_im, d_skip):
    T = S5_T
    G, P = a_re.shape[1:]
    I = b_re.shape[-1]
    lam = lax.complex(a_re.astype(F32), a_im.astype(F32))
    ldt = lam * jnp.exp(log_dt.astype(F32))[..., None]
    lam_bar = jnp.exp(ldt)
    b_bar = ((lam_bar - 1.0) / lam)[..., None] * lax.complex(b_re.astype(F32), b_im.astype(F32))
    cmat = lax.complex(c_re.astype(F32), c_im.astype(F32))
    steps = jnp.arange(T + 1, dtype=F32)
    pw = jnp.exp(ldt[..., None] * steps)
    taps = jnp.einsum('dgop,dgpk,dgpi->dgiko', cmat, pw[..., :T], b_bar).real
    taps = taps.at[0, :, :, 0, :].add(jnp.eye(I, dtype=F32)[None] * d_skip.astype(F32)[:, :, None])
    row = T * I
    zeros = jnp.zeros((G, I, row), F32)
    fwd = jnp.concatenate([zeros, taps[0].reshape(G, I, row)], axis=-1)
    bwd = jnp.concatenate([jnp.flip(taps[1], axis=2).reshape(G, I, row), zeros], axis=-1)
    m = jnp.stack([fwd[..., (T - s) * I:(T - s) * I + row] + bwd[..., (T - 1 - s) * I:(T - 1 - s) * I + row]
                   for s in range(T)], axis=1).reshape(G, row, row)

    ar = jnp.arange(T)
    pf = pw[0][..., T - 1 - ar]
    pb = pw[1][..., ar]
    bf = jnp.einsum('gps,gpi->gsip', pf, b_bar[0]).reshape(G, T * I, P)
    bb = jnp.einsum('gps,gpi->gsip', pb, b_bar[1]).reshape(G, T * I, P)
    bmat = jnp.concatenate([bf.real, bf.imag, bf.imag, bf.real, bb.real, bb.imag, bb.imag, bb.real], axis=-1)

    cf = jnp.einsum('gop,gpt->gpto', cmat[0], pw[0][..., 1 + ar]).reshape(G, P, T * I)
    cb = jnp.einsum('gop,gpt->gpto', cmat[1], pw[1][..., T - ar]).reshape(G, P, T * I)
    w = jnp.concatenate([m, cf.real, -cf.imag, cb.real, -cb.imag], axis=1)

    a = pw[..., T]
    acoef = jnp.stack([jnp.concatenate([a[0].real, a[0].real], axis=-1), jnp.concatenate([-a[0].imag, a[0].imag], axis=-1),
                       jnp.concatenate([a[1].real, a[1].real], axis=-1), jnp.concatenate([-a[1].imag, a[1].imag], axis=-1)])
    return bmat.astype(BF16), w.astype(BF16), acoef


def _s5_scan_kernel(xf_ref, xb_ref, bm_ref, a_ref, h0_ref, hf_ref, hb_ref, hfin_ref, st_scr, sf_scr, sb_scr, hf_scr, hb_scr):
    j = pl.program_id(1)
    G, tc, _ = xf_ref.shape
    pitch = tc + S5_PITCH_PAD
    half = hf_ref.shape[-1]

    @pl.when(j == 0)
    def _():
        st_scr[...] = h0_ref[...]

    for g in range(G):
        sf = _dot(xf_ref[g], bm_ref[g, :, 0:2 * half])
        sb = _dot(xb_ref[g], bm_ref[g, :, 2 * half:4 * half])
        for k in range(2):
            sf_scr[k, g * pitch:g * pitch + tc, :] = sf[:, k * half:(k + 1) * half]
            sb_scr[k, g * pitch:g * pitch + tc, :] = sb[:, k * half:(k + 1) * half]

    a1f, a2f, a1b, a2b = a_ref[0], a_ref[1], a_ref[2], a_ref[3]
    chunk = lambda i: pl.ds(i, G, stride=pitch)

    def body(i, hs):
        h1f, h2f, h1b, h2b = hs
        ib = tc - 1 - i
        hf_scr[chunk(i), :] = h1f
        hb_scr[chunk(ib), :] = h1b
        return (h1f * a1f + h2f * a2f + sf_scr[0, chunk(i), :], h2f * a1f - h1f * a2f + sf_scr[1, chunk(i), :],
                h1b * a1b + h2b * a2b + sb_scr[0, chunk(ib), :], h2b * a1b - h1b * a2b + sb_scr[1, chunk(ib), :])

    hs = lax.fori_loop(0, tc, body, (st_scr[0], st_scr[1], st_scr[2], st_scr[3]))
    for k in range(4):
        st_scr[k] = hs[k]
    for g in range(G):
        hf_ref[g] = hf_scr[g * pitch:g * pitch + tc, :]
        hb_ref[g] = hb_scr[g * pitch:g * pitch + tc, :]

    @pl.when(j == pl.num_programs(1) - 1)
    def _():
        hfin_ref[...] = st_scr[...]


def _s5_scan(l, xg, bmat, acoef, h0, B, tc):
    G, R, K = xg.shape
    nt = R // B // tc
    half = bmat.shape[-1] // 4
    fwd = lambda b, j: (0, b * nt + j, 0)
    bwd = lambda b, j: (0, b * nt + nt - 1 - j, 0)
    st_spec = pl.BlockSpec((None, 4, G, half), lambda b, j: (b, 0, 0, 0))
    pitch = tc + S5_PITCH_PAD
    return pl.pallas_call(
        _s5_scan_kernel,
        grid=(B, nt),
        in_specs=[pl.BlockSpec((G, tc, K), fwd),
                  pl.BlockSpec((G, tc, K), bwd),
                  pl.BlockSpec((None, G, K, 4 * half), lambda b, j: (l, 0, 0, 0)),
                  pl.BlockSpec((None, 4, G, half), lambda b, j: (l, 0, 0, 0)),
                  st_spec],
        out_specs=[pl.BlockSpec((G, tc, half), fwd), pl.BlockSpec((G, tc, half), bwd), st_spec],
        out_shape=[jax.ShapeDtypeStruct((G, R, half), F32), jax.ShapeDtypeStruct((G, R, half), F32),
                   jax.ShapeDtypeStruct((B, 4, G, half), F32)],
        scratch_shapes=[pltpu.VMEM((4, G, half), F32),
                        pltpu.VMEM((2, G * pitch, half), F32), pltpu.VMEM((2, G * pitch, half), F32),
                        pltpu.VMEM((G * pitch, half), F32), pltpu.VMEM((G * pitch, half), F32)],
        compiler_params=_cparams("arbitrary", "arbitrary"),
        name="s5_scan",
    )(xg, xg, bmat, acoef, h0)


def _s5_gather_kernel(x_ref, o_ref, xs_scr, xt_scr):
    G, tc, _ = o_ref.shape
    nh = x_ref.shape[1] // LANES
    gh = G // nh
    for h in range(nh):
        xs_scr[h] = x_ref[:, h * LANES:(h + 1) * LANES].astype(F32)
    for t in range(S5_T):
        for h in range(nh):
            rt = xs_scr[h, pl.ds(t, tc, stride=S5_T), :].T
            for g in range(gh):
                xt_scr[h * gh + g, t * S5_IN:(t + 1) * S5_IN, :] = rt[g * S5_IN:(g + 1) * S5_IN, :]
    for g in range(G):
        o_ref[g] = xt_scr[g].T.astype(o_ref.dtype)


def _s5_gather(x2, tc):
    N, W = x2.shape
    G = W // S5_IN
    R = N // S5_T
    return pl.pallas_call(
        _s5_gather_kernel,
        grid=(R // tc,),
        in_specs=[pl.BlockSpec((tc * S5_T, W), lambda i: (i, 0))],
        out_specs=pl.BlockSpec((G, tc, S5_T * S5_IN), lambda i: (0, i, 0)),
        out_shape=jax.ShapeDtypeStruct((G, R, S5_T * S5_IN), BF16),
        scratch_shapes=[pltpu.VMEM((W // LANES, tc * S5_T, LANES), F32), pltpu.VMEM((G, S5_T * S5_IN, tc), F32)],
        compiler_params=_cparams("parallel"),
        name="s5_gather",
    )(x2)


def _s5_scatter_kernel(x_ref, hf_ref, hb_ref, wy_ref, w_ref, b_ref, o_ref, yt_scr, tok_scr):
    G, tc, k = x_ref.shape
    kh = hf_ref.shape[-1]
    nh = tok_scr.shape[0]
    gh = G // nh
    for g in range(G):
        y = _dot(x_ref[g], wy_ref[g, 0:k, :])
        y = y + _dot(hf_ref[g].astype(BF16), wy_ref[g, k:k + kh, :])
        y = y + _dot(hb_ref[g].astype(BF16), wy_ref[g, k + kh:, :])
        yt = y.T
        for t in range(S5_T):
            yt_scr[t, g // gh, (g % gh) * S5_IN:(g % gh + 1) * S5_IN, :] = yt[t * S5_IN:(t + 1) * S5_IN, :]
    for t in range(S5_T):
        for h in range(nh):
            tok_scr[h, pl.ds(t, tc, stride=S5_T), :] = yt_scr[t, h].T
    y = _gelu(jnp.concatenate([tok_scr[h] for h in range(nh)], axis=1))
    gate = _sigmoid(_dot(y.astype(BF16), w_ref[...]) + b_ref[...])
    o_ref[...] = (y * gate).astype(o_ref.dtype)


def _s5_scatter_finish(l, xg, hf, hb, wy, w_glu, b_glu, tc):
    G, R, K = xg.shape
    kh = hf.shape[-1]
    W = G * S5_IN
    grp = lambda width: pl.BlockSpec((G, tc, width), lambda i: (0, i, 0))
    return pl.pallas_call(
        _s5_scatter_kernel,
        grid=(R // tc,),
        in_specs=[grp(K), grp(kh), grp(kh),
                  pl.BlockSpec((None, G, K + 2 * kh, K), lambda i: (l, 0, 0, 0)),
                  pl.BlockSpec((None, W, W), lambda i: (l, 0, 0)),
                  _vec_spec(l, W)],
        out_specs=pl.BlockSpec((tc * S5_T, W), lambda i: (i, 0)),
        out_shape=jax.ShapeDtypeStruct((R * S5_T, W), BF16),
        scratch_shapes=[pltpu.VMEM((S5_T, W // LANES, LANES, tc), F32), pltpu.VMEM((W // LANES, tc * S5_T, LANES), F32)],
        compiler_params=_cparams("parallel"),
        name="s5_scatter",
    )(xg, hf, hb, wy, w_glu, b_glu)


def _s5(l, pc_c, pc_x, mats, w_glu, b_glu, need_ctx):
    bmat, w, acoef = mats
    B, n, W = pc_x.shape
    nctx = pc_c.shape[1]
    G = W // S5_IN
    tc_c = min(S5_TILE, B * nctx // S5_T)
    tc_x = min(S5_TILE, n // S5_T)
    xg_c = _s5_gather(pc_c.reshape(B * nctx, W), tc_c)
    xg_x = _s5_gather(pc_x.reshape(B * n, W), tc_x)
    h0 = jnp.zeros((B, 4, G, acoef.shape[-1]), F32)
    hf_c, hb_c, h1 = _s5_scan(l, xg_c, bmat, acoef, h0, B, nctx // S5_T)
    hf_x, hb_x, _ = _s5_scan(l, xg_x, bmat, acoef, h1, B, tc_x)
    oc_x = _s5_scatter_finish(l, xg_x, hf_x, hb_x, w, w_glu, b_glu, tc_x)
    oc_c = _s5_scatter_finish(l, xg_c, hf_c, hb_c, w, w_glu, b_glu, tc_c) if need_ctx else None
    return oc_c, oc_x


def _attn_kernel(q_ref, k_ref, v_ref, o_ref, m_scr, l_scr, acc_scr):
    kv = pl.program_id(2)
    HP = MLA_HEAD_PAD
    tk = k_ref.shape[0]

    @pl.when(kv == 0)
    def _():
        m_scr[...] = jnp.full(m_scr.shape, -jnp.inf, F32)
        l_scr[...] = jnp.zeros(l_scr.shape, F32)
        acc_scr[...] = jnp.zeros(acc_scr.shape, F32)

    def scores(h):
        lanes = slice(h * HP, (h + 1) * HP)
        return _dot_nt(q_ref[:, lanes], k_ref[:, lanes])

    s_next = scores(0)
    for h in range(MLA_HEADS):
        lanes = slice(h * HP, (h + 1) * HP)
        s = s_next
        if h + 1 < MLA_HEADS:
            s_next = scores(h + 1)
        m_prev = m_scr[h]
        m_new = jnp.maximum(m_prev, jnp.max(s, axis=1, keepdims=True))
        alpha = jnp.exp2(m_prev - m_new)
        p = jnp.exp2(s - m_new[:, 0:1])
        lp = p[:, 0:LANES]
        for c in range(1, tk // LANES):
            lp = lp + p[:, c * LANES:(c + 1) * LANES]
        l_scr[h] = alpha * l_scr[h] + lp
        acc_scr[h] = alpha * acc_scr[h] + _dot(p.astype(BF16), v_ref[:, lanes])
        m_scr[h] = m_new

    @pl.when(kv == pl.num_programs(2) - 1)
    def _():
        low = lax.broadcasted_iota(jnp.int32, acc_scr.shape[1:], 1) < MLA_V
        norm = lambda h: acc_scr[h] * (1.0 / jnp.sum(l_scr[h], axis=1, keepdims=True))
        outs = [jnp.where(low, norm(h), norm(h + 1)) for h in range(0, MLA_HEADS, 2)]
        o_ref[...] = jnp.concatenate(outs, axis=1).astype(o_ref.dtype)


def _attention(q3, k3, v3, tq, tk, k0, nk):
    B, nq, HP = q3.shape
    kb = k0 // tk
    return pl.pallas_call(
        _attn_kernel,
        grid=(B, nq // tq, nk // tk),
        in_specs=[pl.BlockSpec((None, tq, HP), lambda b, i, j: (b, i, 0)),
                  pl.BlockSpec((None, tk, HP), lambda b, i, j: (b, kb + j, 0)),
                  pl.BlockSpec((None, tk, HP), lambda b, i, j: (b, kb + j, 0))],
        out_specs=pl.BlockSpec((None, tq, MLA_HEADS * MLA_V), lambda b, i, j: (b, i, 0)),
        out_shape=jax.ShapeDtypeStruct((B, nq, MLA_HEADS * MLA_V), BF16),
        scratch_shapes=[pltpu.VMEM((MLA_HEADS, tq, LANES), F32),
                        pltpu.VMEM((MLA_HEADS, tq, LANES), F32),
                        pltpu.VMEM((MLA_HEADS, tq, MLA_HEAD_PAD), F32)],
        compiler_params=_cparams("parallel", "parallel", "arbitrary"),
        name="attention",
    )(q3, k3, v3)


def _outproj_kernel(oa, ob, oc, od, w_ref, x_ref, gpost_ref, gt_ref, gpre_ref, sc_ref, sh_ref, xo_ref, h_ref):
    W = GROUP_W
    mix = _dot(oa[...], w_ref[0:W, :])
    mix = mix + _dot(ob[...], w_ref[W:2 * W, :])
    mix = mix + _dot(oc[...], w_ref[2 * W:3 * W, :])
    mix = mix + _dot(od[...], w_ref[3 * W:4 * W, :])
    x = x_ref[...] + gt_ref[...] * (_rms(mix) * gpost_ref[...])
    xo_ref[...] = x
    h_ref[...] = (_rms(x) * gpre_ref[...] * (1.0 + sc_ref[...]) + sh_ref[...]).astype(h_ref.dtype)


def _outproj(l, parts, w_out, x2, mod4, brow, g_post, g_pre_ffn, tm):
    R, D = x2.shape
    W = GROUP_W
    part_spec = pl.BlockSpec((tm, W), lambda i: (i, 0))
    row_spec = pl.BlockSpec((tm, D), lambda i: (i, 0))
    return pl.pallas_call(
        _outproj_kernel,
        grid=(R // tm,),
        in_specs=[part_spec] * 4 + [
            pl.BlockSpec((None, 4 * W, D), lambda i: (l, 0, 0)),
            row_spec,
            _vec_spec(l, D),
            _mod_spec(l, 2, D, brow),
            _vec_spec(l, D),
            _mod_spec(l, 4, D, brow),
            _mod_spec(l, 3, D, brow),
        ],
        out_specs=[row_spec, row_spec],
        out_shape=[jax.ShapeDtypeStruct((R, D), F32), jax.ShapeDtypeStruct((R, D), BF16)],
        compiler_params=_cparams("parallel"),
        name="outproj",
    )(*parts, w_out, x2, g_post, mod4, g_pre_ffn, mod4, mod4)


FFN_HALO = 16
FFN_COLS = 256


def _ffn_kernel(hp_ref, h_ref, hn_ref, wup_ref, cw_ref, cb_ref, wdn_ref, x_ref, gpost_ref, gt_ref, o_ref, y_scr):
    i = pl.program_id(1)
    tm = h_ref.shape[0]
    dff = wdn_ref.shape[0]
    use = FFN_HALO // 2
    rows = tm + 2 * use
    prev = jnp.where(i == 0, jnp.zeros_like(hp_ref[...]), hp_ref[...])[FFN_HALO - use:]
    nxt = jnp.where(i == pl.num_programs(1) - 1, jnp.zeros_like(hn_ref[...]), hn_ref[...])[:use]
    hb = jnp.concatenate([prev, h_ref[...], nxt], axis=0)

    def conv(z, cols):
        w = cw_ref[:, cols]
        out = cb_ref[:, cols] + w[0:1] * pltpu.roll(z, 1, axis=0) + w[1:2] * z + w[2:3] * pltpu.roll(z, rows - 1, axis=0)
        return out[use:use + tm]

    def cols(j):
        return slice(j * FFN_COLS, (j + 1) * FFN_COLS), slice(dff + j * FFN_COLS, dff + (j + 1) * FFN_COLS)

    def up(j):
        ca, cg = cols(j)
        return _dot(hb, wup_ref[:, ca]), _dot(hb, wup_ref[:, cg])

    nchunks = dff // FFN_COLS
    z_next = up(0)
    for j in range(nchunks):
        ca, cg = cols(j)
        za, zg = z_next
        if j + 1 < nchunks:
            z_next = up(j + 1)
        y_scr[:, ca] = _gelu_gate(conv(za, ca), conv(zg, cg)).astype(BF16)
    acc = _dot(y_scr[...], wdn_ref[...])
    o_ref[...] = x_ref[...] + gt_ref[...] * (_rms(acc) * gpost_ref[...])


def _ffn(l, h3, x3, w_up, conv_w, conv_b, w_down, mod4, brow, g_post, tm):
    B, n, D = x3.shape
    nt = n // tm
    hb = tm // FFN_HALO
    nh = n // FFN_HALO
    dff = w_down.shape[1]
    once = pl.Buffered(1)
    return pl.pallas_call(
        _ffn_kernel,
        grid=(B, nt),
        in_specs=[
            pl.BlockSpec((None, FFN_HALO, D), lambda b, i: (b, jnp.maximum(i * hb - 1, 0), 0)),
            pl.BlockSpec((None, tm, D), lambda b, i: (b, i, 0)),
            pl.BlockSpec((None, FFN_HALO, D), lambda b, i: (b, jnp.minimum((i + 1) * hb, nh - 1), 0)),
            pl.BlockSpec((None, D, 2 * dff), lambda b, i: (l, 0, 0), pipeline_mode=once),
            pl.BlockSpec((None, 3, 2 * dff), lambda b, i: (l, 0, 0)),
            pl.BlockSpec((None, 1, 2 * dff), lambda b, i: (l, 0, 0)),
            pl.BlockSpec((None, dff, D), lambda b, i: (l, 0, 0), pipeline_mode=once),
            pl.BlockSpec((None, tm, D), lambda b, i: (b, i, 0)),
            _vec_spec(l, D),
            _mod_spec(l, 5, D, brow),
        ],
        out_specs=pl.BlockSpec((None, tm, D), lambda b, i: (b, i, 0)),
        out_shape=jax.ShapeDtypeStruct((B, n, D), F32),
        scratch_shapes=[pltpu.VMEM((tm, dff), BF16)],
        compiler_params=_cparams("parallel", "parallel"),
        name="conv_ffn",
    )(h3, h3, h3, w_up, conv_w, conv_b, w_down, x3, g_post, mod4)


def _prep_w_in(w_in):
    w_in = w_in.astype(BF16)
    z = lambda n: jnp.zeros(w_in.shape[:-1] + (n,), w_in.dtype)
    a = w_in[..., 0:512]
    b = w_in[..., 512:1536]
    gl = w_in[..., 1536:1568]
    c = w_in[..., 1568:1824]
    cq = w_in[..., 1824:2048]
    ckv_kr = w_in[..., 2048:2176]
    return jnp.concatenate([a, b, c, cq, z(32), ckv_kr, gl, z(96)], axis=-1)


def _rope_swap(t):
    q = MLA_ROPE // 4
    return jnp.concatenate([t[..., q:2 * q], t[..., 0:q], t[..., 3 * q:4 * q], t[..., 2 * q:3 * q]], axis=-1)


def _prep_mla(w_uq, w_ukv):
    L = w_uq.shape[0]
    H, NP, RP, HP = MLA_HEADS, MLA_NOPE, MLA_ROPE, MLA_HEAD_PAD
    wq = w_uq.reshape(L, MLA_Q_RANK, H, NP + RP)
    zq = jnp.zeros((L, MLA_Q_RANK, H, HP - NP - RP), w_uq.dtype)
    znope = jnp.zeros((L, MLA_Q_RANK, H, NP), w_uq.dtype)
    q_main = jnp.concatenate([wq, zq], axis=-1)
    q_swap = jnp.concatenate([znope, _rope_swap(wq[..., NP:]), zq], axis=-1)
    padq = lambda w: jnp.pad(w.reshape(L, MLA_Q_RANK, H * HP), ((0, 0), (0, MLA_Q_LANES - MLA_Q_RANK), (0, 0)))

    wkv = w_ukv.reshape(L, MLA_KV_RANK, H, NP + MLA_V)
    zk = jnp.zeros((L, MLA_KV_RANK, H, HP - NP), w_ukv.dtype)
    k_lat = jnp.concatenate([wkv[..., :NP], zk], axis=-1)
    eye = jnp.eye(RP, dtype=w_ukv.dtype)
    place = lambda e: jnp.broadcast_to(
        jnp.concatenate([jnp.zeros((RP, NP), e.dtype), e, jnp.zeros((RP, HP - NP - RP), e.dtype)], axis=-1)[None, :, None, :],
        (L, RP, H, HP))
    k_main = jnp.concatenate([k_lat, place(eye)], axis=1)
    k_swap = jnp.concatenate([jnp.zeros_like(k_lat), place(_rope_swap(eye))], axis=1)
    zv = jnp.zeros((L, MLA_KV_RANK, H, HP - MLA_V), w_ukv.dtype)
    odd = (jnp.arange(H) % 2 == 1)[None, None, :, None]
    v_lat = jnp.where(odd, jnp.concatenate([zv, wkv[..., NP:]], axis=-1), jnp.concatenate([wkv[..., NP:], zv], axis=-1))
    v_main = jnp.concatenate([v_lat, jnp.zeros((L, RP, H, HP), w_ukv.dtype)], axis=1)
    flat = lambda w: w.reshape(L, w.shape[1], H * HP).astype(BF16)
    return (padq(q_main).astype(BF16), padq(q_swap).astype(BF16), flat(k_main), flat(k_swap), flat(v_main))


def _rope_tables(n):
    rows = n // GRID_W
    nf = MLA_ROPE // 4
    inv = ROPE_BASE ** (-jnp.arange(nf, dtype=F32) / nf)
    ar = jnp.arange(rows, dtype=F32)[:, None] * inv[None, :]
    ac = jnp.arange(GRID_W, dtype=F32)[:, None] * inv[None, :]
    by_row = lambda t: jnp.repeat(t, GRID_W, axis=0)
    by_col = lambda t: jnp.tile(t, (rows, 1))
    cr, sr, cc, sn = by_row(jnp.cos(ar)), by_row(jnp.sin(ar)), by_col(jnp.cos(ac)), by_col(jnp.sin(ac))
    one = jnp.ones((n, MLA_NOPE), F32)
    zero = jnp.zeros((n, MLA_HEAD_PAD - MLA_NOPE - MLA_ROPE), F32)
    cos = jnp.concatenate([one, cr, cr, cc, cc, zero], axis=1)
    sin = jnp.concatenate([0.0 * one, -sr, sr, -sn, sn, zero], axis=1)
    return cos, sin


def _pick_tile(n, want):
    t = min(n, want)
    while n % t:
        t //= 2
    return t


def kernel(x, c, ctx, c_ctx, w_mod, b_mod, g_pre_mix, g_post_mix, g_pre_ffn, g_post_ffn, w_in,
           sgu_norm, sgu_w, sgu_b, gla_w_gate, gla_b_gate, gla_norm,
           s5_a_re, s5_a_im, s5_log_dt, s5_b_re, s5_b_im, s5_c_re, s5_c_im, s5_d, s5_w_glu, s5_b_glu,
           mla_q_norm, mla_w_uq, mla_kv_norm, mla_w_ukv, w_out,
           ffn_w_up, ffn_conv_w, ffn_conv_b, ffn_w_down):
    B, n, D = x.shape
    nctx = ctx.shape[1]
    L = w_mod.shape[0]
    W = GROUP_W
    assert B < 8 and n % TOKEN_TILE == 0 and nctx % GLA_TILE == 0 and n % GRID_W == 0

    c8 = jnp.concatenate([c, c_ctx[None, :], jnp.zeros((8 - B - 1, D), F32)], axis=0)
    mod4 = _modulation(c8, w_mod, b_mod).reshape(L, 8, 1, 6 * D)
    vec = lambda p: p.reshape(L, 1, -1).astype(F32)
    g_pre_mix, g_post_mix, g_pre_ffn, g_post_ffn = map(vec, (g_pre_mix, g_post_mix, g_pre_ffn, g_post_ffn))
    w_in_p = _prep_w_in(w_in)
    sgu_gn = vec(sgu_norm)
    sgu_w_st = sgu_w.reshape(L, -1, MLP_CHUNK).astype(BF16)
    sgu_bias = jnp.repeat(jnp.swapaxes(sgu_b, 1, 2), HEAD_D, axis=2).astype(F32)
    ones_bd = jnp.kron(jnp.eye(W // HEAD_D, dtype=F32), jnp.ones((HEAD_D, HEAD_D), F32)).astype(BF16)
    gla_wg = jnp.zeros((L, 2, LANES, W), F32)
    gla_wg = gla_wg.at[:, 0, 0:GATE_RANK].set(gla_w_gate[:, 0]).at[:, 1, GATE_RANK:2 * GATE_RANK].set(gla_w_gate[:, 1])
    gla_bg = gla_b_gate.reshape(L, 2, 1, W).astype(F32)
    gla_gn = vec(gla_norm)
    s5_wglu = s5_w_glu.astype(BF16)
    s5_bglu = vec(s5_b_glu)
    mla_qn = jnp.pad(mla_q_norm, ((0, 0), (0, MLA_Q_LANES - MLA_Q_RANK))).reshape(L, 1, MLA_Q_LANES).astype(F32)
    mla_kvn = jnp.pad(mla_kv_norm, ((0, 0), (0, MLA_KV_LANES - MLA_KV_RANK))).reshape(L, 1, MLA_KV_LANES).astype(F32)
    mla_wts = _prep_mla(mla_w_uq, mla_w_ukv)
    tables = _rope_tables(n)
    w_out_b = w_out.astype(BF16)
    w_up_b = ffn_w_up.astype(BF16)
    w_down_b = ffn_w_down.astype(BF16)
    dff = ffn_w_down.shape[1]
    half_gate = jnp.concatenate([jnp.ones((dff,), F32), jnp.full((dff,), 0.5, F32)])
    conv_w = ffn_conv_w.astype(F32) * half_gate
    conv_b = (ffn_conv_b.astype(F32) * half_gate).reshape(L, 1, -1)

    s5_mats = jax.vmap(_s5_matrices)(s5_a_re, s5_a_im, s5_log_dt, s5_b_re, s5_b_im, s5_c_re, s5_c_im, s5_d)

    tm_x = _pick_tile(n, TOKEN_TILE)
    tm_c = _pick_tile(nctx, TOKEN_TILE)
    tpb_x = n // tm_x
    brow_x = lambda i: i // tpb_x
    brow_c = lambda *g: B
    brow_x2 = lambda b, i: b

    xs = x.reshape(B * n, D)
    cs = ctx.reshape(B * nctx, D)

    for l in range(L):
        need_ctx = l < L - 1
        sgu = (sgu_gn, sgu_w_st, sgu_bias, ones_bd)
        wq, wqs, wk, wks, wv = mla_wts
        nk = n + nctx
        kv0 = jnp.zeros((B, nk, MLA_HEADS * MLA_HEAD_PAD), BF16)
        oa_x, pb_x, pc_x, pg_x, q_x, k_all, v_all = _inproj(
            l, xs, mod4, brow_x, g_pre_mix, w_in_p, sgu, (mla_qn, mla_kvn, mla_wts, tables), tm_x, n, B, 0, nk, (kv0, kv0))
        oa_c, pb_c, pc_c, pg_c, q_c, k_all, v_all = _inproj(
            l, cs, mod4, brow_c, g_pre_mix, w_in_p, sgu, (mla_qn, mla_kvn, (wq, None, wk, None, wv), None), tm_c, nctx, B, n, nk,
            (k_all, v_all))

        r3 = lambda t, m: t.reshape(B, m, t.shape[-1])
        ob_c, ob_x = _gla(l, r3(pb_c, nctx), r3(pg_c, nctx), r3(pb_x, n), r3(pg_x, n),
                          gla_wg, gla_bg, gla_gn, ones_bd, _pick_tile(nctx, GLA_TILE), _pick_tile(n, 4 * GLA_TILE))
        oc_c, oc_x = _s5(l, r3(pc_c, nctx), r3(pc_x, n), s5_mats, s5_wglu, s5_bglu, need_ctx)
        tk = next(t for t in ATTN_KV_TILES if nk % t == 0)
        od_x = _attention(r3(q_x, n), k_all, v_all, tm_x, tk, 0, nk).reshape(B * n, W)

        xs, hx = _outproj(l, (oa_x, ob_x.reshape(B * n, W), oc_x, od_x), w_out_b, xs, mod4, brow_x,
                          g_post_mix, g_pre_ffn, tm_x)
        xs = _ffn(l, hx.reshape(B, n, D), xs.reshape(B, n, D), w_up_b, conv_w, conv_b, w_down_b, mod4, brow_x2,
                  g_post_ffn, tm_x).reshape(B * n, D)

        if need_ctx:
            od_c = _attention(r3(q_c, nctx), k_all, v_all, tm_c, nctx, n, nctx).reshape(B * nctx, W)
            cs, hc = _outproj(l, (oa_c, ob_c.reshape(B * nctx, W), oc_c, od_c), w_out_b, cs, mod4, brow_c,
                              g_post_mix, g_pre_ffn, tm_c)
            cs = _ffn(l, hc.reshape(B, nctx, D), cs.reshape(B, nctx, D), w_up_b, conv_w, conv_b, w_down_b, mod4,
                      brow_c, g_post_ffn, tm_c).reshape(B * nctx, D)
    return xs.reshape(B, n, D)
```

```python
import functools

import jax
import jax.numpy as jnp
from jax import lax
from jax.experimental import pallas as pl
from jax.experimental.pallas import tpu as pltpu

F32 = jnp.float32
BF16 = jnp.bfloat16

EPS = 1e-6
GRID_W = 64
GROUP_W = 256
HEAD_D = 64
MLP_CHUNK = 128
GATE_RANK = 16
GATE_TEMP = 16.0
GLA_CHUNK = 64
GLA_TILE = 256
GLA_STEP_TILES = 8
S5_IN = 16
S5_T = 16
S5_TILE = 128
S5_PITCH_PAD = 8
MLA_HEADS = 4
MLA_NOPE = 64
MLA_ROPE = 32
MLA_V = 64
MLA_Q_RANK = 224
MLA_KV_RANK = 96
MLA_HEAD_PAD = 128
ROPE_BASE = 10000.0
LOG2E = 1.4426950408889634

LANES = 128
VMEM_LIMIT = 48 * 1024 * 1024
TOKEN_TILE = 512
MOD_COL_TILE = 1536
ATTN_KV_TILES = (2816, 1408, 768, 512, 256, 128)
MLA_Q_LANES = 256
MLA_KV_LANES = 128


def _cparams(*sem):
    return pltpu.CompilerParams(dimension_semantics=sem, vmem_limit_bytes=VMEM_LIMIT)


def _dot(a, b):
    return jnp.dot(a, b, preferred_element_type=F32)


def _dot_nt(a, b):
    return lax.dot_general(a, b, (((1,), (1,)), ((), ())), preferred_element_type=F32)


def _dot_tn(a, b):
    return lax.dot_general(a, b, (((0,), (0,)), ((), ())), preferred_element_type=F32)


def _split(a):
    hi = a.astype(BF16)
    lo = (a - hi.astype(F32)).astype(BF16)
    return hi, lo


def _dot_x2(a, b_bf16):
    hi, lo = _split(a)
    return _dot(hi, b_bf16) + _dot(lo, b_bf16)


def _dot_x3(a, b):
    ah, al = _split(a)
    bh, bl = _split(b)
    return _dot(ah, bh) + _dot(al, bh) + _dot(ah, bl)


def _rms(x):
    return x * lax.rsqrt(jnp.mean(x * x, axis=-1, keepdims=True) + EPS)


def _gelu(x):
    return 0.5 * x * (1.0 + jnp.tanh(0.7978845608028654 * (x + 0.044715 * (x * x * x))))


def _gelu_gate(a, half_g):
    u = a * (0.7978845608028654 + 0.035677408136300125 * (a * a))
    return (a * half_g) * (1.0 + jnp.tanh(u))


def _sigmoid(x):
    return 1.0 / (1.0 + jnp.exp(-x))


def _lane_group(shape, width):
    return lax.broadcasted_iota(jnp.int32, shape, len(shape) - 1) // width


def _mod_kernel(c_ref, w_ref, b_ref, o_ref):
    c = c_ref[...]
    s = c * _sigmoid(c)
    o_ref[...] = _dot_x3(s, w_ref[...]) + b_ref[...]


def _modulation(c8, w_mod, b_mod):
    L, D, W = w_mod.shape
    tn = MOD_COL_TILE
    return pl.pallas_call(
        _mod_kernel,
        grid=(L, W // tn),
        in_specs=[
            pl.BlockSpec((8, D), lambda l, j: (0, 0)),
            pl.BlockSpec((None, D, tn), lambda l, j: (l, 0, j)),
            pl.BlockSpec((None, 1, tn), lambda l, j: (l, 0, j)),
        ],
        out_specs=pl.BlockSpec((None, 8, tn), lambda l, j: (l, 0, j)),
        out_shape=jax.ShapeDtypeStruct((L, 8, W), F32),
        compiler_params=_cparams("arbitrary", "arbitrary"),
        name="modulation",
    )(c8, w_mod, b_mod.reshape(L, 1, W))


def _mod_spec(l, j, D, bfn):
    return pl.BlockSpec((None, None, 1, D), lambda *g: (l, bfn(*g), 0, j))


def _vec_spec(l, width):
    return pl.BlockSpec((None, 1, width), lambda *g: (l, 0, 0))


IN_SLABS = (("a", 0, 2 * GROUP_W), ("b", 512, 4 * GROUP_W), ("c", 1536, GROUP_W), ("d", 1792, MLA_Q_LANES + MLA_KV_LANES),
            ("g", 2176, LANES))
IN_PAD_COLS = 2304


def _sgu_tile(p, gn_ref, w_ref, b_ref, ones_ref, o_ref):
    tm = p.shape[0]
    g = _gelu(p)
    u = g[:, :GROUP_W]
    v = g[:, GROUP_W:]
    ms = _dot_x2(v * v, ones_ref[...]) * (1.0 / HEAD_D)
    vb = (v * lax.rsqrt(ms + EPS) * gn_ref[...]).astype(BF16)
    head = _lane_group((MLP_CHUNK, GROUP_W), HEAD_D)
    w = w_ref[...]
    for c in range(tm // MLP_CHUNK):
        rows = slice(c * MLP_CHUNK, (c + 1) * MLP_CHUNK)
        r = _dot(w, vb[rows])
        s = b_ref[...]
        for h in range(GROUP_W // HEAD_D):
            s = s + jnp.where(head == h, r[h * MLP_CHUNK:(h + 1) * MLP_CHUNK], 0.0)
        o_ref[rows, :] = (u[rows] * s).astype(o_ref.dtype)


def _mla_tile(pd, qn_ref, kvn_ref, wq_ref, wqs_ref, wk_ref, wks_ref, wv_ref, cos_ref, sin_ref, q_ref, k_ref, v_ref):
    cq = pd[:, 0:MLA_Q_LANES]
    ms = jnp.sum(cq * cq, axis=-1, keepdims=True) * (1.0 / MLA_Q_RANK)
    cqn = (cq * lax.rsqrt(ms + EPS) * qn_ref[...]).astype(BF16)
    ck = pd[:, MLA_Q_LANES:MLA_Q_LANES + MLA_KV_LANES]
    lane = lax.broadcasted_iota(jnp.int32, ck.shape, 1)
    is_lat = lane < MLA_KV_RANK
    ms = jnp.sum(jnp.where(is_lat, ck * ck, 0.0), axis=-1, keepdims=True) * (1.0 / MLA_KV_RANK)
    ckn = jnp.where(is_lat, ck * lax.rsqrt(ms + EPS) * kvn_ref[...], ck).astype(BF16)
    q = _dot(cqn, wq_ref[...])
    k = _dot(ckn, wk_ref[...])
    if cos_ref is not None:
        cos = jnp.concatenate([cos_ref[...]] * MLA_HEADS, axis=1)
        sin = jnp.concatenate([sin_ref[...]] * MLA_HEADS, axis=1)
        q = q * cos + _dot(cqn, wqs_ref[...]) * sin
        k = k * cos + _dot(ckn, wks_ref[...]) * sin
    q_ref[...] = (q * ((MLA_NOPE + MLA_ROPE) ** -0.5 * LOG2E)).astype(BF16)
    k_ref[...] = k.astype(BF16)
    v_ref[...] = _dot(ckn, wv_ref[...]).astype(BF16)


def _inproj_kernel(*refs, rope):
    x_ref, g_ref, sc_ref, sh_ref, w_ref, gn_ref, wsp_ref, bsp_ref, ones_ref, qn_ref, kvn_ref = refs[:11]
    if rope:
        wq_ref, wqs_ref, wk_ref, wks_ref, wv_ref, cos_ref, sin_ref = refs[11:18]
    else:
        wq_ref, wk_ref, wv_ref = refs[11:14]
        wqs_ref = wks_ref = cos_ref = sin_ref = None
    oa, ob, oc, og, q_ref, k_ref, v_ref = refs[-7:]
    h = _rms(x_ref[...]) * g_ref[...] * (1.0 + sc_ref[...]) + sh_ref[...]
    hb = h.astype(BF16)
    slab = {name: slice(off, off + width) for name, off, width in IN_SLABS}
    pa = _dot(hb, w_ref[:, slab["a"]])
    for name, o_ref in (("b", ob), ("c", oc), ("g", og)):
        o_ref[...] = _dot(hb, w_ref[:, slab[name]]).astype(o_ref.dtype)
    pd = _dot(hb, w_ref[:, slab["d"]])
    _mla_tile(pd, qn_ref, kvn_ref, wq_ref, wqs_ref, wk_ref, wks_ref, wv_ref, cos_ref, sin_ref, q_ref, k_ref, v_ref)
    _sgu_tile(pa, gn_ref, wsp_ref, bsp_ref, ones_ref, oa)


def _inproj(l, x2, mod4, brow, g_pre, w_in_p, sgu, mla, tm, n, B, row0, nk, kv_into):
    R, D = x2.shape
    gn, w_st, bias, ones_bd = sgu
    qn, kvn, (wq, wqs, wk, wks, wv), tables = mla
    rope = tables is not None
    H = GROUP_W // HEAD_D
    HP = MLA_HEADS * MLA_HEAD_PAD
    QL, KL = MLA_Q_LANES, MLA_KV_LANES
    npt = n // tm
    width = {name: w for name, _, w in IN_SLABS}
    wspec = lambda r: pl.BlockSpec((None, r, HP), lambda i: (l, 0, 0))
    in_specs = [
        pl.BlockSpec((tm, D), lambda i: (i, 0)),
        _vec_spec(l, D),
        _mod_spec(l, 1, D, brow),
        _mod_spec(l, 0, D, brow),
        pl.BlockSpec((None, D, IN_PAD_COLS), lambda i: (l, 0, 0)),
        _vec_spec(l, GROUP_W),
        pl.BlockSpec((None, H * MLP_CHUNK, MLP_CHUNK), lambda i: (l, 0, 0)),
        pl.BlockSpec((None, MLP_CHUNK, GROUP_W), lambda i: (l, 0, 0)),
        pl.BlockSpec((GROUP_W, GROUP_W), lambda i: (0, 0)),
        _vec_spec(l, QL),
        _vec_spec(l, KL),
    ]
    args = [x2, g_pre, mod4, mod4, w_in_p, gn, w_st, bias, ones_bd, qn, kvn]
    if rope:
        tspec = pl.BlockSpec((tm, MLA_HEAD_PAD), lambda i: (i % npt, 0))
        in_specs += [wspec(QL), wspec(QL), wspec(KL), wspec(KL), wspec(KL), tspec, tspec]
        args += [wq, wqs, wk, wks, wv, tables[0], tables[1]]
    else:
        in_specs += [wspec(QL), wspec(KL), wspec(KL)]
        args += [wq, wk, wv]
    aliases = {len(args): 5, len(args) + 1: 6}
    in_specs += [pl.BlockSpec(memory_space=pl.ANY)] * 2
    args += list(kv_into)
    row = lambda w: pl.BlockSpec((tm, w), lambda i: (i, 0))
    kv_spec = pl.BlockSpec((None, tm, HP), lambda i: (i // npt, row0 // tm + i % npt, 0))
    kv_shape = jax.ShapeDtypeStruct((B, nk, HP), BF16)
    return pl.pallas_call(
        functools.partial(_inproj_kernel, rope=rope),
        grid=(R // tm,),
        in_specs=in_specs,
        out_specs=[row(GROUP_W), row(width["b"]), row(width["c"]), row(width["g"]), row(HP), kv_spec, kv_spec],
        out_shape=[jax.ShapeDtypeStruct((R, w), BF16) for w in (GROUP_W, width["b"], width["c"], width["g"], HP)]
        + [kv_shape, kv_shape],
        input_output_aliases=aliases,
        compiler_params=_cparams("parallel"),
        name="inproj",
    )(*args)


def _gla_kernel(*refs, rev, finish):
    if finish:
        (pb_ref, pg_ref, wg_ref, bg_ref, s0_ref, ones_ref, dec_ref, oprev_ref, gn_ref, o_ref, sfin_ref, st_scr) = refs
    else:
        (pb_ref, pg_ref, wg_ref, bg_ref, s0_ref, ones_ref, dec_ref, o_ref, sfin_ref, st_scr) = refs
    i = pl.program_id(1)
    C, W, T = GLA_CHUNK, GROUP_W, GLA_TILE
    H = W // HEAD_D
    nsub = T // C
    ntile = pb_ref.shape[0] // T

    @pl.when(i == 0)
    def _():
        st_scr[...] = s0_ref[...]

    tri = dec_ref[...]
    mask4 = jnp.concatenate([tri.astype(F32)] * H, axis=0)
    head_t = _lane_group((T, W), HEAD_D)
    bd = (lax.broadcasted_iota(jnp.int32, (W, W), 0) // HEAD_D) == _lane_group((W, W), HEAD_D)
    tiles = list(range(ntile - 1, -1, -1)) if rev else list(range(ntile))
    subs = list(range(nsub - 1, -1, -1)) if rev else list(range(nsub))
    rows = lambda a: slice(a * T, (a + 1) * T)
    sub = lambda j: slice(j * C, (j + 1) * C)
    stack = lambda blocks: jnp.concatenate([blocks[j] for j in range(nsub)], axis=0)

    wg_hi, wg_lo = _split(wg_ref[...])
    logg = {}
    for a in tiles:
        z = _dot(pg_ref[rows(a), :], wg_hi) + _dot(pg_ref[rows(a), :], wg_lo) + bg_ref[...]
        logg[a] = (jnp.minimum(z, 0.0) - jnp.log(1.0 + jnp.exp(-jnp.abs(z)))) * (1.0 / GATE_TEMP)
    cums = {}
    for a in tiles:
        g_hi, g_lo = _split(logg[a])
        cums[a] = _dot(tri, g_hi) + _dot(tri, g_lo)
    qin, qoff, kin, kend, vb, tot, ptile = {}, {}, {}, {}, {}, {}, {}
    for a in tiles:
        cum = cums[a]
        q = pb_ref[rows(a), 0:W].astype(F32) * (HEAD_D ** -0.5)
        k = pb_ref[rows(a), W:2 * W].astype(F32)
        last_row = (lambda j: j * C) if rev else (lambda j: j * C + C - 1)
        off, run = {}, jnp.zeros((1, W), F32)
        for j in subs:
            tot[(a, j)] = cum[last_row(j):last_row(j) + 1]
            off[j] = run
            run = run + tot[(a, j)]
        ptile[a] = jnp.exp(run)
        e = jnp.exp(cum)
        qin[a] = (q * e).astype(BF16)
        qoff[a] = (q * (e * jnp.exp(stack({j: jnp.broadcast_to(off[j], (C, W)) for j in subs})))).astype(BF16)
        kin[a] = (k * jnp.exp(-cum)).astype(BF16)
        kend[a] = (k * jnp.exp(stack({j: jnp.broadcast_to(tot[(a, j)], (C, W)) for j in subs}) - cum)).astype(BF16)
        vb[a] = pb_ref[rows(a), 2 * W:3 * W]
    sc = {}
    for a in tiles:
        qst = jnp.concatenate([jnp.where(head_t == h, qin[a], jnp.zeros_like(qin[a])) for h in range(H)], axis=0)
        sc[a] = (_dot_nt(qst, kin[a]) * mask4).astype(BF16)
    ost = {a: _dot(sc[a], vb[a]) for a in tiles}
    kvt = {(a, j): jnp.where(bd, _dot_tn(vb[a][sub(j)], kend[a][sub(j)]), 0.0) for a in tiles for j in subs}
    cross, rend = {}, {}
    for a in tiles:
        r = None
        for j in subs:
            if r is not None:
                cross[(a, j)] = _dot_nt(qin[a][sub(j)], r.astype(BF16))
                r = r * jnp.exp(tot[(a, j)]) + kvt[(a, j)]
            else:
                cross[(a, j)] = jnp.zeros((C, W), F32)
                r = kvt[(a, j)]
        rend[a] = r
    st = st_scr[...]
    inter = {}
    for a in tiles:
        inter[a] = _dot_nt(qoff[a], st.astype(BF16))
        st = st * ptile[a] + rend[a]
    st_scr[...] = st

    for a in tiles:
        o = inter[a] + jnp.concatenate([cross[(a, j)] for j in range(nsub)], axis=0)
        for h in range(H):
            o = o + jnp.where(head_t == h, ost[a][h * T:(h + 1) * T], 0.0)
        if finish:
            o = o + oprev_ref[rows(a), :]
            ms = _dot_x2(o * o, ones_ref[...]) * (1.0 / HEAD_D)
            o = o * lax.rsqrt(ms + EPS) * gn_ref[...]
            r = pb_ref[rows(a), 3 * W:4 * W].astype(F32)
            o_ref[rows(a), :] = (o * (r * _sigmoid(r))).astype(o_ref.dtype)
        else:
            o_ref[rows(a), :] = o

    @pl.when(i == pl.num_programs(1) - 1)
    def _():
        sfin_ref[...] = st_scr[...]


def _gla_decay_matrices():
    T, C = GLA_TILE, GLA_CHUNK
    t = jnp.arange(T)[:, None]
    s = jnp.arange(T)[None, :]
    same = (t // C) == (s // C)
    return jnp.stack([same & (s <= t), same & (s >= t)]).astype(BF16)


def _gla_pass(l, d, pb3, pg3, wg, bg, s0, ones_bd, dec, oprev, gn, tt):
    B, n, _ = pb3.shape
    nt = n // tt
    rev = d == 1
    finish = oprev is not None
    W = GROUP_W

    def tok(b, i):
        return (b, (nt - 1 - i) if rev else i, 0)

    in_specs = [
        pl.BlockSpec((None, tt, 4 * W), tok),
        pl.BlockSpec((None, tt, LANES), tok),
        pl.BlockSpec((None, None, LANES, W), lambda b, i: (l, d, 0, 0)),
        pl.BlockSpec((None, None, 1, W), lambda b, i: (l, d, 0, 0)),
        pl.BlockSpec((None, W, W), lambda b, i: (b, 0, 0)),
        pl.BlockSpec((W, W), lambda b, i: (0, 0)),
        pl.BlockSpec((None, GLA_TILE, GLA_TILE), lambda b, i: (d, 0, 0)),
    ]
    args = [pb3, pg3, wg, bg, s0, ones_bd, dec]
    if finish:
        in_specs += [pl.BlockSpec((None, tt, W), tok), _vec_spec(l, W)]
        args += [oprev, gn]
    return pl.pallas_call(
        functools.partial(_gla_kernel, rev=rev, finish=finish),
        grid=(B, nt),
        in_specs=in_specs,
        out_specs=[pl.BlockSpec((None, tt, W), tok), pl.BlockSpec((None, W, W), lambda b, i: (b, 0, 0))],
        out_shape=[jax.ShapeDtypeStruct((B, n, W), BF16 if finish else F32),
                   jax.ShapeDtypeStruct((B, W, W), F32)],
        scratch_shapes=[pltpu.VMEM((W, W), F32)],
        compiler_params=_cparams("arbitrary", "arbitrary"),
        name="gla_bwd" if rev else "gla_fwd",
    )(*args)


def _gla(l, pb_c, pg_c, pb_x, pg_x, wg, bg, gn, ones_bd, tt_c, tt_x):
    B = pb_x.shape[0]
    zero = jnp.zeros((B, GROUP_W, GROUP_W), F32)
    dec = _gla_decay_matrices()
    ofc, sfc = _gla_pass(l, 0, pb_c, pg_c, wg, bg, zero, ones_bd, dec, None, None, tt_c)
    ofx, _ = _gla_pass(l, 0, pb_x, pg_x, wg, bg, sfc, ones_bd, dec, None, None, tt_x)
    ob_c, sbc = _gla_pass(l, 1, pb_c, pg_c, wg, bg, zero, ones_bd, dec, ofc, gn, tt_c)
    ob_x, _ = _gla_pass(l, 1, pb_x, pg_x, wg, bg, sbc, ones_bd, dec, ofx, gn, tt_x)
    return ob_c, ob_x


def _s5_matrices(a_re, a_im, log_dt, b_re, b_im, c_re, c_im, d_skip):
    T = S5_T
    G, P = a_re.shape[1:]
    I = b_re.shape[-1]
    lam = lax.complex(a_re.astype(F32), a_im.astype(F32))
    ldt = lam * jnp.exp(log_dt.astype(F32))[..., None]
    lam_bar = jnp.exp(ldt)
    b_bar = ((lam_bar - 1.0) / lam)[..., None] * lax.complex(b_re.astype(F32), b_im.astype(F32))
    cmat = lax.complex(c_re.astype(F32), c_im.astype(F32))
    steps = jnp.arange(T + 1, dtype=F32)
    pw = jnp.exp(ldt[..., None] * steps)
    taps = jnp.einsum('dgop,dgpk,dgpi->dgiko', cmat, pw[..., :T], b_bar).real
    taps = taps.at[0, :, :, 0, :].add(jnp.eye(I, dtype=F32)[None] * d_skip.astype(F32)[:, :, None])
    row = T * I
    zeros = jnp.zeros((G, I, row), F32)
    fwd = jnp.concatenate([zeros, taps[0].reshape(G, I, row)], axis=-1)
    bwd = jnp.concatenate([jnp.flip(taps[1], axis=2).reshape(G, I, row), zeros], axis=-1)
    m = jnp.stack([fwd[..., (T - s) * I:(T - s) * I + row] + bwd[..., (T - 1 - s) * I:(T - 1 - s) * I + row]
                   for s in range(T)], axis=1).reshape(G, row, row)

    ar = jnp.arange(T)
    pf = pw[0][..., T - 1 - ar]
    pb = pw[1][..., ar]
    bf = jnp.einsum('gps,gpi->gsip', pf, b_bar[0]).reshape(G, T * I, P)
    bb = jnp.einsum('gps,gpi->gsip', pb, b_bar[1]).reshape(G, T * I, P)
    bmat = jnp.concatenate([bf.real, bf.imag, bf.imag, bf.real, bb.real, bb.imag, bb.imag, bb.real], axis=-1)

    cf = jnp.einsum('gop,gpt->gpto', cmat[0], pw[0][..., 1 + ar]).reshape(G, P, T * I)
    cb = jnp.einsum('gop,gpt->gpto', cmat[1], pw[1][..., T - ar]).reshape(G, P, T * I)
    w = jnp.concatenate([m, cf.real, -cf.imag, cb.real, -cb.imag], axis=1)

    a = pw[..., T]
    acoef = jnp.stack([jnp.concatenate([a[0].real, a[0].real], axis=-1), jnp.concatenate([-a[0].imag, a[0].imag], axis=-1),
                       jnp.concatenate([a[1].real, a[1].real], axis=-1), jnp.concatenate([-a[1].imag, a[1].imag], axis=-1)])
    return bmat.astype(BF16), w.astype(BF16), acoef


def _s5_scan_kernel(xf_ref, xb_ref, bm_ref, a_ref, h0_ref, hf_ref, hb_ref, hfin_ref, st_scr, sf_scr, sb_scr, hf_scr, hb_scr):
    j = pl.program_id(1)
    G, tc, _ = xf_ref.shape
    pitch = tc + S5_PITCH_PAD
    half = hf_ref.shape[-1]

    @pl.when(j == 0)
    def _():
        st_scr[...] = h0_ref[...]

    for g in range(G):
        sf = _dot(xf_ref[g], bm_ref[g, :, 0:2 * half])
        sb = _dot(xb_ref[g], bm_ref[g, :, 2 * half:4 * half])
        for k in range(2):
            sf_scr[k, g * pitch:g * pitch + tc, :] = sf[:, k * half:(k + 1) * half]
            sb_scr[k, g * pitch:g * pitch + tc, :] = sb[:, k * half:(k + 1) * half]

    a1f, a2f, a1b, a2b = a_ref[0], a_ref[1], a_ref[2], a_ref[3]
    chunk = lambda i: pl.ds(i, G, stride=pitch)

    def body(i, hs):
        h1f, h2f, h1b, h2b = hs
        ib = tc - 1 - i
        hf_scr[chunk(i), :] = h1f
        hb_scr[chunk(ib), :] = h1b
        return (h1f * a1f + h2f * a2f + sf_scr[0, chunk(i), :], h2f * a1f - h1f * a2f + sf_scr[1, chunk(i), :],
                h1b * a1b + h2b * a2b + sb_scr[0, chunk(ib), :], h2b * a1b - h1b * a2b + sb_scr[1, chunk(ib), :])

    hs = lax.fori_loop(0, tc, body, (st_scr[0], st_scr[1], st_scr[2], st_scr[3]))
    for k in range(4):
        st_scr[k] = hs[k]
    for g in range(G):
        hf_ref[g] = hf_scr[g * pitch:g * pitch + tc, :]
        hb_ref[g] = hb_scr[g * pitch:g * pitch + tc, :]

    @pl.when(j == pl.num_programs(1) - 1)
    def _():
        hfin_ref[...] = st_scr[...]


def _s5_scan(l, xg, bmat, acoef, h0, B, tc):
    G, R, K = xg.shape
    nt = R // B // tc
    half = bmat.shape[-1] // 4
    fwd = lambda b, j: (0, b * nt + j, 0)
    bwd = lambda b, j: (0, b * nt + nt - 1 - j, 0)
    st_spec = pl.BlockSpec((None, 4, G, half), lambda b, j: (b, 0, 0, 0))
    pitch = tc + S5_PITCH_PAD
    return pl.pallas_call(
        _s5_scan_kernel,
        grid=(B, nt),
        in_specs=[pl.BlockSpec((G, tc, K), fwd),
                  pl.BlockSpec((G, tc, K), bwd),
                  pl.BlockSpec((None, G, K, 4 * half), lambda b, j: (l, 0, 0, 0)),
                  pl.BlockSpec((None, 4, G, half), lambda b, j: (l, 0, 0, 0)),
                  st_spec],
        out_specs=[pl.BlockSpec((G, tc, half), fwd), pl.BlockSpec((G, tc, half), bwd), st_spec],
        out_shape=[jax.ShapeDtypeStruct((G, R, half), F32), jax.ShapeDtypeStruct((G, R, half), F32),
                   jax.ShapeDtypeStruct((B, 4, G, half), F32)],
        scratch_shapes=[pltpu.VMEM((4, G, half), F32),
                        pltpu.VMEM((2, G * pitch, half), F32), pltpu.VMEM((2, G * pitch, half), F32),
                        pltpu.VMEM((G * pitch, half), F32), pltpu.VMEM((G * pitch, half), F32)],
        compiler_params=_cparams("arbitrary", "arbitrary"),
        name="s5_scan",
    )(xg, xg, bmat, acoef, h0)


def _s5_gather_kernel(x_ref, o_ref, xs_scr, xt_scr):
    G, tc, _ = o_ref.shape
    nh = x_ref.shape[1] // LANES
    gh = G // nh
    for h in range(nh):
        xs_scr[h] = x_ref[:, h * LANES:(h + 1) * LANES].astype(F32)
    for t in range(S5_T):
        for h in range(nh):
            rt = xs_scr[h, pl.ds(t, tc, stride=S5_T), :].T
            for g in range(gh):
                xt_scr[h * gh + g, t * S5_IN:(t + 1) * S5_IN, :] = rt[g * S5_IN:(g + 1) * S5_IN, :]
    for g in range(G):
        o_ref[g] = xt_scr[g].T.astype(o_ref.dtype)


def _s5_gather(x2, tc):
    N, W = x2.shape
    G = W // S5_IN
    R = N // S5_T
    return pl.pallas_call(
        _s5_gather_kernel,
        grid=(R // tc,),
        in_specs=[pl.BlockSpec((tc * S5_T, W), lambda i: (i, 0))],
        out_specs=pl.BlockSpec((G, tc, S5_T * S5_IN), lambda i: (0, i, 0)),
        out_shape=jax.ShapeDtypeStruct((G, R, S5_T * S5_IN), BF16),
        scratch_shapes=[pltpu.VMEM((W // LANES, tc * S5_T, LANES), F32), pltpu.VMEM((G, S5_T * S5_IN, tc), F32)],
        compiler_params=_cparams("parallel"),
        name="s5_gather",
    )(x2)


def _s5_scatter_kernel(x_ref, hf_ref, hb_ref, wy_ref, w_ref, b_ref, o_ref, yt_scr, tok_scr):
    G, tc, k = x_ref.shape
    kh = hf_ref.shape[-1]
    nh = tok_scr.shape[0]
    gh = G // nh
    for g in range(G):
        y = _dot(x_ref[g], wy_ref[g, 0:k, :])
        y = y + _dot(hf_ref[g].astype(BF16), wy_ref[g, k:k + kh, :])
        y = y + _dot(hb_ref[g].astype(BF16), wy_ref[g, k + kh:, :])
        yt = y.T
        for t in range(S5_T):
            yt_scr[t, g // gh, (g % gh) * S5_IN:(g % gh + 1) * S5_IN, :] = yt[t * S5_IN:(t + 1) * S5_IN, :]
    for t in range(S5_T):
        for h in range(nh):
            tok_scr[h, pl.ds(t, tc, stride=S5_T), :] = yt_scr[t, h].T
    y = _gelu(jnp.concatenate([tok_scr[h] for h in range(nh)], axis=1))
    gate = _sigmoid(_dot(y.astype(BF16), w_ref[...]) + b_ref[...])
    o_ref[...] = (y * gate).astype(o_ref.dtype)


def _s5_scatter_finish(l, xg, hf, hb, wy, w_glu, b_glu, tc):
    G, R, K = xg.shape
    kh = hf.shape[-1]
    W = G * S5_IN
    grp = lambda width: pl.BlockSpec((G, tc, width), lambda i: (0, i, 0))
    return pl.pallas_call(
        _s5_scatter_kernel,
        grid=(R // tc,),
        in_specs=[grp(K), grp(kh), grp(kh),
                  pl.BlockSpec((None, G, K + 2 * kh, K), lambda i: (l, 0, 0, 0)),
                  pl.BlockSpec((None, W, W), lambda i: (l, 0, 0)),
                  _vec_spec(l, W)],
        out_specs=pl.BlockSpec((tc * S5_T, W), lambda i: (i, 0)),
        out_shape=jax.ShapeDtypeStruct((R * S5_T, W), BF16),
        scratch_shapes=[pltpu.VMEM((S5_T, W // LANES, LANES, tc), F32), pltpu.VMEM((W // LANES, tc * S5_T, LANES), F32)],
        compiler_params=_cparams("parallel"),
        name="s5_scatter",
    )(xg, hf, hb, wy, w_glu, b_glu)


def _s5(l, pc_c, pc_x, mats, w_glu, b_glu, need_ctx):
    bmat, w, acoef = mats
    B, n, W = pc_x.shape
    nctx = pc_c.shape[1]
    G = W // S5_IN
    tc_c = min(S5_TILE, B * nctx // S5_T)
    tc_x = min(S5_TILE, n // S5_T)
    xg_c = _s5_gather(pc_c.reshape(B * nctx, W), tc_c)
    xg_x = _s5_gather(pc_x.reshape(B * n, W), tc_x)
    h0 = jnp.zeros((B, 4, G, acoef.shape[-1]), F32)
    hf_c, hb_c, h1 = _s5_scan(l, xg_c, bmat, acoef, h0, B, nctx // S5_T)
    hf_x, hb_x, _ = _s5_scan(l, xg_x, bmat, acoef, h1, B, tc_x)
    oc_x = _s5_scatter_finish(l, xg_x, hf_x, hb_x, w, w_glu, b_glu, tc_x)
    oc_c = _s5_scatter_finish(l, xg_c, hf_c, hb_c, w, w_glu, b_glu, tc_c) if need_ctx else None
    return oc_c, oc_x


def _attn_kernel(q_ref, k_ref, v_ref, o_ref, m_scr, l_scr, acc_scr):
    kv = pl.program_id(2)
    HP = MLA_HEAD_PAD
    tk = k_ref.shape[0]

    @pl.when(kv == 0)
    def _():
        m_scr[...] = jnp.full(m_scr.shape, -jnp.inf, F32)
        l_scr[...] = jnp.zeros(l_scr.shape, F32)
        acc_scr[...] = jnp.zeros(acc_scr.shape, F32)

    def scores(h):
        lanes = slice(h * HP, (h + 1) * HP)
        return _dot_nt(q_ref[:, lanes], k_ref[:, lanes])

    s_next = scores(0)
    for h in range(MLA_HEADS):
        lanes = slice(h * HP, (h + 1) * HP)
        s = s_next
        if h + 1 < MLA_HEADS:
            s_next = scores(h + 1)
        m_prev = m_scr[h]
        m_new = jnp.maximum(m_prev, jnp.max(s, axis=1, keepdims=True))
        alpha = jnp.exp2(m_prev - m_new)
        p = jnp.exp2(s - m_new[:, 0:1])
        lp = p[:, 0:LANES]
        for c in range(1, tk // LANES):
            lp = lp + p[:, c * LANES:(c + 1) * LANES]
        l_scr[h] = alpha * l_scr[h] + lp
        acc_scr[h] = alpha * acc_scr[h] + _dot(p.astype(BF16), v_ref[:, lanes])
        m_scr[h] = m_new

    @pl.when(kv == pl.num_programs(2) - 1)
    def _():
        low = lax.broadcasted_iota(jnp.int32, acc_scr.shape[1:], 1) < MLA_V
        norm = lambda h: acc_scr[h] * (1.0 / jnp.sum(l_scr[h], axis=1, keepdims=True))
        outs = [jnp.where(low, norm(h), norm(h + 1)) for h in range(0, MLA_HEADS, 2)]
        o_ref[...] = jnp.concatenate(outs, axis=1).astype(o_ref.dtype)


def _attention(q3, k3, v3, tq, tk, k0, nk):
    B, nq, HP = q3.shape
    kb = k0 // tk
    return pl.pallas_call(
        _attn_kernel,
        grid=(B, nq // tq, nk // tk),
        in_specs=[pl.BlockSpec((None, tq, HP), lambda b, i, j: (b, i, 0)),
                  pl.BlockSpec((None, tk, HP), lambda b, i, j: (b, kb + j, 0)),
                  pl.BlockSpec((None, tk, HP), lambda b, i, j: (b, kb + j, 0))],
        out_specs=pl.BlockSpec((None, tq, MLA_HEADS * MLA_V), lambda b, i, j: (b, i, 0)),
        out_shape=jax.ShapeDtypeStruct((B, nq, MLA_HEADS * MLA_V), BF16),
        scratch_shapes=[pltpu.VMEM((MLA_HEADS, tq, LANES), F32),
                        pltpu.VMEM((MLA_HEADS, tq, LANES), F32),
                        pltpu.VMEM((MLA_HEADS, tq, MLA_HEAD_PAD), F32)],
        compiler_params=_cparams("parallel", "parallel", "arbitrary"),
        name="attention",
    )(q3, k3, v3)


def _outproj_kernel(oa, ob, oc, od, w_ref, x_ref, gpost_ref, gt_ref, gpre_ref, sc_ref, sh_ref, xo_ref, h_ref):
    W = GROUP_W
    mix = _dot(oa[...], w_ref[0:W, :])
    mix = mix + _dot(ob[...], w_ref[W:2 * W, :])
    mix = mix + _dot(oc[...], w_ref[2 * W:3 * W, :])
    mix = mix + _dot(od[...], w_ref[3 * W:4 * W, :])
    x = x_ref[...] + gt_ref[...] * (_rms(mix) * gpost_ref[...])
    xo_ref[...] = x
    h_ref[...] = (_rms(x) * gpre_ref[...] * (1.0 + sc_ref[...]) + sh_ref[...]).astype(h_ref.dtype)


def _outproj(l, parts, w_out, x2, mod4, brow, g_post, g_pre_ffn, tm):
    R, D = x2.shape
    W = GROUP_W
    part_spec = pl.BlockSpec((tm, W), lambda i: (i, 0))
    row_spec = pl.BlockSpec((tm, D), lambda i: (i, 0))
    return pl.pallas_call(
        _outproj_kernel,
        grid=(R // tm,),
        in_specs=[part_spec] * 4 + [
            pl.BlockSpec((None, 4 * W, D), lambda i: (l, 0, 0)),
            row_spec,
            _vec_spec(l, D),
            _mod_spec(l, 2, D, brow),
            _vec_spec(l, D),
            _mod_spec(l, 4, D, brow),
            _mod_spec(l, 3, D, brow),
        ],
        out_specs=[row_spec, row_spec],
        out_shape=[jax.ShapeDtypeStruct((R, D), F32), jax.ShapeDtypeStruct((R, D), BF16)],
        compiler_params=_cparams("parallel"),
        name="outproj",
    )(*parts, w_out, x2, g_post, mod4, g_pre_ffn, mod4, mod4)


FFN_HALO = 16
FFN_COLS = 256


def _ffn_kernel(hp_ref, h_ref, hn_ref, wup_ref, cw_ref, cb_ref, wdn_ref, x_ref, gpost_ref, gt_ref, o_ref, y_scr):
    i = pl.program_id(1)
    tm = h_ref.shape[0]
    dff = wdn_ref.shape[0]
    use = FFN_HALO // 2
    rows = tm + 2 * use
    prev = jnp.where(i == 0, jnp.zeros_like(hp_ref[...]), hp_ref[...])[FFN_HALO - use:]
    nxt = jnp.where(i == pl.num_programs(1) - 1, jnp.zeros_like(hn_ref[...]), hn_ref[...])[:use]
    hb = jnp.concatenate([prev, h_ref[...], nxt], axis=0)

    def conv(z, cols):
        w = cw_ref[:, cols]
        out = cb_ref[:, cols] + w[0:1] * pltpu.roll(z, 1, axis=0) + w[1:2] * z + w[2:3] * pltpu.roll(z, rows - 1, axis=0)
        return out[use:use + tm]

    def cols(j):
        return slice(j * FFN_COLS, (j + 1) * FFN_COLS), slice(dff + j * FFN_COLS, dff + (j + 1) * FFN_COLS)

    def up(j):
        ca, cg = cols(j)
        return _dot(hb, wup_ref[:, ca]), _dot(hb, wup_ref[:, cg])

    nchunks = dff // FFN_COLS
    z_next = up(0)
    for j in range(nchunks):
        ca, cg = cols(j)
        za, zg = z_next
        if j + 1 < nchunks:
            z_next = up(j + 1)
        y_scr[:, ca] = _gelu_gate(conv(za, ca), conv(zg, cg)).astype(BF16)
    acc = _dot(y_scr[...], wdn_ref[...])
    o_ref[...] = x_ref[...] + gt_ref[...] * (_rms(acc) * gpost_ref[...])


def _ffn(l, h3, x3, w_up, conv_w, conv_b, w_down, mod4, brow, g_post, tm):
    B, n, D = x3.shape
    nt = n // tm
    hb = tm // FFN_HALO
    nh = n // FFN_HALO
    dff = w_down.shape[1]
    once = pl.Buffered(1)
    return pl.pallas_call(
        _ffn_kernel,
        grid=(B, nt),
        in_specs=[
            pl.BlockSpec((None, FFN_HALO, D), lambda b, i: (b, jnp.maximum(i * hb - 1, 0), 0)),
            pl.BlockSpec((None, tm, D), lambda b, i: (b, i, 0)),
            pl.BlockSpec((None, FFN_HALO, D), lambda b, i: (b, jnp.minimum((i + 1) * hb, nh - 1), 0)),
            pl.BlockSpec((None, D, 2 * dff), lambda b, i: (l, 0, 0), pipeline_mode=once),
            pl.BlockSpec((None, 3, 2 * dff), lambda b, i: (l, 0, 0)),
            pl.BlockSpec((None, 1, 2 * dff), lambda b, i: (l, 0, 0)),
            pl.BlockSpec((None, dff, D), lambda b, i: (l, 0, 0), pipeline_mode=once),
            pl.BlockSpec((None, tm, D), lambda b, i: (b, i, 0)),
            _vec_spec(l, D),
            _mod_spec(l, 5, D, brow),
        ],
        out_specs=pl.BlockSpec((None, tm, D), lambda b, i: (b, i, 0)),
        out_shape=jax.ShapeDtypeStruct((B, n, D), F32),
        scratch_shapes=[pltpu.VMEM((tm, dff), BF16)],
        compiler_params=_cparams("parallel", "parallel"),
        name="conv_ffn",
    )(h3, h3, h3, w_up, conv_w, conv_b, w_down, x3, g_post, mod4)


def _prep_w_in(w_in):
    w_in = w_in.astype(BF16)
    z = lambda n: jnp.zeros(w_in.shape[:-1] + (n,), w_in.dtype)
    a = w_in[..., 0:512]
    b = w_in[..., 512:1536]
    gl = w_in[..., 1536:1568]
    c = w_in[..., 1568:1824]
    cq = w_in[..., 1824:2048]
    ckv_kr = w_in[..., 2048:2176]
    return jnp.concatenate([a, b, c, cq, z(32), ckv_kr, gl, z(96)], axis=-1)


def _rope_swap(t):
    q = MLA_ROPE // 4
    return jnp.concatenate([t[..., q:2 * q], t[..., 0:q], t[..., 3 * q:4 * q], t[..., 2 * q:3 * q]], axis=-1)


def _prep_mla(w_uq, w_ukv):
    L = w_uq.shape[0]
    H, NP, RP, HP = MLA_HEADS, MLA_NOPE, MLA_ROPE, MLA_HEAD_PAD
    wq = w_uq.reshape(L, MLA_Q_RANK, H, NP + RP)
    zq = jnp.zeros((L, MLA_Q_RANK, H, HP - NP - RP), w_uq.dtype)
    znope = jnp.zeros((L, MLA_Q_RANK, H, NP), w_uq.dtype)
    q_main = jnp.concatenate([wq, zq], axis=-1)
    q_swap = jnp.concatenate([znope, _rope_swap(wq[..., NP:]), zq], axis=-1)
    padq = lambda w: jnp.pad(w.reshape(L, MLA_Q_RANK, H * HP), ((0, 0), (0, MLA_Q_LANES - MLA_Q_RANK), (0, 0)))

    wkv = w_ukv.reshape(L, MLA_KV_RANK, H, NP + MLA_V)
    zk = jnp.zeros((L, MLA_KV_RANK, H, HP - NP), w_ukv.dtype)
    k_lat = jnp.concatenate([wkv[..., :NP], zk], axis=-1)
    eye = jnp.eye(RP, dtype=w_ukv.dtype)
    place = lambda e: jnp.broadcast_to(
        jnp.concatenate([jnp.zeros((RP, NP), e.dtype), e, jnp.zeros((RP, HP - NP - RP), e.dtype)], axis=-1)[None, :, None, :],
        (L, RP, H, HP))
    k_main = jnp.concatenate([k_lat, place(eye)], axis=1)
    k_swap = jnp.concatenate([jnp.zeros_like(k_lat), place(_rope_swap(eye))], axis=1)
    zv = jnp.zeros((L, MLA_KV_RANK, H, HP - MLA_V), w_ukv.dtype)
    odd = (jnp.arange(H) % 2 == 1)[None, None, :, None]
    v_lat = jnp.where(odd, jnp.concatenate([zv, wkv[..., NP:]], axis=-1), jnp.concatenate([wkv[..., NP:], zv], axis=-1))
    v_main = jnp.concatenate([v_lat, jnp.zeros((L, RP, H, HP), w_ukv.dtype)], axis=1)
    flat = lambda w: w.reshape(L, w.shape[1], H * HP).astype(BF16)
    return (padq(q_main).astype(BF16), padq(q_swap).astype(BF16), flat(k_main), flat(k_swap), flat(v_main))


def _rope_tables(n):
    rows = n // GRID_W
    nf = MLA_ROPE // 4
    inv = ROPE_BASE ** (-jnp.arange(nf, dtype=F32) / nf)
    ar = jnp.arange(rows, dtype=F32)[:, None] * inv[None, :]
    ac = jnp.arange(GRID_W, dtype=F32)[:, None] * inv[None, :]
    by_row = lambda t: jnp.repeat(t, GRID_W, axis=0)
    by_col = lambda t: jnp.tile(t, (rows, 1))
    cr, sr, cc, sn = by_row(jnp.cos(ar)), by_row(jnp.sin(ar)), by_col(jnp.cos(ac)), by_col(jnp.sin(ac))
    one = jnp.ones((n, MLA_NOPE), F32)
    zero = jnp.zeros((n, MLA_HEAD_PAD - MLA_NOPE - MLA_ROPE), F32)
    cos = jnp.concatenate([one, cr, cr, cc, cc, zero], axis=1)
    sin = jnp.concatenate([0.0 * one, -sr, sr, -sn, sn, zero], axis=1)
    return cos, sin


def _pick_tile(n, want):
    t = min(n, want)
    while n % t:
        t //= 2
    return t


def kernel(x, c, ctx, c_ctx, w_mod, b_mod, g_pre_mix, g_post_mix, g_pre_ffn, g_post_ffn, w_in,
           sgu_norm, sgu_w, sgu_b, gla_w_gate, gla_b_gate, gla_norm,
           s5_a_re, s5_a_im, s5_log_dt, s5_b_re, s5_b_im, s5_c_re, s5_c_im, s5_d, s5_w_glu, s5_b_glu,
           mla_q_norm, mla_w_uq, mla_kv_norm, mla_w_ukv, w_out,
           ffn_w_up, ffn_conv_w, ffn_conv_b, ffn_w_down):
    B, n, D = x.shape
    nctx = ctx.shape[1]
    L = w_mod.shape[0]
    W = GROUP_W
    assert B < 8 and n % TOKEN_TILE == 0 and nctx % GLA_TILE == 0 and n % GRID_W == 0

    c8 = jnp.concatenate([c, c_ctx[None, :], jnp.zeros((8 - B - 1, D), F32)], axis=0)
    mod4 = _modulation(c8, w_mod, b_mod).reshape(L, 8, 1, 6 * D)
    vec = lambda p: p.reshape(L, 1, -1).astype(F32)
    g_pre_mix, g_post_mix, g_pre_ffn, g_post_ffn = map(vec, (g_pre_mix, g_post_mix, g_pre_ffn, g_post_ffn))
    w_in_p = _prep_w_in(w_in)
    sgu_gn = vec(sgu_norm)
    sgu_w_st = sgu_w.reshape(L, -1, MLP_CHUNK).astype(BF16)
    sgu_bias = jnp.repeat(jnp.swapaxes(sgu_b, 1, 2), HEAD_D, axis=2).astype(F32)
    ones_bd = jnp.kron(jnp.eye(W // HEAD_D, dtype=F32), jnp.ones((HEAD_D, HEAD_D), F32)).astype(BF16)
    gla_wg = jnp.zeros((L, 2, LANES, W), F32)
    gla_wg = gla_wg.at[:, 0, 0:GATE_RANK].set(gla_w_gate[:, 0]).at[:, 1, GATE_RANK:2 * GATE_RANK].set(gla_w_gate[:, 1])
    gla_bg = gla_b_gate.reshape(L, 2, 1, W).astype(F32)
    gla_gn = vec(gla_norm)
    s5_wglu = s5_w_glu.astype(BF16)
    s5_bglu = vec(s5_b_glu)
    mla_qn = jnp.pad(mla_q_norm, ((0, 0), (0, MLA_Q_LANES - MLA_Q_RANK))).reshape(L, 1, MLA_Q_LANES).astype(F32)
    mla_kvn = jnp.pad(mla_kv_norm, ((0, 0), (0, MLA_KV_LANES - MLA_KV_RANK))).reshape(L, 1, MLA_KV_LANES).astype(F32)
    mla_wts = _prep_mla(mla_w_uq, mla_w_ukv)
    tables = _rope_tables(n)
    w_out_b = w_out.astype(BF16)
    w_up_b = ffn_w_up.astype(BF16)
    w_down_b = ffn_w_down.astype(BF16)
    dff = ffn_w_down.shape[1]
    half_gate = jnp.concatenate([jnp.ones((dff,), F32), jnp.full((dff,), 0.5, F32)])
    conv_w = ffn_conv_w.astype(F32) * half_gate
    conv_b = (ffn_conv_b.astype(F32) * half_gate).reshape(L, 1, -1)

    s5_mats = jax.vmap(_s5_matrices)(s5_a_re, s5_a_im, s5_log_dt, s5_b_re, s5_b_im, s5_c_re, s5_c_im, s5_d)

    tm_x = _pick_tile(n, TOKEN_TILE)
    tm_c = _pick_tile(nctx, TOKEN_TILE)
    tpb_x = n // tm_x
    brow_x = lambda i: i // tpb_x
    brow_c = lambda *g: B
    brow_x2 = lambda b, i: b

    xs = x.reshape(B * n, D)
    cs = ctx.reshape(B * nctx, D)
    nk = n + nctx
    k_all = v_all = jnp.zeros((B, nk, MLA_HEADS * MLA_HEAD_PAD), BF16)

    for l in range(L):
        need_ctx = l < L - 1
        sgu = (sgu_gn, sgu_w_st, sgu_bias, ones_bd)
        wq, wqs, wk, wks, wv = mla_wts
        oa_x, pb_x, pc_x, pg_x, q_x, k_all, v_all = _inproj(
            l, xs, mod4, brow_x, g_pre_mix, w_in_p, sgu, (mla_qn, mla_kvn, mla_wts, tables), tm_x, n, B, 0, nk, (k_all, v_all))
        oa_c, pb_c, pc_c, pg_c, q_c, k_all, v_all = _inproj(
            l, cs, mod4, brow_c, g_pre_mix, w_in_p, sgu, (mla_qn, mla_kvn, (wq, None, wk, None, wv), None), tm_c, nctx, B, n, nk,
            (k_all, v_all))

        r3 = lambda t, m: t.reshape(B, m, t.shape[-1])
        ob_c, ob_x = _gla(l, r3(pb_c, nctx), r3(pg_c, nctx), r3(pb_x, n), r3(pg_x, n),
                          gla_wg, gla_bg, gla_gn, ones_bd, _pick_tile(nctx, GLA_TILE), _pick_tile(n, GLA_STEP_TILES * GLA_TILE))
        oc_c, oc_x = _s5(l, r3(pc_c, nctx), r3(pc_x, n), s5_mats, s5_wglu, s5_bglu, need_ctx)
        tk = next(t for t in ATTN_KV_TILES if nk % t == 0)
        od_x = _attention(r3(q_x, n), k_all, v_all, tm_x, tk, 0, nk).reshape(B * n, W)

        xs, hx = _outproj(l, (oa_x, ob_x.reshape(B * n, W), oc_x, od_x), w_out_b, xs, mod4, brow_x,
                          g_post_mix, g_pre_ffn, tm_x)
        xs = _ffn(l, hx.reshape(B, n, D), xs.reshape(B, n, D), w_up_b, conv_w, conv_b, w_down_b, mod4, brow_x2,
                  g_post_ffn, tm_x).reshape(B * n, D)

        if need_ctx:
            od_c = _attention(r3(q_c, nctx), k_all, v_all, tm_c, nctx, n, nctx).reshape(B * nctx, W)
            cs, hc = _outproj(l, (oa_c, ob_c.reshape(B * nctx, W), oc_c, od_c), w_out_b, cs, mod4, brow_c,
                              g_post_mix, g_pre_ffn, tm_c)
            cs = _ffn(l, hc.reshape(B, nctx, D), cs.reshape(B, nctx, D), w_up_b, conv_w, conv_b, w_down_b, mod4,
                      brow_c, g_post_ffn, tm_c).reshape(B * nctx, D)
    return xs.reshape(B, n, D)
```

```python
import functools

import jax
import jax.numpy as jnp
from jax import lax
from jax.experimental import pallas as pl
from jax.experimental.pallas import tpu as pltpu

F32 = jnp.float32
BF16 = jnp.bfloat16

EPS = 1e-6
GRID_W = 64
GROUP_W = 256
HEAD_D = 64
MLP_CHUNK = 128
GATE_RANK = 16
GATE_TEMP = 16.0
GLA_CHUNK = 64
GLA_TILE = 256
GLA_STEP_TILES = 8
S5_IN = 16
S5_T = 16
S5_TILE = 128
S5_PITCH_PAD = 8
MLA_HEADS = 4
MLA_NOPE = 64
MLA_ROPE = 32
MLA_V = 64
MLA_Q_RANK = 224
MLA_KV_RANK = 96
MLA_HEAD_PAD = 128
ROPE_BASE = 10000.0
LOG2E = 1.4426950408889634

LANES = 128
VMEM_LIMIT = 48 * 1024 * 1024
TOKEN_TILE = 512
INPROJ_TILE = 1024
MOD_COL_TILE = 1536
ATTN_KV_TILES = (2816, 1408, 768, 512, 256, 128)
MLA_Q_LANES = 256
MLA_KV_LANES = 128


def _cparams(*sem):
    return pltpu.CompilerParams(dimension_semantics=sem, vmem_limit_bytes=VMEM_LIMIT)


def _dot(a, b):
    return jnp.dot(a, b, preferred_element_type=F32)


def _dot_nt(a, b):
    return lax.dot_general(a, b, (((1,), (1,)), ((), ())), preferred_element_type=F32)


def _dot_tn(a, b):
    return lax.dot_general(a, b, (((0,), (0,)), ((), ())), preferred_element_type=F32)


def _split(a):
    hi = a.astype(BF16)
    lo = (a - hi.astype(F32)).astype(BF16)
    return hi, lo


def _dot_x2(a, b_bf16):
    hi, lo = _split(a)
    return _dot(hi, b_bf16) + _dot(lo, b_bf16)


def _dot_x3(a, b):
    ah, al = _split(a)
    bh, bl = _split(b)
    return _dot(ah, bh) + _dot(al, bh) + _dot(ah, bl)


def _rms(x):
    return x * lax.rsqrt(jnp.mean(x * x, axis=-1, keepdims=True) + EPS)


def _gelu(x):
    return 0.5 * x * (1.0 + jnp.tanh(0.7978845608028654 * (x + 0.044715 * (x * x * x))))


def _gelu_gate(a, half_g):
    u = a * (0.7978845608028654 + 0.035677408136300125 * (a * a))
    return (a * half_g) * (1.0 + jnp.tanh(u))


def _sigmoid(x):
    return 1.0 / (1.0 + jnp.exp(-x))


def _lane_group(shape, width):
    return lax.broadcasted_iota(jnp.int32, shape, len(shape) - 1) // width


def _mod_kernel(c_ref, w_ref, b_ref, o_ref):
    c = c_ref[...]
    s = c * _sigmoid(c)
    o_ref[...] = _dot_x3(s, w_ref[...]) + b_ref[...]


def _modulation(c8, w_mod, b_mod):
    L, D, W = w_mod.shape
    tn = MOD_COL_TILE
    return pl.pallas_call(
        _mod_kernel,
        grid=(L, W // tn),
        in_specs=[
            pl.BlockSpec((8, D), lambda l, j: (0, 0)),
            pl.BlockSpec((None, D, tn), lambda l, j: (l, 0, j)),
            pl.BlockSpec((None, 1, tn), lambda l, j: (l, 0, j)),
        ],
        out_specs=pl.BlockSpec((None, 8, tn), lambda l, j: (l, 0, j)),
        out_shape=jax.ShapeDtypeStruct((L, 8, W), F32),
        compiler_params=_cparams("arbitrary", "arbitrary"),
        name="modulation",
    )(c8, w_mod, b_mod.reshape(L, 1, W))


def _mod_spec(l, j, D, bfn):
    return pl.BlockSpec((None, None, 1, D), lambda *g: (l, bfn(*g), 0, j))


def _vec_spec(l, width):
    return pl.BlockSpec((None, 1, width), lambda *g: (l, 0, 0))


IN_SLABS = (("a", 0, 2 * GROUP_W), ("b", 512, 4 * GROUP_W), ("c", 1536, GROUP_W), ("d", 1792, MLA_Q_LANES + MLA_KV_LANES),
            ("g", 2176, LANES))
IN_PAD_COLS = 2304


def _sgu_tile(p, gn_ref, w_ref, b_ref, ones_ref, o_ref):
    tm = p.shape[0]
    g = _gelu(p)
    u = g[:, :GROUP_W]
    v = g[:, GROUP_W:]
    ms = _dot_x2(v * v, ones_ref[...]) * (1.0 / HEAD_D)
    vb = (v * lax.rsqrt(ms + EPS) * gn_ref[...]).astype(BF16)
    head = _lane_group((MLP_CHUNK, GROUP_W), HEAD_D)
    w = w_ref[...]
    for c in range(tm // MLP_CHUNK):
        rows = slice(c * MLP_CHUNK, (c + 1) * MLP_CHUNK)
        r = _dot(w, vb[rows])
        s = b_ref[...]
        for h in range(GROUP_W // HEAD_D):
            s = s + jnp.where(head == h, r[h * MLP_CHUNK:(h + 1) * MLP_CHUNK], 0.0)
        o_ref[rows, :] = (u[rows] * s).astype(o_ref.dtype)


def _mla_tile(pd, qn_ref, kvn_ref, wq_ref, wqs_ref, wk_ref, wks_ref, wv_ref, cos_ref, sin_ref, q_ref, k_ref, v_ref):
    cq = pd[:, 0:MLA_Q_LANES]
    ms = jnp.sum(cq * cq, axis=-1, keepdims=True) * (1.0 / MLA_Q_RANK)
    cqn = (cq * lax.rsqrt(ms + EPS) * qn_ref[...]).astype(BF16)
    ck = pd[:, MLA_Q_LANES:MLA_Q_LANES + MLA_KV_LANES]
    lane = lax.broadcasted_iota(jnp.int32, ck.shape, 1)
    is_lat = lane < MLA_KV_RANK
    ms = jnp.sum(jnp.where(is_lat, ck * ck, 0.0), axis=-1, keepdims=True) * (1.0 / MLA_KV_RANK)
    ckn = jnp.where(is_lat, ck * lax.rsqrt(ms + EPS) * kvn_ref[...], ck).astype(BF16)
    q = _dot(cqn, wq_ref[...])
    k = _dot(ckn, wk_ref[...])
    if cos_ref is not None:
        cos = jnp.concatenate([cos_ref[...]] * MLA_HEADS, axis=1)
        sin = jnp.concatenate([sin_ref[...]] * MLA_HEADS, axis=1)
        q = q * cos + _dot(cqn, wqs_ref[...]) * sin
        k = k * cos + _dot(ckn, wks_ref[...]) * sin
    q_ref[...] = (q * ((MLA_NOPE + MLA_ROPE) ** -0.5 * LOG2E)).astype(BF16)
    k_ref[...] = k.astype(BF16)
    v_ref[...] = _dot(ckn, wv_ref[...]).astype(BF16)


def _inproj_kernel(*refs, rope):
    x_ref, g_ref, sc_ref, sh_ref, w_ref, gn_ref, wsp_ref, bsp_ref, ones_ref, qn_ref, kvn_ref = refs[:11]
    if rope:
        wq_ref, wqs_ref, wk_ref, wks_ref, wv_ref, cos_ref, sin_ref = refs[11:18]
    else:
        wq_ref, wk_ref, wv_ref = refs[11:14]
        wqs_ref = wks_ref = cos_ref = sin_ref = None
    oa, ob, oc, og, q_ref, k_ref, v_ref = refs[-7:]
    half = x_ref.shape[0] // 2
    slab = {name: slice(off, off + width) for name, off, width in IN_SLABS}

    def project(rows):
        h = _rms(x_ref[rows, :]) * g_ref[...] * (1.0 + sc_ref[...]) + sh_ref[...]
        hb = h.astype(BF16)
        pa = _dot(hb, w_ref[:, slab["a"]])
        for name, o_ref in (("b", ob), ("c", oc), ("g", og)):
            o_ref[rows, :] = _dot(hb, w_ref[:, slab[name]]).astype(o_ref.dtype)
        return pa, _dot(hb, w_ref[:, slab["d"]])

    def mixers(pa, pd, rows):
        cos, sin = (cos_ref.at[rows], sin_ref.at[rows]) if rope else (None, None)
        _mla_tile(pd, qn_ref, kvn_ref, wq_ref, wqs_ref, wk_ref, wks_ref, wv_ref, cos, sin,
                  q_ref.at[rows], k_ref.at[rows], v_ref.at[rows])
        _sgu_tile(pa, gn_ref, wsp_ref, bsp_ref, ones_ref, oa.at[rows])

    first, second = slice(0, half), slice(half, 2 * half)
    pa0, pd0 = project(first)
    pa1, pd1 = project(second)
    mixers(pa0, pd0, first)
    mixers(pa1, pd1, second)


def _inproj(l, x2, mod4, brow, g_pre, w_in_p, sgu, mla, tm, n, B, row0, nk, kv_into):
    R, D = x2.shape
    gn, w_st, bias, ones_bd = sgu
    qn, kvn, (wq, wqs, wk, wks, wv), tables = mla
    rope = tables is not None
    H = GROUP_W // HEAD_D
    HP = MLA_HEADS * MLA_HEAD_PAD
    QL, KL = MLA_Q_LANES, MLA_KV_LANES
    npt = n // tm
    width = {name: w for name, _, w in IN_SLABS}
    wspec = lambda r: pl.BlockSpec((None, r, HP), lambda i: (l, 0, 0))
    in_specs = [
        pl.BlockSpec((tm, D), lambda i: (i, 0)),
        _vec_spec(l, D),
        _mod_spec(l, 1, D, brow),
        _mod_spec(l, 0, D, brow),
        pl.BlockSpec((None, D, IN_PAD_COLS), lambda i: (l, 0, 0)),
        _vec_spec(l, GROUP_W),
        pl.BlockSpec((None, H * MLP_CHUNK, MLP_CHUNK), lambda i: (l, 0, 0)),
        pl.BlockSpec((None, MLP_CHUNK, GROUP_W), lambda i: (l, 0, 0)),
        pl.BlockSpec((GROUP_W, GROUP_W), lambda i: (0, 0)),
        _vec_spec(l, QL),
        _vec_spec(l, KL),
    ]
    args = [x2, g_pre, mod4, mod4, w_in_p, gn, w_st, bias, ones_bd, qn, kvn]
    if rope:
        tspec = pl.BlockSpec((tm, MLA_HEAD_PAD), lambda i: (i % npt, 0))
        in_specs += [wspec(QL), wspec(QL), wspec(KL), wspec(KL), wspec(KL), tspec, tspec]
        args += [wq, wqs, wk, wks, wv, tables[0], tables[1]]
    else:
        in_specs += [wspec(QL), wspec(KL), wspec(KL)]
        args += [wq, wk, wv]
    aliases = {len(args): 5, len(args) + 1: 6}
    in_specs += [pl.BlockSpec(memory_space=pl.ANY)] * 2
    args += list(kv_into)
    row = lambda w: pl.BlockSpec((tm, w), lambda i: (i, 0))
    kv_spec = pl.BlockSpec((None, tm, HP), lambda i: (i // npt, row0 // tm + i % npt, 0))
    kv_shape = jax.ShapeDtypeStruct((B, nk, HP), BF16)
    return pl.pallas_call(
        functools.partial(_inproj_kernel, rope=rope),
        grid=(R // tm,),
        in_specs=in_specs,
        out_specs=[row(GROUP_W), row(width["b"]), row(width["c"]), row(width["g"]), row(HP), kv_spec, kv_spec],
        out_shape=[jax.ShapeDtypeStruct((R, w), BF16) for w in (GROUP_W, width["b"], width["c"], width["g"], HP)]
        + [kv_shape, kv_shape],
        input_output_aliases=aliases,
        compiler_params=_cparams("parallel"),
        name="inproj",
    )(*args)


def _gla_kernel(*refs, rev, finish):
    if finish:
        (pb_ref, pg_ref, wg_ref, bg_ref, s0_ref, ones_ref, dec_ref, oprev_ref, gn_ref, o_ref, sfin_ref, st_scr) = refs
    else:
        (pb_ref, pg_ref, wg_ref, bg_ref, s0_ref, ones_ref, dec_ref, o_ref, sfin_ref, st_scr) = refs
    i = pl.program_id(1)
    C, W, T = GLA_CHUNK, GROUP_W, GLA_TILE
    H = W // HEAD_D
    nsub = T // C
    ntile = pb_ref.shape[0] // T

    @pl.when(i == 0)
    def _():
        st_scr[...] = s0_ref[...]

    tri = dec_ref[...]
    mask4 = jnp.concatenate([tri.astype(F32)] * H, axis=0)
    head_t = _lane_group((T, W), HEAD_D)
    bd = (lax.broadcasted_iota(jnp.int32, (W, W), 0) // HEAD_D) == _lane_group((W, W), HEAD_D)
    tiles = list(range(ntile - 1, -1, -1)) if rev else list(range(ntile))
    subs = list(range(nsub - 1, -1, -1)) if rev else list(range(nsub))
    rows = lambda a: slice(a * T, (a + 1) * T)
    sub = lambda j: slice(j * C, (j + 1) * C)
    stack = lambda blocks: jnp.concatenate([blocks[j] for j in range(nsub)], axis=0)

    wg_hi, wg_lo = _split(wg_ref[...])
    logg = {}
    for a in tiles:
        z = _dot(pg_ref[rows(a), :], wg_hi) + _dot(pg_ref[rows(a), :], wg_lo) + bg_ref[...]
        logg[a] = (jnp.minimum(z, 0.0) - jnp.log(1.0 + jnp.exp(-jnp.abs(z)))) * (1.0 / GATE_TEMP)
    cums = {}
    for a in tiles:
        g_hi, g_lo = _split(logg[a])
        cums[a] = _dot(tri, g_hi) + _dot(tri, g_lo)
    qin, qoff, kin, kend, vb, tot, ptile = {}, {}, {}, {}, {}, {}, {}
    for a in tiles:
        cum = cums[a]
        q = pb_ref[rows(a), 0:W].astype(F32) * (HEAD_D ** -0.5)
        k = pb_ref[rows(a), W:2 * W].astype(F32)
        last_row = (lambda j: j * C) if rev else (lambda j: j * C + C - 1)
        off, run = {}, jnp.zeros((1, W), F32)
        for j in subs:
            tot[(a, j)] = cum[last_row(j):last_row(j) + 1]
            off[j] = run
            run = run + tot[(a, j)]
        ptile[a] = jnp.exp(run)
        e = jnp.exp(cum)
        qin[a] = (q * e).astype(BF16)
        qoff[a] = (q * (e * jnp.exp(stack({j: jnp.broadcast_to(off[j], (C, W)) for j in subs})))).astype(BF16)
        kin[a] = (k * jnp.exp(-cum)).astype(BF16)
        kend[a] = (k * jnp.exp(stack({j: jnp.broadcast_to(tot[(a, j)], (C, W)) for j in subs}) - cum)).astype(BF16)
        vb[a] = pb_ref[rows(a), 2 * W:3 * W]
    sc = {}
    for a in tiles:
        qst = jnp.concatenate([jnp.where(head_t == h, qin[a], jnp.zeros_like(qin[a])) for h in range(H)], axis=0)
        sc[a] = (_dot_nt(qst, kin[a]) * mask4).astype(BF16)
    ost = {a: _dot(sc[a], vb[a]) for a in tiles}
    kvt = {(a, j): jnp.where(bd, _dot_tn(vb[a][sub(j)], kend[a][sub(j)]), 0.0) for a in tiles for j in subs}
    cross, rend = {}, {}
    for a in tiles:
        r = None
        for j in subs:
            if r is not None:
                cross[(a, j)] = _dot_nt(qin[a][sub(j)], r.astype(BF16))
                r = r * jnp.exp(tot[(a, j)]) + kvt[(a, j)]
            else:
                cross[(a, j)] = jnp.zeros((C, W), F32)
                r = kvt[(a, j)]
        rend[a] = r
    st = st_scr[...]
    inter = {}
    for a in tiles:
        inter[a] = _dot_nt(qoff[a], st.astype(BF16))
        st = st * ptile[a] + rend[a]
    st_scr[...] = st

    for a in tiles:
        o = inter[a] + jnp.concatenate([cross[(a, j)] for j in range(nsub)], axis=0)
        for h in range(H):
            o = o + jnp.where(head_t == h, ost[a][h * T:(h + 1) * T], 0.0)
        if finish:
            o = o + oprev_ref[rows(a), :]
            ms = _dot_x2(o * o, ones_ref[...]) * (1.0 / HEAD_D)
            o = o * lax.rsqrt(ms + EPS) * gn_ref[...]
            r = pb_ref[rows(a), 3 * W:4 * W].astype(F32)
            o_ref[rows(a), :] = (o * (r * _sigmoid(r))).astype(o_ref.dtype)
        else:
            o_ref[rows(a), :] = o

    @pl.when(i == pl.num_programs(1) - 1)
    def _():
        sfin_ref[...] = st_scr[...]


def _gla_decay_matrices():
    T, C = GLA_TILE, GLA_CHUNK
    t = jnp.arange(T)[:, None]
    s = jnp.arange(T)[None, :]
    same = (t // C) == (s // C)
    return jnp.stack([same & (s <= t), same & (s >= t)]).astype(BF16)


def _gla_pass(l, d, pb3, pg3, wg, bg, s0, ones_bd, dec, oprev, gn, tt):
    B, n, _ = pb3.shape
    nt = n // tt
    rev = d == 1
    finish = oprev is not None
    W = GROUP_W

    def tok(b, i):
        return (b, (nt - 1 - i) if rev else i, 0)

    in_specs = [
        pl.BlockSpec((None, tt, 4 * W), tok),
        pl.BlockSpec((None, tt, LANES), tok),
        pl.BlockSpec((None, None, LANES, W), lambda b, i: (l, d, 0, 0)),
        pl.BlockSpec((None, None, 1, W), lambda b, i: (l, d, 0, 0)),
        pl.BlockSpec((None, W, W), lambda b, i: (b, 0, 0)),
        pl.BlockSpec((W, W), lambda b, i: (0, 0)),
        pl.BlockSpec((None, GLA_TILE, GLA_TILE), lambda b, i: (d, 0, 0)),
    ]
    args = [pb3, pg3, wg, bg, s0, ones_bd, dec]
    if finish:
        in_specs += [pl.BlockSpec((None, tt, W), tok), _vec_spec(l, W)]
        args += [oprev, gn]
    return pl.pallas_call(
        functools.partial(_gla_kernel, rev=rev, finish=finish),
        grid=(B, nt),
        in_specs=in_specs,
        out_specs=[pl.BlockSpec((None, tt, W), tok), pl.BlockSpec((None, W, W), lambda b, i: (b, 0, 0))],
        out_shape=[jax.ShapeDtypeStruct((B, n, W), BF16 if finish else F32),
                   jax.ShapeDtypeStruct((B, W, W), F32)],
        scratch_shapes=[pltpu.VMEM((W, W), F32)],
        compiler_params=_cparams("arbitrary", "arbitrary"),
        name="gla_bwd" if rev else "gla_fwd",
    )(*args)


def _gla(l, pb_c, pg_c, pb_x, pg_x, wg, bg, gn, ones_bd, tt_c, tt_x):
    B = pb_x.shape[0]
    zero = jnp.zeros((B, GROUP_W, GROUP_W), F32)
    dec = _gla_decay_matrices()
    ofc, sfc = _gla_pass(l, 0, pb_c, pg_c, wg, bg, zero, ones_bd, dec, None, None, tt_c)
    ofx, _ = _gla_pass(l, 0, pb_x, pg_x, wg, bg, sfc, ones_bd, dec, None, None, tt_x)
    ob_c, sbc = _gla_pass(l, 1, pb_c, pg_c, wg, bg, zero, ones_bd, dec, ofc, gn, tt_c)
    ob_x, _ = _gla_pass(l, 1, pb_x, pg_x, wg, bg, sbc, ones_bd, dec, ofx, gn, tt_x)
    return ob_c, ob_x


def _s5_matrices(a_re, a_im, log_dt, b_re, b_im, c_re, c_im, d_skip):
    T = S5_T
    G, P = a_re.shape[1:]
    I = b_re.shape[-1]
    lam = lax.complex(a_re.astype(F32), a_im.astype(F32))
    ldt = lam * jnp.exp(log_dt.astype(F32))[..., None]
    lam_bar = jnp.exp(ldt)
    b_bar = ((lam_bar - 1.0) / lam)[..., None] * lax.complex(b_re.astype(F32), b_im.astype(F32))
    cmat = lax.complex(c_re.astype(F32), c_im.astype(F32))
    steps = jnp.arange(T + 1, dtype=F32)
    pw = jnp.exp(ldt[..., None] * steps)
    taps = jnp.einsum('dgop,dgpk,dgpi->dgiko', cmat, pw[..., :T], b_bar).real
    taps = taps.at[0, :, :, 0, :].add(jnp.eye(I, dtype=F32)[None] * d_skip.astype(F32)[:, :, None])
    row = T * I
    zeros = jnp.zeros((G, I, row), F32)
    fwd = jnp.concatenate([zeros, taps[0].reshape(G, I, row)], axis=-1)
    bwd = jnp.concatenate([jnp.flip(taps[1], axis=2).reshape(G, I, row), zeros], axis=-1)
    m = jnp.stack([fwd[..., (T - s) * I:(T - s) * I + row] + bwd[..., (T - 1 - s) * I:(T - 1 - s) * I + row]
                   for s in range(T)], axis=1).reshape(G, row, row)

    ar = jnp.arange(T)
    pf = pw[0][..., T - 1 - ar]
    pb = pw[1][..., ar]
    bf = jnp.einsum('gps,gpi->gsip', pf, b_bar[0]).reshape(G, T * I, P)
    bb = jnp.einsum('gps,gpi->gsip', pb, b_bar[1]).reshape(G, T * I, P)
    bmat = jnp.concatenate([bf.real, bf.imag, bf.imag, bf.real, bb.real, bb.imag, bb.imag, bb.real], axis=-1)

    cf = jnp.einsum('gop,gpt->gpto', cmat[0], pw[0][..., 1 + ar]).reshape(G, P, T * I)
    cb = jnp.einsum('gop,gpt->gpto', cmat[1], pw[1][..., T - ar]).reshape(G, P, T * I)
    w = jnp.concatenate([m, cf.real, -cf.imag, cb.real, -cb.imag], axis=1)

    a = pw[..., T]
    acoef = jnp.stack([jnp.concatenate([a[0].real, a[0].real], axis=-1), jnp.concatenate([-a[0].imag, a[0].imag], axis=-1),
                       jnp.concatenate([a[1].real, a[1].real], axis=-1), jnp.concatenate([-a[1].imag, a[1].imag], axis=-1)])
    return bmat.astype(BF16), w.astype(BF16), acoef


def _s5_scan_kernel(xf_ref, xb_ref, bm_ref, a_ref, h0_ref, hf_ref, hb_ref, hfin_ref, st_scr, sf_scr, sb_scr, hf_scr, hb_scr):
    j = pl.program_id(1)
    G, tc, _ = xf_ref.shape
    pitch = tc + S5_PITCH_PAD
    half = hf_ref.shape[-1]

    @pl.when(j == 0)
    def _():
        st_scr[...] = h0_ref[...]

    for g in range(G):
        sf = _dot(xf_ref[g], bm_ref[g, :, 0:2 * half])
        sb = _dot(xb_ref[g], bm_ref[g, :, 2 * half:4 * half])
        for k in range(2):
            sf_scr[k, g * pitch:g * pitch + tc, :] = sf[:, k * half:(k + 1) * half]
            sb_scr[k, g * pitch:g * pitch + tc, :] = sb[:, k * half:(k + 1) * half]

    a1f, a2f, a1b, a2b = a_ref[0], a_ref[1], a_ref[2], a_ref[3]
    chunk = lambda i: pl.ds(i, G, stride=pitch)

    def body(i, hs):
        h1f, h2f, h1b, h2b = hs
        ib = tc - 1 - i
        hf_scr[chunk(i), :] = h1f
        hb_scr[chunk(ib), :] = h1b
        return (h1f * a1f + h2f * a2f + sf_scr[0, chunk(i), :], h2f * a1f - h1f * a2f + sf_scr[1, chunk(i), :],
                h1b * a1b + h2b * a2b + sb_scr[0, chunk(ib), :], h2b * a1b - h1b * a2b + sb_scr[1, chunk(ib), :])

    hs = lax.fori_loop(0, tc, body, (st_scr[0], st_scr[1], st_scr[2], st_scr[3]))
    for k in range(4):
        st_scr[k] = hs[k]
    for g in range(G):
        hf_ref[g] = hf_scr[g * pitch:g * pitch + tc, :]
        hb_ref[g] = hb_scr[g * pitch:g * pitch + tc, :]

    @pl.when(j == pl.num_programs(1) - 1)
    def _():
        hfin_ref[...] = st_scr[...]


def _s5_scan(l, xg, bmat, acoef, h0, B, tc):
    G, R, K = xg.shape
    nt = R // B // tc
    half = bmat.shape[-1] // 4
    fwd = lambda b, j: (0, b * nt + j, 0)
    bwd = lambda b, j: (0, b * nt + nt - 1 - j, 0)
    st_spec = pl.BlockSpec((None, 4, G, half), lambda b, j: (b, 0, 0, 0))
    pitch = tc + S5_PITCH_PAD
    return pl.pallas_call(
        _s5_scan_kernel,
        grid=(B, nt),
        in_specs=[pl.BlockSpec((G, tc, K), fwd),
                  pl.BlockSpec((G, tc, K), bwd),
                  pl.BlockSpec((None, G, K, 4 * half), lambda b, j: (l, 0, 0, 0)),
                  pl.BlockSpec((None, 4, G, half), lambda b, j: (l, 0, 0, 0)),
                  st_spec],
        out_specs=[pl.BlockSpec((G, tc, half), fwd), pl.BlockSpec((G, tc, half), bwd), st_spec],
        out_shape=[jax.ShapeDtypeStruct((G, R, half), F32), jax.ShapeDtypeStruct((G, R, half), F32),
                   jax.ShapeDtypeStruct((B, 4, G, half), F32)],
        scratch_shapes=[pltpu.VMEM((4, G, half), F32),
                        pltpu.VMEM((2, G * pitch, half), F32), pltpu.VMEM((2, G * pitch, half), F32),
                        pltpu.VMEM((G * pitch, half), F32), pltpu.VMEM((G * pitch, half), F32)],
        compiler_params=_cparams("arbitrary", "arbitrary"),
        name="s5_scan",
    )(xg, xg, bmat, acoef, h0)


def _s5_gather_kernel(x_ref, o_ref, xs_scr, xt_scr):
    G, tc, _ = o_ref.shape
    nh = x_ref.shape[1] // LANES
    gh = G // nh
    for h in range(nh):
        xs_scr[h] = x_ref[:, h * LANES:(h + 1) * LANES].astype(F32)
    for t in range(S5_T):
        for h in range(nh):
            rt = xs_scr[h, pl.ds(t, tc, stride=S5_T), :].T
            for g in range(gh):
                xt_scr[h * gh + g, t * S5_IN:(t + 1) * S5_IN, :] = rt[g * S5_IN:(g + 1) * S5_IN, :]
    for g in range(G):
        o_ref[g] = xt_scr[g].T.astype(o_ref.dtype)


def _s5_gather(x2, tc):
    N, W = x2.shape
    G = W // S5_IN
    R = N // S5_T
    return pl.pallas_call(
        _s5_gather_kernel,
        grid=(R // tc,),
        in_specs=[pl.BlockSpec((tc * S5_T, W), lambda i: (i, 0))],
        out_specs=pl.BlockSpec((G, tc, S5_T * S5_IN), lambda i: (0, i, 0)),
        out_shape=jax.ShapeDtypeStruct((G, R, S5_T * S5_IN), BF16),
        scratch_shapes=[pltpu.VMEM((W // LANES, tc * S5_T, LANES), F32), pltpu.VMEM((G, S5_T * S5_IN, tc), F32)],
        compiler_params=_cparams("parallel"),
        name="s5_gather",
    )(x2)


def _s5_scatter_kernel(x_ref, hf_ref, hb_ref, wy_ref, w_ref, b_ref, o_ref, yt_scr, tok_scr):
    G, tc, k = x_ref.shape
    kh = hf_ref.shape[-1]
    nh = tok_scr.shape[0]
    gh = G // nh
    for g in range(G):
        y = _dot(x_ref[g], wy_ref[g, 0:k, :])
        y = y + _dot(hf_ref[g].astype(BF16), wy_ref[g, k:k + kh, :])
        y = y + _dot(hb_ref[g].astype(BF16), wy_ref[g, k + kh:, :])
        yt = y.T
        for t in range(S5_T):
            yt_scr[t, g // gh, (g % gh) * S5_IN:(g % gh + 1) * S5_IN, :] = yt[t * S5_IN:(t + 1) * S5_IN, :]
    for t in range(S5_T):
        for h in range(nh):
            tok_scr[h, pl.ds(t, tc, stride=S5_T), :] = yt_scr[t, h].T
    y = _gelu(jnp.concatenate([tok_scr[h] for h in range(nh)], axis=1))
    gate = _sigmoid(_dot(y.astype(BF16), w_ref[...]) + b_ref[...])
    o_ref[...] = (y * gate).astype(o_ref.dtype)


def _s5_scatter_finish(l, xg, hf, hb, wy, w_glu, b_glu, tc):
    G, R, K = xg.shape
    kh = hf.shape[-1]
    W = G * S5_IN
    grp = lambda width: pl.BlockSpec((G, tc, width), lambda i: (0, i, 0))
    return pl.pallas_call(
        _s5_scatter_kernel,
        grid=(R // tc,),
        in_specs=[grp(K), grp(kh), grp(kh),
                  pl.BlockSpec((None, G, K + 2 * kh, K), lambda i: (l, 0, 0, 0)),
                  pl.BlockSpec((None, W, W), lambda i: (l, 0, 0)),
                  _vec_spec(l, W)],
        out_specs=pl.BlockSpec((tc * S5_T, W), lambda i: (i, 0)),
        out_shape=jax.ShapeDtypeStruct((R * S5_T, W), BF16),
        scratch_shapes=[pltpu.VMEM((S5_T, W // LANES, LANES, tc), F32), pltpu.VMEM((W // LANES, tc * S5_T, LANES), F32)],
        compiler_params=_cparams("parallel"),
        name="s5_scatter",
    )(xg, hf, hb, wy, w_glu, b_glu)


def _s5(l, pc_c, pc_x, mats, w_glu, b_glu, need_ctx):
    bmat, w, acoef = mats
    B, n, W = pc_x.shape
    nctx = pc_c.shape[1]
    G = W // S5_IN
    tc_c = min(S5_TILE, B * nctx // S5_T)
    tc_x = min(S5_TILE, n // S5_T)
    xg_c = _s5_gather(pc_c.reshape(B * nctx, W), tc_c)
    xg_x = _s5_gather(pc_x.reshape(B * n, W), tc_x)
    h0 = jnp.zeros((B, 4, G, acoef.shape[-1]), F32)
    hf_c, hb_c, h1 = _s5_scan(l, xg_c, bmat, acoef, h0, B, nctx // S5_T)
    hf_x, hb_x, _ = _s5_scan(l, xg_x, bmat, acoef, h1, B, tc_x)
    oc_x = _s5_scatter_finish(l, xg_x, hf_x, hb_x, w, w_glu, b_glu, tc_x)
    oc_c = _s5_scatter_finish(l, xg_c, hf_c, hb_c, w, w_glu, b_glu, tc_c) if need_ctx else None
    return oc_c, oc_x


def _attn_kernel(q_ref, k_ref, v_ref, o_ref, m_scr, l_scr, acc_scr):
    kv = pl.program_id(2)
    HP = MLA_HEAD_PAD
    tk = k_ref.shape[0]

    @pl.when(kv == 0)
    def _():
        m_scr[...] = jnp.full(m_scr.shape, -jnp.inf, F32)
        l_scr[...] = jnp.zeros(l_scr.shape, F32)
        acc_scr[...] = jnp.zeros(acc_scr.shape, F32)

    def scores(h):
        lanes = slice(h * HP, (h + 1) * HP)
        return _dot_nt(q_ref[:, lanes], k_ref[:, lanes])

    s_next = scores(0)
    for h in range(MLA_HEADS):
        lanes = slice(h * HP, (h + 1) * HP)
        s = s_next
        if h + 1 < MLA_HEADS:
            s_next = scores(h + 1)
        m_prev = m_scr[h]
        m_new = jnp.maximum(m_prev, jnp.max(s, axis=1, keepdims=True))
        alpha = jnp.exp2(m_prev - m_new)
        p = jnp.exp2(s - m_new[:, 0:1])
        lp = p[:, 0:LANES]
        for c in range(1, tk // LANES):
            lp = lp + p[:, c * LANES:(c + 1) * LANES]
        l_scr[h] = alpha * l_scr[h] + lp
        acc_scr[h] = alpha * acc_scr[h] + _dot(p.astype(BF16), v_ref[:, lanes])
        m_scr[h] = m_new

    @pl.when(kv == pl.num_programs(2) - 1)
    def _():
        low = lax.broadcasted_iota(jnp.int32, acc_scr.shape[1:], 1) < MLA_V
        norm = lambda h: acc_scr[h] * (1.0 / jnp.sum(l_scr[h], axis=1, keepdims=True))
        outs = [jnp.where(low, norm(h), norm(h + 1)) for h in range(0, MLA_HEADS, 2)]
        o_ref[...] = jnp.concatenate(outs, axis=1).astype(o_ref.dtype)


def _attention(q3, k3, v3, tq, tk, k0, nk):
    B, nq, HP = q3.shape
    kb = k0 // tk
    return pl.pallas_call(
        _attn_kernel,
        grid=(B, nq // tq, nk // tk),
        in_specs=[pl.BlockSpec((None, tq, HP), lambda b, i, j: (b, i, 0)),
                  pl.BlockSpec((None, tk, HP), lambda b, i, j: (b, kb + j, 0)),
                  pl.BlockSpec((None, tk, HP), lambda b, i, j: (b, kb + j, 0))],
        out_specs=pl.BlockSpec((None, tq, MLA_HEADS * MLA_V), lambda b, i, j: (b, i, 0)),
        out_shape=jax.ShapeDtypeStruct((B, nq, MLA_HEADS * MLA_V), BF16),
        scratch_shapes=[pltpu.VMEM((MLA_HEADS, tq, LANES), F32),
                        pltpu.VMEM((MLA_HEADS, tq, LANES), F32),
                        pltpu.VMEM((MLA_HEADS, tq, MLA_HEAD_PAD), F32)],
        compiler_params=_cparams("parallel", "parallel", "arbitrary"),
        name="attention",
    )(q3, k3, v3)


def _outproj_kernel(oa, ob, oc, od, w_ref, x_ref, gpost_ref, gt_ref, gpre_ref, sc_ref, sh_ref, xo_ref, h_ref):
    W = GROUP_W
    mix = _dot(oa[...], w_ref[0:W, :])
    mix = mix + _dot(ob[...], w_ref[W:2 * W, :])
    mix = mix + _dot(oc[...], w_ref[2 * W:3 * W, :])
    mix = mix + _dot(od[...], w_ref[3 * W:4 * W, :])
    x = x_ref[...] + gt_ref[...] * (_rms(mix) * gpost_ref[...])
    xo_ref[...] = x
    h_ref[...] = (_rms(x) * gpre_ref[...] * (1.0 + sc_ref[...]) + sh_ref[...]).astype(h_ref.dtype)


def _outproj(l, parts, w_out, x2, mod4, brow, g_post, g_pre_ffn, tm):
    R, D = x2.shape
    W = GROUP_W
    part_spec = pl.BlockSpec((tm, W), lambda i: (i, 0))
    row_spec = pl.BlockSpec((tm, D), lambda i: (i, 0))
    return pl.pallas_call(
        _outproj_kernel,
        grid=(R // tm,),
        in_specs=[part_spec] * 4 + [
            pl.BlockSpec((None, 4 * W, D), lambda i: (l, 0, 0)),
            row_spec,
            _vec_spec(l, D),
            _mod_spec(l, 2, D, brow),
            _vec_spec(l, D),
            _mod_spec(l, 4, D, brow),
            _mod_spec(l, 3, D, brow),
        ],
        out_specs=[row_spec, row_spec],
        out_shape=[jax.ShapeDtypeStruct((R, D), F32), jax.ShapeDtypeStruct((R, D), BF16)],
        compiler_params=_cparams("parallel"),
        name="outproj",
    )(*parts, w_out, x2, g_post, mod4, g_pre_ffn, mod4, mod4)


FFN_HALO = 16
FFN_COLS = 256


def _ffn_kernel(hp_ref, h_ref, hn_ref, wup_ref, cw_ref, cb_ref, wdn_ref, x_ref, gpost_ref, gt_ref, o_ref, y_scr):
    i = pl.program_id(1)
    tm = h_ref.shape[0]
    dff = wdn_ref.shape[0]
    use = FFN_HALO // 2
    rows = tm + 2 * use
    prev = jnp.where(i == 0, jnp.zeros_like(hp_ref[...]), hp_ref[...])[FFN_HALO - use:]
    nxt = jnp.where(i == pl.num_programs(1) - 1, jnp.zeros_like(hn_ref[...]), hn_ref[...])[:use]
    hb = jnp.concatenate([prev, h_ref[...], nxt], axis=0)

    def conv(z, cols):
        w = cw_ref[:, cols]
        out = cb_ref[:, cols] + w[0:1] * pltpu.roll(z, 1, axis=0) + w[1:2] * z + w[2:3] * pltpu.roll(z, rows - 1, axis=0)
        return out[use:use + tm]

    def cols(j):
        return slice(j * FFN_COLS, (j + 1) * FFN_COLS), slice(dff + j * FFN_COLS, dff + (j + 1) * FFN_COLS)

    def up(j):
        ca, cg = cols(j)
        return _dot(hb, wup_ref[:, ca]), _dot(hb, wup_ref[:, cg])

    nchunks = dff // FFN_COLS
    z_next = up(0)
    for j in range(nchunks):
        ca, cg = cols(j)
        za, zg = z_next
        if j + 1 < nchunks:
            z_next = up(j + 1)
        y_scr[:, ca] = _gelu_gate(conv(za, ca), conv(zg, cg)).astype(BF16)
    acc = _dot(y_scr[...], wdn_ref[...])
    o_ref[...] = x_ref[...] + gt_ref[...] * (_rms(acc) * gpost_ref[...])


def _ffn(l, h3, x3, w_up, conv_w, conv_b, w_down, mod4, brow, g_post, tm):
    B, n, D = x3.shape
    nt = n // tm
    hb = tm // FFN_HALO
    nh = n // FFN_HALO
    dff = w_down.shape[1]
    once = pl.Buffered(1)
    return pl.pallas_call(
        _ffn_kernel,
        grid=(B, nt),
        in_specs=[
            pl.BlockSpec((None, FFN_HALO, D), lambda b, i: (b, jnp.maximum(i * hb - 1, 0), 0)),
            pl.BlockSpec((None, tm, D), lambda b, i: (b, i, 0)),
            pl.BlockSpec((None, FFN_HALO, D), lambda b, i: (b, jnp.minimum((i + 1) * hb, nh - 1), 0)),
            pl.BlockSpec((None, D, 2 * dff), lambda b, i: (l, 0, 0), pipeline_mode=once),
            pl.BlockSpec((None, 3, 2 * dff), lambda b, i: (l, 0, 0)),
            pl.BlockSpec((None, 1, 2 * dff), lambda b, i: (l, 0, 0)),
            pl.BlockSpec((None, dff, D), lambda b, i: (l, 0, 0), pipeline_mode=once),
            pl.BlockSpec((None, tm, D), lambda b, i: (b, i, 0)),
            _vec_spec(l, D),
            _mod_spec(l, 5, D, brow),
        ],
        out_specs=pl.BlockSpec((None, tm, D), lambda b, i: (b, i, 0)),
        out_shape=jax.ShapeDtypeStruct((B, n, D), F32),
        scratch_shapes=[pltpu.VMEM((tm, dff), BF16)],
        compiler_params=_cparams("parallel", "parallel"),
        name="conv_ffn",
    )(h3, h3, h3, w_up, conv_w, conv_b, w_down, x3, g_post, mod4)


def _prep_w_in(w_in):
    w_in = w_in.astype(BF16)
    z = lambda n: jnp.zeros(w_in.shape[:-1] + (n,), w_in.dtype)
    a = w_in[..., 0:512]
    b = w_in[..., 512:1536]
    gl = w_in[..., 1536:1568]
    c = w_in[..., 1568:1824]
    cq = w_in[..., 1824:2048]
    ckv_kr = w_in[..., 2048:2176]
    return jnp.concatenate([a, b, c, cq, z(32), ckv_kr, gl, z(96)], axis=-1)


def _rope_swap(t):
    q = MLA_ROPE // 4
    return jnp.concatenate([t[..., q:2 * q], t[..., 0:q], t[..., 3 * q:4 * q], t[..., 2 * q:3 * q]], axis=-1)


def _prep_mla(w_uq, w_ukv):
    L = w_uq.shape[0]
    H, NP, RP, HP = MLA_HEADS, MLA_NOPE, MLA_ROPE, MLA_HEAD_PAD
    wq = w_uq.reshape(L, MLA_Q_RANK, H, NP + RP)
    zq = jnp.zeros((L, MLA_Q_RANK, H, HP - NP - RP), w_uq.dtype)
    znope = jnp.zeros((L, MLA_Q_RANK, H, NP), w_uq.dtype)
    q_main = jnp.concatenate([wq, zq], axis=-1)
    q_swap = jnp.concatenate([znope, _rope_swap(wq[..., NP:]), zq], axis=-1)
    padq = lambda w: jnp.pad(w.reshape(L, MLA_Q_RANK, H * HP), ((0, 0), (0, MLA_Q_LANES - MLA_Q_RANK), (0, 0)))

    wkv = w_ukv.reshape(L, MLA_KV_RANK, H, NP + MLA_V)
    zk = jnp.zeros((L, MLA_KV_RANK, H, HP - NP), w_ukv.dtype)
    k_lat = jnp.concatenate([wkv[..., :NP], zk], axis=-1)
    eye = jnp.eye(RP, dtype=w_ukv.dtype)
    place = lambda e: jnp.broadcast_to(
        jnp.concatenate([jnp.zeros((RP, NP), e.dtype), e, jnp.zeros((RP, HP - NP - RP), e.dtype)], axis=-1)[None, :, None, :],
        (L, RP, H, HP))
    k_main = jnp.concatenate([k_lat, place(eye)], axis=1)
    k_swap = jnp.concatenate([jnp.zeros_like(k_lat), place(_rope_swap(eye))], axis=1)
    zv = jnp.zeros((L, MLA_KV_RANK, H, HP - MLA_V), w_ukv.dtype)
    odd = (jnp.arange(H) % 2 == 1)[None, None, :, None]
    v_lat = jnp.where(odd, jnp.concatenate([zv, wkv[..., NP:]], axis=-1), jnp.concatenate([wkv[..., NP:], zv], axis=-1))
    v_main = jnp.concatenate([v_lat, jnp.zeros((L, RP, H, HP), w_ukv.dtype)], axis=1)
    flat = lambda w: w.reshape(L, w.shape[1], H * HP).astype(BF16)
    return (padq(q_main).astype(BF16), padq(q_swap).astype(BF16), flat(k_main), flat(k_swap), flat(v_main))


def _rope_tables(n):
    rows = n // GRID_W
    nf = MLA_ROPE // 4
    inv = ROPE_BASE ** (-jnp.arange(nf, dtype=F32) / nf)
    ar = jnp.arange(rows, dtype=F32)[:, None] * inv[None, :]
    ac = jnp.arange(GRID_W, dtype=F32)[:, None] * inv[None, :]
    by_row = lambda t: jnp.repeat(t, GRID_W, axis=0)
    by_col = lambda t: jnp.tile(t, (rows, 1))
    cr, sr, cc, sn = by_row(jnp.cos(ar)), by_row(jnp.sin(ar)), by_col(jnp.cos(ac)), by_col(jnp.sin(ac))
    one = jnp.ones((n, MLA_NOPE), F32)
    zero = jnp.zeros((n, MLA_HEAD_PAD - MLA_NOPE - MLA_ROPE), F32)
    cos = jnp.concatenate([one, cr, cr, cc, cc, zero], axis=1)
    sin = jnp.concatenate([0.0 * one, -sr, sr, -sn, sn, zero], axis=1)
    return cos, sin


def _pick_tile(n, want):
    t = min(n, want)
    while n % t:
        t //= 2
    return t


def kernel(x, c, ctx, c_ctx, w_mod, b_mod, g_pre_mix, g_post_mix, g_pre_ffn, g_post_ffn, w_in,
           sgu_norm, sgu_w, sgu_b, gla_w_gate, gla_b_gate, gla_norm,
           s5_a_re, s5_a_im, s5_log_dt, s5_b_re, s5_b_im, s5_c_re, s5_c_im, s5_d, s5_w_glu, s5_b_glu,
           mla_q_norm, mla_w_uq, mla_kv_norm, mla_w_ukv, w_out,
           ffn_w_up, ffn_conv_w, ffn_conv_b, ffn_w_down):
    B, n, D = x.shape
    nctx = ctx.shape[1]
    L = w_mod.shape[0]
    W = GROUP_W
    assert B < 8 and n % TOKEN_TILE == 0 and nctx % GLA_TILE == 0 and n % GRID_W == 0

    c8 = jnp.concatenate([c, c_ctx[None, :], jnp.zeros((8 - B - 1, D), F32)], axis=0)
    mod4 = _modulation(c8, w_mod, b_mod).reshape(L, 8, 1, 6 * D)
    vec = lambda p: p.reshape(L, 1, -1).astype(F32)
    g_pre_mix, g_post_mix, g_pre_ffn, g_post_ffn = map(vec, (g_pre_mix, g_post_mix, g_pre_ffn, g_post_ffn))
    w_in_p = _prep_w_in(w_in)
    sgu_gn = vec(sgu_norm)
    sgu_w_st = sgu_w.reshape(L, -1, MLP_CHUNK).astype(BF16)
    sgu_bias = jnp.repeat(jnp.swapaxes(sgu_b, 1, 2), HEAD_D, axis=2).astype(F32)
    ones_bd = jnp.kron(jnp.eye(W // HEAD_D, dtype=F32), jnp.ones((HEAD_D, HEAD_D), F32)).astype(BF16)
    gla_wg = jnp.zeros((L, 2, LANES, W), F32)
    gla_wg = gla_wg.at[:, 0, 0:GATE_RANK].set(gla_w_gate[:, 0]).at[:, 1, GATE_RANK:2 * GATE_RANK].set(gla_w_gate[:, 1])
    gla_bg = gla_b_gate.reshape(L, 2, 1, W).astype(F32)
    gla_gn = vec(gla_norm)
    s5_wglu = s5_w_glu.astype(BF16)
    s5_bglu = vec(s5_b_glu)
    mla_qn = jnp.pad(mla_q_norm, ((0, 0), (0, MLA_Q_LANES - MLA_Q_RANK))).reshape(L, 1, MLA_Q_LANES).astype(F32)
    mla_kvn = jnp.pad(mla_kv_norm, ((0, 0), (0, MLA_KV_LANES - MLA_KV_RANK))).reshape(L, 1, MLA_KV_LANES).astype(F32)
    mla_wts = _prep_mla(mla_w_uq, mla_w_ukv)
    tables = _rope_tables(n)
    w_out_b = w_out.astype(BF16)
    w_up_b = ffn_w_up.astype(BF16)
    w_down_b = ffn_w_down.astype(BF16)
    dff = ffn_w_down.shape[1]
    half_gate = jnp.concatenate([jnp.ones((dff,), F32), jnp.full((dff,), 0.5, F32)])
    conv_w = ffn_conv_w.astype(F32) * half_gate
    conv_b = (ffn_conv_b.astype(F32) * half_gate).reshape(L, 1, -1)

    s5_mats = jax.vmap(_s5_matrices)(s5_a_re, s5_a_im, s5_log_dt, s5_b_re, s5_b_im, s5_c_re, s5_c_im, s5_d)

    tm_x = _pick_tile(n, TOKEN_TILE)
    tm_c = _pick_tile(nctx, TOKEN_TILE)
    tpb_x = n // tm_x
    brow_x = lambda i: i // tpb_x
    tm_in = _pick_tile(n, INPROJ_TILE)
    brow_in = lambda i: i // (n // tm_in)
    brow_c = lambda *g: B
    brow_x2 = lambda b, i: b

    xs = x.reshape(B * n, D)
    cs = ctx.reshape(B * nctx, D)
    nk = n + nctx
    k_all = v_all = jnp.zeros((B, nk, MLA_HEADS * MLA_HEAD_PAD), BF16)

    for l in range(L):
        need_ctx = l < L - 1
        sgu = (sgu_gn, sgu_w_st, sgu_bias, ones_bd)
        wq, wqs, wk, wks, wv = mla_wts
        oa_x, pb_x, pc_x, pg_x, q_x, k_all, v_all = _inproj(
            l, xs, mod4, brow_in, g_pre_mix, w_in_p, sgu, (mla_qn, mla_kvn, mla_wts, tables), tm_in, n, B, 0, nk, (k_all, v_all))
        oa_c, pb_c, pc_c, pg_c, q_c, k_all, v_all = _inproj(
            l, cs, mod4, brow_c, g_pre_mix, w_in_p, sgu, (mla_qn, mla_kvn, (wq, None, wk, None, wv), None), tm_c, nctx, B, n, nk,
            (k_all, v_all))

        r3 = lambda t, m: t.reshape(B, m, t.shape[-1])
        ob_c, ob_x = _gla(l, r3(pb_c, nctx), r3(pg_c, nctx), r3(pb_x, n), r3(pg_x, n),
                          gla_wg, gla_bg, gla_gn, ones_bd, _pick_tile(nctx, GLA_TILE), _pick_tile(n, GLA_STEP_TILES * GLA_TILE))
        oc_c, oc_x = _s5(l, r3(pc_c, nctx), r3(pc_x, n), s5_mats, s5_wglu, s5_bglu, need_ctx)
        tk = next(t for t in ATTN_KV_TILES if nk % t == 0)
        od_x = _attention(r3(q_x, n), k_all, v_all, tm_x, tk, 0, nk).reshape(B * n, W)

        xs, hx = _outproj(l, (oa_x, ob_x.reshape(B * n, W), oc_x, od_x), w_out_b, xs, mod4, brow_x,
                          g_post_mix, g_pre_ffn, tm_x)
        xs = _ffn(l, hx.reshape(B, n, D), xs.reshape(B, n, D), w_up_b, conv_w, conv_b, w_down_b, mod4, brow_x2,
                  g_post_ffn, tm_x).reshape(B * n, D)

        if need_ctx:
            od_c = _attention(r3(q_c, nctx), k_all, v_all, tm_c, nctx, n, nctx).reshape(B * nctx, W)
            cs, hc = _outproj(l, (oa_c, ob_c.reshape(B * nctx, W), oc_c, od_c), w_out_b, cs, mod4, brow_c,
                              g_post_mix, g_pre_ffn, tm_c)
            cs = _ffn(l, hc.reshape(B, nctx, D), cs.reshape(B, nctx, D), w_up_b, conv_w, conv_b, w_down_b, mod4,
                      brow_c, g_post_ffn, tm_c).reshape(B * nctx, D)
    return xs.reshape(B, n, D)
```

```python
import functools

import jax
import jax.numpy as jnp
from jax import lax
from jax.experimental import pallas as pl
from jax.experimental.pallas import tpu as pltpu

F32 = jnp.float32
BF16 = jnp.bfloat16

EPS = 1e-6
GRID_W = 64
GROUP_W = 256
HEAD_D = 64
MLP_CHUNK = 128
GATE_RANK = 16
GATE_TEMP = 16.0
GLA_CHUNK = 64
GLA_TILE = 256
GLA_STEP_TILES = 8
S5_IN = 16
S5_T = 16
S5_TILE = 128
S5_PITCH_PAD = 8
MLA_HEADS = 4
MLA_NOPE = 64
MLA_ROPE = 32
MLA_V = 64
MLA_Q_RANK = 224
MLA_KV_RANK = 96
MLA_HEAD_PAD = 128
ROPE_BASE = 10000.0
LOG2E = 1.4426950408889634

LANES = 128
VMEM_LIMIT = 48 * 1024 * 1024
TOKEN_TILE = 512
INPROJ_TILE = 1024
MOD_COL_TILE = 1536
ATTN_KV_TILES = (2816, 1408, 768, 512, 256, 128)
MLA_Q_LANES = 256
MLA_KV_LANES = 128


def _cparams(*sem):
    return pltpu.CompilerParams(dimension_semantics=sem, vmem_limit_bytes=VMEM_LIMIT)


def _dot(a, b):
    return jnp.dot(a, b, preferred_element_type=F32)


def _dot_nt(a, b):
    return lax.dot_general(a, b, (((1,), (1,)), ((), ())), preferred_element_type=F32)


def _dot_tn(a, b):
    return lax.dot_general(a, b, (((0,), (0,)), ((), ())), preferred_element_type=F32)


def _split(a):
    hi = a.astype(BF16)
    lo = (a - hi.astype(F32)).astype(BF16)
    return hi, lo


def _dot_x2(a, b_bf16):
    hi, lo = _split(a)
    return _dot(hi, b_bf16) + _dot(lo, b_bf16)


def _dot_x3(a, b):
    ah, al = _split(a)
    bh, bl = _split(b)
    return _dot(ah, bh) + _dot(al, bh) + _dot(ah, bl)


def _rms(x):
    return x * lax.rsqrt(jnp.mean(x * x, axis=-1, keepdims=True) + EPS)


def _gelu(x):
    return 0.5 * x * (1.0 + jnp.tanh(0.7978845608028654 * (x + 0.044715 * (x * x * x))))


def _gelu_gate(a, half_g):
    u = a * (0.7978845608028654 + 0.035677408136300125 * (a * a))
    return (a * half_g) * (1.0 + jnp.tanh(u))


def _sigmoid(x):
    return 1.0 / (1.0 + jnp.exp(-x))


def _lane_group(shape, width):
    return lax.broadcasted_iota(jnp.int32, shape, len(shape) - 1) // width


def _mod_kernel(c_ref, w_ref, b_ref, o_ref):
    c = c_ref[...]
    s = c * _sigmoid(c)
    o_ref[...] = _dot_x3(s, w_ref[...]) + b_ref[...]


def _modulation(c8, w_mod, b_mod):
    L, D, W = w_mod.shape
    tn = MOD_COL_TILE
    return pl.pallas_call(
        _mod_kernel,
        grid=(L, W // tn),
        in_specs=[
            pl.BlockSpec((8, D), lambda l, j: (0, 0)),
            pl.BlockSpec((None, D, tn), lambda l, j: (l, 0, j)),
            pl.BlockSpec((None, 1, tn), lambda l, j: (l, 0, j)),
        ],
        out_specs=pl.BlockSpec((None, 8, tn), lambda l, j: (l, 0, j)),
        out_shape=jax.ShapeDtypeStruct((L, 8, W), F32),
        compiler_params=_cparams("arbitrary", "arbitrary"),
        name="modulation",
    )(c8, w_mod, b_mod.reshape(L, 1, W))


def _mod_spec(l, j, D, bfn):
    return pl.BlockSpec((None, None, 1, D), lambda *g: (l, bfn(*g), 0, j))


def _vec_spec(l, width):
    return pl.BlockSpec((None, 1, width), lambda *g: (l, 0, 0))


IN_SLABS = (("a", 0, 2 * GROUP_W), ("b", 512, 4 * GROUP_W), ("c", 1536, GROUP_W), ("d", 1792, MLA_Q_LANES + MLA_KV_LANES),
            ("g", 2176, LANES))
IN_PAD_COLS = 2304


def _sgu_tile(p, gn_ref, w_ref, b_ref, ones_ref, o_ref):
    tm = p.shape[0]
    g = _gelu(p)
    u = g[:, :GROUP_W]
    v = g[:, GROUP_W:]
    ms = _dot_x2(v * v, ones_ref[...]) * (1.0 / HEAD_D)
    vb = (v * lax.rsqrt(ms + EPS) * gn_ref[...]).astype(BF16)
    head = _lane_group((MLP_CHUNK, GROUP_W), HEAD_D)
    w = w_ref[...]
    for c in range(tm // MLP_CHUNK):
        rows = slice(c * MLP_CHUNK, (c + 1) * MLP_CHUNK)
        r = _dot(w, vb[rows])
        s = b_ref[...]
        for h in range(GROUP_W // HEAD_D):
            s = s + jnp.where(head == h, r[h * MLP_CHUNK:(h + 1) * MLP_CHUNK], 0.0)
        o_ref[rows, :] = (u[rows] * s).astype(o_ref.dtype)


def _mla_tile(pd, qn_ref, kvn_ref, wq_ref, wqs_ref, wk_ref, wks_ref, wv_ref, cos_ref, sin_ref, q_ref, k_ref, v_ref):
    cq = pd[:, 0:MLA_Q_LANES]
    ms = jnp.sum(cq * cq, axis=-1, keepdims=True) * (1.0 / MLA_Q_RANK)
    cqn = (cq * lax.rsqrt(ms + EPS) * qn_ref[...]).astype(BF16)
    ck = pd[:, MLA_Q_LANES:MLA_Q_LANES + MLA_KV_LANES]
    lane = lax.broadcasted_iota(jnp.int32, ck.shape, 1)
    is_lat = lane < MLA_KV_RANK
    ms = jnp.sum(jnp.where(is_lat, ck * ck, 0.0), axis=-1, keepdims=True) * (1.0 / MLA_KV_RANK)
    ckn = jnp.where(is_lat, ck * lax.rsqrt(ms + EPS) * kvn_ref[...], ck).astype(BF16)
    q = _dot(cqn, wq_ref[...])
    k = _dot(ckn, wk_ref[...])
    if cos_ref is not None:
        cos = jnp.concatenate([cos_ref[...]] * MLA_HEADS, axis=1)
        sin = jnp.concatenate([sin_ref[...]] * MLA_HEADS, axis=1)
        q = q * cos + _dot(cqn, wqs_ref[...]) * sin
        k = k * cos + _dot(ckn, wks_ref[...]) * sin
    q_ref[...] = (q * ((MLA_NOPE + MLA_ROPE) ** -0.5 * LOG2E)).astype(BF16)
    k_ref[...] = k.astype(BF16)
    v_ref[...] = _dot(ckn, wv_ref[...]).astype(BF16)


def _inproj_kernel(*refs, rope):
    x_ref, g_ref, sc_ref, sh_ref, w_ref, gn_ref, wsp_ref, bsp_ref, ones_ref, qn_ref, kvn_ref = refs[:11]
    if rope:
        wq_ref, wqs_ref, wk_ref, wks_ref, wv_ref, cos_ref, sin_ref = refs[11:18]
    else:
        wq_ref, wk_ref, wv_ref = refs[11:14]
        wqs_ref = wks_ref = cos_ref = sin_ref = None
    oa, ob, oc, og, q_ref, k_ref, v_ref = refs[-7:]
    half = x_ref.shape[0] // 2
    slab = {name: slice(off, off + width) for name, off, width in IN_SLABS}

    def project(rows):
        h = _rms(x_ref[rows, :]) * g_ref[...] * (1.0 + sc_ref[...]) + sh_ref[...]
        hb = h.astype(BF16)
        pa = _dot(hb, w_ref[:, slab["a"]])
        for name, o_ref in (("b", ob), ("c", oc), ("g", og)):
            o_ref[rows, :] = _dot(hb, w_ref[:, slab[name]]).astype(o_ref.dtype)
        return pa, _dot(hb, w_ref[:, slab["d"]])

    def mixers(pa, pd, rows):
        cos, sin = (cos_ref.at[rows], sin_ref.at[rows]) if rope else (None, None)
        _mla_tile(pd, qn_ref, kvn_ref, wq_ref, wqs_ref, wk_ref, wks_ref, wv_ref, cos, sin,
                  q_ref.at[rows], k_ref.at[rows], v_ref.at[rows])
        _sgu_tile(pa, gn_ref, wsp_ref, bsp_ref, ones_ref, oa.at[rows])

    first, second = slice(0, half), slice(half, 2 * half)
    pa0, pd0 = project(first)
    pa1, pd1 = project(second)
    mixers(pa0, pd0, first)
    mixers(pa1, pd1, second)


def _inproj(l, x2, mod4, brow, g_pre, w_in_p, sgu, mla, tm, n, B, row0, nk, kv_into):
    R, D = x2.shape
    gn, w_st, bias, ones_bd = sgu
    qn, kvn, (wq, wqs, wk, wks, wv), tables = mla
    rope = tables is not None
    H = GROUP_W // HEAD_D
    HP = MLA_HEADS * MLA_HEAD_PAD
    QL, KL = MLA_Q_LANES, MLA_KV_LANES
    npt = n // tm
    width = {name: w for name, _, w in IN_SLABS}
    wspec = lambda r: pl.BlockSpec((None, r, HP), lambda i: (l, 0, 0))
    in_specs = [
        pl.BlockSpec((tm, D), lambda i: (i, 0)),
        _vec_spec(l, D),
        _mod_spec(l, 1, D, brow),
        _mod_spec(l, 0, D, brow),
        pl.BlockSpec((None, D, IN_PAD_COLS), lambda i: (l, 0, 0)),
        _vec_spec(l, GROUP_W),
        pl.BlockSpec((None, H * MLP_CHUNK, MLP_CHUNK), lambda i: (l, 0, 0)),
        pl.BlockSpec((None, MLP_CHUNK, GROUP_W), lambda i: (l, 0, 0)),
        pl.BlockSpec((GROUP_W, GROUP_W), lambda i: (0, 0)),
        _vec_spec(l, QL),
        _vec_spec(l, KL),
    ]
    args = [x2, g_pre, mod4, mod4, w_in_p, gn, w_st, bias, ones_bd, qn, kvn]
    if rope:
        tspec = pl.BlockSpec((tm, MLA_HEAD_PAD), lambda i: (i % npt, 0))
        in_specs += [wspec(QL), wspec(QL), wspec(KL), wspec(KL), wspec(KL), tspec, tspec]
        args += [wq, wqs, wk, wks, wv, tables[0], tables[1]]
    else:
        in_specs += [wspec(QL), wspec(KL), wspec(KL)]
        args += [wq, wk, wv]
    aliases = {len(args): 5, len(args) + 1: 6}
    in_specs += [pl.BlockSpec(memory_space=pl.ANY)] * 2
    args += list(kv_into)
    row = lambda w: pl.BlockSpec((tm, w), lambda i: (i, 0))
    kv_spec = pl.BlockSpec((None, tm, HP), lambda i: (i // npt, row0 // tm + i % npt, 0))
    kv_shape = jax.ShapeDtypeStruct((B, nk, HP), BF16)
    return pl.pallas_call(
        functools.partial(_inproj_kernel, rope=rope),
        grid=(R // tm,),
        in_specs=in_specs,
        out_specs=[row(GROUP_W), row(width["b"]), row(width["c"]), row(width["g"]), row(HP), kv_spec, kv_spec],
        out_shape=[jax.ShapeDtypeStruct((R, w), BF16) for w in (GROUP_W, width["b"], width["c"], width["g"], HP)]
        + [kv_shape, kv_shape],
        input_output_aliases=aliases,
        compiler_params=_cparams("parallel"),
        name="inproj",
    )(*args)


def _gla_kernel(*refs, rev, finish):
    if finish:
        (pb_ref, pg_ref, wg_ref, bg_ref, s0_ref, ones_ref, dec_ref, oprev_ref, gn_ref, o_ref, sfin_ref, st_scr) = refs
    else:
        (pb_ref, pg_ref, wg_ref, bg_ref, s0_ref, ones_ref, dec_ref, o_ref, sfin_ref, st_scr) = refs
    i = pl.program_id(1)
    C, W, T = GLA_CHUNK, GROUP_W, GLA_TILE
    H = W // HEAD_D
    nsub = T // C
    ntile = pb_ref.shape[0] // T

    @pl.when(i == 0)
    def _():
        st_scr[...] = s0_ref[...]

    tri = dec_ref[...]
    mask4 = jnp.concatenate([tri.astype(F32)] * H, axis=0)
    head_t = _lane_group((T, W), HEAD_D)
    bd = (lax.broadcasted_iota(jnp.int32, (W, W), 0) // HEAD_D) == _lane_group((W, W), HEAD_D)
    tiles = list(range(ntile - 1, -1, -1)) if rev else list(range(ntile))
    subs = list(range(nsub - 1, -1, -1)) if rev else list(range(nsub))
    rows = lambda a: slice(a * T, (a + 1) * T)
    sub = lambda j: slice(j * C, (j + 1) * C)
    stack = lambda blocks: jnp.concatenate([blocks[j] for j in range(nsub)], axis=0)

    wg_hi, wg_lo = _split(wg_ref[...])
    logg = {}
    for a in tiles:
        z = _dot(pg_ref[rows(a), :], wg_hi) + _dot(pg_ref[rows(a), :], wg_lo) + bg_ref[...]
        logg[a] = (jnp.minimum(z, 0.0) - jnp.log(1.0 + jnp.exp(-jnp.abs(z)))) * (1.0 / GATE_TEMP)
    cums = {}
    for a in tiles:
        g_hi, g_lo = _split(logg[a])
        cums[a] = _dot(tri, g_hi) + _dot(tri, g_lo)
    qin, qoff, kin, kend, vb, tot, ptile = {}, {}, {}, {}, {}, {}, {}
    for a in tiles:
        cum = cums[a]
        q = pb_ref[rows(a), 0:W].astype(F32) * (HEAD_D ** -0.5)
        k = pb_ref[rows(a), W:2 * W].astype(F32)
        last_row = (lambda j: j * C) if rev else (lambda j: j * C + C - 1)
        off, run = {}, jnp.zeros((1, W), F32)
        for j in subs:
            tot[(a, j)] = cum[last_row(j):last_row(j) + 1]
            off[j] = run
            run = run + tot[(a, j)]
        ptile[a] = jnp.exp(run)
        e = jnp.exp(cum)
        qin[a] = (q * e).astype(BF16)
        qoff[a] = (q * (e * jnp.exp(stack({j: jnp.broadcast_to(off[j], (C, W)) for j in subs})))).astype(BF16)
        kin[a] = (k * jnp.exp(-cum)).astype(BF16)
        kend[a] = (k * jnp.exp(stack({j: jnp.broadcast_to(tot[(a, j)], (C, W)) for j in subs}) - cum)).astype(BF16)
        vb[a] = pb_ref[rows(a), 2 * W:3 * W]
    sc = {}
    for a in tiles:
        qst = jnp.concatenate([jnp.where(head_t == h, qin[a], jnp.zeros_like(qin[a])) for h in range(H)], axis=0)
        sc[a] = (_dot_nt(qst, kin[a]) * mask4).astype(BF16)
    ost = {a: _dot(sc[a], vb[a]) for a in tiles}
    kvt = {(a, j): jnp.where(bd, _dot_tn(vb[a][sub(j)], kend[a][sub(j)]), 0.0) for a in tiles for j in subs}
    cross, rend = {}, {}
    for a in tiles:
        r = None
        for j in subs:
            if r is not None:
                cross[(a, j)] = _dot_nt(qin[a][sub(j)], r.astype(BF16))
                r = r * jnp.exp(tot[(a, j)]) + kvt[(a, j)]
            else:
                cross[(a, j)] = jnp.zeros((C, W), F32)
                r = kvt[(a, j)]
        rend[a] = r
    st = st_scr[...]
    inter = {}
    for a in tiles:
        inter[a] = _dot_nt(qoff[a], st.astype(BF16))
        st = st * ptile[a] + rend[a]
    st_scr[...] = st

    for a in tiles:
        o = inter[a] + jnp.concatenate([cross[(a, j)] for j in range(nsub)], axis=0)
        for h in range(H):
            o = o + jnp.where(head_t == h, ost[a][h * T:(h + 1) * T], 0.0)
        if finish:
            o = o + oprev_ref[rows(a), :]
            ms = _dot_x2(o * o, ones_ref[...]) * (1.0 / HEAD_D)
            o = o * lax.rsqrt(ms + EPS) * gn_ref[...]
            r = pb_ref[rows(a), 3 * W:4 * W].astype(F32)
            o_ref[rows(a), :] = (o * (r * _sigmoid(r))).astype(o_ref.dtype)
        else:
            o_ref[rows(a), :] = o

    @pl.when(i == pl.num_programs(1) - 1)
    def _():
        sfin_ref[...] = st_scr[...]


def _gla_decay_matrices():
    T, C = GLA_TILE, GLA_CHUNK
    t = jnp.arange(T)[:, None]
    s = jnp.arange(T)[None, :]
    same = (t // C) == (s // C)
    return jnp.stack([same & (s <= t), same & (s >= t)]).astype(BF16)


def _gla_pass(l, d, pb3, pg3, wg, bg, s0, ones_bd, dec, oprev, gn, tt):
    B, n, _ = pb3.shape
    nt = n // tt
    rev = d == 1
    finish = oprev is not None
    W = GROUP_W

    def tok(b, i):
        return (b, (nt - 1 - i) if rev else i, 0)

    in_specs = [
        pl.BlockSpec((None, tt, 4 * W), tok),
        pl.BlockSpec((None, tt, LANES), tok),
        pl.BlockSpec((None, None, LANES, W), lambda b, i: (l, d, 0, 0)),
        pl.BlockSpec((None, None, 1, W), lambda b, i: (l, d, 0, 0)),
        pl.BlockSpec((None, W, W), lambda b, i: (b, 0, 0)),
        pl.BlockSpec((W, W), lambda b, i: (0, 0)),
        pl.BlockSpec((None, GLA_TILE, GLA_TILE), lambda b, i: (d, 0, 0)),
    ]
    args = [pb3, pg3, wg, bg, s0, ones_bd, dec]
    if finish:
        in_specs += [pl.BlockSpec((None, tt, W), tok), _vec_spec(l, W)]
        args += [oprev, gn]
    return pl.pallas_call(
        functools.partial(_gla_kernel, rev=rev, finish=finish),
        grid=(B, nt),
        in_specs=in_specs,
        out_specs=[pl.BlockSpec((None, tt, W), tok), pl.BlockSpec((None, W, W), lambda b, i: (b, 0, 0))],
        out_shape=[jax.ShapeDtypeStruct((B, n, W), BF16 if finish else F32),
                   jax.ShapeDtypeStruct((B, W, W), F32)],
        scratch_shapes=[pltpu.VMEM((W, W), F32)],
        compiler_params=_cparams("arbitrary", "arbitrary"),
        name="gla_bwd" if rev else "gla_fwd",
    )(*args)


def _gla(l, pb_c, pg_c, pb_x, pg_x, wg, bg, gn, ones_bd, tt_c, tt_x):
    B = pb_x.shape[0]
    zero = jnp.zeros((B, GROUP_W, GROUP_W), F32)
    dec = _gla_decay_matrices()
    ofc, sfc = _gla_pass(l, 0, pb_c, pg_c, wg, bg, zero, ones_bd, dec, None, None, tt_c)
    ofx, _ = _gla_pass(l, 0, pb_x, pg_x, wg, bg, sfc, ones_bd, dec, None, None, tt_x)
    ob_c, sbc = _gla_pass(l, 1, pb_c, pg_c, wg, bg, zero, ones_bd, dec, ofc, gn, tt_c)
    ob_x, _ = _gla_pass(l, 1, pb_x, pg_x, wg, bg, sbc, ones_bd, dec, ofx, gn, tt_x)
    return ob_c, ob_x


def _s5_matrices(a_re, a_im, log_dt, b_re, b_im, c_re, c_im, d_skip):
    T = S5_T
    G, P = a_re.shape[1:]
    I = b_re.shape[-1]
    lam = lax.complex(a_re.astype(F32), a_im.astype(F32))
    ldt = lam * jnp.exp(log_dt.astype(F32))[..., None]
    lam_bar = jnp.exp(ldt)
    b_bar = ((lam_bar - 1.0) / lam)[..., None] * lax.complex(b_re.astype(F32), b_im.astype(F32))
    cmat = lax.complex(c_re.astype(F32), c_im.astype(F32))
    steps = jnp.arange(T + 1, dtype=F32)
    pw = jnp.exp(ldt[..., None] * steps)
    taps = jnp.einsum('dgop,dgpk,dgpi->dgiko', cmat, pw[..., :T], b_bar).real
    taps = taps.at[0, :, :, 0, :].add(jnp.eye(I, dtype=F32)[None] * d_skip.astype(F32)[:, :, None])
    row = T * I
    zeros = jnp.zeros((G, I, row), F32)
    fwd = jnp.concatenate([zeros, taps[0].reshape(G, I, row)], axis=-1)
    bwd = jnp.concatenate([jnp.flip(taps[1], axis=2).reshape(G, I, row), zeros], axis=-1)
    m = jnp.stack([fwd[..., (T - s) * I:(T - s) * I + row] + bwd[..., (T - 1 - s) * I:(T - 1 - s) * I + row]
                   for s in range(T)], axis=1).reshape(G, row, row)

    ar = jnp.arange(T)
    pf = pw[0][..., T - 1 - ar]
    pb = pw[1][..., ar]
    bf = jnp.einsum('gps,gpi->gsip', pf, b_bar[0]).reshape(G, T * I, P)
    bb = jnp.einsum('gps,gpi->gsip', pb, b_bar[1]).reshape(G, T * I, P)
    bmat = jnp.concatenate([bf.real, bf.imag, bf.imag, bf.real, bb.real, bb.imag, bb.imag, bb.real], axis=-1)

    cf = jnp.einsum('gop,gpt->gpto', cmat[0], pw[0][..., 1 + ar]).reshape(G, P, T * I)
    cb = jnp.einsum('gop,gpt->gpto', cmat[1], pw[1][..., T - ar]).reshape(G, P, T * I)
    w = jnp.concatenate([m, cf.real, -cf.imag, cb.real, -cb.imag], axis=1)

    a = pw[..., T]
    acoef = jnp.stack([jnp.concatenate([a[0].real, a[0].real], axis=-1), jnp.concatenate([-a[0].imag, a[0].imag], axis=-1),
                       jnp.concatenate([a[1].real, a[1].real], axis=-1), jnp.concatenate([-a[1].imag, a[1].imag], axis=-1)])
    return bmat.astype(BF16), w.astype(BF16), acoef


def _s5_scan_kernel(xf_ref, xb_ref, bm_ref, a_ref, h0_ref, hf_ref, hb_ref, hfin_ref, st_scr, sf_scr, sb_scr, hf_scr, hb_scr):
    j = pl.program_id(1)
    G, tc, _ = xf_ref.shape
    pitch = tc + S5_PITCH_PAD
    half = hf_ref.shape[-1]

    @pl.when(j == 0)
    def _():
        st_scr[...] = h0_ref[...]

    for g in range(G):
        sf = _dot(xf_ref[g], bm_ref[g, :, 0:2 * half])
        sb = _dot(xb_ref[g], bm_ref[g, :, 2 * half:4 * half])
        for k in range(2):
            sf_scr[k, g * pitch:g * pitch + tc, :] = sf[:, k * half:(k + 1) * half]
            sb_scr[k, g * pitch:g * pitch + tc, :] = sb[:, k * half:(k + 1) * half]

    a1f, a2f, a1b, a2b = a_ref[0], a_ref[1], a_ref[2], a_ref[3]
    chunk = lambda i: pl.ds(i, G, stride=pitch)

    def body(i, hs):
        h1f, h2f, h1b, h2b = hs
        ib = tc - 1 - i
        hf_scr[chunk(i), :] = h1f
        hb_scr[chunk(ib), :] = h1b
        return (h1f * a1f + h2f * a2f + sf_scr[0, chunk(i), :], h2f * a1f - h1f * a2f + sf_scr[1, chunk(i), :],
                h1b * a1b + h2b * a2b + sb_scr[0, chunk(ib), :], h2b * a1b - h1b * a2b + sb_scr[1, chunk(ib), :])

    hs = lax.fori_loop(0, tc, body, (st_scr[0], st_scr[1], st_scr[2], st_scr[3]))
    for k in range(4):
        st_scr[k] = hs[k]
    for g in range(G):
        hf_ref[g] = hf_scr[g * pitch:g * pitch + tc, :]
        hb_ref[g] = hb_scr[g * pitch:g * pitch + tc, :]

    @pl.when(j == pl.num_programs(1) - 1)
    def _():
        hfin_ref[...] = st_scr[...]


def _s5_scan(l, xg, bmat, acoef, h0, B, tc):
    G, R, K = xg.shape
    nt = R // B // tc
    half = bmat.shape[-1] // 4
    fwd = lambda b, j: (0, b * nt + j, 0)
    bwd = lambda b, j: (0, b * nt + nt - 1 - j, 0)
    st_spec = pl.BlockSpec((None, 4, G, half), lambda b, j: (b, 0, 0, 0))
    pitch = tc + S5_PITCH_PAD
    return pl.pallas_call(
        _s5_scan_kernel,
        grid=(B, nt),
        in_specs=[pl.BlockSpec((G, tc, K), fwd),
                  pl.BlockSpec((G, tc, K), bwd),
                  pl.BlockSpec((None, G, K, 4 * half), lambda b, j: (l, 0, 0, 0)),
                  pl.BlockSpec((None, 4, G, half), lambda b, j: (l, 0, 0, 0)),
                  st_spec],
        out_specs=[pl.BlockSpec((G, tc, half), fwd), pl.BlockSpec((G, tc, half), bwd), st_spec],
        out_shape=[jax.ShapeDtypeStruct((G, R, half), F32), jax.ShapeDtypeStruct((G, R, half), F32),
                   jax.ShapeDtypeStruct((B, 4, G, half), F32)],
        scratch_shapes=[pltpu.VMEM((4, G, half), F32),
                        pltpu.VMEM((2, G * pitch, half), F32), pltpu.VMEM((2, G * pitch, half), F32),
                        pltpu.VMEM((G * pitch, half), F32), pltpu.VMEM((G * pitch, half), F32)],
        compiler_params=_cparams("arbitrary", "arbitrary"),
        name="s5_scan",
    )(xg, xg, bmat, acoef, h0)


def _s5_gather_kernel(x_ref, o_ref, xs_scr, xt_scr):
    G, tc, _ = o_ref.shape
    nh = x_ref.shape[1] // LANES
    gh = G // nh
    for h in range(nh):
        xs_scr[h] = x_ref[:, h * LANES:(h + 1) * LANES].astype(F32)
    for t in range(S5_T):
        for h in range(nh):
            rt = xs_scr[h, pl.ds(t, tc, stride=S5_T), :].T
            for g in range(gh):
                xt_scr[h * gh + g, t * S5_IN:(t + 1) * S5_IN, :] = rt[g * S5_IN:(g + 1) * S5_IN, :]
    for g in range(G):
        o_ref[g] = xt_scr[g].T.astype(o_ref.dtype)


def _s5_gather(x2, tc):
    N, W = x2.shape
    G = W // S5_IN
    R = N // S5_T
    return pl.pallas_call(
        _s5_gather_kernel,
        grid=(R // tc,),
        in_specs=[pl.BlockSpec((tc * S5_T, W), lambda i: (i, 0))],
        out_specs=pl.BlockSpec((G, tc, S5_T * S5_IN), lambda i: (0, i, 0)),
        out_shape=jax.ShapeDtypeStruct((G, R, S5_T * S5_IN), BF16),
        scratch_shapes=[pltpu.VMEM((W // LANES, tc * S5_T, LANES), F32), pltpu.VMEM((G, S5_T * S5_IN, tc), F32)],
        compiler_params=_cparams("parallel"),
        name="s5_gather",
    )(x2)


def _s5_scatter_kernel(x_ref, hf_ref, hb_ref, wy_ref, w_ref, b_ref, o_ref, yt_scr, tok_scr):
    G, tc, k = x_ref.shape
    kh = hf_ref.shape[-1]
    nh = tok_scr.shape[0]
    gh = G // nh
    for g in range(G):
        y = _dot(x_ref[g], wy_ref[g, 0:k, :])
        y = y + _dot(hf_ref[g].astype(BF16), wy_ref[g, k:k + kh, :])
        y = y + _dot(hb_ref[g].astype(BF16), wy_ref[g, k + kh:, :])
        yt = y.T
        for t in range(S5_T):
            yt_scr[t, g // gh, (g % gh) * S5_IN:(g % gh + 1) * S5_IN, :] = yt[t * S5_IN:(t + 1) * S5_IN, :]
    for t in range(S5_T):
        for h in range(nh):
            tok_scr[h, pl.ds(t, tc, stride=S5_T), :] = yt_scr[t, h].T
    y = _gelu(jnp.concatenate([tok_scr[h] for h in range(nh)], axis=1))
    gate = _sigmoid(_dot(y.astype(BF16), w_ref[...]) + b_ref[...])
    o_ref[...] = (y * gate).astype(o_ref.dtype)


def _s5_scatter_finish(l, xg, hf, hb, wy, w_glu, b_glu, tc):
    G, R, K = xg.shape
    kh = hf.shape[-1]
    W = G * S5_IN
    grp = lambda width: pl.BlockSpec((G, tc, width), lambda i: (0, i, 0))
    return pl.pallas_call(
        _s5_scatter_kernel,
        grid=(R // tc,),
        in_specs=[grp(K), grp(kh), grp(kh),
                  pl.BlockSpec((None, G, K + 2 * kh, K), lambda i: (l, 0, 0, 0)),
                  pl.BlockSpec((None, W, W), lambda i: (l, 0, 0)),
                  _vec_spec(l, W)],
        out_specs=pl.BlockSpec((tc * S5_T, W), lambda i: (i, 0)),
        out_shape=jax.ShapeDtypeStruct((R * S5_T, W), BF16),
        scratch_shapes=[pltpu.VMEM((S5_T, W // LANES, LANES, tc), F32), pltpu.VMEM((W // LANES, tc * S5_T, LANES), F32)],
        compiler_params=_cparams("parallel"),
        name="s5_scatter",
    )(xg, hf, hb, wy, w_glu, b_glu)


def _s5(l, pc_c, pc_x, mats, w_glu, b_glu, need_ctx):
    bmat, w, acoef = mats
    B, n, W = pc_x.shape
    nctx = pc_c.shape[1]
    G = W // S5_IN
    tc_c = min(S5_TILE, B * nctx // S5_T)
    tc_x = min(S5_TILE, n // S5_T)
    xg_c = _s5_gather(pc_c.reshape(B * nctx, W), tc_c)
    xg_x = _s5_gather(pc_x.reshape(B * n, W), tc_x)
    h0 = jnp.zeros((B, 4, G, acoef.shape[-1]), F32)
    hf_c, hb_c, h1 = _s5_scan(l, xg_c, bmat, acoef, h0, B, nctx // S5_T)
    hf_x, hb_x, _ = _s5_scan(l, xg_x, bmat, acoef, h1, B, tc_x)
    oc_x = _s5_scatter_finish(l, xg_x, hf_x, hb_x, w, w_glu, b_glu, tc_x)
    oc_c = _s5_scatter_finish(l, xg_c, hf_c, hb_c, w, w_glu, b_glu, tc_c) if need_ctx else None
    return oc_c, oc_x


def _attn_kernel(q_ref, k_ref, v_ref, o_ref, m_scr, l_scr, acc_scr):
    kv = pl.program_id(2)
    HP = MLA_HEAD_PAD
    tk = k_ref.shape[0]

    @pl.when(kv == 0)
    def _():
        m_scr[...] = jnp.full(m_scr.shape, -jnp.inf, F32)
        l_scr[...] = jnp.zeros(l_scr.shape, F32)
        acc_scr[...] = jnp.zeros(acc_scr.shape, F32)

    def scores(h):
        lanes = slice(h * HP, (h + 1) * HP)
        return _dot_nt(q_ref[:, lanes], k_ref[:, lanes])

    s_next = scores(0)
    for h in range(MLA_HEADS):
        lanes = slice(h * HP, (h + 1) * HP)
        s = s_next
        if h + 1 < MLA_HEADS:
            s_next = scores(h + 1)
        m_prev = m_scr[h]
        m_new = jnp.maximum(m_prev, jnp.max(s, axis=1, keepdims=True))
        alpha = jnp.exp2(m_prev - m_new)
        p = jnp.exp2(s - m_new[:, 0:1])
        lp = p[:, 0:LANES]
        for c in range(1, tk // LANES):
            lp = lp + p[:, c * LANES:(c + 1) * LANES]
        l_scr[h] = alpha * l_scr[h] + lp
        acc_scr[h] = alpha * acc_scr[h] + _dot(p.astype(BF16), v_ref[:, lanes])
        m_scr[h] = m_new

    @pl.when(kv == pl.num_programs(2) - 1)
    def _():
        low = lax.broadcasted_iota(jnp.int32, acc_scr.shape[1:], 1) < MLA_V
        norm = lambda h: acc_scr[h] * (1.0 / jnp.sum(l_scr[h], axis=1, keepdims=True))
        outs = [jnp.where(low, norm(h), norm(h + 1)) for h in range(0, MLA_HEADS, 2)]
        o_ref[...] = jnp.concatenate(outs, axis=1).astype(o_ref.dtype)


def _attention(q3, k3, v3, tq, tk, k0, nk):
    B, nq, HP = q3.shape
    kb = k0 // tk
    return pl.pallas_call(
        _attn_kernel,
        grid=(B, nq // tq, nk // tk),
        in_specs=[pl.BlockSpec((None, tq, HP), lambda b, i, j: (b, i, 0)),
                  pl.BlockSpec((None, tk, HP), lambda b, i, j: (b, kb + j, 0)),
                  pl.BlockSpec((None, tk, HP), lambda b, i, j: (b, kb + j, 0))],
        out_specs=pl.BlockSpec((None, tq, MLA_HEADS * MLA_V), lambda b, i, j: (b, i, 0)),
        out_shape=jax.ShapeDtypeStruct((B, nq, MLA_HEADS * MLA_V), BF16),
        scratch_shapes=[pltpu.VMEM((MLA_HEADS, tq, LANES), F32),
                        pltpu.VMEM((MLA_HEADS, tq, LANES), F32),
                        pltpu.VMEM((MLA_HEADS, tq, MLA_HEAD_PAD), F32)],
        compiler_params=_cparams("parallel", "parallel", "arbitrary"),
        name="attention",
    )(q3, k3, v3)


def _outproj_kernel(oa, ob, oc, od, w_ref, x_ref, gpost_ref, gt_ref, gpre_ref, sc_ref, sh_ref, xo_ref, h_ref):
    W = GROUP_W
    mix = _dot(oa[...], w_ref[0:W, :])
    mix = mix + _dot(ob[...], w_ref[W:2 * W, :])
    mix = mix + _dot(oc[...], w_ref[2 * W:3 * W, :])
    mix = mix + _dot(od[...], w_ref[3 * W:4 * W, :])
    x = x_ref[...] + gt_ref[...] * (_rms(mix) * gpost_ref[...])
    xo_ref[...] = x
    h_ref[...] = (_rms(x) * gpre_ref[...] * (1.0 + sc_ref[...]) + sh_ref[...]).astype(h_ref.dtype)


def _outproj(l, parts, w_out, x2, mod4, brow, g_post, g_pre_ffn, tm):
    R, D = x2.shape
    W = GROUP_W
    part_spec = pl.BlockSpec((tm, W), lambda i: (i, 0))
    row_spec = pl.BlockSpec((tm, D), lambda i: (i, 0))
    return pl.pallas_call(
        _outproj_kernel,
        grid=(R // tm,),
        in_specs=[part_spec] * 4 + [
            pl.BlockSpec((None, 4 * W, D), lambda i: (l, 0, 0)),
            row_spec,
            _vec_spec(l, D),
            _mod_spec(l, 2, D, brow),
            _vec_spec(l, D),
            _mod_spec(l, 4, D, brow),
            _mod_spec(l, 3, D, brow),
        ],
        out_specs=[row_spec, row_spec],
        out_shape=[jax.ShapeDtypeStruct((R, D), F32), jax.ShapeDtypeStruct((R, D), BF16)],
        compiler_params=_cparams("parallel"),
        name="outproj",
    )(*parts, w_out, x2, g_post, mod4, g_pre_ffn, mod4, mod4)


FFN_HALO = 16
FFN_COLS = 256


def _ffn_kernel(hp_ref, h_ref, hn_ref, wup_ref, cw_ref, cb_ref, wdn_ref, x_ref, gpost_ref, gt_ref, o_ref, y_scr):
    i = pl.program_id(1)
    tm = h_ref.shape[0]
    dff = wdn_ref.shape[0]
    use = FFN_HALO // 2
    rows = tm + 2 * use
    prev = jnp.where(i == 0, jnp.zeros_like(hp_ref[...]), hp_ref[...])[FFN_HALO - use:]
    nxt = jnp.where(i == pl.num_programs(1) - 1, jnp.zeros_like(hn_ref[...]), hn_ref[...])[:use]
    hb = jnp.concatenate([prev, h_ref[...], nxt], axis=0)

    def conv(z, cols):
        w = cw_ref[:, cols]
        out = cb_ref[:, cols] + w[0:1] * pltpu.roll(z, 1, axis=0) + w[1:2] * z + w[2:3] * pltpu.roll(z, rows - 1, axis=0)
        return out[use:use + tm]

    def cols(j):
        return slice(j * FFN_COLS, (j + 1) * FFN_COLS), slice(dff + j * FFN_COLS, dff + (j + 1) * FFN_COLS)

    def up(j):
        ca, cg = cols(j)
        return _dot(hb, wup_ref[:, ca]), _dot(hb, wup_ref[:, cg])

    nchunks = dff // FFN_COLS
    z_next = up(0)
    for j in range(nchunks):
        ca, cg = cols(j)
        za, zg = z_next
        if j + 1 < nchunks:
            z_next = up(j + 1)
        y_scr[:, ca] = _gelu_gate(conv(za, ca), conv(zg, cg)).astype(BF16)
    acc = _dot(y_scr[...], wdn_ref[...])
    o_ref[...] = x_ref[...] + gt_ref[...] * (_rms(acc) * gpost_ref[...])


def _ffn(l, h3, x3, w_up, conv_w, conv_b, w_down, mod4, brow, g_post, tm):
    B, n, D = x3.shape
    nt = n // tm
    hb = tm // FFN_HALO
    nh = n // FFN_HALO
    dff = w_down.shape[1]
    once = pl.Buffered(1)
    return pl.pallas_call(
        _ffn_kernel,
        grid=(B, nt),
        in_specs=[
            pl.BlockSpec((None, FFN_HALO, D), lambda b, i: (b, jnp.maximum(i * hb - 1, 0), 0)),
            pl.BlockSpec((None, tm, D), lambda b, i: (b, i, 0)),
            pl.BlockSpec((None, FFN_HALO, D), lambda b, i: (b, jnp.minimum((i + 1) * hb, nh - 1), 0)),
            pl.BlockSpec((None, D, 2 * dff), lambda b, i: (l, 0, 0), pipeline_mode=once),
            pl.BlockSpec((None, 3, 2 * dff), lambda b, i: (l, 0, 0)),
            pl.BlockSpec((None, 1, 2 * dff), lambda b, i: (l, 0, 0)),
            pl.BlockSpec((None, dff, D), lambda b, i: (l, 0, 0), pipeline_mode=once),
            pl.BlockSpec((None, tm, D), lambda b, i: (b, i, 0)),
            _vec_spec(l, D),
            _mod_spec(l, 5, D, brow),
        ],
        out_specs=pl.BlockSpec((None, tm, D), lambda b, i: (b, i, 0)),
        out_shape=jax.ShapeDtypeStruct((B, n, D), F32),
        scratch_shapes=[pltpu.VMEM((tm, dff), BF16)],
        compiler_params=_cparams("parallel", "parallel"),
        name="conv_ffn",
    )(h3, h3, h3, w_up, conv_w, conv_b, w_down, x3, g_post, mod4)


def _prep_w_in(w_in):
    w_in = w_in.astype(BF16)
    z = lambda n: jnp.zeros(w_in.shape[:-1] + (n,), w_in.dtype)
    a = w_in[..., 0:512]
    b = w_in[..., 512:1536]
    gl = w_in[..., 1536:1568]
    c = w_in[..., 1568:1824]
    cq = w_in[..., 1824:2048]
    ckv_kr = w_in[..., 2048:2176]
    return jnp.concatenate([a, b, c, cq, z(32), ckv_kr, gl, z(96)], axis=-1)


def _rope_swap(t):
    q = MLA_ROPE // 4
    return jnp.concatenate([t[..., q:2 * q], t[..., 0:q], t[..., 3 * q:4 * q], t[..., 2 * q:3 * q]], axis=-1)


def _prep_mla(w_uq, w_ukv):
    L = w_uq.shape[0]
    H, NP, RP, HP = MLA_HEADS, MLA_NOPE, MLA_ROPE, MLA_HEAD_PAD
    wq = w_uq.reshape(L, MLA_Q_RANK, H, NP + RP)
    zq = jnp.zeros((L, MLA_Q_RANK, H, HP - NP - RP), w_uq.dtype)
    znope = jnp.zeros((L, MLA_Q_RANK, H, NP), w_uq.dtype)
    q_main = jnp.concatenate([wq, zq], axis=-1)
    q_swap = jnp.concatenate([znope, _rope_swap(wq[..., NP:]), zq], axis=-1)
    padq = lambda w: jnp.pad(w.reshape(L, MLA_Q_RANK, H * HP), ((0, 0), (0, MLA_Q_LANES - MLA_Q_RANK), (0, 0)))

    wkv = w_ukv.reshape(L, MLA_KV_RANK, H, NP + MLA_V)
    zk = jnp.zeros((L, MLA_KV_RANK, H, HP - NP), w_ukv.dtype)
    k_lat = jnp.concatenate([wkv[..., :NP], zk], axis=-1)
    eye = jnp.eye(RP, dtype=w_ukv.dtype)
    place = lambda e: jnp.broadcast_to(
        jnp.concatenate([jnp.zeros((RP, NP), e.dtype), e, jnp.zeros((RP, HP - NP - RP), e.dtype)], axis=-1)[None, :, None, :],
        (L, RP, H, HP))
    k_main = jnp.concatenate([k_lat, place(eye)], axis=1)
    k_swap = jnp.concatenate([jnp.zeros_like(k_lat), place(_rope_swap(eye))], axis=1)
    zv = jnp.zeros((L, MLA_KV_RANK, H, HP - MLA_V), w_ukv.dtype)
    odd = (jnp.arange(H) % 2 == 1)[None, None, :, None]
    v_lat = jnp.where(odd, jnp.concatenate([zv, wkv[..., NP:]], axis=-1), jnp.concatenate([wkv[..., NP:], zv], axis=-1))
    v_main = jnp.concatenate([v_lat, jnp.zeros((L, RP, H, HP), w_ukv.dtype)], axis=1)
    flat = lambda w: w.reshape(L, w.shape[1], H * HP).astype(BF16)
    return (padq(q_main).astype(BF16), padq(q_swap).astype(BF16), flat(k_main), flat(k_swap), flat(v_main))


def _rope_tables(n):
    rows = n // GRID_W
    nf = MLA_ROPE // 4
    inv = ROPE_BASE ** (-jnp.arange(nf, dtype=F32) / nf)
    ar = jnp.arange(rows, dtype=F32)[:, None] * inv[None, :]
    ac = jnp.arange(GRID_W, dtype=F32)[:, None] * inv[None, :]
    by_row = lambda t: jnp.repeat(t, GRID_W, axis=0)
    by_col = lambda t: jnp.tile(t, (rows, 1))
    cr, sr, cc, sn = by_row(jnp.cos(ar)), by_row(jnp.sin(ar)), by_col(jnp.cos(ac)), by_col(jnp.sin(ac))
    one = jnp.ones((n, MLA_NOPE), F32)
    zero = jnp.zeros((n, MLA_HEAD_PAD - MLA_NOPE - MLA_ROPE), F32)
    cos = jnp.concatenate([one, cr, cr, cc, cc, zero], axis=1)
    sin = jnp.concatenate([0.0 * one, -sr, sr, -sn, sn, zero], axis=1)
    return cos, sin


def _pick_tile(n, want):
    t = min(n, want)
    while n % t:
        t //= 2
    return t


def kernel(x, c, ctx, c_ctx, w_mod, b_mod, g_pre_mix, g_post_mix, g_pre_ffn, g_post_ffn, w_in,
           sgu_norm, sgu_w, sgu_b, gla_w_gate, gla_b_gate, gla_norm,
           s5_a_re, s5_a_im, s5_log_dt, s5_b_re, s5_b_im, s5_c_re, s5_c_im, s5_d, s5_w_glu, s5_b_glu,
           mla_q_norm, mla_w_uq, mla_kv_norm, mla_w_ukv, w_out,
           ffn_w_up, ffn_conv_w, ffn_conv_b, ffn_w_down):
    B, n, D = x.shape
    nctx = ctx.shape[1]
    L = w_mod.shape[0]
    W = GROUP_W
    assert B < 8 and n % TOKEN_TILE == 0 and nctx % GLA_TILE == 0 and n % GRID_W == 0

    c8 = jnp.concatenate([c, c_ctx[None, :], jnp.zeros((8 - B - 1, D), F32)], axis=0)
    mod4 = _modulation(c8, w_mod, b_mod).reshape(L, 8, 1, 6 * D)
    vec = lambda p: p.reshape(L, 1, -1).astype(F32)
    g_pre_mix, g_post_mix, g_pre_ffn, g_post_ffn = map(vec, (g_pre_mix, g_post_mix, g_pre_ffn, g_post_ffn))
    w_in_p = _prep_w_in(w_in)
    sgu_gn = vec(sgu_norm)
    sgu_w_st = sgu_w.reshape(L, -1, MLP_CHUNK).astype(BF16)
    sgu_bias = jnp.repeat(jnp.swapaxes(sgu_b, 1, 2), HEAD_D, axis=2).astype(F32)
    ones_bd = jnp.kron(jnp.eye(W // HEAD_D, dtype=F32), jnp.ones((HEAD_D, HEAD_D), F32)).astype(BF16)
    gla_wg = jnp.zeros((L, 2, LANES, W), F32)
    gla_wg = gla_wg.at[:, 0, 0:GATE_RANK].set(gla_w_gate[:, 0]).at[:, 1, GATE_RANK:2 * GATE_RANK].set(gla_w_gate[:, 1])
    gla_bg = gla_b_gate.reshape(L, 2, 1, W).astype(F32)
    gla_gn = vec(gla_norm)
    s5_wglu = s5_w_glu.astype(BF16)
    s5_bglu = vec(s5_b_glu)
    mla_qn = jnp.pad(mla_q_norm, ((0, 0), (0, MLA_Q_LANES - MLA_Q_RANK))).reshape(L, 1, MLA_Q_LANES).astype(F32)
    mla_kvn = jnp.pad(mla_kv_norm, ((0, 0), (0, MLA_KV_LANES - MLA_KV_RANK))).reshape(L, 1, MLA_KV_LANES).astype(F32)
    mla_wts = _prep_mla(mla_w_uq, mla_w_ukv)
    tables = _rope_tables(n)
    w_out_b = w_out.astype(BF16)
    w_up_b = ffn_w_up.astype(BF16)
    w_down_b = ffn_w_down.astype(BF16)
    dff = ffn_w_down.shape[1]
    half_gate = jnp.concatenate([jnp.ones((dff,), F32), jnp.full((dff,), 0.5, F32)])
    conv_w = ffn_conv_w.astype(F32) * half_gate
    conv_b = (ffn_conv_b.astype(F32) * half_gate).reshape(L, 1, -1)

    s5_mats = jax.vmap(_s5_matrices)(s5_a_re, s5_a_im, s5_log_dt, s5_b_re, s5_b_im, s5_c_re, s5_c_im, s5_d)

    tm_x = _pick_tile(n, TOKEN_TILE)
    tm_c = _pick_tile(nctx, TOKEN_TILE)
    tpb_x = n // tm_x
    brow_x = lambda i: i // tpb_x
    tm_in = _pick_tile(n, INPROJ_TILE)
    brow_in = lambda i: i // (n // tm_in)
    brow_c = lambda *g: B
    brow_x2 = lambda b, i: b

    xs = x.reshape(B * n, D)
    cs = ctx.reshape(B * nctx, D)
    nk = n + nctx
    k_all = v_all = jnp.zeros((B, nk, MLA_HEADS * MLA_HEAD_PAD), BF16)

    for l in range(L):
        need_ctx = l < L - 1
        sgu = (sgu_gn, sgu_w_st, sgu_bias, ones_bd)
        wq, wqs, wk, wks, wv = mla_wts
        oa_x, pb_x, pc_x, pg_x, q_x, k_all, v_all = _inproj(
            l, xs, mod4, brow_in, g_pre_mix, w_in_p, sgu, (mla_qn, mla_kvn, mla_wts, tables), tm_in, n, B, 0, nk, (k_all, v_all))
        oa_c, pb_c, pc_c, pg_c, q_c, k_all, v_all = _inproj(
            l, cs, mod4, brow_c, g_pre_mix, w_in_p, sgu, (mla_qn, mla_kvn, (wq, None, wk, None, wv), None), tm_c, nctx, B, n, nk,
            (k_all, v_all))

        r3 = lambda t, m: t.reshape(B, m, t.shape[-1])
        ob_c, ob_x = _gla(l, r3(pb_c, nctx), r3(pg_c, nctx), r3(pb_x, n), r3(pg_x, n),
                          gla_wg, gla_bg, gla_gn, ones_bd, _pick_tile(nctx, GLA_TILE), _pick_tile(n, GLA_STEP_TILES * GLA_TILE))
        oc_c, oc_x = _s5(l, r3(pc_c, nctx), r3(pc_x, n), s5_mats, s5_wglu, s5_bglu, need_ctx)
        tk = next(t for t in ATTN_KV_TILES if nk % t == 0)
        od_x = _attention(r3(q_x, n), k_all, v_all, tm_x, tk, 0, nk).reshape(B * n, W)

        xs, hx = _outproj(l, (oa_x, ob_x.reshape(B * n, W), oc_x, od_x), w_out_b, xs, mod4, brow_in,
                          g_post_mix, g_pre_ffn, tm_in)
        xs = _ffn(l, hx.reshape(B, n, D), xs.reshape(B, n, D), w_up_b, conv_w, conv_b, w_down_b, mod4, brow_x2,
                  g_post_ffn, tm_x).reshape(B * n, D)

        if need_ctx:
            od_c = _attention(r3(q_c, nctx), k_all, v_all, tm_c, nctx, n, nctx).reshape(B * nctx, W)
            cs, hc = _outproj(l, (oa_c, ob_c.reshape(B * nctx, W), oc_c, od_c), w_out_b, cs, mod4, brow_c,
                              g_post_mix, g_pre_ffn, tm_c)
            cs = _ffn(l, hc.reshape(B, nctx, D), cs.reshape(B, nctx, D), w_up_b, conv_w, conv_b, w_down_b, mod4,
                      brow_c, g_post_ffn, tm_c).reshape(B * nctx, D)
    return xs.reshape(B, n, D)
```

```python
import functools

import jax
import jax.numpy as jnp
from jax import lax
from jax.experimental import pallas as pl
from jax.experimental.pallas import tpu as pltpu

F32 = jnp.float32
BF16 = jnp.bfloat16

EPS = 1e-6
GRID_W = 64
GROUP_W = 256
HEAD_D = 64
MLP_CHUNK = 128
GATE_RANK = 16
GATE_TEMP = 16.0
GLA_CHUNK = 64
GLA_TILE = 256
GLA_STEP_TILES = 8
S5_IN = 16
S5_T = 16
S5_TILE = 128
S5_PITCH_PAD = 8
MLA_HEADS = 4
MLA_NOPE = 64
MLA_ROPE = 32
MLA_V = 64
MLA_Q_RANK = 224
MLA_KV_RANK = 96
MLA_HEAD_PAD = 128
ROPE_BASE = 10000.0
LOG2E = 1.4426950408889634

LANES = 128
VMEM_LIMIT = 48 * 1024 * 1024
TOKEN_TILE = 512
INPROJ_TILE = 1024
MOD_COL_TILE = 1536
ATTN_KV_TILES = (2816, 1408, 768, 512, 256, 128)
MLA_Q_LANES = 256
MLA_KV_LANES = 128


def _cparams(*sem):
    return pltpu.CompilerParams(dimension_semantics=sem, vmem_limit_bytes=VMEM_LIMIT)


def _dot(a, b):
    return jnp.dot(a, b, preferred_element_type=F32)


def _dot_nt(a, b):
    return lax.dot_general(a, b, (((1,), (1,)), ((), ())), preferred_element_type=F32)


def _dot_tn(a, b):
    return lax.dot_general(a, b, (((0,), (0,)), ((), ())), preferred_element_type=F32)


def _split(a):
    hi = a.astype(BF16)
    lo = (a - hi.astype(F32)).astype(BF16)
    return hi, lo


def _dot_x2(a, b_bf16):
    hi, lo = _split(a)
    return _dot(hi, b_bf16) + _dot(lo, b_bf16)


def _dot_x3(a, b):
    ah, al = _split(a)
    bh, bl = _split(b)
    return _dot(ah, bh) + _dot(al, bh) + _dot(ah, bl)


def _rms(x):
    return x * lax.rsqrt(jnp.mean(x * x, axis=-1, keepdims=True) + EPS)


def _gelu(x):
    return 0.5 * x * (1.0 + jnp.tanh(0.7978845608028654 * (x + 0.044715 * (x * x * x))))


def _gelu_gate(a, half_g):
    u = a * (0.7978845608028654 + 0.035677408136300125 * (a * a))
    return (a * half_g) * (1.0 + jnp.tanh(u))


def _sigmoid(x):
    return 1.0 / (1.0 + jnp.exp(-x))


def _lane_group(shape, width):
    return lax.broadcasted_iota(jnp.int32, shape, len(shape) - 1) // width


def _mod_kernel(c_ref, w_ref, b_ref, o_ref):
    c = c_ref[...]
    s = c * _sigmoid(c)
    o_ref[...] = _dot_x3(s, w_ref[...]) + b_ref[...]


def _modulation(c8, w_mod, b_mod):
    L, D, W = w_mod.shape
    tn = MOD_COL_TILE
    return pl.pallas_call(
        _mod_kernel,
        grid=(L, W // tn),
        in_specs=[
            pl.BlockSpec((8, D), lambda l, j: (0, 0)),
            pl.BlockSpec((None, D, tn), lambda l, j: (l, 0, j)),
            pl.BlockSpec((None, 1, tn), lambda l, j: (l, 0, j)),
        ],
        out_specs=pl.BlockSpec((None, 8, tn), lambda l, j: (l, 0, j)),
        out_shape=jax.ShapeDtypeStruct((L, 8, W), F32),
        compiler_params=_cparams("arbitrary", "arbitrary"),
        name="modulation",
    )(c8, w_mod, b_mod.reshape(L, 1, W))


def _mod_spec(l, j, D, bfn):
    return pl.BlockSpec((None, None, 1, D), lambda *g: (l, bfn(*g), 0, j))


def _vec_spec(l, width):
    return pl.BlockSpec((None, 1, width), lambda *g: (l, 0, 0))


IN_SLABS = (("a", 0, 2 * GROUP_W), ("b", 512, 4 * GROUP_W), ("c", 1536, GROUP_W), ("d", 1792, MLA_Q_LANES + MLA_KV_LANES),
            ("g", 2176, LANES))
IN_PAD_COLS = 2304


def _sgu_tile(p, gn_ref, w_ref, b_ref, ones_ref, o_ref):
    tm = p.shape[0]
    g = _gelu(p)
    u = g[:, :GROUP_W]
    v = g[:, GROUP_W:]
    ms = _dot_x2(v * v, ones_ref[...]) * (1.0 / HEAD_D)
    vb = (v * lax.rsqrt(ms + EPS) * gn_ref[...]).astype(BF16)
    head = _lane_group((MLP_CHUNK, GROUP_W), HEAD_D)
    w = w_ref[...]
    for c in range(tm // MLP_CHUNK):
        rows = slice(c * MLP_CHUNK, (c + 1) * MLP_CHUNK)
        r = _dot(w, vb[rows])
        s = b_ref[...]
        for h in range(GROUP_W // HEAD_D):
            s = s + jnp.where(head == h, r[h * MLP_CHUNK:(h + 1) * MLP_CHUNK], 0.0)
        o_ref[rows, :] = (u[rows] * s).astype(o_ref.dtype)


def _mla_tile(pd, qn_ref, kvn_ref, wq_ref, wqs_ref, wk_ref, wks_ref, wv_ref, cos_ref, sin_ref, q_ref, k_ref, v_ref):
    cq = pd[:, 0:MLA_Q_LANES]
    ms = jnp.sum(cq * cq, axis=-1, keepdims=True) * (1.0 / MLA_Q_RANK)
    cqn = (cq * lax.rsqrt(ms + EPS) * qn_ref[...]).astype(BF16)
    ck = pd[:, MLA_Q_LANES:MLA_Q_LANES + MLA_KV_LANES]
    lane = lax.broadcasted_iota(jnp.int32, ck.shape, 1)
    is_lat = lane < MLA_KV_RANK
    ms = jnp.sum(jnp.where(is_lat, ck * ck, 0.0), axis=-1, keepdims=True) * (1.0 / MLA_KV_RANK)
    ckn = jnp.where(is_lat, ck * lax.rsqrt(ms + EPS) * kvn_ref[...], ck).astype(BF16)
    q = _dot(cqn, wq_ref[...])
    k = _dot(ckn, wk_ref[...])
    if cos_ref is not None:
        cos = jnp.concatenate([cos_ref[...]] * MLA_HEADS, axis=1)
        sin = jnp.concatenate([sin_ref[...]] * MLA_HEADS, axis=1)
        q = q * cos + _dot(cqn, wqs_ref[...]) * sin
        k = k * cos + _dot(ckn, wks_ref[...]) * sin
    q_ref[...] = (q * ((MLA_NOPE + MLA_ROPE) ** -0.5 * LOG2E)).astype(BF16)
    k_ref[...] = k.astype(BF16)
    v_ref[...] = _dot(ckn, wv_ref[...]).astype(BF16)


def _inproj_kernel(*refs, rope):
    x_ref, g_ref, sc_ref, sh_ref, w_ref, gn_ref, wsp_ref, bsp_ref, ones_ref, qn_ref, kvn_ref = refs[:11]
    if rope:
        wq_ref, wqs_ref, wk_ref, wks_ref, wv_ref, cos_ref, sin_ref = refs[11:18]
    else:
        wq_ref, wk_ref, wv_ref = refs[11:14]
        wqs_ref = wks_ref = cos_ref = sin_ref = None
    oa, ob, oc, og, q_ref, k_ref, v_ref = refs[-7:]
    half = x_ref.shape[0] // 2
    slab = {name: slice(off, off + width) for name, off, width in IN_SLABS}

    def project(rows):
        h = _rms(x_ref[rows, :]) * g_ref[...] * (1.0 + sc_ref[...]) + sh_ref[...]
        hb = h.astype(BF16)
        pa = _dot(hb, w_ref[:, slab["a"]])
        for name, o_ref in (("b", ob), ("c", oc), ("g", og)):
            o_ref[rows, :] = _dot(hb, w_ref[:, slab[name]]).astype(o_ref.dtype)
        return pa, _dot(hb, w_ref[:, slab["d"]])

    def mixers(pa, pd, rows):
        cos, sin = (cos_ref.at[rows], sin_ref.at[rows]) if rope else (None, None)
        _mla_tile(pd, qn_ref, kvn_ref, wq_ref, wqs_ref, wk_ref, wks_ref, wv_ref, cos, sin,
                  q_ref.at[rows], k_ref.at[rows], v_ref.at[rows])
        _sgu_tile(pa, gn_ref, wsp_ref, bsp_ref, ones_ref, oa.at[rows])

    first, second = slice(0, half), slice(half, 2 * half)
    pa0, pd0 = project(first)
    pa1, pd1 = project(second)
    mixers(pa0, pd0, first)
    mixers(pa1, pd1, second)


def _inproj(l, x2, mod4, brow, g_pre, w_in_p, sgu, mla, tm, n, B, row0, nk, kv_into):
    R, D = x2.shape
    gn, w_st, bias, ones_bd = sgu
    qn, kvn, (wq, wqs, wk, wks, wv), tables = mla
    rope = tables is not None
    H = GROUP_W // HEAD_D
    HP = MLA_HEADS * MLA_HEAD_PAD
    QL, KL = MLA_Q_LANES, MLA_KV_LANES
    npt = n // tm
    width = {name: w for name, _, w in IN_SLABS}
    wspec = lambda r: pl.BlockSpec((None, r, HP), lambda i: (l, 0, 0))
    in_specs = [
        pl.BlockSpec((tm, D), lambda i: (i, 0)),
        _vec_spec(l, D),
        _mod_spec(l, 1, D, brow),
        _mod_spec(l, 0, D, brow),
        pl.BlockSpec((None, D, IN_PAD_COLS), lambda i: (l, 0, 0)),
        _vec_spec(l, GROUP_W),
        pl.BlockSpec((None, H * MLP_CHUNK, MLP_CHUNK), lambda i: (l, 0, 0)),
        pl.BlockSpec((None, MLP_CHUNK, GROUP_W), lambda i: (l, 0, 0)),
        pl.BlockSpec((GROUP_W, GROUP_W), lambda i: (0, 0)),
        _vec_spec(l, QL),
        _vec_spec(l, KL),
    ]
    args = [x2, g_pre, mod4, mod4, w_in_p, gn, w_st, bias, ones_bd, qn, kvn]
    if rope:
        tspec = pl.BlockSpec((tm, MLA_HEAD_PAD), lambda i: (i % npt, 0))
        in_specs += [wspec(QL), wspec(QL), wspec(KL), wspec(KL), wspec(KL), tspec, tspec]
        args += [wq, wqs, wk, wks, wv, tables[0], tables[1]]
    else:
        in_specs += [wspec(QL), wspec(KL), wspec(KL)]
        args += [wq, wk, wv]
    aliases = {len(args): 5, len(args) + 1: 6}
    in_specs += [pl.BlockSpec(memory_space=pl.ANY)] * 2
    args += list(kv_into)
    row = lambda w: pl.BlockSpec((tm, w), lambda i: (i, 0))
    kv_spec = pl.BlockSpec((None, tm, HP), lambda i: (i // npt, row0 // tm + i % npt, 0))
    kv_shape = jax.ShapeDtypeStruct((B, nk, HP), BF16)
    return pl.pallas_call(
        functools.partial(_inproj_kernel, rope=rope),
        grid=(R // tm,),
        in_specs=in_specs,
        out_specs=[row(GROUP_W), row(width["b"]), row(width["c"]), row(width["g"]), row(HP), kv_spec, kv_spec],
        out_shape=[jax.ShapeDtypeStruct((R, w), BF16) for w in (GROUP_W, width["b"], width["c"], width["g"], HP)]
        + [kv_shape, kv_shape],
        input_output_aliases=aliases,
        compiler_params=_cparams("parallel"),
        name="inproj",
    )(*args)


def _gla_kernel(*refs, rev, finish):
    if finish:
        (pb_ref, pg_ref, wg_ref, bg_ref, s0_ref, ones_ref, dec_ref, oprev_ref, gn_ref, o_ref, sfin_ref, st_scr) = refs
    else:
        (pb_ref, pg_ref, wg_ref, bg_ref, s0_ref, ones_ref, dec_ref, o_ref, sfin_ref, st_scr) = refs
    i = pl.program_id(1)
    C, W, T = GLA_CHUNK, GROUP_W, GLA_TILE
    H = W // HEAD_D
    nsub = T // C
    ntile = pb_ref.shape[0] // T

    @pl.when(i == 0)
    def _():
        st_scr[...] = s0_ref[...]

    tri = dec_ref[...]
    mask4 = jnp.concatenate([tri.astype(F32)] * H, axis=0)
    head_t = _lane_group((T, W), HEAD_D)
    bd = (lax.broadcasted_iota(jnp.int32, (W, W), 0) // HEAD_D) == _lane_group((W, W), HEAD_D)
    tiles = list(range(ntile - 1, -1, -1)) if rev else list(range(ntile))
    subs = list(range(nsub - 1, -1, -1)) if rev else list(range(nsub))
    rows = lambda a: slice(a * T, (a + 1) * T)
    sub = lambda j: slice(j * C, (j + 1) * C)
    stack = lambda blocks: jnp.concatenate([blocks[j] for j in range(nsub)], axis=0)

    wg_hi, wg_lo = _split(wg_ref[...])
    logg = {}
    for a in tiles:
        z = _dot(pg_ref[rows(a), :], wg_hi) + _dot(pg_ref[rows(a), :], wg_lo) + bg_ref[...]
        logg[a] = (jnp.minimum(z, 0.0) - jnp.log(1.0 + jnp.exp(-jnp.abs(z)))) * (1.0 / GATE_TEMP)
    cums = {}
    for a in tiles:
        g_hi, g_lo = _split(logg[a])
        cums[a] = _dot(tri, g_hi) + _dot(tri, g_lo)
    qin, qoff, kin, kend, vb, tot, ptile = {}, {}, {}, {}, {}, {}, {}
    for a in tiles:
        cum = cums[a]
        q = pb_ref[rows(a), 0:W].astype(F32) * (HEAD_D ** -0.5)
        k = pb_ref[rows(a), W:2 * W].astype(F32)
        last_row = (lambda j: j * C) if rev else (lambda j: j * C + C - 1)
        off, run = {}, jnp.zeros((1, W), F32)
        for j in subs:
            tot[(a, j)] = cum[last_row(j):last_row(j) + 1]
            off[j] = run
            run = run + tot[(a, j)]
        ptile[a] = jnp.exp(run)
        e = jnp.exp(cum)
        qin[a] = (q * e).astype(BF16)
        qoff[a] = (q * (e * jnp.exp(stack({j: jnp.broadcast_to(off[j], (C, W)) for j in subs})))).astype(BF16)
        kin[a] = (k * jnp.exp(-cum)).astype(BF16)
        kend[a] = (k * jnp.exp(stack({j: jnp.broadcast_to(tot[(a, j)], (C, W)) for j in subs}) - cum)).astype(BF16)
        vb[a] = pb_ref[rows(a), 2 * W:3 * W]
    sc = {}
    for a in tiles:
        qst = jnp.concatenate([jnp.where(head_t == h, qin[a], jnp.zeros_like(qin[a])) for h in range(H)], axis=0)
        sc[a] = (_dot_nt(qst, kin[a]) * mask4).astype(BF16)
    ost = {a: _dot(sc[a], vb[a]) for a in tiles}
    kvt = {(a, j): jnp.where(bd, _dot_tn(vb[a][sub(j)], kend[a][sub(j)]), 0.0) for a in tiles for j in subs}
    cross, rend = {}, {}
    for a in tiles:
        r = None
        for j in subs:
            if r is not None:
                cross[(a, j)] = _dot_nt(qin[a][sub(j)], r.astype(BF16))
                r = r * jnp.exp(tot[(a, j)]) + kvt[(a, j)]
            else:
                cross[(a, j)] = jnp.zeros((C, W), F32)
                r = kvt[(a, j)]
        rend[a] = r
    st = st_scr[...]
    inter = {}
    for a in tiles:
        inter[a] = _dot_nt(qoff[a], st.astype(BF16))
        st = st * ptile[a] + rend[a]
    st_scr[...] = st

    for a in tiles:
        o = inter[a] + jnp.concatenate([cross[(a, j)] for j in range(nsub)], axis=0)
        for h in range(H):
            o = o + jnp.where(head_t == h, ost[a][h * T:(h + 1) * T], 0.0)
        if finish:
            o = o + oprev_ref[rows(a), :]
            ms = _dot_x2(o * o, ones_ref[...]) * (1.0 / HEAD_D)
            o = o * lax.rsqrt(ms + EPS) * gn_ref[...]
            r = pb_ref[rows(a), 3 * W:4 * W].astype(F32)
            o_ref[rows(a), :] = (o * (r * _sigmoid(r))).astype(o_ref.dtype)
        else:
            o_ref[rows(a), :] = o

    @pl.when(i == pl.num_programs(1) - 1)
    def _():
        sfin_ref[...] = st_scr[...]


def _gla_decay_matrices():
    T, C = GLA_TILE, GLA_CHUNK
    t = jnp.arange(T)[:, None]
    s = jnp.arange(T)[None, :]
    same = (t // C) == (s // C)
    return jnp.stack([same & (s <= t), same & (s >= t)]).astype(BF16)


def _gla_pass(l, d, pb3, pg3, wg, bg, s0, ones_bd, dec, oprev, gn, tt):
    B, n, _ = pb3.shape
    nt = n // tt
    rev = d == 1
    finish = oprev is not None
    W = GROUP_W

    def tok(b, i):
        return (b, (nt - 1 - i) if rev else i, 0)

    in_specs = [
        pl.BlockSpec((None, tt, 4 * W), tok),
        pl.BlockSpec((None, tt, LANES), tok),
        pl.BlockSpec((None, None, LANES, W), lambda b, i: (l, d, 0, 0)),
        pl.BlockSpec((None, None, 1, W), lambda b, i: (l, d, 0, 0)),
        pl.BlockSpec((None, W, W), lambda b, i: (b, 0, 0)),
        pl.BlockSpec((W, W), lambda b, i: (0, 0)),
        pl.BlockSpec((None, GLA_TILE, GLA_TILE), lambda b, i: (d, 0, 0)),
    ]
    args = [pb3, pg3, wg, bg, s0, ones_bd, dec]
    if finish:
        in_specs += [pl.BlockSpec((None, tt, W), tok), _vec_spec(l, W)]
        args += [oprev, gn]
    return pl.pallas_call(
        functools.partial(_gla_kernel, rev=rev, finish=finish),
        grid=(B, nt),
        in_specs=in_specs,
        out_specs=[pl.BlockSpec((None, tt, W), tok), pl.BlockSpec((None, W, W), lambda b, i: (b, 0, 0))],
        out_shape=[jax.ShapeDtypeStruct((B, n, W), BF16 if finish else F32),
                   jax.ShapeDtypeStruct((B, W, W), F32)],
        scratch_shapes=[pltpu.VMEM((W, W), F32)],
        compiler_params=_cparams("arbitrary", "arbitrary"),
        name="gla_bwd" if rev else "gla_fwd",
    )(*args)


def _gla(l, pb_c, pg_c, pb_x, pg_x, wg, bg, gn, ones_bd, tt_c, tt_x):
    B = pb_x.shape[0]
    zero = jnp.zeros((B, GROUP_W, GROUP_W), F32)
    dec = _gla_decay_matrices()
    ofc, sfc = _gla_pass(l, 0, pb_c, pg_c, wg, bg, zero, ones_bd, dec, None, None, tt_c)
    ofx, _ = _gla_pass(l, 0, pb_x, pg_x, wg, bg, sfc, ones_bd, dec, None, None, tt_x)
    ob_c, sbc = _gla_pass(l, 1, pb_c, pg_c, wg, bg, zero, ones_bd, dec, ofc, gn, tt_c)
    ob_x, _ = _gla_pass(l, 1, pb_x, pg_x, wg, bg, sbc, ones_bd, dec, ofx, gn, tt_x)
    return ob_c, ob_x


def _s5_matrices(a_re, a_im, log_dt, b_re, b_im, c_re, c_im, d_skip):
    T = S5_T
    G, P = a_re.shape[1:]
    I = b_re.shape[-1]
    lam = lax.complex(a_re.astype(F32), a_im.astype(F32))
    ldt = lam * jnp.exp(log_dt.astype(F32))[..., None]
    lam_bar = jnp.exp(ldt)
    b_bar = ((lam_bar - 1.0) / lam)[..., None] * lax.complex(b_re.astype(F32), b_im.astype(F32))
    cmat = lax.complex(c_re.astype(F32), c_im.astype(F32))
    steps = jnp.arange(T + 1, dtype=F32)
    pw = jnp.exp(ldt[..., None] * steps)
    taps = jnp.einsum('dgop,dgpk,dgpi->dgiko', cmat, pw[..., :T], b_bar).real
    taps = taps.at[0, :, :, 0, :].add(jnp.eye(I, dtype=F32)[None] * d_skip.astype(F32)[:, :, None])
    row = T * I
    zeros = jnp.zeros((G, I, row), F32)
    fwd = jnp.concatenate([zeros, taps[0].reshape(G, I, row)], axis=-1)
    bwd = jnp.concatenate([jnp.flip(taps[1], axis=2).reshape(G, I, row), zeros], axis=-1)
    m = jnp.stack([fwd[..., (T - s) * I:(T - s) * I + row] + bwd[..., (T - 1 - s) * I:(T - 1 - s) * I + row]
                   for s in range(T)], axis=1).reshape(G, row, row)

    ar = jnp.arange(T)
    pf = pw[0][..., T - 1 - ar]
    pb = pw[1][..., ar]
    bf = jnp.einsum('gps,gpi->gsip', pf, b_bar[0]).reshape(G, T * I, P)
    bb = jnp.einsum('gps,gpi->gsip', pb, b_bar[1]).reshape(G, T * I, P)
    bmat = jnp.concatenate([bf.real, bf.imag, bf.imag, bf.real, bb.real, bb.imag, bb.imag, bb.real], axis=-1)

    cf = jnp.einsum('gop,gpt->gpto', cmat[0], pw[0][..., 1 + ar]).reshape(G, P, T * I)
    cb = jnp.einsum('gop,gpt->gpto', cmat[1], pw[1][..., T - ar]).reshape(G, P, T * I)
    w = jnp.concatenate([m, cf.real, -cf.imag, cb.real, -cb.imag], axis=1)

    a = pw[..., T]
    acoef = jnp.stack([jnp.concatenate([a[0].real, a[0].real], axis=-1), jnp.concatenate([-a[0].imag, a[0].imag], axis=-1),
                       jnp.concatenate([a[1].real, a[1].real], axis=-1), jnp.concatenate([-a[1].imag, a[1].imag], axis=-1)])
    return bmat.astype(BF16), w.astype(BF16), acoef


def _s5_scan_kernel(xf_ref, xb_ref, bm_ref, a_ref, h0_ref, hf_ref, hb_ref, hfin_ref, st_scr, sf_scr, sb_scr, hf_scr, hb_scr):
    j = pl.program_id(1)
    G, tc, _ = xf_ref.shape
    pitch = tc + S5_PITCH_PAD
    half = hf_ref.shape[-1]

    @pl.when(j == 0)
    def _():
        st_scr[...] = h0_ref[...]

    for g in range(G):
        sf = _dot(xf_ref[g], bm_ref[g, :, 0:2 * half])
        sb = _dot(xb_ref[g], bm_ref[g, :, 2 * half:4 * half])
        for k in range(2):
            sf_scr[k, g * pitch:g * pitch + tc, :] = sf[:, k * half:(k + 1) * half]
            sb_scr[k, g * pitch:g * pitch + tc, :] = sb[:, k * half:(k + 1) * half]

    a1f, a2f, a1b, a2b = a_ref[0], a_ref[1], a_ref[2], a_ref[3]
    chunk = lambda i: pl.ds(i, G, stride=pitch)

    def body(i, hs):
        h1f, h2f, h1b, h2b = hs
        ib = tc - 1 - i
        hf_scr[chunk(i), :] = h1f
        hb_scr[chunk(ib), :] = h1b
        return (h1f * a1f + h2f * a2f + sf_scr[0, chunk(i), :], h2f * a1f - h1f * a2f + sf_scr[1, chunk(i), :],
                h1b * a1b + h2b * a2b + sb_scr[0, chunk(ib), :], h2b * a1b - h1b * a2b + sb_scr[1, chunk(ib), :])

    hs = lax.fori_loop(0, tc, body, (st_scr[0], st_scr[1], st_scr[2], st_scr[3]))
    for k in range(4):
        st_scr[k] = hs[k]
    for g in range(G):
        hf_ref[g] = hf_scr[g * pitch:g * pitch + tc, :]
        hb_ref[g] = hb_scr[g * pitch:g * pitch + tc, :]

    @pl.when(j == pl.num_programs(1) - 1)
    def _():
        hfin_ref[...] = st_scr[...]


def _s5_scan(l, xg, bmat, acoef, h0, B, tc):
    G, R, K = xg.shape
    nt = R // B // tc
    half = bmat.shape[-1] // 4
    fwd = lambda b, j: (0, b * nt + j, 0)
    bwd = lambda b, j: (0, b * nt + nt - 1 - j, 0)
    st_spec = pl.BlockSpec((None, 4, G, half), lambda b, j: (b, 0, 0, 0))
    pitch = tc + S5_PITCH_PAD
    return pl.pallas_call(
        _s5_scan_kernel,
        grid=(B, nt),
        in_specs=[pl.BlockSpec((G, tc, K), fwd),
                  pl.BlockSpec((G, tc, K), bwd),
                  pl.BlockSpec((None, G, K, 4 * half), lambda b, j: (l, 0, 0, 0)),
                  pl.BlockSpec((None, 4, G, half), lambda b, j: (l, 0, 0, 0)),
                  st_spec],
        out_specs=[pl.BlockSpec((G, tc, half), fwd), pl.BlockSpec((G, tc, half), bwd), st_spec],
        out_shape=[jax.ShapeDtypeStruct((G, R, half), F32), jax.ShapeDtypeStruct((G, R, half), F32),
                   jax.ShapeDtypeStruct((B, 4, G, half), F32)],
        scratch_shapes=[pltpu.VMEM((4, G, half), F32),
                        pltpu.VMEM((2, G * pitch, half), F32), pltpu.VMEM((2, G * pitch, half), F32),
                        pltpu.VMEM((G * pitch, half), F32), pltpu.VMEM((G * pitch, half), F32)],
        compiler_params=_cparams("arbitrary", "arbitrary"),
        name="s5_scan",
    )(xg, xg, bmat, acoef, h0)


def _s5_gather_kernel(x_ref, o_ref, xs_scr, xt_scr):
    G, tc, _ = o_ref.shape
    nh = x_ref.shape[1] // LANES
    gh = G // nh
    for h in range(nh):
        xs_scr[h] = x_ref[:, h * LANES:(h + 1) * LANES].astype(F32)
    for t in range(S5_T):
        for h in range(nh):
            rt = xs_scr[h, pl.ds(t, tc, stride=S5_T), :].T
            for g in range(gh):
                xt_scr[h * gh + g, t * S5_IN:(t + 1) * S5_IN, :] = rt[g * S5_IN:(g + 1) * S5_IN, :]
    for g in range(G):
        o_ref[g] = xt_scr[g].T.astype(o_ref.dtype)


def _s5_gather(x2, tc):
    N, W = x2.shape
    G = W // S5_IN
    R = N // S5_T
    return pl.pallas_call(
        _s5_gather_kernel,
        grid=(R // tc,),
        in_specs=[pl.BlockSpec((tc * S5_T, W), lambda i: (i, 0))],
        out_specs=pl.BlockSpec((G, tc, S5_T * S5_IN), lambda i: (0, i, 0)),
        out_shape=jax.ShapeDtypeStruct((G, R, S5_T * S5_IN), BF16),
        scratch_shapes=[pltpu.VMEM((W // LANES, tc * S5_T, LANES), F32), pltpu.VMEM((G, S5_T * S5_IN, tc), F32)],
        compiler_params=_cparams("parallel"),
        name="s5_gather",
    )(x2)


def _s5_scatter_kernel(x_ref, hf_ref, hb_ref, wy_ref, w_ref, b_ref, o_ref, yt_scr, tok_scr):
    G, tc, k = x_ref.shape
    kh = hf_ref.shape[-1]
    nh = tok_scr.shape[0]
    gh = G // nh
    for g in range(G):
        y = _dot(x_ref[g], wy_ref[g, 0:k, :])
        y = y + _dot(hf_ref[g].astype(BF16), wy_ref[g, k:k + kh, :])
        y = y + _dot(hb_ref[g].astype(BF16), wy_ref[g, k + kh:, :])
        yt = y.T
        for t in range(S5_T):
            yt_scr[t, g // gh, (g % gh) * S5_IN:(g % gh + 1) * S5_IN, :] = yt[t * S5_IN:(t + 1) * S5_IN, :]
    for t in range(S5_T):
        for h in range(nh):
            tok_scr[h, pl.ds(t, tc, stride=S5_T), :] = yt_scr[t, h].T
    y = _gelu(jnp.concatenate([tok_scr[h] for h in range(nh)], axis=1))
    gate = _sigmoid(_dot(y.astype(BF16), w_ref[...]) + b_ref[...])
    o_ref[...] = (y * gate).astype(o_ref.dtype)


def _s5_scatter_finish(l, xg, hf, hb, wy, w_glu, b_glu, tc):
    G, R, K = xg.shape
    kh = hf.shape[-1]
    W = G * S5_IN
    grp = lambda width: pl.BlockSpec((G, tc, width), lambda i: (0, i, 0))
    return pl.pallas_call(
        _s5_scatter_kernel,
        grid=(R // tc,),
        in_specs=[grp(K), grp(kh), grp(kh),
                  pl.BlockSpec((None, G, K + 2 * kh, K), lambda i: (l, 0, 0, 0)),
                  pl.BlockSpec((None, W, W), lambda i: (l, 0, 0)),
                  _vec_spec(l, W)],
        out_specs=pl.BlockSpec((tc * S5_T, W), lambda i: (i, 0)),
        out_shape=jax.ShapeDtypeStruct((R * S5_T, W), BF16),
        scratch_shapes=[pltpu.VMEM((S5_T, W // LANES, LANES, tc), F32), pltpu.VMEM((W // LANES, tc * S5_T, LANES), F32)],
        compiler_params=_cparams("parallel"),
        name="s5_scatter",
    )(xg, hf, hb, wy, w_glu, b_glu)


def _s5(l, pc_c, pc_x, mats, w_glu, b_glu, need_ctx):
    bmat, w, acoef = mats
    B, n, W = pc_x.shape
    nctx = pc_c.shape[1]
    G = W // S5_IN
    tc_c = min(S5_TILE, B * nctx // S5_T)
    tc_x = min(S5_TILE, n // S5_T)
    xg_c = _s5_gather(pc_c.reshape(B * nctx, W), tc_c)
    xg_x = _s5_gather(pc_x.reshape(B * n, W), tc_x)
    h0 = jnp.zeros((B, 4, G, acoef.shape[-1]), F32)
    hf_c, hb_c, h1 = _s5_scan(l, xg_c, bmat, acoef, h0, B, nctx // S5_T)
    hf_x, hb_x, _ = _s5_scan(l, xg_x, bmat, acoef, h1, B, tc_x)
    oc_x = _s5_scatter_finish(l, xg_x, hf_x, hb_x, w, w_glu, b_glu, tc_x)
    oc_c = _s5_scatter_finish(l, xg_c, hf_c, hb_c, w, w_glu, b_glu, tc_c) if need_ctx else None
    return oc_c, oc_x


def _attn_kernel(q_ref, k_ref, v_ref, o_ref, m_scr, l_scr, acc_scr):
    kv = pl.program_id(2)
    HP = MLA_HEAD_PAD
    tk = k_ref.shape[0]

    @pl.when(kv == 0)
    def _():
        m_scr[...] = jnp.full(m_scr.shape, -jnp.inf, F32)
        l_scr[...] = jnp.zeros(l_scr.shape, F32)
        acc_scr[...] = jnp.zeros(acc_scr.shape, F32)

    def scores(h):
        lanes = slice(h * HP, (h + 1) * HP)
        return _dot_nt(q_ref[:, lanes], k_ref[:, lanes])

    s_next = scores(0)
    for h in range(MLA_HEADS):
        lanes = slice(h * HP, (h + 1) * HP)
        s = s_next
        if h + 1 < MLA_HEADS:
            s_next = scores(h + 1)
        m_prev = m_scr[h]
        m_new = jnp.maximum(m_prev, jnp.max(s, axis=1, keepdims=True))
        alpha = jnp.exp2(m_prev - m_new)
        p = jnp.exp2(s - m_new[:, 0:1])
        lp = p[:, 0:LANES]
        for c in range(1, tk // LANES):
            lp = lp + p[:, c * LANES:(c + 1) * LANES]
        l_scr[h] = alpha * l_scr[h] + lp
        acc_scr[h] = alpha * acc_scr[h] + _dot(p.astype(BF16), v_ref[:, lanes])
        m_scr[h] = m_new

    @pl.when(kv == pl.num_programs(2) - 1)
    def _():
        low = lax.broadcasted_iota(jnp.int32, acc_scr.shape[1:], 1) < MLA_V
        norm = lambda h: acc_scr[h] * (1.0 / jnp.sum(l_scr[h], axis=1, keepdims=True))
        outs = [jnp.where(low, norm(h), norm(h + 1)) for h in range(0, MLA_HEADS, 2)]
        o_ref[...] = jnp.concatenate(outs, axis=1).astype(o_ref.dtype)


def _attention(q3, k3, v3, tq, tk, k0, nk):
    B, nq, HP = q3.shape
    kb = k0 // tk
    return pl.pallas_call(
        _attn_kernel,
        grid=(B, nq // tq, nk // tk),
        in_specs=[pl.BlockSpec((None, tq, HP), lambda b, i, j: (b, i, 0)),
                  pl.BlockSpec((None, tk, HP), lambda b, i, j: (b, kb + j, 0)),
                  pl.BlockSpec((None, tk, HP), lambda b, i, j: (b, kb + j, 0))],
        out_specs=pl.BlockSpec((None, tq, MLA_HEADS * MLA_V), lambda b, i, j: (b, i, 0)),
        out_shape=jax.ShapeDtypeStruct((B, nq, MLA_HEADS * MLA_V), BF16),
        scratch_shapes=[pltpu.VMEM((MLA_HEADS, tq, LANES), F32),
                        pltpu.VMEM((MLA_HEADS, tq, LANES), F32),
                        pltpu.VMEM((MLA_HEADS, tq, MLA_HEAD_PAD), F32)],
        compiler_params=_cparams("parallel", "parallel", "arbitrary"),
        name="attention",
    )(q3, k3, v3)


def _outproj_kernel(oa, ob, oc, od, w_ref, x_ref, gpost_ref, gt_ref, gpre_ref, sc_ref, sh_ref, xo_ref, h_ref):
    W = GROUP_W
    mix = _dot(oa[...], w_ref[0:W, :])
    mix = mix + _dot(ob[...], w_ref[W:2 * W, :])
    mix = mix + _dot(oc[...], w_ref[2 * W:3 * W, :])
    mix = mix + _dot(od[...], w_ref[3 * W:4 * W, :])
    x = x_ref[...] + gt_ref[...] * (_rms(mix) * gpost_ref[...])
    xo_ref[...] = x
    h_ref[...] = (_rms(x) * gpre_ref[...] * (1.0 + sc_ref[...]) + sh_ref[...]).astype(h_ref.dtype)


def _outproj(l, parts, w_out, x2, mod4, brow, g_post, g_pre_ffn, tm):
    R, D = x2.shape
    W = GROUP_W
    part_spec = pl.BlockSpec((tm, W), lambda i: (i, 0))
    row_spec = pl.BlockSpec((tm, D), lambda i: (i, 0))
    return pl.pallas_call(
        _outproj_kernel,
        grid=(R // tm,),
        in_specs=[part_spec] * 4 + [
            pl.BlockSpec((None, 4 * W, D), lambda i: (l, 0, 0)),
            row_spec,
            _vec_spec(l, D),
            _mod_spec(l, 2, D, brow),
            _vec_spec(l, D),
            _mod_spec(l, 4, D, brow),
            _mod_spec(l, 3, D, brow),
        ],
        out_specs=[row_spec, row_spec],
        out_shape=[jax.ShapeDtypeStruct((R, D), F32), jax.ShapeDtypeStruct((R, D), BF16)],
        compiler_params=_cparams("parallel"),
        name="outproj",
    )(*parts, w_out, x2, g_post, mod4, g_pre_ffn, mod4, mod4)


FFN_HALO = 16
FFN_COLS = 256


def _ffn_kernel(hp_ref, h_ref, hn_ref, wup_ref, cw_ref, cb_ref, wdn_ref, x_ref, gpost_ref, gt_ref, o_ref, y_scr):
    i = pl.program_id(1)
    tm = h_ref.shape[0]
    dff = wdn_ref.shape[0]
    use = FFN_HALO // 2
    rows = tm + 2 * use
    prev = jnp.where(i == 0, jnp.zeros_like(hp_ref[...]), hp_ref[...])[FFN_HALO - use:]
    nxt = jnp.where(i == pl.num_programs(1) - 1, jnp.zeros_like(hn_ref[...]), hn_ref[...])[:use]
    hb = jnp.concatenate([prev, h_ref[...], nxt], axis=0)

    def conv(z, cols):
        w = cw_ref[:, cols]
        out = cb_ref[:, cols] + w[0:1] * pltpu.roll(z, 1, axis=0) + w[1:2] * z + w[2:3] * pltpu.roll(z, rows - 1, axis=0)
        return out[use:use + tm]

    def cols(j):
        return slice(j * FFN_COLS, (j + 1) * FFN_COLS), slice(dff + j * FFN_COLS, dff + (j + 1) * FFN_COLS)

    def up(j):
        ca, cg = cols(j)
        return _dot(hb, wup_ref[:, ca]), _dot(hb, wup_ref[:, cg])

    nchunks = dff // FFN_COLS
    z_next = up(0)
    for j in range(nchunks):
        ca, cg = cols(j)
        za, zg = z_next
        if j + 1 < nchunks:
            z_next = up(j + 1)
        y_scr[:, ca] = _gelu_gate(conv(za, ca), conv(zg, cg)).astype(BF16)
    acc = _dot(y_scr[...], wdn_ref[...])
    o_ref[...] = x_ref[...] + gt_ref[...] * (_rms(acc) * gpost_ref[...])


def _ffn(l, h3, x3, w_up, conv_w, conv_b, w_down, mod4, brow, g_post, tm):
    B, n, D = x3.shape
    nt = n // tm
    hb = tm // FFN_HALO
    nh = n // FFN_HALO
    dff = w_down.shape[1]
    once = pl.Buffered(1)
    return pl.pallas_call(
        _ffn_kernel,
        grid=(B, nt),
        in_specs=[
            pl.BlockSpec((None, FFN_HALO, D), lambda b, i: (b, jnp.maximum(i * hb - 1, 0), 0)),
            pl.BlockSpec((None, tm, D), lambda b, i: (b, i, 0)),
            pl.BlockSpec((None, FFN_HALO, D), lambda b, i: (b, jnp.minimum((i + 1) * hb, nh - 1), 0)),
            pl.BlockSpec((None, D, 2 * dff), lambda b, i: (l, 0, 0), pipeline_mode=once),
            pl.BlockSpec((None, 3, 2 * dff), lambda b, i: (l, 0, 0)),
            pl.BlockSpec((None, 1, 2 * dff), lambda b, i: (l, 0, 0)),
            pl.BlockSpec((None, dff, D), lambda b, i: (l, 0, 0), pipeline_mode=once),
            pl.BlockSpec((None, tm, D), lambda b, i: (b, i, 0)),
            _vec_spec(l, D),
            _mod_spec(l, 5, D, brow),
        ],
        out_specs=pl.BlockSpec((None, tm, D), lambda b, i: (b, i, 0)),
        out_shape=jax.ShapeDtypeStruct((B, n, D), F32),
        scratch_shapes=[pltpu.VMEM((tm, dff), BF16)],
        compiler_params=_cparams("parallel", "parallel"),
        name="conv_ffn",
    )(h3, h3, h3, w_up, conv_w, conv_b, w_down, x3, g_post, mod4)


def _mixffn_kernel(*refs):
    parts = refs[0:12]
    (xp_ref, x_ref, xn_ref, wout_ref, gpm_ref, gt1_ref, gpf_ref, sc_ref, sh_ref,
     wup_ref, cw_ref, cb_ref, wdn_ref, gpost_ref, gt2_ref, o_ref, y_scr) = refs[12:]
    i = pl.program_id(1)
    tm = x_ref.shape[0]
    dff = wdn_ref.shape[0]
    use = FFN_HALO // 2
    rows = tm + 2 * use
    W = GROUP_W
    ext = lambda p, m, nx: jnp.concatenate([p[...][FFN_HALO - use:], m[...], nx[...][:use]], axis=0)

    mix = None
    for g in range(4):
        d = _dot(ext(parts[3 * g], parts[3 * g + 1], parts[3 * g + 2]), wout_ref[g * W:(g + 1) * W, :])
        mix = d if mix is None else mix + d
    xm = ext(xp_ref, x_ref, xn_ref) + gt1_ref[...] * (_rms(mix) * gpm_ref[...])
    h = _rms(xm) * gpf_ref[...] * (1.0 + sc_ref[...]) + sh_ref[...]
    r = lax.broadcasted_iota(jnp.int32, (rows, 1), 0)
    outside = ((i == 0) & (r < use)) | ((i == pl.num_programs(1) - 1) & (r >= use + tm))
    hb = jnp.where(outside, 0.0, h).astype(BF16)

    def conv(z, cols):
        w = cw_ref[:, cols]
        out = cb_ref[:, cols] + w[0:1] * pltpu.roll(z, 1, axis=0) + w[1:2] * z + w[2:3] * pltpu.roll(z, rows - 1, axis=0)
        return out[use:use + tm]

    def cols(j):
        return slice(j * FFN_COLS, (j + 1) * FFN_COLS), slice(dff + j * FFN_COLS, dff + (j + 1) * FFN_COLS)

    def up(j):
        ca, cg = cols(j)
        return _dot(hb, wup_ref[:, ca]), _dot(hb, wup_ref[:, cg])

    nchunks = dff // FFN_COLS
    z_next = up(0)
    for j in range(nchunks):
        ca, cg = cols(j)
        za, zg = z_next
        if j + 1 < nchunks:
            z_next = up(j + 1)
        y_scr[:, ca] = _gelu_gate(conv(za, ca), conv(zg, cg)).astype(BF16)
    acc = _dot(y_scr[...], wdn_ref[...])
    o_ref[...] = xm[use:use + tm] + gt2_ref[...] * (_rms(acc) * gpost_ref[...])


def _mixffn(l, parts, x3, w_out, w_up, conv_w, conv_b, w_down, mod4, brow, g_post_mix, g_pre_ffn, g_post_ffn, tm):
    B, n, D = x3.shape
    hb = tm // FFN_HALO
    nh = n // FFN_HALO
    dff = w_down.shape[1]
    W = GROUP_W
    once = pl.Buffered(1)
    prev = lambda b, i: (b, jnp.maximum(i * hb - 1, 0), 0)
    main = lambda b, i: (b, i, 0)
    nxt = lambda b, i: (b, jnp.minimum((i + 1) * hb, nh - 1), 0)
    halo3 = lambda width: [pl.BlockSpec((None, FFN_HALO, width), prev), pl.BlockSpec((None, tm, width), main),
                           pl.BlockSpec((None, FFN_HALO, width), nxt)]
    in_specs = halo3(W) * 4 + halo3(D) + [
        pl.BlockSpec((None, 4 * W, D), lambda b, i: (l, 0, 0)),
        _vec_spec(l, D),
        _mod_spec(l, 2, D, brow),
        _vec_spec(l, D),
        _mod_spec(l, 4, D, brow),
        _mod_spec(l, 3, D, brow),
        pl.BlockSpec((None, D, 2 * dff), lambda b, i: (l, 0, 0), pipeline_mode=once),
        pl.BlockSpec((None, 3, 2 * dff), lambda b, i: (l, 0, 0)),
        pl.BlockSpec((None, 1, 2 * dff), lambda b, i: (l, 0, 0)),
        pl.BlockSpec((None, dff, D), lambda b, i: (l, 0, 0), pipeline_mode=once),
        _vec_spec(l, D),
        _mod_spec(l, 5, D, brow),
    ]
    args = [p for part in parts for p in (part, part, part)] + [x3, x3, x3, w_out, g_post_mix, mod4, g_pre_ffn, mod4, mod4,
                                                               w_up, conv_w, conv_b, w_down, g_post_ffn, mod4]
    return pl.pallas_call(
        _mixffn_kernel,
        grid=(B, n // tm),
        in_specs=in_specs,
        out_specs=pl.BlockSpec((None, tm, D), main),
        out_shape=jax.ShapeDtypeStruct((B, n, D), F32),
        scratch_shapes=[pltpu.VMEM((tm, dff), BF16)],
        compiler_params=_cparams("parallel", "parallel"),
        name="mix_ffn",
    )(*args)


def _prep_w_in(w_in):
    w_in = w_in.astype(BF16)
    z = lambda n: jnp.zeros(w_in.shape[:-1] + (n,), w_in.dtype)
    a = w_in[..., 0:512]
    b = w_in[..., 512:1536]
    gl = w_in[..., 1536:1568]
    c = w_in[..., 1568:1824]
    cq = w_in[..., 1824:2048]
    ckv_kr = w_in[..., 2048:2176]
    return jnp.concatenate([a, b, c, cq, z(32), ckv_kr, gl, z(96)], axis=-1)


def _rope_swap(t):
    q = MLA_ROPE // 4
    return jnp.concatenate([t[..., q:2 * q], t[..., 0:q], t[..., 3 * q:4 * q], t[..., 2 * q:3 * q]], axis=-1)


def _prep_mla(w_uq, w_ukv):
    L = w_uq.shape[0]
    H, NP, RP, HP = MLA_HEADS, MLA_NOPE, MLA_ROPE, MLA_HEAD_PAD
    wq = w_uq.reshape(L, MLA_Q_RANK, H, NP + RP)
    zq = jnp.zeros((L, MLA_Q_RANK, H, HP - NP - RP), w_uq.dtype)
    znope = jnp.zeros((L, MLA_Q_RANK, H, NP), w_uq.dtype)
    q_main = jnp.concatenate([wq, zq], axis=-1)
    q_swap = jnp.concatenate([znope, _rope_swap(wq[..., NP:]), zq], axis=-1)
    padq = lambda w: jnp.pad(w.reshape(L, MLA_Q_RANK, H * HP), ((0, 0), (0, MLA_Q_LANES - MLA_Q_RANK), (0, 0)))

    wkv = w_ukv.reshape(L, MLA_KV_RANK, H, NP + MLA_V)
    zk = jnp.zeros((L, MLA_KV_RANK, H, HP - NP), w_ukv.dtype)
    k_lat = jnp.concatenate([wkv[..., :NP], zk], axis=-1)
    eye = jnp.eye(RP, dtype=w_ukv.dtype)
    place = lambda e: jnp.broadcast_to(
        jnp.concatenate([jnp.zeros((RP, NP), e.dtype), e, jnp.zeros((RP, HP - NP - RP), e.dtype)], axis=-1)[None, :, None, :],
        (L, RP, H, HP))
    k_main = jnp.concatenate([k_lat, place(eye)], axis=1)
    k_swap = jnp.concatenate([jnp.zeros_like(k_lat), place(_rope_swap(eye))], axis=1)
    zv = jnp.zeros((L, MLA_KV_RANK, H, HP - MLA_V), w_ukv.dtype)
    odd = (jnp.arange(H) % 2 == 1)[None, None, :, None]
    v_lat = jnp.where(odd, jnp.concatenate([zv, wkv[..., NP:]], axis=-1), jnp.concatenate([wkv[..., NP:], zv], axis=-1))
    v_main = jnp.concatenate([v_lat, jnp.zeros((L, RP, H, HP), w_ukv.dtype)], axis=1)
    flat = lambda w: w.reshape(L, w.shape[1], H * HP).astype(BF16)
    return (padq(q_main).astype(BF16), padq(q_swap).astype(BF16), flat(k_main), flat(k_swap), flat(v_main))


def _rope_tables(n):
    rows = n // GRID_W
    nf = MLA_ROPE // 4
    inv = ROPE_BASE ** (-jnp.arange(nf, dtype=F32) / nf)
    ar = jnp.arange(rows, dtype=F32)[:, None] * inv[None, :]
    ac = jnp.arange(GRID_W, dtype=F32)[:, None] * inv[None, :]
    by_row = lambda t: jnp.repeat(t, GRID_W, axis=0)
    by_col = lambda t: jnp.tile(t, (rows, 1))
    cr, sr, cc, sn = by_row(jnp.cos(ar)), by_row(jnp.sin(ar)), by_col(jnp.cos(ac)), by_col(jnp.sin(ac))
    one = jnp.ones((n, MLA_NOPE), F32)
    zero = jnp.zeros((n, MLA_HEAD_PAD - MLA_NOPE - MLA_ROPE), F32)
    cos = jnp.concatenate([one, cr, cr, cc, cc, zero], axis=1)
    sin = jnp.concatenate([0.0 * one, -sr, sr, -sn, sn, zero], axis=1)
    return cos, sin


def _pick_tile(n, want):
    t = min(n, want)
    while n % t:
        t //= 2
    return t


def kernel(x, c, ctx, c_ctx, w_mod, b_mod, g_pre_mix, g_post_mix, g_pre_ffn, g_post_ffn, w_in,
           sgu_norm, sgu_w, sgu_b, gla_w_gate, gla_b_gate, gla_norm,
           s5_a_re, s5_a_im, s5_log_dt, s5_b_re, s5_b_im, s5_c_re, s5_c_im, s5_d, s5_w_glu, s5_b_glu,
           mla_q_norm, mla_w_uq, mla_kv_norm, mla_w_ukv, w_out,
           ffn_w_up, ffn_conv_w, ffn_conv_b, ffn_w_down):
    B, n, D = x.shape
    nctx = ctx.shape[1]
    L = w_mod.shape[0]
    W = GROUP_W
    assert B < 8 and n % TOKEN_TILE == 0 and nctx % GLA_TILE == 0 and n % GRID_W == 0

    c8 = jnp.concatenate([c, c_ctx[None, :], jnp.zeros((8 - B - 1, D), F32)], axis=0)
    mod4 = _modulation(c8, w_mod, b_mod).reshape(L, 8, 1, 6 * D)
    vec = lambda p: p.reshape(L, 1, -1).astype(F32)
    g_pre_mix, g_post_mix, g_pre_ffn, g_post_ffn = map(vec, (g_pre_mix, g_post_mix, g_pre_ffn, g_post_ffn))
    w_in_p = _prep_w_in(w_in)
    sgu_gn = vec(sgu_norm)
    sgu_w_st = sgu_w.reshape(L, -1, MLP_CHUNK).astype(BF16)
    sgu_bias = jnp.repeat(jnp.swapaxes(sgu_b, 1, 2), HEAD_D, axis=2).astype(F32)
    ones_bd = jnp.kron(jnp.eye(W // HEAD_D, dtype=F32), jnp.ones((HEAD_D, HEAD_D), F32)).astype(BF16)
    gla_wg = jnp.zeros((L, 2, LANES, W), F32)
    gla_wg = gla_wg.at[:, 0, 0:GATE_RANK].set(gla_w_gate[:, 0]).at[:, 1, GATE_RANK:2 * GATE_RANK].set(gla_w_gate[:, 1])
    gla_bg = gla_b_gate.reshape(L, 2, 1, W).astype(F32)
    gla_gn = vec(gla_norm)
    s5_wglu = s5_w_glu.astype(BF16)
    s5_bglu = vec(s5_b_glu)
    mla_qn = jnp.pad(mla_q_norm, ((0, 0), (0, MLA_Q_LANES - MLA_Q_RANK))).reshape(L, 1, MLA_Q_LANES).astype(F32)
    mla_kvn = jnp.pad(mla_kv_norm, ((0, 0), (0, MLA_KV_LANES - MLA_KV_RANK))).reshape(L, 1, MLA_KV_LANES).astype(F32)
    mla_wts = _prep_mla(mla_w_uq, mla_w_ukv)
    tables = _rope_tables(n)
    w_out_b = w_out.astype(BF16)
    w_up_b = ffn_w_up.astype(BF16)
    w_down_b = ffn_w_down.astype(BF16)
    dff = ffn_w_down.shape[1]
    half_gate = jnp.concatenate([jnp.ones((dff,), F32), jnp.full((dff,), 0.5, F32)])
    conv_w = ffn_conv_w.astype(F32) * half_gate
    conv_b = (ffn_conv_b.astype(F32) * half_gate).reshape(L, 1, -1)

    s5_mats = jax.vmap(_s5_matrices)(s5_a_re, s5_a_im, s5_log_dt, s5_b_re, s5_b_im, s5_c_re, s5_c_im, s5_d)

    tm_x = _pick_tile(n, TOKEN_TILE)
    tm_c = _pick_tile(nctx, TOKEN_TILE)
    tpb_x = n // tm_x
    brow_x = lambda i: i // tpb_x
    tm_in = _pick_tile(n, INPROJ_TILE)
    brow_in = lambda i: i // (n // tm_in)
    brow_c = lambda *g: B
    brow_x2 = lambda b, i: b

    xs = x.reshape(B * n, D)
    cs = ctx.reshape(B * nctx, D)
    nk = n + nctx
    k_all = v_all = jnp.zeros((B, nk, MLA_HEADS * MLA_HEAD_PAD), BF16)

    for l in range(L):
        need_ctx = l < L - 1
        sgu = (sgu_gn, sgu_w_st, sgu_bias, ones_bd)
        wq, wqs, wk, wks, wv = mla_wts
        oa_x, pb_x, pc_x, pg_x, q_x, k_all, v_all = _inproj(
            l, xs, mod4, brow_in, g_pre_mix, w_in_p, sgu, (mla_qn, mla_kvn, mla_wts, tables), tm_in, n, B, 0, nk, (k_all, v_all))
        oa_c, pb_c, pc_c, pg_c, q_c, k_all, v_all = _inproj(
            l, cs, mod4, brow_c, g_pre_mix, w_in_p, sgu, (mla_qn, mla_kvn, (wq, None, wk, None, wv), None), tm_c, nctx, B, n, nk,
            (k_all, v_all))

        r3 = lambda t, m: t.reshape(B, m, t.shape[-1])
        ob_c, ob_x = _gla(l, r3(pb_c, nctx), r3(pg_c, nctx), r3(pb_x, n), r3(pg_x, n),
                          gla_wg, gla_bg, gla_gn, ones_bd, _pick_tile(nctx, GLA_TILE), _pick_tile(n, GLA_STEP_TILES * GLA_TILE))
        oc_c, oc_x = _s5(l, r3(pc_c, nctx), r3(pc_x, n), s5_mats, s5_wglu, s5_bglu, need_ctx)
        tk = next(t for t in ATTN_KV_TILES if nk % t == 0)
        od_x = _attention(r3(q_x, n), k_all, v_all, tm_x, tk, 0, nk).reshape(B * n, W)

        t3 = lambda t, m: t.reshape(B, m, W)
        xs = _mixffn(l, (t3(oa_x, n), ob_x, t3(oc_x, n), t3(od_x, n)), xs.reshape(B, n, D), w_out_b, w_up_b, conv_w, conv_b,
                     w_down_b, mod4, brow_x2, g_post_mix, g_pre_ffn, g_post_ffn, tm_x).reshape(B * n, D)

        if need_ctx:
            od_c = _attention(r3(q_c, nctx), k_all, v_all, tm_c, nctx, n, nctx).reshape(B * nctx, W)
            cs = _mixffn(l, (t3(oa_c, nctx), ob_c, t3(oc_c, nctx), t3(od_c, nctx)), cs.reshape(B, nctx, D), w_out_b, w_up_b,
                         conv_w, conv_b, w_down_b, mod4, brow_c, g_post_mix, g_pre_ffn, g_post_ffn, tm_c).reshape(B * nctx, D)
    return xs.reshape(B, n, D)
```

```python
import functools

import jax
import jax.numpy as jnp
from jax import lax
from jax.experimental import pallas as pl
from jax.experimental.pallas import tpu as pltpu

F32 = jnp.float32
BF16 = jnp.bfloat16

EPS = 1e-6
GRID_W = 64
GROUP_W = 256
HEAD_D = 64
MLP_CHUNK = 128
GATE_RANK = 16
GATE_TEMP = 16.0
GLA_CHUNK = 64
GLA_TILE = 256
GLA_STEP_TILES = 8
S5_IN = 16
S5_T = 16
S5_TILE = 128
S5_PITCH_PAD = 8
MLA_HEADS = 4
MLA_NOPE = 64
MLA_ROPE = 32
MLA_V = 64
MLA_Q_RANK = 224
MLA_KV_RANK = 96
MLA_HEAD_PAD = 128
ROPE_BASE = 10000.0
LOG2E = 1.4426950408889634

LANES = 128
VMEM_LIMIT = 48 * 1024 * 1024
TOKEN_TILE = 512
INPROJ_TILE = 1024
MOD_COL_TILE = 1536
ATTN_KV_TILES = (2816, 1408, 768, 512, 256, 128)
MLA_Q_LANES = 256
MLA_KV_LANES = 128


def _cparams(*sem):
    return pltpu.CompilerParams(dimension_semantics=sem, vmem_limit_bytes=VMEM_LIMIT)


def _dot(a, b):
    return jnp.dot(a, b, preferred_element_type=F32)


def _dot_nt(a, b):
    return lax.dot_general(a, b, (((1,), (1,)), ((), ())), preferred_element_type=F32)


def _dot_tn(a, b):
    return lax.dot_general(a, b, (((0,), (0,)), ((), ())), preferred_element_type=F32)


def _split(a):
    hi = a.astype(BF16)
    lo = (a - hi.astype(F32)).astype(BF16)
    return hi, lo


def _dot_x2(a, b_bf16):
    hi, lo = _split(a)
    return _dot(hi, b_bf16) + _dot(lo, b_bf16)


def _dot_x3(a, b):
    ah, al = _split(a)
    bh, bl = _split(b)
    return _dot(ah, bh) + _dot(al, bh) + _dot(ah, bl)


def _rms(x):
    return x * lax.rsqrt(jnp.mean(x * x, axis=-1, keepdims=True) + EPS)


def _gelu(x):
    return 0.5 * x * (1.0 + jnp.tanh(0.7978845608028654 * (x + 0.044715 * (x * x * x))))


def _gelu_gate(a, half_g):
    u = a * (0.7978845608028654 + 0.035677408136300125 * (a * a))
    return (a * half_g) * (1.0 + jnp.tanh(u))


def _sigmoid(x):
    return 1.0 / (1.0 + jnp.exp(-x))


def _lane_group(shape, width):
    return lax.broadcasted_iota(jnp.int32, shape, len(shape) - 1) // width


def _mod_kernel(c_ref, w_ref, b_ref, o_ref):
    c = c_ref[...]
    s = c * _sigmoid(c)
    o_ref[...] = _dot_x3(s, w_ref[...]) + b_ref[...]


def _modulation(c8, w_mod, b_mod):
    L, D, W = w_mod.shape
    tn = MOD_COL_TILE
    return pl.pallas_call(
        _mod_kernel,
        grid=(L, W // tn),
        in_specs=[
            pl.BlockSpec((8, D), lambda l, j: (0, 0)),
            pl.BlockSpec((None, D, tn), lambda l, j: (l, 0, j)),
            pl.BlockSpec((None, 1, tn), lambda l, j: (l, 0, j)),
        ],
        out_specs=pl.BlockSpec((None, 8, tn), lambda l, j: (l, 0, j)),
        out_shape=jax.ShapeDtypeStruct((L, 8, W), F32),
        compiler_params=_cparams("arbitrary", "arbitrary"),
        name="modulation",
    )(c8, w_mod, b_mod.reshape(L, 1, W))


def _mod_spec(l, j, D, bfn):
    return pl.BlockSpec((None, None, 1, D), lambda *g: (l, bfn(*g), 0, j))


def _vec_spec(l, width):
    return pl.BlockSpec((None, 1, width), lambda *g: (l, 0, 0))


IN_SLABS = (("a", 0, 2 * GROUP_W), ("b", 512, 4 * GROUP_W), ("c", 1536, GROUP_W), ("d", 1792, MLA_Q_LANES + MLA_KV_LANES),
            ("g", 2176, LANES))
IN_PAD_COLS = 2304


def _sgu_tile(p, gn_ref, w_ref, b_ref, ones_ref, o_ref):
    tm = p.shape[0]
    g = _gelu(p)
    u = g[:, :GROUP_W]
    v = g[:, GROUP_W:]
    ms = _dot_x2(v * v, ones_ref[...]) * (1.0 / HEAD_D)
    vb = (v * lax.rsqrt(ms + EPS) * gn_ref[...]).astype(BF16)
    head = _lane_group((MLP_CHUNK, GROUP_W), HEAD_D)
    w = w_ref[...]
    for c in range(tm // MLP_CHUNK):
        rows = slice(c * MLP_CHUNK, (c + 1) * MLP_CHUNK)
        r = _dot(w, vb[rows])
        s = b_ref[...]
        for h in range(GROUP_W // HEAD_D):
            s = s + jnp.where(head == h, r[h * MLP_CHUNK:(h + 1) * MLP_CHUNK], 0.0)
        o_ref[rows, :] = (u[rows] * s).astype(o_ref.dtype)


def _mla_tile(pd, qn_ref, kvn_ref, wq_ref, wqs_ref, wk_ref, wks_ref, wv_ref, cos_ref, sin_ref, q_ref, k_ref, v_ref):
    cq = pd[:, 0:MLA_Q_LANES]
    ms = jnp.sum(cq * cq, axis=-1, keepdims=True) * (1.0 / MLA_Q_RANK)
    cqn = (cq * lax.rsqrt(ms + EPS) * qn_ref[...]).astype(BF16)
    ck = pd[:, MLA_Q_LANES:MLA_Q_LANES + MLA_KV_LANES]
    lane = lax.broadcasted_iota(jnp.int32, ck.shape, 1)
    is_lat = lane < MLA_KV_RANK
    ms = jnp.sum(jnp.where(is_lat, ck * ck, 0.0), axis=-1, keepdims=True) * (1.0 / MLA_KV_RANK)
    ckn = jnp.where(is_lat, ck * lax.rsqrt(ms + EPS) * kvn_ref[...], ck).astype(BF16)
    q = _dot(cqn, wq_ref[...])
    k = _dot(ckn, wk_ref[...])
    if cos_ref is not None:
        cos = jnp.concatenate([cos_ref[...]] * MLA_HEADS, axis=1)
        sin = jnp.concatenate([sin_ref[...]] * MLA_HEADS, axis=1)
        q = q * cos + _dot(cqn, wqs_ref[...]) * sin
        k = k * cos + _dot(ckn, wks_ref[...]) * sin
    q_ref[...] = (q * ((MLA_NOPE + MLA_ROPE) ** -0.5 * LOG2E)).astype(BF16)
    k_ref[...] = k.astype(BF16)
    v_ref[...] = _dot(ckn, wv_ref[...]).astype(BF16)


def _inproj_kernel(*refs, rope):
    x_ref, g_ref, sc_ref, sh_ref, w_ref, gn_ref, wsp_ref, bsp_ref, ones_ref, qn_ref, kvn_ref = refs[:11]
    if rope:
        wq_ref, wqs_ref, wk_ref, wks_ref, wv_ref, cos_ref, sin_ref = refs[11:18]
    else:
        wq_ref, wk_ref, wv_ref = refs[11:14]
        wqs_ref = wks_ref = cos_ref = sin_ref = None
    oa, ob, oc, og, q_ref, k_ref, v_ref = refs[-7:]
    half = x_ref.shape[0] // 2
    slab = {name: slice(off, off + width) for name, off, width in IN_SLABS}

    def project(rows):
        h = _rms(x_ref[rows, :]) * g_ref[...] * (1.0 + sc_ref[...]) + sh_ref[...]
        hb = h.astype(BF16)
        pa = _dot(hb, w_ref[:, slab["a"]])
        for name, o_ref in (("b", ob), ("c", oc), ("g", og)):
            o_ref[rows, :] = _dot(hb, w_ref[:, slab[name]]).astype(o_ref.dtype)
        return pa, _dot(hb, w_ref[:, slab["d"]])

    def mixers(pa, pd, rows):
        cos, sin = (cos_ref.at[rows], sin_ref.at[rows]) if rope else (None, None)
        _mla_tile(pd, qn_ref, kvn_ref, wq_ref, wqs_ref, wk_ref, wks_ref, wv_ref, cos, sin,
                  q_ref.at[rows], k_ref.at[rows], v_ref.at[rows])
        _sgu_tile(pa, gn_ref, wsp_ref, bsp_ref, ones_ref, oa.at[rows])

    first, second = slice(0, half), slice(half, 2 * half)
    pa0, pd0 = project(first)
    pa1, pd1 = project(second)
    mixers(pa0, pd0, first)
    mixers(pa1, pd1, second)


def _inproj(l, x2, mod4, brow, g_pre, w_in_p, sgu, mla, tm, n, B, row0, nk, kv_into):
    R, D = x2.shape
    gn, w_st, bias, ones_bd = sgu
    qn, kvn, (wq, wqs, wk, wks, wv), tables = mla
    rope = tables is not None
    H = GROUP_W // HEAD_D
    HP = MLA_HEADS * MLA_HEAD_PAD
    QL, KL = MLA_Q_LANES, MLA_KV_LANES
    npt = n // tm
    width = {name: w for name, _, w in IN_SLABS}
    wspec = lambda r: pl.BlockSpec((None, r, HP), lambda i: (l, 0, 0))
    in_specs = [
        pl.BlockSpec((tm, D), lambda i: (i, 0)),
        _vec_spec(l, D),
        _mod_spec(l, 1, D, brow),
        _mod_spec(l, 0, D, brow),
        pl.BlockSpec((None, D, IN_PAD_COLS), lambda i: (l, 0, 0)),
        _vec_spec(l, GROUP_W),
        pl.BlockSpec((None, H * MLP_CHUNK, MLP_CHUNK), lambda i: (l, 0, 0)),
        pl.BlockSpec((None, MLP_CHUNK, GROUP_W), lambda i: (l, 0, 0)),
        pl.BlockSpec((GROUP_W, GROUP_W), lambda i: (0, 0)),
        _vec_spec(l, QL),
        _vec_spec(l, KL),
    ]
    args = [x2, g_pre, mod4, mod4, w_in_p, gn, w_st, bias, ones_bd, qn, kvn]
    if rope:
        tspec = pl.BlockSpec((tm, MLA_HEAD_PAD), lambda i: (i % npt, 0))
        in_specs += [wspec(QL), wspec(QL), wspec(KL), wspec(KL), wspec(KL), tspec, tspec]
        args += [wq, wqs, wk, wks, wv, tables[0], tables[1]]
    else:
        in_specs += [wspec(QL), wspec(KL), wspec(KL)]
        args += [wq, wk, wv]
    aliases = {len(args): 5, len(args) + 1: 6}
    in_specs += [pl.BlockSpec(memory_space=pl.ANY)] * 2
    args += list(kv_into)
    row = lambda w: pl.BlockSpec((tm, w), lambda i: (i, 0))
    kv_spec = pl.BlockSpec((None, tm, HP), lambda i: (i // npt, row0 // tm + i % npt, 0))
    kv_shape = jax.ShapeDtypeStruct((B, nk, HP), BF16)
    return pl.pallas_call(
        functools.partial(_inproj_kernel, rope=rope),
        grid=(R // tm,),
        in_specs=in_specs,
        out_specs=[row(GROUP_W), row(width["b"]), row(width["c"]), row(width["g"]), row(HP), kv_spec, kv_spec],
        out_shape=[jax.ShapeDtypeStruct((R, w), BF16) for w in (GROUP_W, width["b"], width["c"], width["g"], HP)]
        + [kv_shape, kv_shape],
        input_output_aliases=aliases,
        compiler_params=_cparams("parallel"),
        name="inproj",
    )(*args)


def _gla_kernel(*refs, rev, finish):
    if finish:
        (pb_ref, pg_ref, wg_ref, bg_ref, s0_ref, ones_ref, dec_ref, oprev_ref, gn_ref, o_ref, sfin_ref, st_scr) = refs
    else:
        (pb_ref, pg_ref, wg_ref, bg_ref, s0_ref, ones_ref, dec_ref, o_ref, sfin_ref, st_scr) = refs
    i = pl.program_id(1)
    C, W, T = GLA_CHUNK, GROUP_W, GLA_TILE
    H = W // HEAD_D
    nsub = T // C
    ntile = pb_ref.shape[0] // T

    @pl.when(i == 0)
    def _():
        st_scr[...] = s0_ref[...]

    tri = dec_ref[...]
    mask4 = jnp.concatenate([tri.astype(F32)] * H, axis=0)
    head_t = _lane_group((T, W), HEAD_D)
    bd = (lax.broadcasted_iota(jnp.int32, (W, W), 0) // HEAD_D) == _lane_group((W, W), HEAD_D)
    tiles = list(range(ntile - 1, -1, -1)) if rev else list(range(ntile))
    subs = list(range(nsub - 1, -1, -1)) if rev else list(range(nsub))
    rows = lambda a: slice(a * T, (a + 1) * T)
    sub = lambda j: slice(j * C, (j + 1) * C)
    stack = lambda blocks: jnp.concatenate([blocks[j] for j in range(nsub)], axis=0)

    wg_hi, wg_lo = _split(wg_ref[...])
    logg = {}
    for a in tiles:
        z = _dot(pg_ref[rows(a), :], wg_hi) + _dot(pg_ref[rows(a), :], wg_lo) + bg_ref[...]
        logg[a] = (jnp.minimum(z, 0.0) - jnp.log(1.0 + jnp.exp(-jnp.abs(z)))) * (1.0 / GATE_TEMP)
    cums = {}
    for a in tiles:
        g_hi, g_lo = _split(logg[a])
        cums[a] = _dot(tri, g_hi) + _dot(tri, g_lo)
    qin, qoff, kin, kend, vb, tot, ptile = {}, {}, {}, {}, {}, {}, {}
    for a in tiles:
        cum = cums[a]
        q = pb_ref[rows(a), 0:W].astype(F32) * (HEAD_D ** -0.5)
        k = pb_ref[rows(a), W:2 * W].astype(F32)
        last_row = (lambda j: j * C) if rev else (lambda j: j * C + C - 1)
        off, run = {}, jnp.zeros((1, W), F32)
        for j in subs:
            tot[(a, j)] = cum[last_row(j):last_row(j) + 1]
            off[j] = run
            run = run + tot[(a, j)]
        ptile[a] = jnp.exp(run)
        e = jnp.exp(cum)
        qin[a] = (q * e).astype(BF16)
        qoff[a] = (q * (e * jnp.exp(stack({j: jnp.broadcast_to(off[j], (C, W)) for j in subs})))).astype(BF16)
        kin[a] = (k * jnp.exp(-cum)).astype(BF16)
        kend[a] = (k * jnp.exp(stack({j: jnp.broadcast_to(tot[(a, j)], (C, W)) for j in subs}) - cum)).astype(BF16)
        vb[a] = pb_ref[rows(a), 2 * W:3 * W]
    sc = {}
    for a in tiles:
        qst = jnp.concatenate([jnp.where(head_t == h, qin[a], jnp.zeros_like(qin[a])) for h in range(H)], axis=0)
        sc[a] = (_dot_nt(qst, kin[a]) * mask4).astype(BF16)
    ost = {a: _dot(sc[a], vb[a]) for a in tiles}
    kvt = {(a, j): jnp.where(bd, _dot_tn(vb[a][sub(j)], kend[a][sub(j)]), 0.0) for a in tiles for j in subs}
    cross, rend = {}, {}
    for a in tiles:
        r = None
        for j in subs:
            if r is not None:
                cross[(a, j)] = _dot_nt(qin[a][sub(j)], r.astype(BF16))
                r = r * jnp.exp(tot[(a, j)]) + kvt[(a, j)]
            else:
                cross[(a, j)] = jnp.zeros((C, W), F32)
                r = kvt[(a, j)]
        rend[a] = r
    st = st_scr[...]
    inter = {}
    for a in tiles:
        inter[a] = _dot_nt(qoff[a], st.astype(BF16))
        st = st * ptile[a] + rend[a]
    st_scr[...] = st

    for a in tiles:
        o = inter[a] + jnp.concatenate([cross[(a, j)] for j in range(nsub)], axis=0)
        for h in range(H):
            o = o + jnp.where(head_t == h, ost[a][h * T:(h + 1) * T], 0.0)
        if finish:
            o = o + oprev_ref[rows(a), :]
            ms = _dot_x2(o * o, ones_ref[...]) * (1.0 / HEAD_D)
            o = o * lax.rsqrt(ms + EPS) * gn_ref[...]
            r = pb_ref[rows(a), 3 * W:4 * W].astype(F32)
            o_ref[rows(a), :] = (o * (r * _sigmoid(r))).astype(o_ref.dtype)
        else:
            o_ref[rows(a), :] = o

    @pl.when(i == pl.num_programs(1) - 1)
    def _():
        sfin_ref[...] = st_scr[...]


def _gla_decay_matrices():
    T, C = GLA_TILE, GLA_CHUNK
    t = jnp.arange(T)[:, None]
    s = jnp.arange(T)[None, :]
    same = (t // C) == (s // C)
    return jnp.stack([same & (s <= t), same & (s >= t)]).astype(BF16)


def _gla_pass(l, d, pb3, pg3, wg, bg, s0, ones_bd, dec, oprev, gn, tt):
    B, n, _ = pb3.shape
    nt = n // tt
    rev = d == 1
    finish = oprev is not None
    W = GROUP_W

    def tok(b, i):
        return (b, (nt - 1 - i) if rev else i, 0)

    in_specs = [
        pl.BlockSpec((None, tt, 4 * W), tok),
        pl.BlockSpec((None, tt, LANES), tok),
        pl.BlockSpec((None, None, LANES, W), lambda b, i: (l, d, 0, 0)),
        pl.BlockSpec((None, None, 1, W), lambda b, i: (l, d, 0, 0)),
        pl.BlockSpec((None, W, W), lambda b, i: (b, 0, 0)),
        pl.BlockSpec((W, W), lambda b, i: (0, 0)),
        pl.BlockSpec((None, GLA_TILE, GLA_TILE), lambda b, i: (d, 0, 0)),
    ]
    args = [pb3, pg3, wg, bg, s0, ones_bd, dec]
    if finish:
        in_specs += [pl.BlockSpec((None, tt, W), tok), _vec_spec(l, W)]
        args += [oprev, gn]
    return pl.pallas_call(
        functools.partial(_gla_kernel, rev=rev, finish=finish),
        grid=(B, nt),
        in_specs=in_specs,
        out_specs=[pl.BlockSpec((None, tt, W), tok), pl.BlockSpec((None, W, W), lambda b, i: (b, 0, 0))],
        out_shape=[jax.ShapeDtypeStruct((B, n, W), BF16 if finish else F32),
                   jax.ShapeDtypeStruct((B, W, W), F32)],
        scratch_shapes=[pltpu.VMEM((W, W), F32)],
        compiler_params=_cparams("arbitrary", "arbitrary"),
        name="gla_bwd" if rev else "gla_fwd",
    )(*args)


def _gla(l, pb_c, pg_c, pb_x, pg_x, wg, bg, gn, ones_bd, tt_c, tt_x):
    B = pb_x.shape[0]
    zero = jnp.zeros((B, GROUP_W, GROUP_W), F32)
    dec = _gla_decay_matrices()
    ofc, sfc = _gla_pass(l, 0, pb_c, pg_c, wg, bg, zero, ones_bd, dec, None, None, tt_c)
    ofx, _ = _gla_pass(l, 0, pb_x, pg_x, wg, bg, sfc, ones_bd, dec, None, None, tt_x)
    ob_c, sbc = _gla_pass(l, 1, pb_c, pg_c, wg, bg, zero, ones_bd, dec, ofc, gn, tt_c)
    ob_x, _ = _gla_pass(l, 1, pb_x, pg_x, wg, bg, sbc, ones_bd, dec, ofx, gn, tt_x)
    return ob_c, ob_x


def _s5_matrices(a_re, a_im, log_dt, b_re, b_im, c_re, c_im, d_skip):
    T = S5_T
    G, P = a_re.shape[1:]
    I = b_re.shape[-1]
    lam = lax.complex(a_re.astype(F32), a_im.astype(F32))
    ldt = lam * jnp.exp(log_dt.astype(F32))[..., None]
    lam_bar = jnp.exp(ldt)
    b_bar = ((lam_bar - 1.0) / lam)[..., None] * lax.complex(b_re.astype(F32), b_im.astype(F32))
    cmat = lax.complex(c_re.astype(F32), c_im.astype(F32))
    steps = jnp.arange(T + 1, dtype=F32)
    pw = jnp.exp(ldt[..., None] * steps)
    taps = jnp.einsum('dgop,dgpk,dgpi->dgiko', cmat, pw[..., :T], b_bar).real
    taps = taps.at[0, :, :, 0, :].add(jnp.eye(I, dtype=F32)[None] * d_skip.astype(F32)[:, :, None])
    row = T * I
    zeros = jnp.zeros((G, I, row), F32)
    fwd = jnp.concatenate([zeros, taps[0].reshape(G, I, row)], axis=-1)
    bwd = jnp.concatenate([jnp.flip(taps[1], axis=2).reshape(G, I, row), zeros], axis=-1)
    m = jnp.stack([fwd[..., (T - s) * I:(T - s) * I + row] + bwd[..., (T - 1 - s) * I:(T - 1 - s) * I + row]
                   for s in range(T)], axis=1).reshape(G, row, row)

    ar = jnp.arange(T)
    pf = pw[0][..., T - 1 - ar]
    pb = pw[1][..., ar]
    bf = jnp.einsum('gps,gpi->gsip', pf, b_bar[0]).reshape(G, T * I, P)
    bb = jnp.einsum('gps,gpi->gsip', pb, b_bar[1]).reshape(G, T * I, P)
    bmat = jnp.concatenate([bf.real, bf.imag, bf.imag, bf.real, bb.real, bb.imag, bb.imag, bb.real], axis=-1)

    cf = jnp.einsum('gop,gpt->gpto', cmat[0], pw[0][..., 1 + ar]).reshape(G, P, T * I)
    cb = jnp.einsum('gop,gpt->gpto', cmat[1], pw[1][..., T - ar]).reshape(G, P, T * I)
    w = jnp.concatenate([m, cf.real, -cf.imag, cb.real, -cb.imag], axis=1)

    a = pw[..., T]
    acoef = jnp.stack([jnp.concatenate([a[0].real, a[0].real], axis=-1), jnp.concatenate([-a[0].imag, a[0].imag], axis=-1),
                       jnp.concatenate([a[1].real, a[1].real], axis=-1), jnp.concatenate([-a[1].imag, a[1].imag], axis=-1)])
    return bmat.astype(BF16), w.astype(BF16), acoef


def _s5_scan_kernel(xf_ref, xb_ref, bm_ref, a_ref, h0_ref, hf_ref, hb_ref, hfin_ref, st_scr, sf_scr, sb_scr, hf_scr, hb_scr):
    j = pl.program_id(1)
    G, tc, _ = xf_ref.shape
    pitch = tc + S5_PITCH_PAD
    half = hf_ref.shape[-1]

    @pl.when(j == 0)
    def _():
        st_scr[...] = h0_ref[...]

    for g in range(G):
        sf = _dot(xf_ref[g], bm_ref[g, :, 0:2 * half])
        sb = _dot(xb_ref[g], bm_ref[g, :, 2 * half:4 * half])
        for k in range(2):
            sf_scr[k, g * pitch:g * pitch + tc, :] = sf[:, k * half:(k + 1) * half]
            sb_scr[k, g * pitch:g * pitch + tc, :] = sb[:, k * half:(k + 1) * half]

    a1f, a2f, a1b, a2b = a_ref[0], a_ref[1], a_ref[2], a_ref[3]
    chunk = lambda i: pl.ds(i, G, stride=pitch)

    def body(i, hs):
        h1f, h2f, h1b, h2b = hs
        ib = tc - 1 - i
        hf_scr[chunk(i), :] = h1f
        hb_scr[chunk(ib), :] = h1b
        return (h1f * a1f + h2f * a2f + sf_scr[0, chunk(i), :], h2f * a1f - h1f * a2f + sf_scr[1, chunk(i), :],
                h1b * a1b + h2b * a2b + sb_scr[0, chunk(ib), :], h2b * a1b - h1b * a2b + sb_scr[1, chunk(ib), :])

    hs = lax.fori_loop(0, tc, body, (st_scr[0], st_scr[1], st_scr[2], st_scr[3]))
    for k in range(4):
        st_scr[k] = hs[k]
    for g in range(G):
        hf_ref[g] = hf_scr[g * pitch:g * pitch + tc, :]
        hb_ref[g] = hb_scr[g * pitch:g * pitch + tc, :]

    @pl.when(j == pl.num_programs(1) - 1)
    def _():
        hfin_ref[...] = st_scr[...]


def _s5_scan(l, xg, bmat, acoef, h0, B, tc):
    G, R, K = xg.shape
    nt = R // B // tc
    half = bmat.shape[-1] // 4
    fwd = lambda b, j: (0, b * nt + j, 0)
    bwd = lambda b, j: (0, b * nt + nt - 1 - j, 0)
    st_spec = pl.BlockSpec((None, 4, G, half), lambda b, j: (b, 0, 0, 0))
    pitch = tc + S5_PITCH_PAD
    return pl.pallas_call(
        _s5_scan_kernel,
        grid=(B, nt),
        in_specs=[pl.BlockSpec((G, tc, K), fwd),
                  pl.BlockSpec((G, tc, K), bwd),
                  pl.BlockSpec((None, G, K, 4 * half), lambda b, j: (l, 0, 0, 0)),
                  pl.BlockSpec((None, 4, G, half), lambda b, j: (l, 0, 0, 0)),
                  st_spec],
        out_specs=[pl.BlockSpec((G, tc, half), fwd), pl.BlockSpec((G, tc, half), bwd), st_spec],
        out_shape=[jax.ShapeDtypeStruct((G, R, half), F32), jax.ShapeDtypeStruct((G, R, half), F32),
                   jax.ShapeDtypeStruct((B, 4, G, half), F32)],
        scratch_shapes=[pltpu.VMEM((4, G, half), F32),
                        pltpu.VMEM((2, G * pitch, half), F32), pltpu.VMEM((2, G * pitch, half), F32),
                        pltpu.VMEM((G * pitch, half), F32), pltpu.VMEM((G * pitch, half), F32)],
        compiler_params=_cparams("arbitrary", "arbitrary"),
        name="s5_scan",
    )(xg, xg, bmat, acoef, h0)


def _s5_gather_kernel(x_ref, o_ref, xs_scr, xt_scr):
    G, tc, _ = o_ref.shape
    nh = x_ref.shape[1] // LANES
    gh = G // nh
    for h in range(nh):
        xs_scr[h] = x_ref[:, h * LANES:(h + 1) * LANES].astype(F32)
    for t in range(S5_T):
        for h in range(nh):
            rt = xs_scr[h, pl.ds(t, tc, stride=S5_T), :].T
            for g in range(gh):
                xt_scr[h * gh + g, t * S5_IN:(t + 1) * S5_IN, :] = rt[g * S5_IN:(g + 1) * S5_IN, :]
    for g in range(G):
        o_ref[g] = xt_scr[g].T.astype(o_ref.dtype)


def _s5_gather(x2, tc):
    N, W = x2.shape
    G = W // S5_IN
    R = N // S5_T
    return pl.pallas_call(
        _s5_gather_kernel,
        grid=(R // tc,),
        in_specs=[pl.BlockSpec((tc * S5_T, W), lambda i: (i, 0))],
        out_specs=pl.BlockSpec((G, tc, S5_T * S5_IN), lambda i: (0, i, 0)),
        out_shape=jax.ShapeDtypeStruct((G, R, S5_T * S5_IN), BF16),
        scratch_shapes=[pltpu.VMEM((W // LANES, tc * S5_T, LANES), F32), pltpu.VMEM((G, S5_T * S5_IN, tc), F32)],
        compiler_params=_cparams("parallel"),
        name="s5_gather",
    )(x2)


def _s5_scatter_kernel(x_ref, hf_ref, hb_ref, wy_ref, w_ref, b_ref, o_ref, yt_scr, tok_scr):
    G, tc, k = x_ref.shape
    kh = hf_ref.shape[-1]
    nh = tok_scr.shape[0]
    gh = G // nh
    for g in range(G):
        y = _dot(x_ref[g], wy_ref[g, 0:k, :])
        y = y + _dot(hf_ref[g].astype(BF16), wy_ref[g, k:k + kh, :])
        y = y + _dot(hb_ref[g].astype(BF16), wy_ref[g, k + kh:, :])
        yt = y.T
        for t in range(S5_T):
            yt_scr[t, g // gh, (g % gh) * S5_IN:(g % gh + 1) * S5_IN, :] = yt[t * S5_IN:(t + 1) * S5_IN, :]
    for t in range(S5_T):
        for h in range(nh):
            tok_scr[h, pl.ds(t, tc, stride=S5_T), :] = yt_scr[t, h].T
    y = _gelu(jnp.concatenate([tok_scr[h] for h in range(nh)], axis=1))
    gate = _sigmoid(_dot(y.astype(BF16), w_ref[...]) + b_ref[...])
    o_ref[...] = (y * gate).astype(o_ref.dtype)


def _s5_scatter_finish(l, xg, hf, hb, wy, w_glu, b_glu, tc):
    G, R, K = xg.shape
    kh = hf.shape[-1]
    W = G * S5_IN
    grp = lambda width: pl.BlockSpec((G, tc, width), lambda i: (0, i, 0))
    return pl.pallas_call(
        _s5_scatter_kernel,
        grid=(R // tc,),
        in_specs=[grp(K), grp(kh), grp(kh),
                  pl.BlockSpec((None, G, K + 2 * kh, K), lambda i: (l, 0, 0, 0)),
                  pl.BlockSpec((None, W, W), lambda i: (l, 0, 0)),
                  _vec_spec(l, W)],
        out_specs=pl.BlockSpec((tc * S5_T, W), lambda i: (i, 0)),
        out_shape=jax.ShapeDtypeStruct((R * S5_T, W), BF16),
        scratch_shapes=[pltpu.VMEM((S5_T, W // LANES, LANES, tc), F32), pltpu.VMEM((W // LANES, tc * S5_T, LANES), F32)],
        compiler_params=_cparams("parallel"),
        name="s5_scatter",
    )(xg, hf, hb, wy, w_glu, b_glu)


def _s5(l, pc_c, pc_x, mats, w_glu, b_glu, need_ctx):
    bmat, w, acoef = mats
    B, n, W = pc_x.shape
    nctx = pc_c.shape[1]
    G = W // S5_IN
    tc_c = min(S5_TILE, B * nctx // S5_T)
    tc_x = min(S5_TILE, n // S5_T)
    xg_c = _s5_gather(pc_c.reshape(B * nctx, W), tc_c)
    xg_x = _s5_gather(pc_x.reshape(B * n, W), tc_x)
    h0 = jnp.zeros((B, 4, G, acoef.shape[-1]), F32)
    hf_c, hb_c, h1 = _s5_scan(l, xg_c, bmat, acoef, h0, B, nctx // S5_T)
    hf_x, hb_x, _ = _s5_scan(l, xg_x, bmat, acoef, h1, B, tc_x)
    oc_x = _s5_scatter_finish(l, xg_x, hf_x, hb_x, w, w_glu, b_glu, tc_x)
    oc_c = _s5_scatter_finish(l, xg_c, hf_c, hb_c, w, w_glu, b_glu, tc_c) if need_ctx else None
    return oc_c, oc_x


def _attn_kernel(q_ref, k_ref, v_ref, o_ref, m_scr, l_scr, acc_scr):
    kv = pl.program_id(2)
    HP = MLA_HEAD_PAD
    tk = k_ref.shape[0]

    @pl.when(kv == 0)
    def _():
        m_scr[...] = jnp.full(m_scr.shape, -jnp.inf, F32)
        l_scr[...] = jnp.zeros(l_scr.shape, F32)
        acc_scr[...] = jnp.zeros(acc_scr.shape, F32)

    def scores(h):
        lanes = slice(h * HP, (h + 1) * HP)
        return _dot_nt(q_ref[:, lanes], k_ref[:, lanes])

    s_next = scores(0)
    for h in range(MLA_HEADS):
        lanes = slice(h * HP, (h + 1) * HP)
        s = s_next
        if h + 1 < MLA_HEADS:
            s_next = scores(h + 1)
        m_prev = m_scr[h]
        m_new = jnp.maximum(m_prev, jnp.max(s, axis=1, keepdims=True))
        alpha = jnp.exp2(m_prev - m_new)
        p = jnp.exp2(s - m_new[:, 0:1])
        lp = p[:, 0:LANES]
        for c in range(1, tk // LANES):
            lp = lp + p[:, c * LANES:(c + 1) * LANES]
        l_scr[h] = alpha * l_scr[h] + lp
        acc_scr[h] = alpha * acc_scr[h] + _dot(p.astype(BF16), v_ref[:, lanes])
        m_scr[h] = m_new

    @pl.when(kv == pl.num_programs(2) - 1)
    def _():
        low = lax.broadcasted_iota(jnp.int32, acc_scr.shape[1:], 1) < MLA_V
        norm = lambda h: acc_scr[h] * (1.0 / jnp.sum(l_scr[h], axis=1, keepdims=True))
        outs = [jnp.where(low, norm(h), norm(h + 1)) for h in range(0, MLA_HEADS, 2)]
        o_ref[...] = jnp.concatenate(outs, axis=1).astype(o_ref.dtype)


def _attention(q3, k3, v3, tq, tk, k0, nk):
    B, nq, HP = q3.shape
    kb = k0 // tk
    return pl.pallas_call(
        _attn_kernel,
        grid=(B, nq // tq, nk // tk),
        in_specs=[pl.BlockSpec((None, tq, HP), lambda b, i, j: (b, i, 0)),
                  pl.BlockSpec((None, tk, HP), lambda b, i, j: (b, kb + j, 0)),
                  pl.BlockSpec((None, tk, HP), lambda b, i, j: (b, kb + j, 0))],
        out_specs=pl.BlockSpec((None, tq, MLA_HEADS * MLA_V), lambda b, i, j: (b, i, 0)),
        out_shape=jax.ShapeDtypeStruct((B, nq, MLA_HEADS * MLA_V), BF16),
        scratch_shapes=[pltpu.VMEM((MLA_HEADS, tq, LANES), F32),
                        pltpu.VMEM((MLA_HEADS, tq, LANES), F32),
                        pltpu.VMEM((MLA_HEADS, tq, MLA_HEAD_PAD), F32)],
        compiler_params=_cparams("parallel", "parallel", "arbitrary"),
        name="attention",
    )(q3, k3, v3)


FFN_HALO = 16
FFN_COLS = 256


def _mixffn_kernel(*refs):
    parts = refs[0:12]
    (xp_ref, x_ref, xn_ref, wout_ref, gpm_ref, gt1_ref, gpf_ref, sc_ref, sh_ref,
     wup_ref, cw_ref, cb_ref, wdn_ref, gpost_ref, gt2_ref, o_ref, y_scr) = refs[12:]
    i = pl.program_id(1)
    tm = x_ref.shape[0]
    dff = wdn_ref.shape[0]
    use = FFN_HALO // 2
    rows = tm + 2 * use
    W = GROUP_W
    ext = lambda p, m, nx: jnp.concatenate([p[...][FFN_HALO - use:], m[...], nx[...][:use]], axis=0)

    mix = None
    for g in range(4):
        d = _dot(ext(parts[3 * g], parts[3 * g + 1], parts[3 * g + 2]), wout_ref[g * W:(g + 1) * W, :])
        mix = d if mix is None else mix + d
    xm = ext(xp_ref, x_ref, xn_ref) + gt1_ref[...] * (_rms(mix) * gpm_ref[...])
    h = _rms(xm) * gpf_ref[...] * (1.0 + sc_ref[...]) + sh_ref[...]
    r = lax.broadcasted_iota(jnp.int32, (rows, 1), 0)
    outside = ((i == 0) & (r < use)) | ((i == pl.num_programs(1) - 1) & (r >= use + tm))
    hb = jnp.where(outside, 0.0, h).astype(BF16)

    def conv(z, cols):
        w = cw_ref[:, cols]
        out = cb_ref[:, cols] + w[0:1] * pltpu.roll(z, 1, axis=0) + w[1:2] * z + w[2:3] * pltpu.roll(z, rows - 1, axis=0)
        return out[use:use + tm]

    def cols(j):
        return slice(j * FFN_COLS, (j + 1) * FFN_COLS), slice(dff + j * FFN_COLS, dff + (j + 1) * FFN_COLS)

    def up(j):
        ca, cg = cols(j)
        return _dot(hb, wup_ref[:, ca]), _dot(hb, wup_ref[:, cg])

    nchunks = dff // FFN_COLS
    z_next = up(0)
    for j in range(nchunks):
        ca, cg = cols(j)
        za, zg = z_next
        if j + 1 < nchunks:
            z_next = up(j + 1)
        y_scr[:, ca] = _gelu_gate(conv(za, ca), conv(zg, cg)).astype(BF16)
    acc = _dot(y_scr[...], wdn_ref[...])
    o_ref[...] = xm[use:use + tm] + gt2_ref[...] * (_rms(acc) * gpost_ref[...])


def _mixffn(l, parts, x3, w_out, w_up, conv_w, conv_b, w_down, mod4, brow, g_post_mix, g_pre_ffn, g_post_ffn, tm):
    B, n, D = x3.shape
    hb = tm // FFN_HALO
    nh = n // FFN_HALO
    dff = w_down.shape[1]
    W = GROUP_W
    once = pl.Buffered(1)
    prev = lambda b, i: (b, jnp.maximum(i * hb - 1, 0), 0)
    main = lambda b, i: (b, i, 0)
    nxt = lambda b, i: (b, jnp.minimum((i + 1) * hb, nh - 1), 0)
    halo3 = lambda width: [pl.BlockSpec((None, FFN_HALO, width), prev), pl.BlockSpec((None, tm, width), main),
                           pl.BlockSpec((None, FFN_HALO, width), nxt)]
    in_specs = halo3(W) * 4 + halo3(D) + [
        pl.BlockSpec((None, 4 * W, D), lambda b, i: (l, 0, 0)),
        _vec_spec(l, D),
        _mod_spec(l, 2, D, brow),
        _vec_spec(l, D),
        _mod_spec(l, 4, D, brow),
        _mod_spec(l, 3, D, brow),
        pl.BlockSpec((None, D, 2 * dff), lambda b, i: (l, 0, 0), pipeline_mode=once),
        pl.BlockSpec((None, 3, 2 * dff), lambda b, i: (l, 0, 0)),
        pl.BlockSpec((None, 1, 2 * dff), lambda b, i: (l, 0, 0)),
        pl.BlockSpec((None, dff, D), lambda b, i: (l, 0, 0), pipeline_mode=once),
        _vec_spec(l, D),
        _mod_spec(l, 5, D, brow),
    ]
    args = [p for part in parts for p in (part, part, part)] + [x3, x3, x3, w_out, g_post_mix, mod4, g_pre_ffn, mod4, mod4,
                                                               w_up, conv_w, conv_b, w_down, g_post_ffn, mod4]
    return pl.pallas_call(
        _mixffn_kernel,
        grid=(B, n // tm),
        in_specs=in_specs,
        out_specs=pl.BlockSpec((None, tm, D), main),
        out_shape=jax.ShapeDtypeStruct((B, n, D), F32),
        scratch_shapes=[pltpu.VMEM((tm, dff), BF16)],
        compiler_params=_cparams("parallel", "parallel"),
        name="mix_ffn",
    )(*args)


def _prep_w_in(w_in):
    w_in = w_in.astype(BF16)
    z = lambda n: jnp.zeros(w_in.shape[:-1] + (n,), w_in.dtype)
    a = w_in[..., 0:512]
    b = w_in[..., 512:1536]
    gl = w_in[..., 1536:1568]
    c = w_in[..., 1568:1824]
    cq = w_in[..., 1824:2048]
    ckv_kr = w_in[..., 2048:2176]
    return jnp.concatenate([a, b, c, cq, z(32), ckv_kr, gl, z(96)], axis=-1)


def _rope_swap(t):
    q = MLA_ROPE // 4
    return jnp.concatenate([t[..., q:2 * q], t[..., 0:q], t[..., 3 * q:4 * q], t[..., 2 * q:3 * q]], axis=-1)


def _prep_mla(w_uq, w_ukv):
    L = w_uq.shape[0]
    H, NP, RP, HP = MLA_HEADS, MLA_NOPE, MLA_ROPE, MLA_HEAD_PAD
    wq = w_uq.reshape(L, MLA_Q_RANK, H, NP + RP)
    zq = jnp.zeros((L, MLA_Q_RANK, H, HP - NP - RP), w_uq.dtype)
    znope = jnp.zeros((L, MLA_Q_RANK, H, NP), w_uq.dtype)
    q_main = jnp.concatenate([wq, zq], axis=-1)
    q_swap = jnp.concatenate([znope, _rope_swap(wq[..., NP:]), zq], axis=-1)
    padq = lambda w: jnp.pad(w.reshape(L, MLA_Q_RANK, H * HP), ((0, 0), (0, MLA_Q_LANES - MLA_Q_RANK), (0, 0)))

    wkv = w_ukv.reshape(L, MLA_KV_RANK, H, NP + MLA_V)
    zk = jnp.zeros((L, MLA_KV_RANK, H, HP - NP), w_ukv.dtype)
    k_lat = jnp.concatenate([wkv[..., :NP], zk], axis=-1)
    eye = jnp.eye(RP, dtype=w_ukv.dtype)
    place = lambda e: jnp.broadcast_to(
        jnp.concatenate([jnp.zeros((RP, NP), e.dtype), e, jnp.zeros((RP, HP - NP - RP), e.dtype)], axis=-1)[None, :, None, :],
        (L, RP, H, HP))
    k_main = jnp.concatenate([k_lat, place(eye)], axis=1)
    k_swap = jnp.concatenate([jnp.zeros_like(k_lat), place(_rope_swap(eye))], axis=1)
    zv = jnp.zeros((L, MLA_KV_RANK, H, HP - MLA_V), w_ukv.dtype)
    odd = (jnp.arange(H) % 2 == 1)[None, None, :, None]
    v_lat = jnp.where(odd, jnp.concatenate([zv, wkv[..., NP:]], axis=-1), jnp.concatenate([wkv[..., NP:], zv], axis=-1))
    v_main = jnp.concatenate([v_lat, jnp.zeros((L, RP, H, HP), w_ukv.dtype)], axis=1)
    flat = lambda w: w.reshape(L, w.shape[1], H * HP).astype(BF16)
    return (padq(q_main).astype(BF16), padq(q_swap).astype(BF16), flat(k_main), flat(k_swap), flat(v_main))


def _rope_tables(n):
    rows = n // GRID_W
    nf = MLA_ROPE // 4
    inv = ROPE_BASE ** (-jnp.arange(nf, dtype=F32) / nf)
    ar = jnp.arange(rows, dtype=F32)[:, None] * inv[None, :]
    ac = jnp.arange(GRID_W, dtype=F32)[:, None] * inv[None, :]
    by_row = lambda t: jnp.repeat(t, GRID_W, axis=0)
    by_col = lambda t: jnp.tile(t, (rows, 1))
    cr, sr, cc, sn = by_row(jnp.cos(ar)), by_row(jnp.sin(ar)), by_col(jnp.cos(ac)), by_col(jnp.sin(ac))
    one = jnp.ones((n, MLA_NOPE), F32)
    zero = jnp.zeros((n, MLA_HEAD_PAD - MLA_NOPE - MLA_ROPE), F32)
    cos = jnp.concatenate([one, cr, cr, cc, cc, zero], axis=1)
    sin = jnp.concatenate([0.0 * one, -sr, sr, -sn, sn, zero], axis=1)
    return cos, sin


def _pick_tile(n, want):
    t = min(n, want)
    while n % t:
        t //= 2
    return t


def kernel(x, c, ctx, c_ctx, w_mod, b_mod, g_pre_mix, g_post_mix, g_pre_ffn, g_post_ffn, w_in,
           sgu_norm, sgu_w, sgu_b, gla_w_gate, gla_b_gate, gla_norm,
           s5_a_re, s5_a_im, s5_log_dt, s5_b_re, s5_b_im, s5_c_re, s5_c_im, s5_d, s5_w_glu, s5_b_glu,
           mla_q_norm, mla_w_uq, mla_kv_norm, mla_w_ukv, w_out,
           ffn_w_up, ffn_conv_w, ffn_conv_b, ffn_w_down):
    B, n, D = x.shape
    nctx = ctx.shape[1]
    L = w_mod.shape[0]
    W = GROUP_W
    assert B < 8 and n % TOKEN_TILE == 0 and nctx % GLA_TILE == 0 and n % GRID_W == 0

    c8 = jnp.concatenate([c, c_ctx[None, :], jnp.zeros((8 - B - 1, D), F32)], axis=0)
    mod4 = _modulation(c8, w_mod, b_mod).reshape(L, 8, 1, 6 * D)
    vec = lambda p: p.reshape(L, 1, -1).astype(F32)
    g_pre_mix, g_post_mix, g_pre_ffn, g_post_ffn = map(vec, (g_pre_mix, g_post_mix, g_pre_ffn, g_post_ffn))
    w_in_p = _prep_w_in(w_in)
    sgu_gn = vec(sgu_norm)
    sgu_w_st = sgu_w.reshape(L, -1, MLP_CHUNK).astype(BF16)
    sgu_bias = jnp.repeat(jnp.swapaxes(sgu_b, 1, 2), HEAD_D, axis=2).astype(F32)
    ones_bd = jnp.kron(jnp.eye(W // HEAD_D, dtype=F32), jnp.ones((HEAD_D, HEAD_D), F32)).astype(BF16)
    gla_wg = jnp.zeros((L, 2, LANES, W), F32)
    gla_wg = gla_wg.at[:, 0, 0:GATE_RANK].set(gla_w_gate[:, 0]).at[:, 1, GATE_RANK:2 * GATE_RANK].set(gla_w_gate[:, 1])
    gla_bg = gla_b_gate.reshape(L, 2, 1, W).astype(F32)
    gla_gn = vec(gla_norm)
    s5_wglu = s5_w_glu.astype(BF16)
    s5_bglu = vec(s5_b_glu)
    mla_qn = jnp.pad(mla_q_norm, ((0, 0), (0, MLA_Q_LANES - MLA_Q_RANK))).reshape(L, 1, MLA_Q_LANES).astype(F32)
    mla_kvn = jnp.pad(mla_kv_norm, ((0, 0), (0, MLA_KV_LANES - MLA_KV_RANK))).reshape(L, 1, MLA_KV_LANES).astype(F32)
    mla_wts = _prep_mla(mla_w_uq, mla_w_ukv)
    tables = _rope_tables(n)
    w_out_b = w_out.astype(BF16)
    w_up_b = ffn_w_up.astype(BF16)
    w_down_b = ffn_w_down.astype(BF16)
    dff = ffn_w_down.shape[1]
    half_gate = jnp.concatenate([jnp.ones((dff,), F32), jnp.full((dff,), 0.5, F32)])
    conv_w = ffn_conv_w.astype(F32) * half_gate
    conv_b = (ffn_conv_b.astype(F32) * half_gate).reshape(L, 1, -1)

    s5_mats = jax.vmap(_s5_matrices)(s5_a_re, s5_a_im, s5_log_dt, s5_b_re, s5_b_im, s5_c_re, s5_c_im, s5_d)

    tm_x = _pick_tile(n, TOKEN_TILE)
    tm_c = _pick_tile(nctx, TOKEN_TILE)
    tm_in = _pick_tile(n, INPROJ_TILE)
    brow_in = lambda i: i // (n // tm_in)
    brow_c = lambda *g: B
    brow_x2 = lambda b, i: b

    xs = x.reshape(B * n, D)
    cs = ctx.reshape(B * nctx, D)
    nk = n + nctx
    k_all = v_all = jnp.zeros((B, nk, MLA_HEADS * MLA_HEAD_PAD), BF16)

    for l in range(L):
        need_ctx = l < L - 1
        sgu = (sgu_gn, sgu_w_st, sgu_bias, ones_bd)
        wq, wqs, wk, wks, wv = mla_wts
        oa_x, pb_x, pc_x, pg_x, q_x, k_all, v_all = _inproj(
            l, xs, mod4, brow_in, g_pre_mix, w_in_p, sgu, (mla_qn, mla_kvn, mla_wts, tables), tm_in, n, B, 0, nk, (k_all, v_all))
        oa_c, pb_c, pc_c, pg_c, q_c, k_all, v_all = _inproj(
            l, cs, mod4, brow_c, g_pre_mix, w_in_p, sgu, (mla_qn, mla_kvn, (wq, None, wk, None, wv), None), tm_c, nctx, B, n, nk,
            (k_all, v_all))

        r3 = lambda t, m: t.reshape(B, m, t.shape[-1])
        ob_c, ob_x = _gla(l, r3(pb_c, nctx), r3(pg_c, nctx), r3(pb_x, n), r3(pg_x, n),
                          gla_wg, gla_bg, gla_gn, ones_bd, _pick_tile(nctx, GLA_TILE), _pick_tile(n, GLA_STEP_TILES * GLA_TILE))
        oc_c, oc_x = _s5(l, r3(pc_c, nctx), r3(pc_x, n), s5_mats, s5_wglu, s5_bglu, need_ctx)
        tk = next(t for t in ATTN_KV_TILES if nk % t == 0)
        od_x = _attention(r3(q_x, n), k_all, v_all, tm_x, tk, 0, nk).reshape(B * n, W)

        t3 = lambda t, m: t.reshape(B, m, W)
        xs = _mixffn(l, (t3(oa_x, n), ob_x, t3(oc_x, n), t3(od_x, n)), xs.reshape(B, n, D), w_out_b, w_up_b, conv_w, conv_b,
                     w_down_b, mod4, brow_x2, g_post_mix, g_pre_ffn, g_post_ffn, tm_x).reshape(B * n, D)

        if need_ctx:
            od_c = _attention(r3(q_c, nctx), k_all, v_all, tm_c, nctx, n, nctx).reshape(B * nctx, W)
            cs = _mixffn(l, (t3(oa_c, nctx), ob_c, t3(oc_c, nctx), t3(od_c, nctx)), cs.reshape(B, nctx, D), w_out_b, w_up_b,
                         conv_w, conv_b, w_down_b, mod4, brow_c, g_post_mix, g_pre_ffn, g_post_ffn, tm_c).reshape(B * nctx, D)
    return xs.reshape(B, n, D)
```
